```python
import jax, jax.numpy as jnp
from jax import lax
import numpy as np

D_MODEL = 1024
BATCH = 8
SEQ = 2048
DEPTH = 1

N_META = 16
HEAD_DIM = 64
MIX_WIDTH = D_MODEL
W_A = MIX_WIDTH // 2
W_B = MIX_WIDTH - W_A
N_HEADS_A = W_A // HEAD_DIM
N_HEADS_B = W_B // HEAD_DIM
CONV_A_WIDTH = 3
CONV_B_WIDTH = 31
IN_PROJ_WIDTH = 3 * W_A + 2 * W_B
D_FF = ((8 * D_MODEL // 3 + 255) // 256) * 256
RMS_EPS = 1e-6
LN_EPS = 1e-5

kernel_name = "hybrid_shortconv_conformer_block"


def rms_norm(x, g):
    xf = x.astype(jnp.float32)
    y = xf * lax.rsqrt(jnp.mean(xf * xf, axis=-1, keepdims=True) + RMS_EPS)
    return (y * g.astype(jnp.float32)).astype(x.dtype)


def layer_norm(x, g, b):
    xf = x.astype(jnp.float32)
    mu = jnp.mean(xf, axis=-1, keepdims=True)
    var = jnp.mean(jnp.square(xf - mu), axis=-1, keepdims=True)
    y = (xf - mu) * lax.rsqrt(var + LN_EPS)
    return (y * g.astype(jnp.float32) + b.astype(jnp.float32)).astype(x.dtype)


def causal_depthwise_conv(u, w):
    k = w.shape[0]
    return lax.conv_general_dilated(
        u, w[:, None, :].astype(u.dtype),
        window_strides=(1,), padding=[(k - 1, 0)],
        dimension_numbers=("NWC", "WIO", "NWC"),
        feature_group_count=u.shape[-1])


def mixer(xn, w_in, conv_a_w, conv_b_w, conv_b_bias, ln_b_gain, ln_b_bias, w_out):
    h = jnp.einsum("btd,de->bte", xn, w_in)
    b_gate, c_gate, h_a, glu_val, glu_gate = jnp.split(
        h, [W_A, 2 * W_A, 3 * W_A, 3 * W_A + W_B], axis=-1)
    y_a = b_gate * causal_depthwise_conv(c_gate * h_a, conv_a_w)
    g = glu_val * jax.nn.sigmoid(glu_gate)
    z = causal_depthwise_conv(g, conv_b_w) + conv_b_bias.astype(g.dtype)
    y_b = jax.nn.silu(layer_norm(z, ln_b_gain, ln_b_bias))
    y = jnp.concatenate([y_a, y_b], axis=-1)
    return jnp.einsum("bte,ed->btd", y, w_out)


def swiglu(xn, w_gate, w_up, w_down):
    a = jnp.einsum("btd,df->btf", xn, w_gate)
    u = jnp.einsum("btd,df->btf", xn, w_up)
    return jnp.einsum("btf,fd->btd", jax.nn.silu(a) * u, w_down)


def _fwd_setup_inputs(seed: int = 0) -> dict:
    key = jax.random.key(seed)
    ks = jax.random.split(key, 20)
    f32 = jnp.float32

    def nrm(k, shape, scale):
        return jax.random.normal(k, shape, f32) * scale

    def gain(k, n):
        return jnp.ones((DEPTH, n), f32) + 0.05 * jax.random.normal(k, (DEPTH, n), f32)

    return {
        "x": jax.random.normal(ks[0], (BATCH, SEQ, D_MODEL), f32),
        "meta_tokens": nrm(ks[1], (N_META, D_MODEL), 1.0),
        "pre_mix_norm": gain(ks[2], D_MODEL),
        "w_in": nrm(ks[3], (DEPTH, D_MODEL, IN_PROJ_WIDTH), D_MODEL ** -0.5),
        "conv_a_w": nrm(ks[4], (DEPTH, CONV_A_WIDTH, W_A), CONV_A_WIDTH ** -0.5),
        "conv_b_w": nrm(ks[5], (DEPTH, CONV_B_WIDTH, W_B), CONV_B_WIDTH ** -0.5),
        "conv_b_bias": nrm(ks[6], (DEPTH, W_B), 0.02),
        "ln_b_gain": gain(ks[7], W_B),
        "ln_b_bias": nrm(ks[8], (DEPTH, W_B), 0.02),
        "w_out": nrm(ks[9], (DEPTH, MIX_WIDTH, D_MODEL), MIX_WIDTH ** -0.5),
        "post_mix_norm": gain(ks[10], D_MODEL),
        "pre_ffn_norm": gain(ks[11], D_MODEL),
        "w_gate": nrm(ks[12], (DEPTH, D_MODEL, D_FF), D_MODEL ** -0.5),
        "w_up": nrm(ks[13], (DEPTH, D_MODEL, D_FF), D_MODEL ** -0.5),
        "w_down": nrm(ks[14], (DEPTH, D_FF, D_MODEL), D_FF ** -0.5),
        "post_ffn_norm": gain(ks[15], D_MODEL),
    }


def _fwd_reference(x, meta_tokens, pre_mix_norm, w_in, conv_a_w, conv_b_w, conv_b_bias,
              ln_b_gain, ln_b_bias, w_out, post_mix_norm, pre_ffn_norm,
              w_gate, w_up, w_down, post_ffn_norm):
    b = x.shape[0]
    meta = jnp.broadcast_to(meta_tokens[None].astype(x.dtype), (b, N_META, x.shape[-1]))
    h = jnp.concatenate([meta, x], axis=1)
    for l in range(DEPTH):
        mix = mixer(rms_norm(h, pre_mix_norm[l]), w_in[l], conv_a_w[l], conv_b_w[l],
                    conv_b_bias[l], ln_b_gain[l], ln_b_bias[l], w_out[l])
        h = h + rms_norm(mix, post_mix_norm[l])
        ff = swiglu(rms_norm(h, pre_ffn_norm[l]), w_gate[l], w_up[l], w_down[l])
        h = h + rms_norm(ff, post_ffn_norm[l])
    return h[:, N_META:, :]


import jax as _jax
import jax.numpy as _jnp

TWIN_FORMAT = 'train_step'
FWD_PARAMS = ['x', 'meta_tokens', 'pre_mix_norm', 'w_in', 'conv_a_w', 'conv_b_w', 'conv_b_bias', 'ln_b_gain', 'ln_b_bias', 'w_out', 'post_mix_norm', 'pre_ffn_norm', 'w_gate', 'w_up', 'w_down', 'post_ffn_norm']
TWIN_WEIGHTS = ['meta_tokens', 'pre_mix_norm', 'w_in', 'conv_a_w', 'conv_b_w', 'conv_b_bias', 'ln_b_gain', 'ln_b_bias', 'w_out', 'post_mix_norm', 'pre_ffn_norm', 'w_gate', 'w_up', 'w_down', 'post_ffn_norm']
TWIN_DIFF_INPUT = 'x'
TWIN_INPUTS = ['x', 'meta_tokens', 'pre_mix_norm', 'w_in', 'conv_a_w', 'conv_b_w', 'conv_b_bias', 'ln_b_gain', 'ln_b_bias', 'w_out', 'post_mix_norm', 'pre_ffn_norm', 'w_gate', 'w_up', 'w_down', 'post_ffn_norm', 'loss_target', 'm_meta_tokens', 'm_pre_mix_norm', 'm_w_in', 'm_conv_a_w', 'm_conv_b_w', 'm_conv_b_bias', 'm_ln_b_gain', 'm_ln_b_bias', 'm_w_out', 'm_post_mix_norm', 'm_pre_ffn_norm', 'm_w_gate', 'm_w_up', 'm_w_down', 'm_post_ffn_norm', 'v_meta_tokens', 'v_pre_mix_norm', 'v_w_in', 'v_conv_a_w', 'v_conv_b_w', 'v_conv_b_bias', 'v_ln_b_gain', 'v_ln_b_bias', 'v_w_out', 'v_post_mix_norm', 'v_pre_ffn_norm', 'v_w_gate', 'v_w_up', 'v_w_down', 'v_post_ffn_norm']
TWIN_OUTPUTS = ['loss', 'grad_x', 'grad_meta_tokens', 'grad_pre_mix_norm', 'grad_w_in', 'grad_conv_a_w', 'grad_conv_b_w', 'grad_conv_b_bias', 'grad_ln_b_gain', 'grad_ln_b_bias', 'grad_w_out', 'grad_post_mix_norm', 'grad_pre_ffn_norm', 'grad_w_gate', 'grad_w_up', 'grad_w_down', 'grad_post_ffn_norm', 'delta_meta_tokens', 'delta_pre_mix_norm', 'delta_w_in', 'delta_conv_a_w', 'delta_conv_b_w', 'delta_conv_b_bias', 'delta_ln_b_gain', 'delta_ln_b_bias', 'delta_w_out', 'delta_post_mix_norm', 'delta_pre_ffn_norm', 'delta_w_gate', 'delta_w_up', 'delta_w_down', 'delta_post_ffn_norm', 'new_m_meta_tokens', 'new_m_pre_mix_norm', 'new_m_w_in', 'new_m_conv_a_w', 'new_m_conv_b_w', 'new_m_conv_b_bias', 'new_m_ln_b_gain', 'new_m_ln_b_bias', 'new_m_w_out', 'new_m_post_mix_norm', 'new_m_pre_ffn_norm', 'new_m_w_gate', 'new_m_w_up', 'new_m_w_down', 'new_m_post_ffn_norm', 'new_v_meta_tokens', 'new_v_pre_mix_norm', 'new_v_w_in', 'new_v_conv_a_w', 'new_v_conv_b_w', 'new_v_conv_b_bias', 'new_v_ln_b_gain', 'new_v_ln_b_bias', 'new_v_w_out', 'new_v_post_mix_norm', 'new_v_pre_ffn_norm', 'new_v_w_gate', 'new_v_w_up', 'new_v_w_down', 'new_v_post_ffn_norm']
TWIN_LEAF_KINDS = {'loss': 'loss', 'grad_x': 'grad_x', 'grad_meta_tokens': 'grad_w', 'grad_pre_mix_norm': 'grad_w', 'grad_w_in': 'grad_w', 'grad_conv_a_w': 'grad_w', 'grad_conv_b_w': 'grad_w', 'grad_conv_b_bias': 'grad_w', 'grad_ln_b_gain': 'grad_w', 'grad_ln_b_bias': 'grad_w', 'grad_w_out': 'grad_w', 'grad_post_mix_norm': 'grad_w', 'grad_pre_ffn_norm': 'grad_w', 'grad_w_gate': 'grad_w', 'grad_w_up': 'grad_w', 'grad_w_down': 'grad_w', 'grad_post_ffn_norm': 'grad_w', 'delta_meta_tokens': 'delta_w', 'delta_pre_mix_norm': 'delta_w', 'delta_w_in': 'delta_w', 'delta_conv_a_w': 'delta_w', 'delta_conv_b_w': 'delta_w', 'delta_conv_b_bias': 'delta_w', 'delta_ln_b_gain': 'delta_w', 'delta_ln_b_bias': 'delta_w', 'delta_w_out': 'delta_w', 'delta_post_mix_norm': 'delta_w', 'delta_pre_ffn_norm': 'delta_w', 'delta_w_gate': 'delta_w', 'delta_w_up': 'delta_w', 'delta_w_down': 'delta_w', 'delta_post_ffn_norm': 'delta_w', 'new_m_meta_tokens': 'new_m', 'new_m_pre_mix_norm': 'new_m', 'new_m_w_in': 'new_m', 'new_m_conv_a_w': 'new_m', 'new_m_conv_b_w': 'new_m', 'new_m_conv_b_bias': 'new_m', 'new_m_ln_b_gain': 'new_m', 'new_m_ln_b_bias': 'new_m', 'new_m_w_out': 'new_m', 'new_m_post_mix_norm': 'new_m', 'new_m_pre_ffn_norm': 'new_m', 'new_m_w_gate': 'new_m', 'new_m_w_up': 'new_m', 'new_m_w_down': 'new_m', 'new_m_post_ffn_norm': 'new_m', 'new_v_meta_tokens': 'new_v', 'new_v_pre_mix_norm': 'new_v', 'new_v_w_in': 'new_v', 'new_v_conv_a_w': 'new_v', 'new_v_conv_b_w': 'new_v', 'new_v_conv_b_bias': 'new_v', 'new_v_ln_b_gain': 'new_v', 'new_v_ln_b_bias': 'new_v', 'new_v_w_out': 'new_v', 'new_v_post_mix_norm': 'new_v', 'new_v_pre_ffn_norm': 'new_v', 'new_v_w_gate': 'new_v', 'new_v_w_up': 'new_v', 'new_v_w_down': 'new_v', 'new_v_post_ffn_norm': 'new_v'}


def _forward(args):
    return _fwd_reference(*[args[k] for k in FWD_PARAMS])


def _output_shape():
    out = _jax.eval_shape(lambda: _forward(_fwd_setup_inputs(0)))
    return out.shape, out.dtype

N_MICROBATCH = 1
ADAM_LR = 0.001
ADAM_B1 = 0.9
ADAM_B2 = 0.999
ADAM_EPS = 1e-08
ADAM_WD = 0.01
ADAM_STEP = 10
PER_EXAMPLE_BATCH_AXIS = {'x': 0, 'loss_target': 0}
SHARED_INPUTS = []
_WEIGHT_DTYPES = {'meta_tokens': _jnp.float32, 'pre_mix_norm': _jnp.float32, 'w_in': _jnp.float32, 'conv_a_w': _jnp.float32, 'conv_b_w': _jnp.float32, 'conv_b_bias': _jnp.float32, 'ln_b_gain': _jnp.float32, 'ln_b_bias': _jnp.float32, 'w_out': _jnp.float32, 'post_mix_norm': _jnp.float32, 'pre_ffn_norm': _jnp.float32, 'w_gate': _jnp.float32, 'w_up': _jnp.float32, 'w_down': _jnp.float32, 'post_ffn_norm': _jnp.float32}
MOMENT_SCALE = {'meta_tokens': 1.199685e-02, 'pre_mix_norm': 4.975920e-01, 'w_in': 3.157328e-01, 'conv_a_w': 4.104151e-01, 'conv_b_w': 2.542010e-01, 'conv_b_bias': 1.485950e+00, 'ln_b_gain': 5.990191e-01, 'ln_b_bias': 8.161793e-01, 'w_out': 4.088060e-01, 'post_mix_norm': 1.601473e+01, 'pre_ffn_norm': 4.463435e-01, 'w_gate': 1.525650e-01, 'w_up': 2.058578e-01, 'w_down': 3.401903e-01, 'post_ffn_norm': 1.603598e+01}


def _to_microbatches(a, axis):
    t = _jnp.moveaxis(a, axis, 0)
    t = t.reshape((N_MICROBATCH, t.shape[0] // N_MICROBATCH) + t.shape[1:])
    return _jnp.moveaxis(t, 1, axis + 1)


def setup_inputs(seed: int = 0) -> dict:
    inp = _fwd_setup_inputs(seed)
    key = _jax.random.fold_in(_jax.random.key(seed), 7919)
    shape, _ = _output_shape()
    out = dict(inp)
    out["loss_target"] = _jax.random.normal(_jax.random.fold_in(key, 0), shape, _jnp.float32)
    for i, name in enumerate(TWIN_WEIGHTS):
        w = inp[name].astype(_jnp.float32)
        if MOMENT_SCALE is None:
            s = _jnp.sqrt(_jnp.mean(_jnp.square(w)) + 1e-30)
        else:
            s = MOMENT_SCALE[name]
        km, kv = _jax.random.split(_jax.random.fold_in(key, i + 1))
        out[name] = w
        out["m_" + name] = s * _jax.random.normal(km, w.shape, _jnp.float32)
        out["v_" + name] = (s * s) * _jax.random.uniform(kv, w.shape, _jnp.float32, 0.5, 1.5)
    if N_MICROBATCH > 1:
        for name, axis in PER_EXAMPLE_BATCH_AXIS.items():
            out[name] = _to_microbatches(out[name], axis)
    return {'x': out['x'], 'meta_tokens': out['meta_tokens'], 'pre_mix_norm': out['pre_mix_norm'], 'w_in': out['w_in'], 'conv_a_w': out['conv_a_w'], 'conv_b_w': out['conv_b_w'], 'conv_b_bias': out['conv_b_bias'], 'ln_b_gain': out['ln_b_gain'], 'ln_b_bias': out['ln_b_bias'], 'w_out': out['w_out'], 'post_mix_norm': out['post_mix_norm'], 'pre_ffn_norm': out['pre_ffn_norm'], 'w_gate': out['w_gate'], 'w_up': out['w_up'], 'w_down': out['w_down'], 'post_ffn_norm': out['post_ffn_norm'], 'loss_target': out['loss_target'], 'm_meta_tokens': out['m_meta_tokens'], 'm_pre_mix_norm': out['m_pre_mix_norm'], 'm_w_in': out['m_w_in'], 'm_conv_a_w': out['m_conv_a_w'], 'm_conv_b_w': out['m_conv_b_w'], 'm_conv_b_bias': out['m_conv_b_bias'], 'm_ln_b_gain': out['m_ln_b_gain'], 'm_ln_b_bias': out['m_ln_b_bias'], 'm_w_out': out['m_w_out'], 'm_post_mix_norm': out['m_post_mix_norm'], 'm_pre_ffn_norm': out['m_pre_ffn_norm'], 'm_w_gate': out['m_w_gate'], 'm_w_up': out['m_w_up'], 'm_w_down': out['m_w_down'], 'm_post_ffn_norm': out['m_post_ffn_norm'], 'v_meta_tokens': out['v_meta_tokens'], 'v_pre_mix_norm': out['v_pre_mix_norm'], 'v_w_in': out['v_w_in'], 'v_conv_a_w': out['v_conv_a_w'], 'v_conv_b_w': out['v_conv_b_w'], 'v_conv_b_bias': out['v_conv_b_bias'], 'v_ln_b_gain': out['v_ln_b_gain'], 'v_ln_b_bias': out['v_ln_b_bias'], 'v_w_out': out['v_w_out'], 'v_post_mix_norm': out['v_post_mix_norm'], 'v_pre_ffn_norm': out['v_pre_ffn_norm'], 'v_w_gate': out['v_w_gate'], 'v_w_up': out['v_w_up'], 'v_w_down': out['v_w_down'], 'v_post_ffn_norm': out['v_post_ffn_norm']}


def _loss(weights, diff, rest, loss_target):
    with _jax.named_scope("forward"):
        args = {**rest, TWIN_DIFF_INPUT: diff, **{k: w.astype(_WEIGHT_DTYPES[k]) for k, w in weights.items()}}
        y = _forward(args)
    with _jax.named_scope("loss_head"):
        err = _jnp.square(y.astype(_jnp.float32) - loss_target)
        return 0.5 * _jnp.sum(_jnp.mean(err, axis=-1)) if err.ndim else 0.5 * err


def _adamw(w, g, m, v):
    m = ADAM_B1 * m + (1.0 - ADAM_B1) * g
    v = ADAM_B2 * v + (1.0 - ADAM_B2) * _jnp.square(g)
    m_hat = m / (1.0 - ADAM_B1 ** ADAM_STEP)
    v_hat = v / (1.0 - ADAM_B2 ** ADAM_STEP)
    delta = -ADAM_LR * (m_hat / (_jnp.sqrt(v_hat) + ADAM_EPS) + ADAM_WD * w)
    return delta, m, v


def reference(x, meta_tokens, pre_mix_norm, w_in, conv_a_w, conv_b_w, conv_b_bias, ln_b_gain, ln_b_bias, w_out, post_mix_norm, pre_ffn_norm, w_gate, w_up, w_down, post_ffn_norm, loss_target, m_meta_tokens, m_pre_mix_norm, m_w_in, m_conv_a_w, m_conv_b_w, m_conv_b_bias, m_ln_b_gain, m_ln_b_bias, m_w_out, m_post_mix_norm, m_pre_ffn_norm, m_w_gate, m_w_up, m_w_down, m_post_ffn_norm, v_meta_tokens, v_pre_mix_norm, v_w_in, v_conv_a_w, v_conv_b_w, v_conv_b_bias, v_ln_b_gain, v_ln_b_bias, v_w_out, v_post_mix_norm, v_pre_ffn_norm, v_w_gate, v_w_up, v_w_down, v_post_ffn_norm):
    given = dict(x=x, meta_tokens=meta_tokens, pre_mix_norm=pre_mix_norm, w_in=w_in, conv_a_w=conv_a_w, conv_b_w=conv_b_w, conv_b_bias=conv_b_bias, ln_b_gain=ln_b_gain, ln_b_bias=ln_b_bias, w_out=w_out, post_mix_norm=post_mix_norm, pre_ffn_norm=pre_ffn_norm, w_gate=w_gate, w_up=w_up, w_down=w_down, post_ffn_norm=post_ffn_norm, loss_target=loss_target, m_meta_tokens=m_meta_tokens, m_pre_mix_norm=m_pre_mix_norm, m_w_in=m_w_in, m_conv_a_w=m_conv_a_w, m_conv_b_w=m_conv_b_w, m_conv_b_bias=m_conv_b_bias, m_ln_b_gain=m_ln_b_gain, m_ln_b_bias=m_ln_b_bias, m_w_out=m_w_out, m_post_mix_norm=m_post_mix_norm, m_pre_ffn_norm=m_pre_ffn_norm, m_w_gate=m_w_gate, m_w_up=m_w_up, m_w_down=m_w_down, m_post_ffn_norm=m_post_ffn_norm, v_meta_tokens=v_meta_tokens, v_pre_mix_norm=v_pre_mix_norm, v_w_in=v_w_in, v_conv_a_w=v_conv_a_w, v_conv_b_w=v_conv_b_w, v_conv_b_bias=v_conv_b_bias, v_ln_b_gain=v_ln_b_gain, v_ln_b_bias=v_ln_b_bias, v_w_out=v_w_out, v_post_mix_norm=v_post_mix_norm, v_pre_ffn_norm=v_pre_ffn_norm, v_w_gate=v_w_gate, v_w_up=v_w_up, v_w_down=v_w_down, v_post_ffn_norm=v_post_ffn_norm)
    weights = {n: given[n] for n in TWIN_WEIGHTS}
    shared = {n: given[n] for n in SHARED_INPUTS}
    per_example = {n: given[n] for n in ['x']}
    grad_fn = _jax.value_and_grad(_loss, argnums=(0, 1))

    def one_microbatch(ex, loss_target):
        ex = dict(ex)
        diff = ex.pop(TWIN_DIFF_INPUT)
        return grad_fn(weights, diff, {**shared, **ex}, loss_target)

    if N_MICROBATCH == 1:
        loss, (grad_w, grad_x) = one_microbatch(per_example, given["loss_target"])
    else:
        def body(carry, xs):
            loss_sum, grad_sum = carry
            l_k, (gw_k, gx_k) = one_microbatch(xs[0], xs[1])
            with _jax.named_scope("update"):
                return (loss_sum + l_k, _jax.tree.map(_jnp.add, grad_sum, gw_k)), gx_k

        init = (_jnp.zeros((), _jnp.float32), _jax.tree.map(_jnp.zeros_like, weights))
        (loss, grad_w), grad_x = _jax.lax.scan(body, init, (per_example, given["loss_target"]))
    with _jax.named_scope("update"):
        delta_w, new_m, new_v = {}, {}, {}
        for n in TWIN_WEIGHTS:
            delta_w[n], new_m[n], new_v[n] = _adamw(weights[n], grad_w[n], given["m_" + n], given["v_" + n])
    return (loss, grad_x, *[grad_w[n] for n in TWIN_WEIGHTS], *[delta_w[n] for n in TWIN_WEIGHTS],
            *[new_m[n] for n in TWIN_WEIGHTS], *[new_v[n] for n in TWIN_WEIGHTS])
```

```python
import functools

import jax
import jax.numpy as jnp
from jax import lax
from jax.experimental import pallas as pl
from jax.experimental.pallas import tpu as pltpu

F32 = jnp.float32
BF16 = jnp.bfloat16
MESH = pl.DeviceIdType.MESH

N_META = 16
TAIL_ROWS = 128
RMS_EPS = 1e-6
LN_EPS = 1e-5
ADAM_LR = 0.001
ADAM_B1 = 0.9
ADAM_B2 = 0.999
ADAM_EPS = 1e-08
ADAM_WD = 0.01
ADAM_STEP = 10

N_CHIPS = 4
LANES = 128
CONV_CHUNK = 48
CONV_HIST = 32
ROW_TILE_CAP = 640
VMEM_LIMIT = 56 * 1024 * 1024

NN = (((1,), (0,)), ((), ()))
NT = (((1,), (1,)), ((), ()))
TN = (((0,), (0,)), ((), ()))


def _dot(a, b, dims=NN):
    return lax.dot_general(a, b, dims, preferred_element_type=F32)


def _sig(v):
    return 1.0 / (1.0 + jnp.exp(-v))


def _mean(v):
    return jnp.mean(v, axis=-1, keepdims=True)


def _row_tile(rows):
    best = 16
    for t in range(16, min(rows, ROW_TILE_CAP) + 1, 16):
        if rows % t == 0:
            best = t
    assert rows % best == 0
    return best


def _pieces(n_shards, shard_w, piece_w):
    total = n_shards * shard_w
    cuts = sorted(set(range(0, total + 1, shard_w)) | set(range(0, total + 1, piece_w)))
    out = []
    for lo, hi in zip(cuts[:-1], cuts[1:]):
        out.append((lo // shard_w, lo % shard_w, lo // piece_w, lo % piece_w, hi - lo))
    return out


def _params(semantics=None):
    kw = dict(vmem_limit_bytes=VMEM_LIMIT)
    if semantics is not None:
        kw["dimension_semantics"] = semantics
    return pltpu.CompilerParams(**kw)


def _full(shape):
    nd = len(shape)
    return pl.BlockSpec(shape, lambda *_: (0,) * nd)


def _sds(shape, dtype):
    return jax.ShapeDtypeStruct(shape, dtype)


ANY = pl.BlockSpec(memory_space=pl.ANY)
VMEM = pl.BlockSpec(memory_space=pltpu.VMEM)


def _mesh_pos():
    return lax.axis_index("x"), lax.axis_index("y"), lax.axis_index("c")


def _flip(v, bit):
    return 1 - v if bit else v


def _mm_in(h, win4, g1):
    tp, d = h.shape
    tm = _row_tile(tp)
    n_sh, _, csh = win4.shape
    pw = n_sh * csh // 5
    pieces = _pieces(n_sh, csh, pw)

    def body(h_ref, w_ref, g_ref, xn_ref, hp_ref):
        hh = h_ref[...]
        r = lax.rsqrt(_mean(hh * hh) + RMS_EPS)
        xn = (hh * r * g_ref[...]).astype(BF16)
        xn_ref[...] = xn
        for k, klo, p, plo, w in pieces:
            hp_ref[p, :, plo:plo + w] = _dot(xn, w_ref[k, :, klo:klo + w])

    return pl.pallas_call(
        body, name="mm_in", grid=(tp // tm,),
        in_specs=[pl.BlockSpec((tm, d), lambda i: (i, 0)), _full(win4.shape), _full(g1.shape)],
        out_specs=[pl.BlockSpec((tm, d), lambda i: (i, 0)), pl.BlockSpec((5, tm, pw), lambda i: (0, i, 0))],
        out_shape=[_sds((tp, d), BF16), _sds((5, tp, pw), F32)],
        compiler_params=_params(("arbitrary",)),
    )(h, win4, g1)


def _seq_rows(tp):
    seq = tp - TAIL_ROWS
    nseq = seq + N_META
    assert nseq % CONV_CHUNK == 0 and seq % 16 == 0
    return seq, nseq


def _conv_taps(src_ref, w_ref, dst_ref, width, nseq, transpose):
    w = w_ref[...]

    def step(n, carry):
        out0 = pl.multiple_of(CONV_HIST + n * CONV_CHUNK, 8)
        win0 = out0 if transpose else pl.multiple_of(n * CONV_CHUNK, 8)
        win = src_ref[pl.ds(win0, CONV_CHUNK + CONV_HIST), :]
        acc = jnp.zeros((CONV_CHUNK, w.shape[1]), F32)
        for k in range(width):
            off = (width - 1 - k) if transpose else (CONV_HIST - (width - 1) + k)
            acc = acc + w[k:k + 1, :] * win[off:off + CONV_CHUNK, :]
        dst_ref[pl.ds(out0, CONV_CHUNK), :] = acc
        return carry

    lax.fori_loop(0, nseq // CONV_CHUNK, step, 0)


def _conv_wgrad(src_ref, dz_ref, acc_ref, width, nseq):
    acc_ref[...] = jnp.zeros(acc_ref.shape, F32)

    def step(n, carry):
        win = src_ref[pl.ds(pl.multiple_of(n * CONV_CHUNK, 8), CONV_CHUNK + CONV_HIST), :]
        dzc = dz_ref[pl.ds(pl.multiple_of(CONV_HIST + n * CONV_CHUNK, 8), CONV_CHUNK), :]
        for k in range(width):
            off = CONV_HIST - (width - 1) + k
            prod = dzc * win[off:off + CONV_CHUNK, :]
            part = prod[0:8, :]
            for s in range(1, CONV_CHUNK // 8):
                part = part + prod[8 * s:8 * s + 8, :]
            acc_ref[8 * k:8 * k + 8, :] += part
        return carry

    lax.fori_loop(0, nseq // CONV_CHUNK, step, 0)


def _to_seq(buf_ref, x_part, meta_part, seq):
    buf_ref[CONV_HIST:CONV_HIST + N_META, :] = meta_part
    buf_ref[CONV_HIST + N_META:CONV_HIST + N_META + seq, :] = x_part


def _zero_ends(buf_ref, nseq):
    zeros = jnp.zeros((CONV_HIST, buf_ref.shape[1]), F32)
    buf_ref[0:CONV_HIST, :] = zeros
    buf_ref[CONV_HIST + nseq:CONV_HIST + nseq + CONV_HIST, :] = zeros


def _mix_conv_fwd(hp5, wa, wb, bb):
    _, tp, wgrp = hp5.shape
    seq, nseq = _seq_rows(tp)
    sb = nseq + 2 * CONV_HIST
    ka, kb = wa.shape[0], wb.shape[0]
    xs, ms = slice(0, seq), slice(seq, seq + N_META)
    ox, om = slice(CONV_HIST + N_META, CONV_HIST + nseq), slice(CONV_HIST, CONV_HIST + N_META)

    def body(hp_ref, wa_ref, wb_ref, bb_ref, ya_ref, z_ref, s_ref, o_ref):
        _zero_ends(s_ref, nseq)
        _to_seq(s_ref, hp_ref[1, xs, :] * hp_ref[2, xs, :], hp_ref[1, ms, :] * hp_ref[2, ms, :], seq)
        _conv_taps(s_ref, wa_ref, o_ref, ka, nseq, False)
        ya_ref[xs, :] = (hp_ref[0, xs, :] * o_ref[ox, :]).astype(BF16)
        ya_ref[ms, :] = (hp_ref[0, ms, :] * o_ref[om, :]).astype(BF16)
        ya_ref[seq + N_META:tp, :] = jnp.zeros((tp - seq - N_META, LANES), BF16)
        _to_seq(s_ref, hp_ref[3, xs, :] * _sig(hp_ref[4, xs, :]), hp_ref[3, ms, :] * _sig(hp_ref[4, ms, :]), seq)
        _conv_taps(s_ref, wb_ref, o_ref, kb, nseq, False)
        z_ref[xs, :] = o_ref[ox, :] + bb_ref[...]
        z_ref[ms, :] = o_ref[om, :] + bb_ref[...]
        z_ref[seq + N_META:tp, :] = jnp.zeros((tp - seq - N_META, LANES), F32)

    col = lambda j: (0, j)
    return pl.pallas_call(
        body, name="mix_conv_fwd", grid=(wgrp // LANES,),
        in_specs=[pl.BlockSpec((5, tp, LANES), lambda j: (0, 0, j)), pl.BlockSpec((ka, LANES), col),
                  pl.BlockSpec((kb, LANES), col), pl.BlockSpec((1, LANES), col)],
        out_specs=[pl.BlockSpec((tp, LANES), col), pl.BlockSpec((tp, LANES), col)],
        out_shape=[_sds((tp, wgrp), BF16), _sds((tp, wgrp), F32)],
        scratch_shapes=[pltpu.VMEM((sb, LANES), F32), pltpu.VMEM((sb, LANES), F32)],
        compiler_params=_params(("arbitrary",)),
    )(hp5, wa, wb, bb)


def _layer_norm_parts(z, lg, lb):
    mu = _mean(z)
    zc = z - mu
    rl = lax.rsqrt(_mean(zc * zc) + LN_EPS)
    zh = zc * rl
    return rl, zh, zh * lg + lb


def _mm_out(ya, z, h, wout, lg, lb, gpm, g2):
    tp, d = h.shape
    wa_ = ya.shape[1]
    tm = _row_tile(tp)

    def body(ya_ref, z_ref, h_ref, w_ref, lg_ref, lb_ref, gpm_ref, g2_ref, yb_ref, mix_ref, h1_ref, xn2_ref):
        _, _, l = _layer_norm_parts(z_ref[...], lg_ref[...], lb_ref[...])
        yb = (l * _sig(l)).astype(BF16)
        yb_ref[...] = yb
        mix = _dot(ya_ref[...], w_ref[0:wa_, :]) + _dot(yb, w_ref[wa_:d, :])
        mix_ref[...] = mix
        rm = lax.rsqrt(_mean(mix * mix) + RMS_EPS)
        h1 = h_ref[...] + mix * rm * gpm_ref[...]
        h1_ref[...] = h1
        r2 = lax.rsqrt(_mean(h1 * h1) + RMS_EPS)
        xn2_ref[...] = (h1 * r2 * g2_ref[...]).astype(BF16)

    row = lambda i: (i, 0)
    return pl.pallas_call(
        body, name="mm_out", grid=(tp // tm,),
        in_specs=[pl.BlockSpec((tm, wa_), row), pl.BlockSpec((tm, wa_), row), pl.BlockSpec((tm, d), row),
                  _full(wout.shape), _full(lg.shape), _full(lb.shape), _full(gpm.shape), _full(g2.shape)],
        out_specs=[pl.BlockSpec((tm, wa_), row), pl.BlockSpec((tm, d), row), pl.BlockSpec((tm, d), row),
                   pl.BlockSpec((tm, d), row)],
        out_shape=[_sds((tp, wa_), BF16), _sds((tp, d), F32), _sds((tp, d), F32), _sds((tp, d), BF16)],
        compiler_params=_params(("arbitrary",)),
    )(ya, z, h, wout, lg, lb, gpm, g2)


def _ffn_up(xn2, wg4, wu4):
    tp, d = xn2.shape
    n_sh, _, fs = wg4.shape
    tm = _row_tile(tp)

    def body(xn_ref, wg_ref, wu_ref, a_ref, u_ref, f_ref):
        xn = xn_ref[...]
        a = _dot(xn, wg_ref[0])
        u = _dot(xn, wu_ref[0])
        a_ref[0] = a.astype(BF16)
        u_ref[0] = u.astype(BF16)
        f_ref[0] = (a * _sig(a) * u).astype(BF16)

    wspec = pl.BlockSpec((1, d, fs), lambda k, i: (k, 0, 0))
    ospec = pl.BlockSpec((1, tm, fs), lambda k, i: (k, i, 0))
    return pl.pallas_call(
        body, name="ffn_up", grid=(n_sh, tp // tm),
        in_specs=[pl.BlockSpec((tm, d), lambda k, i: (i, 0)), wspec, wspec],
        out_specs=[ospec, ospec, ospec],
        out_shape=[_sds((n_sh, tp, fs), BF16)] * 3,
        compiler_params=_params(("arbitrary", "arbitrary")),
    )(xn2, wg4, wu4)


def _ffn_down(f4, wd4, h1, tgt, gpf):
    n_sh, tp, fs = f4.shape
    d = h1.shape[1]
    tm = _row_tile(tp)
    seq, _ = _seq_rows(tp)

    def body(f_ref, w_ref, h1_ref, t_ref, gpf_ref, dff_ref, dh2_ref, loss_ref, dgpf_ref):
        i = pl.program_id(0)
        ff = _dot(f_ref[0], w_ref[0])
        for k in range(1, n_sh):
            ff = ff + _dot(f_ref[k], w_ref[k])
        rf = lax.rsqrt(_mean(ff * ff) + RMS_EPS)
        nf = ff * rf
        gpf_ = gpf_ref[...]
        h2 = h1_ref[...] + nf * gpf_
        rows = i * tm + lax.broadcasted_iota(jnp.int32, (tm, 1), 0)
        err = jnp.where(rows < seq, h2 - t_ref[...], 0.0)
        dh2 = err * (1.0 / d)
        dh2_ref[...] = dh2
        dn = dh2 * gpf_
        dff_ref[...] = (rf * (dn - nf * _mean(dn * nf))).astype(BF16)

        @pl.when(i == 0)
        def _():
            loss_ref[...] = jnp.zeros(loss_ref.shape, F32)
            dgpf_ref[...] = jnp.zeros(dgpf_ref.shape, F32)

        loss_ref[...] += (0.5 / d) * jnp.sum(err * err, axis=(0, 1), keepdims=True)
        dgpf_ref[...] += jnp.sum(dh2 * nf, axis=0, keepdims=True)

    row = lambda i: (i, 0)
    return pl.pallas_call(
        body, name="ffn_down", grid=(tp // tm,),
        in_specs=[pl.BlockSpec((n_sh, tm, fs), lambda i: (0, i, 0)), _full(wd4.shape), pl.BlockSpec((tm, d), row),
                  pl.BlockSpec((tm, d), row), _full(gpf.shape)],
        out_specs=[pl.BlockSpec((tm, d), row), pl.BlockSpec((tm, d), row), _full((8, LANES)), _full((1, d))],
        out_shape=[_sds((tp, d), BF16), _sds((tp, d), F32), _sds((8, LANES), F32), _sds((1, d), F32)],
        compiler_params=_params(("arbitrary",)),
    )(f4, wd4, h1, tgt, gpf)


def _ffn_bwd_act(dff, wd4, a4, u4):
    tp, d = dff.shape
    n_sh, fs, _ = wd4.shape
    tm = _row_tile(tp)

    def body(dff_ref, w_ref, a_ref, u_ref, da_ref, du_ref):
        df = _dot(dff_ref[...], w_ref[0], NT)
        a = a_ref[0].astype(F32)
        u = u_ref[0].astype(F32)
        s = _sig(a)
        da_ref[0] = (df * u * (s * (1.0 + a * (1.0 - s)))).astype(BF16)
        du_ref[0] = (df * a * s).astype(BF16)

    aspec = pl.BlockSpec((1, tm, fs), lambda k, i: (k, i, 0))
    return pl.pallas_call(
        body, name="ffn_bwd_act", grid=(n_sh, tp // tm),
        in_specs=[pl.BlockSpec((tm, d), lambda k, i: (i, 0)), pl.BlockSpec((1, fs, d), lambda k, i: (k, 0, 0)),
                  aspec, aspec],
        out_specs=[aspec, aspec],
        out_shape=[_sds((n_sh, tp, fs), BF16)] * 2,
        compiler_params=_params(("arbitrary", "arbitrary")),
    )(dff, wd4, a4, u4)


def _grad_w_down(f4, dff):
    n_sh, tp, fs = f4.shape
    d = dff.shape[1]

    def body(f_ref, dff_ref, g_ref):
        g_ref[0] = _dot(f_ref[0], dff_ref[...], TN).astype(BF16)

    return pl.pallas_call(
        body, name="grad_w_down", grid=(n_sh,),
        in_specs=[pl.BlockSpec((1, tp, fs), lambda k: (k, 0, 0)), _full(dff.shape)],
        out_specs=pl.BlockSpec((1, fs, d), lambda k: (k, 0, 0)),
        out_shape=_sds((n_sh, fs, d), BF16),
        compiler_params=_params(("arbitrary",)),
    )(f4, dff)


def _grad_w_gate_up(xn2, da4, du4):
    n_sh, tp, fs = da4.shape
    d = xn2.shape[1]

    def body(xn_ref, da_ref, du_ref, gg_ref, gu_ref):
        xn = xn_ref[...]
        gg_ref[0] = _dot(xn, da_ref[0], TN).astype(BF16)
        gu_ref[0] = _dot(xn, du_ref[0], TN).astype(BF16)

    aspec = pl.BlockSpec((1, tp, fs), lambda k: (k, 0, 0))
    gspec = pl.BlockSpec((1, d, fs), lambda k: (k, 0, 0))
    return pl.pallas_call(
        body, name="grad_w_gate_up", grid=(n_sh,),
        in_specs=[_full(xn2.shape), aspec, aspec],
        out_specs=[gspec, gspec],
        out_shape=[_sds((n_sh, d, fs), BF16)] * 2,
        compiler_params=_params(("arbitrary",)),
    )(xn2, da4, du4)


def _rms_bwd(dy, x, r, g):
    n = x * r
    dn = dy * g
    return r * (dn - n * _mean(dn * n)), dy * n


def _ffn_bwd_in(da4, du4, wg4, wu4, h1, mix, dh2, g2, gpm):
    n_sh, tp, fs = da4.shape
    d = h1.shape[1]
    tm = _row_tile(tp)

    def body(da_ref, du_ref, wg_ref, wu_ref, h1_ref, mix_ref, dh2_ref, g2_ref, gpm_ref,
             dh1_ref, dmix_ref, dg2_ref, dgpm_ref, acc_ref):
        i, k = pl.program_id(0), pl.program_id(1)
        part = _dot(da_ref[0], wg_ref[0], NT) + _dot(du_ref[0], wu_ref[0], NT)

        @pl.when(k == 0)
        def _():
            acc_ref[...] = part

        @pl.when(k > 0)
        def _():
            acc_ref[...] += part

        @pl.when((i == 0) & (k == 0))
        def _():
            dg2_ref[...] = jnp.zeros(dg2_ref.shape, F32)
            dgpm_ref[...] = jnp.zeros(dgpm_ref.shape, F32)

        @pl.when(k == n_sh - 1)
        def _():
            h1v = h1_ref[...]
            r2 = lax.rsqrt(_mean(h1v * h1v) + RMS_EPS)
            dres, dg2_rows = _rms_bwd(acc_ref[...], h1v, r2, g2_ref[...])
            dh1 = dh2_ref[...] + dres
            dh1_ref[...] = dh1
            mixv = mix_ref[...]
            rm = lax.rsqrt(_mean(mixv * mixv) + RMS_EPS)
            dmix, dgpm_rows = _rms_bwd(dh1, mixv, rm, gpm_ref[...])
            dmix_ref[...] = dmix.astype(BF16)
            dg2_ref[...] += jnp.sum(dg2_rows, axis=0, keepdims=True)
            dgpm_ref[...] += jnp.sum(dgpm_rows, axis=0, keepdims=True)

    aspec = pl.BlockSpec((1, tm, fs), lambda i, k: (k, i, 0))
    wspec = pl.BlockSpec((1, d, fs), lambda i, k: (k, 0, 0))
    row = pl.BlockSpec((tm, d), lambda i, k: (i, 0))
    return pl.pallas_call(
        body, name="ffn_bwd_in", grid=(tp // tm, n_sh),
        in_specs=[aspec, aspec, wspec, wspec, row, row, row, _full(g2.shape), _full(gpm.shape)],
        out_specs=[row, row, _full((1, d)), _full((1, d))],
        out_shape=[_sds((tp, d), F32), _sds((tp, d), BF16), _sds((1, d), F32), _sds((1, d), F32)],
        scratch_shapes=[pltpu.VMEM((tm, d), F32)],
        compiler_params=_params(("arbitrary", "arbitrary")),
    )(da4, du4, wg4, wu4, h1, mix, dh2, g2, gpm)


def _grad_w_out(ya, yb, dmix):
    tp, wa_ = ya.shape
    d = dmix.shape[1]

    def body(ya_ref, yb_ref, dmix_ref, g_ref):
        dm = dmix_ref[...]
        g_ref[0:wa_, :] = _dot(ya_ref[...], dm, TN).astype(BF16)
        g_ref[wa_:2 * wa_, :] = _dot(yb_ref[...], dm, TN).astype(BF16)

    return pl.pallas_call(
        body, name="grad_w_out", grid=(1,),
        in_specs=[_full(ya.shape), _full(yb.shape), _full(dmix.shape)],
        out_specs=_full((2 * wa_, d)),
        out_shape=_sds((2 * wa_, d), BF16),
        compiler_params=_params(("arbitrary",)),
    )(ya, yb, dmix)


def _mix_bwd_out(dmix, wout, z, lg, lb):
    tp, d = dmix.shape
    wa_ = z.shape[1]
    tm = _row_tile(tp)

    def body(dmix_ref, w_ref, z_ref, lg_ref, lb_ref, dya_ref, dz_ref, dlg_ref, dlb_ref):
        i = pl.program_id(0)
        dm = dmix_ref[...]
        dya_ref[...] = _dot(dm, w_ref[0:wa_, :], NT)
        dyb = _dot(dm, w_ref[wa_:d, :], NT)
        lg_ = lg_ref[...]
        rl, zh, l = _layer_norm_parts(z_ref[...], lg_, lb_ref[...])
        sl = _sig(l)
        dl = dyb * (sl * (1.0 + l * (1.0 - sl)))
        dzh = dl * lg_
        dz_ref[...] = rl * (dzh - _mean(dzh) - zh * _mean(dzh * zh))

        @pl.when(i == 0)
        def _():
            dlg_ref[...] = jnp.zeros(dlg_ref.shape, F32)
            dlb_ref[...] = jnp.zeros(dlb_ref.shape, F32)

        dlg_ref[...] += jnp.sum(dl * zh, axis=0, keepdims=True)
        dlb_ref[...] += jnp.sum(dl, axis=0, keepdims=True)

    row = lambda i: (i, 0)
    return pl.pallas_call(
        body, name="mix_bwd_out", grid=(tp // tm,),
        in_specs=[pl.BlockSpec((tm, d), row), _full(wout.shape), pl.BlockSpec((tm, wa_), row), _full(lg.shape),
                  _full(lb.shape)],
        out_specs=[pl.BlockSpec((tm, wa_), row), pl.BlockSpec((tm, wa_), row), _full((1, wa_)), _full((1, wa_))],
        out_shape=[_sds((tp, wa_), F32), _sds((tp, wa_), F32), _sds((1, wa_), F32), _sds((1, wa_), F32)],
        compiler_params=_params(("arbitrary",)),
    )(dmix, wout, z, lg, lb)


def _mix_conv_bwd(hp5, dya, dz, wa, wb):
    _, tp, wgrp = hp5.shape
    seq, nseq = _seq_rows(tp)
    sb = nseq + 2 * CONV_HIST
    ka, kb = wa.shape[0], wb.shape[0]
    xs, ms = slice(0, seq), slice(seq, seq + N_META)
    ox, om = slice(CONV_HIST + N_META, CONV_HIST + nseq), slice(CONV_HIST, CONV_HIST + N_META)
    n_tail = tp - seq - N_META

    def body(hp_ref, dya_ref, dz_ref, wa_ref, wb_ref, dhp_ref, dwa_ref, dwb_ref, dbb_ref, s_ref, d_ref, o_ref, acc_ref):
        _zero_ends(s_ref, nseq)
        _zero_ends(d_ref, nseq)

        def put(p, ox_val, om_val):
            dhp_ref[p, xs, :] = ox_val.astype(BF16)
            dhp_ref[p, ms, :] = om_val.astype(BF16)
            dhp_ref[p, seq + N_META:tp, :] = jnp.zeros((n_tail, LANES), BF16)

        def wgrad(dw_ref, width):
            for k in range(width):
                dw_ref[k:k + 1, :] = jnp.sum(acc_ref[8 * k:8 * k + 8, :], axis=0, keepdims=True)

        _to_seq(s_ref, hp_ref[1, xs, :] * hp_ref[2, xs, :], hp_ref[1, ms, :] * hp_ref[2, ms, :], seq)
        _conv_taps(s_ref, wa_ref, o_ref, ka, nseq, False)
        put(0, dya_ref[xs, :] * o_ref[ox, :], dya_ref[ms, :] * o_ref[om, :])
        _to_seq(d_ref, dya_ref[xs, :] * hp_ref[0, xs, :], dya_ref[ms, :] * hp_ref[0, ms, :], seq)
        _conv_wgrad(s_ref, d_ref, acc_ref, ka, nseq)
        wgrad(dwa_ref, ka)
        _conv_taps(d_ref, wa_ref, o_ref, ka, nseq, True)
        put(1, o_ref[ox, :] * hp_ref[2, xs, :], o_ref[om, :] * hp_ref[2, ms, :])
        put(2, o_ref[ox, :] * hp_ref[1, xs, :], o_ref[om, :] * hp_ref[1, ms, :])

        _to_seq(s_ref, hp_ref[3, xs, :] * _sig(hp_ref[4, xs, :]), hp_ref[3, ms, :] * _sig(hp_ref[4, ms, :]), seq)
        _to_seq(d_ref, dz_ref[xs, :], dz_ref[ms, :], seq)
        dbb_ref[...] = (jnp.sum(dz_ref[xs, :], axis=0, keepdims=True)
                        + jnp.sum(dz_ref[ms, :], axis=0, keepdims=True))
        _conv_wgrad(s_ref, d_ref, acc_ref, kb, nseq)
        wgrad(dwb_ref, kb)
        _conv_taps(d_ref, wb_ref, o_ref, kb, nseq, True)
        sx, sm = _sig(hp_ref[4, xs, :]), _sig(hp_ref[4, ms, :])
        put(3, o_ref[ox, :] * sx, o_ref[om, :] * sm)
        put(4, o_ref[ox, :] * hp_ref[3, xs, :] * sx * (1.0 - sx), o_ref[om, :] * hp_ref[3, ms, :] * sm * (1.0 - sm))

    col = lambda j: (0, j)
    blk5 = pl.BlockSpec((5, tp, LANES), lambda j: (0, 0, j))
    return pl.pallas_call(
        body, name="mix_conv_bwd", grid=(wgrp // LANES,),
        in_specs=[blk5, pl.BlockSpec((tp, LANES), col), pl.BlockSpec((tp, LANES), col),
                  pl.BlockSpec((ka, LANES), col), pl.BlockSpec((kb, LANES), col)],
        out_specs=[blk5, pl.BlockSpec((ka, LANES), col), pl.BlockSpec((kb, LANES), col), pl.BlockSpec((1, LANES), col)],
        out_shape=[_sds((5, tp, wgrp), BF16), _sds((ka, wgrp), F32), _sds((kb, wgrp), F32), _sds((1, wgrp), F32)],
        scratch_shapes=[pltpu.VMEM((sb, LANES), F32), pltpu.VMEM((sb, LANES), F32), pltpu.VMEM((sb, LANES), F32),
                        pltpu.VMEM((8 * kb, LANES), F32)],
        compiler_params=_params(("arbitrary",)),
    )(hp5, dya, dz, wa, wb)


def _grad_w_in(xn1, dhp5):
    n_p, tp, pw = dhp5.shape
    d = xn1.shape[1]

    def body(xn_ref, dhp_ref, g_ref):
        g_ref[...] = _dot(xn_ref[...], dhp_ref[0], TN).astype(BF16)

    return pl.pallas_call(
        body, name="grad_w_in", grid=(n_p,),
        in_specs=[_full(xn1.shape), pl.BlockSpec((1, tp, pw), lambda p: (p, 0, 0))],
        out_specs=pl.BlockSpec((d, pw), lambda p: (0, p)),
        out_shape=_sds((d, n_p * pw), BF16),
        compiler_params=_params(("arbitrary",)),
    )(xn1, dhp5)


def _mix_bwd_in(dhp5, win4, h, dh1, g1):
    n_p, tp, pw = dhp5.shape
    d = h.shape[1]
    n_sh, _, csh = win4.shape
    tm = _row_tile(tp)
    pieces = _pieces(n_sh, csh, pw)

    def body(dhp_ref, w_ref, h_ref, dh1_ref, g_ref, dh_ref, dg1_ref):
        i = pl.program_id(0)
        dxn = None
        for k, klo, p, plo, w in pieces:
            t = _dot(dhp_ref[p, :, plo:plo + w], w_ref[k, :, klo:klo + w], NT)
            dxn = t if dxn is None else dxn + t
        hh = h_ref[...]
        r1 = lax.rsqrt(_mean(hh * hh) + RMS_EPS)
        dres, dg_rows = _rms_bwd(dxn, hh, r1, g_ref[...])
        dh_ref[...] = dh1_ref[...] + dres

        @pl.when(i == 0)
        def _():
            dg1_ref[...] = jnp.zeros(dg1_ref.shape, F32)

        dg1_ref[...] += jnp.sum(dg_rows, axis=0, keepdims=True)

    row = lambda i: (i, 0)
    return pl.pallas_call(
        body, name="mix_bwd_in", grid=(tp // tm,),
        in_specs=[pl.BlockSpec((n_p, tm, pw), lambda i: (0, i, 0)), _full(win4.shape), pl.BlockSpec((tm, d), row),
                  pl.BlockSpec((tm, d), row), _full(g1.shape)],
        out_specs=[pl.BlockSpec((tm, d), row), _full((1, d))],
        out_shape=[_sds((tp, d), F32), _sds((1, d), F32)],
        compiler_params=_params(("arbitrary",)),
    )(dhp5, win4, h, dh1, g1)


def _other_chips(x, y):
    out = []
    for j in (1, 2, 3):
        px, py = _flip(x, j >> 1), _flip(y, j & 1)
        out.append((px, py, 2 * px + py))
    return out


def _half_rows(c, rows_half):
    return pl.ds(pl.multiple_of(c * rows_half, 8), rows_half)


def _gather_shards(locs):
    n = len(locs)
    halves = [a.shape[0] // 2 for a in locs]

    def body(*refs):
        loc, full = refs[:n], refs[n:2 * n]
        ssem, rsem, lsem = refs[2 * n:]
        x, y, c = _mesh_pos()
        q = 2 * x + y
        chips = _other_chips(x, y)

        def remote(i, chip_no, half, to, s):
            part = full[i].at[chip_no, _half_rows(half, halves[i]), :]
            return pltpu.make_async_remote_copy(src_ref=part, dst_ref=part, send_sem=ssem.at[s], recv_sem=rsem.at[s],
                                                device_id=to, device_id_type=MESH)

        own = [pltpu.make_async_copy(loc[i], full[i].at[q], lsem.at[i]) for i in range(n)]
        for cp in own:
            cp.start()
        first = []
        for i in range(n):
            for j, (px, py, _) in enumerate(chips):
                src = loc[i].at[_half_rows(c, halves[i]), :]
                dst = full[i].at[q, _half_rows(c, halves[i]), :]
                cp = pltpu.make_async_remote_copy(src_ref=src, dst_ref=dst, send_sem=ssem.at[3 * i + j],
                                                  recv_sem=rsem.at[3 * i + j], device_id=(px, py, c), device_id_type=MESH)
                cp.start()
                first.append(cp)
        passed = []
        for i in range(n):
            for j, (_, _, qj) in enumerate(chips):
                remote(i, qj, c, (x, y, c), 3 * i + j).wait_recv()
                cp = remote(i, qj, c, (x, y, 1 - c), 3 * n + 3 * i + j)
                cp.start()
                passed.append(cp)
        for i in range(n):
            for j, (_, _, qj) in enumerate(chips):
                remote(i, qj, 1 - c, (x, y, c), 3 * n + 3 * i + j).wait_recv()
        for cp in first + passed:
            cp.wait_send()
        for cp in own:
            cp.wait()

    return pl.pallas_call(
        body, name="gather_shards",
        in_specs=[ANY] * n, out_specs=[ANY] * n,
        out_shape=[_sds((N_CHIPS,) + a.shape, a.dtype) for a in locs],
        scratch_shapes=[pltpu.SemaphoreType.DMA((6 * n,)), pltpu.SemaphoreType.DMA((6 * n,)),
                        pltpu.SemaphoreType.DMA((n,))],
    )(*locs)


def _pair_exchange_grads(grads, half_axis):
    n = len(grads)

    def half_of(ref, i, which):
        rows = grads[i].shape[half_axis[i]] // 2
        if half_axis[i] == 0:
            return ref.at[_half_rows(which, rows), :]
        return ref.at[:, _half_rows(which, rows), :]

    def out_shape(i):
        s = list(grads[i].shape)
        s[half_axis[i]] //= 2
        return _sds(tuple(s), grads[i].dtype)

    def body(*refs):
        g, got = refs[:n], refs[n:2 * n]
        ssem, rsem = refs[2 * n:]
        x, y, c = _mesh_pos()
        cps = []
        for i in range(n):
            cp = pltpu.make_async_remote_copy(src_ref=half_of(g[i], i, 1 - c), dst_ref=got[i], send_sem=ssem.at[i],
                                              recv_sem=rsem.at[i], device_id=(x, y, 1 - c), device_id_type=MESH)
            cp.start()
            cps.append(cp)
        for cp in cps:
            cp.wait()

    return pl.pallas_call(
        body, name="pair_exchange_grads",
        in_specs=[ANY] * n, out_specs=[ANY] * n,
        out_shape=[out_shape(i) for i in range(n)],
        scratch_shapes=[pltpu.SemaphoreType.DMA((n,)), pltpu.SemaphoreType.DMA((n,))],
    )(*grads)


def _pair_sum(g, got, c_arr, col_sharded):
    if col_sharded:
        rows, cols = g.shape
        rh, cs = rows // 2, cols // N_CHIPS
        g_spec = pl.BlockSpec((rh, cs), lambda k, c_ref: (c_ref[0], k))
        got_spec = pl.BlockSpec((rh, cs), lambda k, c_ref: (0, k))
    else:
        _, rows, cs = g.shape
        rh = rows // 2
        g_spec = pl.BlockSpec((1, rh, cs), lambda k, c_ref: (k, c_ref[0], 0))
        got_spec = pl.BlockSpec((1, rh, cs), lambda k, c_ref: (k, 0, 0))

    def body(c_ref, g_ref, got_ref, out_ref):
        total = g_ref[...].astype(F32) + got_ref[...].astype(F32)
        out_ref[...] = total.astype(BF16).reshape(out_ref.shape)

    return pl.pallas_call(
        body, name="pair_sum",
        grid_spec=pltpu.PrefetchScalarGridSpec(
            num_scalar_prefetch=1, grid=(N_CHIPS,), in_specs=[g_spec, got_spec],
            out_specs=pl.BlockSpec((1, rh, cs), lambda k, c_ref: (k, 0, 0))),
        out_shape=_sds((N_CHIPS, rh, cs), BF16),
        compiler_params=_params(("arbitrary",)),
    )(c_arr, g, got)


def _chip_exchange_grads(parts):
    n = len(parts)

    def body(*refs):
        src, got = refs[:n], refs[n:2 * n]
        ssem, rsem = refs[2 * n:]
        x, y, c = _mesh_pos()
        cps = []
        for i in range(n):
            for j, (px, py, qj) in enumerate(_other_chips(x, y)):
                cp = pltpu.make_async_remote_copy(src_ref=src[i].at[qj], dst_ref=got[i].at[j], send_sem=ssem.at[3 * i + j],
                                                  recv_sem=rsem.at[3 * i + j], device_id=(px, py, c), device_id_type=MESH)
                cp.start()
                cps.append(cp)
        for cp in cps:
            cp.wait()

    return pl.pallas_call(
        body, name="chip_exchange_grads",
        in_specs=[ANY] * n, out_specs=[ANY] * n,
        out_shape=[_sds((3,) + a.shape[1:], a.dtype) for a in parts],
        scratch_shapes=[pltpu.SemaphoreType.DMA((3 * n,)), pltpu.SemaphoreType.DMA((3 * n,))],
    )(*parts)


def _chip_sum(part, got, qc_arr):
    _, rh, cs = part.shape

    def body(qc_ref, part_ref, got_ref, out_ref):
        total = part_ref[0].astype(F32)
        for j in range(3):
            total = total + got_ref[j].astype(F32)
        out_ref[...] = total

    return pl.pallas_call(
        body, name="chip_sum",
        grid_spec=pltpu.PrefetchScalarGridSpec(
            num_scalar_prefetch=1, grid=(1,),
            in_specs=[pl.BlockSpec((1, rh, cs), lambda i, qc: (qc[0], 0, 0)), pl.BlockSpec((3, rh, cs), lambda i, qc: (0, 0, 0))],
            out_specs=pl.BlockSpec((rh, cs), lambda i, qc: (qc[1], 0))),
        out_shape=_sds((2 * rh, cs), F32),
        compiler_params=_params(("arbitrary",)),
    )(qc_arr, part, got)


def _pair_share_grads(grads):
    n = len(grads)

    def body(*refs):
        g = refs[n:2 * n]
        ssem, rsem = refs[2 * n:]
        x, y, c = _mesh_pos()
        cps = []
        for i in range(n):
            mine = g[i].at[_half_rows(c, grads[i].shape[0] // 2), :]
            cp = pltpu.make_async_remote_copy(src_ref=mine, dst_ref=mine, send_sem=ssem.at[i], recv_sem=rsem.at[i],
                                              device_id=(x, y, 1 - c), device_id_type=MESH)
            cp.start()
            cps.append(cp)
        for cp in cps:
            cp.wait()

    return pl.pallas_call(
        body, name="pair_share_grads",
        in_specs=[ANY] * n, out_specs=[ANY] * n,
        out_shape=[_sds(a.shape, a.dtype) for a in grads],
        input_output_aliases={i: i for i in range(n)},
        scratch_shapes=[pltpu.SemaphoreType.DMA((n,)), pltpu.SemaphoreType.DMA((n,))],
    )(*grads)


def _small_allreduce(parts, rows_total):
    n = len(parts)
    width = max(p.shape[1] for p in parts)

    def body(*refs):
        ins, out_ref = refs[:n], refs[n]
        pack, got, ssem, rsem = refs[n + 1:]
        x, y, c = _mesh_pos()
        me = 4 * x + 2 * y + c
        pack[...] = jnp.zeros(pack.shape, F32)
        r0 = 0
        for i in range(n):
            r, w = parts[i].shape
            pack[r0:r0 + r, 0:w] = ins[i][...]
            r0 += r
        cps = []
        for rel in range(1, 8):
            to = (_flip(x, (rel >> 2) & 1), _flip(y, (rel >> 1) & 1), _flip(c, rel & 1))
            cp = pltpu.make_async_remote_copy(src_ref=pack, dst_ref=got.at[rel - 1], send_sem=ssem.at[rel - 1],
                                              recv_sem=rsem.at[rel - 1], device_id=to, device_id_type=MESH)
            cp.start()
            cps.append(cp)
        for cp in cps:
            cp.wait()
        total = jnp.zeros(pack.shape, F32)
        for dev in range(8):
            rel = jnp.bitwise_xor(me, dev)
            theirs = got[jnp.maximum(rel - 1, 0)]
            total = total + jnp.where(rel == 0, pack[...], theirs)
        out_ref[...] = total

    return pl.pallas_call(
        body, name="small_allreduce",
        in_specs=[VMEM] * n, out_specs=VMEM,
        out_shape=_sds((rows_total, width), F32),
        scratch_shapes=[pltpu.VMEM((rows_total, width), F32), pltpu.VMEM((7, rows_total, width), F32),
                        pltpu.SemaphoreType.DMA((7,)), pltpu.SemaphoreType.DMA((7,))],
        compiler_params=_params(),
    )(*parts)


def _adamw_math(w, g, m, v):
    m2 = ADAM_B1 * m + (1.0 - ADAM_B1) * g
    v2 = ADAM_B2 * v + (1.0 - ADAM_B2) * (g * g)
    m_hat = m2 / (1.0 - ADAM_B1 ** ADAM_STEP)
    v_hat = v2 / (1.0 - ADAM_B2 ** ADAM_STEP)
    delta = -ADAM_LR * (m_hat / (jnp.sqrt(v_hat) + ADAM_EPS) + ADAM_WD * w)
    return delta, m2, v2


def _adamw_big(w, g, m, v):
    rows, cols = w.shape
    tr = _row_tile(rows)

    def body(w_ref, g_ref, m_ref, v_ref, go_ref, d_ref, m2_ref, v2_ref):
        gg = g_ref[...]
        go_ref[...] = gg
        d_ref[...], m2_ref[...], v2_ref[...] = _adamw_math(w_ref[...], gg, m_ref[...], v_ref[...])

    spec = pl.BlockSpec((tr, cols), lambda i: (i, 0))
    return pl.pallas_call(
        body, name="adamw_big", grid=(rows // tr,),
        in_specs=[spec] * 4, out_specs=[spec] * 4,
        out_shape=[_sds((rows, cols), F32)] * 4,
        compiler_params=_params(("arbitrary",)),
    )(w, g, m, v)


def _adamw_small(ws, gs, ms, vs):
    n = len(ws)

    def body(*refs):
        w_r, g_r, m_r, v_r = refs[:n], refs[n:2 * n], refs[2 * n:3 * n], refs[3 * n:4 * n]
        d_o, m_o, v_o = refs[4 * n:5 * n], refs[5 * n:6 * n], refs[6 * n:7 * n]
        for i in range(n):
            d_o[i][...], m_o[i][...], v_o[i][...] = _adamw_math(w_r[i][...], g_r[i][...], m_r[i][...], v_r[i][...])

    shapes = [_sds(w.shape, F32) for w in ws]
    outs = pl.pallas_call(
        body, name="adamw_small",
        in_specs=[VMEM] * (4 * n), out_specs=[VMEM] * (3 * n),
        out_shape=shapes * 3,
        compiler_params=_params(),
    )(*ws, *gs, *ms, *vs)
    return outs[:n], outs[n:2 * n], outs[2 * n:]


SMALL_ROWS = 64
PACK_ROWS = 64


def kernel(x, meta_tokens, pre_mix_norm, w_in, conv_a_w, conv_b_w, conv_b_bias, ln_b_gain, ln_b_bias, w_out, post_mix_norm, pre_ffn_norm, w_gate, w_up, w_down, post_ffn_norm, loss_target, m_meta_tokens, m_pre_mix_norm, m_w_in, m_conv_a_w, m_conv_b_w, m_conv_b_bias, m_ln_b_gain, m_ln_b_bias, m_w_out, m_post_mix_norm, m_pre_ffn_norm, m_w_gate, m_w_up, m_w_down, m_post_ffn_norm, v_meta_tokens, v_pre_mix_norm, v_w_in, v_conv_a_w, v_conv_b_w, v_conv_b_bias, v_ln_b_gain, v_ln_b_bias, v_w_out, v_post_mix_norm, v_pre_ffn_norm, v_w_gate, v_w_up, v_w_down, v_post_ffn_norm):
    xq, yq, cq = lax.axis_index("x"), lax.axis_index("y"), lax.axis_index("c")
    chip = 2 * xq + yq
    c_arr = jnp.reshape(cq, (1,)).astype(jnp.int32)
    qc_arr = jnp.stack([chip, cq]).astype(jnp.int32)

    seq, d = x.shape[1], x.shape[2]
    x2, tgt2 = x[0], loss_target[0]
    w_in2, w_out2, w_gate2, w_up2, w_down2 = w_in[0], w_out[0], w_gate[0], w_up[0], w_down[0]
    ka, wa_sh = conv_a_w.shape[1], conv_a_w.shape[2]
    kb = conv_b_w.shape[1]
    meta_sh = meta_tokens.shape[1]

    small = jnp.zeros((PACK_ROWS, meta_sh), F32)
    small = small.at[0:N_META, :].set(meta_tokens)
    small = small.at[16:16 + ka, 0:wa_sh].set(conv_a_w[0])
    small = small.at[24:24 + kb, 0:wa_sh].set(conv_b_w[0])
    locs = [w_in2.astype(BF16), w_out2.astype(BF16), w_gate2.astype(BF16), w_up2.astype(BF16), w_down2.astype(BF16), small]
    win4, wout4, wg4, wu4, wd4, small4 = _gather_shards(locs)
    wout_f = wout4.reshape(N_CHIPS * wout4.shape[1], wout4.shape[2])
    meta_f = jnp.concatenate([small4[k, 0:N_META, :] for k in range(N_CHIPS)], axis=1)
    wa_f = jnp.concatenate([small4[k, 16:16 + ka, 0:wa_sh] for k in range(N_CHIPS)], axis=1)
    wb_f = jnp.concatenate([small4[k, 24:24 + kb, 0:wa_sh] for k in range(N_CHIPS)], axis=1)

    tail = jnp.zeros((TAIL_ROWS - N_META, d), F32)
    h = jnp.concatenate([x2, meta_f, tail], axis=0)
    tgt = jnp.concatenate([tgt2, jnp.zeros((TAIL_ROWS, d), F32)], axis=0)
    xn1, hp5 = _mm_in(h, win4, pre_mix_norm)
    ya, z = _mix_conv_fwd(hp5, wa_f, wb_f, conv_b_bias)
    yb, mix, h1, xn2 = _mm_out(ya, z, h, wout_f, ln_b_gain, ln_b_bias, post_mix_norm, pre_ffn_norm)
    a4, u4, f4 = _ffn_up(xn2, wg4, wu4)
    dff, dh2, loss_blk, d_gpf = _ffn_down(f4, wd4, h1, tgt, post_ffn_norm)

    da4, du4 = _ffn_bwd_act(dff, wd4, a4, u4)
    g_down = _grad_w_down(f4, dff)
    g_gate, g_up = _grad_w_gate_up(xn2, da4, du4)
    dh1, dmix, d_g2, d_gpm = _ffn_bwd_in(da4, du4, wg4, wu4, h1, mix, dh2, pre_ffn_norm, post_mix_norm)
    g_out = _grad_w_out(ya, yb, dmix)
    dya, dz, d_lg, d_lb = _mix_bwd_out(dmix, wout_f, z, ln_b_gain, ln_b_bias)
    dhp5, d_wa, d_wb, d_bb = _mix_conv_bwd(hp5, dya, dz, wa_f, wb_f)
    g_in = _grad_w_in(xn1, dhp5)
    dh, d_g1 = _mix_bwd_in(dhp5, win4, h, dh1, pre_mix_norm)
    grad_x = dh[0:seq][None]
    d_meta = dh[seq:seq + N_META]

    g_out4 = g_out.reshape(N_CHIPS, g_out.shape[0] // N_CHIPS, g_out.shape[1])
    big = [g_in, g_out4, g_gate, g_up, g_down]
    got = _pair_exchange_grads(big, [0, 1, 1, 1, 1])
    parts = [_pair_sum(big[i], got[i], c_arr, i == 0) for i in range(5)]
    recv = _chip_exchange_grads(parts)
    halves = [_chip_sum(parts[i], recv[i], qc_arr) for i in range(5)]
    gsum = _pair_share_grads(halves)

    small_parts = [d_g1, d_gpm, d_g2, d_gpf, d_bb, d_lg, d_lb, d_wa, d_wb, d_meta]
    red = _small_allreduce(small_parts, SMALL_ROWS)
    r0 = 0
    small_sum = []
    for p in small_parts:
        small_sum.append(red[r0:r0 + p.shape[0], 0:p.shape[1]])
        r0 += p.shape[0]
    s_g1, s_gpm, s_g2, s_gpf, s_bb, s_lg, s_lb, s_wa, s_wb, s_meta = small_sum
    g_meta = lax.dynamic_slice_in_dim(s_meta, chip * meta_sh, meta_sh, axis=1)
    g_wa = lax.dynamic_slice_in_dim(s_wa, chip * wa_sh, wa_sh, axis=1)[None]
    g_wb = lax.dynamic_slice_in_dim(s_wb, chip * wa_sh, wa_sh, axis=1)[None]

    loss = lax.psum(loss_blk[0, 0], ("x", "y", "c"))

    names_big = ["w_in", "w_out", "w_gate", "w_up", "w_down"]
    w_big = dict(zip(names_big, [w_in2, w_out2, w_gate2, w_up2, w_down2]))
    m_big = dict(zip(names_big, [m_w_in[0], m_w_out[0], m_w_gate[0], m_w_up[0], m_w_down[0]]))
    v_big = dict(zip(names_big, [v_w_in[0], v_w_out[0], v_w_gate[0], v_w_up[0], v_w_down[0]]))
    grads, deltas, new_m, new_v = {}, {}, {}, {}
    for nm, g in zip(names_big, gsum):
        go, dl, m2, v2 = _adamw_big(w_big[nm], g, m_big[nm], v_big[nm])
        grads[nm], deltas[nm], new_m[nm], new_v[nm] = go[None], dl[None], m2[None], v2[None]

    names_small = ["meta_tokens", "pre_mix_norm", "conv_a_w", "conv_b_w", "conv_b_bias", "ln_b_gain", "ln_b_bias",
                   "post_mix_norm", "pre_ffn_norm", "post_ffn_norm"]
    w_small = [meta_tokens, pre_mix_norm, conv_a_w[0], conv_b_w[0], conv_b_bias, ln_b_gain, ln_b_bias, post_mix_norm,
               pre_ffn_norm, post_ffn_norm]
    g_small = [g_meta, s_g1, g_wa[0], g_wb[0], s_bb, s_lg, s_lb, s_gpm, s_g2, s_gpf]
    m_small = [m_meta_tokens, m_pre_mix_norm, m_conv_a_w[0], m_conv_b_w[0], m_conv_b_bias, m_ln_b_gain, m_ln_b_bias,
               m_post_mix_norm, m_pre_ffn_norm, m_post_ffn_norm]
    v_small = [v_meta_tokens, v_pre_mix_norm, v_conv_a_w[0], v_conv_b_w[0], v_conv_b_bias, v_ln_b_gain, v_ln_b_bias,
               v_post_mix_norm, v_pre_ffn_norm, v_post_ffn_norm]
    d_s, m_s, v_s = _adamw_small(w_small, g_small, m_small, v_small)
    for i, nm in enumerate(names_small):
        lead = nm in ("conv_a_w", "conv_b_w")
        fix = (lambda a: a[None]) if lead else (lambda a: a)
        grads[nm], deltas[nm], new_m[nm], new_v[nm] = fix(g_small[i]), fix(d_s[i]), fix(m_s[i]), fix(v_s[i])

    order = ["meta_tokens", "pre_mix_norm", "w_in", "conv_a_w", "conv_b_w", "conv_b_bias", "ln_b_gain", "ln_b_bias", "w_out",
             "post_mix_norm", "pre_ffn_norm", "w_gate", "w_up", "w_down", "post_ffn_norm"]
    return (loss, grad_x, *[grads[k] for k in order], *[deltas[k] for k in order], *[new_m[k] for k in order],
            *[new_v[k] for k in order])
```

```python
import functools

import jax
import jax.numpy as jnp
from jax import lax
from jax.experimental import pallas as pl
from jax.experimental.pallas import tpu as pltpu

F32 = jnp.float32
BF16 = jnp.bfloat16
MESH = pl.DeviceIdType.MESH

N_META = 16
TAIL_ROWS = 128
RMS_EPS = 1e-6
LN_EPS = 1e-5
ADAM_LR = 0.001
ADAM_B1 = 0.9
ADAM_B2 = 0.999
ADAM_EPS = 1e-08
ADAM_WD = 0.01
ADAM_STEP = 10

N_CHIPS = 4
LANES = 128
CONV_CHUNK = 48
CONV_HIST = 32
ROW_TILE_CAP = 640
VMEM_LIMIT = 56 * 1024 * 1024

NN = (((1,), (0,)), ((), ()))
NT = (((1,), (1,)), ((), ()))
TN = (((0,), (0,)), ((), ()))


def _dot(a, b, dims=NN):
    return lax.dot_general(a, b, dims, preferred_element_type=F32)


def _sig(v):
    return 1.0 / (1.0 + jnp.exp(-v))


def _mean(v):
    return jnp.mean(v, axis=-1, keepdims=True)


def _row_tile(rows):
    best = 16
    for t in range(16, min(rows, ROW_TILE_CAP) + 1, 16):
        if rows % t == 0:
            best = t
    assert rows % best == 0
    return best


def _pieces(n_shards, shard_w, piece_w):
    total = n_shards * shard_w
    cuts = sorted(set(range(0, total + 1, shard_w)) | set(range(0, total + 1, piece_w)))
    out = []
    for lo, hi in zip(cuts[:-1], cuts[1:]):
        out.append((lo // shard_w, lo % shard_w, lo // piece_w, lo % piece_w, hi - lo))
    return out


def _params(semantics=None):
    kw = dict(vmem_limit_bytes=VMEM_LIMIT)
    if semantics is not None:
        kw["dimension_semantics"] = semantics
    return pltpu.CompilerParams(**kw)


def _full(shape):
    nd = len(shape)
    return pl.BlockSpec(shape, lambda *_: (0,) * nd)


def _sds(shape, dtype):
    return jax.ShapeDtypeStruct(shape, dtype)


ANY = pl.BlockSpec(memory_space=pl.ANY)
VMEM = pl.BlockSpec(memory_space=pltpu.VMEM)


def _mesh_pos():
    return lax.axis_index("x"), lax.axis_index("y"), lax.axis_index("c")


def _flip(v, bit):
    return 1 - v if bit else v


def _mm_in(h, win4, g1):
    tp, d = h.shape
    tm = _row_tile(tp)
    n_sh, _, csh = win4.shape
    pw = n_sh * csh // 5
    pieces = _pieces(n_sh, csh, pw)

    def body(h_ref, w_ref, g_ref, xn_ref, hp_ref):
        hh = h_ref[...]
        r = lax.rsqrt(_mean(hh * hh) + RMS_EPS)
        xn = (hh * r * g_ref[...]).astype(BF16)
        xn_ref[...] = xn
        for k, klo, p, plo, w in pieces:
            hp_ref[p, :, plo:plo + w] = _dot(xn, w_ref[k, :, klo:klo + w])

    return pl.pallas_call(
        body, name="mm_in", grid=(tp // tm,),
        in_specs=[pl.BlockSpec((tm, d), lambda i: (i, 0)), _full(win4.shape), _full(g1.shape)],
        out_specs=[pl.BlockSpec((tm, d), lambda i: (i, 0)), pl.BlockSpec((5, tm, pw), lambda i: (0, i, 0))],
        out_shape=[_sds((tp, d), BF16), _sds((5, tp, pw), F32)],
        compiler_params=_params(("arbitrary",)),
    )(h, win4, g1)


def _seq_rows(tp):
    seq = tp - TAIL_ROWS
    nseq = seq + N_META
    assert nseq % CONV_CHUNK == 0 and seq % 16 == 0
    return seq, nseq


def _conv_taps(src_ref, w_ref, dst_ref, width, nseq, transpose):
    w = w_ref[...]

    def step(n, carry):
        out0 = pl.multiple_of(CONV_HIST + n * CONV_CHUNK, 8)
        win0 = out0 if transpose else pl.multiple_of(n * CONV_CHUNK, 8)
        win = src_ref[pl.ds(win0, CONV_CHUNK + CONV_HIST), :]
        acc = jnp.zeros((CONV_CHUNK, w.shape[1]), F32)
        for k in range(width):
            off = (width - 1 - k) if transpose else (CONV_HIST - (width - 1) + k)
            acc = acc + w[k:k + 1, :] * win[off:off + CONV_CHUNK, :]
        dst_ref[pl.ds(out0, CONV_CHUNK), :] = acc
        return carry

    lax.fori_loop(0, nseq // CONV_CHUNK, step, 0)


def _conv_wgrad(src_ref, dz_ref, acc_ref, width, nseq):
    acc_ref[...] = jnp.zeros(acc_ref.shape, F32)

    def step(n, carry):
        win = src_ref[pl.ds(pl.multiple_of(n * CONV_CHUNK, 8), CONV_CHUNK + CONV_HIST), :]
        dzc = dz_ref[pl.ds(pl.multiple_of(CONV_HIST + n * CONV_CHUNK, 8), CONV_CHUNK), :]
        for k in range(width):
            off = CONV_HIST - (width - 1) + k
            prod = dzc * win[off:off + CONV_CHUNK, :]
            part = prod[0:8, :]
            for s in range(1, CONV_CHUNK // 8):
                part = part + prod[8 * s:8 * s + 8, :]
            acc_ref[8 * k:8 * k + 8, :] += part
        return carry

    lax.fori_loop(0, nseq // CONV_CHUNK, step, 0)


def _to_seq(buf_ref, x_part, meta_part, seq):
    buf_ref[CONV_HIST:CONV_HIST + N_META, :] = meta_part
    buf_ref[CONV_HIST + N_META:CONV_HIST + N_META + seq, :] = x_part


def _zero_ends(buf_ref, nseq):
    zeros = jnp.zeros((CONV_HIST, buf_ref.shape[1]), F32)
    buf_ref[0:CONV_HIST, :] = zeros
    buf_ref[CONV_HIST + nseq:CONV_HIST + nseq + CONV_HIST, :] = zeros


def _mix_conv_fwd(hp5, wa, wb, bb):
    _, tp, wgrp = hp5.shape
    seq, nseq = _seq_rows(tp)
    sb = nseq + 2 * CONV_HIST
    ka, kb = wa.shape[0], wb.shape[0]
    xs, ms = slice(0, seq), slice(seq, seq + N_META)
    ox, om = slice(CONV_HIST + N_META, CONV_HIST + nseq), slice(CONV_HIST, CONV_HIST + N_META)

    def body(hp_ref, wa_ref, wb_ref, bb_ref, ya_ref, z_ref, s_ref, o_ref):
        _zero_ends(s_ref, nseq)
        _to_seq(s_ref, hp_ref[1, xs, :] * hp_ref[2, xs, :], hp_ref[1, ms, :] * hp_ref[2, ms, :], seq)
        _conv_taps(s_ref, wa_ref, o_ref, ka, nseq, False)
        ya_ref[xs, :] = (hp_ref[0, xs, :] * o_ref[ox, :]).astype(BF16)
        ya_ref[ms, :] = (hp_ref[0, ms, :] * o_ref[om, :]).astype(BF16)
        ya_ref[seq + N_META:tp, :] = jnp.zeros((tp - seq - N_META, LANES), BF16)
        _to_seq(s_ref, hp_ref[3, xs, :] * _sig(hp_ref[4, xs, :]), hp_ref[3, ms, :] * _sig(hp_ref[4, ms, :]), seq)
        _conv_taps(s_ref, wb_ref, o_ref, kb, nseq, False)
        z_ref[xs, :] = o_ref[ox, :] + bb_ref[...]
        z_ref[ms, :] = o_ref[om, :] + bb_ref[...]
        z_ref[seq + N_META:tp, :] = jnp.zeros((tp - seq - N_META, LANES), F32)

    col = lambda j: (0, j)
    return pl.pallas_call(
        body, name="mix_conv_fwd", grid=(wgrp // LANES,),
        in_specs=[pl.BlockSpec((5, tp, LANES), lambda j: (0, 0, j)), pl.BlockSpec((ka, LANES), col),
                  pl.BlockSpec((kb, LANES), col), pl.BlockSpec((1, LANES), col)],
        out_specs=[pl.BlockSpec((tp, LANES), col), pl.BlockSpec((tp, LANES), col)],
        out_shape=[_sds((tp, wgrp), BF16), _sds((tp, wgrp), F32)],
        scratch_shapes=[pltpu.VMEM((sb, LANES), F32), pltpu.VMEM((sb, LANES), F32)],
        compiler_params=_params(("arbitrary",)),
    )(hp5, wa, wb, bb)


def _layer_norm_parts(z, lg, lb):
    mu = _mean(z)
    zc = z - mu
    rl = lax.rsqrt(_mean(zc * zc) + LN_EPS)
    zh = zc * rl
    return rl, zh, zh * lg + lb


def _mm_out(ya, z, h, wout, lg, lb, gpm, g2):
    tp, d = h.shape
    wa_ = ya.shape[1]
    tm = _row_tile(tp)

    def body(ya_ref, z_ref, h_ref, w_ref, lg_ref, lb_ref, gpm_ref, g2_ref, yb_ref, mix_ref, h1_ref, xn2_ref):
        _, _, l = _layer_norm_parts(z_ref[...], lg_ref[...], lb_ref[...])
        yb = (l * _sig(l)).astype(BF16)
        yb_ref[...] = yb
        mix = _dot(ya_ref[...], w_ref[0:wa_, :]) + _dot(yb, w_ref[wa_:d, :])
        mix_ref[...] = mix
        rm = lax.rsqrt(_mean(mix * mix) + RMS_EPS)
        h1 = h_ref[...] + mix * rm * gpm_ref[...]
        h1_ref[...] = h1
        r2 = lax.rsqrt(_mean(h1 * h1) + RMS_EPS)
        xn2_ref[...] = (h1 * r2 * g2_ref[...]).astype(BF16)

    row = lambda i: (i, 0)
    return pl.pallas_call(
        body, name="mm_out", grid=(tp // tm,),
        in_specs=[pl.BlockSpec((tm, wa_), row), pl.BlockSpec((tm, wa_), row), pl.BlockSpec((tm, d), row),
                  _full(wout.shape), _full(lg.shape), _full(lb.shape), _full(gpm.shape), _full(g2.shape)],
        out_specs=[pl.BlockSpec((tm, wa_), row), pl.BlockSpec((tm, d), row), pl.BlockSpec((tm, d), row),
                   pl.BlockSpec((tm, d), row)],
        out_shape=[_sds((tp, wa_), BF16), _sds((tp, d), F32), _sds((tp, d), F32), _sds((tp, d), BF16)],
        compiler_params=_params(("arbitrary",)),
    )(ya, z, h, wout, lg, lb, gpm, g2)


def _ffn_up(xn2, wg4, wu4):
    tp, d = xn2.shape
    n_sh, fs, _ = wg4.shape
    tm = _row_tile(tp)

    def body(xn_ref, wg_ref, wu_ref, a_ref, u_ref, f_ref):
        xn = xn_ref[...]
        a = _dot(xn, wg_ref[0], NT)
        u = _dot(xn, wu_ref[0], NT)
        a_ref[0] = a.astype(BF16)
        u_ref[0] = u.astype(BF16)
        f_ref[0] = (a * _sig(a) * u).astype(BF16)

    wspec = pl.BlockSpec((1, fs, d), lambda k, i: (k, 0, 0))
    ospec = pl.BlockSpec((1, tm, fs), lambda k, i: (k, i, 0))
    return pl.pallas_call(
        body, name="ffn_up", grid=(n_sh, tp // tm),
        in_specs=[pl.BlockSpec((tm, d), lambda k, i: (i, 0)), wspec, wspec],
        out_specs=[ospec, ospec, ospec],
        out_shape=[_sds((n_sh, tp, fs), BF16)] * 3,
        compiler_params=_params(("arbitrary", "arbitrary")),
    )(xn2, wg4, wu4)


def _ffn_down(f4, wd4, h1, tgt, gpf):
    n_sh, tp, fs = f4.shape
    d = h1.shape[1]
    tm = _row_tile(tp)
    seq, _ = _seq_rows(tp)

    def body(f_ref, w_ref, h1_ref, t_ref, gpf_ref, dff_ref, dh2_ref, loss_ref, dgpf_ref):
        i = pl.program_id(0)
        ff = _dot(f_ref[0], w_ref[0])
        for k in range(1, n_sh):
            ff = ff + _dot(f_ref[k], w_ref[k])
        rf = lax.rsqrt(_mean(ff * ff) + RMS_EPS)
        nf = ff * rf
        gpf_ = gpf_ref[...]
        h2 = h1_ref[...] + nf * gpf_
        rows = i * tm + lax.broadcasted_iota(jnp.int32, (tm, 1), 0)
        err = jnp.where(rows < seq, h2 - t_ref[...], 0.0)
        dh2 = err * (1.0 / d)
        dh2_ref[...] = dh2
        dn = dh2 * gpf_
        dff_ref[...] = (rf * (dn - nf * _mean(dn * nf))).astype(BF16)

        @pl.when(i == 0)
        def _():
            loss_ref[...] = jnp.zeros(loss_ref.shape, F32)
            dgpf_ref[...] = jnp.zeros(dgpf_ref.shape, F32)

        loss_ref[...] += (0.5 / d) * jnp.sum(err * err, axis=(0, 1), keepdims=True)
        dgpf_ref[...] += jnp.sum(dh2 * nf, axis=0, keepdims=True)

    row = lambda i: (i, 0)
    return pl.pallas_call(
        body, name="ffn_down", grid=(tp // tm,),
        in_specs=[pl.BlockSpec((n_sh, tm, fs), lambda i: (0, i, 0)), _full(wd4.shape), pl.BlockSpec((tm, d), row),
                  pl.BlockSpec((tm, d), row), _full(gpf.shape)],
        out_specs=[pl.BlockSpec((tm, d), row), pl.BlockSpec((tm, d), row), _full((8, LANES)), _full((1, d))],
        out_shape=[_sds((tp, d), BF16), _sds((tp, d), F32), _sds((8, LANES), F32), _sds((1, d), F32)],
        compiler_params=_params(("arbitrary",)),
    )(f4, wd4, h1, tgt, gpf)


def _ffn_bwd_act(dff, wd4, a4, u4):
    tp, d = dff.shape
    n_sh, fs, _ = wd4.shape
    tm = _row_tile(tp)

    def body(dff_ref, w_ref, a_ref, u_ref, da_ref, du_ref):
        df = _dot(dff_ref[...], w_ref[0], NT)
        a = a_ref[0].astype(F32)
        u = u_ref[0].astype(F32)
        s = _sig(a)
        da_ref[0] = (df * u * (s * (1.0 + a * (1.0 - s)))).astype(BF16)
        du_ref[0] = (df * a * s).astype(BF16)

    aspec = pl.BlockSpec((1, tm, fs), lambda k, i: (k, i, 0))
    return pl.pallas_call(
        body, name="ffn_bwd_act", grid=(n_sh, tp // tm),
        in_specs=[pl.BlockSpec((tm, d), lambda k, i: (i, 0)), pl.BlockSpec((1, fs, d), lambda k, i: (k, 0, 0)),
                  aspec, aspec],
        out_specs=[aspec, aspec],
        out_shape=[_sds((n_sh, tp, fs), BF16)] * 2,
        compiler_params=_params(("arbitrary", "arbitrary")),
    )(dff, wd4, a4, u4)


def _grad_w_down(f4, dff):
    n_sh, tp, fs = f4.shape
    d = dff.shape[1]

    def body(f_ref, dff_ref, g_ref):
        g_ref[0] = _dot(f_ref[0], dff_ref[...], TN).astype(BF16)

    return pl.pallas_call(
        body, name="grad_w_down", grid=(n_sh,),
        in_specs=[pl.BlockSpec((1, tp, fs), lambda k: (k, 0, 0)), _full(dff.shape)],
        out_specs=pl.BlockSpec((1, fs, d), lambda k: (k, 0, 0)),
        out_shape=_sds((n_sh, fs, d), BF16),
        compiler_params=_params(("arbitrary",)),
    )(f4, dff)


def _grad_w_gate_up(xn2, da4, du4):
    n_sh, tp, fs = da4.shape
    d = xn2.shape[1]

    def body(xn_ref, da_ref, du_ref, gg_ref, gu_ref):
        xn = xn_ref[...]
        gg_ref[0] = _dot(da_ref[0], xn, TN).astype(BF16)
        gu_ref[0] = _dot(du_ref[0], xn, TN).astype(BF16)

    aspec = pl.BlockSpec((1, tp, fs), lambda k: (k, 0, 0))
    gspec = pl.BlockSpec((1, fs, d), lambda k: (k, 0, 0))
    return pl.pallas_call(
        body, name="grad_w_gate_up", grid=(n_sh,),
        in_specs=[_full(xn2.shape), aspec, aspec],
        out_specs=[gspec, gspec],
        out_shape=[_sds((n_sh, fs, d), BF16)] * 2,
        compiler_params=_params(("arbitrary",)),
    )(xn2, da4, du4)


def _rms_bwd(dy, x, r, g):
    n = x * r
    dn = dy * g
    return r * (dn - n * _mean(dn * n)), dy * n


def _ffn_bwd_in(da4, du4, wg4, wu4, h1, mix, dh2, g2, gpm):
    n_sh, tp, fs = da4.shape
    d = h1.shape[1]
    tm = _row_tile(tp)

    def body(da_ref, du_ref, wg_ref, wu_ref, h1_ref, mix_ref, dh2_ref, g2_ref, gpm_ref,
             dh1_ref, dmix_ref, dg2_ref, dgpm_ref, acc_ref):
        i, k = pl.program_id(0), pl.program_id(1)
        part = _dot(da_ref[0], wg_ref[0]) + _dot(du_ref[0], wu_ref[0])

        @pl.when(k == 0)
        def _():
            acc_ref[...] = part

        @pl.when(k > 0)
        def _():
            acc_ref[...] += part

        @pl.when((i == 0) & (k == 0))
        def _():
            dg2_ref[...] = jnp.zeros(dg2_ref.shape, F32)
            dgpm_ref[...] = jnp.zeros(dgpm_ref.shape, F32)

        @pl.when(k == n_sh - 1)
        def _():
            h1v = h1_ref[...]
            r2 = lax.rsqrt(_mean(h1v * h1v) + RMS_EPS)
            dres, dg2_rows = _rms_bwd(acc_ref[...], h1v, r2, g2_ref[...])
            dh1 = dh2_ref[...] + dres
            dh1_ref[...] = dh1
            mixv = mix_ref[...]
            rm = lax.rsqrt(_mean(mixv * mixv) + RMS_EPS)
            dmix, dgpm_rows = _rms_bwd(dh1, mixv, rm, gpm_ref[...])
            dmix_ref[...] = dmix.astype(BF16)
            dg2_ref[...] += jnp.sum(dg2_rows, axis=0, keepdims=True)
            dgpm_ref[...] += jnp.sum(dgpm_rows, axis=0, keepdims=True)

    aspec = pl.BlockSpec((1, tm, fs), lambda i, k: (k, i, 0))
    wspec = pl.BlockSpec((1, fs, d), lambda i, k: (k, 0, 0))
    row = pl.BlockSpec((tm, d), lambda i, k: (i, 0))
    return pl.pallas_call(
        body, name="ffn_bwd_in", grid=(tp // tm, n_sh),
        in_specs=[aspec, aspec, wspec, wspec, row, row, row, _full(g2.shape), _full(gpm.shape)],
        out_specs=[row, row, _full((1, d)), _full((1, d))],
        out_shape=[_sds((tp, d), F32), _sds((tp, d), BF16), _sds((1, d), F32), _sds((1, d), F32)],
        scratch_shapes=[pltpu.VMEM((tm, d), F32)],
        compiler_params=_params(("arbitrary", "arbitrary")),
    )(da4, du4, wg4, wu4, h1, mix, dh2, g2, gpm)


def _grad_w_out(ya, yb, dmix):
    tp, wa_ = ya.shape
    d = dmix.shape[1]

    def body(ya_ref, yb_ref, dmix_ref, g_ref):
        dm = dmix_ref[...]
        g_ref[0:wa_, :] = _dot(ya_ref[...], dm, TN).astype(BF16)
        g_ref[wa_:2 * wa_, :] = _dot(yb_ref[...], dm, TN).astype(BF16)

    return pl.pallas_call(
        body, name="grad_w_out", grid=(1,),
        in_specs=[_full(ya.shape), _full(yb.shape), _full(dmix.shape)],
        out_specs=_full((2 * wa_, d)),
        out_shape=_sds((2 * wa_, d), BF16),
        compiler_params=_params(("arbitrary",)),
    )(ya, yb, dmix)


def _mix_bwd_out(dmix, wout, z, lg, lb):
    tp, d = dmix.shape
    wa_ = z.shape[1]
    tm = _row_tile(tp)

    def body(dmix_ref, w_ref, z_ref, lg_ref, lb_ref, dya_ref, dz_ref, dlg_ref, dlb_ref):
        i = pl.program_id(0)
        dm = dmix_ref[...]
        dya_ref[...] = _dot(dm, w_ref[0:wa_, :], NT)
        dyb = _dot(dm, w_ref[wa_:d, :], NT)
        lg_ = lg_ref[...]
        rl, zh, l = _layer_norm_parts(z_ref[...], lg_, lb_ref[...])
        sl = _sig(l)
        dl = dyb * (sl * (1.0 + l * (1.0 - sl)))
        dzh = dl * lg_
        dz_ref[...] = rl * (dzh - _mean(dzh) - zh * _mean(dzh * zh))

        @pl.when(i == 0)
        def _():
            dlg_ref[...] = jnp.zeros(dlg_ref.shape, F32)
            dlb_ref[...] = jnp.zeros(dlb_ref.shape, F32)

        dlg_ref[...] += jnp.sum(dl * zh, axis=0, keepdims=True)
        dlb_ref[...] += jnp.sum(dl, axis=0, keepdims=True)

    row = lambda i: (i, 0)
    return pl.pallas_call(
        body, name="mix_bwd_out", grid=(tp // tm,),
        in_specs=[pl.BlockSpec((tm, d), row), _full(wout.shape), pl.BlockSpec((tm, wa_), row), _full(lg.shape),
                  _full(lb.shape)],
        out_specs=[pl.BlockSpec((tm, wa_), row), pl.BlockSpec((tm, wa_), row), _full((1, wa_)), _full((1, wa_))],
        out_shape=[_sds((tp, wa_), F32), _sds((tp, wa_), F32), _sds((1, wa_), F32), _sds((1, wa_), F32)],
        compiler_params=_params(("arbitrary",)),
    )(dmix, wout, z, lg, lb)


def _mix_conv_bwd(hp5, dya, dz, wa, wb):
    _, tp, wgrp = hp5.shape
    seq, nseq = _seq_rows(tp)
    sb = nseq + 2 * CONV_HIST
    ka, kb = wa.shape[0], wb.shape[0]
    xs, ms = slice(0, seq), slice(seq, seq + N_META)
    ox, om = slice(CONV_HIST + N_META, CONV_HIST + nseq), slice(CONV_HIST, CONV_HIST + N_META)
    n_tail = tp - seq - N_META

    def body(hp_ref, dya_ref, dz_ref, wa_ref, wb_ref, dhp_ref, dwa_ref, dwb_ref, dbb_ref, s_ref, d_ref, o_ref, acc_ref):
        _zero_ends(s_ref, nseq)
        _zero_ends(d_ref, nseq)

        def put(p, ox_val, om_val):
            dhp_ref[p, xs, :] = ox_val.astype(BF16)
            dhp_ref[p, ms, :] = om_val.astype(BF16)
            dhp_ref[p, seq + N_META:tp, :] = jnp.zeros((n_tail, LANES), BF16)

        def wgrad(dw_ref, width):
            for k in range(width):
                dw_ref[k:k + 1, :] = jnp.sum(acc_ref[8 * k:8 * k + 8, :], axis=0, keepdims=True)

        _to_seq(s_ref, hp_ref[1, xs, :] * hp_ref[2, xs, :], hp_ref[1, ms, :] * hp_ref[2, ms, :], seq)
        _conv_taps(s_ref, wa_ref, o_ref, ka, nseq, False)
        put(0, dya_ref[xs, :] * o_ref[ox, :], dya_ref[ms, :] * o_ref[om, :])
        _to_seq(d_ref, dya_ref[xs, :] * hp_ref[0, xs, :], dya_ref[ms, :] * hp_ref[0, ms, :], seq)
        _conv_wgrad(s_ref, d_ref, acc_ref, ka, nseq)
        wgrad(dwa_ref, ka)
        _conv_taps(d_ref, wa_ref, o_ref, ka, nseq, True)
        put(1, o_ref[ox, :] * hp_ref[2, xs, :], o_ref[om, :] * hp_ref[2, ms, :])
        put(2, o_ref[ox, :] * hp_ref[1, xs, :], o_ref[om, :] * hp_ref[1, ms, :])

        _to_seq(s_ref, hp_ref[3, xs, :] * _sig(hp_ref[4, xs, :]), hp_ref[3, ms, :] * _sig(hp_ref[4, ms, :]), seq)
        _to_seq(d_ref, dz_ref[xs, :], dz_ref[ms, :], seq)
        dbb_ref[...] = (jnp.sum(dz_ref[xs, :], axis=0, keepdims=True)
                        + jnp.sum(dz_ref[ms, :], axis=0, keepdims=True))
        _conv_wgrad(s_ref, d_ref, acc_ref, kb, nseq)
        wgrad(dwb_ref, kb)
        _conv_taps(d_ref, wb_ref, o_ref, kb, nseq, True)
        sx, sm = _sig(hp_ref[4, xs, :]), _sig(hp_ref[4, ms, :])
        put(3, o_ref[ox, :] * sx, o_ref[om, :] * sm)
        put(4, o_ref[ox, :] * hp_ref[3, xs, :] * sx * (1.0 - sx), o_ref[om, :] * hp_ref[3, ms, :] * sm * (1.0 - sm))

    col = lambda j: (0, j)
    blk5 = pl.BlockSpec((5, tp, LANES), lambda j: (0, 0, j))
    return pl.pallas_call(
        body, name="mix_conv_bwd", grid=(wgrp // LANES,),
        in_specs=[blk5, pl.BlockSpec((tp, LANES), col), pl.BlockSpec((tp, LANES), col),
                  pl.BlockSpec((ka, LANES), col), pl.BlockSpec((kb, LANES), col)],
        out_specs=[blk5, pl.BlockSpec((ka, LANES), col), pl.BlockSpec((kb, LANES), col), pl.BlockSpec((1, LANES), col)],
        out_shape=[_sds((5, tp, wgrp), BF16), _sds((ka, wgrp), F32), _sds((kb, wgrp), F32), _sds((1, wgrp), F32)],
        scratch_shapes=[pltpu.VMEM((sb, LANES), F32), pltpu.VMEM((sb, LANES), F32), pltpu.VMEM((sb, LANES), F32),
                        pltpu.VMEM((8 * kb, LANES), F32)],
        compiler_params=_params(("arbitrary",)),
    )(hp5, dya, dz, wa, wb)


def _grad_w_in(xn1, dhp5):
    n_p, tp, pw = dhp5.shape
    d = xn1.shape[1]

    def body(xn_ref, dhp_ref, g_ref):
        g_ref[...] = _dot(xn_ref[...], dhp_ref[0], TN).astype(BF16)

    return pl.pallas_call(
        body, name="grad_w_in", grid=(n_p,),
        in_specs=[_full(xn1.shape), pl.BlockSpec((1, tp, pw), lambda p: (p, 0, 0))],
        out_specs=pl.BlockSpec((d, pw), lambda p: (0, p)),
        out_shape=_sds((d, n_p * pw), BF16),
        compiler_params=_params(("arbitrary",)),
    )(xn1, dhp5)


def _mix_bwd_in(dhp5, win4, h, dh1, g1):
    n_p, tp, pw = dhp5.shape
    d = h.shape[1]
    n_sh, _, csh = win4.shape
    tm = _row_tile(tp)
    pieces = _pieces(n_sh, csh, pw)

    def body(dhp_ref, w_ref, h_ref, dh1_ref, g_ref, dh_ref, dg1_ref):
        i = pl.program_id(0)
        dxn = None
        for k, klo, p, plo, w in pieces:
            t = _dot(dhp_ref[p, :, plo:plo + w], w_ref[k, :, klo:klo + w], NT)
            dxn = t if dxn is None else dxn + t
        hh = h_ref[...]
        r1 = lax.rsqrt(_mean(hh * hh) + RMS_EPS)
        dres, dg_rows = _rms_bwd(dxn, hh, r1, g_ref[...])
        dh_ref[...] = dh1_ref[...] + dres

        @pl.when(i == 0)
        def _():
            dg1_ref[...] = jnp.zeros(dg1_ref.shape, F32)

        dg1_ref[...] += jnp.sum(dg_rows, axis=0, keepdims=True)

    row = lambda i: (i, 0)
    return pl.pallas_call(
        body, name="mix_bwd_in", grid=(tp // tm,),
        in_specs=[pl.BlockSpec((n_p, tm, pw), lambda i: (0, i, 0)), _full(win4.shape), pl.BlockSpec((tm, d), row),
                  pl.BlockSpec((tm, d), row), _full(g1.shape)],
        out_specs=[pl.BlockSpec((tm, d), row), _full((1, d))],
        out_shape=[_sds((tp, d), F32), _sds((1, d), F32)],
        compiler_params=_params(("arbitrary",)),
    )(dhp5, win4, h, dh1, g1)


def _other_chips(x, y):
    out = []
    for j in (1, 2, 3):
        px, py = _flip(x, j >> 1), _flip(y, j & 1)
        out.append((px, py, 2 * px + py))
    return out


def _half_rows(c, rows_half):
    return pl.ds(pl.multiple_of(c * rows_half, 8), rows_half)


def _gather_shards(locs):
    n = len(locs)
    halves = [a.shape[0] // 2 for a in locs]

    def body(*refs):
        loc, full = refs[:n], refs[n:2 * n]
        ssem, rsem, lsem = refs[2 * n:]
        x, y, c = _mesh_pos()
        q = 2 * x + y
        chips = _other_chips(x, y)

        def remote(i, chip_no, half, to, s):
            part = full[i].at[chip_no, _half_rows(half, halves[i]), :]
            return pltpu.make_async_remote_copy(src_ref=part, dst_ref=part, send_sem=ssem.at[s], recv_sem=rsem.at[s],
                                                device_id=to, device_id_type=MESH)

        own = [pltpu.make_async_copy(loc[i], full[i].at[q], lsem.at[i]) for i in range(n)]
        for cp in own:
            cp.start()
        first = []
        for i in range(n):
            for j, (px, py, _) in enumerate(chips):
                src = loc[i].at[_half_rows(c, halves[i]), :]
                dst = full[i].at[q, _half_rows(c, halves[i]), :]
                cp = pltpu.make_async_remote_copy(src_ref=src, dst_ref=dst, send_sem=ssem.at[3 * i + j],
                                                  recv_sem=rsem.at[3 * i + j], device_id=(px, py, c), device_id_type=MESH)
                cp.start()
                first.append(cp)
        passed = []
        for i in range(n):
            for j, (_, _, qj) in enumerate(chips):
                remote(i, qj, c, (x, y, c), 3 * i + j).wait_recv()
                cp = remote(i, qj, c, (x, y, 1 - c), 3 * n + 3 * i + j)
                cp.start()
                passed.append(cp)
        for i in range(n):
            for j, (_, _, qj) in enumerate(chips):
                remote(i, qj, 1 - c, (x, y, c), 3 * n + 3 * i + j).wait_recv()
        for cp in first + passed:
            cp.wait_send()
        for cp in own:
            cp.wait()

    return pl.pallas_call(
        body, name="gather_shards",
        in_specs=[ANY] * n, out_specs=[ANY] * n,
        out_shape=[_sds((N_CHIPS,) + a.shape, a.dtype) for a in locs],
        scratch_shapes=[pltpu.SemaphoreType.DMA((6 * n,)), pltpu.SemaphoreType.DMA((6 * n,)),
                        pltpu.SemaphoreType.DMA((n,))],
    )(*locs)


HBM = pl.BlockSpec(memory_space=pltpu.HBM)
SEM = pl.BlockSpec(memory_space=pltpu.SEMAPHORE)
EFFECT = pltpu.SideEffectType.DATAFLOW_SIDE_EFFECTING


def _in_hbm(a):
    return pltpu.with_memory_space_constraint(a, pltpu.HBM)


def _gather_start(locs, after):
    n = len(locs)
    halves = [a.shape[0] // 2 for a in locs]

    def body(*refs):
        loc, land = refs[:n], refs[n:2 * n]
        ssem, rsem = refs[2 * n + 1], refs[2 * n + 2]
        token = refs[-1]
        x, y, c = _mesh_pos()
        q = 2 * x + y
        for i in range(n):
            for j, (px, py, _) in enumerate(_other_chips(x, y)):
                rows = _half_rows(c, halves[i])
                pltpu.make_async_remote_copy(src_ref=loc[i].at[rows, :], dst_ref=land[i].at[q, rows, :],
                                             send_sem=ssem.at[3 * i + j], recv_sem=rsem.at[3 * i + j],
                                             device_id=(px, py, c), device_id_type=MESH).start()
        token[...] = jnp.zeros(token.shape, F32)

    lands = [lax.empty((N_CHIPS,) + a.shape, a.dtype) for a in locs]
    outs = pl.pallas_call(
        body, name="gather_start",
        in_specs=[HBM] * (2 * n) + [ANY], out_specs=[SEM, SEM] + [HBM] * (2 * n) + [VMEM],
        out_shape=[pltpu.SemaphoreType.DMA((3 * n,)), pltpu.SemaphoreType.DMA((3 * n,))]
        + [pltpu.HBM(a.shape, a.dtype) for a in locs] + [pltpu.HBM(a.shape, a.dtype) for a in lands]
        + [_sds((8, LANES), F32)],
        input_output_aliases={i: 2 + i for i in range(2 * n)},
        compiler_params=pltpu.CompilerParams(has_side_effects=EFFECT),
    )(*[_in_hbm(a) for a in locs], *[_in_hbm(a) for a in lands], after)
    return outs[0], outs[1], list(outs[2:2 + n]), list(outs[2 + n:2 + 2 * n]), outs[-1]


def _gather_wait(which, ssem, rsem, locs, lands, after, tag):
    m = len(which)
    halves = [a.shape[0] // 2 for a in locs]

    def body(*refs):
        loc, land = refs[:m], refs[m:2 * m]
        ssem_, rsem_ = refs[2 * m], refs[2 * m + 1]
        x, y, c = _mesh_pos()
        for t, i in enumerate(which):
            for j, (px, py, qj) in enumerate(_other_chips(x, y)):
                rows = _half_rows(c, halves[t])
                cp = pltpu.make_async_remote_copy(src_ref=loc[t].at[rows, :], dst_ref=land[t].at[qj, rows, :],
                                                  send_sem=ssem_.at[3 * i + j], recv_sem=rsem_.at[3 * i + j],
                                                  device_id=(px, py, c), device_id_type=MESH)
                cp.wait_send()
                cp.wait_recv()

    outs = pl.pallas_call(
        body, name="gather_wait_" + tag,
        in_specs=[HBM] * (2 * m) + [SEM, SEM, ANY], out_specs=[HBM] * (2 * m),
        out_shape=[pltpu.HBM(a.shape, a.dtype) for a in locs] + [pltpu.HBM(a.shape, a.dtype) for a in lands],
        input_output_aliases={i: i for i in range(2 * m)},
        compiler_params=pltpu.CompilerParams(has_side_effects=EFFECT),
    )(*locs, *lands, ssem, rsem, after)
    return list(outs[:m]), list(outs[m:])


def _forward_pair(locs, lands, tag):
    n = len(locs)
    halves = [a.shape[0] // 2 for a in locs]

    def body(*refs):
        loc, full = refs[:n], refs[2 * n:3 * n]
        ssem, rsem, lsem = refs[3 * n:]
        x, y, c = _mesh_pos()
        q = 2 * x + y
        own = [pltpu.make_async_copy(loc[i], full[i].at[q], lsem.at[i]) for i in range(n)]
        for cp in own:
            cp.start()
        cps = []
        for i in range(n):
            for j, (_, _, qj) in enumerate(_other_chips(x, y)):
                part = full[i].at[qj, _half_rows(c, halves[i]), :]
                cp = pltpu.make_async_remote_copy(src_ref=part, dst_ref=part, send_sem=ssem.at[3 * i + j],
                                                  recv_sem=rsem.at[3 * i + j], device_id=(x, y, 1 - c), device_id_type=MESH)
                cp.start()
                cps.append(cp)
        for cp in cps:
            cp.wait()
        for cp in own:
            cp.wait()

    return pl.pallas_call(
        body, name="forward_pair_" + tag,
        in_specs=[ANY] * (2 * n), out_specs=[ANY] * n,
        out_shape=[_sds(a.shape, a.dtype) for a in lands],
        input_output_aliases={n + i: i for i in range(n)},
        scratch_shapes=[pltpu.SemaphoreType.DMA((3 * n,)), pltpu.SemaphoreType.DMA((3 * n,)),
                        pltpu.SemaphoreType.DMA((n,))],
    )(*locs, *lands)


def _chip_exchange_start(parts, after, tag):
    n = len(parts)

    def body(*refs):
        src, land = refs[:n], refs[n:2 * n]
        ssem, rsem = refs[2 * n + 1], refs[2 * n + 2]
        token = refs[-1]
        x, y, c = _mesh_pos()
        for i in range(n):
            for j, (px, py, qj) in enumerate(_other_chips(x, y)):
                pltpu.make_async_remote_copy(src_ref=src[i].at[qj], dst_ref=land[i].at[j], send_sem=ssem.at[3 * i + j],
                                             recv_sem=rsem.at[3 * i + j], device_id=(px, py, c), device_id_type=MESH).start()
        token[...] = jnp.zeros(token.shape, F32)

    lands = [lax.empty((3,) + a.shape[1:], a.dtype) for a in parts]
    outs = pl.pallas_call(
        body, name="chip_exchange_start_" + tag,
        in_specs=[HBM] * (2 * n) + [ANY], out_specs=[SEM, SEM] + [HBM] * (2 * n) + [VMEM],
        out_shape=[pltpu.SemaphoreType.DMA((3 * n,)), pltpu.SemaphoreType.DMA((3 * n,))]
        + [pltpu.HBM(a.shape, a.dtype) for a in parts] + [pltpu.HBM(a.shape, a.dtype) for a in lands]
        + [_sds((8, LANES), F32)],
        input_output_aliases={i: 2 + i for i in range(2 * n)},
        compiler_params=pltpu.CompilerParams(has_side_effects=EFFECT),
    )(*[_in_hbm(a) for a in parts], *[_in_hbm(a) for a in lands], after)
    return outs[0], outs[1], list(outs[2:2 + n]), list(outs[2 + n:2 + 2 * n]), outs[-1]


def _chip_exchange_wait(ssem, rsem, parts, lands, after, tag):
    n = len(parts)

    def body(*refs):
        src, land = refs[:n], refs[n:2 * n]
        ssem_, rsem_ = refs[2 * n], refs[2 * n + 1]
        x, y, c = _mesh_pos()
        for i in range(n):
            for j, (px, py, qj) in enumerate(_other_chips(x, y)):
                cp = pltpu.make_async_remote_copy(src_ref=src[i].at[qj], dst_ref=land[i].at[j], send_sem=ssem_.at[3 * i + j],
                                                  recv_sem=rsem_.at[3 * i + j], device_id=(px, py, c), device_id_type=MESH)
                cp.wait_send()
                cp.wait_recv()

    outs = pl.pallas_call(
        body, name="chip_exchange_wait_" + tag,
        in_specs=[HBM] * (2 * n) + [SEM, SEM, ANY], out_specs=[HBM] * (2 * n),
        out_shape=[pltpu.HBM(a.shape, a.dtype) for a in parts] + [pltpu.HBM(a.shape, a.dtype) for a in lands],
        input_output_aliases={i: i for i in range(2 * n)},
        compiler_params=pltpu.CompilerParams(has_side_effects=EFFECT),
    )(*parts, *lands, ssem, rsem, after)
    return list(outs[:n]), list(outs[n:])


def _pair_exchange_grads(grads, half_axis, tag):
    n = len(grads)

    def half_of(ref, i, which):
        rows = grads[i].shape[half_axis[i]] // 2
        if half_axis[i] == 0:
            return ref.at[_half_rows(which, rows), :]
        return ref.at[:, _half_rows(which, rows), :]

    def out_shape(i):
        s = list(grads[i].shape)
        s[half_axis[i]] //= 2
        return _sds(tuple(s), grads[i].dtype)

    def body(*refs):
        g, got = refs[:n], refs[n:2 * n]
        ssem, rsem = refs[2 * n:]
        x, y, c = _mesh_pos()
        cps = []
        for i in range(n):
            cp = pltpu.make_async_remote_copy(src_ref=half_of(g[i], i, 1 - c), dst_ref=got[i], send_sem=ssem.at[i],
                                              recv_sem=rsem.at[i], device_id=(x, y, 1 - c), device_id_type=MESH)
            cp.start()
            cps.append(cp)
        for cp in cps:
            cp.wait()

    return pl.pallas_call(
        body, name="pair_exchange_grads_" + tag,
        in_specs=[ANY] * n, out_specs=[ANY] * n,
        out_shape=[out_shape(i) for i in range(n)],
        scratch_shapes=[pltpu.SemaphoreType.DMA((n,)), pltpu.SemaphoreType.DMA((n,))],
    )(*grads)


def _pair_sum(g, got, c_arr, col_sharded, tag):
    if col_sharded:
        rows, cols = g.shape
        rh, cs = rows // 2, cols // N_CHIPS
        g_spec = pl.BlockSpec((rh, cs), lambda k, c_ref: (c_ref[0], k))
        got_spec = pl.BlockSpec((rh, cs), lambda k, c_ref: (0, k))
    else:
        _, rows, cs = g.shape
        rh = rows // 2
        g_spec = pl.BlockSpec((1, rh, cs), lambda k, c_ref: (k, c_ref[0], 0))
        got_spec = pl.BlockSpec((1, rh, cs), lambda k, c_ref: (k, 0, 0))

    def body(c_ref, g_ref, got_ref, out_ref):
        total = g_ref[...].astype(F32) + got_ref[...].astype(F32)
        out_ref[...] = total.astype(BF16).reshape(out_ref.shape)

    return pl.pallas_call(
        body, name="pair_sum_" + tag,
        grid_spec=pltpu.PrefetchScalarGridSpec(
            num_scalar_prefetch=1, grid=(N_CHIPS,), in_specs=[g_spec, got_spec],
            out_specs=pl.BlockSpec((1, rh, cs), lambda k, c_ref: (k, 0, 0))),
        out_shape=_sds((N_CHIPS, rh, cs), BF16),
        compiler_params=_params(("arbitrary",)),
    )(c_arr, g, got)


def _chip_sum(part, got, qc_arr, tag):
    _, rh, cs = part.shape

    def body(qc_ref, part_ref, got_ref, out_ref):
        total = part_ref[0].astype(F32)
        for j in range(3):
            total = total + got_ref[j].astype(F32)
        out_ref[...] = total

    return pl.pallas_call(
        body, name="chip_sum_" + tag,
        grid_spec=pltpu.PrefetchScalarGridSpec(
            num_scalar_prefetch=1, grid=(1,),
            in_specs=[pl.BlockSpec((1, rh, cs), lambda i, qc: (qc[0], 0, 0)), pl.BlockSpec((3, rh, cs), lambda i, qc: (0, 0, 0))],
            out_specs=pl.BlockSpec((rh, cs), lambda i, qc: (qc[1], 0))),
        out_shape=_sds((2 * rh, cs), F32),
        compiler_params=_params(("arbitrary",)),
    )(qc_arr, part, got)


def _pair_share_grads(grads, tag):
    n = len(grads)

    def body(*refs):
        g = refs[n:2 * n]
        ssem, rsem = refs[2 * n:]
        x, y, c = _mesh_pos()
        cps = []
        for i in range(n):
            mine = g[i].at[_half_rows(c, grads[i].shape[0] // 2), :]
            cp = pltpu.make_async_remote_copy(src_ref=mine, dst_ref=mine, send_sem=ssem.at[i], recv_sem=rsem.at[i],
                                              device_id=(x, y, 1 - c), device_id_type=MESH)
            cp.start()
            cps.append(cp)
        for cp in cps:
            cp.wait()

    return pl.pallas_call(
        body, name="pair_share_grads_" + tag,
        in_specs=[ANY] * n, out_specs=[ANY] * n,
        out_shape=[_sds(a.shape, a.dtype) for a in grads],
        input_output_aliases={i: i for i in range(n)},
        scratch_shapes=[pltpu.SemaphoreType.DMA((n,)), pltpu.SemaphoreType.DMA((n,))],
    )(*grads)


def _small_allreduce(parts, rows_total):
    n = len(parts)
    width = max(p.shape[1] for p in parts)

    def body(*refs):
        ins, out_ref = refs[:n], refs[n]
        pack, got, ssem, rsem = refs[n + 1:]
        x, y, c = _mesh_pos()
        me = 4 * x + 2 * y + c
        pack[...] = jnp.zeros(pack.shape, F32)
        r0 = 0
        for i in range(n):
            r, w = parts[i].shape
            pack[r0:r0 + r, 0:w] = ins[i][...]
            r0 += r
        cps = []
        for rel in range(1, 8):
            to = (_flip(x, (rel >> 2) & 1), _flip(y, (rel >> 1) & 1), _flip(c, rel & 1))
            cp = pltpu.make_async_remote_copy(src_ref=pack, dst_ref=got.at[rel - 1], send_sem=ssem.at[rel - 1],
                                              recv_sem=rsem.at[rel - 1], device_id=to, device_id_type=MESH)
            cp.start()
            cps.append(cp)
        for cp in cps:
            cp.wait()
        total = jnp.zeros(pack.shape, F32)
        for dev in range(8):
            rel = jnp.bitwise_xor(me, dev)
            theirs = got[jnp.maximum(rel - 1, 0)]
            total = total + jnp.where(rel == 0, pack[...], theirs)
        out_ref[...] = total

    return pl.pallas_call(
        body, name="small_allreduce",
        in_specs=[VMEM] * n, out_specs=VMEM,
        out_shape=_sds((rows_total, width), F32),
        scratch_shapes=[pltpu.VMEM((rows_total, width), F32), pltpu.VMEM((7, rows_total, width), F32),
                        pltpu.SemaphoreType.DMA((7,)), pltpu.SemaphoreType.DMA((7,))],
        compiler_params=_params(),
    )(*parts)


def _adamw_math(w, g, m, v):
    m2 = ADAM_B1 * m + (1.0 - ADAM_B1) * g
    v2 = ADAM_B2 * v + (1.0 - ADAM_B2) * (g * g)
    m_hat = m2 / (1.0 - ADAM_B1 ** ADAM_STEP)
    v_hat = v2 / (1.0 - ADAM_B2 ** ADAM_STEP)
    delta = -ADAM_LR * (m_hat / (jnp.sqrt(v_hat) + ADAM_EPS) + ADAM_WD * w)
    return delta, m2, v2


def _adamw_big(w, g, m, v, tag):
    rows, cols = w.shape
    tr = _row_tile(rows)

    def body(w_ref, g_ref, m_ref, v_ref, go_ref, d_ref, m2_ref, v2_ref):
        gg = g_ref[...]
        go_ref[...] = gg
        d_ref[...], m2_ref[...], v2_ref[...] = _adamw_math(w_ref[...], gg, m_ref[...], v_ref[...])

    spec = pl.BlockSpec((tr, cols), lambda i: (i, 0))
    return pl.pallas_call(
        body, name="adamw_" + tag, grid=(rows // tr,),
        in_specs=[spec] * 4, out_specs=[spec] * 4,
        out_shape=[_sds((rows, cols), F32)] * 4,
        compiler_params=_params(("arbitrary",)),
    )(w, g, m, v)


def _adamw_small(ws, gs, ms, vs):
    n = len(ws)

    def body(*refs):
        w_r, g_r, m_r, v_r = refs[:n], refs[n:2 * n], refs[2 * n:3 * n], refs[3 * n:4 * n]
        d_o, m_o, v_o = refs[4 * n:5 * n], refs[5 * n:6 * n], refs[6 * n:7 * n]
        for i in range(n):
            d_o[i][...], m_o[i][...], v_o[i][...] = _adamw_math(w_r[i][...], g_r[i][...], m_r[i][...], v_r[i][...])

    shapes = [_sds(w.shape, F32) for w in ws]
    outs = pl.pallas_call(
        body, name="adamw_small",
        in_specs=[VMEM] * (4 * n), out_specs=[VMEM] * (3 * n),
        out_shape=shapes * 3,
        compiler_params=_params(),
    )(*ws, *gs, *ms, *vs)
    return outs[:n], outs[n:2 * n], outs[2 * n:]


SMALL_ROWS = 64
PACK_ROWS = 64


def kernel(x, meta_tokens, pre_mix_norm, w_in, conv_a_w, conv_b_w, conv_b_bias, ln_b_gain, ln_b_bias, w_out, post_mix_norm, pre_ffn_norm, w_gate, w_up, w_down, post_ffn_norm, loss_target, m_meta_tokens, m_pre_mix_norm, m_w_in, m_conv_a_w, m_conv_b_w, m_conv_b_bias, m_ln_b_gain, m_ln_b_bias, m_w_out, m_post_mix_norm, m_pre_ffn_norm, m_w_gate, m_w_up, m_w_down, m_post_ffn_norm, v_meta_tokens, v_pre_mix_norm, v_w_in, v_conv_a_w, v_conv_b_w, v_conv_b_bias, v_ln_b_gain, v_ln_b_bias, v_w_out, v_post_mix_norm, v_pre_ffn_norm, v_w_gate, v_w_up, v_w_down, v_post_ffn_norm):
    xq, yq, cq = lax.axis_index("x"), lax.axis_index("y"), lax.axis_index("c")
    chip = 2 * xq + yq
    c_arr = jnp.reshape(cq, (1,)).astype(jnp.int32)
    qc_arr = jnp.stack([chip, cq]).astype(jnp.int32)

    seq, d = x.shape[1], x.shape[2]
    x2, tgt2 = x[0], loss_target[0]
    tr = lambda a: jnp.swapaxes(a, 1, 2)[0]
    w_in2, w_out2, w_gate2, w_up2, w_down2 = w_in[0], w_out[0], tr(w_gate), tr(w_up), w_down[0]
    ka, wa_sh = conv_a_w.shape[1], conv_a_w.shape[2]
    kb = conv_b_w.shape[1]
    meta_sh = meta_tokens.shape[1]

    small = jnp.zeros((PACK_ROWS, meta_sh), F32)
    small = small.at[0:N_META, :].set(meta_tokens)
    small = small.at[16:16 + ka, 0:wa_sh].set(conv_a_w[0])
    small = small.at[24:24 + kb, 0:wa_sh].set(conv_b_w[0])
    win4, small4 = _gather_shards([w_in2.astype(BF16), small])
    rest = [w_out2.astype(BF16), w_gate2.astype(BF16), w_up2.astype(BF16), w_down2.astype(BF16)]
    g_ssem, g_rsem, rest, lands, g_token = _gather_start(rest, win4)
    meta_f = jnp.concatenate([small4[k, 0:N_META, :] for k in range(N_CHIPS)], axis=1)
    wa_f = jnp.concatenate([small4[k, 16:16 + ka, 0:wa_sh] for k in range(N_CHIPS)], axis=1)
    wb_f = jnp.concatenate([small4[k, 24:24 + kb, 0:wa_sh] for k in range(N_CHIPS)], axis=1)

    tail = jnp.zeros((TAIL_ROWS - N_META, d), F32)
    h = jnp.concatenate([x2, meta_f, tail], axis=0)
    tgt = jnp.concatenate([tgt2, jnp.zeros((TAIL_ROWS, d), F32)], axis=0)
    xn1, hp5 = _mm_in(h, win4, pre_mix_norm + g_token[0:1, 0:1])
    ya, z = _mix_conv_fwd(hp5, wa_f, wb_f, conv_b_bias)
    loc_o, land_o = _gather_wait([0], g_ssem, g_rsem, rest[0:1], lands[0:1], z, "out")
    (wout4,) = _forward_pair(loc_o, land_o, "out")
    wout_f = wout4.reshape(N_CHIPS * wout4.shape[1], wout4.shape[2])
    yb, mix, h1, xn2 = _mm_out(ya, z, h, wout_f, ln_b_gain, ln_b_bias, post_mix_norm, pre_ffn_norm)
    loc_gu, land_gu = _gather_wait([1, 2], g_ssem, g_rsem, rest[1:3], lands[1:3], xn2, "gate_up")
    wg4, wu4 = _forward_pair(loc_gu, land_gu, "gate_up")
    a4, u4, f4 = _ffn_up(xn2, wg4, wu4)
    loc_d, land_d = _gather_wait([3], g_ssem, g_rsem, rest[3:4], lands[3:4], f4, "down")
    (wd4,) = _forward_pair(loc_d, land_d, "down")
    dff, dh2, loss_blk, d_gpf = _ffn_down(f4, wd4, h1, tgt, post_ffn_norm)

    da4, du4 = _ffn_bwd_act(dff, wd4, a4, u4)
    g_down = _grad_w_down(f4, dff)
    g_gate, g_up = _grad_w_gate_up(xn2, da4, du4)
    ffn = [g_gate, g_up, g_down]
    got = _pair_exchange_grads(ffn, [1, 1, 1], "ffn")
    parts = [_pair_sum(ffn[i], got[i], c_arr, False, ("gate", "up", "down")[i]) for i in range(3)]
    f_ssem, f_rsem, parts, f_lands, f_token = _chip_exchange_start(parts, dff, "ffn")
    dh1, dmix, d_g2, d_gpm = _ffn_bwd_in(da4, du4, wg4, wu4, h1, mix, dh2, pre_ffn_norm + f_token[0:1, 0:1], post_mix_norm)
    g_out = _grad_w_out(ya, yb, dmix)
    dya, dz, d_lg, d_lb = _mix_bwd_out(dmix, wout_f, z, ln_b_gain, ln_b_bias)
    dhp5, d_wa, d_wb, d_bb = _mix_conv_bwd(hp5, dya, dz, wa_f, wb_f)
    g_in = _grad_w_in(xn1, dhp5)
    dh, d_g1 = _mix_bwd_in(dhp5, win4, h, dh1, pre_mix_norm)
    grad_x = dh[0:seq][None]
    d_meta = dh[seq:seq + N_META]

    g_out4 = g_out.reshape(N_CHIPS, g_out.shape[0] // N_CHIPS, g_out.shape[1])
    mixw = [g_in, g_out4]
    got2 = _pair_exchange_grads(mixw, [0, 1], "mix")
    parts2 = [_pair_sum(mixw[i], got2[i], c_arr, i == 0, ("in", "out")[i]) for i in range(2)]
    m_ssem, m_rsem, parts2, m_lands, m_token = _chip_exchange_start(parts2, dh, "mix")

    parts, f_recv = _chip_exchange_wait(f_ssem, f_rsem, parts, f_lands, m_token, "ffn")
    halves = [_chip_sum(parts[i], f_recv[i], qc_arr, ("gate", "up", "down")[i]) for i in range(3)]
    gsum_ffn = _pair_share_grads(halves, "ffn")

    small_parts = [d_g1, d_gpm, d_g2, d_gpf, d_bb, d_lg, d_lb, d_wa, d_wb, d_meta, loss_blk[0:1, :]]
    red = _small_allreduce(small_parts, SMALL_ROWS)
    r0 = 0
    small_sum = []
    for p in small_parts:
        small_sum.append(red[r0:r0 + p.shape[0], 0:p.shape[1]])
        r0 += p.shape[0]
    s_g1, s_gpm, s_g2, s_gpf, s_bb, s_lg, s_lb, s_wa, s_wb, s_meta, s_loss = small_sum
    g_meta = lax.dynamic_slice_in_dim(s_meta, chip * meta_sh, meta_sh, axis=1)
    g_wa = lax.dynamic_slice_in_dim(s_wa, chip * wa_sh, wa_sh, axis=1)[None]
    g_wb = lax.dynamic_slice_in_dim(s_wb, chip * wa_sh, wa_sh, axis=1)[None]
    loss = s_loss[0, 0]

    names_big = ["w_in", "w_out", "w_gate", "w_up", "w_down"]
    w_big = dict(zip(names_big, [w_in2, w_out2, w_gate2, w_up2, w_down2]))
    m_big = dict(zip(names_big, [m_w_in[0], m_w_out[0], tr(m_w_gate), tr(m_w_up), m_w_down[0]]))
    v_big = dict(zip(names_big, [v_w_in[0], v_w_out[0], tr(v_w_gate), tr(v_w_up), v_w_down[0]]))
    grads, deltas, new_m, new_v = {}, {}, {}, {}

    def update(nm, g):
        outs = _adamw_big(w_big[nm], g, m_big[nm], v_big[nm], nm)
        if nm in ("w_gate", "w_up"):
            outs = [jnp.swapaxes(o[None], 1, 2) for o in outs]
        else:
            outs = [o[None] for o in outs]
        grads[nm], deltas[nm], new_m[nm], new_v[nm] = outs
        return outs[1]

    last = None
    for nm, g in zip(["w_gate", "w_up", "w_down"], gsum_ffn):
        last = update(nm, g)

    parts2, m_recv = _chip_exchange_wait(m_ssem, m_rsem, parts2, m_lands, last, "mix")
    halves2 = [_chip_sum(parts2[i], m_recv[i], qc_arr, ("in", "out")[i]) for i in range(2)]
    gsum_mix = _pair_share_grads(halves2, "mix")
    for nm, g in zip(["w_in", "w_out"], gsum_mix):
        update(nm, g)

    names_small = ["meta_tokens", "pre_mix_norm", "conv_a_w", "conv_b_w", "conv_b_bias", "ln_b_gain", "ln_b_bias",
                   "post_mix_norm", "pre_ffn_norm", "post_ffn_norm"]
    w_small = [meta_tokens, pre_mix_norm, conv_a_w[0], conv_b_w[0], conv_b_bias, ln_b_gain, ln_b_bias, post_mix_norm,
               pre_ffn_norm, post_ffn_norm]
    g_small = [g_meta, s_g1, g_wa[0], g_wb[0], s_bb, s_lg, s_lb, s_gpm, s_g2, s_gpf]
    m_small = [m_meta_tokens, m_pre_mix_norm, m_conv_a_w[0], m_conv_b_w[0], m_conv_b_bias, m_ln_b_gain, m_ln_b_bias,
               m_post_mix_norm, m_pre_ffn_norm, m_post_ffn_norm]
    v_small = [v_meta_tokens, v_pre_mix_norm, v_conv_a_w[0], v_conv_b_w[0], v_conv_b_bias, v_ln_b_gain, v_ln_b_bias,
               v_post_mix_norm, v_pre_ffn_norm, v_post_ffn_norm]
    d_s, m_s, v_s = _adamw_small(w_small, g_small, m_small, v_small)
    for i, nm in enumerate(names_small):
        lead = nm in ("conv_a_w", "conv_b_w")
        fix = (lambda a: a[None]) if lead else (lambda a: a)
        grads[nm], deltas[nm], new_m[nm], new_v[nm] = fix(g_small[i]), fix(d_s[i]), fix(m_s[i]), fix(v_s[i])

    order = ["meta_tokens", "pre_mix_norm", "w_in", "conv_a_w", "conv_b_w", "conv_b_bias", "ln_b_gain", "ln_b_bias", "w_out",
             "post_mix_norm", "pre_ffn_norm", "w_gate", "w_up", "w_down", "post_ffn_norm"]
    return (loss, grad_x, *[grads[k] for k in order], *[deltas[k] for k in order], *[new_m[k] for k in order],
            *[new_v[k] for k in order])
```

```python
import functools

import jax
import jax.numpy as jnp
from jax import lax
from jax.experimental import pallas as pl
from jax.experimental.pallas import tpu as pltpu

F32 = jnp.float32
BF16 = jnp.bfloat16
MESH = pl.DeviceIdType.MESH

N_META = 16
TAIL_ROWS = 128
RMS_EPS = 1e-6
LN_EPS = 1e-5
ADAM_LR = 0.001
ADAM_B1 = 0.9
ADAM_B2 = 0.999
ADAM_EPS = 1e-08
ADAM_WD = 0.01
ADAM_STEP = 10

N_CHIPS = 4
LANES = 128
CONV_CHUNK = 48
CONV_HIST = 32
ROW_TILE_CAP = 640
VMEM_LIMIT = 56 * 1024 * 1024

NN = (((1,), (0,)), ((), ()))
NT = (((1,), (1,)), ((), ()))
TN = (((0,), (0,)), ((), ()))


def _dot(a, b, dims=NN):
    return lax.dot_general(a, b, dims, preferred_element_type=F32)


def _sig(v):
    return 1.0 / (1.0 + jnp.exp(-v))


def _mean(v):
    return jnp.mean(v, axis=-1, keepdims=True)


def _row_tile(rows):
    best = 16
    for t in range(16, min(rows, ROW_TILE_CAP) + 1, 16):
        if rows % t == 0:
            best = t
    assert rows % best == 0
    return best


def _pieces(n_shards, shard_w, piece_w):
    total = n_shards * shard_w
    cuts = sorted(set(range(0, total + 1, shard_w)) | set(range(0, total + 1, piece_w)))
    out = []
    for lo, hi in zip(cuts[:-1], cuts[1:]):
        out.append((lo // shard_w, lo % shard_w, lo // piece_w, lo % piece_w, hi - lo))
    return out


def _params(semantics=None):
    kw = dict(vmem_limit_bytes=VMEM_LIMIT)
    if semantics is not None:
        kw["dimension_semantics"] = semantics
    return pltpu.CompilerParams(**kw)


def _full(shape):
    nd = len(shape)
    return pl.BlockSpec(shape, lambda *_: (0,) * nd)


def _sds(shape, dtype):
    return jax.ShapeDtypeStruct(shape, dtype)


ANY = pl.BlockSpec(memory_space=pl.ANY)
VMEM = pl.BlockSpec(memory_space=pltpu.VMEM)


def _mesh_pos():
    return lax.axis_index("x"), lax.axis_index("y"), lax.axis_index("c")


def _flip(v, bit):
    return 1 - v if bit else v


def _mm_in(h, win4, g1):
    tp, d = h.shape
    tm = _row_tile(tp)
    n_sh, _, csh = win4.shape
    pw = n_sh * csh // 5
    pieces = _pieces(n_sh, csh, pw)

    def body(h_ref, w_ref, g_ref, xn_ref, hp_ref):
        hh = h_ref[...]
        r = lax.rsqrt(_mean(hh * hh) + RMS_EPS)
        xn = (hh * r * g_ref[...]).astype(BF16)
        xn_ref[...] = xn
        for k, klo, p, plo, w in pieces:
            hp_ref[p, :, plo:plo + w] = _dot(xn, w_ref[k, :, klo:klo + w])

    return pl.pallas_call(
        body, name="mm_in", grid=(tp // tm,),
        in_specs=[pl.BlockSpec((tm, d), lambda i: (i, 0)), _full(win4.shape), _full(g1.shape)],
        out_specs=[pl.BlockSpec((tm, d), lambda i: (i, 0)), pl.BlockSpec((5, tm, pw), lambda i: (0, i, 0))],
        out_shape=[_sds((tp, d), BF16), _sds((5, tp, pw), F32)],
        compiler_params=_params(("arbitrary",)),
    )(h, win4, g1)


def _seq_rows(tp):
    seq = tp - TAIL_ROWS
    nseq = seq + N_META
    assert nseq % CONV_CHUNK == 0 and seq % 16 == 0
    return seq, nseq


def _conv_taps(src_ref, w_ref, dst_ref, width, nseq, transpose):
    w = w_ref[...]

    def step(n, carry):
        out0 = pl.multiple_of(CONV_HIST + n * CONV_CHUNK, 8)
        win0 = out0 if transpose else pl.multiple_of(n * CONV_CHUNK, 8)
        win = src_ref[pl.ds(win0, CONV_CHUNK + CONV_HIST), :]
        acc = jnp.zeros((CONV_CHUNK, w.shape[1]), F32)
        for k in range(width):
            off = (width - 1 - k) if transpose else (CONV_HIST - (width - 1) + k)
            acc = acc + w[k:k + 1, :] * win[off:off + CONV_CHUNK, :]
        dst_ref[pl.ds(out0, CONV_CHUNK), :] = acc
        return carry

    lax.fori_loop(0, nseq // CONV_CHUNK, step, 0)


def _conv_wgrad(src_ref, dz_ref, acc_ref, width, nseq):
    acc_ref[...] = jnp.zeros(acc_ref.shape, F32)

    def step(n, carry):
        win = src_ref[pl.ds(pl.multiple_of(n * CONV_CHUNK, 8), CONV_CHUNK + CONV_HIST), :]
        dzc = dz_ref[pl.ds(pl.multiple_of(CONV_HIST + n * CONV_CHUNK, 8), CONV_CHUNK), :]
        for k in range(width):
            off = CONV_HIST - (width - 1) + k
            prod = dzc * win[off:off + CONV_CHUNK, :]
            part = prod[0:8, :]
            for s in range(1, CONV_CHUNK // 8):
                part = part + prod[8 * s:8 * s + 8, :]
            acc_ref[8 * k:8 * k + 8, :] += part
        return carry

    lax.fori_loop(0, nseq // CONV_CHUNK, step, 0)


def _to_seq(buf_ref, x_part, meta_part, seq):
    buf_ref[CONV_HIST:CONV_HIST + N_META, :] = meta_part
    buf_ref[CONV_HIST + N_META:CONV_HIST + N_META + seq, :] = x_part


def _zero_ends(buf_ref, nseq):
    zeros = jnp.zeros((CONV_HIST, buf_ref.shape[1]), F32)
    buf_ref[0:CONV_HIST, :] = zeros
    buf_ref[CONV_HIST + nseq:CONV_HIST + nseq + CONV_HIST, :] = zeros


def _mix_conv_fwd(hp5, wa, wb, bb):
    _, tp, wgrp = hp5.shape
    seq, nseq = _seq_rows(tp)
    sb = nseq + 2 * CONV_HIST
    ka, kb = wa.shape[0], wb.shape[0]
    xs, ms = slice(0, seq), slice(seq, seq + N_META)
    ox, om = slice(CONV_HIST + N_META, CONV_HIST + nseq), slice(CONV_HIST, CONV_HIST + N_META)

    def body(hp_ref, wa_ref, wb_ref, bb_ref, ya_ref, z_ref, s_ref, o_ref):
        _zero_ends(s_ref, nseq)
        _to_seq(s_ref, hp_ref[1, xs, :] * hp_ref[2, xs, :], hp_ref[1, ms, :] * hp_ref[2, ms, :], seq)
        _conv_taps(s_ref, wa_ref, o_ref, ka, nseq, False)
        ya_ref[xs, :] = (hp_ref[0, xs, :] * o_ref[ox, :]).astype(BF16)
        ya_ref[ms, :] = (hp_ref[0, ms, :] * o_ref[om, :]).astype(BF16)
        ya_ref[seq + N_META:tp, :] = jnp.zeros((tp - seq - N_META, LANES), BF16)
        _to_seq(s_ref, hp_ref[3, xs, :] * _sig(hp_ref[4, xs, :]), hp_ref[3, ms, :] * _sig(hp_ref[4, ms, :]), seq)
        _conv_taps(s_ref, wb_ref, o_ref, kb, nseq, False)
        z_ref[xs, :] = o_ref[ox, :] + bb_ref[...]
        z_ref[ms, :] = o_ref[om, :] + bb_ref[...]
        z_ref[seq + N_META:tp, :] = jnp.zeros((tp - seq - N_META, LANES), F32)

    col = lambda j: (0, j)
    return pl.pallas_call(
        body, name="mix_conv_fwd", grid=(wgrp // LANES,),
        in_specs=[pl.BlockSpec((5, tp, LANES), lambda j: (0, 0, j)), pl.BlockSpec((ka, LANES), col),
                  pl.BlockSpec((kb, LANES), col), pl.BlockSpec((1, LANES), col)],
        out_specs=[pl.BlockSpec((tp, LANES), col), pl.BlockSpec((tp, LANES), col)],
        out_shape=[_sds((tp, wgrp), BF16), _sds((tp, wgrp), F32)],
        scratch_shapes=[pltpu.VMEM((sb, LANES), F32), pltpu.VMEM((sb, LANES), F32)],
        compiler_params=_params(("arbitrary",)),
    )(hp5, wa, wb, bb)


def _layer_norm_parts(z, lg, lb):
    mu = _mean(z)
    zc = z - mu
    rl = lax.rsqrt(_mean(zc * zc) + LN_EPS)
    zh = zc * rl
    return rl, zh, zh * lg + lb


def _mm_out(ya, z, h, wout, lg, lb, gpm, g2):
    tp, d = h.shape
    wa_ = ya.shape[1]
    tm = _row_tile(tp)

    def body(ya_ref, z_ref, h_ref, w_ref, lg_ref, lb_ref, gpm_ref, g2_ref, yb_ref, mix_ref, h1_ref, xn2_ref):
        _, _, l = _layer_norm_parts(z_ref[...], lg_ref[...], lb_ref[...])
        yb = (l * _sig(l)).astype(BF16)
        yb_ref[...] = yb
        mix = _dot(ya_ref[...], w_ref[0:wa_, :]) + _dot(yb, w_ref[wa_:d, :])
        mix_ref[...] = mix
        rm = lax.rsqrt(_mean(mix * mix) + RMS_EPS)
        h1 = h_ref[...] + mix * rm * gpm_ref[...]
        h1_ref[...] = h1
        r2 = lax.rsqrt(_mean(h1 * h1) + RMS_EPS)
        xn2_ref[...] = (h1 * r2 * g2_ref[...]).astype(BF16)

    row = lambda i: (i, 0)
    return pl.pallas_call(
        body, name="mm_out", grid=(tp // tm,),
        in_specs=[pl.BlockSpec((tm, wa_), row), pl.BlockSpec((tm, wa_), row), pl.BlockSpec((tm, d), row),
                  _full(wout.shape), _full(lg.shape), _full(lb.shape), _full(gpm.shape), _full(g2.shape)],
        out_specs=[pl.BlockSpec((tm, wa_), row), pl.BlockSpec((tm, d), row), pl.BlockSpec((tm, d), row),
                   pl.BlockSpec((tm, d), row)],
        out_shape=[_sds((tp, wa_), BF16), _sds((tp, d), F32), _sds((tp, d), F32), _sds((tp, d), BF16)],
        compiler_params=_params(("arbitrary",)),
    )(ya, z, h, wout, lg, lb, gpm, g2)


def _ffn_up(xn2, wg4, wu4):
    tp, d = xn2.shape
    n_sh, fs, _ = wg4.shape
    tm = _row_tile(tp)

    def body(xn_ref, wg_ref, wu_ref, a_ref, u_ref, f_ref):
        xn = xn_ref[...]
        a = _dot(xn, wg_ref[0], NT)
        u = _dot(xn, wu_ref[0], NT)
        a_ref[0] = a.astype(BF16)
        u_ref[0] = u.astype(BF16)
        f_ref[0] = (a * _sig(a) * u).astype(BF16)

    wspec = pl.BlockSpec((1, fs, d), lambda k, i: (k, 0, 0))
    ospec = pl.BlockSpec((1, tm, fs), lambda k, i: (k, i, 0))
    return pl.pallas_call(
        body, name="ffn_up", grid=(n_sh, tp // tm),
        in_specs=[pl.BlockSpec((tm, d), lambda k, i: (i, 0)), wspec, wspec],
        out_specs=[ospec, ospec, ospec],
        out_shape=[_sds((n_sh, tp, fs), BF16)] * 3,
        compiler_params=_params(("arbitrary", "arbitrary")),
    )(xn2, wg4, wu4)


def _ffn_down(f4, wd4, h1, tgt, gpf):
    n_sh, tp, fs = f4.shape
    d = h1.shape[1]
    tm = _row_tile(tp)
    seq, _ = _seq_rows(tp)

    def body(f_ref, w_ref, h1_ref, t_ref, gpf_ref, dff_ref, dh2_ref, loss_ref, dgpf_ref):
        i = pl.program_id(0)
        ff = _dot(f_ref[0], w_ref[0])
        for k in range(1, n_sh):
            ff = ff + _dot(f_ref[k], w_ref[k])
        rf = lax.rsqrt(_mean(ff * ff) + RMS_EPS)
        nf = ff * rf
        gpf_ = gpf_ref[...]
        h2 = h1_ref[...] + nf * gpf_
        rows = i * tm + lax.broadcasted_iota(jnp.int32, (tm, 1), 0)
        err = jnp.where(rows < seq, h2 - t_ref[...], 0.0)
        dh2 = err * (1.0 / d)
        dh2_ref[...] = dh2
        dn = dh2 * gpf_
        dff_ref[...] = (rf * (dn - nf * _mean(dn * nf))).astype(BF16)

        @pl.when(i == 0)
        def _():
            loss_ref[...] = jnp.zeros(loss_ref.shape, F32)
            dgpf_ref[...] = jnp.zeros(dgpf_ref.shape, F32)

        loss_ref[...] += (0.5 / d) * jnp.sum(err * err, axis=(0, 1), keepdims=True)
        dgpf_ref[...] += jnp.sum(dh2 * nf, axis=0, keepdims=True)

    row = lambda i: (i, 0)
    return pl.pallas_call(
        body, name="ffn_down", grid=(tp // tm,),
        in_specs=[pl.BlockSpec((n_sh, tm, fs), lambda i: (0, i, 0)), _full(wd4.shape), pl.BlockSpec((tm, d), row),
                  pl.BlockSpec((tm, d), row), _full(gpf.shape)],
        out_specs=[pl.BlockSpec((tm, d), row), pl.BlockSpec((tm, d), row), _full((8, LANES)), _full((1, d))],
        out_shape=[_sds((tp, d), BF16), _sds((tp, d), F32), _sds((8, LANES), F32), _sds((1, d), F32)],
        compiler_params=_params(("arbitrary",)),
    )(f4, wd4, h1, tgt, gpf)


def _ffn_bwd_act(dff, wd4, a4, u4):
    tp, d = dff.shape
    n_sh, fs, _ = wd4.shape
    tm = _row_tile(tp)

    def body(dff_ref, w_ref, a_ref, u_ref, da_ref, du_ref):
        df = _dot(dff_ref[...], w_ref[0], NT)
        a = a_ref[0].astype(F32)
        u = u_ref[0].astype(F32)
        s = _sig(a)
        da_ref[0] = (df * u * (s * (1.0 + a * (1.0 - s)))).astype(BF16)
        du_ref[0] = (df * a * s).astype(BF16)

    aspec = pl.BlockSpec((1, tm, fs), lambda k, i: (k, i, 0))
    return pl.pallas_call(
        body, name="ffn_bwd_act", grid=(n_sh, tp // tm),
        in_specs=[pl.BlockSpec((tm, d), lambda k, i: (i, 0)), pl.BlockSpec((1, fs, d), lambda k, i: (k, 0, 0)),
                  aspec, aspec],
        out_specs=[aspec, aspec],
        out_shape=[_sds((n_sh, tp, fs), BF16)] * 2,
        compiler_params=_params(("arbitrary", "arbitrary")),
    )(dff, wd4, a4, u4)


def _grad_w_down(f4, dff):
    n_sh, tp, fs = f4.shape
    d = dff.shape[1]

    def body(f_ref, dff_ref, g_ref):
        g_ref[0] = _dot(f_ref[0], dff_ref[...], TN).astype(BF16)

    return pl.pallas_call(
        body, name="grad_w_down", grid=(n_sh,),
        in_specs=[pl.BlockSpec((1, tp, fs), lambda k: (k, 0, 0)), _full(dff.shape)],
        out_specs=pl.BlockSpec((1, fs, d), lambda k: (k, 0, 0)),
        out_shape=_sds((n_sh, fs, d), BF16),
        compiler_params=_params(("arbitrary",)),
    )(f4, dff)


def _grad_w_gate_up(xn2, da4, du4):
    n_sh, tp, fs = da4.shape
    d = xn2.shape[1]

    def body(xn_ref, da_ref, du_ref, gg_ref, gu_ref):
        xn = xn_ref[...]
        gg_ref[0] = _dot(da_ref[0], xn, TN).astype(BF16)
        gu_ref[0] = _dot(du_ref[0], xn, TN).astype(BF16)

    aspec = pl.BlockSpec((1, tp, fs), lambda k: (k, 0, 0))
    gspec = pl.BlockSpec((1, fs, d), lambda k: (k, 0, 0))
    return pl.pallas_call(
        body, name="grad_w_gate_up", grid=(n_sh,),
        in_specs=[_full(xn2.shape), aspec, aspec],
        out_specs=[gspec, gspec],
        out_shape=[_sds((n_sh, fs, d), BF16)] * 2,
        compiler_params=_params(("arbitrary",)),
    )(xn2, da4, du4)


def _rms_bwd(dy, x, r, g):
    n = x * r
    dn = dy * g
    return r * (dn - n * _mean(dn * n)), dy * n


def _ffn_bwd_in(da4, du4, wg4, wu4, h1, mix, dh2, g2, gpm):
    n_sh, tp, fs = da4.shape
    d = h1.shape[1]
    tm = _row_tile(tp)

    def body(da_ref, du_ref, wg_ref, wu_ref, h1_ref, mix_ref, dh2_ref, g2_ref, gpm_ref,
             dh1_ref, dmix_ref, dg2_ref, dgpm_ref, acc_ref):
        i, k = pl.program_id(0), pl.program_id(1)
        part = _dot(da_ref[0], wg_ref[0]) + _dot(du_ref[0], wu_ref[0])

        @pl.when(k == 0)
        def _():
            acc_ref[...] = part

        @pl.when(k > 0)
        def _():
            acc_ref[...] += part

        @pl.when((i == 0) & (k == 0))
        def _():
            dg2_ref[...] = jnp.zeros(dg2_ref.shape, F32)
            dgpm_ref[...] = jnp.zeros(dgpm_ref.shape, F32)

        @pl.when(k == n_sh - 1)
        def _():
            h1v = h1_ref[...]
            r2 = lax.rsqrt(_mean(h1v * h1v) + RMS_EPS)
            dres, dg2_rows = _rms_bwd(acc_ref[...], h1v, r2, g2_ref[...])
            dh1 = dh2_ref[...] + dres
            dh1_ref[...] = dh1
            mixv = mix_ref[...]
            rm = lax.rsqrt(_mean(mixv * mixv) + RMS_EPS)
            dmix, dgpm_rows = _rms_bwd(dh1, mixv, rm, gpm_ref[...])
            dmix_ref[...] = dmix.astype(BF16)
            dg2_ref[...] += jnp.sum(dg2_rows, axis=0, keepdims=True)
            dgpm_ref[...] += jnp.sum(dgpm_rows, axis=0, keepdims=True)

    aspec = pl.BlockSpec((1, tm, fs), lambda i, k: (k, i, 0))
    wspec = pl.BlockSpec((1, fs, d), lambda i, k: (k, 0, 0))
    row = pl.BlockSpec((tm, d), lambda i, k: (i, 0))
    return pl.pallas_call(
        body, name="ffn_bwd_in", grid=(tp // tm, n_sh),
        in_specs=[aspec, aspec, wspec, wspec, row, row, row, _full(g2.shape), _full(gpm.shape)],
        out_specs=[row, row, _full((1, d)), _full((1, d))],
        out_shape=[_sds((tp, d), F32), _sds((tp, d), BF16), _sds((1, d), F32), _sds((1, d), F32)],
        scratch_shapes=[pltpu.VMEM((tm, d), F32)],
        compiler_params=_params(("arbitrary", "arbitrary")),
    )(da4, du4, wg4, wu4, h1, mix, dh2, g2, gpm)


def _grad_w_out(ya, yb, dmix):
    tp, wa_ = ya.shape
    d = dmix.shape[1]

    def body(ya_ref, yb_ref, dmix_ref, g_ref):
        dm = dmix_ref[...]
        g_ref[0:wa_, :] = _dot(ya_ref[...], dm, TN).astype(BF16)
        g_ref[wa_:2 * wa_, :] = _dot(yb_ref[...], dm, TN).astype(BF16)

    return pl.pallas_call(
        body, name="grad_w_out", grid=(1,),
        in_specs=[_full(ya.shape), _full(yb.shape), _full(dmix.shape)],
        out_specs=_full((2 * wa_, d)),
        out_shape=_sds((2 * wa_, d), BF16),
        compiler_params=_params(("arbitrary",)),
    )(ya, yb, dmix)


def _mix_bwd_out(dmix, wout, z, lg, lb):
    tp, d = dmix.shape
    wa_ = z.shape[1]
    tm = _row_tile(tp)

    def body(dmix_ref, w_ref, z_ref, lg_ref, lb_ref, dya_ref, dz_ref, dlg_ref, dlb_ref):
        i = pl.program_id(0)
        dm = dmix_ref[...]
        dya_ref[...] = _dot(dm, w_ref[0:wa_, :], NT)
        dyb = _dot(dm, w_ref[wa_:d, :], NT)
        lg_ = lg_ref[...]
        rl, zh, l = _layer_norm_parts(z_ref[...], lg_, lb_ref[...])
        sl = _sig(l)
        dl = dyb * (sl * (1.0 + l * (1.0 - sl)))
        dzh = dl * lg_
        dz_ref[...] = rl * (dzh - _mean(dzh) - zh * _mean(dzh * zh))

        @pl.when(i == 0)
        def _():
            dlg_ref[...] = jnp.zeros(dlg_ref.shape, F32)
            dlb_ref[...] = jnp.zeros(dlb_ref.shape, F32)

        dlg_ref[...] += jnp.sum(dl * zh, axis=0, keepdims=True)
        dlb_ref[...] += jnp.sum(dl, axis=0, keepdims=True)

    row = lambda i: (i, 0)
    return pl.pallas_call(
        body, name="mix_bwd_out", grid=(tp // tm,),
        in_specs=[pl.BlockSpec((tm, d), row), _full(wout.shape), pl.BlockSpec((tm, wa_), row), _full(lg.shape),
                  _full(lb.shape)],
        out_specs=[pl.BlockSpec((tm, wa_), row), pl.BlockSpec((tm, wa_), row), _full((1, wa_)), _full((1, wa_))],
        out_shape=[_sds((tp, wa_), F32), _sds((tp, wa_), F32), _sds((1, wa_), F32), _sds((1, wa_), F32)],
        compiler_params=_params(("arbitrary",)),
    )(dmix, wout, z, lg, lb)


def _mix_conv_bwd(hp5, dya, dz, wa, wb):
    _, tp, wgrp = hp5.shape
    seq, nseq = _seq_rows(tp)
    sb = nseq + 2 * CONV_HIST
    ka, kb = wa.shape[0], wb.shape[0]
    xs, ms = slice(0, seq), slice(seq, seq + N_META)
    ox, om = slice(CONV_HIST + N_META, CONV_HIST + nseq), slice(CONV_HIST, CONV_HIST + N_META)
    n_tail = tp - seq - N_META

    def body(hp_ref, dya_ref, dz_ref, wa_ref, wb_ref, dhp_ref, dwa_ref, dwb_ref, dbb_ref, s_ref, d_ref, o_ref, acc_ref):
        _zero_ends(s_ref, nseq)
        _zero_ends(d_ref, nseq)

        def put(p, ox_val, om_val):
            dhp_ref[p, xs, :] = ox_val.astype(BF16)
            dhp_ref[p, ms, :] = om_val.astype(BF16)
            dhp_ref[p, seq + N_META:tp, :] = jnp.zeros((n_tail, LANES), BF16)

        def wgrad(dw_ref, width):
            for k in range(width):
                dw_ref[k:k + 1, :] = jnp.sum(acc_ref[8 * k:8 * k + 8, :], axis=0, keepdims=True)

        _to_seq(s_ref, hp_ref[1, xs, :] * hp_ref[2, xs, :], hp_ref[1, ms, :] * hp_ref[2, ms, :], seq)
        _conv_taps(s_ref, wa_ref, o_ref, ka, nseq, False)
        put(0, dya_ref[xs, :] * o_ref[ox, :], dya_ref[ms, :] * o_ref[om, :])
        _to_seq(d_ref, dya_ref[xs, :] * hp_ref[0, xs, :], dya_ref[ms, :] * hp_ref[0, ms, :], seq)
        _conv_wgrad(s_ref, d_ref, acc_ref, ka, nseq)
        wgrad(dwa_ref, ka)
        _conv_taps(d_ref, wa_ref, o_ref, ka, nseq, True)
        put(1, o_ref[ox, :] * hp_ref[2, xs, :], o_ref[om, :] * hp_ref[2, ms, :])
        put(2, o_ref[ox, :] * hp_ref[1, xs, :], o_ref[om, :] * hp_ref[1, ms, :])

        _to_seq(s_ref, hp_ref[3, xs, :] * _sig(hp_ref[4, xs, :]), hp_ref[3, ms, :] * _sig(hp_ref[4, ms, :]), seq)
        _to_seq(d_ref, dz_ref[xs, :], dz_ref[ms, :], seq)
        dbb_ref[...] = (jnp.sum(dz_ref[xs, :], axis=0, keepdims=True)
                        + jnp.sum(dz_ref[ms, :], axis=0, keepdims=True))
        _conv_wgrad(s_ref, d_ref, acc_ref, kb, nseq)
        wgrad(dwb_ref, kb)
        _conv_taps(d_ref, wb_ref, o_ref, kb, nseq, True)
        sx, sm = _sig(hp_ref[4, xs, :]), _sig(hp_ref[4, ms, :])
        put(3, o_ref[ox, :] * sx, o_ref[om, :] * sm)
        put(4, o_ref[ox, :] * hp_ref[3, xs, :] * sx * (1.0 - sx), o_ref[om, :] * hp_ref[3, ms, :] * sm * (1.0 - sm))

    col = lambda j: (0, j)
    blk5 = pl.BlockSpec((5, tp, LANES), lambda j: (0, 0, j))
    return pl.pallas_call(
        body, name="mix_conv_bwd", grid=(wgrp // LANES,),
        in_specs=[blk5, pl.BlockSpec((tp, LANES), col), pl.BlockSpec((tp, LANES), col),
                  pl.BlockSpec((ka, LANES), col), pl.BlockSpec((kb, LANES), col)],
        out_specs=[blk5, pl.BlockSpec((ka, LANES), col), pl.BlockSpec((kb, LANES), col), pl.BlockSpec((1, LANES), col)],
        out_shape=[_sds((5, tp, wgrp), BF16), _sds((ka, wgrp), F32), _sds((kb, wgrp), F32), _sds((1, wgrp), F32)],
        scratch_shapes=[pltpu.VMEM((sb, LANES), F32), pltpu.VMEM((sb, LANES), F32), pltpu.VMEM((sb, LANES), F32),
                        pltpu.VMEM((8 * kb, LANES), F32)],
        compiler_params=_params(("arbitrary",)),
    )(hp5, dya, dz, wa, wb)


def _grad_w_in(xn1, dhp5):
    n_p, tp, pw = dhp5.shape
    d = xn1.shape[1]

    def body(xn_ref, dhp_ref, g_ref):
        g_ref[...] = _dot(xn_ref[...], dhp_ref[0], TN).astype(BF16)

    return pl.pallas_call(
        body, name="grad_w_in", grid=(n_p,),
        in_specs=[_full(xn1.shape), pl.BlockSpec((1, tp, pw), lambda p: (p, 0, 0))],
        out_specs=pl.BlockSpec((d, pw), lambda p: (0, p)),
        out_shape=_sds((d, n_p * pw), BF16),
        compiler_params=_params(("arbitrary",)),
    )(xn1, dhp5)


def _mix_bwd_in(dhp5, win4, h, dh1, g1):
    n_p, tp, pw = dhp5.shape
    d = h.shape[1]
    n_sh, _, csh = win4.shape
    tm = _row_tile(tp)
    pieces = _pieces(n_sh, csh, pw)

    def body(dhp_ref, w_ref, h_ref, dh1_ref, g_ref, dh_ref, dg1_ref):
        i = pl.program_id(0)
        dxn = None
        for k, klo, p, plo, w in pieces:
            t = _dot(dhp_ref[p, :, plo:plo + w], w_ref[k, :, klo:klo + w], NT)
            dxn = t if dxn is None else dxn + t
        hh = h_ref[...]
        r1 = lax.rsqrt(_mean(hh * hh) + RMS_EPS)
        dres, dg_rows = _rms_bwd(dxn, hh, r1, g_ref[...])
        dh_ref[...] = dh1_ref[...] + dres

        @pl.when(i == 0)
        def _():
            dg1_ref[...] = jnp.zeros(dg1_ref.shape, F32)

        dg1_ref[...] += jnp.sum(dg_rows, axis=0, keepdims=True)

    row = lambda i: (i, 0)
    return pl.pallas_call(
        body, name="mix_bwd_in", grid=(tp // tm,),
        in_specs=[pl.BlockSpec((n_p, tm, pw), lambda i: (0, i, 0)), _full(win4.shape), pl.BlockSpec((tm, d), row),
                  pl.BlockSpec((tm, d), row), _full(g1.shape)],
        out_specs=[pl.BlockSpec((tm, d), row), _full((1, d))],
        out_shape=[_sds((tp, d), F32), _sds((1, d), F32)],
        compiler_params=_params(("arbitrary",)),
    )(dhp5, win4, h, dh1, g1)


def _other_chips(x, y):
    out = []
    for j in (1, 2, 3):
        px, py = _flip(x, j >> 1), _flip(y, j & 1)
        out.append((px, py, 2 * px + py))
    return out


PAIR_COLLECTIVE_ID = 0


def _pair_barrier(x, y, c):
    sem = pltpu.get_barrier_semaphore()
    pl.semaphore_signal(sem, inc=1, device_id=(x, y, 1 - c), device_id_type=MESH)
    pl.semaphore_wait(sem, 1)


def _pair_params():
    return pltpu.CompilerParams(collective_id=PAIR_COLLECTIVE_ID)


def _half_rows(c, rows_half):
    return pl.ds(pl.multiple_of(c * rows_half, 8), rows_half)


def _gather_shards(locs):
    n = len(locs)
    halves = [a.shape[0] // 2 for a in locs]

    def body(*refs):
        loc, full = refs[:n], refs[n:2 * n]
        ssem, rsem, lsem = refs[2 * n:]
        x, y, c = _mesh_pos()
        q = 2 * x + y
        chips = _other_chips(x, y)

        def remote(i, chip_no, half, to, s):
            part = full[i].at[chip_no, _half_rows(half, halves[i]), :]
            return pltpu.make_async_remote_copy(src_ref=part, dst_ref=part, send_sem=ssem.at[s], recv_sem=rsem.at[s],
                                                device_id=to, device_id_type=MESH)

        own = [pltpu.make_async_copy(loc[i], full[i].at[q], lsem.at[i]) for i in range(n)]
        for cp in own:
            cp.start()
        first = []
        for i in range(n):
            for j, (px, py, _) in enumerate(chips):
                src = loc[i].at[_half_rows(c, halves[i]), :]
                dst = full[i].at[q, _half_rows(c, halves[i]), :]
                cp = pltpu.make_async_remote_copy(src_ref=src, dst_ref=dst, send_sem=ssem.at[3 * i + j],
                                                  recv_sem=rsem.at[3 * i + j], device_id=(px, py, c), device_id_type=MESH)
                cp.start()
                first.append(cp)
        passed = []
        for i in range(n):
            for j, (_, _, qj) in enumerate(chips):
                remote(i, qj, c, (x, y, c), 3 * i + j).wait_recv()
                cp = remote(i, qj, c, (x, y, 1 - c), 3 * n + 3 * i + j)
                cp.start()
                passed.append(cp)
        for i in range(n):
            for j, (_, _, qj) in enumerate(chips):
                remote(i, qj, 1 - c, (x, y, c), 3 * n + 3 * i + j).wait_recv()
        for cp in first + passed:
            cp.wait_send()
        for cp in own:
            cp.wait()

    return pl.pallas_call(
        body, name="gather_shards",
        in_specs=[ANY] * n, out_specs=[ANY] * n,
        out_shape=[_sds((N_CHIPS,) + a.shape, a.dtype) for a in locs],
        scratch_shapes=[pltpu.SemaphoreType.DMA((6 * n,)), pltpu.SemaphoreType.DMA((6 * n,)),
                        pltpu.SemaphoreType.DMA((n,))],
    )(*locs)


HBM = pl.BlockSpec(memory_space=pltpu.HBM)
SEM = pl.BlockSpec(memory_space=pltpu.SEMAPHORE)
EFFECT = pltpu.SideEffectType.DATAFLOW_SIDE_EFFECTING


def _in_hbm(a):
    return pltpu.with_memory_space_constraint(a, pltpu.HBM)


def _gather_start(locs, after):
    n = len(locs)
    halves = [a.shape[0] // 2 for a in locs]

    def body(*refs):
        loc, land = refs[:n], refs[n:2 * n]
        ssem, rsem = refs[2 * n + 1], refs[2 * n + 2]
        token = refs[-1]
        x, y, c = _mesh_pos()
        q = 2 * x + y
        for i in range(n):
            for j, (px, py, _) in enumerate(_other_chips(x, y)):
                rows = _half_rows(c, halves[i])
                pltpu.make_async_remote_copy(src_ref=loc[i].at[rows, :], dst_ref=land[i].at[q, rows, :],
                                             send_sem=ssem.at[3 * i + j], recv_sem=rsem.at[3 * i + j],
                                             device_id=(px, py, c), device_id_type=MESH).start()
        token[...] = jnp.zeros(token.shape, F32)

    lands = [lax.empty((N_CHIPS,) + a.shape, a.dtype) for a in locs]
    outs = pl.pallas_call(
        body, name="gather_start",
        in_specs=[HBM] * (2 * n) + [ANY], out_specs=[SEM, SEM] + [HBM] * (2 * n) + [VMEM],
        out_shape=[pltpu.SemaphoreType.DMA((3 * n,)), pltpu.SemaphoreType.DMA((3 * n,))]
        + [pltpu.HBM(a.shape, a.dtype) for a in locs] + [pltpu.HBM(a.shape, a.dtype) for a in lands]
        + [_sds((8, LANES), F32)],
        input_output_aliases={i: 2 + i for i in range(2 * n)},
        compiler_params=pltpu.CompilerParams(has_side_effects=EFFECT),
    )(*[_in_hbm(a) for a in locs], *[_in_hbm(a) for a in lands], after)
    return outs[0], outs[1], list(outs[2:2 + n]), list(outs[2 + n:2 + 2 * n]), outs[-1]


def _gather_wait(which, ssem, rsem, locs, lands, after, tag):
    m = len(which)
    halves = [a.shape[0] // 2 for a in locs]

    def body(*refs):
        loc, land = refs[:m], refs[m:2 * m]
        ssem_, rsem_ = refs[2 * m], refs[2 * m + 1]
        x, y, c = _mesh_pos()
        for t, i in enumerate(which):
            for j, (px, py, qj) in enumerate(_other_chips(x, y)):
                rows = _half_rows(c, halves[t])
                cp = pltpu.make_async_remote_copy(src_ref=loc[t].at[rows, :], dst_ref=land[t].at[qj, rows, :],
                                                  send_sem=ssem_.at[3 * i + j], recv_sem=rsem_.at[3 * i + j],
                                                  device_id=(px, py, c), device_id_type=MESH)
                cp.wait_send()
                cp.wait_recv()

    outs = pl.pallas_call(
        body, name="gather_wait_" + tag,
        in_specs=[HBM] * (2 * m) + [SEM, SEM, ANY], out_specs=[HBM] * (2 * m),
        out_shape=[pltpu.HBM(a.shape, a.dtype) for a in locs] + [pltpu.HBM(a.shape, a.dtype) for a in lands],
        input_output_aliases={i: i for i in range(2 * m)},
        compiler_params=pltpu.CompilerParams(has_side_effects=EFFECT),
    )(*locs, *lands, ssem, rsem, after)
    return list(outs[:m]), list(outs[m:])


def _forward_pair(locs, lands, tag):
    n = len(locs)
    halves = [a.shape[0] // 2 for a in locs]

    def body(*refs):
        loc, full = refs[:n], refs[2 * n:3 * n]
        ssem, rsem, lsem = refs[3 * n:]
        x, y, c = _mesh_pos()
        q = 2 * x + y
        _pair_barrier(x, y, c)
        own = [pltpu.make_async_copy(loc[i], full[i].at[q], lsem.at[i]) for i in range(n)]
        for cp in own:
            cp.start()
        cps = []
        for i in range(n):
            for j, (_, _, qj) in enumerate(_other_chips(x, y)):
                part = full[i].at[qj, _half_rows(c, halves[i]), :]
                cp = pltpu.make_async_remote_copy(src_ref=part, dst_ref=part, send_sem=ssem.at[3 * i + j],
                                                  recv_sem=rsem.at[3 * i + j], device_id=(x, y, 1 - c), device_id_type=MESH)
                cp.start()
                cps.append(cp)
        for cp in cps:
            cp.wait()
        for cp in own:
            cp.wait()

    return pl.pallas_call(
        body, name="forward_pair_" + tag,
        in_specs=[ANY] * (2 * n), out_specs=[ANY] * n,
        out_shape=[_sds(a.shape, a.dtype) for a in lands],
        input_output_aliases={n + i: i for i in range(n)},
        scratch_shapes=[pltpu.SemaphoreType.DMA((3 * n,)), pltpu.SemaphoreType.DMA((3 * n,)),
                        pltpu.SemaphoreType.DMA((n,))],
        compiler_params=_pair_params(),
    )(*locs, *lands)


def _chip_exchange_start(parts, after, tag):
    n = len(parts)

    def body(*refs):
        src, land = refs[:n], refs[n:2 * n]
        ssem, rsem = refs[2 * n + 1], refs[2 * n + 2]
        token = refs[-1]
        x, y, c = _mesh_pos()
        for i in range(n):
            for j, (px, py, qj) in enumerate(_other_chips(x, y)):
                pltpu.make_async_remote_copy(src_ref=src[i].at[qj], dst_ref=land[i].at[j], send_sem=ssem.at[3 * i + j],
                                             recv_sem=rsem.at[3 * i + j], device_id=(px, py, c), device_id_type=MESH).start()
        token[...] = jnp.zeros(token.shape, F32)

    lands = [lax.empty((3,) + a.shape[1:], a.dtype) for a in parts]
    outs = pl.pallas_call(
        body, name="chip_exchange_start_" + tag,
        in_specs=[HBM] * (2 * n) + [ANY], out_specs=[SEM, SEM] + [HBM] * (2 * n) + [VMEM],
        out_shape=[pltpu.SemaphoreType.DMA((3 * n,)), pltpu.SemaphoreType.DMA((3 * n,))]
        + [pltpu.HBM(a.shape, a.dtype) for a in parts] + [pltpu.HBM(a.shape, a.dtype) for a in lands]
        + [_sds((8, LANES), F32)],
        input_output_aliases={i: 2 + i for i in range(2 * n)},
        compiler_params=pltpu.CompilerParams(has_side_effects=EFFECT),
    )(*[_in_hbm(a) for a in parts], *[_in_hbm(a) for a in lands], after)
    return outs[0], outs[1], list(outs[2:2 + n]), list(outs[2 + n:2 + 2 * n]), outs[-1]


def _chip_exchange_wait(ssem, rsem, parts, lands, after, tag):
    n = len(parts)

    def body(*refs):
        src, land = refs[:n], refs[n:2 * n]
        ssem_, rsem_ = refs[2 * n], refs[2 * n + 1]
        x, y, c = _mesh_pos()
        for i in range(n):
            for j, (px, py, qj) in enumerate(_other_chips(x, y)):
                cp = pltpu.make_async_remote_copy(src_ref=src[i].at[qj], dst_ref=land[i].at[j], send_sem=ssem_.at[3 * i + j],
                                                  recv_sem=rsem_.at[3 * i + j], device_id=(px, py, c), device_id_type=MESH)
                cp.wait_send()
                cp.wait_recv()

    outs = pl.pallas_call(
        body, name="chip_exchange_wait_" + tag,
        in_specs=[HBM] * (2 * n) + [SEM, SEM, ANY], out_specs=[HBM] * (2 * n),
        out_shape=[pltpu.HBM(a.shape, a.dtype) for a in parts] + [pltpu.HBM(a.shape, a.dtype) for a in lands],
        input_output_aliases={i: i for i in range(2 * n)},
        compiler_params=pltpu.CompilerParams(has_side_effects=EFFECT),
    )(*parts, *lands, ssem, rsem, after)
    return list(outs[:n]), list(outs[n:])


def _pair_exchange_grads(grads, half_axis, tag):
    n = len(grads)

    def half_of(ref, i, which):
        rows = grads[i].shape[half_axis[i]] // 2
        if half_axis[i] == 0:
            return ref.at[_half_rows(which, rows), :]
        return ref.at[:, _half_rows(which, rows), :]

    def out_shape(i):
        s = list(grads[i].shape)
        s[half_axis[i]] //= 2
        return _sds(tuple(s), grads[i].dtype)

    def body(*refs):
        g, got = refs[:n], refs[n:2 * n]
        ssem, rsem = refs[2 * n:]
        x, y, c = _mesh_pos()
        _pair_barrier(x, y, c)
        cps = []
        for i in range(n):
            cp = pltpu.make_async_remote_copy(src_ref=half_of(g[i], i, 1 - c), dst_ref=got[i], send_sem=ssem.at[i],
                                              recv_sem=rsem.at[i], device_id=(x, y, 1 - c), device_id_type=MESH)
            cp.start()
            cps.append(cp)
        for cp in cps:
            cp.wait()

    return pl.pallas_call(
        body, name="pair_exchange_grads_" + tag,
        in_specs=[ANY] * n, out_specs=[ANY] * n,
        out_shape=[out_shape(i) for i in range(n)],
        scratch_shapes=[pltpu.SemaphoreType.DMA((n,)), pltpu.SemaphoreType.DMA((n,))],
        compiler_params=_pair_params(),
    )(*grads)


def _pair_sum(g, got, c_arr, col_sharded, tag):
    if col_sharded:
        rows, cols = g.shape
        rh, cs = rows // 2, cols // N_CHIPS
        g_spec = pl.BlockSpec((rh, cs), lambda k, c_ref: (c_ref[0], k))
        got_spec = pl.BlockSpec((rh, cs), lambda k, c_ref: (0, k))
    else:
        _, rows, cs = g.shape
        rh = rows // 2
        g_spec = pl.BlockSpec((1, rh, cs), lambda k, c_ref: (k, c_ref[0], 0))
        got_spec = pl.BlockSpec((1, rh, cs), lambda k, c_ref: (k, 0, 0))

    def body(c_ref, g_ref, got_ref, out_ref):
        total = g_ref[...].astype(F32) + got_ref[...].astype(F32)
        out_ref[...] = total.astype(BF16).reshape(out_ref.shape)

    return pl.pallas_call(
        body, name="pair_sum_" + tag,
        grid_spec=pltpu.PrefetchScalarGridSpec(
            num_scalar_prefetch=1, grid=(N_CHIPS,), in_specs=[g_spec, got_spec],
            out_specs=pl.BlockSpec((1, rh, cs), lambda k, c_ref: (k, 0, 0))),
        out_shape=_sds((N_CHIPS, rh, cs), BF16),
        compiler_params=_params(("arbitrary",)),
    )(c_arr, g, got)


def _chip_sum(part, got, qc_arr, tag):
    _, rh, cs = part.shape

    def body(qc_ref, part_ref, got_ref, out_ref):
        total = part_ref[0].astype(F32)
        for j in range(3):
            total = total + got_ref[j].astype(F32)
        out_ref[...] = total

    return pl.pallas_call(
        body, name="chip_sum_" + tag,
        grid_spec=pltpu.PrefetchScalarGridSpec(
            num_scalar_prefetch=1, grid=(1,),
            in_specs=[pl.BlockSpec((1, rh, cs), lambda i, qc: (qc[0], 0, 0)), pl.BlockSpec((3, rh, cs), lambda i, qc: (0, 0, 0))],
            out_specs=pl.BlockSpec((rh, cs), lambda i, qc: (qc[1], 0))),
        out_shape=_sds((2 * rh, cs), F32),
        compiler_params=_params(("arbitrary",)),
    )(qc_arr, part, got)


def _pair_share_grads(grads, tag):
    n = len(grads)

    def body(*refs):
        g = refs[n:2 * n]
        ssem, rsem = refs[2 * n:]
        x, y, c = _mesh_pos()
        _pair_barrier(x, y, c)
        cps = []
        for i in range(n):
            mine = g[i].at[_half_rows(c, grads[i].shape[0] // 2), :]
            cp = pltpu.make_async_remote_copy(src_ref=mine, dst_ref=mine, send_sem=ssem.at[i], recv_sem=rsem.at[i],
                                              device_id=(x, y, 1 - c), device_id_type=MESH)
            cp.start()
            cps.append(cp)
        for cp in cps:
            cp.wait()

    return pl.pallas_call(
        body, name="pair_share_grads_" + tag,
        in_specs=[ANY] * n, out_specs=[ANY] * n,
        out_shape=[_sds(a.shape, a.dtype) for a in grads],
        input_output_aliases={i: i for i in range(n)},
        scratch_shapes=[pltpu.SemaphoreType.DMA((n,)), pltpu.SemaphoreType.DMA((n,))],
        compiler_params=_pair_params(),
    )(*grads)


def _small_allreduce(parts, places, rows_total, width):
    n = len(parts)

    def body(*refs):
        ins, out_ref = refs[:n], refs[n]
        pack, pair_got, chip_sum, got, ssem, rsem = refs[n + 1:]
        x, y, c = _mesh_pos()
        chip = 2 * x + y
        pack[...] = jnp.zeros(pack.shape, F32)
        for i in range(n):
            for row, col, src_row, rows in places[i]:
                w = parts[i].shape[1]
                pack[row:row + rows, col:col + w] = ins[i][src_row:src_row + rows, :]
        swap = pltpu.make_async_remote_copy(src_ref=pack, dst_ref=pair_got, send_sem=ssem.at[3], recv_sem=rsem.at[3],
                                            device_id=(x, y, 1 - c), device_id_type=MESH)
        swap.start()
        swap.wait()
        chip_sum[...] = pack[...] + pair_got[...]
        cps = []
        for j, (px, py, _) in enumerate(_other_chips(x, y)):
            cp = pltpu.make_async_remote_copy(src_ref=chip_sum, dst_ref=got.at[j], send_sem=ssem.at[j],
                                              recv_sem=rsem.at[j], device_id=(px, py, c), device_id_type=MESH)
            cp.start()
            cps.append(cp)
        for cp in cps:
            cp.wait()
        total = jnp.zeros(pack.shape, F32)
        for q in range(N_CHIPS):
            rel = jnp.bitwise_xor(chip, q)
            theirs = got[jnp.maximum(rel - 1, 0)]
            total = total + jnp.where(rel == 0, chip_sum[...], theirs)
        out_ref[...] = total

    return pl.pallas_call(
        body, name="small_allreduce",
        in_specs=[VMEM] * n, out_specs=VMEM,
        out_shape=_sds((rows_total, width), F32),
        scratch_shapes=[pltpu.VMEM((rows_total, width), F32), pltpu.VMEM((rows_total, width), F32),
                        pltpu.VMEM((rows_total, width), F32), pltpu.VMEM((3, rows_total, width), F32),
                        pltpu.SemaphoreType.DMA((4,)), pltpu.SemaphoreType.DMA((4,))],
        compiler_params=_params(),
    )(*parts)


def _adamw_math(w, g, m, v):
    m2 = ADAM_B1 * m + (1.0 - ADAM_B1) * g
    v2 = ADAM_B2 * v + (1.0 - ADAM_B2) * (g * g)
    m_hat = m2 / (1.0 - ADAM_B1 ** ADAM_STEP)
    v_hat = v2 / (1.0 - ADAM_B2 ** ADAM_STEP)
    delta = -ADAM_LR * (m_hat / (jnp.sqrt(v_hat) + ADAM_EPS) + ADAM_WD * w)
    return delta, m2, v2


def _adamw_big(w, g, m, v, tag):
    rows, cols = w.shape
    tr = _row_tile(rows)

    def body(w_ref, g_ref, m_ref, v_ref, go_ref, d_ref, m2_ref, v2_ref):
        gg = g_ref[...]
        go_ref[...] = gg
        d_ref[...], m2_ref[...], v2_ref[...] = _adamw_math(w_ref[...], gg, m_ref[...], v_ref[...])

    spec = pl.BlockSpec((tr, cols), lambda i: (i, 0))
    return pl.pallas_call(
        body, name="adamw_" + tag, grid=(rows // tr,),
        in_specs=[spec] * 4, out_specs=[spec] * 4,
        out_shape=[_sds((rows, cols), F32)] * 4,
        compiler_params=_params(("arbitrary",)),
    )(w, g, m, v)


def _adamw_small(ws, gs, ms, vs):
    n = len(ws)

    def body(*refs):
        w_r, g_r, m_r, v_r = refs[:n], refs[n:2 * n], refs[2 * n:3 * n], refs[3 * n:4 * n]
        d_o, m_o, v_o = refs[4 * n:5 * n], refs[5 * n:6 * n], refs[6 * n:7 * n]
        for i in range(n):
            d_o[i][...], m_o[i][...], v_o[i][...] = _adamw_math(w_r[i][...], g_r[i][...], m_r[i][...], v_r[i][...])

    shapes = [_sds(w.shape, F32) for w in ws]
    outs = pl.pallas_call(
        body, name="adamw_small",
        in_specs=[VMEM] * (4 * n), out_specs=[VMEM] * (3 * n),
        out_shape=shapes * 3,
        compiler_params=_params(),
    )(*ws, *gs, *ms, *vs)
    return outs[:n], outs[n:2 * n], outs[2 * n:]


SMALL_ROWS = 40
PACK_ROWS = 64


def kernel(x, meta_tokens, pre_mix_norm, w_in, conv_a_w, conv_b_w, conv_b_bias, ln_b_gain, ln_b_bias, w_out, post_mix_norm, pre_ffn_norm, w_gate, w_up, w_down, post_ffn_norm, loss_target, m_meta_tokens, m_pre_mix_norm, m_w_in, m_conv_a_w, m_conv_b_w, m_conv_b_bias, m_ln_b_gain, m_ln_b_bias, m_w_out, m_post_mix_norm, m_pre_ffn_norm, m_w_gate, m_w_up, m_w_down, m_post_ffn_norm, v_meta_tokens, v_pre_mix_norm, v_w_in, v_conv_a_w, v_conv_b_w, v_conv_b_bias, v_ln_b_gain, v_ln_b_bias, v_w_out, v_post_mix_norm, v_pre_ffn_norm, v_w_gate, v_w_up, v_w_down, v_post_ffn_norm):
    xq, yq, cq = lax.axis_index("x"), lax.axis_index("y"), lax.axis_index("c")
    chip = 2 * xq + yq
    c_arr = jnp.reshape(cq, (1,)).astype(jnp.int32)
    qc_arr = jnp.stack([chip, cq]).astype(jnp.int32)

    seq, d = x.shape[1], x.shape[2]
    x2, tgt2 = x[0], loss_target[0]
    tr = lambda a: jnp.swapaxes(a, 1, 2)[0]
    w_in2, w_out2, w_gate2, w_up2, w_down2 = w_in[0], w_out[0], tr(w_gate), tr(w_up), w_down[0]
    ka, wa_sh = conv_a_w.shape[1], conv_a_w.shape[2]
    kb = conv_b_w.shape[1]
    meta_sh = meta_tokens.shape[1]

    small = jnp.zeros((PACK_ROWS, meta_sh), F32)
    small = small.at[0:N_META, :].set(meta_tokens)
    small = small.at[16:16 + ka, 0:wa_sh].set(conv_a_w[0])
    small = small.at[24:24 + kb, 0:wa_sh].set(conv_b_w[0])
    win4, small4 = _gather_shards([w_in2.astype(BF16), small])
    rest = [w_out2.astype(BF16), w_gate2.astype(BF16), w_up2.astype(BF16), w_down2.astype(BF16)]
    g_ssem, g_rsem, rest, lands, g_token = _gather_start(rest, win4)
    meta_f = jnp.concatenate([small4[k, 0:N_META, :] for k in range(N_CHIPS)], axis=1)
    wa_f = jnp.concatenate([small4[k, 16:16 + ka, 0:wa_sh] for k in range(N_CHIPS)], axis=1)
    wb_f = jnp.concatenate([small4[k, 24:24 + kb, 0:wa_sh] for k in range(N_CHIPS)], axis=1)

    tail = jnp.zeros((TAIL_ROWS - N_META, d), F32)
    h = jnp.concatenate([x2, meta_f, tail], axis=0)
    tgt = jnp.concatenate([tgt2, jnp.zeros((TAIL_ROWS, d), F32)], axis=0)
    xn1, hp5 = _mm_in(h, win4, pre_mix_norm + g_token[0:1, 0:1])
    ya, z = _mix_conv_fwd(hp5, wa_f, wb_f, conv_b_bias)
    loc_o, land_o = _gather_wait([0], g_ssem, g_rsem, rest[0:1], lands[0:1], z, "out")
    (wout4,) = _forward_pair(loc_o, land_o, "out")
    wout_f = wout4.reshape(N_CHIPS * wout4.shape[1], wout4.shape[2])
    yb, mix, h1, xn2 = _mm_out(ya, z, h, wout_f, ln_b_gain, ln_b_bias, post_mix_norm, pre_ffn_norm)
    loc_gu, land_gu = _gather_wait([1, 2], g_ssem, g_rsem, rest[1:3], lands[1:3], xn2, "gate_up")
    wg4, wu4 = _forward_pair(loc_gu, land_gu, "gate_up")
    a4, u4, f4 = _ffn_up(xn2, wg4, wu4)
    loc_d, land_d = _gather_wait([3], g_ssem, g_rsem, rest[3:4], lands[3:4], f4, "down")
    (wd4,) = _forward_pair(loc_d, land_d, "down")
    dff, dh2, loss_blk, d_gpf = _ffn_down(f4, wd4, h1, tgt, post_ffn_norm)

    da4, du4 = _ffn_bwd_act(dff, wd4, a4, u4)
    g_down = _grad_w_down(f4, dff)
    g_gate, g_up = _grad_w_gate_up(xn2, da4, du4)
    ffn = [g_gate, g_up, g_down]
    got = _pair_exchange_grads(ffn, [1, 1, 1], "ffn")
    parts = [_pair_sum(ffn[i], got[i], c_arr, False, ("gate", "up", "down")[i]) for i in range(3)]
    f_ssem, f_rsem, parts, f_lands, f_token = _chip_exchange_start(parts, dff, "ffn")
    dh1, dmix, d_g2, d_gpm = _ffn_bwd_in(da4, du4, wg4, wu4, h1, mix, dh2, pre_ffn_norm + f_token[0:1, 0:1], post_mix_norm)
    g_out = _grad_w_out(ya, yb, dmix)
    dya, dz, d_lg, d_lb = _mix_bwd_out(dmix, wout_f, z, ln_b_gain, ln_b_bias)
    dhp5, d_wa, d_wb, d_bb = _mix_conv_bwd(hp5, dya, dz, wa_f, wb_f)
    g_in = _grad_w_in(xn1, dhp5)
    dh, d_g1 = _mix_bwd_in(dhp5, win4, h, dh1, pre_mix_norm)
    grad_x = dh[0:seq][None]
    d_meta = dh[seq:seq + N_META]

    g_out4 = g_out.reshape(N_CHIPS, g_out.shape[0] // N_CHIPS, g_out.shape[1])
    mixw = [g_in, g_out4]
    got2 = _pair_exchange_grads(mixw, [0, 1], "mix")
    parts2 = [_pair_sum(mixw[i], got2[i], c_arr, i == 0, ("in", "out")[i]) for i in range(2)]
    m_ssem, m_rsem, parts2, m_lands, m_token = _chip_exchange_start(parts2, dh, "mix")

    parts, f_recv = _chip_exchange_wait(f_ssem, f_rsem, parts, f_lands, m_token, "ffn")
    halves = [_chip_sum(parts[i], f_recv[i], qc_arr, ("gate", "up", "down")[i]) for i in range(3)]
    gsum_ffn = _pair_share_grads(halves, "ffn")

    hw = d // 2
    assert d_wa.shape == (3, hw) and d_wb.shape == (31, hw) and d_bb.shape == (1, hw)
    small_parts = [d_meta, d_g1, d_gpm, d_g2, d_gpf, d_bb, d_lg, d_lb, loss_blk[0:1, :], d_wa, d_wb]
    places = [[(0, 0, 0, N_META)], [(16, 0, 0, 1)], [(17, 0, 0, 1)], [(18, 0, 0, 1)], [(19, 0, 0, 1)],
              [(20, 0, 0, 1)], [(20, hw, 0, 1)], [(21, 0, 0, 1)], [(21, hw, 0, 1)], [(22, 0, 0, 3)],
              [(22, hw, 0, 3), (25, 0, 3, 14), (25, hw, 17, 14)]]
    red = _small_allreduce(small_parts, places, SMALL_ROWS, d)
    s_meta, s_g1, s_gpm, s_g2, s_gpf = red[0:N_META], red[16:17], red[17:18], red[18:19], red[19:20]
    s_bb, s_lg, s_lb, s_loss = red[20:21, 0:hw], red[20:21, hw:d], red[21:22, 0:hw], red[21:22, hw:hw + LANES]
    s_wa = red[22:25, 0:hw]
    s_wb = jnp.concatenate([red[22:25, hw:d], red[25:39, 0:hw], red[25:39, hw:d]], axis=0)
    g_meta = lax.dynamic_slice_in_dim(s_meta, chip * meta_sh, meta_sh, axis=1)
    g_wa = lax.dynamic_slice_in_dim(s_wa, chip * wa_sh, wa_sh, axis=1)[None]
    g_wb = lax.dynamic_slice_in_dim(s_wb, chip * wa_sh, wa_sh, axis=1)[None]
    loss = s_loss[0, 0]

    names_big = ["w_in", "w_out", "w_gate", "w_up", "w_down"]
    w_big = dict(zip(names_big, [w_in2, w_out2, w_gate2, w_up2, w_down2]))
    m_big = dict(zip(names_big, [m_w_in[0], m_w_out[0], tr(m_w_gate), tr(m_w_up), m_w_down[0]]))
    v_big = dict(zip(names_big, [v_w_in[0], v_w_out[0], tr(v_w_gate), tr(v_w_up), v_w_down[0]]))
    grads, deltas, new_m, new_v = {}, {}, {}, {}

    def update(nm, g):
        outs = _adamw_big(w_big[nm], g, m_big[nm], v_big[nm], nm)
        if nm in ("w_gate", "w_up"):
            outs = [jnp.swapaxes(o[None], 1, 2) for o in outs]
        else:
            outs = [o[None] for o in outs]
        grads[nm], deltas[nm], new_m[nm], new_v[nm] = outs
        return outs[1]

    last = None
    for nm, g in zip(["w_gate", "w_up", "w_down"], gsum_ffn):
        last = update(nm, g)

    parts2, m_recv = _chip_exchange_wait(m_ssem, m_rsem, parts2, m_lands, last, "mix")
    halves2 = [_chip_sum(parts2[i], m_recv[i], qc_arr, ("in", "out")[i]) for i in range(2)]
    gsum_mix = _pair_share_grads(halves2, "mix")
    for nm, g in zip(["w_in", "w_out"], gsum_mix):
        update(nm, g)

    names_small = ["meta_tokens", "pre_mix_norm", "conv_a_w", "conv_b_w", "conv_b_bias", "ln_b_gain", "ln_b_bias",
                   "post_mix_norm", "pre_ffn_norm", "post_ffn_norm"]
    w_small = [meta_tokens, pre_mix_norm, conv_a_w[0], conv_b_w[0], conv_b_bias, ln_b_gain, ln_b_bias, post_mix_norm,
               pre_ffn_norm, post_ffn_norm]
    g_small = [g_meta, s_g1, g_wa[0], g_wb[0], s_bb, s_lg, s_lb, s_gpm, s_g2, s_gpf]
    m_small = [m_meta_tokens, m_pre_mix_norm, m_conv_a_w[0], m_conv_b_w[0], m_conv_b_bias, m_ln_b_gain, m_ln_b_bias,
               m_post_mix_norm, m_pre_ffn_norm, m_post_ffn_norm]
    v_small = [v_meta_tokens, v_pre_mix_norm, v_conv_a_w[0], v_conv_b_w[0], v_conv_b_bias, v_ln_b_gain, v_ln_b_bias,
               v_post_mix_norm, v_pre_ffn_norm, v_post_ffn_norm]
    d_s, m_s, v_s = _adamw_small(w_small, g_small, m_small, v_small)
    for i, nm in enumerate(names_small):
        lead = nm in ("conv_a_w", "conv_b_w")
        fix = (lambda a: a[None]) if lead else (lambda a: a)
        grads[nm], deltas[nm], new_m[nm], new_v[nm] = fix(g_small[i]), fix(d_s[i]), fix(m_s[i]), fix(v_s[i])

    order = ["meta_tokens", "pre_mix_norm", "w_in", "conv_a_w", "conv_b_w", "conv_b_bias", "ln_b_gain", "ln_b_bias", "w_out",
             "post_mix_norm", "pre_ffn_norm", "w_gate", "w_up", "w_down", "post_ffn_norm"]
    return (loss, grad_x, *[grads[k] for k in order], *[deltas[k] for k in order], *[new_m[k] for k in order],
            *[new_v[k] for k in order])
```

```python
import functools

import jax
import jax.numpy as jnp
from jax import lax
from jax.experimental import pallas as pl
from jax.experimental.pallas import tpu as pltpu

F32 = jnp.float32
BF16 = jnp.bfloat16
MESH = pl.DeviceIdType.MESH

N_META = 16
TAIL_ROWS = 128
RMS_EPS = 1e-6
LN_EPS = 1e-5
ADAM_LR = 0.001
ADAM_B1 = 0.9
ADAM_B2 = 0.999
ADAM_EPS = 1e-08
ADAM_WD = 0.01
ADAM_STEP = 10

N_CHIPS = 4
LANES = 128
CONV_CHUNK = 48
CONV_HIST = 32
ROW_TILE_CAP = 640
VMEM_LIMIT = 56 * 1024 * 1024

NN = (((1,), (0,)), ((), ()))
NT = (((1,), (1,)), ((), ()))
TN = (((0,), (0,)), ((), ()))


def _dot(a, b, dims=NN):
    return lax.dot_general(a, b, dims, preferred_element_type=F32)


def _sig(v):
    return 1.0 / (1.0 + jnp.exp(-v))


def _mean(v):
    return jnp.mean(v, axis=-1, keepdims=True)


def _row_tile(rows):
    best = 16
    for t in range(16, min(rows, ROW_TILE_CAP) + 1, 16):
        if rows % t == 0:
            best = t
    assert rows % best == 0
    return best


def _pieces(n_shards, shard_w, piece_w):
    total = n_shards * shard_w
    cuts = sorted(set(range(0, total + 1, shard_w)) | set(range(0, total + 1, piece_w)))
    out = []
    for lo, hi in zip(cuts[:-1], cuts[1:]):
        out.append((lo // shard_w, lo % shard_w, lo // piece_w, lo % piece_w, hi - lo))
    return out


def _params(semantics=None):
    kw = dict(vmem_limit_bytes=VMEM_LIMIT)
    if semantics is not None:
        kw["dimension_semantics"] = semantics
    return pltpu.CompilerParams(**kw)


def _full(shape):
    nd = len(shape)
    return pl.BlockSpec(shape, lambda *_: (0,) * nd)


def _sds(shape, dtype):
    return jax.ShapeDtypeStruct(shape, dtype)


ANY = pl.BlockSpec(memory_space=pl.ANY)
VMEM = pl.BlockSpec(memory_space=pltpu.VMEM)


def _mesh_pos():
    return lax.axis_index("x"), lax.axis_index("y"), lax.axis_index("c")


def _flip(v, bit):
    return 1 - v if bit else v


def _mm_in(h, win4, g1):
    tp, d = h.shape
    tm = _row_tile(tp)
    n_sh, _, csh = win4.shape
    pw = n_sh * csh // 5
    pieces = _pieces(n_sh, csh, pw)

    def body(h_ref, w_ref, g_ref, xn_ref, hp_ref):
        hh = h_ref[...]
        r = lax.rsqrt(_mean(hh * hh) + RMS_EPS)
        xn = (hh * r * g_ref[...]).astype(BF16)
        xn_ref[...] = xn
        for k, klo, p, plo, w in pieces:
            hp_ref[p, :, plo:plo + w] = _dot(xn, w_ref[k, :, klo:klo + w])

    return pl.pallas_call(
        body, name="mm_in", grid=(tp // tm,),
        in_specs=[pl.BlockSpec((tm, d), lambda i: (i, 0)), _full(win4.shape), _full(g1.shape)],
        out_specs=[pl.BlockSpec((tm, d), lambda i: (i, 0)), pl.BlockSpec((5, tm, pw), lambda i: (0, i, 0))],
        out_shape=[_sds((tp, d), BF16), _sds((5, tp, pw), F32)],
        compiler_params=_params(("arbitrary",)),
    )(h, win4, g1)


def _seq_rows(tp):
    seq = tp - TAIL_ROWS
    nseq = seq + N_META
    assert nseq % CONV_CHUNK == 0 and seq % 16 == 0
    return seq, nseq


def _conv_taps(src_ref, w_ref, dst_ref, width, nseq, transpose):
    w = w_ref[...]

    def step(n, carry):
        out0 = pl.multiple_of(CONV_HIST + n * CONV_CHUNK, 8)
        win0 = out0 if transpose else pl.multiple_of(n * CONV_CHUNK, 8)
        win = src_ref[pl.ds(win0, CONV_CHUNK + CONV_HIST), :]
        acc = jnp.zeros((CONV_CHUNK, w.shape[1]), F32)
        for k in range(width):
            off = (width - 1 - k) if transpose else (CONV_HIST - (width - 1) + k)
            acc = acc + w[k:k + 1, :] * win[off:off + CONV_CHUNK, :]
        dst_ref[pl.ds(out0, CONV_CHUNK), :] = acc
        return carry

    lax.fori_loop(0, nseq // CONV_CHUNK, step, 0)


def _conv_wgrad(src_ref, dz_ref, acc_ref, width, nseq):
    acc_ref[...] = jnp.zeros(acc_ref.shape, F32)

    def step(n, carry):
        win = src_ref[pl.ds(pl.multiple_of(n * CONV_CHUNK, 8), CONV_CHUNK + CONV_HIST), :]
        dzc = dz_ref[pl.ds(pl.multiple_of(CONV_HIST + n * CONV_CHUNK, 8), CONV_CHUNK), :]
        for k in range(width):
            off = CONV_HIST - (width - 1) + k
            prod = dzc * win[off:off + CONV_CHUNK, :]
            part = prod[0:8, :]
            for s in range(1, CONV_CHUNK // 8):
                part = part + prod[8 * s:8 * s + 8, :]
            acc_ref[8 * k:8 * k + 8, :] += part
        return carry

    lax.fori_loop(0, nseq // CONV_CHUNK, step, 0)


def _to_seq(buf_ref, x_part, meta_part, seq):
    buf_ref[CONV_HIST:CONV_HIST + N_META, :] = meta_part
    buf_ref[CONV_HIST + N_META:CONV_HIST + N_META + seq, :] = x_part


def _zero_ends(buf_ref, nseq):
    zeros = jnp.zeros((CONV_HIST, buf_ref.shape[1]), F32)
    buf_ref[0:CONV_HIST, :] = zeros
    buf_ref[CONV_HIST + nseq:CONV_HIST + nseq + CONV_HIST, :] = zeros


def _mix_conv_fwd(hp5, wa, wb, bb):
    _, tp, wgrp = hp5.shape
    seq, nseq = _seq_rows(tp)
    sb = nseq + 2 * CONV_HIST
    ka, kb = wa.shape[0], wb.shape[0]
    xs, ms = slice(0, seq), slice(seq, seq + N_META)
    ox, om = slice(CONV_HIST + N_META, CONV_HIST + nseq), slice(CONV_HIST, CONV_HIST + N_META)

    def body(hp_ref, wa_ref, wb_ref, bb_ref, ya_ref, z_ref, s_ref, o_ref):
        _zero_ends(s_ref, nseq)
        _to_seq(s_ref, hp_ref[1, xs, :] * hp_ref[2, xs, :], hp_ref[1, ms, :] * hp_ref[2, ms, :], seq)
        _conv_taps(s_ref, wa_ref, o_ref, ka, nseq, False)
        ya_ref[xs, :] = (hp_ref[0, xs, :] * o_ref[ox, :]).astype(BF16)
        ya_ref[ms, :] = (hp_ref[0, ms, :] * o_ref[om, :]).astype(BF16)
        ya_ref[seq + N_META:tp, :] = jnp.zeros((tp - seq - N_META, LANES), BF16)
        _to_seq(s_ref, hp_ref[3, xs, :] * _sig(hp_ref[4, xs, :]), hp_ref[3, ms, :] * _sig(hp_ref[4, ms, :]), seq)
        _conv_taps(s_ref, wb_ref, o_ref, kb, nseq, False)
        z_ref[xs, :] = o_ref[ox, :] + bb_ref[...]
        z_ref[ms, :] = o_ref[om, :] + bb_ref[...]
        z_ref[seq + N_META:tp, :] = jnp.zeros((tp - seq - N_META, LANES), F32)

    col = lambda j: (0, j)
    return pl.pallas_call(
        body, name="mix_conv_fwd", grid=(wgrp // LANES,),
        in_specs=[pl.BlockSpec((5, tp, LANES), lambda j: (0, 0, j)), pl.BlockSpec((ka, LANES), col),
                  pl.BlockSpec((kb, LANES), col), pl.BlockSpec((1, LANES), col)],
        out_specs=[pl.BlockSpec((tp, LANES), col), pl.BlockSpec((tp, LANES), col)],
        out_shape=[_sds((tp, wgrp), BF16), _sds((tp, wgrp), F32)],
        scratch_shapes=[pltpu.VMEM((sb, LANES), F32), pltpu.VMEM((sb, LANES), F32)],
        compiler_params=_params(("arbitrary",)),
    )(hp5, wa, wb, bb)


def _layer_norm_parts(z, lg, lb):
    mu = _mean(z)
    zc = z - mu
    rl = lax.rsqrt(_mean(zc * zc) + LN_EPS)
    zh = zc * rl
    return rl, zh, zh * lg + lb


def _mm_out(ya, z, h, wout, lg, lb, gpm, g2):
    tp, d = h.shape
    wa_ = ya.shape[1]
    tm = _row_tile(tp)

    def body(ya_ref, z_ref, h_ref, w_ref, lg_ref, lb_ref, gpm_ref, g2_ref, yb_ref, mix_ref, h1_ref, xn2_ref):
        _, _, l = _layer_norm_parts(z_ref[...], lg_ref[...], lb_ref[...])
        yb = (l * _sig(l)).astype(BF16)
        yb_ref[...] = yb
        mix = _dot(ya_ref[...], w_ref[0:wa_, :]) + _dot(yb, w_ref[wa_:d, :])
        mix_ref[...] = mix
        rm = lax.rsqrt(_mean(mix * mix) + RMS_EPS)
        h1 = h_ref[...] + mix * rm * gpm_ref[...]
        h1_ref[...] = h1
        r2 = lax.rsqrt(_mean(h1 * h1) + RMS_EPS)
        xn2_ref[...] = (h1 * r2 * g2_ref[...]).astype(BF16)

    row = lambda i: (i, 0)
    return pl.pallas_call(
        body, name="mm_out", grid=(tp // tm,),
        in_specs=[pl.BlockSpec((tm, wa_), row), pl.BlockSpec((tm, wa_), row), pl.BlockSpec((tm, d), row),
                  _full(wout.shape), _full(lg.shape), _full(lb.shape), _full(gpm.shape), _full(g2.shape)],
        out_specs=[pl.BlockSpec((tm, wa_), row), pl.BlockSpec((tm, d), row), pl.BlockSpec((tm, d), row),
                   pl.BlockSpec((tm, d), row)],
        out_shape=[_sds((tp, wa_), BF16), _sds((tp, d), F32), _sds((tp, d), F32), _sds((tp, d), BF16)],
        compiler_params=_params(("arbitrary",)),
    )(ya, z, h, wout, lg, lb, gpm, g2)


def _ffn_up(xn2, wg4, wu4):
    tp, d = xn2.shape
    n_sh, fs, _ = wg4.shape
    tm = _row_tile(tp)

    def body(xn_ref, wg_ref, wu_ref, a_ref, u_ref, f_ref):
        xn = xn_ref[...]
        a = _dot(xn, wg_ref[0], NT)
        u = _dot(xn, wu_ref[0], NT)
        a_ref[0] = a.astype(BF16)
        u_ref[0] = u.astype(BF16)
        f_ref[0] = (a * _sig(a) * u).astype(BF16)

    wspec = pl.BlockSpec((1, fs, d), lambda k, i: (k, 0, 0))
    ospec = pl.BlockSpec((1, tm, fs), lambda k, i: (k, i, 0))
    return pl.pallas_call(
        body, name="ffn_up", grid=(n_sh, tp // tm),
        in_specs=[pl.BlockSpec((tm, d), lambda k, i: (i, 0)), wspec, wspec],
        out_specs=[ospec, ospec, ospec],
        out_shape=[_sds((n_sh, tp, fs), BF16)] * 3,
        compiler_params=_params(("arbitrary", "arbitrary")),
    )(xn2, wg4, wu4)


def _ffn_down(f4, wd4, h1, tgt, gpf):
    n_sh, tp, fs = f4.shape
    d = h1.shape[1]
    tm = _row_tile(tp)
    seq, _ = _seq_rows(tp)

    def body(f_ref, w_ref, h1_ref, t_ref, gpf_ref, dff_ref, dh2_ref, loss_ref, dgpf_ref):
        i = pl.program_id(0)
        ff = _dot(f_ref[0], w_ref[0])
        for k in range(1, n_sh):
            ff = ff + _dot(f_ref[k], w_ref[k])
        rf = lax.rsqrt(_mean(ff * ff) + RMS_EPS)
        nf = ff * rf
        gpf_ = gpf_ref[...]
        h2 = h1_ref[...] + nf * gpf_
        rows = i * tm + lax.broadcasted_iota(jnp.int32, (tm, 1), 0)
        err = jnp.where(rows < seq, h2 - t_ref[...], 0.0)
        dh2 = err * (1.0 / d)
        dh2_ref[...] = dh2
        dn = dh2 * gpf_
        dff_ref[...] = (rf * (dn - nf * _mean(dn * nf))).astype(BF16)

        @pl.when(i == 0)
        def _():
            loss_ref[...] = jnp.zeros(loss_ref.shape, F32)
            dgpf_ref[...] = jnp.zeros(dgpf_ref.shape, F32)

        loss_ref[...] += (0.5 / d) * jnp.sum(err * err, axis=(0, 1), keepdims=True)
        dgpf_ref[...] += jnp.sum(dh2 * nf, axis=0, keepdims=True)

    row = lambda i: (i, 0)
    return pl.pallas_call(
        body, name="ffn_down", grid=(tp // tm,),
        in_specs=[pl.BlockSpec((n_sh, tm, fs), lambda i: (0, i, 0)), _full(wd4.shape), pl.BlockSpec((tm, d), row),
                  pl.BlockSpec((tm, d), row), _full(gpf.shape)],
        out_specs=[pl.BlockSpec((tm, d), row), pl.BlockSpec((tm, d), row), _full((8, LANES)), _full((1, d))],
        out_shape=[_sds((tp, d), BF16), _sds((tp, d), F32), _sds((8, LANES), F32), _sds((1, d), F32)],
        compiler_params=_params(("arbitrary",)),
    )(f4, wd4, h1, tgt, gpf)


def _ffn_bwd_act(dff, wd4, a4, u4):
    tp, d = dff.shape
    n_sh, fs, _ = wd4.shape
    tm = _row_tile(tp)

    def body(dff_ref, w_ref, a_ref, u_ref, da_ref, du_ref):
        df = _dot(dff_ref[...], w_ref[0], NT)
        a = a_ref[0].astype(F32)
        u = u_ref[0].astype(F32)
        s = _sig(a)
        da_ref[0] = (df * u * (s * (1.0 + a * (1.0 - s)))).astype(BF16)
        du_ref[0] = (df * a * s).astype(BF16)

    aspec = pl.BlockSpec((1, tm, fs), lambda k, i: (k, i, 0))
    return pl.pallas_call(
        body, name="ffn_bwd_act", grid=(n_sh, tp // tm),
        in_specs=[pl.BlockSpec((tm, d), lambda k, i: (i, 0)), pl.BlockSpec((1, fs, d), lambda k, i: (k, 0, 0)),
                  aspec, aspec],
        out_specs=[aspec, aspec],
        out_shape=[_sds((n_sh, tp, fs), BF16)] * 2,
        compiler_params=_params(("arbitrary", "arbitrary")),
    )(dff, wd4, a4, u4)


def _grad_w_down(f4, dff):
    n_sh, tp, fs = f4.shape
    d = dff.shape[1]

    def body(f_ref, dff_ref, g_ref):
        g_ref[0] = _dot(f_ref[0], dff_ref[...], TN).astype(BF16)

    return pl.pallas_call(
        body, name="grad_w_down", grid=(n_sh,),
        in_specs=[pl.BlockSpec((1, tp, fs), lambda k: (k, 0, 0)), _full(dff.shape)],
        out_specs=pl.BlockSpec((1, fs, d), lambda k: (k, 0, 0)),
        out_shape=_sds((n_sh, fs, d), BF16),
        compiler_params=_params(("arbitrary",)),
    )(f4, dff)


def _grad_w_gate_up(xn2, da4, du4):
    n_sh, tp, fs = da4.shape
    d = xn2.shape[1]

    def body(xn_ref, da_ref, du_ref, gg_ref, gu_ref):
        xn = xn_ref[...]
        gg_ref[0] = _dot(da_ref[0], xn, TN).astype(BF16)
        gu_ref[0] = _dot(du_ref[0], xn, TN).astype(BF16)

    aspec = pl.BlockSpec((1, tp, fs), lambda k: (k, 0, 0))
    gspec = pl.BlockSpec((1, fs, d), lambda k: (k, 0, 0))
    return pl.pallas_call(
        body, name="grad_w_gate_up", grid=(n_sh,),
        in_specs=[_full(xn2.shape), aspec, aspec],
        out_specs=[gspec, gspec],
        out_shape=[_sds((n_sh, fs, d), BF16)] * 2,
        compiler_params=_params(("arbitrary",)),
    )(xn2, da4, du4)


def _rms_bwd(dy, x, r, g):
    n = x * r
    dn = dy * g
    return r * (dn - n * _mean(dn * n)), dy * n


def _ffn_bwd_in(da4, du4, wg4, wu4, h1, mix, dh2, g2, gpm):
    n_sh, tp, fs = da4.shape
    d = h1.shape[1]
    tm = _row_tile(tp)

    def body(da_ref, du_ref, wg_ref, wu_ref, h1_ref, mix_ref, dh2_ref, g2_ref, gpm_ref,
             dh1_ref, dmix_ref, dg2_ref, dgpm_ref, acc_ref):
        i, k = pl.program_id(0), pl.program_id(1)
        part = _dot(da_ref[0], wg_ref[0]) + _dot(du_ref[0], wu_ref[0])

        @pl.when(k == 0)
        def _():
            acc_ref[...] = part

        @pl.when(k > 0)
        def _():
            acc_ref[...] += part

        @pl.when((i == 0) & (k == 0))
        def _():
            dg2_ref[...] = jnp.zeros(dg2_ref.shape, F32)
            dgpm_ref[...] = jnp.zeros(dgpm_ref.shape, F32)

        @pl.when(k == n_sh - 1)
        def _():
            h1v = h1_ref[...]
            r2 = lax.rsqrt(_mean(h1v * h1v) + RMS_EPS)
            dres, dg2_rows = _rms_bwd(acc_ref[...], h1v, r2, g2_ref[...])
            dh1 = dh2_ref[...] + dres
            dh1_ref[...] = dh1
            mixv = mix_ref[...]
            rm = lax.rsqrt(_mean(mixv * mixv) + RMS_EPS)
            dmix, dgpm_rows = _rms_bwd(dh1, mixv, rm, gpm_ref[...])
            dmix_ref[...] = dmix.astype(BF16)
            dg2_ref[...] += jnp.sum(dg2_rows, axis=0, keepdims=True)
            dgpm_ref[...] += jnp.sum(dgpm_rows, axis=0, keepdims=True)

    aspec = pl.BlockSpec((1, tm, fs), lambda i, k: (k, i, 0))
    wspec = pl.BlockSpec((1, fs, d), lambda i, k: (k, 0, 0))
    row = pl.BlockSpec((tm, d), lambda i, k: (i, 0))
    return pl.pallas_call(
        body, name="ffn_bwd_in", grid=(tp // tm, n_sh),
        in_specs=[aspec, aspec, wspec, wspec, row, row, row, _full(g2.shape), _full(gpm.shape)],
        out_specs=[row, row, _full((1, d)), _full((1, d))],
        out_shape=[_sds((tp, d), F32), _sds((tp, d), BF16), _sds((1, d), F32), _sds((1, d), F32)],
        scratch_shapes=[pltpu.VMEM((tm, d), F32)],
        compiler_params=_params(("arbitrary", "arbitrary")),
    )(da4, du4, wg4, wu4, h1, mix, dh2, g2, gpm)


def _grad_w_out(ya, yb, dmix):
    tp, wa_ = ya.shape
    d = dmix.shape[1]

    def body(ya_ref, yb_ref, dmix_ref, g_ref):
        dm = dmix_ref[...]
        g_ref[0:wa_, :] = _dot(ya_ref[...], dm, TN).astype(BF16)
        g_ref[wa_:2 * wa_, :] = _dot(yb_ref[...], dm, TN).astype(BF16)

    return pl.pallas_call(
        body, name="grad_w_out", grid=(1,),
        in_specs=[_full(ya.shape), _full(yb.shape), _full(dmix.shape)],
        out_specs=_full((2 * wa_, d)),
        out_shape=_sds((2 * wa_, d), BF16),
        compiler_params=_params(("arbitrary",)),
    )(ya, yb, dmix)


def _mix_bwd_out(dmix, wout, z, lg, lb):
    tp, d = dmix.shape
    wa_ = z.shape[1]
    tm = _row_tile(tp)

    def body(dmix_ref, w_ref, z_ref, lg_ref, lb_ref, dya_ref, dz_ref, dlg_ref, dlb_ref):
        i = pl.program_id(0)
        dm = dmix_ref[...]
        dya_ref[...] = _dot(dm, w_ref[0:wa_, :], NT)
        dyb = _dot(dm, w_ref[wa_:d, :], NT)
        lg_ = lg_ref[...]
        rl, zh, l = _layer_norm_parts(z_ref[...], lg_, lb_ref[...])
        sl = _sig(l)
        dl = dyb * (sl * (1.0 + l * (1.0 - sl)))
        dzh = dl * lg_
        dz_ref[...] = rl * (dzh - _mean(dzh) - zh * _mean(dzh * zh))

        @pl.when(i == 0)
        def _():
            dlg_ref[...] = jnp.zeros(dlg_ref.shape, F32)
            dlb_ref[...] = jnp.zeros(dlb_ref.shape, F32)

        dlg_ref[...] += jnp.sum(dl * zh, axis=0, keepdims=True)
        dlb_ref[...] += jnp.sum(dl, axis=0, keepdims=True)

    row = lambda i: (i, 0)
    return pl.pallas_call(
        body, name="mix_bwd_out", grid=(tp // tm,),
        in_specs=[pl.BlockSpec((tm, d), row), _full(wout.shape), pl.BlockSpec((tm, wa_), row), _full(lg.shape),
                  _full(lb.shape)],
        out_specs=[pl.BlockSpec((tm, wa_), row), pl.BlockSpec((tm, wa_), row), _full((1, wa_)), _full((1, wa_))],
        out_shape=[_sds((tp, wa_), F32), _sds((tp, wa_), F32), _sds((1, wa_), F32), _sds((1, wa_), F32)],
        compiler_params=_params(("arbitrary",)),
    )(dmix, wout, z, lg, lb)


def _mix_conv_bwd(hp5, dya, dz, wa, wb):
    _, tp, wgrp = hp5.shape
    seq, nseq = _seq_rows(tp)
    sb = nseq + 2 * CONV_HIST
    ka, kb = wa.shape[0], wb.shape[0]
    xs, ms = slice(0, seq), slice(seq, seq + N_META)
    ox, om = slice(CONV_HIST + N_META, CONV_HIST + nseq), slice(CONV_HIST, CONV_HIST + N_META)
    n_tail = tp - seq - N_META

    def body(hp_ref, dya_ref, dz_ref, wa_ref, wb_ref, dhp_ref, dwa_ref, dwb_ref, dbb_ref, s_ref, d_ref, o_ref, acc_ref):
        _zero_ends(s_ref, nseq)
        _zero_ends(d_ref, nseq)

        def put(p, ox_val, om_val):
            dhp_ref[p, xs, :] = ox_val.astype(BF16)
            dhp_ref[p, ms, :] = om_val.astype(BF16)
            dhp_ref[p, seq + N_META:tp, :] = jnp.zeros((n_tail, LANES), BF16)

        def wgrad(dw_ref, width):
            for k in range(width):
                dw_ref[k:k + 1, :] = jnp.sum(acc_ref[8 * k:8 * k + 8, :], axis=0, keepdims=True)

        _to_seq(s_ref, hp_ref[1, xs, :] * hp_ref[2, xs, :], hp_ref[1, ms, :] * hp_ref[2, ms, :], seq)
        _conv_taps(s_ref, wa_ref, o_ref, ka, nseq, False)
        put(0, dya_ref[xs, :] * o_ref[ox, :], dya_ref[ms, :] * o_ref[om, :])
        _to_seq(d_ref, dya_ref[xs, :] * hp_ref[0, xs, :], dya_ref[ms, :] * hp_ref[0, ms, :], seq)
        _conv_wgrad(s_ref, d_ref, acc_ref, ka, nseq)
        wgrad(dwa_ref, ka)
        _conv_taps(d_ref, wa_ref, o_ref, ka, nseq, True)
        put(1, o_ref[ox, :] * hp_ref[2, xs, :], o_ref[om, :] * hp_ref[2, ms, :])
        put(2, o_ref[ox, :] * hp_ref[1, xs, :], o_ref[om, :] * hp_ref[1, ms, :])

        _to_seq(s_ref, hp_ref[3, xs, :] * _sig(hp_ref[4, xs, :]), hp_ref[3, ms, :] * _sig(hp_ref[4, ms, :]), seq)
        _to_seq(d_ref, dz_ref[xs, :], dz_ref[ms, :], seq)
        dbb_ref[...] = (jnp.sum(dz_ref[xs, :], axis=0, keepdims=True)
                        + jnp.sum(dz_ref[ms, :], axis=0, keepdims=True))
        _conv_wgrad(s_ref, d_ref, acc_ref, kb, nseq)
        wgrad(dwb_ref, kb)
        _conv_taps(d_ref, wb_ref, o_ref, kb, nseq, True)
        sx, sm = _sig(hp_ref[4, xs, :]), _sig(hp_ref[4, ms, :])
        put(3, o_ref[ox, :] * sx, o_ref[om, :] * sm)
        put(4, o_ref[ox, :] * hp_ref[3, xs, :] * sx * (1.0 - sx), o_ref[om, :] * hp_ref[3, ms, :] * sm * (1.0 - sm))

    col = lambda j: (0, j)
    blk5 = pl.BlockSpec((5, tp, LANES), lambda j: (0, 0, j))
    return pl.pallas_call(
        body, name="mix_conv_bwd", grid=(wgrp // LANES,),
        in_specs=[blk5, pl.BlockSpec((tp, LANES), col), pl.BlockSpec((tp, LANES), col),
                  pl.BlockSpec((ka, LANES), col), pl.BlockSpec((kb, LANES), col)],
        out_specs=[blk5, pl.BlockSpec((ka, LANES), col), pl.BlockSpec((kb, LANES), col), pl.BlockSpec((1, LANES), col)],
        out_shape=[_sds((5, tp, wgrp), BF16), _sds((ka, wgrp), F32), _sds((kb, wgrp), F32), _sds((1, wgrp), F32)],
        scratch_shapes=[pltpu.VMEM((sb, LANES), F32), pltpu.VMEM((sb, LANES), F32), pltpu.VMEM((sb, LANES), F32),
                        pltpu.VMEM((8 * kb, LANES), F32)],
        compiler_params=_params(("arbitrary",)),
    )(hp5, dya, dz, wa, wb)


def _grad_w_in(xn1, dhp5):
    n_p, tp, pw = dhp5.shape
    d = xn1.shape[1]

    def body(xn_ref, dhp_ref, g_ref):
        g_ref[...] = _dot(xn_ref[...], dhp_ref[0], TN).astype(BF16)

    return pl.pallas_call(
        body, name="grad_w_in", grid=(n_p,),
        in_specs=[_full(xn1.shape), pl.BlockSpec((1, tp, pw), lambda p: (p, 0, 0))],
        out_specs=pl.BlockSpec((d, pw), lambda p: (0, p)),
        out_shape=_sds((d, n_p * pw), BF16),
        compiler_params=_params(("arbitrary",)),
    )(xn1, dhp5)


def _mix_bwd_in(dhp5, win4, h, dh1, g1):
    n_p, tp, pw = dhp5.shape
    d = h.shape[1]
    n_sh, _, csh = win4.shape
    tm = _row_tile(tp)
    pieces = _pieces(n_sh, csh, pw)

    def body(dhp_ref, w_ref, h_ref, dh1_ref, g_ref, dh_ref, dg1_ref):
        i = pl.program_id(0)
        dxn = None
        for k, klo, p, plo, w in pieces:
            t = _dot(dhp_ref[p, :, plo:plo + w], w_ref[k, :, klo:klo + w], NT)
            dxn = t if dxn is None else dxn + t
        hh = h_ref[...]
        r1 = lax.rsqrt(_mean(hh * hh) + RMS_EPS)
        dres, dg_rows = _rms_bwd(dxn, hh, r1, g_ref[...])
        dh_ref[...] = dh1_ref[...] + dres

        @pl.when(i == 0)
        def _():
            dg1_ref[...] = jnp.zeros(dg1_ref.shape, F32)

        dg1_ref[...] += jnp.sum(dg_rows, axis=0, keepdims=True)

    row = lambda i: (i, 0)
    return pl.pallas_call(
        body, name="mix_bwd_in", grid=(tp // tm,),
        in_specs=[pl.BlockSpec((n_p, tm, pw), lambda i: (0, i, 0)), _full(win4.shape), pl.BlockSpec((tm, d), row),
                  pl.BlockSpec((tm, d), row), _full(g1.shape)],
        out_specs=[pl.BlockSpec((tm, d), row), _full((1, d))],
        out_shape=[_sds((tp, d), F32), _sds((1, d), F32)],
        compiler_params=_params(("arbitrary",)),
    )(dhp5, win4, h, dh1, g1)


def _other_chips(x, y):
    out = []
    for j in (1, 2, 3):
        px, py = _flip(x, j >> 1), _flip(y, j & 1)
        out.append((px, py, 2 * px + py))
    return out


PAIR_COLLECTIVE_ID = 0


def _pair_barrier(x, y, c):
    sem = pltpu.get_barrier_semaphore()
    pl.semaphore_signal(sem, inc=1, device_id=(x, y, 1 - c), device_id_type=MESH)
    pl.semaphore_wait(sem, 1)


def _pair_params():
    return pltpu.CompilerParams(collective_id=PAIR_COLLECTIVE_ID)


def _half_rows(c, rows_half):
    return pl.ds(pl.multiple_of(c * rows_half, 8), rows_half)


def _cast_place(w, q_arr, tag):
    rows, cols = w.shape
    tr = _row_tile(rows)

    def body(q_ref, w_ref, out_ref):
        out_ref[0] = w_ref[...].astype(BF16)

    return pl.pallas_call(
        body, name="cast_place_" + tag,
        grid_spec=pltpu.PrefetchScalarGridSpec(
            num_scalar_prefetch=1, grid=(rows // tr,),
            in_specs=[pl.BlockSpec((tr, cols), lambda i, q: (i, 0))],
            out_specs=pl.BlockSpec((1, tr, cols), lambda i, q: (q[0], i, 0))),
        out_shape=_sds((N_CHIPS, rows, cols), BF16),
        compiler_params=_params(("arbitrary",)),
    )(q_arr, w)


def _gather_shards(fulls):
    n = len(fulls)
    halves = [a.shape[1] // 2 for a in fulls]

    def body(*refs):
        full = refs[n:2 * n]
        ssem, rsem = refs[2 * n:]
        x, y, c = _mesh_pos()
        chips = _other_chips(x, y)

        def remote(i, chip_no, half, to, s):
            part = full[i].at[chip_no, _half_rows(half, halves[i]), :]
            return pltpu.make_async_remote_copy(src_ref=part, dst_ref=part, send_sem=ssem.at[s], recv_sem=rsem.at[s],
                                                device_id=to, device_id_type=MESH)

        first = []
        for i in range(n):
            for j, (px, py, _) in enumerate(chips):
                cp = remote(i, 2 * x + y, c, (px, py, c), 3 * i + j)
                cp.start()
                first.append(cp)
        passed = []
        for i in range(n):
            for j, (_, _, qj) in enumerate(chips):
                remote(i, qj, c, (x, y, c), 3 * i + j).wait_recv()
                cp = remote(i, qj, c, (x, y, 1 - c), 3 * n + 3 * i + j)
                cp.start()
                passed.append(cp)
        for i in range(n):
            for j, (_, _, qj) in enumerate(chips):
                remote(i, qj, 1 - c, (x, y, c), 3 * n + 3 * i + j).wait_recv()
        for cp in first + passed:
            cp.wait_send()

    return pl.pallas_call(
        body, name="gather_shards",
        in_specs=[ANY] * n, out_specs=[ANY] * n,
        out_shape=[_sds(a.shape, a.dtype) for a in fulls],
        input_output_aliases={i: i for i in range(n)},
        scratch_shapes=[pltpu.SemaphoreType.DMA((6 * n,)), pltpu.SemaphoreType.DMA((6 * n,))],
    )(*fulls)


HBM = pl.BlockSpec(memory_space=pltpu.HBM)
SEM = pl.BlockSpec(memory_space=pltpu.SEMAPHORE)
EFFECT = pltpu.SideEffectType.DATAFLOW_SIDE_EFFECTING


def _in_hbm(a):
    return pltpu.with_memory_space_constraint(a, pltpu.HBM)


def _gather_start(fulls, after):
    n = len(fulls)
    halves = [a.shape[1] // 2 for a in fulls]

    def body(*refs):
        land = refs[:n]
        ssem, rsem = refs[n + 1], refs[n + 2]
        token = refs[-1]
        x, y, c = _mesh_pos()
        q = 2 * x + y
        for i in range(n):
            for j, (px, py, _) in enumerate(_other_chips(x, y)):
                mine = land[i].at[q, _half_rows(c, halves[i]), :]
                pltpu.make_async_remote_copy(src_ref=mine, dst_ref=mine, send_sem=ssem.at[3 * i + j],
                                             recv_sem=rsem.at[3 * i + j], device_id=(px, py, c), device_id_type=MESH).start()
        token[...] = jnp.zeros(token.shape, F32)

    outs = pl.pallas_call(
        body, name="gather_start",
        in_specs=[HBM] * n + [ANY], out_specs=[SEM, SEM] + [HBM] * n + [VMEM],
        out_shape=[pltpu.SemaphoreType.DMA((3 * n,)), pltpu.SemaphoreType.DMA((3 * n,))]
        + [pltpu.HBM(a.shape, a.dtype) for a in fulls] + [_sds((8, LANES), F32)],
        input_output_aliases={i: 2 + i for i in range(n)},
        compiler_params=pltpu.CompilerParams(has_side_effects=EFFECT),
    )(*[_in_hbm(a) for a in fulls], after)
    return outs[0], outs[1], list(outs[2:2 + n]), outs[-1]


def _gather_wait(which, ssem, rsem, lands, after, tag):
    m = len(which)
    halves = [a.shape[1] // 2 for a in lands]

    def body(*refs):
        land = refs[:m]
        ssem_, rsem_ = refs[m], refs[m + 1]
        x, y, c = _mesh_pos()
        for t, i in enumerate(which):
            for j, (px, py, qj) in enumerate(_other_chips(x, y)):
                rows = _half_rows(c, halves[t])
                cp = pltpu.make_async_remote_copy(src_ref=land[t].at[2 * x + y, rows, :], dst_ref=land[t].at[qj, rows, :],
                                                  send_sem=ssem_.at[3 * i + j], recv_sem=rsem_.at[3 * i + j],
                                                  device_id=(px, py, c), device_id_type=MESH)
                cp.wait_send()
                cp.wait_recv()

    outs = pl.pallas_call(
        body, name="gather_wait_" + tag,
        in_specs=[HBM] * m + [SEM, SEM, ANY], out_specs=[HBM] * m,
        out_shape=[pltpu.HBM(a.shape, a.dtype) for a in lands],
        input_output_aliases={i: i for i in range(m)},
        compiler_params=pltpu.CompilerParams(has_side_effects=EFFECT),
    )(*lands, ssem, rsem, after)
    return list(outs)


def _forward_pair(lands, tag):
    n = len(lands)
    halves = [a.shape[1] // 2 for a in lands]

    def body(*refs):
        full = refs[n:2 * n]
        ssem, rsem = refs[2 * n:]
        x, y, c = _mesh_pos()
        _pair_barrier(x, y, c)
        cps = []
        for i in range(n):
            for j, (_, _, qj) in enumerate(_other_chips(x, y)):
                part = full[i].at[qj, _half_rows(c, halves[i]), :]
                cp = pltpu.make_async_remote_copy(src_ref=part, dst_ref=part, send_sem=ssem.at[3 * i + j],
                                                  recv_sem=rsem.at[3 * i + j], device_id=(x, y, 1 - c), device_id_type=MESH)
                cp.start()
                cps.append(cp)
        for cp in cps:
            cp.wait()

    return pl.pallas_call(
        body, name="forward_pair_" + tag,
        in_specs=[ANY] * n, out_specs=[ANY] * n,
        out_shape=[_sds(a.shape, a.dtype) for a in lands],
        input_output_aliases={i: i for i in range(n)},
        scratch_shapes=[pltpu.SemaphoreType.DMA((3 * n,)), pltpu.SemaphoreType.DMA((3 * n,))],
        compiler_params=_pair_params(),
    )(*lands)


def _chip_exchange_start(parts, after, tag):
    n = len(parts)

    def body(*refs):
        src, land = refs[:n], refs[n:2 * n]
        ssem, rsem = refs[2 * n + 1], refs[2 * n + 2]
        token = refs[-1]
        x, y, c = _mesh_pos()
        for i in range(n):
            for j, (px, py, qj) in enumerate(_other_chips(x, y)):
                pltpu.make_async_remote_copy(src_ref=src[i].at[qj], dst_ref=land[i].at[j], send_sem=ssem.at[3 * i + j],
                                             recv_sem=rsem.at[3 * i + j], device_id=(px, py, c), device_id_type=MESH).start()
        token[...] = jnp.zeros(token.shape, F32)

    lands = [lax.empty((3,) + a.shape[1:], a.dtype) for a in parts]
    outs = pl.pallas_call(
        body, name="chip_exchange_start_" + tag,
        in_specs=[HBM] * (2 * n) + [ANY], out_specs=[SEM, SEM] + [HBM] * (2 * n) + [VMEM],
        out_shape=[pltpu.SemaphoreType.DMA((3 * n,)), pltpu.SemaphoreType.DMA((3 * n,))]
        + [pltpu.HBM(a.shape, a.dtype) for a in parts] + [pltpu.HBM(a.shape, a.dtype) for a in lands]
        + [_sds((8, LANES), F32)],
        input_output_aliases={i: 2 + i for i in range(2 * n)},
        compiler_params=pltpu.CompilerParams(has_side_effects=EFFECT),
    )(*[_in_hbm(a) for a in parts], *[_in_hbm(a) for a in lands], after)
    return outs[0], outs[1], list(outs[2:2 + n]), list(outs[2 + n:2 + 2 * n]), outs[-1]


def _chip_exchange_wait(ssem, rsem, parts, lands, after, tag):
    n = len(parts)

    def body(*refs):
        src, land = refs[:n], refs[n:2 * n]
        ssem_, rsem_ = refs[2 * n], refs[2 * n + 1]
        x, y, c = _mesh_pos()
        for i in range(n):
            for j, (px, py, qj) in enumerate(_other_chips(x, y)):
                cp = pltpu.make_async_remote_copy(src_ref=src[i].at[qj], dst_ref=land[i].at[j], send_sem=ssem_.at[3 * i + j],
                                                  recv_sem=rsem_.at[3 * i + j], device_id=(px, py, c), device_id_type=MESH)
                cp.wait_send()
                cp.wait_recv()

    outs = pl.pallas_call(
        body, name="chip_exchange_wait_" + tag,
        in_specs=[HBM] * (2 * n) + [SEM, SEM, ANY], out_specs=[HBM] * (2 * n),
        out_shape=[pltpu.HBM(a.shape, a.dtype) for a in parts] + [pltpu.HBM(a.shape, a.dtype) for a in lands],
        input_output_aliases={i: i for i in range(2 * n)},
        compiler_params=pltpu.CompilerParams(has_side_effects=EFFECT),
    )(*parts, *lands, ssem, rsem, after)
    return list(outs[:n]), list(outs[n:])


def _pair_exchange_grads(grads, half_axis, tag):
    n = len(grads)

    def half_of(ref, i, which):
        rows = grads[i].shape[half_axis[i]] // 2
        if half_axis[i] == 0:
            return ref.at[_half_rows(which, rows), :]
        return ref.at[:, _half_rows(which, rows), :]

    def out_shape(i):
        s = list(grads[i].shape)
        s[half_axis[i]] //= 2
        return _sds(tuple(s), grads[i].dtype)

    def body(*refs):
        g, got = refs[:n], refs[n:2 * n]
        ssem, rsem = refs[2 * n:]
        x, y, c = _mesh_pos()
        _pair_barrier(x, y, c)
        cps = []
        for i in range(n):
            cp = pltpu.make_async_remote_copy(src_ref=half_of(g[i], i, 1 - c), dst_ref=got[i], send_sem=ssem.at[i],
                                              recv_sem=rsem.at[i], device_id=(x, y, 1 - c), device_id_type=MESH)
            cp.start()
            cps.append(cp)
        for cp in cps:
            cp.wait()

    return pl.pallas_call(
        body, name="pair_exchange_grads_" + tag,
        in_specs=[ANY] * n, out_specs=[ANY] * n,
        out_shape=[out_shape(i) for i in range(n)],
        scratch_shapes=[pltpu.SemaphoreType.DMA((n,)), pltpu.SemaphoreType.DMA((n,))],
        compiler_params=_pair_params(),
    )(*grads)


def _pair_sum(g, got, c_arr, col_sharded, tag):
    if col_sharded:
        rows, cols = g.shape
        rh, cs = rows // 2, cols // N_CHIPS
        g_spec = pl.BlockSpec((rh, cs), lambda k, c_ref: (c_ref[0], k))
        got_spec = pl.BlockSpec((rh, cs), lambda k, c_ref: (0, k))
    else:
        _, rows, cs = g.shape
        rh = rows // 2
        g_spec = pl.BlockSpec((1, rh, cs), lambda k, c_ref: (k, c_ref[0], 0))
        got_spec = pl.BlockSpec((1, rh, cs), lambda k, c_ref: (k, 0, 0))

    def body(c_ref, g_ref, got_ref, out_ref):
        total = g_ref[...].astype(F32) + got_ref[...].astype(F32)
        out_ref[...] = total.astype(BF16).reshape(out_ref.shape)

    return pl.pallas_call(
        body, name="pair_sum_" + tag,
        grid_spec=pltpu.PrefetchScalarGridSpec(
            num_scalar_prefetch=1, grid=(N_CHIPS,), in_specs=[g_spec, got_spec],
            out_specs=pl.BlockSpec((1, rh, cs), lambda k, c_ref: (k, 0, 0))),
        out_shape=_sds((N_CHIPS, rh, cs), BF16),
        compiler_params=_params(("arbitrary",)),
    )(c_arr, g, got)


def _chip_sum(part, got, qc_arr, tag):
    _, rh, cs = part.shape

    def body(qc_ref, part_ref, got_ref, out_ref):
        total = part_ref[0].astype(F32)
        for j in range(3):
            total = total + got_ref[j].astype(F32)
        out_ref[...] = total

    return pl.pallas_call(
        body, name="chip_sum_" + tag,
        grid_spec=pltpu.PrefetchScalarGridSpec(
            num_scalar_prefetch=1, grid=(1,),
            in_specs=[pl.BlockSpec((1, rh, cs), lambda i, qc: (qc[0], 0, 0)), pl.BlockSpec((3, rh, cs), lambda i, qc: (0, 0, 0))],
            out_specs=pl.BlockSpec((rh, cs), lambda i, qc: (qc[1], 0))),
        out_shape=_sds((2 * rh, cs), F32),
        compiler_params=_params(("arbitrary",)),
    )(qc_arr, part, got)


def _pair_share_grads(grads, tag):
    n = len(grads)

    def body(*refs):
        g = refs[n:2 * n]
        ssem, rsem = refs[2 * n:]
        x, y, c = _mesh_pos()
        _pair_barrier(x, y, c)
        cps = []
        for i in range(n):
            mine = g[i].at[_half_rows(c, grads[i].shape[0] // 2), :]
            cp = pltpu.make_async_remote_copy(src_ref=mine, dst_ref=mine, send_sem=ssem.at[i], recv_sem=rsem.at[i],
                                              device_id=(x, y, 1 - c), device_id_type=MESH)
            cp.start()
            cps.append(cp)
        for cp in cps:
            cp.wait()

    return pl.pallas_call(
        body, name="pair_share_grads_" + tag,
        in_specs=[ANY] * n, out_specs=[ANY] * n,
        out_shape=[_sds(a.shape, a.dtype) for a in grads],
        input_output_aliases={i: i for i in range(n)},
        scratch_shapes=[pltpu.SemaphoreType.DMA((n,)), pltpu.SemaphoreType.DMA((n,))],
        compiler_params=_pair_params(),
    )(*grads)


def _small_allreduce(parts, places, rows_total, width):
    n = len(parts)

    def body(*refs):
        ins, out_ref = refs[:n], refs[n]
        pack, pair_got, chip_sum, got, ssem, rsem = refs[n + 1:]
        x, y, c = _mesh_pos()
        chip = 2 * x + y
        pack[...] = jnp.zeros(pack.shape, F32)
        for i in range(n):
            for row, col, src_row, rows in places[i]:
                w = parts[i].shape[1]
                pack[row:row + rows, col:col + w] = ins[i][src_row:src_row + rows, :]
        swap = pltpu.make_async_remote_copy(src_ref=pack, dst_ref=pair_got, send_sem=ssem.at[3], recv_sem=rsem.at[3],
                                            device_id=(x, y, 1 - c), device_id_type=MESH)
        swap.start()
        swap.wait()
        chip_sum[...] = pack[...] + pair_got[...]
        cps = []
        for j, (px, py, _) in enumerate(_other_chips(x, y)):
            cp = pltpu.make_async_remote_copy(src_ref=chip_sum, dst_ref=got.at[j], send_sem=ssem.at[j],
                                              recv_sem=rsem.at[j], device_id=(px, py, c), device_id_type=MESH)
            cp.start()
            cps.append(cp)
        for cp in cps:
            cp.wait()
        total = jnp.zeros(pack.shape, F32)
        for q in range(N_CHIPS):
            rel = jnp.bitwise_xor(chip, q)
            theirs = got[jnp.maximum(rel - 1, 0)]
            total = total + jnp.where(rel == 0, chip_sum[...], theirs)
        out_ref[...] = total

    return pl.pallas_call(
        body, name="small_allreduce",
        in_specs=[VMEM] * n, out_specs=VMEM,
        out_shape=_sds((rows_total, width), F32),
        scratch_shapes=[pltpu.VMEM((rows_total, width), F32), pltpu.VMEM((rows_total, width), F32),
                        pltpu.VMEM((rows_total, width), F32), pltpu.VMEM((3, rows_total, width), F32),
                        pltpu.SemaphoreType.DMA((4,)), pltpu.SemaphoreType.DMA((4,))],
        compiler_params=_params(),
    )(*parts)


def _adamw_math(w, g, m, v):
    m2 = ADAM_B1 * m + (1.0 - ADAM_B1) * g
    v2 = ADAM_B2 * v + (1.0 - ADAM_B2) * (g * g)
    m_hat = m2 / (1.0 - ADAM_B1 ** ADAM_STEP)
    v_hat = v2 / (1.0 - ADAM_B2 ** ADAM_STEP)
    delta = -ADAM_LR * (m_hat / (jnp.sqrt(v_hat) + ADAM_EPS) + ADAM_WD * w)
    return delta, m2, v2


def _adamw_big(w, g, m, v, tag):
    rows, cols = w.shape
    tr = _row_tile(rows)

    def body(w_ref, g_ref, m_ref, v_ref, go_ref, d_ref, m2_ref, v2_ref):
        gg = g_ref[...]
        go_ref[...] = gg
        d_ref[...], m2_ref[...], v2_ref[...] = _adamw_math(w_ref[...], gg, m_ref[...], v_ref[...])

    spec = pl.BlockSpec((tr, cols), lambda i: (i, 0))
    return pl.pallas_call(
        body, name="adamw_" + tag, grid=(rows // tr,),
        in_specs=[spec] * 4, out_specs=[spec] * 4,
        out_shape=[_sds((rows, cols), F32)] * 4,
        compiler_params=_params(("arbitrary",)),
    )(w, g, m, v)


def _adamw_small(ws, gs, ms, vs):
    n = len(ws)

    def body(*refs):
        w_r, g_r, m_r, v_r = refs[:n], refs[n:2 * n], refs[2 * n:3 * n], refs[3 * n:4 * n]
        d_o, m_o, v_o = refs[4 * n:5 * n], refs[5 * n:6 * n], refs[6 * n:7 * n]
        for i in range(n):
            d_o[i][...], m_o[i][...], v_o[i][...] = _adamw_math(w_r[i][...], g_r[i][...], m_r[i][...], v_r[i][...])

    shapes = [_sds(w.shape, F32) for w in ws]
    outs = pl.pallas_call(
        body, name="adamw_small",
        in_specs=[VMEM] * (4 * n), out_specs=[VMEM] * (3 * n),
        out_shape=shapes * 3,
        compiler_params=_params(),
    )(*ws, *gs, *ms, *vs)
    return outs[:n], outs[n:2 * n], outs[2 * n:]


SMALL_ROWS = 40
PACK_ROWS = 64


def kernel(x, meta_tokens, pre_mix_norm, w_in, conv_a_w, conv_b_w, conv_b_bias, ln_b_gain, ln_b_bias, w_out, post_mix_norm, pre_ffn_norm, w_gate, w_up, w_down, post_ffn_norm, loss_target, m_meta_tokens, m_pre_mix_norm, m_w_in, m_conv_a_w, m_conv_b_w, m_conv_b_bias, m_ln_b_gain, m_ln_b_bias, m_w_out, m_post_mix_norm, m_pre_ffn_norm, m_w_gate, m_w_up, m_w_down, m_post_ffn_norm, v_meta_tokens, v_pre_mix_norm, v_w_in, v_conv_a_w, v_conv_b_w, v_conv_b_bias, v_ln_b_gain, v_ln_b_bias, v_w_out, v_post_mix_norm, v_pre_ffn_norm, v_w_gate, v_w_up, v_w_down, v_post_ffn_norm):
    xq, yq, cq = lax.axis_index("x"), lax.axis_index("y"), lax.axis_index("c")
    chip = 2 * xq + yq
    c_arr = jnp.reshape(cq, (1,)).astype(jnp.int32)
    qc_arr = jnp.stack([chip, cq]).astype(jnp.int32)

    seq, d = x.shape[1], x.shape[2]
    x2, tgt2 = x[0], loss_target[0]
    tr = lambda a: jnp.swapaxes(a, 1, 2)[0]
    w_in2, w_out2, w_gate2, w_up2, w_down2 = w_in[0], w_out[0], tr(w_gate), tr(w_up), w_down[0]
    ka, wa_sh = conv_a_w.shape[1], conv_a_w.shape[2]
    kb = conv_b_w.shape[1]
    meta_sh = meta_tokens.shape[1]

    small = jnp.zeros((PACK_ROWS, meta_sh), F32)
    small = small.at[0:N_META, :].set(meta_tokens)
    small = small.at[16:16 + ka, 0:wa_sh].set(conv_a_w[0])
    small = small.at[24:24 + kb, 0:wa_sh].set(conv_b_w[0])
    q_arr = jnp.reshape(chip, (1,)).astype(jnp.int32)
    small_own = lax.dynamic_update_slice(jnp.zeros((N_CHIPS, PACK_ROWS, meta_sh), F32), small[None], (chip, 0, 0))
    win4, small4 = _gather_shards([_cast_place(w_in2, q_arr, "w_in"), small_own])
    rest = [_cast_place(w, q_arr, nm) for w, nm in ((w_out2, "w_out"), (w_gate2, "w_gate"), (w_up2, "w_up"), (w_down2, "w_down"))]
    g_ssem, g_rsem, lands, g_token = _gather_start(rest, win4)
    meta_f = jnp.concatenate([small4[k, 0:N_META, :] for k in range(N_CHIPS)], axis=1)
    wa_f = jnp.concatenate([small4[k, 16:16 + ka, 0:wa_sh] for k in range(N_CHIPS)], axis=1)
    wb_f = jnp.concatenate([small4[k, 24:24 + kb, 0:wa_sh] for k in range(N_CHIPS)], axis=1)

    tail = jnp.zeros((TAIL_ROWS - N_META, d), F32)
    h = jnp.concatenate([x2, meta_f, tail], axis=0)
    tgt = jnp.concatenate([tgt2, jnp.zeros((TAIL_ROWS, d), F32)], axis=0)
    xn1, hp5 = _mm_in(h, win4, pre_mix_norm + g_token[0:1, 0:1])
    ya, z = _mix_conv_fwd(hp5, wa_f, wb_f, conv_b_bias)
    (wout4,) = _forward_pair(_gather_wait([0], g_ssem, g_rsem, lands[0:1], z, "out"), "out")
    wout_f = wout4.reshape(N_CHIPS * wout4.shape[1], wout4.shape[2])
    yb, mix, h1, xn2 = _mm_out(ya, z, h, wout_f, ln_b_gain, ln_b_bias, post_mix_norm, pre_ffn_norm)
    wg4, wu4 = _forward_pair(_gather_wait([1, 2], g_ssem, g_rsem, lands[1:3], xn2, "gate_up"), "gate_up")
    a4, u4, f4 = _ffn_up(xn2, wg4, wu4)
    (wd4,) = _forward_pair(_gather_wait([3], g_ssem, g_rsem, lands[3:4], f4, "down"), "down")
    dff, dh2, loss_blk, d_gpf = _ffn_down(f4, wd4, h1, tgt, post_ffn_norm)

    da4, du4 = _ffn_bwd_act(dff, wd4, a4, u4)
    g_down = _grad_w_down(f4, dff)
    g_gate, g_up = _grad_w_gate_up(xn2, da4, du4)
    ffn = [g_gate, g_up, g_down]
    got = _pair_exchange_grads(ffn, [1, 1, 1], "ffn")
    parts = [_pair_sum(ffn[i], got[i], c_arr, False, ("gate", "up", "down")[i]) for i in range(3)]
    f_ssem, f_rsem, parts, f_lands, f_token = _chip_exchange_start(parts, dff, "ffn")
    dh1, dmix, d_g2, d_gpm = _ffn_bwd_in(da4, du4, wg4, wu4, h1, mix, dh2, pre_ffn_norm + f_token[0:1, 0:1], post_mix_norm)
    g_out = _grad_w_out(ya, yb, dmix)
    dya, dz, d_lg, d_lb = _mix_bwd_out(dmix, wout_f, z, ln_b_gain, ln_b_bias)
    dhp5, d_wa, d_wb, d_bb = _mix_conv_bwd(hp5, dya, dz, wa_f, wb_f)
    g_in = _grad_w_in(xn1, dhp5)
    dh, d_g1 = _mix_bwd_in(dhp5, win4, h, dh1, pre_mix_norm)
    grad_x = dh[0:seq][None]
    d_meta = dh[seq:seq + N_META]

    g_out4 = g_out.reshape(N_CHIPS, g_out.shape[0] // N_CHIPS, g_out.shape[1])
    mixw = [g_in, g_out4]
    got2 = _pair_exchange_grads(mixw, [0, 1], "mix")
    parts2 = [_pair_sum(mixw[i], got2[i], c_arr, i == 0, ("in", "out")[i]) for i in range(2)]
    m_ssem, m_rsem, parts2, m_lands, m_token = _chip_exchange_start(parts2, dh, "mix")

    parts, f_recv = _chip_exchange_wait(f_ssem, f_rsem, parts, f_lands, m_token, "ffn")
    halves = [_chip_sum(parts[i], f_recv[i], qc_arr, ("gate", "up", "down")[i]) for i in range(3)]
    gsum_ffn = _pair_share_grads(halves, "ffn")

    hw = d // 2
    assert d_wa.shape == (3, hw) and d_wb.shape == (31, hw) and d_bb.shape == (1, hw)
    small_parts = [d_meta, d_g1, d_gpm, d_g2, d_gpf, d_bb, d_lg, d_lb, loss_blk[0:1, :], d_wa, d_wb]
    places = [[(0, 0, 0, N_META)], [(16, 0, 0, 1)], [(17, 0, 0, 1)], [(18, 0, 0, 1)], [(19, 0, 0, 1)],
              [(20, 0, 0, 1)], [(20, hw, 0, 1)], [(21, 0, 0, 1)], [(21, hw, 0, 1)], [(22, 0, 0, 3)],
              [(22, hw, 0, 3), (25, 0, 3, 14), (25, hw, 17, 14)]]
    red = _small_allreduce(small_parts, places, SMALL_ROWS, d)
    s_meta, s_g1, s_gpm, s_g2, s_gpf = red[0:N_META], red[16:17], red[17:18], red[18:19], red[19:20]
    s_bb, s_lg, s_lb, s_loss = red[20:21, 0:hw], red[20:21, hw:d], red[21:22, 0:hw], red[21:22, hw:hw + LANES]
    s_wa = red[22:25, 0:hw]
    s_wb = jnp.concatenate([red[22:25, hw:d], red[25:39, 0:hw], red[25:39, hw:d]], axis=0)
    g_meta = lax.dynamic_slice_in_dim(s_meta, chip * meta_sh, meta_sh, axis=1)
    g_wa = lax.dynamic_slice_in_dim(s_wa, chip * wa_sh, wa_sh, axis=1)[None]
    g_wb = lax.dynamic_slice_in_dim(s_wb, chip * wa_sh, wa_sh, axis=1)[None]
    loss = s_loss[0, 0]

    names_big = ["w_in", "w_out", "w_gate", "w_up", "w_down"]
    w_big = dict(zip(names_big, [w_in2, w_out2, w_gate2, w_up2, w_down2]))
    m_big = dict(zip(names_big, [m_w_in[0], m_w_out[0], tr(m_w_gate), tr(m_w_up), m_w_down[0]]))
    v_big = dict(zip(names_big, [v_w_in[0], v_w_out[0], tr(v_w_gate), tr(v_w_up), v_w_down[0]]))
    grads, deltas, new_m, new_v = {}, {}, {}, {}

    def update(nm, g):
        outs = _adamw_big(w_big[nm], g, m_big[nm], v_big[nm], nm)
        if nm in ("w_gate", "w_up"):
            outs = [jnp.swapaxes(o[None], 1, 2) for o in outs]
        else:
            outs = [o[None] for o in outs]
        grads[nm], deltas[nm], new_m[nm], new_v[nm] = outs
        return outs[1]

    last = None
    for nm, g in zip(["w_gate", "w_up", "w_down"], gsum_ffn):
        last = update(nm, g)

    parts2, m_recv = _chip_exchange_wait(m_ssem, m_rsem, parts2, m_lands, last, "mix")
    halves2 = [_chip_sum(parts2[i], m_recv[i], qc_arr, ("in", "out")[i]) for i in range(2)]
    gsum_mix = _pair_share_grads(halves2, "mix")
    for nm, g in zip(["w_in", "w_out"], gsum_mix):
        update(nm, g)

    names_small = ["meta_tokens", "pre_mix_norm", "conv_a_w", "conv_b_w", "conv_b_bias", "ln_b_gain", "ln_b_bias",
                   "post_mix_norm", "pre_ffn_norm", "post_ffn_norm"]
    w_small = [meta_tokens, pre_mix_norm, conv_a_w[0], conv_b_w[0], conv_b_bias, ln_b_gain, ln_b_bias, post_mix_norm,
               pre_ffn_norm, post_ffn_norm]
    g_small = [g_meta, s_g1, g_wa[0], g_wb[0], s_bb, s_lg, s_lb, s_gpm, s_g2, s_gpf]
    m_small = [m_meta_tokens, m_pre_mix_norm, m_conv_a_w[0], m_conv_b_w[0], m_conv_b_bias, m_ln_b_gain, m_ln_b_bias,
               m_post_mix_norm, m_pre_ffn_norm, m_post_ffn_norm]
    v_small = [v_meta_tokens, v_pre_mix_norm, v_conv_a_w[0], v_conv_b_w[0], v_conv_b_bias, v_ln_b_gain, v_ln_b_bias,
               v_post_mix_norm, v_pre_ffn_norm, v_post_ffn_norm]
    d_s, m_s, v_s = _adamw_small(w_small, g_small, m_small, v_small)
    for i, nm in enumerate(names_small):
        lead = nm in ("conv_a_w", "conv_b_w")
        fix = (lambda a: a[None]) if lead else (lambda a: a)
        grads[nm], deltas[nm], new_m[nm], new_v[nm] = fix(g_small[i]), fix(d_s[i]), fix(m_s[i]), fix(v_s[i])

    order = ["meta_tokens", "pre_mix_norm", "w_in", "conv_a_w", "conv_b_w", "conv_b_bias", "ln_b_gain", "ln_b_bias", "w_out",
             "post_mix_norm", "pre_ffn_norm", "w_gate", "w_up", "w_down", "post_ffn_norm"]
    return (loss, grad_x, *[grads[k] for k in order], *[deltas[k] for k in order], *[new_m[k] for k in order],
            *[new_v[k] for k in order])
```

```python
import functools

import jax
import jax.numpy as jnp
from jax import lax
from jax.experimental import pallas as pl
from jax.experimental.pallas import tpu as pltpu

F32 = jnp.float32
BF16 = jnp.bfloat16
MESH = pl.DeviceIdType.MESH

N_META = 16
TAIL_ROWS = 128
RMS_EPS = 1e-6
LN_EPS = 1e-5
ADAM_LR = 0.001
ADAM_B1 = 0.9
ADAM_B2 = 0.999
ADAM_EPS = 1e-08
ADAM_WD = 0.01
ADAM_STEP = 10

N_CHIPS = 4
LANES = 128
MXU_TILE = 256
CONV_CHUNK = 48
CONV_HIST = 32
ROW_TILE_CAP = 640
VMEM_LIMIT = 56 * 1024 * 1024

NN = (((1,), (0,)), ((), ()))
NT = (((1,), (1,)), ((), ()))
TN = (((0,), (0,)), ((), ()))


def _dot(a, b, dims=NN):
    return lax.dot_general(a, b, dims, preferred_element_type=F32)


def _sig(v):
    return 1.0 / (1.0 + jnp.exp(-v))


def _mean(v):
    return jnp.mean(v, axis=-1, keepdims=True)


def _row_tile(rows):
    best = 16
    for t in range(16, min(rows, ROW_TILE_CAP) + 1, 16):
        if rows % t == 0:
            best = t
    assert rows % best == 0
    return best


def _pieces(n_shards, shard_w, piece_w):
    total = n_shards * shard_w
    cuts = sorted(set(range(0, total + 1, shard_w)) | set(range(0, total + 1, piece_w)))
    out = []
    for lo, hi in zip(cuts[:-1], cuts[1:]):
        out.append((lo // shard_w, lo % shard_w, lo // piece_w, lo % piece_w, hi - lo))
    return out


def _params(semantics=None):
    kw = dict(vmem_limit_bytes=VMEM_LIMIT)
    if semantics is not None:
        kw["dimension_semantics"] = semantics
    return pltpu.CompilerParams(**kw)


def _full(shape):
    nd = len(shape)
    return pl.BlockSpec(shape, lambda *_: (0,) * nd)


def _resident(shape):
    nd = len(shape)
    return pl.BlockSpec(shape, lambda *_: (0,) * nd, pipeline_mode=pl.Buffered(1))


def _sds(shape, dtype):
    return jax.ShapeDtypeStruct(shape, dtype)


ANY = pl.BlockSpec(memory_space=pl.ANY)
VMEM = pl.BlockSpec(memory_space=pltpu.VMEM)


def _mesh_pos():
    return lax.axis_index("x"), lax.axis_index("y"), lax.axis_index("c")


def _flip(v, bit):
    return 1 - v if bit else v


def _mm_in(x, tail, win4, g1):
    seq, d = x.shape
    tp = seq + TAIL_ROWS
    tm = _row_tile(tp)
    n_sh, _, csh = win4.shape
    pw = n_sh * csh // 5
    pieces = _pieces(n_sh, csh, pw)

    def body(x_ref, tail_ref, w_ref, g_ref, h_ref, xn_ref, hp_ref):
        rows = pl.program_id(0) * tm + lax.broadcasted_iota(jnp.int32, (tm, 1), 0)
        hh = jnp.where(rows < seq, x_ref[...], tail_ref[...])
        h_ref[...] = hh
        r = lax.rsqrt(_mean(hh * hh) + RMS_EPS)
        xn = (hh * r * g_ref[...]).astype(BF16)
        xn_ref[...] = xn
        for k, klo, p, plo, w in pieces:
            hp_ref[p, :, plo:plo + w] = _dot(xn, w_ref[k, :, klo:klo + w])

    row = pl.BlockSpec((tm, d), lambda i: (i, 0))
    return pl.pallas_call(
        body, name="mm_in", grid=(tp // tm,),
        in_specs=[row, _full(tail.shape), _resident(win4.shape), _full(g1.shape)],
        out_specs=[row, row, pl.BlockSpec((5, tm, pw), lambda i: (0, i, 0))],
        out_shape=[_sds((tp, d), F32), _sds((tp, d), BF16), _sds((5, tp, pw), F32)],
        compiler_params=_params(("arbitrary",)),
    )(x, tail, win4, g1)


def _seq_rows(tp):
    seq = tp - TAIL_ROWS
    nseq = seq + N_META
    assert nseq % CONV_CHUNK == 0 and seq % 16 == 0
    return seq, nseq


def _conv_taps(src_ref, w_ref, dst_ref, width, nseq, transpose):
    w = w_ref[...]

    def step(n, carry):
        out0 = pl.multiple_of(CONV_HIST + n * CONV_CHUNK, 8)
        win0 = out0 if transpose else pl.multiple_of(n * CONV_CHUNK, 8)
        win = src_ref[pl.ds(win0, CONV_CHUNK + CONV_HIST), :]
        acc = jnp.zeros((CONV_CHUNK, w.shape[1]), F32)
        for k in range(width):
            off = (width - 1 - k) if transpose else (CONV_HIST - (width - 1) + k)
            acc = acc + w[k:k + 1, :] * win[off:off + CONV_CHUNK, :]
        dst_ref[pl.ds(out0, CONV_CHUNK), :] = acc
        return carry

    lax.fori_loop(0, nseq // CONV_CHUNK, step, 0)


def _conv_wgrad(src_ref, dz_ref, acc_ref, width, nseq):
    acc_ref[...] = jnp.zeros(acc_ref.shape, F32)

    def step(n, carry):
        win = src_ref[pl.ds(pl.multiple_of(n * CONV_CHUNK, 8), CONV_CHUNK + CONV_HIST), :]
        dzc = dz_ref[pl.ds(pl.multiple_of(CONV_HIST + n * CONV_CHUNK, 8), CONV_CHUNK), :]
        for k in range(width):
            off = CONV_HIST - (width - 1) + k
            prod = dzc * win[off:off + CONV_CHUNK, :]
            part = prod[0:8, :]
            for s in range(1, CONV_CHUNK // 8):
                part = part + prod[8 * s:8 * s + 8, :]
            acc_ref[8 * k:8 * k + 8, :] += part
        return carry

    lax.fori_loop(0, nseq // CONV_CHUNK, step, 0)


def _to_seq(buf_ref, x_part, meta_part, seq):
    buf_ref[CONV_HIST:CONV_HIST + N_META, :] = meta_part
    buf_ref[CONV_HIST + N_META:CONV_HIST + N_META + seq, :] = x_part


def _zero_ends(buf_ref, nseq):
    zeros = jnp.zeros((CONV_HIST, buf_ref.shape[1]), F32)
    buf_ref[0:CONV_HIST, :] = zeros
    buf_ref[CONV_HIST + nseq:CONV_HIST + nseq + CONV_HIST, :] = zeros


def _mix_conv_fwd(hp5, wa, wb, bb):
    _, tp, wgrp = hp5.shape
    seq, nseq = _seq_rows(tp)
    sb = nseq + 2 * CONV_HIST
    ka, kb = wa.shape[0], wb.shape[0]
    xs, ms = slice(0, seq), slice(seq, seq + N_META)
    ox, om = slice(CONV_HIST + N_META, CONV_HIST + nseq), slice(CONV_HIST, CONV_HIST + N_META)

    def body(hp_ref, wa_ref, wb_ref, bb_ref, ya_ref, z_ref, s_ref, o_ref):
        _zero_ends(s_ref, nseq)
        _to_seq(s_ref, hp_ref[1, xs, :] * hp_ref[2, xs, :], hp_ref[1, ms, :] * hp_ref[2, ms, :], seq)
        _conv_taps(s_ref, wa_ref, o_ref, ka, nseq, False)
        ya_ref[xs, :] = (hp_ref[0, xs, :] * o_ref[ox, :]).astype(BF16)
        ya_ref[ms, :] = (hp_ref[0, ms, :] * o_ref[om, :]).astype(BF16)
        ya_ref[seq + N_META:tp, :] = jnp.zeros((tp - seq - N_META, LANES), BF16)
        _to_seq(s_ref, hp_ref[3, xs, :] * _sig(hp_ref[4, xs, :]), hp_ref[3, ms, :] * _sig(hp_ref[4, ms, :]), seq)
        _conv_taps(s_ref, wb_ref, o_ref, kb, nseq, False)
        z_ref[xs, :] = o_ref[ox, :] + bb_ref[...]
        z_ref[ms, :] = o_ref[om, :] + bb_ref[...]
        z_ref[seq + N_META:tp, :] = jnp.zeros((tp - seq - N_META, LANES), F32)

    col = lambda j: (0, j)
    return pl.pallas_call(
        body, name="mix_conv_fwd", grid=(wgrp // LANES,),
        in_specs=[pl.BlockSpec((5, tp, LANES), lambda j: (0, 0, j)), pl.BlockSpec((ka, LANES), col),
                  pl.BlockSpec((kb, LANES), col), pl.BlockSpec((1, LANES), col)],
        out_specs=[pl.BlockSpec((tp, LANES), col), pl.BlockSpec((tp, LANES), col)],
        out_shape=[_sds((tp, wgrp), BF16), _sds((tp, wgrp), F32)],
        scratch_shapes=[pltpu.VMEM((sb, LANES), F32), pltpu.VMEM((sb, LANES), F32)],
        compiler_params=_params(("arbitrary",)),
    )(hp5, wa, wb, bb)


def _layer_norm_parts(z, lg, lb):
    mu = _mean(z)
    zc = z - mu
    rl = lax.rsqrt(_mean(zc * zc) + LN_EPS)
    zh = zc * rl
    return rl, zh, zh * lg + lb


def _mm_out(ya, z, h, wout, lg, lb, gpm, g2):
    tp, d = h.shape
    wa_ = ya.shape[1]
    tm = _row_tile(tp)

    def body(ya_ref, z_ref, h_ref, w_ref, lg_ref, lb_ref, gpm_ref, g2_ref, yb_ref, mix_ref, h1_ref, xn2_ref):
        _, _, l = _layer_norm_parts(z_ref[...], lg_ref[...], lb_ref[...])
        yb = (l * _sig(l)).astype(BF16)
        yb_ref[...] = yb
        mix = _dot(ya_ref[...], w_ref[0:wa_, :]) + _dot(yb, w_ref[wa_:d, :])
        mix_ref[...] = mix
        rm = lax.rsqrt(_mean(mix * mix) + RMS_EPS)
        h1 = h_ref[...] + mix * rm * gpm_ref[...]
        h1_ref[...] = h1
        r2 = lax.rsqrt(_mean(h1 * h1) + RMS_EPS)
        xn2_ref[...] = (h1 * r2 * g2_ref[...]).astype(BF16)

    row = lambda i: (i, 0)
    return pl.pallas_call(
        body, name="mm_out", grid=(tp // tm,),
        in_specs=[pl.BlockSpec((tm, wa_), row), pl.BlockSpec((tm, wa_), row), pl.BlockSpec((tm, d), row),
                  _resident(wout.shape), _full(lg.shape), _full(lb.shape), _full(gpm.shape), _full(g2.shape)],
        out_specs=[pl.BlockSpec((tm, wa_), row), pl.BlockSpec((tm, d), row), pl.BlockSpec((tm, d), row),
                   pl.BlockSpec((tm, d), row)],
        out_shape=[_sds((tp, wa_), BF16), _sds((tp, d), F32), _sds((tp, d), F32), _sds((tp, d), BF16)],
        compiler_params=_params(("arbitrary",)),
    )(ya, z, h, wout, lg, lb, gpm, g2)


def _ffn_up(xn2, wg, wu):
    tp, d = xn2.shape
    ff_dim = wg.shape[0]
    tm = _row_tile(tp)
    assert ff_dim % MXU_TILE == 0

    def body(xn_ref, wg_ref, wu_ref, p_ref, q_ref, f_ref):
        xn = xn_ref[...]
        for lo in range(0, ff_dim, MXU_TILE):
            cols = slice(lo, lo + MXU_TILE)
            a = _dot(xn, wg_ref[cols, :], NT)
            u = _dot(xn, wu_ref[cols, :], NT)
            s = _sig(a)
            q = a * s
            p_ref[:, cols] = (u * (s + q * (1.0 - s))).astype(BF16)
            q_ref[:, cols] = q.astype(BF16)
            f_ref[:, cols] = (q * u).astype(BF16)

    ospec = pl.BlockSpec((tm, ff_dim), lambda i: (i, 0))
    return pl.pallas_call(
        body, name="ffn_up", grid=(tp // tm,),
        in_specs=[pl.BlockSpec((tm, d), lambda i: (i, 0)), _resident(wg.shape), _resident(wu.shape)],
        out_specs=[ospec, ospec, ospec],
        out_shape=[_sds((tp, ff_dim), BF16)] * 3,
        compiler_params=_params(("arbitrary",)),
    )(xn2, wg, wu)


def _ffn_down(f, wd, h1, tgt, gpf):
    tp, ff_dim = f.shape
    d = h1.shape[1]
    tm = _row_tile(tp)
    seq, _ = _seq_rows(tp)

    def body(f_ref, w_ref, h1_ref, t_ref, gpf_ref, dff_ref, dh2_ref, loss_ref, dgpf_ref):
        i = pl.program_id(0)
        ff = _dot(f_ref[...], w_ref[...])
        rf = lax.rsqrt(_mean(ff * ff) + RMS_EPS)
        nf = ff * rf
        gpf_ = gpf_ref[...]
        h2 = h1_ref[...] + nf * gpf_
        rows = i * tm + lax.broadcasted_iota(jnp.int32, (tm, 1), 0)
        err = jnp.where(rows < seq, h2 - t_ref[...], 0.0)
        dh2 = err * (1.0 / d)
        dh2_ref[...] = dh2
        dn = dh2 * gpf_
        dff_ref[...] = (rf * (dn - nf * _mean(dn * nf))).astype(BF16)

        @pl.when(i == 0)
        def _():
            loss_ref[...] = jnp.zeros(loss_ref.shape, F32)
            dgpf_ref[...] = jnp.zeros(dgpf_ref.shape, F32)

        loss_ref[...] += (0.5 / d) * jnp.sum(err * err, axis=(0, 1), keepdims=True)
        dgpf_ref[...] += jnp.sum(dh2 * nf, axis=0, keepdims=True)

    row = lambda i: (i, 0)
    return pl.pallas_call(
        body, name="ffn_down", grid=(tp // tm,),
        in_specs=[pl.BlockSpec((tm, ff_dim), row), _resident(wd.shape), pl.BlockSpec((tm, d), row),
                  pl.BlockSpec((tm, d), row), _full(gpf.shape)],
        out_specs=[pl.BlockSpec((tm, d), row), pl.BlockSpec((tm, d), row), _full((8, LANES)), _full((1, d))],
        out_shape=[_sds((tp, d), BF16), _sds((tp, d), F32), _sds((8, LANES), F32), _sds((1, d), F32)],
        compiler_params=_params(("arbitrary",)),
    )(f, wd, h1, tgt, gpf)


def _ffn_bwd_act(dff, wd, p, q):
    tp, d = dff.shape
    ff_dim = wd.shape[0]
    tm = _row_tile(tp)

    def body(dff_ref, w_ref, p_ref, q_ref, da_ref, du_ref):
        dffv = dff_ref[...]
        for lo in range(0, ff_dim, MXU_TILE):
            cols = slice(lo, lo + MXU_TILE)
            df = _dot(dffv, w_ref[cols, :], NT).astype(BF16)
            da_ref[:, cols] = df * p_ref[:, cols]
            du_ref[:, cols] = df * q_ref[:, cols]

    aspec = pl.BlockSpec((tm, ff_dim), lambda i: (i, 0))
    return pl.pallas_call(
        body, name="ffn_bwd_act", grid=(tp // tm,),
        in_specs=[pl.BlockSpec((tm, d), lambda i: (i, 0)), _resident(wd.shape), aspec, aspec],
        out_specs=[aspec, aspec],
        out_shape=[_sds((tp, ff_dim), BF16)] * 2,
        compiler_params=_params(("arbitrary",)),
    )(dff, wd, p, q)


def _grad_blocks(ff_dim):
    rows = ff_dim // 2
    assert rows % LANES == 0
    return rows


def _grad_w_down(f, dff):
    tp, ff_dim = f.shape
    d = dff.shape[1]
    rows = _grad_blocks(ff_dim)

    def body(f_ref, dff_ref, g_ref):
        g_ref[...] = _dot(f_ref[...], dff_ref[...], TN).astype(BF16)

    return pl.pallas_call(
        body, name="grad_w_down", grid=(ff_dim // rows,),
        in_specs=[pl.BlockSpec((tp, rows), lambda k: (0, k)), _resident(dff.shape)],
        out_specs=pl.BlockSpec((rows, d), lambda k: (k, 0)),
        out_shape=_sds((ff_dim, d), BF16),
        compiler_params=_params(("arbitrary",)),
    )(f, dff)


def _grad_w_gate_up(xn2, da, du):
    tp, ff_dim = da.shape
    d = xn2.shape[1]
    rows = _grad_blocks(ff_dim)

    def body(xn_ref, da_ref, du_ref, gg_ref, gu_ref):
        xn = xn_ref[...]
        gg_ref[...] = _dot(da_ref[...], xn, TN).astype(BF16)
        gu_ref[...] = _dot(du_ref[...], xn, TN).astype(BF16)

    aspec = pl.BlockSpec((tp, rows), lambda k: (0, k))
    gspec = pl.BlockSpec((rows, d), lambda k: (k, 0))
    return pl.pallas_call(
        body, name="grad_w_gate_up", grid=(ff_dim // rows,),
        in_specs=[_resident(xn2.shape), aspec, aspec],
        out_specs=[gspec, gspec],
        out_shape=[_sds((ff_dim, d), BF16)] * 2,
        compiler_params=_params(("arbitrary",)),
    )(xn2, da, du)


def _rms_bwd(dy, x, r, g):
    n = x * r
    dn = dy * g
    return r * (dn - n * _mean(dn * n)), dy * n


def _ffn_bwd_in(da, du, wg, wu, h1, mix, dh2, g2, gpm):
    tp, ff_dim = da.shape
    d = h1.shape[1]
    tm = _row_tile(tp)

    def body(da_ref, du_ref, wg_ref, wu_ref, h1_ref, mix_ref, dh2_ref, g2_ref, gpm_ref,
             dh1_ref, dmix_ref, dg2_ref, dgpm_ref):
        i = pl.program_id(0)
        dxn = _dot(da_ref[...], wg_ref[...]) + _dot(du_ref[...], wu_ref[...])

        @pl.when(i == 0)
        def _():
            dg2_ref[...] = jnp.zeros(dg2_ref.shape, F32)
            dgpm_ref[...] = jnp.zeros(dgpm_ref.shape, F32)

        h1v = h1_ref[...]
        r2 = lax.rsqrt(_mean(h1v * h1v) + RMS_EPS)
        dres, dg2_rows = _rms_bwd(dxn, h1v, r2, g2_ref[...])
        dh1 = dh2_ref[...] + dres
        dh1_ref[...] = dh1
        mixv = mix_ref[...]
        rm = lax.rsqrt(_mean(mixv * mixv) + RMS_EPS)
        dmix, dgpm_rows = _rms_bwd(dh1, mixv, rm, gpm_ref[...])
        dmix_ref[...] = dmix.astype(BF16)
        dg2_ref[...] += jnp.sum(dg2_rows, axis=0, keepdims=True)
        dgpm_ref[...] += jnp.sum(dgpm_rows, axis=0, keepdims=True)

    aspec = pl.BlockSpec((tm, ff_dim), lambda i: (i, 0))
    row = pl.BlockSpec((tm, d), lambda i: (i, 0))
    return pl.pallas_call(
        body, name="ffn_bwd_in", grid=(tp // tm,),
        in_specs=[aspec, aspec, _resident(wg.shape), _resident(wu.shape), row, row, row, _full(g2.shape), _full(gpm.shape)],
        out_specs=[row, row, _full((1, d)), _full((1, d))],
        out_shape=[_sds((tp, d), F32), _sds((tp, d), BF16), _sds((1, d), F32), _sds((1, d), F32)],
        compiler_params=_params(("arbitrary",)),
    )(da, du, wg, wu, h1, mix, dh2, g2, gpm)


def _grad_w_out(ya, yb, dmix):
    tp, wa_ = ya.shape
    d = dmix.shape[1]

    def body(ya_ref, yb_ref, dmix_ref, g_ref):
        dm = dmix_ref[...]
        g_ref[0:wa_, :] = _dot(ya_ref[...], dm, TN).astype(BF16)
        g_ref[wa_:2 * wa_, :] = _dot(yb_ref[...], dm, TN).astype(BF16)

    return pl.pallas_call(
        body, name="grad_w_out", grid=(1,),
        in_specs=[_full(ya.shape), _full(yb.shape), _full(dmix.shape)],
        out_specs=_full((2 * wa_, d)),
        out_shape=_sds((2 * wa_, d), BF16),
        compiler_params=_params(("arbitrary",)),
    )(ya, yb, dmix)


def _mix_bwd_out(dmix, wout, z, lg, lb):
    tp, d = dmix.shape
    wa_ = z.shape[1]
    tm = _row_tile(tp)

    def body(dmix_ref, w_ref, z_ref, lg_ref, lb_ref, dya_ref, dz_ref, dlg_ref, dlb_ref):
        i = pl.program_id(0)
        dm = dmix_ref[...]
        dya_ref[...] = _dot(dm, w_ref[0:wa_, :], NT)
        dyb = _dot(dm, w_ref[wa_:d, :], NT)
        lg_ = lg_ref[...]
        rl, zh, l = _layer_norm_parts(z_ref[...], lg_, lb_ref[...])
        sl = _sig(l)
        dl = dyb * (sl * (1.0 + l * (1.0 - sl)))
        dzh = dl * lg_
        dz_ref[...] = rl * (dzh - _mean(dzh) - zh * _mean(dzh * zh))

        @pl.when(i == 0)
        def _():
            dlg_ref[...] = jnp.zeros(dlg_ref.shape, F32)
            dlb_ref[...] = jnp.zeros(dlb_ref.shape, F32)

        dlg_ref[...] += jnp.sum(dl * zh, axis=0, keepdims=True)
        dlb_ref[...] += jnp.sum(dl, axis=0, keepdims=True)

    row = lambda i: (i, 0)
    return pl.pallas_call(
        body, name="mix_bwd_out", grid=(tp // tm,),
        in_specs=[pl.BlockSpec((tm, d), row), _resident(wout.shape), pl.BlockSpec((tm, wa_), row), _full(lg.shape),
                  _full(lb.shape)],
        out_specs=[pl.BlockSpec((tm, wa_), row), pl.BlockSpec((tm, wa_), row), _full((1, wa_)), _full((1, wa_))],
        out_shape=[_sds((tp, wa_), F32), _sds((tp, wa_), F32), _sds((1, wa_), F32), _sds((1, wa_), F32)],
        compiler_params=_params(("arbitrary",)),
    )(dmix, wout, z, lg, lb)


def _mix_conv_bwd(hp5, dya, dz, wa, wb):
    _, tp, wgrp = hp5.shape
    seq, nseq = _seq_rows(tp)
    sb = nseq + 2 * CONV_HIST
    ka, kb = wa.shape[0], wb.shape[0]
    xs, ms = slice(0, seq), slice(seq, seq + N_META)
    ox, om = slice(CONV_HIST + N_META, CONV_HIST + nseq), slice(CONV_HIST, CONV_HIST + N_META)
    n_tail = tp - seq - N_META

    def body(hp_ref, dya_ref, dz_ref, wa_ref, wb_ref, dhp_ref, dwa_ref, dwb_ref, dbb_ref, s_ref, d_ref, o_ref, acc_ref):
        _zero_ends(s_ref, nseq)
        _zero_ends(d_ref, nseq)

        def put(p, ox_val, om_val):
            dhp_ref[p, xs, :] = ox_val.astype(BF16)
            dhp_ref[p, ms, :] = om_val.astype(BF16)
            dhp_ref[p, seq + N_META:tp, :] = jnp.zeros((n_tail, LANES), BF16)

        def wgrad(dw_ref, width):
            for k in range(width):
                dw_ref[k:k + 1, :] = jnp.sum(acc_ref[8 * k:8 * k + 8, :], axis=0, keepdims=True)

        _to_seq(s_ref, hp_ref[1, xs, :] * hp_ref[2, xs, :], hp_ref[1, ms, :] * hp_ref[2, ms, :], seq)
        _conv_taps(s_ref, wa_ref, o_ref, ka, nseq, False)
        put(0, dya_ref[xs, :] * o_ref[ox, :], dya_ref[ms, :] * o_ref[om, :])
        _to_seq(d_ref, dya_ref[xs, :] * hp_ref[0, xs, :], dya_ref[ms, :] * hp_ref[0, ms, :], seq)
        _conv_wgrad(s_ref, d_ref, acc_ref, ka, nseq)
        wgrad(dwa_ref, ka)
        _conv_taps(d_ref, wa_ref, o_ref, ka, nseq, True)
        put(1, o_ref[ox, :] * hp_ref[2, xs, :], o_ref[om, :] * hp_ref[2, ms, :])
        put(2, o_ref[ox, :] * hp_ref[1, xs, :], o_ref[om, :] * hp_ref[1, ms, :])

        _to_seq(s_ref, hp_ref[3, xs, :] * _sig(hp_ref[4, xs, :]), hp_ref[3, ms, :] * _sig(hp_ref[4, ms, :]), seq)
        _to_seq(d_ref, dz_ref[xs, :], dz_ref[ms, :], seq)
        dbb_ref[...] = (jnp.sum(dz_ref[xs, :], axis=0, keepdims=True)
                        + jnp.sum(dz_ref[ms, :], axis=0, keepdims=True))
        _conv_wgrad(s_ref, d_ref, acc_ref, kb, nseq)
        wgrad(dwb_ref, kb)
        _conv_taps(d_ref, wb_ref, o_ref, kb, nseq, True)
        sx, sm = _sig(hp_ref[4, xs, :]), _sig(hp_ref[4, ms, :])
        put(3, o_ref[ox, :] * sx, o_ref[om, :] * sm)
        put(4, o_ref[ox, :] * hp_ref[3, xs, :] * sx * (1.0 - sx), o_ref[om, :] * hp_ref[3, ms, :] * sm * (1.0 - sm))

    col = lambda j: (0, j)
    blk5 = pl.BlockSpec((5, tp, LANES), lambda j: (0, 0, j))
    return pl.pallas_call(
        body, name="mix_conv_bwd", grid=(wgrp // LANES,),
        in_specs=[blk5, pl.BlockSpec((tp, LANES), col), pl.BlockSpec((tp, LANES), col),
                  pl.BlockSpec((ka, LANES), col), pl.BlockSpec((kb, LANES), col)],
        out_specs=[blk5, pl.BlockSpec((ka, LANES), col), pl.BlockSpec((kb, LANES), col), pl.BlockSpec((1, LANES), col)],
        out_shape=[_sds((5, tp, wgrp), BF16), _sds((ka, wgrp), F32), _sds((kb, wgrp), F32), _sds((1, wgrp), F32)],
        scratch_shapes=[pltpu.VMEM((sb, LANES), F32), pltpu.VMEM((sb, LANES), F32), pltpu.VMEM((sb, LANES), F32),
                        pltpu.VMEM((8 * kb, LANES), F32)],
        compiler_params=_params(("arbitrary",)),
    )(hp5, dya, dz, wa, wb)


def _grad_w_in(xn1, dhp5):
    n_p, tp, pw = dhp5.shape
    d = xn1.shape[1]

    def body(xn_ref, dhp_ref, g_ref):
        g_ref[...] = _dot(xn_ref[...], dhp_ref[0], TN).astype(BF16)

    return pl.pallas_call(
        body, name="grad_w_in", grid=(n_p,),
        in_specs=[_resident(xn1.shape), pl.BlockSpec((1, tp, pw), lambda p: (p, 0, 0))],
        out_specs=pl.BlockSpec((d, pw), lambda p: (0, p)),
        out_shape=_sds((d, n_p * pw), BF16),
        compiler_params=_params(("arbitrary",)),
    )(xn1, dhp5)


def _mix_bwd_in(dhp5, win4, h, dh1, g1):
    n_p, tp, pw = dhp5.shape
    d = h.shape[1]
    n_sh, _, csh = win4.shape
    tm = _row_tile(tp)
    pieces = _pieces(n_sh, csh, pw)

    seq, _ = _seq_rows(tp)
    last, meta_off = seq // tm, seq % tm
    assert last == tp // tm - 1 and meta_off + N_META <= tm

    def body(dhp_ref, w_ref, h_ref, dh1_ref, g_ref, gx_ref, dmeta_ref, dg1_ref):
        i = pl.program_id(0)
        dxn = None
        for k, klo, p, plo, w in pieces:
            t = _dot(dhp_ref[p, :, plo:plo + w], w_ref[k, :, klo:klo + w], NT)
            dxn = t if dxn is None else dxn + t
        hh = h_ref[...]
        r1 = lax.rsqrt(_mean(hh * hh) + RMS_EPS)
        dres, dg_rows = _rms_bwd(dxn, hh, r1, g_ref[...])
        dh = dh1_ref[...] + dres
        gx_ref[...] = dh

        @pl.when(i == 0)
        def _():
            dg1_ref[...] = jnp.zeros(dg1_ref.shape, F32)

        @pl.when(i == last)
        def _():
            dmeta_ref[...] = dh[meta_off:meta_off + N_META, :]

        dg1_ref[...] += jnp.sum(dg_rows, axis=0, keepdims=True)

    row = lambda i: (i, 0)
    return pl.pallas_call(
        body, name="mix_bwd_in", grid=(tp // tm,),
        in_specs=[pl.BlockSpec((n_p, tm, pw), lambda i: (0, i, 0)), _resident(win4.shape), pl.BlockSpec((tm, d), row),
                  pl.BlockSpec((tm, d), row), _full(g1.shape)],
        out_specs=[pl.BlockSpec((tm, d), row), _full((N_META, d)), _full((1, d))],
        out_shape=[_sds((seq, d), F32), _sds((N_META, d), F32), _sds((1, d), F32)],
        compiler_params=_params(("arbitrary",)),
    )(dhp5, win4, h, dh1, g1)


def _other_chips(x, y):
    out = []
    for j in (1, 2, 3):
        px, py = _flip(x, j >> 1), _flip(y, j & 1)
        out.append((px, py, 2 * px + py))
    return out


PAIR_COLLECTIVE_ID = 0


def _pair_barrier(x, y, c):
    sem = pltpu.get_barrier_semaphore()
    pl.semaphore_signal(sem, inc=1, device_id=(x, y, 1 - c), device_id_type=MESH)
    pl.semaphore_wait(sem, 1)


def _pair_params():
    return pltpu.CompilerParams(collective_id=PAIR_COLLECTIVE_ID)


def _half_rows(c, rows_half):
    return pl.ds(pl.multiple_of(c * rows_half, 8), rows_half)


def _cast_place(w, q_arr, tag):
    rows, cols = w.shape
    tr = _row_tile(rows)

    def body(q_ref, w_ref, out_ref):
        out_ref[0] = w_ref[...].astype(BF16)

    return pl.pallas_call(
        body, name="cast_place_" + tag,
        grid_spec=pltpu.PrefetchScalarGridSpec(
            num_scalar_prefetch=1, grid=(rows // tr,),
            in_specs=[pl.BlockSpec((tr, cols), lambda i, q: (i, 0))],
            out_specs=pl.BlockSpec((1, tr, cols), lambda i, q: (q[0], i, 0))),
        out_shape=_sds((N_CHIPS, rows, cols), BF16),
        compiler_params=_params(("arbitrary",)),
    )(q_arr, w)


def _gather_shards(fulls):
    n = len(fulls)
    halves = [a.shape[1] // 2 for a in fulls]

    def body(*refs):
        full = refs[n:2 * n]
        ssem, rsem = refs[2 * n:]
        x, y, c = _mesh_pos()
        chips = _other_chips(x, y)

        def remote(i, chip_no, half, to, s):
            part = full[i].at[chip_no, _half_rows(half, halves[i]), :]
            return pltpu.make_async_remote_copy(src_ref=part, dst_ref=part, send_sem=ssem.at[s], recv_sem=rsem.at[s],
                                                device_id=to, device_id_type=MESH)

        first = []
        for i in range(n):
            for j, (px, py, _) in enumerate(chips):
                cp = remote(i, 2 * x + y, c, (px, py, c), 3 * i + j)
                cp.start()
                first.append(cp)
        passed = []
        for i in range(n):
            for j, (_, _, qj) in enumerate(chips):
                remote(i, qj, c, (x, y, c), 3 * i + j).wait_recv()
                cp = remote(i, qj, c, (x, y, 1 - c), 3 * n + 3 * i + j)
                cp.start()
                passed.append(cp)
        for i in range(n):
            for j, (_, _, qj) in enumerate(chips):
                remote(i, qj, 1 - c, (x, y, c), 3 * n + 3 * i + j).wait_recv()
        for cp in first + passed:
            cp.wait_send()

    return pl.pallas_call(
        body, name="gather_shards",
        in_specs=[ANY] * n, out_specs=[ANY] * n,
        out_shape=[_sds(a.shape, a.dtype) for a in fulls],
        input_output_aliases={i: i for i in range(n)},
        scratch_shapes=[pltpu.SemaphoreType.DMA((6 * n,)), pltpu.SemaphoreType.DMA((6 * n,))],
    )(*fulls)


HBM = pl.BlockSpec(memory_space=pltpu.HBM)
SEM = pl.BlockSpec(memory_space=pltpu.SEMAPHORE)
EFFECT = pltpu.SideEffectType.DATAFLOW_SIDE_EFFECTING


def _in_hbm(a):
    return pltpu.with_memory_space_constraint(a, pltpu.HBM)


def _gather_start(fulls, after):
    n = len(fulls)
    halves = [a.shape[1] // 2 for a in fulls]

    def body(*refs):
        land = refs[:n]
        ssem, rsem = refs[n + 1], refs[n + 2]
        token = refs[-1]
        x, y, c = _mesh_pos()
        q = 2 * x + y
        for i in range(n):
            for j, (px, py, _) in enumerate(_other_chips(x, y)):
                mine = land[i].at[q, _half_rows(c, halves[i]), :]
                pltpu.make_async_remote_copy(src_ref=mine, dst_ref=mine, send_sem=ssem.at[3 * i + j],
                                             recv_sem=rsem.at[3 * i + j], device_id=(px, py, c), device_id_type=MESH).start()
        token[...] = jnp.zeros(token.shape, F32)

    outs = pl.pallas_call(
        body, name="gather_start",
        in_specs=[HBM] * n + [ANY], out_specs=[SEM, SEM] + [HBM] * n + [VMEM],
        out_shape=[pltpu.SemaphoreType.DMA((3 * n,)), pltpu.SemaphoreType.DMA((3 * n,))]
        + [pltpu.HBM(a.shape, a.dtype) for a in fulls] + [_sds((8, LANES), F32)],
        input_output_aliases={i: 2 + i for i in range(n)},
        compiler_params=pltpu.CompilerParams(has_side_effects=EFFECT),
    )(*[_in_hbm(a) for a in fulls], after)
    return outs[0], outs[1], list(outs[2:2 + n]), outs[-1]


def _gather_wait(which, ssem, rsem, lands, after, tag):
    m = len(which)
    halves = [a.shape[1] // 2 for a in lands]

    def body(*refs):
        land = refs[:m]
        ssem_, rsem_ = refs[m], refs[m + 1]
        x, y, c = _mesh_pos()
        for t, i in enumerate(which):
            for j, (px, py, qj) in enumerate(_other_chips(x, y)):
                rows = _half_rows(c, halves[t])
                cp = pltpu.make_async_remote_copy(src_ref=land[t].at[2 * x + y, rows, :], dst_ref=land[t].at[qj, rows, :],
                                                  send_sem=ssem_.at[3 * i + j], recv_sem=rsem_.at[3 * i + j],
                                                  device_id=(px, py, c), device_id_type=MESH)
                cp.wait_send()
                cp.wait_recv()

    outs = pl.pallas_call(
        body, name="gather_wait_" + tag,
        in_specs=[HBM] * m + [SEM, SEM, ANY], out_specs=[HBM] * m,
        out_shape=[pltpu.HBM(a.shape, a.dtype) for a in lands],
        input_output_aliases={i: i for i in range(m)},
        compiler_params=pltpu.CompilerParams(has_side_effects=EFFECT),
    )(*lands, ssem, rsem, after)
    return list(outs)


def _forward_pair(lands, tag):
    n = len(lands)
    halves = [a.shape[1] // 2 for a in lands]

    def body(*refs):
        full = refs[n:2 * n]
        ssem, rsem = refs[2 * n:]
        x, y, c = _mesh_pos()
        _pair_barrier(x, y, c)
        cps = []
        for i in range(n):
            for j, (_, _, qj) in enumerate(_other_chips(x, y)):
                part = full[i].at[qj, _half_rows(c, halves[i]), :]
                cp = pltpu.make_async_remote_copy(src_ref=part, dst_ref=part, send_sem=ssem.at[3 * i + j],
                                                  recv_sem=rsem.at[3 * i + j], device_id=(x, y, 1 - c), device_id_type=MESH)
                cp.start()
                cps.append(cp)
        for cp in cps:
            cp.wait()

    return pl.pallas_call(
        body, name="forward_pair_" + tag,
        in_specs=[ANY] * n, out_specs=[ANY] * n,
        out_shape=[_sds(a.shape, a.dtype) for a in lands],
        input_output_aliases={i: i for i in range(n)},
        scratch_shapes=[pltpu.SemaphoreType.DMA((3 * n,)), pltpu.SemaphoreType.DMA((3 * n,))],
        compiler_params=_pair_params(),
    )(*lands)


def _chip_exchange_start(parts, after, tag):
    n = len(parts)

    def body(*refs):
        src, land = refs[:n], refs[n:2 * n]
        ssem, rsem = refs[2 * n + 1], refs[2 * n + 2]
        token = refs[-1]
        x, y, c = _mesh_pos()
        for i in range(n):
            for j, (px, py, qj) in enumerate(_other_chips(x, y)):
                pltpu.make_async_remote_copy(src_ref=src[i].at[qj], dst_ref=land[i].at[j], send_sem=ssem.at[3 * i + j],
                                             recv_sem=rsem.at[3 * i + j], device_id=(px, py, c), device_id_type=MESH).start()
        token[...] = jnp.zeros(token.shape, F32)

    lands = [lax.empty((3,) + a.shape[1:], a.dtype) for a in parts]
    outs = pl.pallas_call(
        body, name="chip_exchange_start_" + tag,
        in_specs=[HBM] * (2 * n) + [ANY], out_specs=[SEM, SEM] + [HBM] * (2 * n) + [VMEM],
        out_shape=[pltpu.SemaphoreType.DMA((3 * n,)), pltpu.SemaphoreType.DMA((3 * n,))]
        + [pltpu.HBM(a.shape, a.dtype) for a in parts] + [pltpu.HBM(a.shape, a.dtype) for a in lands]
        + [_sds((8, LANES), F32)],
        input_output_aliases={i: 2 + i for i in range(2 * n)},
        compiler_params=pltpu.CompilerParams(has_side_effects=EFFECT),
    )(*[_in_hbm(a) for a in parts], *[_in_hbm(a) for a in lands], after)
    return outs[0], outs[1], list(outs[2:2 + n]), list(outs[2 + n:2 + 2 * n]), outs[-1]


def _chip_exchange_wait(ssem, rsem, parts, lands, after, tag):
    n = len(parts)

    def body(*refs):
        src, land = refs[:n], refs[n:2 * n]
        ssem_, rsem_ = refs[2 * n], refs[2 * n + 1]
        x, y, c = _mesh_pos()
        for i in range(n):
            for j, (px, py, qj) in enumerate(_other_chips(x, y)):
                cp = pltpu.make_async_remote_copy(src_ref=src[i].at[qj], dst_ref=land[i].at[j], send_sem=ssem_.at[3 * i + j],
                                                  recv_sem=rsem_.at[3 * i + j], device_id=(px, py, c), device_id_type=MESH)
                cp.wait_send()
                cp.wait_recv()

    outs = pl.pallas_call(
        body, name="chip_exchange_wait_" + tag,
        in_specs=[HBM] * (2 * n) + [SEM, SEM, ANY], out_specs=[HBM] * (2 * n),
        out_shape=[pltpu.HBM(a.shape, a.dtype) for a in parts] + [pltpu.HBM(a.shape, a.dtype) for a in lands],
        input_output_aliases={i: i for i in range(2 * n)},
        compiler_params=pltpu.CompilerParams(has_side_effects=EFFECT),
    )(*parts, *lands, ssem, rsem, after)
    return list(outs[:n]), list(outs[n:])


def _pair_exchange_grads(grads, half_axis, tag):
    n = len(grads)

    def half_of(ref, i, which):
        rows = grads[i].shape[half_axis[i]] // 2
        if half_axis[i] == 0:
            return ref.at[_half_rows(which, rows), :]
        return ref.at[:, _half_rows(which, rows), :]

    def out_shape(i):
        s = list(grads[i].shape)
        s[half_axis[i]] //= 2
        return _sds(tuple(s), grads[i].dtype)

    def body(*refs):
        g, got = refs[:n], refs[n:2 * n]
        ssem, rsem = refs[2 * n:]
        x, y, c = _mesh_pos()
        _pair_barrier(x, y, c)
        cps = []
        for i in range(n):
            cp = pltpu.make_async_remote_copy(src_ref=half_of(g[i], i, 1 - c), dst_ref=got[i], send_sem=ssem.at[i],
                                              recv_sem=rsem.at[i], device_id=(x, y, 1 - c), device_id_type=MESH)
            cp.start()
            cps.append(cp)
        for cp in cps:
            cp.wait()

    return pl.pallas_call(
        body, name="pair_exchange_grads_" + tag,
        in_specs=[ANY] * n, out_specs=[ANY] * n,
        out_shape=[out_shape(i) for i in range(n)],
        scratch_shapes=[pltpu.SemaphoreType.DMA((n,)), pltpu.SemaphoreType.DMA((n,))],
        compiler_params=_pair_params(),
    )(*grads)


def _pair_sum(g, got, c_arr, col_sharded, tag):
    if col_sharded:
        rows, cols = g.shape
        rh, cs = rows // 2, cols // N_CHIPS
        g_spec = pl.BlockSpec((rh, cs), lambda k, c_ref: (c_ref[0], k))
        got_spec = pl.BlockSpec((rh, cs), lambda k, c_ref: (0, k))
    else:
        _, rows, cs = g.shape
        rh = rows // 2
        g_spec = pl.BlockSpec((1, rh, cs), lambda k, c_ref: (k, c_ref[0], 0))
        got_spec = pl.BlockSpec((1, rh, cs), lambda k, c_ref: (k, 0, 0))

    def body(c_ref, g_ref, got_ref, out_ref):
        total = g_ref[...].astype(F32) + got_ref[...].astype(F32)
        out_ref[...] = total.astype(BF16).reshape(out_ref.shape)

    return pl.pallas_call(
        body, name="pair_sum_" + tag,
        grid_spec=pltpu.PrefetchScalarGridSpec(
            num_scalar_prefetch=1, grid=(N_CHIPS,), in_specs=[g_spec, got_spec],
            out_specs=pl.BlockSpec((1, rh, cs), lambda k, c_ref: (k, 0, 0))),
        out_shape=_sds((N_CHIPS, rh, cs), BF16),
        compiler_params=_params(("arbitrary",)),
    )(c_arr, g, got)


def _chip_sum(part, got, qc_arr, tag):
    _, rh, cs = part.shape

    def body(qc_ref, part_ref, got_ref, out_ref):
        total = part_ref[0].astype(F32)
        for j in range(3):
            total = total + got_ref[j].astype(F32)
        out_ref[...] = total

    return pl.pallas_call(
        body, name="chip_sum_" + tag,
        grid_spec=pltpu.PrefetchScalarGridSpec(
            num_scalar_prefetch=1, grid=(1,),
            in_specs=[pl.BlockSpec((1, rh, cs), lambda i, qc: (qc[0], 0, 0)), pl.BlockSpec((3, rh, cs), lambda i, qc: (0, 0, 0))],
            out_specs=pl.BlockSpec((rh, cs), lambda i, qc: (qc[1], 0))),
        out_shape=_sds((2 * rh, cs), F32),
        compiler_params=_params(("arbitrary",)),
    )(qc_arr, part, got)


def _pair_share_grads(grads, tag):
    n = len(grads)

    def body(*refs):
        g = refs[n:2 * n]
        ssem, rsem = refs[2 * n:]
        x, y, c = _mesh_pos()
        _pair_barrier(x, y, c)
        cps = []
        for i in range(n):
            mine = g[i].at[_half_rows(c, grads[i].shape[0] // 2), :]
            cp = pltpu.make_async_remote_copy(src_ref=mine, dst_ref=mine, send_sem=ssem.at[i], recv_sem=rsem.at[i],
                                              device_id=(x, y, 1 - c), device_id_type=MESH)
            cp.start()
            cps.append(cp)
        for cp in cps:
            cp.wait()

    return pl.pallas_call(
        body, name="pair_share_grads_" + tag,
        in_specs=[ANY] * n, out_specs=[ANY] * n,
        out_shape=[_sds(a.shape, a.dtype) for a in grads],
        input_output_aliases={i: i for i in range(n)},
        scratch_shapes=[pltpu.SemaphoreType.DMA((n,)), pltpu.SemaphoreType.DMA((n,))],
        compiler_params=_pair_params(),
    )(*grads)


def _small_allreduce(parts, places, rows_total, width):
    n = len(parts)

    def body(*refs):
        ins, out_ref = refs[:n], refs[n]
        pack, pair_got, chip_sum, got, ssem, rsem = refs[n + 1:]
        x, y, c = _mesh_pos()
        chip = 2 * x + y
        pack[...] = jnp.zeros(pack.shape, F32)
        for i in range(n):
            for row, col, src_row, rows in places[i]:
                w = parts[i].shape[1]
                pack[row:row + rows, col:col + w] = ins[i][src_row:src_row + rows, :]
        swap = pltpu.make_async_remote_copy(src_ref=pack, dst_ref=pair_got, send_sem=ssem.at[3], recv_sem=rsem.at[3],
                                            device_id=(x, y, 1 - c), device_id_type=MESH)
        swap.start()
        swap.wait()
        chip_sum[...] = pack[...] + pair_got[...]
        cps = []
        for j, (px, py, _) in enumerate(_other_chips(x, y)):
            cp = pltpu.make_async_remote_copy(src_ref=chip_sum, dst_ref=got.at[j], send_sem=ssem.at[j],
                                              recv_sem=rsem.at[j], device_id=(px, py, c), device_id_type=MESH)
            cp.start()
            cps.append(cp)
        for cp in cps:
            cp.wait()
        total = jnp.zeros(pack.shape, F32)
        for q in range(N_CHIPS):
            rel = jnp.bitwise_xor(chip, q)
            theirs = got[jnp.maximum(rel - 1, 0)]
            total = total + jnp.where(rel == 0, chip_sum[...], theirs)
        out_ref[...] = total

    return pl.pallas_call(
        body, name="small_allreduce",
        in_specs=[VMEM] * n, out_specs=VMEM,
        out_shape=_sds((rows_total, width), F32),
        scratch_shapes=[pltpu.VMEM((rows_total, width), F32), pltpu.VMEM((rows_total, width), F32),
                        pltpu.VMEM((rows_total, width), F32), pltpu.VMEM((3, rows_total, width), F32),
                        pltpu.SemaphoreType.DMA((4,)), pltpu.SemaphoreType.DMA((4,))],
        compiler_params=_params(),
    )(*parts)


def _adamw_math(w, g, m, v):
    m2 = ADAM_B1 * m + (1.0 - ADAM_B1) * g
    v2 = ADAM_B2 * v + (1.0 - ADAM_B2) * (g * g)
    m_hat = m2 / (1.0 - ADAM_B1 ** ADAM_STEP)
    v_hat = v2 / (1.0 - ADAM_B2 ** ADAM_STEP)
    delta = -ADAM_LR * (m_hat / (jnp.sqrt(v_hat) + ADAM_EPS) + ADAM_WD * w)
    return delta, m2, v2


def _adamw_big(w, g, m, v, tag):
    rows, cols = w.shape
    tr = _row_tile(rows)

    def body(w_ref, g_ref, m_ref, v_ref, go_ref, d_ref, m2_ref, v2_ref):
        gg = g_ref[...]
        go_ref[...] = gg
        d_ref[...], m2_ref[...], v2_ref[...] = _adamw_math(w_ref[...], gg, m_ref[...], v_ref[...])

    spec = pl.BlockSpec((tr, cols), lambda i: (i, 0))
    return pl.pallas_call(
        body, name="adamw_" + tag, grid=(rows // tr,),
        in_specs=[spec] * 4, out_specs=[spec] * 4,
        out_shape=[_sds((rows, cols), F32)] * 4,
        compiler_params=_params(("arbitrary",)),
    )(w, g, m, v)


def _adamw_small(ws, gs, ms, vs):
    n = len(ws)

    def body(*refs):
        w_r, g_r, m_r, v_r = refs[:n], refs[n:2 * n], refs[2 * n:3 * n], refs[3 * n:4 * n]
        d_o, m_o, v_o = refs[4 * n:5 * n], refs[5 * n:6 * n], refs[6 * n:7 * n]
        for i in range(n):
            d_o[i][...], m_o[i][...], v_o[i][...] = _adamw_math(w_r[i][...], g_r[i][...], m_r[i][...], v_r[i][...])

    shapes = [_sds(w.shape, F32) for w in ws]
    outs = pl.pallas_call(
        body, name="adamw_small",
        in_specs=[VMEM] * (4 * n), out_specs=[VMEM] * (3 * n),
        out_shape=shapes * 3,
        compiler_params=_params(),
    )(*ws, *gs, *ms, *vs)
    return outs[:n], outs[n:2 * n], outs[2 * n:]


SMALL_ROWS = 40
PACK_ROWS = 64


def kernel(x, meta_tokens, pre_mix_norm, w_in, conv_a_w, conv_b_w, conv_b_bias, ln_b_gain, ln_b_bias, w_out, post_mix_norm, pre_ffn_norm, w_gate, w_up, w_down, post_ffn_norm, loss_target, m_meta_tokens, m_pre_mix_norm, m_w_in, m_conv_a_w, m_conv_b_w, m_conv_b_bias, m_ln_b_gain, m_ln_b_bias, m_w_out, m_post_mix_norm, m_pre_ffn_norm, m_w_gate, m_w_up, m_w_down, m_post_ffn_norm, v_meta_tokens, v_pre_mix_norm, v_w_in, v_conv_a_w, v_conv_b_w, v_conv_b_bias, v_ln_b_gain, v_ln_b_bias, v_w_out, v_post_mix_norm, v_pre_ffn_norm, v_w_gate, v_w_up, v_w_down, v_post_ffn_norm):
    xq, yq, cq = lax.axis_index("x"), lax.axis_index("y"), lax.axis_index("c")
    chip = 2 * xq + yq
    c_arr = jnp.reshape(cq, (1,)).astype(jnp.int32)
    qc_arr = jnp.stack([chip, cq]).astype(jnp.int32)

    seq, d = x.shape[1], x.shape[2]
    x2, tgt2 = x[0], loss_target[0]
    tr = lambda a: jnp.swapaxes(a, 1, 2)[0]
    w_in2, w_out2, w_gate2, w_up2, w_down2 = w_in[0], w_out[0], tr(w_gate), tr(w_up), w_down[0]
    ka, wa_sh = conv_a_w.shape[1], conv_a_w.shape[2]
    kb = conv_b_w.shape[1]
    meta_sh = meta_tokens.shape[1]

    small = jnp.zeros((PACK_ROWS, meta_sh), F32)
    small = small.at[0:N_META, :].set(meta_tokens)
    small = small.at[16:16 + ka, 0:wa_sh].set(conv_a_w[0])
    small = small.at[24:24 + kb, 0:wa_sh].set(conv_b_w[0])
    q_arr = jnp.reshape(chip, (1,)).astype(jnp.int32)
    small_own = lax.dynamic_update_slice(jnp.zeros((N_CHIPS, PACK_ROWS, meta_sh), F32), small[None], (chip, 0, 0))
    win4, small4 = _gather_shards([_cast_place(w_in2, q_arr, "w_in"), small_own])
    rest = [_cast_place(w, q_arr, nm) for w, nm in ((w_out2, "w_out"), (w_gate2, "w_gate"), (w_up2, "w_up"), (w_down2, "w_down"))]
    g_ssem, g_rsem, lands, g_token = _gather_start(rest, win4)
    meta_f = jnp.concatenate([small4[k, 0:N_META, :] for k in range(N_CHIPS)], axis=1)
    wa_f = jnp.concatenate([small4[k, 16:16 + ka, 0:wa_sh] for k in range(N_CHIPS)], axis=1)
    wb_f = jnp.concatenate([small4[k, 24:24 + kb, 0:wa_sh] for k in range(N_CHIPS)], axis=1)

    tm = _row_tile(seq + TAIL_ROWS)
    tail = lax.dynamic_update_slice(jnp.zeros((tm, d), F32), meta_f, (seq % tm, 0))
    h, xn1, hp5 = _mm_in(x2, tail, win4, pre_mix_norm + g_token[0:1, 0:1])
    ya, z = _mix_conv_fwd(hp5, wa_f, wb_f, conv_b_bias)
    (wout4,) = _forward_pair(_gather_wait([0], g_ssem, g_rsem, lands[0:1], z, "out"), "out")
    wout_f = wout4.reshape(N_CHIPS * wout4.shape[1], wout4.shape[2])
    yb, mix, h1, xn2 = _mm_out(ya, z, h, wout_f, ln_b_gain, ln_b_bias, post_mix_norm, pre_ffn_norm)
    wg4, wu4 = _forward_pair(_gather_wait([1, 2], g_ssem, g_rsem, lands[1:3], xn2, "gate_up"), "gate_up")
    stacked = lambda a: a.reshape(a.shape[0] * a.shape[1], a.shape[2])
    wg_f, wu_f = stacked(wg4), stacked(wu4)
    p_act, q_act, f_act = _ffn_up(xn2, wg_f, wu_f)
    (wd4,) = _forward_pair(_gather_wait([3], g_ssem, g_rsem, lands[3:4], f_act, "down"), "down")
    wd_f = stacked(wd4)
    dff, dh2, loss_blk, d_gpf = _ffn_down(f_act, wd_f, h1, tgt2, post_ffn_norm)

    da, du = _ffn_bwd_act(dff, wd_f, p_act, q_act)
    by_chip = lambda g: g.reshape(N_CHIPS, g.shape[0] // N_CHIPS, g.shape[1])
    g_down = by_chip(_grad_w_down(f_act, dff))
    g_gate, g_up = [by_chip(g) for g in _grad_w_gate_up(xn2, da, du)]
    ffn = [g_gate, g_up, g_down]
    got = _pair_exchange_grads(ffn, [1, 1, 1], "ffn")
    parts = [_pair_sum(ffn[i], got[i], c_arr, False, ("gate", "up", "down")[i]) for i in range(3)]
    f_ssem, f_rsem, parts, f_lands, f_token = _chip_exchange_start(parts, dff, "ffn")
    dh1, dmix, d_g2, d_gpm = _ffn_bwd_in(da, du, wg_f, wu_f, h1, mix, dh2, pre_ffn_norm + f_token[0:1, 0:1], post_mix_norm)
    g_out = _grad_w_out(ya, yb, dmix)
    dya, dz, d_lg, d_lb = _mix_bwd_out(dmix, wout_f, z, ln_b_gain, ln_b_bias)
    dhp5, d_wa, d_wb, d_bb = _mix_conv_bwd(hp5, dya, dz, wa_f, wb_f)
    g_in = _grad_w_in(xn1, dhp5)
    grad_x2, d_meta, d_g1 = _mix_bwd_in(dhp5, win4, h, dh1, pre_mix_norm)
    grad_x = grad_x2[None]

    g_out4 = g_out.reshape(N_CHIPS, g_out.shape[0] // N_CHIPS, g_out.shape[1])
    mixw = [g_in, g_out4]
    got2 = _pair_exchange_grads(mixw, [0, 1], "mix")
    parts2 = [_pair_sum(mixw[i], got2[i], c_arr, i == 0, ("in", "out")[i]) for i in range(2)]
    m_ssem, m_rsem, parts2, m_lands, m_token = _chip_exchange_start(parts2, d_g1, "mix")

    parts, f_recv = _chip_exchange_wait(f_ssem, f_rsem, parts, f_lands, m_token, "ffn")
    halves = [_chip_sum(parts[i], f_recv[i], qc_arr, ("gate", "up", "down")[i]) for i in range(3)]
    gsum_ffn = _pair_share_grads(halves, "ffn")

    hw = d // 2
    assert d_wa.shape == (3, hw) and d_wb.shape == (31, hw) and d_bb.shape == (1, hw)
    small_parts = [d_meta, d_g1, d_gpm, d_g2, d_gpf, d_bb, d_lg, d_lb, loss_blk[0:1, :], d_wa, d_wb]
    places = [[(0, 0, 0, N_META)], [(16, 0, 0, 1)], [(17, 0, 0, 1)], [(18, 0, 0, 1)], [(19, 0, 0, 1)],
              [(20, 0, 0, 1)], [(20, hw, 0, 1)], [(21, 0, 0, 1)], [(21, hw, 0, 1)], [(22, 0, 0, 3)],
              [(22, hw, 0, 3), (25, 0, 3, 14), (25, hw, 17, 14)]]
    red = _small_allreduce(small_parts, places, SMALL_ROWS, d)
    s_meta, s_g1, s_gpm, s_g2, s_gpf = red[0:N_META], red[16:17], red[17:18], red[18:19], red[19:20]
    s_bb, s_lg, s_lb, s_loss = red[20:21, 0:hw], red[20:21, hw:d], red[21:22, 0:hw], red[21:22, hw:hw + LANES]
    s_wa = red[22:25, 0:hw]
    s_wb = jnp.concatenate([red[22:25, hw:d], red[25:39, 0:hw], red[25:39, hw:d]], axis=0)
    g_meta = lax.dynamic_slice_in_dim(s_meta, chip * meta_sh, meta_sh, axis=1)
    g_wa = lax.dynamic_slice_in_dim(s_wa, chip * wa_sh, wa_sh, axis=1)[None]
    g_wb = lax.dynamic_slice_in_dim(s_wb, chip * wa_sh, wa_sh, axis=1)[None]
    loss = s_loss[0, 0]

    names_big = ["w_in", "w_out", "w_gate", "w_up", "w_down"]
    w_big = dict(zip(names_big, [w_in2, w_out2, w_gate2, w_up2, w_down2]))
    m_big = dict(zip(names_big, [m_w_in[0], m_w_out[0], tr(m_w_gate), tr(m_w_up), m_w_down[0]]))
    v_big = dict(zip(names_big, [v_w_in[0], v_w_out[0], tr(v_w_gate), tr(v_w_up), v_w_down[0]]))
    grads, deltas, new_m, new_v = {}, {}, {}, {}

    def update(nm, g):
        outs = _adamw_big(w_big[nm], g, m_big[nm], v_big[nm], nm)
        if nm in ("w_gate", "w_up"):
            outs = [jnp.swapaxes(o[None], 1, 2) for o in outs]
        else:
            outs = [o[None] for o in outs]
        grads[nm], deltas[nm], new_m[nm], new_v[nm] = outs
        return outs[1]

    last = None
    for nm, g in zip(["w_gate", "w_up", "w_down"], gsum_ffn):
        last = update(nm, g)

    parts2, m_recv = _chip_exchange_wait(m_ssem, m_rsem, parts2, m_lands, last, "mix")
    halves2 = [_chip_sum(parts2[i], m_recv[i], qc_arr, ("in", "out")[i]) for i in range(2)]
    gsum_mix = _pair_share_grads(halves2, "mix")
    for nm, g in zip(["w_in", "w_out"], gsum_mix):
        update(nm, g)

    names_small = ["meta_tokens", "pre_mix_norm", "conv_a_w", "conv_b_w", "conv_b_bias", "ln_b_gain", "ln_b_bias",
                   "post_mix_norm", "pre_ffn_norm", "post_ffn_norm"]
    w_small = [meta_tokens, pre_mix_norm, conv_a_w[0], conv_b_w[0], conv_b_bias, ln_b_gain, ln_b_bias, post_mix_norm,
               pre_ffn_norm, post_ffn_norm]
    g_small = [g_meta, s_g1, g_wa[0], g_wb[0], s_bb, s_lg, s_lb, s_gpm, s_g2, s_gpf]
    m_small = [m_meta_tokens, m_pre_mix_norm, m_conv_a_w[0], m_conv_b_w[0], m_conv_b_bias, m_ln_b_gain, m_ln_b_bias,
               m_post_mix_norm, m_pre_ffn_norm, m_post_ffn_norm]
    v_small = [v_meta_tokens, v_pre_mix_norm, v_conv_a_w[0], v_conv_b_w[0], v_conv_b_bias, v_ln_b_gain, v_ln_b_bias,
               v_post_mix_norm, v_pre_ffn_norm, v_post_ffn_norm]
    d_s, m_s, v_s = _adamw_small(w_small, g_small, m_small, v_small)
    for i, nm in enumerate(names_small):
        lead = nm in ("conv_a_w", "conv_b_w")
        fix = (lambda a: a[None]) if lead else (lambda a: a)
        grads[nm], deltas[nm], new_m[nm], new_v[nm] = fix(g_small[i]), fix(d_s[i]), fix(m_s[i]), fix(v_s[i])

    order = ["meta_tokens", "pre_mix_norm", "w_in", "conv_a_w", "conv_b_w", "conv_b_bias", "ln_b_gain", "ln_b_bias", "w_out",
             "post_mix_norm", "pre_ffn_norm", "w_gate", "w_up", "w_down", "post_ffn_norm"]
    return (loss, grad_x, *[grads[k] for k in order], *[deltas[k] for k in order], *[new_m[k] for k in order],
            *[new_v[k] for k in order])
```

```python
import functools

import jax
import jax.numpy as jnp
from jax import lax
from jax.experimental import pallas as pl
from jax.experimental.pallas import tpu as pltpu

F32 = jnp.float32
BF16 = jnp.bfloat16
MESH = pl.DeviceIdType.MESH

N_META = 16
TAIL_ROWS = 128
RMS_EPS = 1e-6
LN_EPS = 1e-5
ADAM_LR = 0.001
ADAM_B1 = 0.9
ADAM_B2 = 0.999
ADAM_EPS = 1e-08
ADAM_WD = 0.01
ADAM_STEP = 10

N_CHIPS = 4
LANES = 128
MXU_TILE = 256
CONV_CHUNK = 48
CONV_HIST = 32
ROW_TILE_CAP = 640
VMEM_LIMIT = 56 * 1024 * 1024

NN = (((1,), (0,)), ((), ()))
NT = (((1,), (1,)), ((), ()))
TN = (((0,), (0,)), ((), ()))


def _dot(a, b, dims=NN):
    return lax.dot_general(a, b, dims, preferred_element_type=F32)


def _sig(v):
    return 1.0 / (1.0 + jnp.exp(-v))


def _mean(v):
    return jnp.mean(v, axis=-1, keepdims=True)


def _row_tile(rows):
    best = 16
    for t in range(16, min(rows, ROW_TILE_CAP) + 1, 16):
        if rows % t == 0:
            best = t
    assert rows % best == 0
    return best


def _pieces(n_shards, shard_w, piece_w):
    total = n_shards * shard_w
    cuts = sorted(set(range(0, total + 1, shard_w)) | set(range(0, total + 1, piece_w)))
    out = []
    for lo, hi in zip(cuts[:-1], cuts[1:]):
        out.append((lo // shard_w, lo % shard_w, lo // piece_w, lo % piece_w, hi - lo))
    return out


def _params(semantics=None):
    kw = dict(vmem_limit_bytes=VMEM_LIMIT)
    if semantics is not None:
        kw["dimension_semantics"] = semantics
    return pltpu.CompilerParams(**kw)


def _full(shape):
    nd = len(shape)
    return pl.BlockSpec(shape, lambda *_: (0,) * nd)


def _resident(shape):
    nd = len(shape)
    return pl.BlockSpec(shape, lambda *_: (0,) * nd, pipeline_mode=pl.Buffered(1))


def _sds(shape, dtype):
    return jax.ShapeDtypeStruct(shape, dtype)


ANY = pl.BlockSpec(memory_space=pl.ANY)
VMEM = pl.BlockSpec(memory_space=pltpu.VMEM)


def _mesh_pos():
    return lax.axis_index("x"), lax.axis_index("y"), lax.axis_index("c")


def _flip(v, bit):
    return 1 - v if bit else v


def _mm_in(x, tail, win4, g1):
    seq, d = x.shape
    tp = seq + TAIL_ROWS
    tm = _row_tile(tp)
    n_sh, _, csh = win4.shape
    pw = n_sh * csh // 5
    pieces = _pieces(n_sh, csh, pw)

    def body(x_ref, tail_ref, w_ref, g_ref, h_ref, xn_ref, hp_ref):
        rows = pl.program_id(0) * tm + lax.broadcasted_iota(jnp.int32, (tm, 1), 0)
        hh = jnp.where(rows < seq, x_ref[...], tail_ref[...])
        h_ref[...] = hh
        r = lax.rsqrt(_mean(hh * hh) + RMS_EPS)
        xn = (hh * r * g_ref[...]).astype(BF16)
        xn_ref[...] = xn
        for k, klo, p, plo, w in pieces:
            hp_ref[p, :, plo:plo + w] = _dot(xn, w_ref[k, :, klo:klo + w])

    row = pl.BlockSpec((tm, d), lambda i: (i, 0))
    return pl.pallas_call(
        body, name="mm_in", grid=(tp // tm,),
        in_specs=[row, _full(tail.shape), _resident(win4.shape), _full(g1.shape)],
        out_specs=[row, row, pl.BlockSpec((5, tm, pw), lambda i: (0, i, 0))],
        out_shape=[_sds((tp, d), F32), _sds((tp, d), BF16), _sds((5, tp, pw), F32)],
        compiler_params=_params(("arbitrary",)),
    )(x, tail, win4, g1)


def _seq_rows(tp):
    seq = tp - TAIL_ROWS
    nseq = seq + N_META
    assert nseq % CONV_CHUNK == 0 and seq % 16 == 0
    return seq, nseq


SUBLANES = 8


def _conv_offsets(width, transpose):
    return [(width - 1 - k) if transpose else (CONV_HIST - (width - 1) + k) for k in range(width)]


def _shift_copies(src_ref, sh_ref, width, transpose):
    n = src_ref.shape[0] - SUBLANES
    for s in sorted({o % SUBLANES for o in _conv_offsets(width, transpose)} - {0}):
        sh_ref[s - 1, 0:n, :] = src_ref[s:s + n, :]


def _tap_rows(src_ref, sh_ref, base, off):
    start = pl.multiple_of(base + (off // SUBLANES) * SUBLANES, SUBLANES)
    if off % SUBLANES == 0:
        return src_ref[pl.ds(start, CONV_CHUNK), :]
    return sh_ref[off % SUBLANES - 1, pl.ds(start, CONV_CHUNK), :]


def _conv_taps(src_ref, sh_ref, w_ref, dst_ref, width, nseq, transpose):
    w = w_ref[...]
    offs = _conv_offsets(width, transpose)
    _shift_copies(src_ref, sh_ref, width, transpose)

    def step(n, carry):
        out0 = pl.multiple_of(CONV_HIST + n * CONV_CHUNK, SUBLANES)
        base = out0 if transpose else n * CONV_CHUNK
        acc = jnp.zeros((CONV_CHUNK, w.shape[1]), F32)
        for k, off in enumerate(offs):
            acc = acc + w[k:k + 1, :] * _tap_rows(src_ref, sh_ref, base, off)
        dst_ref[pl.ds(out0, CONV_CHUNK), :] = acc
        return carry

    lax.fori_loop(0, nseq // CONV_CHUNK, step, 0)


def _conv_wgrad(src_ref, sh_ref, dz_ref, acc_ref, width, nseq):
    acc_ref[...] = jnp.zeros(acc_ref.shape, F32)
    offs = _conv_offsets(width, False)

    def step(n, carry):
        dzc = dz_ref[pl.ds(pl.multiple_of(CONV_HIST + n * CONV_CHUNK, SUBLANES), CONV_CHUNK), :]
        for k, off in enumerate(offs):
            prod = dzc * _tap_rows(src_ref, sh_ref, n * CONV_CHUNK, off)
            part = prod[0:SUBLANES, :]
            for s in range(1, CONV_CHUNK // SUBLANES):
                part = part + prod[SUBLANES * s:SUBLANES * (s + 1), :]
            acc_ref[SUBLANES * k:SUBLANES * (k + 1), :] += part
        return carry

    lax.fori_loop(0, nseq // CONV_CHUNK, step, 0)


def _to_seq(buf_ref, x_part, meta_part, seq):
    buf_ref[CONV_HIST:CONV_HIST + N_META, :] = meta_part
    buf_ref[CONV_HIST + N_META:CONV_HIST + N_META + seq, :] = x_part


def _zero_ends(buf_ref, nseq):
    zeros = jnp.zeros((CONV_HIST, buf_ref.shape[1]), F32)
    buf_ref[0:CONV_HIST, :] = zeros
    buf_ref[CONV_HIST + nseq:CONV_HIST + nseq + CONV_HIST, :] = zeros


def _mix_conv_fwd(hp5, wa, wb, bb):
    _, tp, wgrp = hp5.shape
    seq, nseq = _seq_rows(tp)
    sb = nseq + 2 * CONV_HIST
    ka, kb = wa.shape[0], wb.shape[0]
    xs, ms = slice(0, seq), slice(seq, seq + N_META)
    ox, om = slice(CONV_HIST + N_META, CONV_HIST + nseq), slice(CONV_HIST, CONV_HIST + N_META)

    def body(hp_ref, wa_ref, wb_ref, bb_ref, ya_ref, z_ref, s_ref, o_ref, sh_ref):
        _zero_ends(s_ref, nseq)
        _to_seq(s_ref, hp_ref[1, xs, :] * hp_ref[2, xs, :], hp_ref[1, ms, :] * hp_ref[2, ms, :], seq)
        _conv_taps(s_ref, sh_ref, wa_ref, o_ref, ka, nseq, False)
        ya_ref[xs, :] = (hp_ref[0, xs, :] * o_ref[ox, :]).astype(BF16)
        ya_ref[ms, :] = (hp_ref[0, ms, :] * o_ref[om, :]).astype(BF16)
        ya_ref[seq + N_META:tp, :] = jnp.zeros((tp - seq - N_META, LANES), BF16)
        _to_seq(s_ref, hp_ref[3, xs, :] * _sig(hp_ref[4, xs, :]), hp_ref[3, ms, :] * _sig(hp_ref[4, ms, :]), seq)
        _conv_taps(s_ref, sh_ref, wb_ref, o_ref, kb, nseq, False)
        z_ref[xs, :] = o_ref[ox, :] + bb_ref[...]
        z_ref[ms, :] = o_ref[om, :] + bb_ref[...]
        z_ref[seq + N_META:tp, :] = jnp.zeros((tp - seq - N_META, LANES), F32)

    col = lambda j: (0, j)
    return pl.pallas_call(
        body, name="mix_conv_fwd", grid=(wgrp // LANES,),
        in_specs=[pl.BlockSpec((5, tp, LANES), lambda j: (0, 0, j)), pl.BlockSpec((ka, LANES), col),
                  pl.BlockSpec((kb, LANES), col), pl.BlockSpec((1, LANES), col)],
        out_specs=[pl.BlockSpec((tp, LANES), col), pl.BlockSpec((tp, LANES), col)],
        out_shape=[_sds((tp, wgrp), BF16), _sds((tp, wgrp), F32)],
        scratch_shapes=[pltpu.VMEM((sb, LANES), F32), pltpu.VMEM((sb, LANES), F32),
                        pltpu.VMEM((SUBLANES - 1, sb, LANES), F32)],
        compiler_params=_params(("arbitrary",)),
    )(hp5, wa, wb, bb)


def _layer_norm_parts(z, lg, lb):
    mu = _mean(z)
    zc = z - mu
    rl = lax.rsqrt(_mean(zc * zc) + LN_EPS)
    zh = zc * rl
    return rl, zh, zh * lg + lb


def _mm_out(ya, z, h, wout, lg, lb, gpm, g2):
    tp, d = h.shape
    wa_ = ya.shape[1]
    tm = _row_tile(tp)

    def body(ya_ref, z_ref, h_ref, w_ref, lg_ref, lb_ref, gpm_ref, g2_ref, yb_ref, mix_ref, h1_ref, xn2_ref):
        _, _, l = _layer_norm_parts(z_ref[...], lg_ref[...], lb_ref[...])
        yb = (l * _sig(l)).astype(BF16)
        yb_ref[...] = yb
        mix = _dot(ya_ref[...], w_ref[0:wa_, :]) + _dot(yb, w_ref[wa_:d, :])
        mix_ref[...] = mix
        rm = lax.rsqrt(_mean(mix * mix) + RMS_EPS)
        h1 = h_ref[...] + mix * rm * gpm_ref[...]
        h1_ref[...] = h1
        r2 = lax.rsqrt(_mean(h1 * h1) + RMS_EPS)
        xn2_ref[...] = (h1 * r2 * g2_ref[...]).astype(BF16)

    row = lambda i: (i, 0)
    return pl.pallas_call(
        body, name="mm_out", grid=(tp // tm,),
        in_specs=[pl.BlockSpec((tm, wa_), row), pl.BlockSpec((tm, wa_), row), pl.BlockSpec((tm, d), row),
                  _resident(wout.shape), _full(lg.shape), _full(lb.shape), _full(gpm.shape), _full(g2.shape)],
        out_specs=[pl.BlockSpec((tm, wa_), row), pl.BlockSpec((tm, d), row), pl.BlockSpec((tm, d), row),
                   pl.BlockSpec((tm, d), row)],
        out_shape=[_sds((tp, wa_), BF16), _sds((tp, d), F32), _sds((tp, d), F32), _sds((tp, d), BF16)],
        compiler_params=_params(("arbitrary",)),
    )(ya, z, h, wout, lg, lb, gpm, g2)


def _ffn_up(xn2, wg, wu):
    tp, d = xn2.shape
    ff_dim = wg.shape[0]
    tm = _row_tile(tp)
    assert ff_dim % MXU_TILE == 0

    def body(xn_ref, wg_ref, wu_ref, p_ref, q_ref, f_ref):
        xn = xn_ref[...]
        for lo in range(0, ff_dim, MXU_TILE):
            cols = slice(lo, lo + MXU_TILE)
            a = _dot(xn, wg_ref[cols, :], NT)
            u = _dot(xn, wu_ref[cols, :], NT)
            s = _sig(a)
            q = a * s
            p_ref[:, cols] = (u * (s + q * (1.0 - s))).astype(BF16)
            q_ref[:, cols] = q.astype(BF16)
            f_ref[:, cols] = (q * u).astype(BF16)

    ospec = pl.BlockSpec((tm, ff_dim), lambda i: (i, 0))
    return pl.pallas_call(
        body, name="ffn_up", grid=(tp // tm,),
        in_specs=[pl.BlockSpec((tm, d), lambda i: (i, 0)), _resident(wg.shape), _resident(wu.shape)],
        out_specs=[ospec, ospec, ospec],
        out_shape=[_sds((tp, ff_dim), BF16)] * 3,
        compiler_params=_params(("arbitrary",)),
    )(xn2, wg, wu)


def _ffn_down(f, wd, h1, tgt, gpf):
    tp, ff_dim = f.shape
    d = h1.shape[1]
    tm = _row_tile(tp)
    seq, _ = _seq_rows(tp)

    def body(f_ref, w_ref, h1_ref, t_ref, gpf_ref, dff_ref, dh2_ref, loss_ref, dgpf_ref):
        i = pl.program_id(0)
        ff = _dot(f_ref[...], w_ref[...])
        rf = lax.rsqrt(_mean(ff * ff) + RMS_EPS)
        nf = ff * rf
        gpf_ = gpf_ref[...]
        h2 = h1_ref[...] + nf * gpf_
        rows = i * tm + lax.broadcasted_iota(jnp.int32, (tm, 1), 0)
        err = jnp.where(rows < seq, h2 - t_ref[...], 0.0)
        dh2 = err * (1.0 / d)
        dh2_ref[...] = dh2
        dn = dh2 * gpf_
        dff_ref[...] = (rf * (dn - nf * _mean(dn * nf))).astype(BF16)

        @pl.when(i == 0)
        def _():
            loss_ref[...] = jnp.zeros(loss_ref.shape, F32)
            dgpf_ref[...] = jnp.zeros(dgpf_ref.shape, F32)

        loss_ref[...] += (0.5 / d) * jnp.sum(err * err, axis=(0, 1), keepdims=True)
        dgpf_ref[...] += jnp.sum(dh2 * nf, axis=0, keepdims=True)

    row = lambda i: (i, 0)
    return pl.pallas_call(
        body, name="ffn_down", grid=(tp // tm,),
        in_specs=[pl.BlockSpec((tm, ff_dim), row), _resident(wd.shape), pl.BlockSpec((tm, d), row),
                  pl.BlockSpec((tm, d), row), _full(gpf.shape)],
        out_specs=[pl.BlockSpec((tm, d), row), pl.BlockSpec((tm, d), row), _full((8, LANES)), _full((1, d))],
        out_shape=[_sds((tp, d), BF16), _sds((tp, d), F32), _sds((8, LANES), F32), _sds((1, d), F32)],
        compiler_params=_params(("arbitrary",)),
    )(f, wd, h1, tgt, gpf)


def _ffn_bwd_act(dff, wd, p, q):
    tp, d = dff.shape
    ff_dim = wd.shape[0]
    tm = _row_tile(tp)

    def body(dff_ref, w_ref, p_ref, q_ref, da_ref, du_ref):
        dffv = dff_ref[...]
        for lo in range(0, ff_dim, MXU_TILE):
            cols = slice(lo, lo + MXU_TILE)
            df = _dot(dffv, w_ref[cols, :], NT).astype(BF16)
            da_ref[:, cols] = df * p_ref[:, cols]
            du_ref[:, cols] = df * q_ref[:, cols]

    aspec = pl.BlockSpec((tm, ff_dim), lambda i: (i, 0))
    return pl.pallas_call(
        body, name="ffn_bwd_act", grid=(tp // tm,),
        in_specs=[pl.BlockSpec((tm, d), lambda i: (i, 0)), _resident(wd.shape), aspec, aspec],
        out_specs=[aspec, aspec],
        out_shape=[_sds((tp, ff_dim), BF16)] * 2,
        compiler_params=_params(("arbitrary",)),
    )(dff, wd, p, q)


def _grad_blocks(ff_dim):
    rows = ff_dim // 2
    assert rows % LANES == 0
    return rows


def _grad_w_down(f, dff):
    tp, ff_dim = f.shape
    d = dff.shape[1]
    rows = _grad_blocks(ff_dim)

    def body(f_ref, dff_ref, g_ref):
        g_ref[...] = _dot(f_ref[...], dff_ref[...], TN).astype(BF16)

    return pl.pallas_call(
        body, name="grad_w_down", grid=(ff_dim // rows,),
        in_specs=[pl.BlockSpec((tp, rows), lambda k: (0, k)), _resident(dff.shape)],
        out_specs=pl.BlockSpec((rows, d), lambda k: (k, 0)),
        out_shape=_sds((ff_dim, d), BF16),
        compiler_params=_params(("arbitrary",)),
    )(f, dff)


def _grad_w_gate_up(xn2, da, du):
    tp, ff_dim = da.shape
    d = xn2.shape[1]
    rows = _grad_blocks(ff_dim)

    def body(xn_ref, da_ref, du_ref, gg_ref, gu_ref):
        xn = xn_ref[...]
        gg_ref[...] = _dot(da_ref[...], xn, TN).astype(BF16)
        gu_ref[...] = _dot(du_ref[...], xn, TN).astype(BF16)

    aspec = pl.BlockSpec((tp, rows), lambda k: (0, k))
    gspec = pl.BlockSpec((rows, d), lambda k: (k, 0))
    return pl.pallas_call(
        body, name="grad_w_gate_up", grid=(ff_dim // rows,),
        in_specs=[_resident(xn2.shape), aspec, aspec],
        out_specs=[gspec, gspec],
        out_shape=[_sds((ff_dim, d), BF16)] * 2,
        compiler_params=_params(("arbitrary",)),
    )(xn2, da, du)


def _rms_bwd(dy, x, r, g):
    n = x * r
    dn = dy * g
    return r * (dn - n * _mean(dn * n)), dy * n


def _ffn_bwd_in(da, du, wg, wu, h1, mix, dh2, g2, gpm):
    tp, ff_dim = da.shape
    d = h1.shape[1]
    tm = _row_tile(tp)

    def body(da_ref, du_ref, wg_ref, wu_ref, h1_ref, mix_ref, dh2_ref, g2_ref, gpm_ref,
             dh1_ref, dmix_ref, dg2_ref, dgpm_ref):
        i = pl.program_id(0)
        dxn = _dot(da_ref[...], wg_ref[...]) + _dot(du_ref[...], wu_ref[...])

        @pl.when(i == 0)
        def _():
            dg2_ref[...] = jnp.zeros(dg2_ref.shape, F32)
            dgpm_ref[...] = jnp.zeros(dgpm_ref.shape, F32)

        h1v = h1_ref[...]
        r2 = lax.rsqrt(_mean(h1v * h1v) + RMS_EPS)
        dres, dg2_rows = _rms_bwd(dxn, h1v, r2, g2_ref[...])
        dh1 = dh2_ref[...] + dres
        dh1_ref[...] = dh1
        mixv = mix_ref[...]
        rm = lax.rsqrt(_mean(mixv * mixv) + RMS_EPS)
        dmix, dgpm_rows = _rms_bwd(dh1, mixv, rm, gpm_ref[...])
        dmix_ref[...] = dmix.astype(BF16)
        dg2_ref[...] += jnp.sum(dg2_rows, axis=0, keepdims=True)
        dgpm_ref[...] += jnp.sum(dgpm_rows, axis=0, keepdims=True)

    aspec = pl.BlockSpec((tm, ff_dim), lambda i: (i, 0))
    row = pl.BlockSpec((tm, d), lambda i: (i, 0))
    return pl.pallas_call(
        body, name="ffn_bwd_in", grid=(tp // tm,),
        in_specs=[aspec, aspec, _resident(wg.shape), _resident(wu.shape), row, row, row, _full(g2.shape), _full(gpm.shape)],
        out_specs=[row, row, _full((1, d)), _full((1, d))],
        out_shape=[_sds((tp, d), F32), _sds((tp, d), BF16), _sds((1, d), F32), _sds((1, d), F32)],
        compiler_params=_params(("arbitrary",)),
    )(da, du, wg, wu, h1, mix, dh2, g2, gpm)


def _grad_w_out(ya, yb, dmix):
    tp, wa_ = ya.shape
    d = dmix.shape[1]

    def body(ya_ref, yb_ref, dmix_ref, g_ref):
        dm = dmix_ref[...]
        g_ref[0:wa_, :] = _dot(ya_ref[...], dm, TN).astype(BF16)
        g_ref[wa_:2 * wa_, :] = _dot(yb_ref[...], dm, TN).astype(BF16)

    return pl.pallas_call(
        body, name="grad_w_out", grid=(1,),
        in_specs=[_full(ya.shape), _full(yb.shape), _full(dmix.shape)],
        out_specs=_full((2 * wa_, d)),
        out_shape=_sds((2 * wa_, d), BF16),
        compiler_params=_params(("arbitrary",)),
    )(ya, yb, dmix)


def _mix_bwd_out(dmix, wout, z, lg, lb):
    tp, d = dmix.shape
    wa_ = z.shape[1]
    tm = _row_tile(tp)

    def body(dmix_ref, w_ref, z_ref, lg_ref, lb_ref, dya_ref, dz_ref, dlg_ref, dlb_ref):
        i = pl.program_id(0)
        dm = dmix_ref[...]
        dya_ref[...] = _dot(dm, w_ref[0:wa_, :], NT)
        dyb = _dot(dm, w_ref[wa_:d, :], NT)
        lg_ = lg_ref[...]
        rl, zh, l = _layer_norm_parts(z_ref[...], lg_, lb_ref[...])
        sl = _sig(l)
        dl = dyb * (sl * (1.0 + l * (1.0 - sl)))
        dzh = dl * lg_
        dz_ref[...] = rl * (dzh - _mean(dzh) - zh * _mean(dzh * zh))

        @pl.when(i == 0)
        def _():
            dlg_ref[...] = jnp.zeros(dlg_ref.shape, F32)
            dlb_ref[...] = jnp.zeros(dlb_ref.shape, F32)

        dlg_ref[...] += jnp.sum(dl * zh, axis=0, keepdims=True)
        dlb_ref[...] += jnp.sum(dl, axis=0, keepdims=True)

    row = lambda i: (i, 0)
    return pl.pallas_call(
        body, name="mix_bwd_out", grid=(tp // tm,),
        in_specs=[pl.BlockSpec((tm, d), row), _resident(wout.shape), pl.BlockSpec((tm, wa_), row), _full(lg.shape),
                  _full(lb.shape)],
        out_specs=[pl.BlockSpec((tm, wa_), row), pl.BlockSpec((tm, wa_), row), _full((1, wa_)), _full((1, wa_))],
        out_shape=[_sds((tp, wa_), F32), _sds((tp, wa_), F32), _sds((1, wa_), F32), _sds((1, wa_), F32)],
        compiler_params=_params(("arbitrary",)),
    )(dmix, wout, z, lg, lb)


def _mix_conv_bwd(hp5, dya, dz, wa, wb):
    _, tp, wgrp = hp5.shape
    seq, nseq = _seq_rows(tp)
    sb = nseq + 2 * CONV_HIST
    ka, kb = wa.shape[0], wb.shape[0]
    xs, ms = slice(0, seq), slice(seq, seq + N_META)
    ox, om = slice(CONV_HIST + N_META, CONV_HIST + nseq), slice(CONV_HIST, CONV_HIST + N_META)
    n_tail = tp - seq - N_META

    def body(hp_ref, dya_ref, dz_ref, wa_ref, wb_ref, dhp_ref, dwa_ref, dwb_ref, dbb_ref, s_ref, d_ref, o_ref, acc_ref,
             shs_ref, shd_ref):
        _zero_ends(s_ref, nseq)
        _zero_ends(d_ref, nseq)

        def put(p, ox_val, om_val):
            dhp_ref[p, xs, :] = ox_val.astype(BF16)
            dhp_ref[p, ms, :] = om_val.astype(BF16)
            dhp_ref[p, seq + N_META:tp, :] = jnp.zeros((n_tail, LANES), BF16)

        def wgrad(dw_ref, width):
            for k in range(width):
                dw_ref[k:k + 1, :] = jnp.sum(acc_ref[8 * k:8 * k + 8, :], axis=0, keepdims=True)

        _to_seq(s_ref, hp_ref[1, xs, :] * hp_ref[2, xs, :], hp_ref[1, ms, :] * hp_ref[2, ms, :], seq)
        _conv_taps(s_ref, shs_ref, wa_ref, o_ref, ka, nseq, False)
        put(0, dya_ref[xs, :] * o_ref[ox, :], dya_ref[ms, :] * o_ref[om, :])
        _to_seq(d_ref, dya_ref[xs, :] * hp_ref[0, xs, :], dya_ref[ms, :] * hp_ref[0, ms, :], seq)
        _conv_wgrad(s_ref, shs_ref, d_ref, acc_ref, ka, nseq)
        wgrad(dwa_ref, ka)
        _conv_taps(d_ref, shd_ref, wa_ref, o_ref, ka, nseq, True)
        put(1, o_ref[ox, :] * hp_ref[2, xs, :], o_ref[om, :] * hp_ref[2, ms, :])
        put(2, o_ref[ox, :] * hp_ref[1, xs, :], o_ref[om, :] * hp_ref[1, ms, :])

        _to_seq(s_ref, hp_ref[3, xs, :] * _sig(hp_ref[4, xs, :]), hp_ref[3, ms, :] * _sig(hp_ref[4, ms, :]), seq)
        _to_seq(d_ref, dz_ref[xs, :], dz_ref[ms, :], seq)
        dbb_ref[...] = (jnp.sum(dz_ref[xs, :], axis=0, keepdims=True)
                        + jnp.sum(dz_ref[ms, :], axis=0, keepdims=True))
        _shift_copies(s_ref, shs_ref, kb, False)
        _conv_wgrad(s_ref, shs_ref, d_ref, acc_ref, kb, nseq)
        wgrad(dwb_ref, kb)
        _conv_taps(d_ref, shd_ref, wb_ref, o_ref, kb, nseq, True)
        sx, sm = _sig(hp_ref[4, xs, :]), _sig(hp_ref[4, ms, :])
        put(3, o_ref[ox, :] * sx, o_ref[om, :] * sm)
        put(4, o_ref[ox, :] * hp_ref[3, xs, :] * sx * (1.0 - sx), o_ref[om, :] * hp_ref[3, ms, :] * sm * (1.0 - sm))

    col = lambda j: (0, j)
    blk5 = pl.BlockSpec((5, tp, LANES), lambda j: (0, 0, j))
    return pl.pallas_call(
        body, name="mix_conv_bwd", grid=(wgrp // LANES,),
        in_specs=[blk5, pl.BlockSpec((tp, LANES), col), pl.BlockSpec((tp, LANES), col),
                  pl.BlockSpec((ka, LANES), col), pl.BlockSpec((kb, LANES), col)],
        out_specs=[blk5, pl.BlockSpec((ka, LANES), col), pl.BlockSpec((kb, LANES), col), pl.BlockSpec((1, LANES), col)],
        out_shape=[_sds((5, tp, wgrp), BF16), _sds((ka, wgrp), F32), _sds((kb, wgrp), F32), _sds((1, wgrp), F32)],
        scratch_shapes=[pltpu.VMEM((sb, LANES), F32), pltpu.VMEM((sb, LANES), F32), pltpu.VMEM((sb, LANES), F32),
                        pltpu.VMEM((SUBLANES * kb, LANES), F32), pltpu.VMEM((SUBLANES - 1, sb, LANES), F32),
                        pltpu.VMEM((SUBLANES - 1, sb, LANES), F32)],
        compiler_params=_params(("arbitrary",)),
    )(hp5, dya, dz, wa, wb)


def _grad_w_in(xn1, dhp5):
    n_p, tp, pw = dhp5.shape
    d = xn1.shape[1]

    def body(xn_ref, dhp_ref, g_ref):
        g_ref[...] = _dot(xn_ref[...], dhp_ref[0], TN).astype(BF16)

    return pl.pallas_call(
        body, name="grad_w_in", grid=(n_p,),
        in_specs=[_resident(xn1.shape), pl.BlockSpec((1, tp, pw), lambda p: (p, 0, 0))],
        out_specs=pl.BlockSpec((d, pw), lambda p: (0, p)),
        out_shape=_sds((d, n_p * pw), BF16),
        compiler_params=_params(("arbitrary",)),
    )(xn1, dhp5)


def _mix_bwd_in(dhp5, win4, h, dh1, g1):
    n_p, tp, pw = dhp5.shape
    d = h.shape[1]
    n_sh, _, csh = win4.shape
    tm = _row_tile(tp)
    pieces = _pieces(n_sh, csh, pw)

    seq, _ = _seq_rows(tp)
    last, meta_off = seq // tm, seq % tm
    assert last == tp // tm - 1 and meta_off + N_META <= tm

    def body(dhp_ref, w_ref, h_ref, dh1_ref, g_ref, gx_ref, dmeta_ref, dg1_ref):
        i = pl.program_id(0)
        dxn = None
        for k, klo, p, plo, w in pieces:
            t = _dot(dhp_ref[p, :, plo:plo + w], w_ref[k, :, klo:klo + w], NT)
            dxn = t if dxn is None else dxn + t
        hh = h_ref[...]
        r1 = lax.rsqrt(_mean(hh * hh) + RMS_EPS)
        dres, dg_rows = _rms_bwd(dxn, hh, r1, g_ref[...])
        dh = dh1_ref[...] + dres
        gx_ref[...] = dh

        @pl.when(i == 0)
        def _():
            dg1_ref[...] = jnp.zeros(dg1_ref.shape, F32)

        @pl.when(i == last)
        def _():
            dmeta_ref[...] = dh[meta_off:meta_off + N_META, :]

        dg1_ref[...] += jnp.sum(dg_rows, axis=0, keepdims=True)

    row = lambda i: (i, 0)
    return pl.pallas_call(
        body, name="mix_bwd_in", grid=(tp // tm,),
        in_specs=[pl.BlockSpec((n_p, tm, pw), lambda i: (0, i, 0)), _resident(win4.shape), pl.BlockSpec((tm, d), row),
                  pl.BlockSpec((tm, d), row), _full(g1.shape)],
        out_specs=[pl.BlockSpec((tm, d), row), _full((N_META, d)), _full((1, d))],
        out_shape=[_sds((seq, d), F32), _sds((N_META, d), F32), _sds((1, d), F32)],
        compiler_params=_params(("arbitrary",)),
    )(dhp5, win4, h, dh1, g1)


def _other_chips(x, y):
    out = []
    for j in (1, 2, 3):
        px, py = _flip(x, j >> 1), _flip(y, j & 1)
        out.append((px, py, 2 * px + py))
    return out


PAIR_COLLECTIVE_ID = 0


def _pair_barrier(x, y, c):
    sem = pltpu.get_barrier_semaphore()
    pl.semaphore_signal(sem, inc=1, device_id=(x, y, 1 - c), device_id_type=MESH)
    pl.semaphore_wait(sem, 1)


def _pair_params():
    return pltpu.CompilerParams(collective_id=PAIR_COLLECTIVE_ID)


def _half_rows(c, rows_half):
    return pl.ds(pl.multiple_of(c * rows_half, 8), rows_half)


def _cast_place(w, q_arr, tag, after=None):
    rows, cols = w.shape
    tr = _row_tile(rows)
    extra = [] if after is None else [after]

    def body(q_ref, w_ref, *rest):
        rest[-1][0] = w_ref[...].astype(BF16)

    return pl.pallas_call(
        body, name="cast_place_" + tag,
        grid_spec=pltpu.PrefetchScalarGridSpec(
            num_scalar_prefetch=1, grid=(rows // tr,),
            in_specs=[pl.BlockSpec((tr, cols), lambda i, q: (i, 0))] + [ANY] * len(extra),
            out_specs=pl.BlockSpec((1, tr, cols), lambda i, q: (q[0], i, 0))),
        out_shape=_sds((N_CHIPS, rows, cols), BF16),
        compiler_params=_params(("arbitrary",)),
    )(q_arr, w, *extra)


def _gather_shards(fulls):
    n = len(fulls)
    halves = [a.shape[1] // 2 for a in fulls]

    def body(*refs):
        full = refs[n:2 * n]
        ssem, rsem = refs[2 * n:]
        x, y, c = _mesh_pos()
        chips = _other_chips(x, y)

        def remote(i, chip_no, half, to, s):
            part = full[i].at[chip_no, _half_rows(half, halves[i]), :]
            return pltpu.make_async_remote_copy(src_ref=part, dst_ref=part, send_sem=ssem.at[s], recv_sem=rsem.at[s],
                                                device_id=to, device_id_type=MESH)

        first = []
        for i in range(n):
            for j, (px, py, _) in enumerate(chips):
                cp = remote(i, 2 * x + y, c, (px, py, c), 3 * i + j)
                cp.start()
                first.append(cp)
        passed = []
        for i in range(n):
            for j, (_, _, qj) in enumerate(chips):
                remote(i, qj, c, (x, y, c), 3 * i + j).wait_recv()
                cp = remote(i, qj, c, (x, y, 1 - c), 3 * n + 3 * i + j)
                cp.start()
                passed.append(cp)
        for i in range(n):
            for j, (_, _, qj) in enumerate(chips):
                remote(i, qj, 1 - c, (x, y, c), 3 * n + 3 * i + j).wait_recv()
        for cp in first + passed:
            cp.wait_send()

    return pl.pallas_call(
        body, name="gather_shards",
        in_specs=[ANY] * n, out_specs=[ANY] * n,
        out_shape=[_sds(a.shape, a.dtype) for a in fulls],
        input_output_aliases={i: i for i in range(n)},
        scratch_shapes=[pltpu.SemaphoreType.DMA((6 * n,)), pltpu.SemaphoreType.DMA((6 * n,))],
    )(*fulls)


HBM = pl.BlockSpec(memory_space=pltpu.HBM)
SEM = pl.BlockSpec(memory_space=pltpu.SEMAPHORE)
EFFECT = pltpu.SideEffectType.DATAFLOW_SIDE_EFFECTING


def _in_hbm(a):
    return pltpu.with_memory_space_constraint(a, pltpu.HBM)


def _gather_start(fulls, after, tag):
    n = len(fulls)
    halves = [a.shape[1] // 2 for a in fulls]

    def body(*refs):
        land = refs[:n]
        ssem, rsem = refs[n + 1], refs[n + 2]
        token = refs[-1]
        x, y, c = _mesh_pos()
        q = 2 * x + y
        for i in range(n):
            for j, (px, py, _) in enumerate(_other_chips(x, y)):
                mine = land[i].at[q, _half_rows(c, halves[i]), :]
                pltpu.make_async_remote_copy(src_ref=mine, dst_ref=mine, send_sem=ssem.at[3 * i + j],
                                             recv_sem=rsem.at[3 * i + j], device_id=(px, py, c), device_id_type=MESH).start()
        token[...] = jnp.zeros(token.shape, F32)

    outs = pl.pallas_call(
        body, name="gather_start_" + tag,
        in_specs=[HBM] * n + [ANY], out_specs=[SEM, SEM] + [HBM] * n + [VMEM],
        out_shape=[pltpu.SemaphoreType.DMA((3 * n,)), pltpu.SemaphoreType.DMA((3 * n,))]
        + [pltpu.HBM(a.shape, a.dtype) for a in fulls] + [_sds((8, LANES), F32)],
        input_output_aliases={i: 2 + i for i in range(n)},
        compiler_params=pltpu.CompilerParams(has_side_effects=EFFECT),
    )(*[_in_hbm(a) for a in fulls], after)
    return outs[0], outs[1], list(outs[2:2 + n]), outs[-1]


def _gather_wait(which, ssem, rsem, lands, after, tag):
    m = len(which)
    halves = [a.shape[1] // 2 for a in lands]

    def body(*refs):
        land = refs[:m]
        ssem_, rsem_ = refs[m], refs[m + 1]
        x, y, c = _mesh_pos()
        for t, i in enumerate(which):
            for j, (px, py, qj) in enumerate(_other_chips(x, y)):
                rows = _half_rows(c, halves[t])
                cp = pltpu.make_async_remote_copy(src_ref=land[t].at[2 * x + y, rows, :], dst_ref=land[t].at[qj, rows, :],
                                                  send_sem=ssem_.at[3 * i + j], recv_sem=rsem_.at[3 * i + j],
                                                  device_id=(px, py, c), device_id_type=MESH)
                cp.wait_send()
                cp.wait_recv()

    outs = pl.pallas_call(
        body, name="gather_wait_" + tag,
        in_specs=[HBM] * m + [SEM, SEM, ANY], out_specs=[HBM] * m,
        out_shape=[pltpu.HBM(a.shape, a.dtype) for a in lands],
        input_output_aliases={i: i for i in range(m)},
        compiler_params=pltpu.CompilerParams(has_side_effects=EFFECT),
    )(*lands, ssem, rsem, after)
    return list(outs)


def _forward_pair(lands, tag):
    n = len(lands)
    halves = [a.shape[1] // 2 for a in lands]

    def body(*refs):
        full = refs[n:2 * n]
        ssem, rsem = refs[2 * n:]
        x, y, c = _mesh_pos()
        _pair_barrier(x, y, c)
        cps = []
        for i in range(n):
            for j, (_, _, qj) in enumerate(_other_chips(x, y)):
                part = full[i].at[qj, _half_rows(c, halves[i]), :]
                cp = pltpu.make_async_remote_copy(src_ref=part, dst_ref=part, send_sem=ssem.at[3 * i + j],
                                                  recv_sem=rsem.at[3 * i + j], device_id=(x, y, 1 - c), device_id_type=MESH)
                cp.start()
                cps.append(cp)
        for cp in cps:
            cp.wait()

    return pl.pallas_call(
        body, name="forward_pair_" + tag,
        in_specs=[ANY] * n, out_specs=[ANY] * n,
        out_shape=[_sds(a.shape, a.dtype) for a in lands],
        input_output_aliases={i: i for i in range(n)},
        scratch_shapes=[pltpu.SemaphoreType.DMA((3 * n,)), pltpu.SemaphoreType.DMA((3 * n,))],
        compiler_params=_pair_params(),
    )(*lands)


def _chip_exchange_start(parts, after, tag):
    n = len(parts)

    def body(*refs):
        src, land = refs[:n], refs[n:2 * n]
        ssem, rsem = refs[2 * n + 1], refs[2 * n + 2]
        token = refs[-1]
        x, y, c = _mesh_pos()
        for i in range(n):
            for j, (px, py, qj) in enumerate(_other_chips(x, y)):
                pltpu.make_async_remote_copy(src_ref=src[i].at[qj], dst_ref=land[i].at[j], send_sem=ssem.at[3 * i + j],
                                             recv_sem=rsem.at[3 * i + j], device_id=(px, py, c), device_id_type=MESH).start()
        token[...] = jnp.zeros(token.shape, F32)

    lands = [lax.empty((3,) + a.shape[1:], a.dtype) for a in parts]
    outs = pl.pallas_call(
        body, name="chip_exchange_start_" + tag,
        in_specs=[HBM] * (2 * n) + [ANY], out_specs=[SEM, SEM] + [HBM] * (2 * n) + [VMEM],
        out_shape=[pltpu.SemaphoreType.DMA((3 * n,)), pltpu.SemaphoreType.DMA((3 * n,))]
        + [pltpu.HBM(a.shape, a.dtype) for a in parts] + [pltpu.HBM(a.shape, a.dtype) for a in lands]
        + [_sds((8, LANES), F32)],
        input_output_aliases={i: 2 + i for i in range(2 * n)},
        compiler_params=pltpu.CompilerParams(has_side_effects=EFFECT),
    )(*[_in_hbm(a) for a in parts], *[_in_hbm(a) for a in lands], after)
    return outs[0], outs[1], list(outs[2:2 + n]), list(outs[2 + n:2 + 2 * n]), outs[-1]


def _chip_exchange_wait(ssem, rsem, parts, lands, after, tag):
    n = len(parts)

    def body(*refs):
        src, land = refs[:n], refs[n:2 * n]
        ssem_, rsem_ = refs[2 * n], refs[2 * n + 1]
        x, y, c = _mesh_pos()
        for i in range(n):
            for j, (px, py, qj) in enumerate(_other_chips(x, y)):
                cp = pltpu.make_async_remote_copy(src_ref=src[i].at[qj], dst_ref=land[i].at[j], send_sem=ssem_.at[3 * i + j],
                                                  recv_sem=rsem_.at[3 * i + j], device_id=(px, py, c), device_id_type=MESH)
                cp.wait_send()
                cp.wait_recv()

    outs = pl.pallas_call(
        body, name="chip_exchange_wait_" + tag,
        in_specs=[HBM] * (2 * n) + [SEM, SEM, ANY], out_specs=[HBM] * (2 * n),
        out_shape=[pltpu.HBM(a.shape, a.dtype) for a in parts] + [pltpu.HBM(a.shape, a.dtype) for a in lands],
        input_output_aliases={i: i for i in range(2 * n)},
        compiler_params=pltpu.CompilerParams(has_side_effects=EFFECT),
    )(*parts, *lands, ssem, rsem, after)
    return list(outs[:n]), list(outs[n:])


def _pair_exchange_grads(grads, half_axis, tag):
    n = len(grads)

    def half_of(ref, i, which):
        rows = grads[i].shape[half_axis[i]] // 2
        if half_axis[i] == 0:
            return ref.at[_half_rows(which, rows), :]
        return ref.at[:, _half_rows(which, rows), :]

    def out_shape(i):
        s = list(grads[i].shape)
        s[half_axis[i]] //= 2
        return _sds(tuple(s), grads[i].dtype)

    def body(*refs):
        g, got = refs[:n], refs[n:2 * n]
        ssem, rsem = refs[2 * n:]
        x, y, c = _mesh_pos()
        _pair_barrier(x, y, c)
        cps = []
        for i in range(n):
            cp = pltpu.make_async_remote_copy(src_ref=half_of(g[i], i, 1 - c), dst_ref=got[i], send_sem=ssem.at[i],
                                              recv_sem=rsem.at[i], device_id=(x, y, 1 - c), device_id_type=MESH)
            cp.start()
            cps.append(cp)
        for cp in cps:
            cp.wait()

    return pl.pallas_call(
        body, name="pair_exchange_grads_" + tag,
        in_specs=[ANY] * n, out_specs=[ANY] * n,
        out_shape=[out_shape(i) for i in range(n)],
        scratch_shapes=[pltpu.SemaphoreType.DMA((n,)), pltpu.SemaphoreType.DMA((n,))],
        compiler_params=_pair_params(),
    )(*grads)


def _pair_sum(g, got, c_arr, col_sharded, tag):
    if col_sharded:
        rows, cols = g.shape
        rh, cs = rows // 2, cols // N_CHIPS
        g_spec = pl.BlockSpec((rh, cs), lambda k, c_ref: (c_ref[0], k))
        got_spec = pl.BlockSpec((rh, cs), lambda k, c_ref: (0, k))
    else:
        _, rows, cs = g.shape
        rh = rows // 2
        g_spec = pl.BlockSpec((1, rh, cs), lambda k, c_ref: (k, c_ref[0], 0))
        got_spec = pl.BlockSpec((1, rh, cs), lambda k, c_ref: (k, 0, 0))

    def body(c_ref, g_ref, got_ref, out_ref):
        total = g_ref[...].astype(F32) + got_ref[...].astype(F32)
        out_ref[...] = total.astype(BF16).reshape(out_ref.shape)

    return pl.pallas_call(
        body, name="pair_sum_" + tag,
        grid_spec=pltpu.PrefetchScalarGridSpec(
            num_scalar_prefetch=1, grid=(N_CHIPS,), in_specs=[g_spec, got_spec],
            out_specs=pl.BlockSpec((1, rh, cs), lambda k, c_ref: (k, 0, 0))),
        out_shape=_sds((N_CHIPS, rh, cs), BF16),
        compiler_params=_params(("arbitrary",)),
    )(c_arr, g, got)


def _chip_sum(part, got, qc_arr, tag):
    _, rh, cs = part.shape

    def body(qc_ref, part_ref, got_ref, out_ref):
        total = part_ref[0].astype(F32)
        for j in range(3):
            total = total + got_ref[j].astype(F32)
        out_ref[...] = total

    return pl.pallas_call(
        body, name="chip_sum_" + tag,
        grid_spec=pltpu.PrefetchScalarGridSpec(
            num_scalar_prefetch=1, grid=(1,),
            in_specs=[pl.BlockSpec((1, rh, cs), lambda i, qc: (qc[0], 0, 0)), pl.BlockSpec((3, rh, cs), lambda i, qc: (0, 0, 0))],
            out_specs=pl.BlockSpec((rh, cs), lambda i, qc: (qc[1], 0))),
        out_shape=_sds((2 * rh, cs), F32),
        compiler_params=_params(("arbitrary",)),
    )(qc_arr, part, got)


def _pair_share_grads(grads, tag):
    n = len(grads)

    def body(*refs):
        g = refs[n:2 * n]
        ssem, rsem = refs[2 * n:]
        x, y, c = _mesh_pos()
        _pair_barrier(x, y, c)
        cps = []
        for i in range(n):
            mine = g[i].at[_half_rows(c, grads[i].shape[0] // 2), :]
            cp = pltpu.make_async_remote_copy(src_ref=mine, dst_ref=mine, send_sem=ssem.at[i], recv_sem=rsem.at[i],
                                              device_id=(x, y, 1 - c), device_id_type=MESH)
            cp.start()
            cps.append(cp)
        for cp in cps:
            cp.wait()

    return pl.pallas_call(
        body, name="pair_share_grads_" + tag,
        in_specs=[ANY] * n, out_specs=[ANY] * n,
        out_shape=[_sds(a.shape, a.dtype) for a in grads],
        input_output_aliases={i: i for i in range(n)},
        scratch_shapes=[pltpu.SemaphoreType.DMA((n,)), pltpu.SemaphoreType.DMA((n,))],
        compiler_params=_pair_params(),
    )(*grads)


def _small_allreduce(parts, places, rows_total, width):
    n = len(parts)

    def body(*refs):
        ins, out_ref = refs[:n], refs[n]
        pack, pair_got, chip_sum, got, ssem, rsem = refs[n + 1:]
        x, y, c = _mesh_pos()
        chip = 2 * x + y
        pack[...] = jnp.zeros(pack.shape, F32)
        for i in range(n):
            for row, col, src_row, rows in places[i]:
                w = parts[i].shape[1]
                pack[row:row + rows, col:col + w] = ins[i][src_row:src_row + rows, :]
        swap = pltpu.make_async_remote_copy(src_ref=pack, dst_ref=pair_got, send_sem=ssem.at[3], recv_sem=rsem.at[3],
                                            device_id=(x, y, 1 - c), device_id_type=MESH)
        swap.start()
        swap.wait()
        chip_sum[...] = pack[...] + pair_got[...]
        cps = []
        for j, (px, py, _) in enumerate(_other_chips(x, y)):
            cp = pltpu.make_async_remote_copy(src_ref=chip_sum, dst_ref=got.at[j], send_sem=ssem.at[j],
                                              recv_sem=rsem.at[j], device_id=(px, py, c), device_id_type=MESH)
            cp.start()
            cps.append(cp)
        for cp in cps:
            cp.wait()
        total = jnp.zeros(pack.shape, F32)
        for q in range(N_CHIPS):
            rel = jnp.bitwise_xor(chip, q)
            theirs = got[jnp.maximum(rel - 1, 0)]
            total = total + jnp.where(rel == 0, chip_sum[...], theirs)
        out_ref[...] = total

    return pl.pallas_call(
        body, name="small_allreduce",
        in_specs=[VMEM] * n, out_specs=VMEM,
        out_shape=_sds((rows_total, width), F32),
        scratch_shapes=[pltpu.VMEM((rows_total, width), F32), pltpu.VMEM((rows_total, width), F32),
                        pltpu.VMEM((rows_total, width), F32), pltpu.VMEM((3, rows_total, width), F32),
                        pltpu.SemaphoreType.DMA((4,)), pltpu.SemaphoreType.DMA((4,))],
        compiler_params=_params(),
    )(*parts)


def _adamw_math(w, g, m, v):
    m2 = ADAM_B1 * m + (1.0 - ADAM_B1) * g
    v2 = ADAM_B2 * v + (1.0 - ADAM_B2) * (g * g)
    m_hat = m2 / (1.0 - ADAM_B1 ** ADAM_STEP)
    v_hat = v2 / (1.0 - ADAM_B2 ** ADAM_STEP)
    delta = -ADAM_LR * (m_hat / (jnp.sqrt(v_hat) + ADAM_EPS) + ADAM_WD * w)
    return delta, m2, v2


def _adamw_big(w, g, m, v, tag):
    rows, cols = w.shape
    tr = _row_tile(rows)

    def body(w_ref, g_ref, m_ref, v_ref, go_ref, d_ref, m2_ref, v2_ref):
        gg = g_ref[...]
        go_ref[...] = gg
        d_ref[...], m2_ref[...], v2_ref[...] = _adamw_math(w_ref[...], gg, m_ref[...], v_ref[...])

    spec = pl.BlockSpec((tr, cols), lambda i: (i, 0))
    return pl.pallas_call(
        body, name="adamw_" + tag, grid=(rows // tr,),
        in_specs=[spec] * 4, out_specs=[spec] * 4,
        out_shape=[_sds((rows, cols), F32)] * 4,
        compiler_params=_params(("arbitrary",)),
    )(w, g, m, v)


def _adamw_small(ws, gs, ms, vs):
    n = len(ws)

    def body(*refs):
        w_r, g_r, m_r, v_r = refs[:n], refs[n:2 * n], refs[2 * n:3 * n], refs[3 * n:4 * n]
        d_o, m_o, v_o = refs[4 * n:5 * n], refs[5 * n:6 * n], refs[6 * n:7 * n]
        for i in range(n):
            d_o[i][...], m_o[i][...], v_o[i][...] = _adamw_math(w_r[i][...], g_r[i][...], m_r[i][...], v_r[i][...])

    shapes = [_sds(w.shape, F32) for w in ws]
    outs = pl.pallas_call(
        body, name="adamw_small",
        in_specs=[VMEM] * (4 * n), out_specs=[VMEM] * (3 * n),
        out_shape=shapes * 3,
        compiler_params=_params(),
    )(*ws, *gs, *ms, *vs)
    return outs[:n], outs[n:2 * n], outs[2 * n:]


SMALL_ROWS = 40
PACK_ROWS = 64


def kernel(x, meta_tokens, pre_mix_norm, w_in, conv_a_w, conv_b_w, conv_b_bias, ln_b_gain, ln_b_bias, w_out, post_mix_norm, pre_ffn_norm, w_gate, w_up, w_down, post_ffn_norm, loss_target, m_meta_tokens, m_pre_mix_norm, m_w_in, m_conv_a_w, m_conv_b_w, m_conv_b_bias, m_ln_b_gain, m_ln_b_bias, m_w_out, m_post_mix_norm, m_pre_ffn_norm, m_w_gate, m_w_up, m_w_down, m_post_ffn_norm, v_meta_tokens, v_pre_mix_norm, v_w_in, v_conv_a_w, v_conv_b_w, v_conv_b_bias, v_ln_b_gain, v_ln_b_bias, v_w_out, v_post_mix_norm, v_pre_ffn_norm, v_w_gate, v_w_up, v_w_down, v_post_ffn_norm):
    xq, yq, cq = lax.axis_index("x"), lax.axis_index("y"), lax.axis_index("c")
    chip = 2 * xq + yq
    c_arr = jnp.reshape(cq, (1,)).astype(jnp.int32)
    qc_arr = jnp.stack([chip, cq]).astype(jnp.int32)

    seq, d = x.shape[1], x.shape[2]
    x2, tgt2 = x[0], loss_target[0]
    tr = lambda a: jnp.swapaxes(a, 1, 2)[0]
    w_in2, w_out2, w_gate2, w_up2, w_down2 = w_in[0], w_out[0], tr(w_gate), tr(w_up), w_down[0]
    ka, wa_sh = conv_a_w.shape[1], conv_a_w.shape[2]
    kb = conv_b_w.shape[1]
    meta_sh = meta_tokens.shape[1]

    small = jnp.zeros((PACK_ROWS, meta_sh), F32)
    small = small.at[0:N_META, :].set(meta_tokens)
    small = small.at[16:16 + ka, 0:wa_sh].set(conv_a_w[0])
    small = small.at[24:24 + kb, 0:wa_sh].set(conv_b_w[0])
    q_arr = jnp.reshape(chip, (1,)).astype(jnp.int32)
    small_own = lax.dynamic_update_slice(jnp.zeros((N_CHIPS, PACK_ROWS, meta_sh), F32), small[None], (chip, 0, 0))
    i_ssem, i_rsem, first, i_token = _gather_start([_cast_place(w_in2, q_arr, "w_in"), small_own], pre_mix_norm, "in")
    rest = [_cast_place(w, q_arr, nm, i_token)
            for w, nm in ((w_out2, "w_out"), (w_gate2, "w_gate"), (w_up2, "w_up"), (w_down2, "w_down"))]
    g_ssem, g_rsem, lands, g_token = _gather_start(rest, i_token, "rest")
    win4, small4 = _forward_pair(_gather_wait([0, 1], i_ssem, i_rsem, first, g_token, "in"), "in")
    meta_f = jnp.concatenate([small4[k, 0:N_META, :] for k in range(N_CHIPS)], axis=1)
    wa_f = jnp.concatenate([small4[k, 16:16 + ka, 0:wa_sh] for k in range(N_CHIPS)], axis=1)
    wb_f = jnp.concatenate([small4[k, 24:24 + kb, 0:wa_sh] for k in range(N_CHIPS)], axis=1)

    tm = _row_tile(seq + TAIL_ROWS)
    tail = lax.dynamic_update_slice(jnp.zeros((tm, d), F32), meta_f, (seq % tm, 0))
    h, xn1, hp5 = _mm_in(x2, tail, win4, pre_mix_norm + g_token[0:1, 0:1])
    ya, z = _mix_conv_fwd(hp5, wa_f, wb_f, conv_b_bias)
    (wout4,) = _forward_pair(_gather_wait([0], g_ssem, g_rsem, lands[0:1], z, "out"), "out")
    wout_f = wout4.reshape(N_CHIPS * wout4.shape[1], wout4.shape[2])
    yb, mix, h1, xn2 = _mm_out(ya, z, h, wout_f, ln_b_gain, ln_b_bias, post_mix_norm, pre_ffn_norm)
    wg4, wu4 = _forward_pair(_gather_wait([1, 2], g_ssem, g_rsem, lands[1:3], xn2, "gate_up"), "gate_up")
    stacked = lambda a: a.reshape(a.shape[0] * a.shape[1], a.shape[2])
    wg_f, wu_f = stacked(wg4), stacked(wu4)
    p_act, q_act, f_act = _ffn_up(xn2, wg_f, wu_f)
    (wd4,) = _forward_pair(_gather_wait([3], g_ssem, g_rsem, lands[3:4], f_act, "down"), "down")
    wd_f = stacked(wd4)
    dff, dh2, loss_blk, d_gpf = _ffn_down(f_act, wd_f, h1, tgt2, post_ffn_norm)

    da, du = _ffn_bwd_act(dff, wd_f, p_act, q_act)
    by_chip = lambda g: g.reshape(N_CHIPS, g.shape[0] // N_CHIPS, g.shape[1])
    g_down = by_chip(_grad_w_down(f_act, dff))
    g_gate, g_up = [by_chip(g) for g in _grad_w_gate_up(xn2, da, du)]
    ffn = [g_gate, g_up, g_down]
    got = _pair_exchange_grads(ffn, [1, 1, 1], "ffn")
    parts = [_pair_sum(ffn[i], got[i], c_arr, False, ("gate", "up", "down")[i]) for i in range(3)]
    f_ssem, f_rsem, parts, f_lands, f_token = _chip_exchange_start(parts, dff, "ffn")
    dh1, dmix, d_g2, d_gpm = _ffn_bwd_in(da, du, wg_f, wu_f, h1, mix, dh2, pre_ffn_norm + f_token[0:1, 0:1], post_mix_norm)
    g_out = _grad_w_out(ya, yb, dmix)
    dya, dz, d_lg, d_lb = _mix_bwd_out(dmix, wout_f, z, ln_b_gain, ln_b_bias)
    dhp5, d_wa, d_wb, d_bb = _mix_conv_bwd(hp5, dya, dz, wa_f, wb_f)
    g_in = _grad_w_in(xn1, dhp5)
    grad_x2, d_meta, d_g1 = _mix_bwd_in(dhp5, win4, h, dh1, pre_mix_norm)
    grad_x = grad_x2[None]

    g_out4 = g_out.reshape(N_CHIPS, g_out.shape[0] // N_CHIPS, g_out.shape[1])
    mixw = [g_in, g_out4]
    got2 = _pair_exchange_grads(mixw, [0, 1], "mix")
    parts2 = [_pair_sum(mixw[i], got2[i], c_arr, i == 0, ("in", "out")[i]) for i in range(2)]
    m_ssem, m_rsem, parts2, m_lands, m_token = _chip_exchange_start(parts2, d_g1, "mix")

    parts, f_recv = _chip_exchange_wait(f_ssem, f_rsem, parts, f_lands, m_token, "ffn")
    halves = [_chip_sum(parts[i], f_recv[i], qc_arr, ("gate", "up", "down")[i]) for i in range(3)]
    gsum_ffn = _pair_share_grads(halves, "ffn")

    hw = d // 2
    assert d_wa.shape == (3, hw) and d_wb.shape == (31, hw) and d_bb.shape == (1, hw)
    small_parts = [d_meta, d_g1, d_gpm, d_g2, d_gpf, d_bb, d_lg, d_lb, loss_blk[0:1, :], d_wa, d_wb]
    places = [[(0, 0, 0, N_META)], [(16, 0, 0, 1)], [(17, 0, 0, 1)], [(18, 0, 0, 1)], [(19, 0, 0, 1)],
              [(20, 0, 0, 1)], [(20, hw, 0, 1)], [(21, 0, 0, 1)], [(21, hw, 0, 1)], [(22, 0, 0, 3)],
              [(22, hw, 0, 3), (25, 0, 3, 14), (25, hw, 17, 14)]]
    red = _small_allreduce(small_parts, places, SMALL_ROWS, d)
    s_meta, s_g1, s_gpm, s_g2, s_gpf = red[0:N_META], red[16:17], red[17:18], red[18:19], red[19:20]
    s_bb, s_lg, s_lb, s_loss = red[20:21, 0:hw], red[20:21, hw:d], red[21:22, 0:hw], red[21:22, hw:hw + LANES]
    s_wa = red[22:25, 0:hw]
    s_wb = jnp.concatenate([red[22:25, hw:d], red[25:39, 0:hw], red[25:39, hw:d]], axis=0)
    g_meta = lax.dynamic_slice_in_dim(s_meta, chip * meta_sh, meta_sh, axis=1)
    g_wa = lax.dynamic_slice_in_dim(s_wa, chip * wa_sh, wa_sh, axis=1)[None]
    g_wb = lax.dynamic_slice_in_dim(s_wb, chip * wa_sh, wa_sh, axis=1)[None]
    loss = s_loss[0, 0]

    names_big = ["w_in", "w_out", "w_gate", "w_up", "w_down"]
    w_big = dict(zip(names_big, [w_in2, w_out2, w_gate2, w_up2, w_down2]))
    m_big = dict(zip(names_big, [m_w_in[0], m_w_out[0], tr(m_w_gate), tr(m_w_up), m_w_down[0]]))
    v_big = dict(zip(names_big, [v_w_in[0], v_w_out[0], tr(v_w_gate), tr(v_w_up), v_w_down[0]]))
    grads, deltas, new_m, new_v = {}, {}, {}, {}

    def update(nm, g):
        outs = _adamw_big(w_big[nm], g, m_big[nm], v_big[nm], nm)
        if nm in ("w_gate", "w_up"):
            outs = [jnp.swapaxes(o[None], 1, 2) for o in outs]
        else:
            outs = [o[None] for o in outs]
        grads[nm], deltas[nm], new_m[nm], new_v[nm] = outs
        return outs[1]

    last = None
    for nm, g in zip(["w_gate", "w_up", "w_down"], gsum_ffn):
        last = update(nm, g)

    parts2, m_recv = _chip_exchange_wait(m_ssem, m_rsem, parts2, m_lands, last, "mix")
    halves2 = [_chip_sum(parts2[i], m_recv[i], qc_arr, ("in", "out")[i]) for i in range(2)]
    gsum_mix = _pair_share_grads(halves2, "mix")
    for nm, g in zip(["w_in", "w_out"], gsum_mix):
        update(nm, g)

    names_small = ["meta_tokens", "pre_mix_norm", "conv_a_w", "conv_b_w", "conv_b_bias", "ln_b_gain", "ln_b_bias",
                   "post_mix_norm", "pre_ffn_norm", "post_ffn_norm"]
    w_small = [meta_tokens, pre_mix_norm, conv_a_w[0], conv_b_w[0], conv_b_bias, ln_b_gain, ln_b_bias, post_mix_norm,
               pre_ffn_norm, post_ffn_norm]
    g_small = [g_meta, s_g1, g_wa[0], g_wb[0], s_bb, s_lg, s_lb, s_gpm, s_g2, s_gpf]
    m_small = [m_meta_tokens, m_pre_mix_norm, m_conv_a_w[0], m_conv_b_w[0], m_conv_b_bias, m_ln_b_gain, m_ln_b_bias,
               m_post_mix_norm, m_pre_ffn_norm, m_post_ffn_norm]
    v_small = [v_meta_tokens, v_pre_mix_norm, v_conv_a_w[0], v_conv_b_w[0], v_conv_b_bias, v_ln_b_gain, v_ln_b_bias,
               v_post_mix_norm, v_pre_ffn_norm, v_post_ffn_norm]
    d_s, m_s, v_s = _adamw_small(w_small, g_small, m_small, v_small)
    for i, nm in enumerate(names_small):
        lead = nm in ("conv_a_w", "conv_b_w")
        fix = (lambda a: a[None]) if lead else (lambda a: a)
        grads[nm], deltas[nm], new_m[nm], new_v[nm] = fix(g_small[i]), fix(d_s[i]), fix(m_s[i]), fix(v_s[i])

    order = ["meta_tokens", "pre_mix_norm", "w_in", "conv_a_w", "conv_b_w", "conv_b_bias", "ln_b_gain", "ln_b_bias", "w_out",
             "post_mix_norm", "pre_ffn_norm", "w_gate", "w_up", "w_down", "post_ffn_norm"]
    return (loss, grad_x, *[grads[k] for k in order], *[deltas[k] for k in order], *[new_m[k] for k in order],
            *[new_v[k] for k in order])
```

```python
import functools

import jax
import jax.numpy as jnp
from jax import lax
from jax.experimental import pallas as pl
from jax.experimental.pallas import tpu as pltpu

F32 = jnp.float32
BF16 = jnp.bfloat16
MESH = pl.DeviceIdType.MESH

N_META = 16
TAIL_ROWS = 128
RMS_EPS = 1e-6
LN_EPS = 1e-5
ADAM_LR = 0.001
ADAM_B1 = 0.9
ADAM_B2 = 0.999
ADAM_EPS = 1e-08
ADAM_WD = 0.01
ADAM_STEP = 10

N_CHIPS = 4
LANES = 128
MXU_TILE = 256
CONV_CHUNK = 48
CONV_HIST = 32
ROW_TILE_CAP = 640
VMEM_LIMIT = 56 * 1024 * 1024

NN = (((1,), (0,)), ((), ()))
NT = (((1,), (1,)), ((), ()))
TN = (((0,), (0,)), ((), ()))


def _dot(a, b, dims=NN):
    return lax.dot_general(a, b, dims, preferred_element_type=F32)


def _sig(v):
    return 1.0 / (1.0 + jnp.exp(-v))


def _mean(v):
    return jnp.mean(v, axis=-1, keepdims=True)


def _row_tile(rows):
    best = 16
    for t in range(16, min(rows, ROW_TILE_CAP) + 1, 16):
        if rows % t == 0:
            best = t
    assert rows % best == 0
    return best


def _row_parts(tm):
    if tm % 32:
        return [slice(0, tm)]
    return [slice(0, tm // 2), slice(tm // 2, tm)]


def _pieces(n_shards, shard_w, piece_w):
    total = n_shards * shard_w
    cuts = sorted(set(range(0, total + 1, shard_w)) | set(range(0, total + 1, piece_w)))
    out = []
    for lo, hi in zip(cuts[:-1], cuts[1:]):
        out.append((lo // shard_w, lo % shard_w, lo // piece_w, lo % piece_w, hi - lo))
    return out


def _params(semantics=None):
    kw = dict(vmem_limit_bytes=VMEM_LIMIT)
    if semantics is not None:
        kw["dimension_semantics"] = semantics
    return pltpu.CompilerParams(**kw)


def _full(shape):
    nd = len(shape)
    return pl.BlockSpec(shape, lambda *_: (0,) * nd)


def _resident(shape):
    nd = len(shape)
    return pl.BlockSpec(shape, lambda *_: (0,) * nd, pipeline_mode=pl.Buffered(1))


def _sds(shape, dtype):
    return jax.ShapeDtypeStruct(shape, dtype)


ANY = pl.BlockSpec(memory_space=pl.ANY)
VMEM = pl.BlockSpec(memory_space=pltpu.VMEM)


def _mesh_pos():
    return lax.axis_index("x"), lax.axis_index("y"), lax.axis_index("c")


def _flip(v, bit):
    return 1 - v if bit else v


def _mm_in(x, tail, win4, g1):
    seq, d = x.shape
    tp = seq + TAIL_ROWS
    tm = _row_tile(tp)
    n_sh, _, csh = win4.shape
    pw = n_sh * csh // 5
    pieces = _pieces(n_sh, csh, pw)

    def body(x_ref, tail_ref, w_ref, g_ref, h_ref, xn_ref, hp_ref):
        rows = pl.program_id(0) * tm + lax.broadcasted_iota(jnp.int32, (tm, 1), 0)
        hh = jnp.where(rows < seq, x_ref[...], tail_ref[...])
        h_ref[...] = hh
        r = lax.rsqrt(_mean(hh * hh) + RMS_EPS)
        xn = (hh * r * g_ref[...]).astype(BF16)
        xn_ref[...] = xn
        for k, klo, p, plo, w in pieces:
            hp_ref[p, :, plo:plo + w] = _dot(xn, w_ref[k, :, klo:klo + w])

    row = pl.BlockSpec((tm, d), lambda i: (i, 0))
    return pl.pallas_call(
        body, name="mm_in", grid=(tp // tm,),
        in_specs=[row, _full(tail.shape), _resident(win4.shape), _full(g1.shape)],
        out_specs=[row, row, pl.BlockSpec((5, tm, pw), lambda i: (0, i, 0))],
        out_shape=[_sds((tp, d), F32), _sds((tp, d), BF16), _sds((5, tp, pw), F32)],
        compiler_params=_params(("arbitrary",)),
    )(x, tail, win4, g1)


def _seq_rows(tp):
    seq = tp - TAIL_ROWS
    nseq = seq + N_META
    assert nseq % CONV_CHUNK == 0 and seq % 16 == 0
    return seq, nseq


SUBLANES = 8


def _conv_offsets(width, transpose):
    return [(width - 1 - k) if transpose else (CONV_HIST - (width - 1) + k) for k in range(width)]


def _shift_copies(src_ref, sh_ref, width, transpose):
    n = src_ref.shape[0] - SUBLANES
    for s in sorted({o % SUBLANES for o in _conv_offsets(width, transpose)} - {0}):
        sh_ref[s - 1, 0:n, :] = src_ref[s:s + n, :]


def _tap_rows(src_ref, sh_ref, base, off):
    start = pl.multiple_of(base + (off // SUBLANES) * SUBLANES, SUBLANES)
    if off % SUBLANES == 0:
        return src_ref[pl.ds(start, CONV_CHUNK), :]
    return sh_ref[off % SUBLANES - 1, pl.ds(start, CONV_CHUNK), :]


def _conv_taps(src_ref, sh_ref, w_ref, dst_ref, width, nseq, transpose):
    w = w_ref[...]
    offs = _conv_offsets(width, transpose)
    _shift_copies(src_ref, sh_ref, width, transpose)

    def step(n, carry):
        out0 = pl.multiple_of(CONV_HIST + n * CONV_CHUNK, SUBLANES)
        base = out0 if transpose else n * CONV_CHUNK
        acc = jnp.zeros((CONV_CHUNK, w.shape[1]), F32)
        for k, off in enumerate(offs):
            acc = acc + w[k:k + 1, :] * _tap_rows(src_ref, sh_ref, base, off)
        dst_ref[pl.ds(out0, CONV_CHUNK), :] = acc
        return carry

    lax.fori_loop(0, nseq // CONV_CHUNK, step, 0)


def _conv_wgrad(src_ref, sh_ref, dz_ref, acc_ref, width, nseq):
    acc_ref[...] = jnp.zeros(acc_ref.shape, F32)
    offs = _conv_offsets(width, False)

    def step(n, carry):
        dzc = dz_ref[pl.ds(pl.multiple_of(CONV_HIST + n * CONV_CHUNK, SUBLANES), CONV_CHUNK), :]
        for k, off in enumerate(offs):
            prod = dzc * _tap_rows(src_ref, sh_ref, n * CONV_CHUNK, off)
            part = prod[0:SUBLANES, :]
            for s in range(1, CONV_CHUNK // SUBLANES):
                part = part + prod[SUBLANES * s:SUBLANES * (s + 1), :]
            acc_ref[SUBLANES * k:SUBLANES * (k + 1), :] += part
        return carry

    lax.fori_loop(0, nseq // CONV_CHUNK, step, 0)


def _to_seq(buf_ref, x_part, meta_part, seq):
    buf_ref[CONV_HIST:CONV_HIST + N_META, :] = meta_part
    buf_ref[CONV_HIST + N_META:CONV_HIST + N_META + seq, :] = x_part


def _zero_ends(buf_ref, nseq):
    zeros = jnp.zeros((CONV_HIST, buf_ref.shape[1]), F32)
    buf_ref[0:CONV_HIST, :] = zeros
    buf_ref[CONV_HIST + nseq:CONV_HIST + nseq + CONV_HIST, :] = zeros


def _mix_conv_fwd(hp5, wa, wb, bb):
    _, tp, wgrp = hp5.shape
    seq, nseq = _seq_rows(tp)
    sb = nseq + 2 * CONV_HIST
    ka, kb = wa.shape[0], wb.shape[0]
    xs, ms = slice(0, seq), slice(seq, seq + N_META)
    ox, om = slice(CONV_HIST + N_META, CONV_HIST + nseq), slice(CONV_HIST, CONV_HIST + N_META)

    def body(hp_ref, wa_ref, wb_ref, bb_ref, ya_ref, z_ref, s_ref, o_ref, sh_ref):
        _zero_ends(s_ref, nseq)
        _to_seq(s_ref, hp_ref[1, xs, :] * hp_ref[2, xs, :], hp_ref[1, ms, :] * hp_ref[2, ms, :], seq)
        _conv_taps(s_ref, sh_ref, wa_ref, o_ref, ka, nseq, False)
        ya_ref[xs, :] = (hp_ref[0, xs, :] * o_ref[ox, :]).astype(BF16)
        ya_ref[ms, :] = (hp_ref[0, ms, :] * o_ref[om, :]).astype(BF16)
        ya_ref[seq + N_META:tp, :] = jnp.zeros((tp - seq - N_META, LANES), BF16)
        _to_seq(s_ref, hp_ref[3, xs, :] * _sig(hp_ref[4, xs, :]), hp_ref[3, ms, :] * _sig(hp_ref[4, ms, :]), seq)
        _conv_taps(s_ref, sh_ref, wb_ref, o_ref, kb, nseq, False)
        z_ref[xs, :] = o_ref[ox, :] + bb_ref[...]
        z_ref[ms, :] = o_ref[om, :] + bb_ref[...]
        z_ref[seq + N_META:tp, :] = jnp.zeros((tp - seq - N_META, LANES), F32)

    col = lambda j: (0, j)
    return pl.pallas_call(
        body, name="mix_conv_fwd", grid=(wgrp // LANES,),
        in_specs=[pl.BlockSpec((5, tp, LANES), lambda j: (0, 0, j)), pl.BlockSpec((ka, LANES), col),
                  pl.BlockSpec((kb, LANES), col), pl.BlockSpec((1, LANES), col)],
        out_specs=[pl.BlockSpec((tp, LANES), col), pl.BlockSpec((tp, LANES), col)],
        out_shape=[_sds((tp, wgrp), BF16), _sds((tp, wgrp), F32)],
        scratch_shapes=[pltpu.VMEM((sb, LANES), F32), pltpu.VMEM((sb, LANES), F32),
                        pltpu.VMEM((SUBLANES - 1, sb, LANES), F32)],
        compiler_params=_params(("arbitrary",)),
    )(hp5, wa, wb, bb)


def _layer_norm_parts(z, lg, lb):
    mu = _mean(z)
    zc = z - mu
    rl = lax.rsqrt(_mean(zc * zc) + LN_EPS)
    zh = zc * rl
    return rl, zh, zh * lg + lb


def _mm_out(ya, z, h, wout, lg, lb, gpm, g2):
    tp, d = h.shape
    wa_ = ya.shape[1]
    tm = _row_tile(tp)

    def body(ya_ref, z_ref, h_ref, w_ref, lg_ref, lb_ref, gpm_ref, g2_ref, yb_ref, mix_ref, h1_ref, xn2_ref):
        for rs in _row_parts(tm):
            _, _, l = _layer_norm_parts(z_ref[rs, :], lg_ref[...], lb_ref[...])
            yb = (l * _sig(l)).astype(BF16)
            yb_ref[rs, :] = yb
            mix = _dot(ya_ref[rs, :], w_ref[0:wa_, :]) + _dot(yb, w_ref[wa_:d, :])
            mix_ref[rs, :] = mix
            rm = lax.rsqrt(_mean(mix * mix) + RMS_EPS)
            h1 = h_ref[rs, :] + mix * rm * gpm_ref[...]
            h1_ref[rs, :] = h1
            r2 = lax.rsqrt(_mean(h1 * h1) + RMS_EPS)
            xn2_ref[rs, :] = (h1 * r2 * g2_ref[...]).astype(BF16)

    row = lambda i: (i, 0)
    return pl.pallas_call(
        body, name="mm_out", grid=(tp // tm,),
        in_specs=[pl.BlockSpec((tm, wa_), row), pl.BlockSpec((tm, wa_), row), pl.BlockSpec((tm, d), row),
                  _resident(wout.shape), _full(lg.shape), _full(lb.shape), _full(gpm.shape), _full(g2.shape)],
        out_specs=[pl.BlockSpec((tm, wa_), row), pl.BlockSpec((tm, d), row), pl.BlockSpec((tm, d), row),
                   pl.BlockSpec((tm, d), row)],
        out_shape=[_sds((tp, wa_), BF16), _sds((tp, d), F32), _sds((tp, d), F32), _sds((tp, d), BF16)],
        compiler_params=_params(("arbitrary",)),
    )(ya, z, h, wout, lg, lb, gpm, g2)


def _ffn_up(xn2, wg, wu):
    tp, d = xn2.shape
    ff_dim = wg.shape[0]
    tm = _row_tile(tp)
    assert ff_dim % MXU_TILE == 0

    def body(xn_ref, wg_ref, wu_ref, p_ref, q_ref, f_ref):
        xn = xn_ref[...]
        for lo in range(0, ff_dim, MXU_TILE):
            cols = slice(lo, lo + MXU_TILE)
            a = _dot(xn, wg_ref[cols, :], NT)
            u = _dot(xn, wu_ref[cols, :], NT)
            s = _sig(a)
            q = a * s
            p_ref[:, cols] = (u * (s + q * (1.0 - s))).astype(BF16)
            q_ref[:, cols] = q.astype(BF16)
            f_ref[:, cols] = (q * u).astype(BF16)

    ospec = pl.BlockSpec((tm, ff_dim), lambda i: (i, 0))
    return pl.pallas_call(
        body, name="ffn_up", grid=(tp // tm,),
        in_specs=[pl.BlockSpec((tm, d), lambda i: (i, 0)), _resident(wg.shape), _resident(wu.shape)],
        out_specs=[ospec, ospec, ospec],
        out_shape=[_sds((tp, ff_dim), BF16)] * 3,
        compiler_params=_params(("arbitrary",)),
    )(xn2, wg, wu)


def _ffn_down(f, wd, h1, tgt, gpf):
    tp, ff_dim = f.shape
    d = h1.shape[1]
    tm = _row_tile(tp)
    seq, _ = _seq_rows(tp)

    def body(f_ref, w_ref, h1_ref, t_ref, gpf_ref, dff_ref, dh2_ref, loss_ref, dgpf_ref):
        i = pl.program_id(0)
        gpf_ = gpf_ref[...]

        @pl.when(i == 0)
        def _():
            loss_ref[...] = jnp.zeros(loss_ref.shape, F32)
            dgpf_ref[...] = jnp.zeros(dgpf_ref.shape, F32)

        for rs in _row_parts(tm):
            ff = _dot(f_ref[rs, :], w_ref[...])
            rf = lax.rsqrt(_mean(ff * ff) + RMS_EPS)
            nf = ff * rf
            h2 = h1_ref[rs, :] + nf * gpf_
            rows = i * tm + rs.start + lax.broadcasted_iota(jnp.int32, (rs.stop - rs.start, 1), 0)
            err = jnp.where(rows < seq, h2 - t_ref[rs, :], 0.0)
            dh2 = err * (1.0 / d)
            dh2_ref[rs, :] = dh2
            dn = dh2 * gpf_
            dff_ref[rs, :] = (rf * (dn - nf * _mean(dn * nf))).astype(BF16)
            loss_ref[...] += (0.5 / d) * jnp.sum(err * err, axis=(0, 1), keepdims=True)
            dgpf_ref[...] += jnp.sum(dh2 * nf, axis=0, keepdims=True)

    row = lambda i: (i, 0)
    return pl.pallas_call(
        body, name="ffn_down", grid=(tp // tm,),
        in_specs=[pl.BlockSpec((tm, ff_dim), row), _resident(wd.shape), pl.BlockSpec((tm, d), row),
                  pl.BlockSpec((tm, d), row), _full(gpf.shape)],
        out_specs=[pl.BlockSpec((tm, d), row), pl.BlockSpec((tm, d), row), _full((8, LANES)), _full((1, d))],
        out_shape=[_sds((tp, d), BF16), _sds((tp, d), F32), _sds((8, LANES), F32), _sds((1, d), F32)],
        compiler_params=_params(("arbitrary",)),
    )(f, wd, h1, tgt, gpf)


def _ffn_bwd_act(dff, wd, p, q):
    tp, d = dff.shape
    ff_dim = wd.shape[0]
    tm = _row_tile(tp)

    def body(dff_ref, w_ref, p_ref, q_ref, da_ref, du_ref):
        dffv = dff_ref[...]
        for lo in range(0, ff_dim, MXU_TILE):
            cols = slice(lo, lo + MXU_TILE)
            df = _dot(dffv, w_ref[cols, :], NT).astype(BF16)
            da_ref[:, cols] = df * p_ref[:, cols]
            du_ref[:, cols] = df * q_ref[:, cols]

    aspec = pl.BlockSpec((tm, ff_dim), lambda i: (i, 0))
    return pl.pallas_call(
        body, name="ffn_bwd_act", grid=(tp // tm,),
        in_specs=[pl.BlockSpec((tm, d), lambda i: (i, 0)), _resident(wd.shape), aspec, aspec],
        out_specs=[aspec, aspec],
        out_shape=[_sds((tp, ff_dim), BF16)] * 2,
        compiler_params=_params(("arbitrary",)),
    )(dff, wd, p, q)


def _grad_blocks(ff_dim):
    rows = ff_dim // 2
    assert rows % LANES == 0
    return rows


def _grad_w_down(f, dff):
    tp, ff_dim = f.shape
    d = dff.shape[1]
    rows = _grad_blocks(ff_dim)

    def body(f_ref, dff_ref, g_ref):
        g_ref[...] = _dot(f_ref[...], dff_ref[...], TN).astype(BF16)

    return pl.pallas_call(
        body, name="grad_w_down", grid=(ff_dim // rows,),
        in_specs=[pl.BlockSpec((tp, rows), lambda k: (0, k)), _resident(dff.shape)],
        out_specs=pl.BlockSpec((rows, d), lambda k: (k, 0)),
        out_shape=_sds((ff_dim, d), BF16),
        compiler_params=_params(("arbitrary",)),
    )(f, dff)


def _grad_w_gate_up(xn2, da, du):
    tp, ff_dim = da.shape
    d = xn2.shape[1]
    rows = _grad_blocks(ff_dim)

    def body(xn_ref, da_ref, du_ref, gg_ref, gu_ref):
        xn = xn_ref[...]
        gg_ref[...] = _dot(da_ref[...], xn, TN).astype(BF16)
        gu_ref[...] = _dot(du_ref[...], xn, TN).astype(BF16)

    aspec = pl.BlockSpec((tp, rows), lambda k: (0, k))
    gspec = pl.BlockSpec((rows, d), lambda k: (k, 0))
    return pl.pallas_call(
        body, name="grad_w_gate_up", grid=(ff_dim // rows,),
        in_specs=[_resident(xn2.shape), aspec, aspec],
        out_specs=[gspec, gspec],
        out_shape=[_sds((ff_dim, d), BF16)] * 2,
        compiler_params=_params(("arbitrary",)),
    )(xn2, da, du)


def _rms_bwd(dy, x, r, g):
    n = x * r
    dn = dy * g
    return r * (dn - n * _mean(dn * n)), dy * n


def _ffn_bwd_in(da, du, wg, wu, h1, mix, dh2, g2, gpm):
    tp, ff_dim = da.shape
    d = h1.shape[1]
    tm = _row_tile(tp)

    def body(da_ref, du_ref, wg_ref, wu_ref, h1_ref, mix_ref, dh2_ref, g2_ref, gpm_ref,
             dh1_ref, dmix_ref, dg2_ref, dgpm_ref):
        i = pl.program_id(0)

        @pl.when(i == 0)
        def _():
            dg2_ref[...] = jnp.zeros(dg2_ref.shape, F32)
            dgpm_ref[...] = jnp.zeros(dgpm_ref.shape, F32)

        for rs in _row_parts(tm):
            dxn = _dot(da_ref[rs, :], wg_ref[...]) + _dot(du_ref[rs, :], wu_ref[...])
            h1v = h1_ref[rs, :]
            r2 = lax.rsqrt(_mean(h1v * h1v) + RMS_EPS)
            dres, dg2_rows = _rms_bwd(dxn, h1v, r2, g2_ref[...])
            dh1 = dh2_ref[rs, :] + dres
            dh1_ref[rs, :] = dh1
            mixv = mix_ref[rs, :]
            rm = lax.rsqrt(_mean(mixv * mixv) + RMS_EPS)
            dmix, dgpm_rows = _rms_bwd(dh1, mixv, rm, gpm_ref[...])
            dmix_ref[rs, :] = dmix.astype(BF16)
            dg2_ref[...] += jnp.sum(dg2_rows, axis=0, keepdims=True)
            dgpm_ref[...] += jnp.sum(dgpm_rows, axis=0, keepdims=True)

    aspec = pl.BlockSpec((tm, ff_dim), lambda i: (i, 0))
    row = pl.BlockSpec((tm, d), lambda i: (i, 0))
    return pl.pallas_call(
        body, name="ffn_bwd_in", grid=(tp // tm,),
        in_specs=[aspec, aspec, _resident(wg.shape), _resident(wu.shape), row, row, row, _full(g2.shape), _full(gpm.shape)],
        out_specs=[row, row, _full((1, d)), _full((1, d))],
        out_shape=[_sds((tp, d), F32), _sds((tp, d), BF16), _sds((1, d), F32), _sds((1, d), F32)],
        compiler_params=_params(("arbitrary",)),
    )(da, du, wg, wu, h1, mix, dh2, g2, gpm)


def _grad_w_out(ya, yb, dmix):
    tp, wa_ = ya.shape
    d = dmix.shape[1]

    def body(ya_ref, yb_ref, dmix_ref, g_ref):
        dm = dmix_ref[...]
        g_ref[0:wa_, :] = _dot(ya_ref[...], dm, TN).astype(BF16)
        g_ref[wa_:2 * wa_, :] = _dot(yb_ref[...], dm, TN).astype(BF16)

    return pl.pallas_call(
        body, name="grad_w_out", grid=(1,),
        in_specs=[_full(ya.shape), _full(yb.shape), _full(dmix.shape)],
        out_specs=_full((2 * wa_, d)),
        out_shape=_sds((2 * wa_, d), BF16),
        compiler_params=_params(("arbitrary",)),
    )(ya, yb, dmix)


def _mix_bwd_out(dmix, wout, z, lg, lb):
    tp, d = dmix.shape
    wa_ = z.shape[1]
    tm = _row_tile(tp)

    def body(dmix_ref, w_ref, z_ref, lg_ref, lb_ref, dya_ref, dz_ref, dlg_ref, dlb_ref):
        i = pl.program_id(0)
        lg_ = lg_ref[...]

        @pl.when(i == 0)
        def _():
            dlg_ref[...] = jnp.zeros(dlg_ref.shape, F32)
            dlb_ref[...] = jnp.zeros(dlb_ref.shape, F32)

        for rs in _row_parts(tm):
            dm = dmix_ref[rs, :]
            dya_ref[rs, :] = _dot(dm, w_ref[0:wa_, :], NT)
            dyb = _dot(dm, w_ref[wa_:d, :], NT)
            rl, zh, l = _layer_norm_parts(z_ref[rs, :], lg_, lb_ref[...])
            sl = _sig(l)
            dl = dyb * (sl * (1.0 + l * (1.0 - sl)))
            dzh = dl * lg_
            dz_ref[rs, :] = rl * (dzh - _mean(dzh) - zh * _mean(dzh * zh))
            dlg_ref[...] += jnp.sum(dl * zh, axis=0, keepdims=True)
            dlb_ref[...] += jnp.sum(dl, axis=0, keepdims=True)

    row = lambda i: (i, 0)
    return pl.pallas_call(
        body, name="mix_bwd_out", grid=(tp // tm,),
        in_specs=[pl.BlockSpec((tm, d), row), _resident(wout.shape), pl.BlockSpec((tm, wa_), row), _full(lg.shape),
                  _full(lb.shape)],
        out_specs=[pl.BlockSpec((tm, wa_), row), pl.BlockSpec((tm, wa_), row), _full((1, wa_)), _full((1, wa_))],
        out_shape=[_sds((tp, wa_), F32), _sds((tp, wa_), F32), _sds((1, wa_), F32), _sds((1, wa_), F32)],
        compiler_params=_params(("arbitrary",)),
    )(dmix, wout, z, lg, lb)


def _mix_conv_bwd(hp5, dya, dz, wa, wb):
    _, tp, wgrp = hp5.shape
    seq, nseq = _seq_rows(tp)
    sb = nseq + 2 * CONV_HIST
    ka, kb = wa.shape[0], wb.shape[0]
    xs, ms = slice(0, seq), slice(seq, seq + N_META)
    ox, om = slice(CONV_HIST + N_META, CONV_HIST + nseq), slice(CONV_HIST, CONV_HIST + N_META)
    n_tail = tp - seq - N_META

    def body(hp_ref, dya_ref, dz_ref, wa_ref, wb_ref, dhp_ref, dwa_ref, dwb_ref, dbb_ref, s_ref, d_ref, o_ref, acc_ref,
             shs_ref, shd_ref):
        _zero_ends(s_ref, nseq)
        _zero_ends(d_ref, nseq)

        def put(p, ox_val, om_val):
            dhp_ref[p, xs, :] = ox_val.astype(BF16)
            dhp_ref[p, ms, :] = om_val.astype(BF16)
            dhp_ref[p, seq + N_META:tp, :] = jnp.zeros((n_tail, LANES), BF16)

        def wgrad(dw_ref, width):
            for k in range(width):
                dw_ref[k:k + 1, :] = jnp.sum(acc_ref[8 * k:8 * k + 8, :], axis=0, keepdims=True)

        _to_seq(s_ref, hp_ref[1, xs, :] * hp_ref[2, xs, :], hp_ref[1, ms, :] * hp_ref[2, ms, :], seq)
        _conv_taps(s_ref, shs_ref, wa_ref, o_ref, ka, nseq, False)
        put(0, dya_ref[xs, :] * o_ref[ox, :], dya_ref[ms, :] * o_ref[om, :])
        _to_seq(d_ref, dya_ref[xs, :] * hp_ref[0, xs, :], dya_ref[ms, :] * hp_ref[0, ms, :], seq)
        _conv_wgrad(s_ref, shs_ref, d_ref, acc_ref, ka, nseq)
        wgrad(dwa_ref, ka)
        _conv_taps(d_ref, shd_ref, wa_ref, o_ref, ka, nseq, True)
        put(1, o_ref[ox, :] * hp_ref[2, xs, :], o_ref[om, :] * hp_ref[2, ms, :])
        put(2, o_ref[ox, :] * hp_ref[1, xs, :], o_ref[om, :] * hp_ref[1, ms, :])

        _to_seq(s_ref, hp_ref[3, xs, :] * _sig(hp_ref[4, xs, :]), hp_ref[3, ms, :] * _sig(hp_ref[4, ms, :]), seq)
        _to_seq(d_ref, dz_ref[xs, :], dz_ref[ms, :], seq)
        dbb_ref[...] = (jnp.sum(dz_ref[xs, :], axis=0, keepdims=True)
                        + jnp.sum(dz_ref[ms, :], axis=0, keepdims=True))
        _shift_copies(s_ref, shs_ref, kb, False)
        _conv_wgrad(s_ref, shs_ref, d_ref, acc_ref, kb, nseq)
        wgrad(dwb_ref, kb)
        _conv_taps(d_ref, shd_ref, wb_ref, o_ref, kb, nseq, True)
        sx, sm = _sig(hp_ref[4, xs, :]), _sig(hp_ref[4, ms, :])
        put(3, o_ref[ox, :] * sx, o_ref[om, :] * sm)
        put(4, o_ref[ox, :] * hp_ref[3, xs, :] * sx * (1.0 - sx), o_ref[om, :] * hp_ref[3, ms, :] * sm * (1.0 - sm))

    col = lambda j: (0, j)
    blk5 = pl.BlockSpec((5, tp, LANES), lambda j: (0, 0, j))
    return pl.pallas_call(
        body, name="mix_conv_bwd", grid=(wgrp // LANES,),
        in_specs=[blk5, pl.BlockSpec((tp, LANES), col), pl.BlockSpec((tp, LANES), col),
                  pl.BlockSpec((ka, LANES), col), pl.BlockSpec((kb, LANES), col)],
        out_specs=[blk5, pl.BlockSpec((ka, LANES), col), pl.BlockSpec((kb, LANES), col), pl.BlockSpec((1, LANES), col)],
        out_shape=[_sds((5, tp, wgrp), BF16), _sds((ka, wgrp), F32), _sds((kb, wgrp), F32), _sds((1, wgrp), F32)],
        scratch_shapes=[pltpu.VMEM((sb, LANES), F32), pltpu.VMEM((sb, LANES), F32), pltpu.VMEM((sb, LANES), F32),
                        pltpu.VMEM((SUBLANES * kb, LANES), F32), pltpu.VMEM((SUBLANES - 1, sb, LANES), F32),
                        pltpu.VMEM((SUBLANES - 1, sb, LANES), F32)],
        compiler_params=_params(("arbitrary",)),
    )(hp5, dya, dz, wa, wb)


def _grad_w_in(xn1, dhp5):
    n_p, tp, pw = dhp5.shape
    d = xn1.shape[1]

    def body(xn_ref, dhp_ref, g_ref):
        g_ref[...] = _dot(xn_ref[...], dhp_ref[0], TN).astype(BF16)

    return pl.pallas_call(
        body, name="grad_w_in", grid=(n_p,),
        in_specs=[_resident(xn1.shape), pl.BlockSpec((1, tp, pw), lambda p: (p, 0, 0))],
        out_specs=pl.BlockSpec((d, pw), lambda p: (0, p)),
        out_shape=_sds((d, n_p * pw), BF16),
        compiler_params=_params(("arbitrary",)),
    )(xn1, dhp5)


def _mix_bwd_in(dhp5, win4, h, dh1, g1):
    n_p, tp, pw = dhp5.shape
    d = h.shape[1]
    n_sh, _, csh = win4.shape
    tm = _row_tile(tp)
    pieces = _pieces(n_sh, csh, pw)

    seq, _ = _seq_rows(tp)
    last, meta_off = seq // tm, seq % tm
    assert last == tp // tm - 1
    assert any(rs.start <= meta_off and meta_off + N_META <= rs.stop for rs in _row_parts(tm))

    def body(dhp_ref, w_ref, h_ref, dh1_ref, g_ref, gx_ref, dmeta_ref, dg1_ref):
        i = pl.program_id(0)

        @pl.when(i == 0)
        def _():
            dg1_ref[...] = jnp.zeros(dg1_ref.shape, F32)

        for rs in _row_parts(tm):
            dxn = None
            for k, klo, p, plo, w in pieces:
                t = _dot(dhp_ref[p, rs, plo:plo + w], w_ref[k, :, klo:klo + w], NT)
                dxn = t if dxn is None else dxn + t
            hh = h_ref[rs, :]
            r1 = lax.rsqrt(_mean(hh * hh) + RMS_EPS)
            dres, dg_rows = _rms_bwd(dxn, hh, r1, g_ref[...])
            dh = dh1_ref[rs, :] + dres
            gx_ref[rs, :] = dh
            dg1_ref[...] += jnp.sum(dg_rows, axis=0, keepdims=True)
            if rs.start <= meta_off and meta_off + N_META <= rs.stop:
                @pl.when(i == last)
                def _():
                    dmeta_ref[...] = dh[meta_off - rs.start:meta_off - rs.start + N_META, :]

    row = lambda i: (i, 0)
    return pl.pallas_call(
        body, name="mix_bwd_in", grid=(tp // tm,),
        in_specs=[pl.BlockSpec((n_p, tm, pw), lambda i: (0, i, 0)), _resident(win4.shape), pl.BlockSpec((tm, d), row),
                  pl.BlockSpec((tm, d), row), _full(g1.shape)],
        out_specs=[pl.BlockSpec((tm, d), row), _full((N_META, d)), _full((1, d))],
        out_shape=[_sds((seq, d), F32), _sds((N_META, d), F32), _sds((1, d), F32)],
        compiler_params=_params(("arbitrary",)),
    )(dhp5, win4, h, dh1, g1)


def _other_chips(x, y):
    out = []
    for j in (1, 2, 3):
        px, py = _flip(x, j >> 1), _flip(y, j & 1)
        out.append((px, py, 2 * px + py))
    return out


PAIR_COLLECTIVE_ID = 0


def _pair_barrier(x, y, c):
    sem = pltpu.get_barrier_semaphore()
    pl.semaphore_signal(sem, inc=1, device_id=(x, y, 1 - c), device_id_type=MESH)
    pl.semaphore_wait(sem, 1)


def _pair_params():
    return pltpu.CompilerParams(collective_id=PAIR_COLLECTIVE_ID)


def _half_rows(c, rows_half):
    return pl.ds(pl.multiple_of(c * rows_half, 8), rows_half)


def _cast_place(w, q_arr, tag, after=None):
    rows, cols = w.shape
    tr = _row_tile(rows)
    extra = [] if after is None else [after]

    def body(q_ref, w_ref, *rest):
        rest[-1][0] = w_ref[...].astype(BF16)

    return pl.pallas_call(
        body, name="cast_place_" + tag,
        grid_spec=pltpu.PrefetchScalarGridSpec(
            num_scalar_prefetch=1, grid=(rows // tr,),
            in_specs=[pl.BlockSpec((tr, cols), lambda i, q: (i, 0))] + [ANY] * len(extra),
            out_specs=pl.BlockSpec((1, tr, cols), lambda i, q: (q[0], i, 0))),
        out_shape=_sds((N_CHIPS, rows, cols), BF16),
        compiler_params=_params(("arbitrary",)),
    )(q_arr, w, *extra)


def _gather_shards(fulls):
    n = len(fulls)
    halves = [a.shape[1] // 2 for a in fulls]

    def body(*refs):
        full = refs[n:2 * n]
        ssem, rsem = refs[2 * n:]
        x, y, c = _mesh_pos()
        chips = _other_chips(x, y)

        def remote(i, chip_no, half, to, s):
            part = full[i].at[chip_no, _half_rows(half, halves[i]), :]
            return pltpu.make_async_remote_copy(src_ref=part, dst_ref=part, send_sem=ssem.at[s], recv_sem=rsem.at[s],
                                                device_id=to, device_id_type=MESH)

        first = []
        for i in range(n):
            for j, (px, py, _) in enumerate(chips):
                cp = remote(i, 2 * x + y, c, (px, py, c), 3 * i + j)
                cp.start()
                first.append(cp)
        passed = []
        for i in range(n):
            for j, (_, _, qj) in enumerate(chips):
                remote(i, qj, c, (x, y, c), 3 * i + j).wait_recv()
                cp = remote(i, qj, c, (x, y, 1 - c), 3 * n + 3 * i + j)
                cp.start()
                passed.append(cp)
        for i in range(n):
            for j, (_, _, qj) in enumerate(chips):
                remote(i, qj, 1 - c, (x, y, c), 3 * n + 3 * i + j).wait_recv()
        for cp in first + passed:
            cp.wait_send()

    return pl.pallas_call(
        body, name="gather_shards",
        in_specs=[ANY] * n, out_specs=[ANY] * n,
        out_shape=[_sds(a.shape, a.dtype) for a in fulls],
        input_output_aliases={i: i for i in range(n)},
        scratch_shapes=[pltpu.SemaphoreType.DMA((6 * n,)), pltpu.SemaphoreType.DMA((6 * n,))],
    )(*fulls)


HBM = pl.BlockSpec(memory_space=pltpu.HBM)
SEM = pl.BlockSpec(memory_space=pltpu.SEMAPHORE)
EFFECT = pltpu.SideEffectType.DATAFLOW_SIDE_EFFECTING


def _in_hbm(a):
    return pltpu.with_memory_space_constraint(a, pltpu.HBM)


def _gather_start(fulls, after, tag):
    n = len(fulls)
    halves = [a.shape[1] // 2 for a in fulls]

    def body(*refs):
        land = refs[:n]
        ssem, rsem = refs[n + 1], refs[n + 2]
        token = refs[-1]
        x, y, c = _mesh_pos()
        q = 2 * x + y
        for i in range(n):
            for j, (px, py, _) in enumerate(_other_chips(x, y)):
                mine = land[i].at[q, _half_rows(c, halves[i]), :]
                pltpu.make_async_remote_copy(src_ref=mine, dst_ref=mine, send_sem=ssem.at[3 * i + j],
                                             recv_sem=rsem.at[3 * i + j], device_id=(px, py, c), device_id_type=MESH).start()
        token[...] = jnp.zeros(token.shape, F32)

    outs = pl.pallas_call(
        body, name="gather_start_" + tag,
        in_specs=[HBM] * n + [ANY], out_specs=[SEM, SEM] + [HBM] * n + [VMEM],
        out_shape=[pltpu.SemaphoreType.DMA((3 * n,)), pltpu.SemaphoreType.DMA((3 * n,))]
        + [pltpu.HBM(a.shape, a.dtype) for a in fulls] + [_sds((8, LANES), F32)],
        input_output_aliases={i: 2 + i for i in range(n)},
        compiler_params=pltpu.CompilerParams(has_side_effects=EFFECT),
    )(*[_in_hbm(a) for a in fulls], after)
    return outs[0], outs[1], list(outs[2:2 + n]), outs[-1]


def _gather_wait(which, ssem, rsem, lands, after, tag):
    m = len(which)
    halves = [a.shape[1] // 2 for a in lands]

    def body(*refs):
        land = refs[:m]
        ssem_, rsem_ = refs[m], refs[m + 1]
        x, y, c = _mesh_pos()
        for t, i in enumerate(which):
            for j, (px, py, qj) in enumerate(_other_chips(x, y)):
                rows = _half_rows(c, halves[t])
                cp = pltpu.make_async_remote_copy(src_ref=land[t].at[2 * x + y, rows, :], dst_ref=land[t].at[qj, rows, :],
                                                  send_sem=ssem_.at[3 * i + j], recv_sem=rsem_.at[3 * i + j],
                                                  device_id=(px, py, c), device_id_type=MESH)
                cp.wait_send()
                cp.wait_recv()

    outs = pl.pallas_call(
        body, name="gather_wait_" + tag,
        in_specs=[HBM] * m + [SEM, SEM, ANY], out_specs=[HBM] * m,
        out_shape=[pltpu.HBM(a.shape, a.dtype) for a in lands],
        input_output_aliases={i: i for i in range(m)},
        compiler_params=pltpu.CompilerParams(has_side_effects=EFFECT),
    )(*lands, ssem, rsem, after)
    return list(outs)


def _forward_pair(lands, tag):
    n = len(lands)
    halves = [a.shape[1] // 2 for a in lands]

    def body(*refs):
        full = refs[n:2 * n]
        ssem, rsem = refs[2 * n:]
        x, y, c = _mesh_pos()
        _pair_barrier(x, y, c)
        cps = []
        for i in range(n):
            for j, (_, _, qj) in enumerate(_other_chips(x, y)):
                part = full[i].at[qj, _half_rows(c, halves[i]), :]
                cp = pltpu.make_async_remote_copy(src_ref=part, dst_ref=part, send_sem=ssem.at[3 * i + j],
                                                  recv_sem=rsem.at[3 * i + j], device_id=(x, y, 1 - c), device_id_type=MESH)
                cp.start()
                cps.append(cp)
        for cp in cps:
            cp.wait()

    return pl.pallas_call(
        body, name="forward_pair_" + tag,
        in_specs=[ANY] * n, out_specs=[ANY] * n,
        out_shape=[_sds(a.shape, a.dtype) for a in lands],
        input_output_aliases={i: i for i in range(n)},
        scratch_shapes=[pltpu.SemaphoreType.DMA((3 * n,)), pltpu.SemaphoreType.DMA((3 * n,))],
        compiler_params=_pair_params(),
    )(*lands)


def _chip_exchange_start(parts, after, tag):
    n = len(parts)

    def body(*refs):
        src, land = refs[:n], refs[n:2 * n]
        ssem, rsem = refs[2 * n + 1], refs[2 * n + 2]
        token = refs[-1]
        x, y, c = _mesh_pos()
        for i in range(n):
            for j, (px, py, qj) in enumerate(_other_chips(x, y)):
                pltpu.make_async_remote_copy(src_ref=src[i].at[qj], dst_ref=land[i].at[j], send_sem=ssem.at[3 * i + j],
                                             recv_sem=rsem.at[3 * i + j], device_id=(px, py, c), device_id_type=MESH).start()
        token[...] = jnp.zeros(token.shape, F32)

    lands = [lax.empty((3,) + a.shape[1:], a.dtype) for a in parts]
    outs = pl.pallas_call(
        body, name="chip_exchange_start_" + tag,
        in_specs=[HBM] * (2 * n) + [ANY], out_specs=[SEM, SEM] + [HBM] * (2 * n) + [VMEM],
        out_shape=[pltpu.SemaphoreType.DMA((3 * n,)), pltpu.SemaphoreType.DMA((3 * n,))]
        + [pltpu.HBM(a.shape, a.dtype) for a in parts] + [pltpu.HBM(a.shape, a.dtype) for a in lands]
        + [_sds((8, LANES), F32)],
        input_output_aliases={i: 2 + i for i in range(2 * n)},
        compiler_params=pltpu.CompilerParams(has_side_effects=EFFECT),
    )(*[_in_hbm(a) for a in parts], *[_in_hbm(a) for a in lands], after)
    return outs[0], outs[1], list(outs[2:2 + n]), list(outs[2 + n:2 + 2 * n]), outs[-1]


def _chip_exchange_wait(ssem, rsem, parts, lands, after, tag):
    n = len(parts)

    def body(*refs):
        src, land = refs[:n], refs[n:2 * n]
        ssem_, rsem_ = refs[2 * n], refs[2 * n + 1]
        x, y, c = _mesh_pos()
        for i in range(n):
            for j, (px, py, qj) in enumerate(_other_chips(x, y)):
                cp = pltpu.make_async_remote_copy(src_ref=src[i].at[qj], dst_ref=land[i].at[j], send_sem=ssem_.at[3 * i + j],
                                                  recv_sem=rsem_.at[3 * i + j], device_id=(px, py, c), device_id_type=MESH)
                cp.wait_send()
                cp.wait_recv()

    outs = pl.pallas_call(
        body, name="chip_exchange_wait_" + tag,
        in_specs=[HBM] * (2 * n) + [SEM, SEM, ANY], out_specs=[HBM] * (2 * n),
        out_shape=[pltpu.HBM(a.shape, a.dtype) for a in parts] + [pltpu.HBM(a.shape, a.dtype) for a in lands],
        input_output_aliases={i: i for i in range(2 * n)},
        compiler_params=pltpu.CompilerParams(has_side_effects=EFFECT),
    )(*parts, *lands, ssem, rsem, after)
    return list(outs[:n]), list(outs[n:])


def _pair_exchange_grads(grads, half_axis, tag):
    n = len(grads)

    def half_of(ref, i, which):
        rows = grads[i].shape[half_axis[i]] // 2
        if half_axis[i] == 0:
            return ref.at[_half_rows(which, rows), :]
        return ref.at[:, _half_rows(which, rows), :]

    def out_shape(i):
        s = list(grads[i].shape)
        s[half_axis[i]] //= 2
        return _sds(tuple(s), grads[i].dtype)

    def body(*refs):
        g, got = refs[:n], refs[n:2 * n]
        ssem, rsem = refs[2 * n:]
        x, y, c = _mesh_pos()
        _pair_barrier(x, y, c)
        cps = []
        for i in range(n):
            cp = pltpu.make_async_remote_copy(src_ref=half_of(g[i], i, 1 - c), dst_ref=got[i], send_sem=ssem.at[i],
                                              recv_sem=rsem.at[i], device_id=(x, y, 1 - c), device_id_type=MESH)
            cp.start()
            cps.append(cp)
        for cp in cps:
            cp.wait()

    return pl.pallas_call(
        body, name="pair_exchange_grads_" + tag,
        in_specs=[ANY] * n, out_specs=[ANY] * n,
        out_shape=[out_shape(i) for i in range(n)],
        scratch_shapes=[pltpu.SemaphoreType.DMA((n,)), pltpu.SemaphoreType.DMA((n,))],
        compiler_params=_pair_params(),
    )(*grads)


def _pair_sum(gs, gots, c_arr, col_sharded, tag):
    n = len(gs)
    if col_sharded:
        rows, cols = gs[0].shape
        rh, cs = rows // 2, cols // N_CHIPS
        g_spec = pl.BlockSpec((rh, cs), lambda k, c_ref: (c_ref[0], k))
        got_spec = pl.BlockSpec((rh, cs), lambda k, c_ref: (0, k))
    else:
        _, rows, cs = gs[0].shape
        rh = rows // 2
        g_spec = pl.BlockSpec((1, rh, cs), lambda k, c_ref: (k, c_ref[0], 0))
        got_spec = pl.BlockSpec((1, rh, cs), lambda k, c_ref: (k, 0, 0))

    def body(c_ref, *refs):
        for g_ref, got_ref, out_ref in zip(refs[:n], refs[n:2 * n], refs[2 * n:]):
            total = g_ref[...].astype(F32) + got_ref[...].astype(F32)
            out_ref[...] = total.astype(BF16).reshape(out_ref.shape)

    return list(pl.pallas_call(
        body, name="pair_sum_" + tag,
        grid_spec=pltpu.PrefetchScalarGridSpec(
            num_scalar_prefetch=1, grid=(N_CHIPS,), in_specs=[g_spec] * n + [got_spec] * n,
            out_specs=[pl.BlockSpec((1, rh, cs), lambda k, c_ref: (k, 0, 0))] * n),
        out_shape=[_sds((N_CHIPS, rh, cs), BF16)] * n,
        compiler_params=_params(("arbitrary",)),
    )(c_arr, *gs, *gots))


def _chip_sum(parts, gots, qc_arr, tag):
    n = len(parts)
    _, rh, cs = parts[0].shape
    steps = 2 if rh % 32 == 0 else 1
    rb = rh // steps

    def body(qc_ref, *refs):
        for part_ref, got_ref, out_ref in zip(refs[:n], refs[n:2 * n], refs[2 * n:]):
            total = part_ref[0].astype(F32)
            for j in range(3):
                total = total + got_ref[j].astype(F32)
            out_ref[...] = total

    return list(pl.pallas_call(
        body, name="chip_sum_" + tag,
        grid_spec=pltpu.PrefetchScalarGridSpec(
            num_scalar_prefetch=1, grid=(steps,),
            in_specs=[pl.BlockSpec((1, rb, cs), lambda i, qc: (qc[0], i, 0))] * n
            + [pl.BlockSpec((3, rb, cs), lambda i, qc: (0, i, 0))] * n,
            out_specs=[pl.BlockSpec((rb, cs), lambda i, qc: (qc[1] * steps + i, 0))] * n),
        out_shape=[_sds((2 * rh, cs), F32)] * n,
        compiler_params=_params(("arbitrary",)),
    )(qc_arr, *parts, *gots))


def _pair_share_grads(grads, tag):
    n = len(grads)

    def body(*refs):
        g = refs[n:2 * n]
        ssem, rsem = refs[2 * n:]
        x, y, c = _mesh_pos()
        _pair_barrier(x, y, c)
        cps = []
        for i in range(n):
            mine = g[i].at[_half_rows(c, grads[i].shape[0] // 2), :]
            cp = pltpu.make_async_remote_copy(src_ref=mine, dst_ref=mine, send_sem=ssem.at[i], recv_sem=rsem.at[i],
                                              device_id=(x, y, 1 - c), device_id_type=MESH)
            cp.start()
            cps.append(cp)
        for cp in cps:
            cp.wait()

    return pl.pallas_call(
        body, name="pair_share_grads_" + tag,
        in_specs=[ANY] * n, out_specs=[ANY] * n,
        out_shape=[_sds(a.shape, a.dtype) for a in grads],
        input_output_aliases={i: i for i in range(n)},
        scratch_shapes=[pltpu.SemaphoreType.DMA((n,)), pltpu.SemaphoreType.DMA((n,))],
        compiler_params=_pair_params(),
    )(*grads)


def _small_allreduce(parts, places, rows_total, width):
    n = len(parts)

    def body(*refs):
        ins, out_ref = refs[:n], refs[n]
        pack, pair_got, chip_sum, got, ssem, rsem = refs[n + 1:]
        x, y, c = _mesh_pos()
        chip = 2 * x + y
        pack[...] = jnp.zeros(pack.shape, F32)
        for i in range(n):
            for row, col, src_row, rows in places[i]:
                w = parts[i].shape[1]
                pack[row:row + rows, col:col + w] = ins[i][src_row:src_row + rows, :]
        swap = pltpu.make_async_remote_copy(src_ref=pack, dst_ref=pair_got, send_sem=ssem.at[3], recv_sem=rsem.at[3],
                                            device_id=(x, y, 1 - c), device_id_type=MESH)
        swap.start()
        swap.wait()
        chip_sum[...] = pack[...] + pair_got[...]
        cps = []
        for j, (px, py, _) in enumerate(_other_chips(x, y)):
            cp = pltpu.make_async_remote_copy(src_ref=chip_sum, dst_ref=got.at[j], send_sem=ssem.at[j],
                                              recv_sem=rsem.at[j], device_id=(px, py, c), device_id_type=MESH)
            cp.start()
            cps.append(cp)
        for cp in cps:
            cp.wait()
        total = jnp.zeros(pack.shape, F32)
        for q in range(N_CHIPS):
            rel = jnp.bitwise_xor(chip, q)
            theirs = got[jnp.maximum(rel - 1, 0)]
            total = total + jnp.where(rel == 0, chip_sum[...], theirs)
        out_ref[...] = total

    return pl.pallas_call(
        body, name="small_allreduce",
        in_specs=[VMEM] * n, out_specs=VMEM,
        out_shape=_sds((rows_total, width), F32),
        scratch_shapes=[pltpu.VMEM((rows_total, width), F32), pltpu.VMEM((rows_total, width), F32),
                        pltpu.VMEM((rows_total, width), F32), pltpu.VMEM((3, rows_total, width), F32),
                        pltpu.SemaphoreType.DMA((4,)), pltpu.SemaphoreType.DMA((4,))],
        compiler_params=_params(),
    )(*parts)


def _adamw_math(w, g, m, v):
    m2 = ADAM_B1 * m + (1.0 - ADAM_B1) * g
    v2 = ADAM_B2 * v + (1.0 - ADAM_B2) * (g * g)
    m_hat = m2 / (1.0 - ADAM_B1 ** ADAM_STEP)
    v_hat = v2 / (1.0 - ADAM_B2 ** ADAM_STEP)
    delta = -ADAM_LR * (m_hat / (jnp.sqrt(v_hat) + ADAM_EPS) + ADAM_WD * w)
    return delta, m2, v2


ADAMW_BLOCK_BYTES = 3 * 2 ** 19


def _adamw_big(ws, gs, ms, vs, tag):
    n = len(ws)
    rows, cols = ws[0].shape
    tr = 16
    for t in range(16, rows + 1, 16):
        if rows % t == 0 and t * cols * 4 * n <= ADAMW_BLOCK_BYTES:
            tr = t

    def body(*refs):
        ins, outs = refs[:4 * n], refs[4 * n:]
        for i in range(n):
            w_ref, g_ref, m_ref, v_ref = ins[i], ins[n + i], ins[2 * n + i], ins[3 * n + i]
            gg = g_ref[...]
            outs[4 * i][...] = gg
            outs[4 * i + 1][...], outs[4 * i + 2][...], outs[4 * i + 3][...] = _adamw_math(
                w_ref[...], gg, m_ref[...], v_ref[...])

    spec = pl.BlockSpec((tr, cols), lambda i: (i, 0))
    outs = pl.pallas_call(
        body, name="adamw_" + tag, grid=(rows // tr,),
        in_specs=[spec] * (4 * n), out_specs=[spec] * (4 * n),
        out_shape=[_sds((rows, cols), F32)] * (4 * n),
        compiler_params=_params(("arbitrary",)),
    )(*ws, *gs, *ms, *vs)
    return [outs[4 * i:4 * i + 4] for i in range(n)]


def _adamw_small(ws, gs, ms, vs):
    n = len(ws)

    def body(*refs):
        w_r, g_r, m_r, v_r = refs[:n], refs[n:2 * n], refs[2 * n:3 * n], refs[3 * n:4 * n]
        d_o, m_o, v_o = refs[4 * n:5 * n], refs[5 * n:6 * n], refs[6 * n:7 * n]
        for i in range(n):
            d_o[i][...], m_o[i][...], v_o[i][...] = _adamw_math(w_r[i][...], g_r[i][...], m_r[i][...], v_r[i][...])

    shapes = [_sds(w.shape, F32) for w in ws]
    outs = pl.pallas_call(
        body, name="adamw_small",
        in_specs=[VMEM] * (4 * n), out_specs=[VMEM] * (3 * n),
        out_shape=shapes * 3,
        compiler_params=_params(),
    )(*ws, *gs, *ms, *vs)
    return outs[:n], outs[n:2 * n], outs[2 * n:]


SMALL_ROWS = 40
PACK_ROWS = 64


def kernel(x, meta_tokens, pre_mix_norm, w_in, conv_a_w, conv_b_w, conv_b_bias, ln_b_gain, ln_b_bias, w_out, post_mix_norm, pre_ffn_norm, w_gate, w_up, w_down, post_ffn_norm, loss_target, m_meta_tokens, m_pre_mix_norm, m_w_in, m_conv_a_w, m_conv_b_w, m_conv_b_bias, m_ln_b_gain, m_ln_b_bias, m_w_out, m_post_mix_norm, m_pre_ffn_norm, m_w_gate, m_w_up, m_w_down, m_post_ffn_norm, v_meta_tokens, v_pre_mix_norm, v_w_in, v_conv_a_w, v_conv_b_w, v_conv_b_bias, v_ln_b_gain, v_ln_b_bias, v_w_out, v_post_mix_norm, v_pre_ffn_norm, v_w_gate, v_w_up, v_w_down, v_post_ffn_norm):
    xq, yq, cq = lax.axis_index("x"), lax.axis_index("y"), lax.axis_index("c")
    chip = 2 * xq + yq
    c_arr = jnp.reshape(cq, (1,)).astype(jnp.int32)
    qc_arr = jnp.stack([chip, cq]).astype(jnp.int32)

    seq, d = x.shape[1], x.shape[2]
    x2, tgt2 = x[0], loss_target[0]
    tr = lambda a: jnp.swapaxes(a, 1, 2)[0]
    w_in2, w_out2, w_gate2, w_up2, w_down2 = w_in[0], w_out[0], tr(w_gate), tr(w_up), w_down[0]
    ka, wa_sh = conv_a_w.shape[1], conv_a_w.shape[2]
    kb = conv_b_w.shape[1]
    meta_sh = meta_tokens.shape[1]

    small = jnp.zeros((PACK_ROWS, meta_sh), F32)
    small = small.at[0:N_META, :].set(meta_tokens)
    small = small.at[16:16 + ka, 0:wa_sh].set(conv_a_w[0])
    small = small.at[24:24 + kb, 0:wa_sh].set(conv_b_w[0])
    q_arr = jnp.reshape(chip, (1,)).astype(jnp.int32)
    small_own = lax.dynamic_update_slice(jnp.zeros((N_CHIPS, PACK_ROWS, meta_sh), F32), small[None], (chip, 0, 0))
    i_ssem, i_rsem, first, i_token = _gather_start([_cast_place(w_in2, q_arr, "w_in"), small_own], pre_mix_norm, "in")
    rest = [_cast_place(w, q_arr, nm, i_token)
            for w, nm in ((w_out2, "w_out"), (w_gate2, "w_gate"), (w_up2, "w_up"), (w_down2, "w_down"))]
    g_ssem, g_rsem, lands, g_token = _gather_start(rest, i_token, "rest")
    win4, small4 = _forward_pair(_gather_wait([0, 1], i_ssem, i_rsem, first, g_token, "in"), "in")
    meta_f = jnp.concatenate([small4[k, 0:N_META, :] for k in range(N_CHIPS)], axis=1)
    wa_f = jnp.concatenate([small4[k, 16:16 + ka, 0:wa_sh] for k in range(N_CHIPS)], axis=1)
    wb_f = jnp.concatenate([small4[k, 24:24 + kb, 0:wa_sh] for k in range(N_CHIPS)], axis=1)

    tm = _row_tile(seq + TAIL_ROWS)
    tail = lax.dynamic_update_slice(jnp.zeros((tm, d), F32), meta_f, (seq % tm, 0))
    h, xn1, hp5 = _mm_in(x2, tail, win4, pre_mix_norm + g_token[0:1, 0:1])
    ya, z = _mix_conv_fwd(hp5, wa_f, wb_f, conv_b_bias)
    (wout4,) = _forward_pair(_gather_wait([0], g_ssem, g_rsem, lands[0:1], z, "out"), "out")
    wout_f = wout4.reshape(N_CHIPS * wout4.shape[1], wout4.shape[2])
    yb, mix, h1, xn2 = _mm_out(ya, z, h, wout_f, ln_b_gain, ln_b_bias, post_mix_norm, pre_ffn_norm)
    wg4, wu4 = _forward_pair(_gather_wait([1, 2], g_ssem, g_rsem, lands[1:3], xn2, "gate_up"), "gate_up")
    stacked = lambda a: a.reshape(a.shape[0] * a.shape[1], a.shape[2])
    wg_f, wu_f = stacked(wg4), stacked(wu4)
    p_act, q_act, f_act = _ffn_up(xn2, wg_f, wu_f)
    (wd4,) = _forward_pair(_gather_wait([3], g_ssem, g_rsem, lands[3:4], f_act, "down"), "down")
    wd_f = stacked(wd4)
    dff, dh2, loss_blk, d_gpf = _ffn_down(f_act, wd_f, h1, tgt2, post_ffn_norm)

    da, du = _ffn_bwd_act(dff, wd_f, p_act, q_act)
    by_chip = lambda g: g.reshape(N_CHIPS, g.shape[0] // N_CHIPS, g.shape[1])
    g_down = by_chip(_grad_w_down(f_act, dff))
    g_gate, g_up = [by_chip(g) for g in _grad_w_gate_up(xn2, da, du)]
    ffn = [g_gate, g_up, g_down]
    got = _pair_exchange_grads(ffn, [1, 1, 1], "ffn")
    parts = _pair_sum(ffn, got, c_arr, False, "ffn")
    f_ssem, f_rsem, parts, f_lands, f_token = _chip_exchange_start(parts, dff, "ffn")
    dh1, dmix, d_g2, d_gpm = _ffn_bwd_in(da, du, wg_f, wu_f, h1, mix, dh2, pre_ffn_norm + f_token[0:1, 0:1], post_mix_norm)
    g_out = _grad_w_out(ya, yb, dmix)
    dya, dz, d_lg, d_lb = _mix_bwd_out(dmix, wout_f, z, ln_b_gain, ln_b_bias)
    dhp5, d_wa, d_wb, d_bb = _mix_conv_bwd(hp5, dya, dz, wa_f, wb_f)
    g_in = _grad_w_in(xn1, dhp5)

    g_out4 = g_out.reshape(N_CHIPS, g_out.shape[0] // N_CHIPS, g_out.shape[1])
    mixw = [g_in, g_out4]
    got2 = _pair_exchange_grads(mixw, [0, 1], "mix")
    parts2 = _pair_sum(mixw[0:1], got2[0:1], c_arr, True, "in") + _pair_sum(mixw[1:2], got2[1:2], c_arr, False, "out")
    m_ssem, m_rsem, parts2, m_lands, m_token = _chip_exchange_start(parts2, dhp5, "mix")
    grad_x2, d_meta, d_g1 = _mix_bwd_in(dhp5, win4, h, dh1, pre_mix_norm + m_token[0:1, 0:1])
    grad_x = grad_x2[None]

    parts, f_recv = _chip_exchange_wait(f_ssem, f_rsem, parts, f_lands, d_g1, "ffn")
    halves = _chip_sum(parts, f_recv, qc_arr, "ffn")
    gsum_ffn = _pair_share_grads(halves, "ffn")

    hw = d // 2
    assert d_wa.shape == (3, hw) and d_wb.shape == (31, hw) and d_bb.shape == (1, hw)
    small_parts = [d_meta, d_g1, d_gpm, d_g2, d_gpf, d_bb, d_lg, d_lb, loss_blk[0:1, :], d_wa, d_wb]
    places = [[(0, 0, 0, N_META)], [(16, 0, 0, 1)], [(17, 0, 0, 1)], [(18, 0, 0, 1)], [(19, 0, 0, 1)],
              [(20, 0, 0, 1)], [(20, hw, 0, 1)], [(21, 0, 0, 1)], [(21, hw, 0, 1)], [(22, 0, 0, 3)],
              [(22, hw, 0, 3), (25, 0, 3, 14), (25, hw, 17, 14)]]
    red = _small_allreduce(small_parts, places, SMALL_ROWS, d)
    s_meta, s_g1, s_gpm, s_g2, s_gpf = red[0:N_META], red[16:17], red[17:18], red[18:19], red[19:20]
    s_bb, s_lg, s_lb, s_loss = red[20:21, 0:hw], red[20:21, hw:d], red[21:22, 0:hw], red[21:22, hw:hw + LANES]
    s_wa = red[22:25, 0:hw]
    s_wb = jnp.concatenate([red[22:25, hw:d], red[25:39, 0:hw], red[25:39, hw:d]], axis=0)
    g_meta = lax.dynamic_slice_in_dim(s_meta, chip * meta_sh, meta_sh, axis=1)
    g_wa = lax.dynamic_slice_in_dim(s_wa, chip * wa_sh, wa_sh, axis=1)[None]
    g_wb = lax.dynamic_slice_in_dim(s_wb, chip * wa_sh, wa_sh, axis=1)[None]
    loss = s_loss[0, 0]

    names_big = ["w_in", "w_out", "w_gate", "w_up", "w_down"]
    w_big = dict(zip(names_big, [w_in2, w_out2, w_gate2, w_up2, w_down2]))
    m_big = dict(zip(names_big, [m_w_in[0], m_w_out[0], tr(m_w_gate), tr(m_w_up), m_w_down[0]]))
    v_big = dict(zip(names_big, [v_w_in[0], v_w_out[0], tr(v_w_gate), tr(v_w_up), v_w_down[0]]))
    grads, deltas, new_m, new_v = {}, {}, {}, {}

    def update(names, gs, tag):
        res = _adamw_big([w_big[k] for k in names], gs, [m_big[k] for k in names], [v_big[k] for k in names], tag)
        for nm, outs in zip(names, res):
            if nm in ("w_gate", "w_up"):
                outs = [jnp.swapaxes(o[None], 1, 2) for o in outs]
            else:
                outs = [o[None] for o in outs]
            grads[nm], deltas[nm], new_m[nm], new_v[nm] = outs
        return res[-1][1]

    last = update(["w_gate", "w_up", "w_down"], list(gsum_ffn), "ffn")

    parts2, m_recv = _chip_exchange_wait(m_ssem, m_rsem, parts2, m_lands, last, "mix")
    halves2 = _chip_sum(parts2[0:1], m_recv[0:1], qc_arr, "in") + _chip_sum(parts2[1:2], m_recv[1:2], qc_arr, "out")
    gsum_mix = _pair_share_grads(halves2, "mix")
    update(["w_in"], [gsum_mix[0]], "w_in")
    update(["w_out"], [gsum_mix[1]], "w_out")

    names_small = ["meta_tokens", "pre_mix_norm", "conv_a_w", "conv_b_w", "conv_b_bias", "ln_b_gain", "ln_b_bias",
                   "post_mix_norm", "pre_ffn_norm", "post_ffn_norm"]
    w_small = [meta_tokens, pre_mix_norm, conv_a_w[0], conv_b_w[0], conv_b_bias, ln_b_gain, ln_b_bias, post_mix_norm,
               pre_ffn_norm, post_ffn_norm]
    g_small = [g_meta, s_g1, g_wa[0], g_wb[0], s_bb, s_lg, s_lb, s_gpm, s_g2, s_gpf]
    m_small = [m_meta_tokens, m_pre_mix_norm, m_conv_a_w[0], m_conv_b_w[0], m_conv_b_bias, m_ln_b_gain, m_ln_b_bias,
               m_post_mix_norm, m_pre_ffn_norm, m_post_ffn_norm]
    v_small = [v_meta_tokens, v_pre_mix_norm, v_conv_a_w[0], v_conv_b_w[0], v_conv_b_bias, v_ln_b_gain, v_ln_b_bias,
               v_post_mix_norm, v_pre_ffn_norm, v_post_ffn_norm]
    d_s, m_s, v_s = _adamw_small(w_small, g_small, m_small, v_small)
    for i, nm in enumerate(names_small):
        lead = nm in ("conv_a_w", "conv_b_w")
        fix = (lambda a: a[None]) if lead else (lambda a: a)
        grads[nm], deltas[nm], new_m[nm], new_v[nm] = fix(g_small[i]), fix(d_s[i]), fix(m_s[i]), fix(v_s[i])

    order = ["meta_tokens", "pre_mix_norm", "w_in", "conv_a_w", "conv_b_w", "conv_b_bias", "ln_b_gain", "ln_b_bias", "w_out",
             "post_mix_norm", "pre_ffn_norm", "w_gate", "w_up", "w_down", "post_ffn_norm"]
    return (loss, grad_x, *[grads[k] for k in order], *[deltas[k] for k in order], *[new_m[k] for k in order],
            *[new_v[k] for k in order])
```

```python
import functools

import jax
import jax.numpy as jnp
from jax import lax
from jax.experimental import pallas as pl
from jax.experimental.pallas import tpu as pltpu

F32 = jnp.float32
BF16 = jnp.bfloat16
MESH = pl.DeviceIdType.MESH

N_META = 16
TAIL_ROWS = 128
RMS_EPS = 1e-6
LN_EPS = 1e-5
ADAM_LR = 0.001
ADAM_B1 = 0.9
ADAM_B2 = 0.999
ADAM_EPS = 1e-08
ADAM_WD = 0.01
ADAM_STEP = 10

N_CHIPS = 4
LANES = 128
MXU_TILE = 256
CONV_CHUNK = 48
CONV_HIST = 32
ROW_TILE_CAP = 640
VMEM_LIMIT = 56 * 1024 * 1024

NN = (((1,), (0,)), ((), ()))
NT = (((1,), (1,)), ((), ()))
TN = (((0,), (0,)), ((), ()))


def _dot(a, b, dims=NN):
    return lax.dot_general(a, b, dims, preferred_element_type=F32)


def _sig(v):
    return 1.0 / (1.0 + jnp.exp(-v))


def _mean(v):
    return jnp.mean(v, axis=-1, keepdims=True)


def _row_tile(rows):
    best = 16
    for t in range(16, min(rows, ROW_TILE_CAP) + 1, 16):
        if rows % t == 0:
            best = t
    assert rows % best == 0
    return best


def _row_parts(tm):
    if tm % 32:
        return [slice(0, tm)]
    return [slice(0, tm // 2), slice(tm // 2, tm)]


def _pieces(n_shards, shard_w, piece_w):
    total = n_shards * shard_w
    cuts = sorted(set(range(0, total + 1, shard_w)) | set(range(0, total + 1, piece_w)))
    out = []
    for lo, hi in zip(cuts[:-1], cuts[1:]):
        out.append((lo // shard_w, lo % shard_w, lo // piece_w, lo % piece_w, hi - lo))
    return out


def _params(semantics=None):
    kw = dict(vmem_limit_bytes=VMEM_LIMIT)
    if semantics is not None:
        kw["dimension_semantics"] = semantics
    return pltpu.CompilerParams(**kw)


def _full(shape):
    nd = len(shape)
    return pl.BlockSpec(shape, lambda *_: (0,) * nd)


def _resident(shape):
    nd = len(shape)
    return pl.BlockSpec(shape, lambda *_: (0,) * nd, pipeline_mode=pl.Buffered(1))


def _sds(shape, dtype):
    return jax.ShapeDtypeStruct(shape, dtype)


ANY = pl.BlockSpec(memory_space=pl.ANY)
VMEM = pl.BlockSpec(memory_space=pltpu.VMEM)


def _mesh_pos():
    return lax.axis_index("x"), lax.axis_index("y"), lax.axis_index("c")


def _flip(v, bit):
    return 1 - v if bit else v


def _mm_in(x, tail, win4, g1, after):
    seq, d = x.shape
    tp = seq + TAIL_ROWS
    tm = _row_tile(tp)
    n_sh, _, csh = win4.shape
    pw = n_sh * csh // 5
    pieces = _pieces(n_sh, csh, pw)

    def body(x_ref, tail_ref, w_ref, g_ref, after_ref, h_ref, xn_ref, hp_ref):
        rows = pl.program_id(0) * tm + lax.broadcasted_iota(jnp.int32, (tm, 1), 0)
        hh = jnp.where(rows < seq, x_ref[...], tail_ref[...])
        h_ref[...] = hh
        r = lax.rsqrt(_mean(hh * hh) + RMS_EPS)
        xn = (hh * r * g_ref[...]).astype(BF16)
        xn_ref[...] = xn
        for k, klo, p, plo, w in pieces:
            hp_ref[p, :, plo:plo + w] = _dot(xn, w_ref[k, :, klo:klo + w])

    row = pl.BlockSpec((tm, d), lambda i: (i, 0))
    return pl.pallas_call(
        body, name="mm_in", grid=(tp // tm,),
        in_specs=[row, _full(tail.shape), _resident(win4.shape), _full(g1.shape), ANY],
        out_specs=[row, row, pl.BlockSpec((5, tm, pw), lambda i: (0, i, 0))],
        out_shape=[_sds((tp, d), F32), _sds((tp, d), BF16), _sds((5, tp, pw), F32)],
        compiler_params=_params(("arbitrary",)),
    )(x, tail, win4, g1, after)


def _seq_rows(tp):
    seq = tp - TAIL_ROWS
    nseq = seq + N_META
    assert nseq % CONV_CHUNK == 0 and seq % 16 == 0
    return seq, nseq


SUBLANES = 8


def _conv_offsets(width, transpose):
    return [(width - 1 - k) if transpose else (CONV_HIST - (width - 1) + k) for k in range(width)]


def _shift_copies(src_ref, sh_ref, width, transpose):
    n = src_ref.shape[0] - SUBLANES
    for s in sorted({o % SUBLANES for o in _conv_offsets(width, transpose)} - {0}):
        sh_ref[s - 1, 0:n, :] = src_ref[s:s + n, :]


def _tap_rows(src_ref, sh_ref, base, off):
    start = pl.multiple_of(base + (off // SUBLANES) * SUBLANES, SUBLANES)
    if off % SUBLANES == 0:
        return src_ref[pl.ds(start, CONV_CHUNK), :]
    return sh_ref[off % SUBLANES - 1, pl.ds(start, CONV_CHUNK), :]


def _conv_taps(src_ref, sh_ref, w_ref, dst_ref, width, nseq, transpose):
    w = w_ref[...]
    offs = _conv_offsets(width, transpose)
    _shift_copies(src_ref, sh_ref, width, transpose)

    def step(n, carry):
        out0 = pl.multiple_of(CONV_HIST + n * CONV_CHUNK, SUBLANES)
        base = out0 if transpose else n * CONV_CHUNK
        acc = jnp.zeros((CONV_CHUNK, w.shape[1]), F32)
        for k, off in enumerate(offs):
            acc = acc + w[k:k + 1, :] * _tap_rows(src_ref, sh_ref, base, off)
        dst_ref[pl.ds(out0, CONV_CHUNK), :] = acc
        return carry

    lax.fori_loop(0, nseq // CONV_CHUNK, step, 0)


def _conv_wgrad(src_ref, sh_ref, dz_ref, acc_ref, width, nseq):
    acc_ref[...] = jnp.zeros(acc_ref.shape, F32)
    offs = _conv_offsets(width, False)

    def step(n, carry):
        dzc = dz_ref[pl.ds(pl.multiple_of(CONV_HIST + n * CONV_CHUNK, SUBLANES), CONV_CHUNK), :]
        for k, off in enumerate(offs):
            prod = dzc * _tap_rows(src_ref, sh_ref, n * CONV_CHUNK, off)
            part = prod[0:SUBLANES, :]
            for s in range(1, CONV_CHUNK // SUBLANES):
                part = part + prod[SUBLANES * s:SUBLANES * (s + 1), :]
            acc_ref[SUBLANES * k:SUBLANES * (k + 1), :] += part
        return carry

    lax.fori_loop(0, nseq // CONV_CHUNK, step, 0)


def _to_seq(buf_ref, x_part, meta_part, seq):
    buf_ref[CONV_HIST:CONV_HIST + N_META, :] = meta_part
    buf_ref[CONV_HIST + N_META:CONV_HIST + N_META + seq, :] = x_part


def _zero_ends(buf_ref, nseq):
    zeros = jnp.zeros((CONV_HIST, buf_ref.shape[1]), F32)
    buf_ref[0:CONV_HIST, :] = zeros
    buf_ref[CONV_HIST + nseq:CONV_HIST + nseq + CONV_HIST, :] = zeros


def _mix_conv_fwd(hp5, wa, wb, bb):
    _, tp, wgrp = hp5.shape
    seq, nseq = _seq_rows(tp)
    sb = nseq + 2 * CONV_HIST
    ka, kb = wa.shape[0], wb.shape[0]
    xs, ms = slice(0, seq), slice(seq, seq + N_META)
    ox, om = slice(CONV_HIST + N_META, CONV_HIST + nseq), slice(CONV_HIST, CONV_HIST + N_META)

    def body(hp_ref, wa_ref, wb_ref, bb_ref, ya_ref, z_ref, s_ref, o_ref, sh_ref):
        _zero_ends(s_ref, nseq)
        _to_seq(s_ref, hp_ref[1, xs, :] * hp_ref[2, xs, :], hp_ref[1, ms, :] * hp_ref[2, ms, :], seq)
        _conv_taps(s_ref, sh_ref, wa_ref, o_ref, ka, nseq, False)
        ya_ref[xs, :] = (hp_ref[0, xs, :] * o_ref[ox, :]).astype(BF16)
        ya_ref[ms, :] = (hp_ref[0, ms, :] * o_ref[om, :]).astype(BF16)
        ya_ref[seq + N_META:tp, :] = jnp.zeros((tp - seq - N_META, LANES), BF16)
        _to_seq(s_ref, hp_ref[3, xs, :] * _sig(hp_ref[4, xs, :]), hp_ref[3, ms, :] * _sig(hp_ref[4, ms, :]), seq)
        _conv_taps(s_ref, sh_ref, wb_ref, o_ref, kb, nseq, False)
        z_ref[xs, :] = o_ref[ox, :] + bb_ref[...]
        z_ref[ms, :] = o_ref[om, :] + bb_ref[...]
        z_ref[seq + N_META:tp, :] = jnp.zeros((tp - seq - N_META, LANES), F32)

    col = lambda j: (0, j)
    return pl.pallas_call(
        body, name="mix_conv_fwd", grid=(wgrp // LANES,),
        in_specs=[pl.BlockSpec((5, tp, LANES), lambda j: (0, 0, j)), pl.BlockSpec((ka, LANES), col),
                  pl.BlockSpec((kb, LANES), col), pl.BlockSpec((1, LANES), col)],
        out_specs=[pl.BlockSpec((tp, LANES), col), pl.BlockSpec((tp, LANES), col)],
        out_shape=[_sds((tp, wgrp), BF16), _sds((tp, wgrp), F32)],
        scratch_shapes=[pltpu.VMEM((sb, LANES), F32), pltpu.VMEM((sb, LANES), F32),
                        pltpu.VMEM((SUBLANES - 1, sb, LANES), F32)],
        compiler_params=_params(("arbitrary",)),
    )(hp5, wa, wb, bb)


def _layer_norm_parts(z, lg, lb):
    mu = _mean(z)
    zc = z - mu
    rl = lax.rsqrt(_mean(zc * zc) + LN_EPS)
    zh = zc * rl
    return rl, zh, zh * lg + lb


def _mm_out(ya, z, h, wout, lg, lb, gpm, g2):
    tp, d = h.shape
    wa_ = ya.shape[1]
    tm = _row_tile(tp)

    def body(ya_ref, z_ref, h_ref, w_ref, lg_ref, lb_ref, gpm_ref, g2_ref, yb_ref, mix_ref, h1_ref, xn2_ref):
        for rs in _row_parts(tm):
            _, _, l = _layer_norm_parts(z_ref[rs, :], lg_ref[...], lb_ref[...])
            yb = (l * _sig(l)).astype(BF16)
            yb_ref[rs, :] = yb
            mix = _dot(ya_ref[rs, :], w_ref[0:wa_, :]) + _dot(yb, w_ref[wa_:d, :])
            mix_ref[rs, :] = mix
            rm = lax.rsqrt(_mean(mix * mix) + RMS_EPS)
            h1 = h_ref[rs, :] + mix * rm * gpm_ref[...]
            h1_ref[rs, :] = h1
            r2 = lax.rsqrt(_mean(h1 * h1) + RMS_EPS)
            xn2_ref[rs, :] = (h1 * r2 * g2_ref[...]).astype(BF16)

    row = lambda i: (i, 0)
    return pl.pallas_call(
        body, name="mm_out", grid=(tp // tm,),
        in_specs=[pl.BlockSpec((tm, wa_), row), pl.BlockSpec((tm, wa_), row), pl.BlockSpec((tm, d), row),
                  _resident(wout.shape), _full(lg.shape), _full(lb.shape), _full(gpm.shape), _full(g2.shape)],
        out_specs=[pl.BlockSpec((tm, wa_), row), pl.BlockSpec((tm, d), row), pl.BlockSpec((tm, d), row),
                   pl.BlockSpec((tm, d), row)],
        out_shape=[_sds((tp, wa_), BF16), _sds((tp, d), F32), _sds((tp, d), F32), _sds((tp, d), BF16)],
        compiler_params=_params(("arbitrary",)),
    )(ya, z, h, wout, lg, lb, gpm, g2)


def _ffn_up(xn2, wg, wu):
    tp, d = xn2.shape
    ff_dim = wg.shape[0]
    tm = _row_tile(tp)
    assert ff_dim % MXU_TILE == 0

    def body(xn_ref, wg_ref, wu_ref, p_ref, q_ref, f_ref):
        xn = xn_ref[...]
        for lo in range(0, ff_dim, MXU_TILE):
            cols = slice(lo, lo + MXU_TILE)
            a = _dot(xn, wg_ref[cols, :], NT)
            u = _dot(xn, wu_ref[cols, :], NT)
            s = _sig(a)
            q = a * s
            p_ref[:, cols] = (u * (s + q * (1.0 - s))).astype(BF16)
            q_ref[:, cols] = q.astype(BF16)
            f_ref[:, cols] = (q * u).astype(BF16)

    ospec = pl.BlockSpec((tm, ff_dim), lambda i: (i, 0))
    return pl.pallas_call(
        body, name="ffn_up", grid=(tp // tm,),
        in_specs=[pl.BlockSpec((tm, d), lambda i: (i, 0)), _resident(wg.shape), _resident(wu.shape)],
        out_specs=[ospec, ospec, ospec],
        out_shape=[_sds((tp, ff_dim), BF16)] * 3,
        compiler_params=_params(("arbitrary",)),
    )(xn2, wg, wu)


def _ffn_down(f, wd, h1, tgt, gpf):
    tp, ff_dim = f.shape
    d = h1.shape[1]
    tm = _row_tile(tp)
    seq, _ = _seq_rows(tp)

    def body(f_ref, w_ref, h1_ref, t_ref, gpf_ref, dff_ref, dh2_ref, loss_ref, dgpf_ref):
        i = pl.program_id(0)
        gpf_ = gpf_ref[...]

        @pl.when(i == 0)
        def _():
            loss_ref[...] = jnp.zeros(loss_ref.shape, F32)
            dgpf_ref[...] = jnp.zeros(dgpf_ref.shape, F32)

        for rs in _row_parts(tm):
            ff = _dot(f_ref[rs, :], w_ref[...])
            rf = lax.rsqrt(_mean(ff * ff) + RMS_EPS)
            nf = ff * rf
            h2 = h1_ref[rs, :] + nf * gpf_
            rows = i * tm + rs.start + lax.broadcasted_iota(jnp.int32, (rs.stop - rs.start, 1), 0)
            err = jnp.where(rows < seq, h2 - t_ref[rs, :], 0.0)
            dh2 = err * (1.0 / d)
            dh2_ref[rs, :] = dh2
            dn = dh2 * gpf_
            dff_ref[rs, :] = (rf * (dn - nf * _mean(dn * nf))).astype(BF16)
            loss_ref[...] += (0.5 / d) * jnp.sum(err * err, axis=(0, 1), keepdims=True)
            dgpf_ref[...] += jnp.sum(dh2 * nf, axis=0, keepdims=True)

    row = lambda i: (i, 0)
    return pl.pallas_call(
        body, name="ffn_down", grid=(tp // tm,),
        in_specs=[pl.BlockSpec((tm, ff_dim), row), _resident(wd.shape), pl.BlockSpec((tm, d), row),
                  pl.BlockSpec((tm, d), row), _full(gpf.shape)],
        out_specs=[pl.BlockSpec((tm, d), row), pl.BlockSpec((tm, d), row), _full((8, LANES)), _full((1, d))],
        out_shape=[_sds((tp, d), BF16), _sds((tp, d), F32), _sds((8, LANES), F32), _sds((1, d), F32)],
        compiler_params=_params(("arbitrary",)),
    )(f, wd, h1, tgt, gpf)


def _ffn_bwd_act(dff, wd, p, q):
    tp, d = dff.shape
    ff_dim = wd.shape[0]
    tm = _row_tile(tp)

    def body(dff_ref, w_ref, p_ref, q_ref, da_ref, du_ref):
        dffv = dff_ref[...]
        for lo in range(0, ff_dim, MXU_TILE):
            cols = slice(lo, lo + MXU_TILE)
            df = _dot(dffv, w_ref[cols, :], NT).astype(BF16)
            da_ref[:, cols] = df * p_ref[:, cols]
            du_ref[:, cols] = df * q_ref[:, cols]

    aspec = pl.BlockSpec((tm, ff_dim), lambda i: (i, 0))
    return pl.pallas_call(
        body, name="ffn_bwd_act", grid=(tp // tm,),
        in_specs=[pl.BlockSpec((tm, d), lambda i: (i, 0)), _resident(wd.shape), aspec, aspec],
        out_specs=[aspec, aspec],
        out_shape=[_sds((tp, ff_dim), BF16)] * 2,
        compiler_params=_params(("arbitrary",)),
    )(dff, wd, p, q)


def _grad_blocks(ff_dim):
    rows = ff_dim // 2
    assert rows % LANES == 0
    return rows


def _grad_w_down(f, dff):
    tp, ff_dim = f.shape
    d = dff.shape[1]
    rows = _grad_blocks(ff_dim)

    def body(f_ref, dff_ref, g_ref):
        g_ref[...] = _dot(f_ref[...], dff_ref[...], TN).astype(BF16)

    return pl.pallas_call(
        body, name="grad_w_down", grid=(ff_dim // rows,),
        in_specs=[pl.BlockSpec((tp, rows), lambda k: (0, k)), _resident(dff.shape)],
        out_specs=pl.BlockSpec((rows, d), lambda k: (k, 0)),
        out_shape=_sds((ff_dim, d), BF16),
        compiler_params=_params(("arbitrary",)),
    )(f, dff)


def _grad_w_gate_up(xn2, da, du):
    tp, ff_dim = da.shape
    d = xn2.shape[1]
    rows = _grad_blocks(ff_dim)

    def body(xn_ref, da_ref, du_ref, gg_ref, gu_ref):
        xn = xn_ref[...]
        gg_ref[...] = _dot(da_ref[...], xn, TN).astype(BF16)
        gu_ref[...] = _dot(du_ref[...], xn, TN).astype(BF16)

    aspec = pl.BlockSpec((tp, rows), lambda k: (0, k))
    gspec = pl.BlockSpec((rows, d), lambda k: (k, 0))
    return pl.pallas_call(
        body, name="grad_w_gate_up", grid=(ff_dim // rows,),
        in_specs=[_resident(xn2.shape), aspec, aspec],
        out_specs=[gspec, gspec],
        out_shape=[_sds((ff_dim, d), BF16)] * 2,
        compiler_params=_params(("arbitrary",)),
    )(xn2, da, du)


def _rms_bwd(dy, x, r, g):
    n = x * r
    dn = dy * g
    return r * (dn - n * _mean(dn * n)), dy * n


def _ffn_bwd_in(da, du, wg, wu, h1, mix, dh2, g2, gpm, after):
    tp, ff_dim = da.shape
    d = h1.shape[1]
    tm = _row_tile(tp)

    def body(da_ref, du_ref, wg_ref, wu_ref, h1_ref, mix_ref, dh2_ref, g2_ref, gpm_ref, after_ref,
             dh1_ref, dmix_ref, dg2_ref, dgpm_ref):
        i = pl.program_id(0)

        @pl.when(i == 0)
        def _():
            dg2_ref[...] = jnp.zeros(dg2_ref.shape, F32)
            dgpm_ref[...] = jnp.zeros(dgpm_ref.shape, F32)

        for rs in _row_parts(tm):
            dxn = _dot(da_ref[rs, :], wg_ref[...]) + _dot(du_ref[rs, :], wu_ref[...])
            h1v = h1_ref[rs, :]
            r2 = lax.rsqrt(_mean(h1v * h1v) + RMS_EPS)
            dres, dg2_rows = _rms_bwd(dxn, h1v, r2, g2_ref[...])
            dh1 = dh2_ref[rs, :] + dres
            dh1_ref[rs, :] = dh1
            mixv = mix_ref[rs, :]
            rm = lax.rsqrt(_mean(mixv * mixv) + RMS_EPS)
            dmix, dgpm_rows = _rms_bwd(dh1, mixv, rm, gpm_ref[...])
            dmix_ref[rs, :] = dmix.astype(BF16)
            dg2_ref[...] += jnp.sum(dg2_rows, axis=0, keepdims=True)
            dgpm_ref[...] += jnp.sum(dgpm_rows, axis=0, keepdims=True)

    aspec = pl.BlockSpec((tm, ff_dim), lambda i: (i, 0))
    row = pl.BlockSpec((tm, d), lambda i: (i, 0))
    return pl.pallas_call(
        body, name="ffn_bwd_in", grid=(tp // tm,),
        in_specs=[aspec, aspec, _resident(wg.shape), _resident(wu.shape), row, row, row, _full(g2.shape), _full(gpm.shape),
                  ANY],
        out_specs=[row, row, _full((1, d)), _full((1, d))],
        out_shape=[_sds((tp, d), F32), _sds((tp, d), BF16), _sds((1, d), F32), _sds((1, d), F32)],
        compiler_params=_params(("arbitrary",)),
    )(da, du, wg, wu, h1, mix, dh2, g2, gpm, after)


def _grad_w_out(ya, yb, dmix, after):
    tp, wa_ = ya.shape
    d = dmix.shape[1]

    def body(ya_ref, yb_ref, dmix_ref, after_ref, g_ref):
        dm = dmix_ref[...]
        g_ref[0:wa_, :] = _dot(ya_ref[...], dm, TN).astype(BF16)
        g_ref[wa_:2 * wa_, :] = _dot(yb_ref[...], dm, TN).astype(BF16)

    return pl.pallas_call(
        body, name="grad_w_out", grid=(1,),
        in_specs=[_full(ya.shape), _full(yb.shape), _full(dmix.shape), ANY],
        out_specs=_full((2 * wa_, d)),
        out_shape=_sds((2 * wa_, d), BF16),
        compiler_params=_params(("arbitrary",)),
    )(ya, yb, dmix, after)


def _mix_bwd_out(dmix, wout, z, lg, lb, after):
    tp, d = dmix.shape
    wa_ = z.shape[1]
    tm = _row_tile(tp)

    def body(dmix_ref, w_ref, z_ref, lg_ref, lb_ref, after_ref, dya_ref, dz_ref, dlg_ref, dlb_ref):
        i = pl.program_id(0)
        lg_ = lg_ref[...]

        @pl.when(i == 0)
        def _():
            dlg_ref[...] = jnp.zeros(dlg_ref.shape, F32)
            dlb_ref[...] = jnp.zeros(dlb_ref.shape, F32)

        for rs in _row_parts(tm):
            dm = dmix_ref[rs, :]
            dya_ref[rs, :] = _dot(dm, w_ref[0:wa_, :], NT)
            dyb = _dot(dm, w_ref[wa_:d, :], NT)
            rl, zh, l = _layer_norm_parts(z_ref[rs, :], lg_, lb_ref[...])
            sl = _sig(l)
            dl = dyb * (sl * (1.0 + l * (1.0 - sl)))
            dzh = dl * lg_
            dz_ref[rs, :] = rl * (dzh - _mean(dzh) - zh * _mean(dzh * zh))
            dlg_ref[...] += jnp.sum(dl * zh, axis=0, keepdims=True)
            dlb_ref[...] += jnp.sum(dl, axis=0, keepdims=True)

    row = lambda i: (i, 0)
    return pl.pallas_call(
        body, name="mix_bwd_out", grid=(tp // tm,),
        in_specs=[pl.BlockSpec((tm, d), row), _resident(wout.shape), pl.BlockSpec((tm, wa_), row), _full(lg.shape),
                  _full(lb.shape), ANY],
        out_specs=[pl.BlockSpec((tm, wa_), row), pl.BlockSpec((tm, wa_), row), _full((1, wa_)), _full((1, wa_))],
        out_shape=[_sds((tp, wa_), F32), _sds((tp, wa_), F32), _sds((1, wa_), F32), _sds((1, wa_), F32)],
        compiler_params=_params(("arbitrary",)),
    )(dmix, wout, z, lg, lb, after)


def _mix_conv_bwd(hp5, dya, dz, wa, wb):
    _, tp, wgrp = hp5.shape
    seq, nseq = _seq_rows(tp)
    sb = nseq + 2 * CONV_HIST
    ka, kb = wa.shape[0], wb.shape[0]
    xs, ms = slice(0, seq), slice(seq, seq + N_META)
    ox, om = slice(CONV_HIST + N_META, CONV_HIST + nseq), slice(CONV_HIST, CONV_HIST + N_META)
    n_tail = tp - seq - N_META

    def body(hp_ref, dya_ref, dz_ref, wa_ref, wb_ref, dhp_ref, dwa_ref, dwb_ref, dbb_ref, s_ref, d_ref, o_ref, acc_ref,
             shs_ref, shd_ref):
        _zero_ends(s_ref, nseq)
        _zero_ends(d_ref, nseq)

        def put(p, ox_val, om_val):
            dhp_ref[p, xs, :] = ox_val.astype(BF16)
            dhp_ref[p, ms, :] = om_val.astype(BF16)
            dhp_ref[p, seq + N_META:tp, :] = jnp.zeros((n_tail, LANES), BF16)

        def wgrad(dw_ref, width):
            for k in range(width):
                dw_ref[k:k + 1, :] = jnp.sum(acc_ref[8 * k:8 * k + 8, :], axis=0, keepdims=True)

        _to_seq(s_ref, hp_ref[1, xs, :] * hp_ref[2, xs, :], hp_ref[1, ms, :] * hp_ref[2, ms, :], seq)
        _conv_taps(s_ref, shs_ref, wa_ref, o_ref, ka, nseq, False)
        put(0, dya_ref[xs, :] * o_ref[ox, :], dya_ref[ms, :] * o_ref[om, :])
        _to_seq(d_ref, dya_ref[xs, :] * hp_ref[0, xs, :], dya_ref[ms, :] * hp_ref[0, ms, :], seq)
        _conv_wgrad(s_ref, shs_ref, d_ref, acc_ref, ka, nseq)
        wgrad(dwa_ref, ka)
        _conv_taps(d_ref, shd_ref, wa_ref, o_ref, ka, nseq, True)
        put(1, o_ref[ox, :] * hp_ref[2, xs, :], o_ref[om, :] * hp_ref[2, ms, :])
        put(2, o_ref[ox, :] * hp_ref[1, xs, :], o_ref[om, :] * hp_ref[1, ms, :])

        _to_seq(s_ref, hp_ref[3, xs, :] * _sig(hp_ref[4, xs, :]), hp_ref[3, ms, :] * _sig(hp_ref[4, ms, :]), seq)
        _to_seq(d_ref, dz_ref[xs, :], dz_ref[ms, :], seq)
        dbb_ref[...] = (jnp.sum(dz_ref[xs, :], axis=0, keepdims=True)
                        + jnp.sum(dz_ref[ms, :], axis=0, keepdims=True))
        _shift_copies(s_ref, shs_ref, kb, False)
        _conv_wgrad(s_ref, shs_ref, d_ref, acc_ref, kb, nseq)
        wgrad(dwb_ref, kb)
        _conv_taps(d_ref, shd_ref, wb_ref, o_ref, kb, nseq, True)
        sx, sm = _sig(hp_ref[4, xs, :]), _sig(hp_ref[4, ms, :])
        put(3, o_ref[ox, :] * sx, o_ref[om, :] * sm)
        put(4, o_ref[ox, :] * hp_ref[3, xs, :] * sx * (1.0 - sx), o_ref[om, :] * hp_ref[3, ms, :] * sm * (1.0 - sm))

    col = lambda j: (0, j)
    blk5 = pl.BlockSpec((5, tp, LANES), lambda j: (0, 0, j))
    return pl.pallas_call(
        body, name="mix_conv_bwd", grid=(wgrp // LANES,),
        in_specs=[blk5, pl.BlockSpec((tp, LANES), col), pl.BlockSpec((tp, LANES), col),
                  pl.BlockSpec((ka, LANES), col), pl.BlockSpec((kb, LANES), col)],
        out_specs=[blk5, pl.BlockSpec((ka, LANES), col), pl.BlockSpec((kb, LANES), col), pl.BlockSpec((1, LANES), col)],
        out_shape=[_sds((5, tp, wgrp), BF16), _sds((ka, wgrp), F32), _sds((kb, wgrp), F32), _sds((1, wgrp), F32)],
        scratch_shapes=[pltpu.VMEM((sb, LANES), F32), pltpu.VMEM((sb, LANES), F32), pltpu.VMEM((sb, LANES), F32),
                        pltpu.VMEM((SUBLANES * kb, LANES), F32), pltpu.VMEM((SUBLANES - 1, sb, LANES), F32),
                        pltpu.VMEM((SUBLANES - 1, sb, LANES), F32)],
        compiler_params=_params(("arbitrary",)),
    )(hp5, dya, dz, wa, wb)


def _grad_w_in(xn1, dhp5):
    n_p, tp, pw = dhp5.shape
    d = xn1.shape[1]

    def body(xn_ref, dhp_ref, g_ref):
        g_ref[...] = _dot(xn_ref[...], dhp_ref[0], TN).astype(BF16)

    return pl.pallas_call(
        body, name="grad_w_in", grid=(n_p,),
        in_specs=[_resident(xn1.shape), pl.BlockSpec((1, tp, pw), lambda p: (p, 0, 0))],
        out_specs=pl.BlockSpec((d, pw), lambda p: (0, p)),
        out_shape=_sds((d, n_p * pw), BF16),
        compiler_params=_params(("arbitrary",)),
    )(xn1, dhp5)


def _mix_bwd_in(dhp5, win4, h, dh1, g1, after):
    n_p, tp, pw = dhp5.shape
    d = h.shape[1]
    n_sh, _, csh = win4.shape
    tm = _row_tile(tp)
    pieces = _pieces(n_sh, csh, pw)

    seq, _ = _seq_rows(tp)
    last, meta_off = seq // tm, seq % tm
    assert last == tp // tm - 1
    assert any(rs.start <= meta_off and meta_off + N_META <= rs.stop for rs in _row_parts(tm))

    def body(dhp_ref, w_ref, h_ref, dh1_ref, g_ref, after_ref, gx_ref, dmeta_ref, dg1_ref):
        i = pl.program_id(0)

        @pl.when(i == 0)
        def _():
            dg1_ref[...] = jnp.zeros(dg1_ref.shape, F32)

        for rs in _row_parts(tm):
            dxn = None
            for k, klo, p, plo, w in pieces:
                t = _dot(dhp_ref[p, rs, plo:plo + w], w_ref[k, :, klo:klo + w], NT)
                dxn = t if dxn is None else dxn + t
            hh = h_ref[rs, :]
            r1 = lax.rsqrt(_mean(hh * hh) + RMS_EPS)
            dres, dg_rows = _rms_bwd(dxn, hh, r1, g_ref[...])
            dh = dh1_ref[rs, :] + dres
            gx_ref[rs, :] = dh
            dg1_ref[...] += jnp.sum(dg_rows, axis=0, keepdims=True)
            if rs.start <= meta_off and meta_off + N_META <= rs.stop:
                @pl.when(i == last)
                def _():
                    dmeta_ref[...] = dh[meta_off - rs.start:meta_off - rs.start + N_META, :]

    row = lambda i: (i, 0)
    return pl.pallas_call(
        body, name="mix_bwd_in", grid=(tp // tm,),
        in_specs=[pl.BlockSpec((n_p, tm, pw), lambda i: (0, i, 0)), _resident(win4.shape), pl.BlockSpec((tm, d), row),
                  pl.BlockSpec((tm, d), row), _full(g1.shape), ANY],
        out_specs=[pl.BlockSpec((tm, d), row), _full((N_META, d)), _full((1, d))],
        out_shape=[_sds((seq, d), F32), _sds((N_META, d), F32), _sds((1, d), F32)],
        compiler_params=_params(("arbitrary",)),
    )(dhp5, win4, h, dh1, g1, after)


def _other_chips(x, y):
    out = []
    for j in (1, 2, 3):
        px, py = _flip(x, j >> 1), _flip(y, j & 1)
        out.append((px, py, 2 * px + py))
    return out


PAIR_COLLECTIVE_ID = 0


def _pair_barrier(x, y, c):
    sem = pltpu.get_barrier_semaphore()
    pl.semaphore_signal(sem, inc=1, device_id=(x, y, 1 - c), device_id_type=MESH)
    pl.semaphore_wait(sem, 1)


def _pair_params():
    return pltpu.CompilerParams(collective_id=PAIR_COLLECTIVE_ID)


def _half_rows(c, rows_half):
    return pl.ds(pl.multiple_of(c * rows_half, 8), rows_half)


def _cast_place(w, q_arr, tag, after=None):
    rows, cols = w.shape
    tr = _row_tile(rows)
    extra = [] if after is None else [after]

    def body(q_ref, w_ref, *rest):
        rest[-1][0] = w_ref[...].astype(BF16)

    return pl.pallas_call(
        body, name="cast_place_" + tag,
        grid_spec=pltpu.PrefetchScalarGridSpec(
            num_scalar_prefetch=1, grid=(rows // tr,),
            in_specs=[pl.BlockSpec((tr, cols), lambda i, q: (i, 0))] + [ANY] * len(extra),
            out_specs=pl.BlockSpec((1, tr, cols), lambda i, q: (q[0], i, 0))),
        out_shape=_sds((N_CHIPS, rows, cols), BF16),
        compiler_params=_params(("arbitrary",)),
    )(q_arr, w, *extra)


def _gather_shards(fulls):
    n = len(fulls)
    halves = [a.shape[1] // 2 for a in fulls]

    def body(*refs):
        full = refs[n:2 * n]
        ssem, rsem = refs[2 * n:]
        x, y, c = _mesh_pos()
        chips = _other_chips(x, y)

        def remote(i, chip_no, half, to, s):
            part = full[i].at[chip_no, _half_rows(half, halves[i]), :]
            return pltpu.make_async_remote_copy(src_ref=part, dst_ref=part, send_sem=ssem.at[s], recv_sem=rsem.at[s],
                                                device_id=to, device_id_type=MESH)

        first = []
        for i in range(n):
            for j, (px, py, _) in enumerate(chips):
                cp = remote(i, 2 * x + y, c, (px, py, c), 3 * i + j)
                cp.start()
                first.append(cp)
        passed = []
        for i in range(n):
            for j, (_, _, qj) in enumerate(chips):
                remote(i, qj, c, (x, y, c), 3 * i + j).wait_recv()
                cp = remote(i, qj, c, (x, y, 1 - c), 3 * n + 3 * i + j)
                cp.start()
                passed.append(cp)
        for i in range(n):
            for j, (_, _, qj) in enumerate(chips):
                remote(i, qj, 1 - c, (x, y, c), 3 * n + 3 * i + j).wait_recv()
        for cp in first + passed:
            cp.wait_send()

    return pl.pallas_call(
        body, name="gather_shards",
        in_specs=[ANY] * n, out_specs=[ANY] * n,
        out_shape=[_sds(a.shape, a.dtype) for a in fulls],
        input_output_aliases={i: i for i in range(n)},
        scratch_shapes=[pltpu.SemaphoreType.DMA((6 * n,)), pltpu.SemaphoreType.DMA((6 * n,))],
    )(*fulls)


HBM = pl.BlockSpec(memory_space=pltpu.HBM)
SEM = pl.BlockSpec(memory_space=pltpu.SEMAPHORE)
EFFECT = pltpu.SideEffectType.DATAFLOW_SIDE_EFFECTING


def _in_hbm(a):
    return pltpu.with_memory_space_constraint(a, pltpu.HBM)


def _gather_start(fulls, after, tag):
    n = len(fulls)
    halves = [a.shape[1] // 2 for a in fulls]

    def body(*refs):
        land = refs[:n]
        ssem, rsem = refs[n + 1], refs[n + 2]
        token = refs[-1]
        x, y, c = _mesh_pos()
        q = 2 * x + y
        for i in range(n):
            for j, (px, py, _) in enumerate(_other_chips(x, y)):
                mine = land[i].at[q, _half_rows(c, halves[i]), :]
                pltpu.make_async_remote_copy(src_ref=mine, dst_ref=mine, send_sem=ssem.at[3 * i + j],
                                             recv_sem=rsem.at[3 * i + j], device_id=(px, py, c), device_id_type=MESH).start()
        token[...] = jnp.zeros(token.shape, F32)

    outs = pl.pallas_call(
        body, name="gather_start_" + tag,
        in_specs=[HBM] * n + [ANY], out_specs=[SEM, SEM] + [HBM] * n + [VMEM],
        out_shape=[pltpu.SemaphoreType.DMA((3 * n,)), pltpu.SemaphoreType.DMA((3 * n,))]
        + [pltpu.HBM(a.shape, a.dtype) for a in fulls] + [_sds((8, LANES), F32)],
        input_output_aliases={i: 2 + i for i in range(n)},
        compiler_params=pltpu.CompilerParams(has_side_effects=EFFECT),
    )(*[_in_hbm(a) for a in fulls], after)
    return outs[0], outs[1], list(outs[2:2 + n]), outs[-1]


def _gather_wait(which, ssem, rsem, lands, after, tag):
    m = len(which)
    halves = [a.shape[1] // 2 for a in lands]

    def body(*refs):
        land = refs[:m]
        ssem_, rsem_ = refs[m], refs[m + 1]
        x, y, c = _mesh_pos()
        for t, i in enumerate(which):
            for j, (px, py, qj) in enumerate(_other_chips(x, y)):
                rows = _half_rows(c, halves[t])
                cp = pltpu.make_async_remote_copy(src_ref=land[t].at[2 * x + y, rows, :], dst_ref=land[t].at[qj, rows, :],
                                                  send_sem=ssem_.at[3 * i + j], recv_sem=rsem_.at[3 * i + j],
                                                  device_id=(px, py, c), device_id_type=MESH)
                cp.wait_send()
                cp.wait_recv()

    outs = pl.pallas_call(
        body, name="gather_wait_" + tag,
        in_specs=[HBM] * m + [SEM, SEM, ANY], out_specs=[HBM] * m,
        out_shape=[pltpu.HBM(a.shape, a.dtype) for a in lands],
        input_output_aliases={i: i for i in range(m)},
        compiler_params=pltpu.CompilerParams(has_side_effects=EFFECT),
    )(*lands, ssem, rsem, after)
    return list(outs)


def _forward_pair(lands, tag):
    n = len(lands)
    halves = [a.shape[1] // 2 for a in lands]

    def body(*refs):
        full = refs[n:2 * n]
        ssem, rsem = refs[2 * n:]
        x, y, c = _mesh_pos()
        _pair_barrier(x, y, c)
        cps = []
        for i in range(n):
            for j, (_, _, qj) in enumerate(_other_chips(x, y)):
                part = full[i].at[qj, _half_rows(c, halves[i]), :]
                cp = pltpu.make_async_remote_copy(src_ref=part, dst_ref=part, send_sem=ssem.at[3 * i + j],
                                                  recv_sem=rsem.at[3 * i + j], device_id=(x, y, 1 - c), device_id_type=MESH)
                cp.start()
                cps.append(cp)
        for cp in cps:
            cp.wait()

    return pl.pallas_call(
        body, name="forward_pair_" + tag,
        in_specs=[ANY] * n, out_specs=[ANY] * n,
        out_shape=[_sds(a.shape, a.dtype) for a in lands],
        input_output_aliases={i: i for i in range(n)},
        scratch_shapes=[pltpu.SemaphoreType.DMA((3 * n,)), pltpu.SemaphoreType.DMA((3 * n,))],
        compiler_params=_pair_params(),
    )(*lands)


def _chip_exchange_start(parts, after, tag):
    n = len(parts)

    def body(*refs):
        src, land = refs[:n], refs[n:2 * n]
        ssem, rsem = refs[2 * n + 1], refs[2 * n + 2]
        token = refs[-1]
        x, y, c = _mesh_pos()
        for i in range(n):
            for j, (px, py, qj) in enumerate(_other_chips(x, y)):
                pltpu.make_async_remote_copy(src_ref=src[i].at[qj], dst_ref=land[i].at[j], send_sem=ssem.at[3 * i + j],
                                             recv_sem=rsem.at[3 * i + j], device_id=(px, py, c), device_id_type=MESH).start()
        token[...] = jnp.zeros(token.shape, F32)

    lands = [lax.empty((3,) + a.shape[1:], a.dtype) for a in parts]
    outs = pl.pallas_call(
        body, name="chip_exchange_start_" + tag,
        in_specs=[HBM] * (2 * n) + [ANY], out_specs=[SEM, SEM] + [HBM] * (2 * n) + [VMEM],
        out_shape=[pltpu.SemaphoreType.DMA((3 * n,)), pltpu.SemaphoreType.DMA((3 * n,))]
        + [pltpu.HBM(a.shape, a.dtype) for a in parts] + [pltpu.HBM(a.shape, a.dtype) for a in lands]
        + [_sds((8, LANES), F32)],
        input_output_aliases={i: 2 + i for i in range(2 * n)},
        compiler_params=pltpu.CompilerParams(has_side_effects=EFFECT),
    )(*[_in_hbm(a) for a in parts], *[_in_hbm(a) for a in lands], after)
    return outs[0], outs[1], list(outs[2:2 + n]), list(outs[2 + n:2 + 2 * n]), outs[-1]


def _chip_exchange_wait(ssem, rsem, parts, lands, after, tag):
    n = len(parts)

    def body(*refs):
        src, land = refs[:n], refs[n:2 * n]
        ssem_, rsem_ = refs[2 * n], refs[2 * n + 1]
        x, y, c = _mesh_pos()
        for i in range(n):
            for j, (px, py, qj) in enumerate(_other_chips(x, y)):
                cp = pltpu.make_async_remote_copy(src_ref=src[i].at[qj], dst_ref=land[i].at[j], send_sem=ssem_.at[3 * i + j],
                                                  recv_sem=rsem_.at[3 * i + j], device_id=(px, py, c), device_id_type=MESH)
                cp.wait_send()
                cp.wait_recv()

    outs = pl.pallas_call(
        body, name="chip_exchange_wait_" + tag,
        in_specs=[HBM] * (2 * n) + [SEM, SEM, ANY], out_specs=[HBM] * (2 * n),
        out_shape=[pltpu.HBM(a.shape, a.dtype) for a in parts] + [pltpu.HBM(a.shape, a.dtype) for a in lands],
        input_output_aliases={i: i for i in range(2 * n)},
        compiler_params=pltpu.CompilerParams(has_side_effects=EFFECT),
    )(*parts, *lands, ssem, rsem, after)
    return list(outs[:n]), list(outs[n:])


def _grad_half(ref, shape, axis, which):
    rows = shape[axis] // 2
    if axis == 0:
        return ref.at[_half_rows(which, rows), :]
    return ref.at[:, _half_rows(which, rows), :]


def _half_shape(a, axis):
    s = list(a.shape)
    s[axis] //= 2
    return tuple(s)


def _pair_exchange_start(grads, half_axis, after, tag):
    n = len(grads)

    def body(*refs):
        g, land = refs[:n], refs[n:2 * n]
        ssem, rsem = refs[2 * n + 1], refs[2 * n + 2]
        token = refs[-1]
        x, y, c = _mesh_pos()
        for i in range(n):
            pltpu.make_async_remote_copy(src_ref=_grad_half(g[i], grads[i].shape, half_axis[i], 1 - c), dst_ref=land[i],
                                         send_sem=ssem.at[i], recv_sem=rsem.at[i], device_id=(x, y, 1 - c),
                                         device_id_type=MESH).start()
        token[...] = jnp.zeros(token.shape, F32)

    lands = [lax.empty(_half_shape(a, half_axis[i]), a.dtype) for i, a in enumerate(grads)]
    outs = pl.pallas_call(
        body, name="pair_exchange_start_" + tag,
        in_specs=[HBM] * (2 * n) + [ANY], out_specs=[SEM, SEM] + [HBM] * (2 * n) + [VMEM],
        out_shape=[pltpu.SemaphoreType.DMA((n,)), pltpu.SemaphoreType.DMA((n,))]
        + [pltpu.HBM(a.shape, a.dtype) for a in grads] + [pltpu.HBM(a.shape, a.dtype) for a in lands]
        + [_sds((8, LANES), F32)],
        input_output_aliases={i: 2 + i for i in range(2 * n)},
        compiler_params=pltpu.CompilerParams(has_side_effects=EFFECT),
    )(*[_in_hbm(a) for a in grads], *[_in_hbm(a) for a in lands], after)
    return outs[0], outs[1], list(outs[2:2 + n]), list(outs[2 + n:2 + 2 * n]), outs[-1]


def _pair_exchange_wait(ssem, rsem, grads, lands, half_axis, after, tag):
    n = len(grads)

    def body(*refs):
        g, land = refs[:n], refs[n:2 * n]
        ssem_, rsem_ = refs[2 * n], refs[2 * n + 1]
        x, y, c = _mesh_pos()
        for i in range(n):
            cp = pltpu.make_async_remote_copy(src_ref=_grad_half(g[i], grads[i].shape, half_axis[i], 1 - c),
                                              dst_ref=land[i], send_sem=ssem_.at[i], recv_sem=rsem_.at[i],
                                              device_id=(x, y, 1 - c), device_id_type=MESH)
            cp.wait_send()
            cp.wait_recv()

    outs = pl.pallas_call(
        body, name="pair_exchange_wait_" + tag,
        in_specs=[HBM] * (2 * n) + [SEM, SEM, ANY], out_specs=[HBM] * (2 * n),
        out_shape=[pltpu.HBM(a.shape, a.dtype) for a in grads] + [pltpu.HBM(a.shape, a.dtype) for a in lands],
        input_output_aliases={i: i for i in range(2 * n)},
        compiler_params=pltpu.CompilerParams(has_side_effects=EFFECT),
    )(*grads, *lands, ssem, rsem, after)
    return list(outs[:n]), list(outs[n:])


def _pair_exchange_grads(grads, half_axis, tag):
    n = len(grads)

    def half_of(ref, i, which):
        rows = grads[i].shape[half_axis[i]] // 2
        if half_axis[i] == 0:
            return ref.at[_half_rows(which, rows), :]
        return ref.at[:, _half_rows(which, rows), :]

    def out_shape(i):
        s = list(grads[i].shape)
        s[half_axis[i]] //= 2
        return _sds(tuple(s), grads[i].dtype)

    def body(*refs):
        g, got = refs[:n], refs[n:2 * n]
        ssem, rsem = refs[2 * n:]
        x, y, c = _mesh_pos()
        _pair_barrier(x, y, c)
        cps = []
        for i in range(n):
            cp = pltpu.make_async_remote_copy(src_ref=half_of(g[i], i, 1 - c), dst_ref=got[i], send_sem=ssem.at[i],
                                              recv_sem=rsem.at[i], device_id=(x, y, 1 - c), device_id_type=MESH)
            cp.start()
            cps.append(cp)
        for cp in cps:
            cp.wait()

    return pl.pallas_call(
        body, name="pair_exchange_grads_" + tag,
        in_specs=[ANY] * n, out_specs=[ANY] * n,
        out_shape=[out_shape(i) for i in range(n)],
        scratch_shapes=[pltpu.SemaphoreType.DMA((n,)), pltpu.SemaphoreType.DMA((n,))],
        compiler_params=_pair_params(),
    )(*grads)


def _pair_sum(gs, gots, c_arr, col_sharded, tag):
    n = len(gs)
    if col_sharded:
        rows, cols = gs[0].shape
        rh, cs = rows // 2, cols // N_CHIPS
        g_spec = pl.BlockSpec((rh, cs), lambda k, c_ref: (c_ref[0], k))
        got_spec = pl.BlockSpec((rh, cs), lambda k, c_ref: (0, k))
    else:
        _, rows, cs = gs[0].shape
        rh = rows // 2
        g_spec = pl.BlockSpec((1, rh, cs), lambda k, c_ref: (k, c_ref[0], 0))
        got_spec = pl.BlockSpec((1, rh, cs), lambda k, c_ref: (k, 0, 0))

    def body(c_ref, *refs):
        for g_ref, got_ref, out_ref in zip(refs[:n], refs[n:2 * n], refs[2 * n:]):
            total = g_ref[...].astype(F32) + got_ref[...].astype(F32)
            out_ref[...] = total.astype(BF16).reshape(out_ref.shape)

    return list(pl.pallas_call(
        body, name="pair_sum_" + tag,
        grid_spec=pltpu.PrefetchScalarGridSpec(
            num_scalar_prefetch=1, grid=(N_CHIPS,), in_specs=[g_spec] * n + [got_spec] * n,
            out_specs=[pl.BlockSpec((1, rh, cs), lambda k, c_ref: (k, 0, 0))] * n),
        out_shape=[_sds((N_CHIPS, rh, cs), BF16)] * n,
        compiler_params=_params(("arbitrary",)),
    )(c_arr, *gs, *gots))


def _chip_sum(parts, gots, qc_arr, tag):
    n = len(parts)
    _, rh, cs = parts[0].shape
    steps = 2 if rh % 32 == 0 else 1
    rb = rh // steps

    def body(qc_ref, *refs):
        for part_ref, got_ref, out_ref in zip(refs[:n], refs[n:2 * n], refs[2 * n:]):
            total = part_ref[0].astype(F32)
            for j in range(3):
                total = total + got_ref[j].astype(F32)
            out_ref[...] = total

    return list(pl.pallas_call(
        body, name="chip_sum_" + tag,
        grid_spec=pltpu.PrefetchScalarGridSpec(
            num_scalar_prefetch=1, grid=(steps,),
            in_specs=[pl.BlockSpec((1, rb, cs), lambda i, qc: (qc[0], i, 0))] * n
            + [pl.BlockSpec((3, rb, cs), lambda i, qc: (0, i, 0))] * n,
            out_specs=[pl.BlockSpec((rb, cs), lambda i, qc: (qc[1] * steps + i, 0))] * n),
        out_shape=[_sds((2 * rh, cs), F32)] * n,
        compiler_params=_params(("arbitrary",)),
    )(qc_arr, *parts, *gots))


def _pair_share_grads(grads, tag):
    n = len(grads)

    def body(*refs):
        g = refs[n:2 * n]
        ssem, rsem = refs[2 * n:]
        x, y, c = _mesh_pos()
        _pair_barrier(x, y, c)
        cps = []
        for i in range(n):
            mine = g[i].at[_half_rows(c, grads[i].shape[0] // 2), :]
            cp = pltpu.make_async_remote_copy(src_ref=mine, dst_ref=mine, send_sem=ssem.at[i], recv_sem=rsem.at[i],
                                              device_id=(x, y, 1 - c), device_id_type=MESH)
            cp.start()
            cps.append(cp)
        for cp in cps:
            cp.wait()

    return pl.pallas_call(
        body, name="pair_share_grads_" + tag,
        in_specs=[ANY] * n, out_specs=[ANY] * n,
        out_shape=[_sds(a.shape, a.dtype) for a in grads],
        input_output_aliases={i: i for i in range(n)},
        scratch_shapes=[pltpu.SemaphoreType.DMA((n,)), pltpu.SemaphoreType.DMA((n,))],
        compiler_params=_pair_params(),
    )(*grads)


def _small_reduce_update(parts, places, takes, loss_at, ws, ms, vs, rows_total, width, after):
    n, n_w = len(parts), len(ws)
    n_in = n + 3 * n_w + 1

    def body(*refs):
        ins = refs[:n]
        w_in, m_in, v_in = refs[n:n + n_w], refs[n + n_w:n + 2 * n_w], refs[n + 2 * n_w:n + 3 * n_w]
        outs = refs[n_in:n_in + 4 * n_w + 1]
        g_out, d_out, m_out, v_out = (outs[0:n_w], outs[n_w:2 * n_w], outs[2 * n_w:3 * n_w], outs[3 * n_w:4 * n_w])
        loss_ref = outs[4 * n_w]
        pack, pair_got, chip_sum, got, ssem, rsem = refs[n_in + 4 * n_w + 1:]
        x, y, c = _mesh_pos()
        chip = 2 * x + y
        pack[...] = jnp.zeros(pack.shape, F32)
        for i in range(n):
            for row, col, src_row, rows in places[i]:
                w = parts[i].shape[1]
                pack[row:row + rows, col:col + w] = ins[i][src_row:src_row + rows, :]
        swap = pltpu.make_async_remote_copy(src_ref=pack, dst_ref=pair_got, send_sem=ssem.at[3], recv_sem=rsem.at[3],
                                            device_id=(x, y, 1 - c), device_id_type=MESH)
        swap.start()
        swap.wait()
        chip_sum[...] = pack[...] + pair_got[...]
        cps = []
        for j, (px, py, _) in enumerate(_other_chips(x, y)):
            cp = pltpu.make_async_remote_copy(src_ref=chip_sum, dst_ref=got.at[j], send_sem=ssem.at[j],
                                              recv_sem=rsem.at[j], device_id=(px, py, c), device_id_type=MESH)
            cp.start()
            cps.append(cp)
        for cp in cps:
            cp.wait()
        total = jnp.zeros(pack.shape, F32)
        for q in range(N_CHIPS):
            rel = jnp.bitwise_xor(chip, q)
            theirs = got[jnp.maximum(rel - 1, 0)]
            total = total + jnp.where(rel == 0, chip_sum[...], theirs)
        pack[...] = total

        def take_own_columns(g_ref, d0, nr, s0, c0, w):
            for k in range(N_CHIPS):
                @pl.when(chip == k)
                def _():
                    g_ref[d0:d0 + nr, :] = pack[s0:s0 + nr, c0 + k * w:c0 + (k + 1) * w]

        for j in range(n_w):
            w = ws[j].shape[1]
            for d0, nr, s0, c0, sharded in takes[j]:
                if sharded:
                    take_own_columns(g_out[j], d0, nr, s0, c0, w)
                else:
                    g_out[j][d0:d0 + nr, :] = pack[s0:s0 + nr, c0:c0 + w]
            d_out[j][...], m_out[j][...], v_out[j][...] = _adamw_math(w_in[j][...], g_out[j][...], m_in[j][...], v_in[j][...])
        loss_ref[...] = pack[loss_at[0]:loss_at[0] + 1, loss_at[1]:loss_at[1] + LANES]

    shapes = [_sds(w.shape, F32) for w in ws]
    outs = pl.pallas_call(
        body, name="small_reduce_update",
        in_specs=[VMEM] * (n + 3 * n_w) + [ANY], out_specs=[VMEM] * (4 * n_w + 1),
        out_shape=shapes * 4 + [_sds((1, LANES), F32)],
        scratch_shapes=[pltpu.VMEM((rows_total, width), F32), pltpu.VMEM((rows_total, width), F32),
                        pltpu.VMEM((rows_total, width), F32), pltpu.VMEM((3, rows_total, width), F32),
                        pltpu.SemaphoreType.DMA((4,)), pltpu.SemaphoreType.DMA((4,))],
        compiler_params=_params(),
    )(*parts, *ws, *ms, *vs, after)
    return outs[0:n_w], outs[n_w:2 * n_w], outs[2 * n_w:3 * n_w], outs[3 * n_w:4 * n_w], outs[4 * n_w]


def _adamw_math(w, g, m, v):
    m2 = ADAM_B1 * m + (1.0 - ADAM_B1) * g
    v2 = ADAM_B2 * v + (1.0 - ADAM_B2) * (g * g)
    m_hat = m2 / (1.0 - ADAM_B1 ** ADAM_STEP)
    v_hat = v2 / (1.0 - ADAM_B2 ** ADAM_STEP)
    delta = -ADAM_LR * (m_hat / (jnp.sqrt(v_hat) + ADAM_EPS) + ADAM_WD * w)
    return delta, m2, v2


ADAMW_BLOCK_BYTES = 3 * 2 ** 19


def _adamw_big(ws, gs, ms, vs, tag):
    n = len(ws)
    rows, cols = ws[0].shape
    tr = 16
    for t in range(16, rows + 1, 16):
        if rows % t == 0 and t * cols * 4 * n <= ADAMW_BLOCK_BYTES:
            tr = t

    def body(*refs):
        ins, outs = refs[:4 * n], refs[4 * n:]
        for i in range(n):
            w_ref, g_ref, m_ref, v_ref = ins[i], ins[n + i], ins[2 * n + i], ins[3 * n + i]
            gg = g_ref[...]
            outs[4 * i][...] = gg
            outs[4 * i + 1][...], outs[4 * i + 2][...], outs[4 * i + 3][...] = _adamw_math(
                w_ref[...], gg, m_ref[...], v_ref[...])

    spec = pl.BlockSpec((tr, cols), lambda i: (i, 0))
    outs = pl.pallas_call(
        body, name="adamw_" + tag, grid=(rows // tr,),
        in_specs=[spec] * (4 * n), out_specs=[spec] * (4 * n),
        out_shape=[_sds((rows, cols), F32)] * (4 * n),
        compiler_params=_params(("arbitrary",)),
    )(*ws, *gs, *ms, *vs)
    return [outs[4 * i:4 * i + 4] for i in range(n)]


SMALL_ROWS = 40
PACK_ROWS = 64


def kernel(x, meta_tokens, pre_mix_norm, w_in, conv_a_w, conv_b_w, conv_b_bias, ln_b_gain, ln_b_bias, w_out, post_mix_norm, pre_ffn_norm, w_gate, w_up, w_down, post_ffn_norm, loss_target, m_meta_tokens, m_pre_mix_norm, m_w_in, m_conv_a_w, m_conv_b_w, m_conv_b_bias, m_ln_b_gain, m_ln_b_bias, m_w_out, m_post_mix_norm, m_pre_ffn_norm, m_w_gate, m_w_up, m_w_down, m_post_ffn_norm, v_meta_tokens, v_pre_mix_norm, v_w_in, v_conv_a_w, v_conv_b_w, v_conv_b_bias, v_ln_b_gain, v_ln_b_bias, v_w_out, v_post_mix_norm, v_pre_ffn_norm, v_w_gate, v_w_up, v_w_down, v_post_ffn_norm):
    xq, yq, cq = lax.axis_index("x"), lax.axis_index("y"), lax.axis_index("c")
    chip = 2 * xq + yq
    c_arr = jnp.reshape(cq, (1,)).astype(jnp.int32)
    qc_arr = jnp.stack([chip, cq]).astype(jnp.int32)

    seq, d = x.shape[1], x.shape[2]
    x2, tgt2 = x[0], loss_target[0]
    tr = lambda a: jnp.swapaxes(a, 1, 2)[0]
    w_in2, w_out2, w_gate2, w_up2, w_down2 = w_in[0], w_out[0], tr(w_gate), tr(w_up), w_down[0]
    ka, wa_sh = conv_a_w.shape[1], conv_a_w.shape[2]
    kb = conv_b_w.shape[1]
    meta_sh = meta_tokens.shape[1]

    small = jnp.zeros((PACK_ROWS, meta_sh), F32)
    small = small.at[0:N_META, :].set(meta_tokens)
    small = small.at[16:16 + ka, 0:wa_sh].set(conv_a_w[0])
    small = small.at[24:24 + kb, 0:wa_sh].set(conv_b_w[0])
    q_arr = jnp.reshape(chip, (1,)).astype(jnp.int32)
    small_own = lax.dynamic_update_slice(jnp.zeros((N_CHIPS, PACK_ROWS, meta_sh), F32), small[None], (chip, 0, 0))
    i_ssem, i_rsem, first, i_token = _gather_start([_cast_place(w_in2, q_arr, "w_in"), small_own], pre_mix_norm, "in")
    rest = [_cast_place(w, q_arr, nm, i_token)
            for w, nm in ((w_out2, "w_out"), (w_gate2, "w_gate"), (w_up2, "w_up"), (w_down2, "w_down"))]
    g_ssem, g_rsem, lands, g_token = _gather_start(rest, i_token, "rest")
    win4, small4 = _forward_pair(_gather_wait([0, 1], i_ssem, i_rsem, first, g_token, "in"), "in")
    meta_f = jnp.concatenate([small4[k, 0:N_META, :] for k in range(N_CHIPS)], axis=1)
    wa_f = jnp.concatenate([small4[k, 16:16 + ka, 0:wa_sh] for k in range(N_CHIPS)], axis=1)
    wb_f = jnp.concatenate([small4[k, 24:24 + kb, 0:wa_sh] for k in range(N_CHIPS)], axis=1)

    tm = _row_tile(seq + TAIL_ROWS)
    tail = lax.dynamic_update_slice(jnp.zeros((tm, d), F32), meta_f, (seq % tm, 0))
    h, xn1, hp5 = _mm_in(x2, tail, win4, pre_mix_norm, g_token)
    ya, z = _mix_conv_fwd(hp5, wa_f, wb_f, conv_b_bias)
    (wout4,) = _forward_pair(_gather_wait([0], g_ssem, g_rsem, lands[0:1], z, "out"), "out")
    wout_f = wout4.reshape(N_CHIPS * wout4.shape[1], wout4.shape[2])
    yb, mix, h1, xn2 = _mm_out(ya, z, h, wout_f, ln_b_gain, ln_b_bias, post_mix_norm, pre_ffn_norm)
    wg4, wu4 = _forward_pair(_gather_wait([1, 2], g_ssem, g_rsem, lands[1:3], xn2, "gate_up"), "gate_up")
    stacked = lambda a: a.reshape(a.shape[0] * a.shape[1], a.shape[2])
    wg_f, wu_f = stacked(wg4), stacked(wu4)
    p_act, q_act, f_act = _ffn_up(xn2, wg_f, wu_f)
    (wd4,) = _forward_pair(_gather_wait([3], g_ssem, g_rsem, lands[3:4], f_act, "down"), "down")
    wd_f = stacked(wd4)
    dff, dh2, loss_blk, d_gpf = _ffn_down(f_act, wd_f, h1, tgt2, post_ffn_norm)

    da, du = _ffn_bwd_act(dff, wd_f, p_act, q_act)
    by_chip = lambda g: g.reshape(N_CHIPS, g.shape[0] // N_CHIPS, g.shape[1])
    g_down = by_chip(_grad_w_down(f_act, dff))
    g_gate, g_up = [by_chip(g) for g in _grad_w_gate_up(xn2, da, du)]
    ffn = [g_gate, g_up, g_down]
    p_ssem, p_rsem, ffn, p_lands, p_token = _pair_exchange_start(ffn, [1, 1, 1], dff, "ffn")
    dh1, dmix, d_g2, d_gpm = _ffn_bwd_in(da, du, wg_f, wu_f, h1, mix, dh2, pre_ffn_norm, post_mix_norm, p_token)
    ffn, got = _pair_exchange_wait(p_ssem, p_rsem, ffn, p_lands, [1, 1, 1], d_g2, "ffn")
    parts = _pair_sum(ffn, got, c_arr, False, "ffn")
    f_ssem, f_rsem, parts, f_lands, f_token = _chip_exchange_start(parts, dff, "ffn")
    g_out = _grad_w_out(ya, yb, dmix, f_token)
    dya, dz, d_lg, d_lb = _mix_bwd_out(dmix, wout_f, z, ln_b_gain, ln_b_bias, f_token)
    dhp5, d_wa, d_wb, d_bb = _mix_conv_bwd(hp5, dya, dz, wa_f, wb_f)
    g_in = _grad_w_in(xn1, dhp5)

    g_out4 = g_out.reshape(N_CHIPS, g_out.shape[0] // N_CHIPS, g_out.shape[1])
    mixw = [g_in, g_out4]
    got2 = _pair_exchange_grads(mixw, [0, 1], "mix")
    parts2 = _pair_sum(mixw[0:1], got2[0:1], c_arr, True, "in") + _pair_sum(mixw[1:2], got2[1:2], c_arr, False, "out")
    m_ssem, m_rsem, parts2, m_lands, m_token = _chip_exchange_start(parts2, dhp5, "mix")
    grad_x2, d_meta, d_g1 = _mix_bwd_in(dhp5, win4, h, dh1, pre_mix_norm, m_token)
    grad_x = grad_x2[None]

    parts, f_recv = _chip_exchange_wait(f_ssem, f_rsem, parts, f_lands, d_g1, "ffn")
    halves = _chip_sum(parts, f_recv, qc_arr, "ffn")
    gsum_ffn = _pair_share_grads(halves, "ffn")

    names_big = ["w_in", "w_out", "w_gate", "w_up", "w_down"]
    w_big = dict(zip(names_big, [w_in2, w_out2, w_gate2, w_up2, w_down2]))
    m_big = dict(zip(names_big, [m_w_in[0], m_w_out[0], tr(m_w_gate), tr(m_w_up), m_w_down[0]]))
    v_big = dict(zip(names_big, [v_w_in[0], v_w_out[0], tr(v_w_gate), tr(v_w_up), v_w_down[0]]))
    grads, deltas, new_m, new_v = {}, {}, {}, {}

    def update(names, gs, tag):
        res = _adamw_big([w_big[k] for k in names], gs, [m_big[k] for k in names], [v_big[k] for k in names], tag)
        for nm, outs in zip(names, res):
            if nm in ("w_gate", "w_up"):
                outs = [jnp.swapaxes(o[None], 1, 2) for o in outs]
            else:
                outs = [o[None] for o in outs]
            grads[nm], deltas[nm], new_m[nm], new_v[nm] = outs
        return res[-1][1]

    last = update(["w_gate", "w_up", "w_down"], list(gsum_ffn), "ffn")

    hw = d // 2
    assert d_wa.shape == (3, hw) and d_wb.shape == (31, hw) and d_bb.shape == (1, hw)
    small_parts = [d_meta, d_g1, d_gpm, d_g2, d_gpf, d_bb, d_lg, d_lb, loss_blk[0:1, :], d_wa, d_wb]
    places = [[(0, 0, 0, N_META)], [(16, 0, 0, 1)], [(17, 0, 0, 1)], [(18, 0, 0, 1)], [(19, 0, 0, 1)],
              [(20, 0, 0, 1)], [(20, hw, 0, 1)], [(21, 0, 0, 1)], [(21, hw, 0, 1)], [(22, 0, 0, 3)],
              [(22, hw, 0, 3), (25, 0, 3, 14), (25, hw, 17, 14)]]
    names_small = ["meta_tokens", "pre_mix_norm", "conv_a_w", "conv_b_w", "conv_b_bias", "ln_b_gain", "ln_b_bias",
                   "post_mix_norm", "pre_ffn_norm", "post_ffn_norm"]
    takes = [[(0, N_META, 0, 0, True)], [(0, 1, 16, 0, False)], [(0, 3, 22, 0, True)],
             [(0, 3, 22, hw, True), (3, 14, 25, 0, True), (17, 14, 25, hw, True)], [(0, 1, 20, 0, False)],
             [(0, 1, 20, hw, False)], [(0, 1, 21, 0, False)], [(0, 1, 17, 0, False)], [(0, 1, 18, 0, False)],
             [(0, 1, 19, 0, False)]]
    w_small = [meta_tokens, pre_mix_norm, conv_a_w[0], conv_b_w[0], conv_b_bias, ln_b_gain, ln_b_bias, post_mix_norm,
               pre_ffn_norm, post_ffn_norm]
    m_small = [m_meta_tokens, m_pre_mix_norm, m_conv_a_w[0], m_conv_b_w[0], m_conv_b_bias, m_ln_b_gain, m_ln_b_bias,
               m_post_mix_norm, m_pre_ffn_norm, m_post_ffn_norm]
    v_small = [v_meta_tokens, v_pre_mix_norm, v_conv_a_w[0], v_conv_b_w[0], v_conv_b_bias, v_ln_b_gain, v_ln_b_bias,
               v_post_mix_norm, v_pre_ffn_norm, v_post_ffn_norm]
    g_s, d_s, m_s, v_s, loss_row = _small_reduce_update(small_parts, places, takes, (21, hw), w_small, m_small, v_small,
                                                        SMALL_ROWS, d, last)
    loss = loss_row[0, 0]
    for i, nm in enumerate(names_small):
        lead = nm in ("conv_a_w", "conv_b_w")
        fix = (lambda a: a[None]) if lead else (lambda a: a)
        grads[nm], deltas[nm], new_m[nm], new_v[nm] = fix(g_s[i]), fix(d_s[i]), fix(m_s[i]), fix(v_s[i])

    parts2, m_recv = _chip_exchange_wait(m_ssem, m_rsem, parts2, m_lands, loss_row, "mix")
    halves2 = _chip_sum(parts2[0:1], m_recv[0:1], qc_arr, "in") + _chip_sum(parts2[1:2], m_recv[1:2], qc_arr, "out")
    gsum_mix = _pair_share_grads(halves2, "mix")
    update(["w_in"], [gsum_mix[0]], "w_in")
    update(["w_out"], [gsum_mix[1]], "w_out")

    order = ["meta_tokens", "pre_mix_norm", "w_in", "conv_a_w", "conv_b_w", "conv_b_bias", "ln_b_gain", "ln_b_bias", "w_out",
             "post_mix_norm", "pre_ffn_norm", "w_gate", "w_up", "w_down", "post_ffn_norm"]
    return (loss, grad_x, *[grads[k] for k in order], *[deltas[k] for k in order], *[new_m[k] for k in order],
            *[new_v[k] for k in order])
```

```python
import functools

import jax
import jax.numpy as jnp
from jax import lax
from jax.experimental import pallas as pl
from jax.experimental.pallas import tpu as pltpu

F32 = jnp.float32
BF16 = jnp.bfloat16
MESH = pl.DeviceIdType.MESH

N_META = 16
TAIL_ROWS = 128
RMS_EPS = 1e-6
LN_EPS = 1e-5
ADAM_LR = 0.001
ADAM_B1 = 0.9
ADAM_B2 = 0.999
ADAM_EPS = 1e-08
ADAM_WD = 0.01
ADAM_STEP = 10

N_CHIPS = 4
LANES = 128
MXU_TILE = 256
CONV_CHUNK = 48
CONV_HIST = 32
ROW_TILE_CAP = 640
VMEM_LIMIT = 56 * 1024 * 1024

NN = (((1,), (0,)), ((), ()))
NT = (((1,), (1,)), ((), ()))
TN = (((0,), (0,)), ((), ()))


def _dot(a, b, dims=NN):
    return lax.dot_general(a, b, dims, preferred_element_type=F32)


def _sig(v):
    return 1.0 / (1.0 + jnp.exp(-v))


def _mean(v):
    return jnp.mean(v, axis=-1, keepdims=True)


def _row_tile(rows):
    best = 16
    for t in range(16, min(rows, ROW_TILE_CAP) + 1, 16):
        if rows % t == 0:
            best = t
    assert rows % best == 0
    return best


def _row_parts(tm):
    if tm % 32:
        return [slice(0, tm)]
    return [slice(0, tm // 2), slice(tm // 2, tm)]


def _pieces(n_shards, shard_w, piece_w):
    total = n_shards * shard_w
    cuts = sorted(set(range(0, total + 1, shard_w)) | set(range(0, total + 1, piece_w)))
    out = []
    for lo, hi in zip(cuts[:-1], cuts[1:]):
        out.append((lo // shard_w, lo % shard_w, lo // piece_w, lo % piece_w, hi - lo))
    return out


def _params(semantics=None):
    kw = dict(vmem_limit_bytes=VMEM_LIMIT)
    if semantics is not None:
        kw["dimension_semantics"] = semantics
    return pltpu.CompilerParams(**kw)


def _full(shape):
    nd = len(shape)
    return pl.BlockSpec(shape, lambda *_: (0,) * nd)


def _resident(shape):
    nd = len(shape)
    return pl.BlockSpec(shape, lambda *_: (0,) * nd, pipeline_mode=pl.Buffered(1))


def _sds(shape, dtype):
    return jax.ShapeDtypeStruct(shape, dtype)


ANY = pl.BlockSpec(memory_space=pl.ANY)
VMEM = pl.BlockSpec(memory_space=pltpu.VMEM)


def _mesh_pos():
    return lax.axis_index("x"), lax.axis_index("y"), lax.axis_index("c")


def _flip(v, bit):
    return 1 - v if bit else v


def _mm_in(x, tail, win4, g1, after):
    seq, d = x.shape
    tp = seq + TAIL_ROWS
    tm = _row_tile(tp)
    n_sh, _, csh = win4.shape
    pw = n_sh * csh // 5
    pieces = _pieces(n_sh, csh, pw)

    def body(x_ref, tail_ref, w_ref, g_ref, after_ref, h_ref, xn_ref, hp_ref):
        rows = pl.program_id(0) * tm + lax.broadcasted_iota(jnp.int32, (tm, 1), 0)
        hh = jnp.where(rows < seq, x_ref[...], tail_ref[...])
        h_ref[...] = hh
        r = lax.rsqrt(_mean(hh * hh) + RMS_EPS)
        xn = (hh * r * g_ref[...]).astype(BF16)
        xn_ref[...] = xn
        for k, klo, p, plo, w in pieces:
            hp_ref[p, :, plo:plo + w] = _dot(xn, w_ref[k, :, klo:klo + w])

    row = pl.BlockSpec((tm, d), lambda i: (i, 0))
    return pl.pallas_call(
        body, name="mm_in", grid=(tp // tm,),
        in_specs=[row, _full(tail.shape), _resident(win4.shape), _full(g1.shape), ANY],
        out_specs=[row, row, pl.BlockSpec((5, tm, pw), lambda i: (0, i, 0))],
        out_shape=[_sds((tp, d), F32), _sds((tp, d), BF16), _sds((5, tp, pw), F32)],
        compiler_params=_params(("arbitrary",)),
    )(x, tail, win4, g1, after)


def _seq_rows(tp):
    seq = tp - TAIL_ROWS
    nseq = seq + N_META
    assert nseq % CONV_CHUNK == 0 and seq % 16 == 0
    return seq, nseq


SUBLANES = 8


def _conv_offsets(width, transpose):
    return [(width - 1 - k) if transpose else (CONV_HIST - (width - 1) + k) for k in range(width)]


def _shift_copies(src_ref, sh_ref, width, transpose):
    n = src_ref.shape[0] - SUBLANES
    for s in sorted({o % SUBLANES for o in _conv_offsets(width, transpose)} - {0}):
        sh_ref[s - 1, 0:n, :] = src_ref[s:s + n, :]


def _tap_rows(src_ref, sh_ref, base, off):
    start = pl.multiple_of(base + (off // SUBLANES) * SUBLANES, SUBLANES)
    if off % SUBLANES == 0:
        return src_ref[pl.ds(start, CONV_CHUNK), :]
    return sh_ref[off % SUBLANES - 1, pl.ds(start, CONV_CHUNK), :]


def _conv_taps(src_ref, sh_ref, w_ref, dst_ref, width, nseq, transpose):
    w = w_ref[...]
    offs = _conv_offsets(width, transpose)
    _shift_copies(src_ref, sh_ref, width, transpose)

    def step(n, carry):
        out0 = pl.multiple_of(CONV_HIST + n * CONV_CHUNK, SUBLANES)
        base = out0 if transpose else n * CONV_CHUNK
        acc = jnp.zeros((CONV_CHUNK, w.shape[1]), F32)
        for k, off in enumerate(offs):
            acc = acc + w[k:k + 1, :] * _tap_rows(src_ref, sh_ref, base, off)
        dst_ref[pl.ds(out0, CONV_CHUNK), :] = acc
        return carry

    lax.fori_loop(0, nseq // CONV_CHUNK, step, 0)


def _conv_wgrad(src_ref, sh_ref, dz_ref, acc_ref, width, nseq):
    acc_ref[...] = jnp.zeros(acc_ref.shape, F32)
    offs = _conv_offsets(width, False)

    def step(n, carry):
        dzc = dz_ref[pl.ds(pl.multiple_of(CONV_HIST + n * CONV_CHUNK, SUBLANES), CONV_CHUNK), :]
        for k, off in enumerate(offs):
            prod = dzc * _tap_rows(src_ref, sh_ref, n * CONV_CHUNK, off)
            part = prod[0:SUBLANES, :]
            for s in range(1, CONV_CHUNK // SUBLANES):
                part = part + prod[SUBLANES * s:SUBLANES * (s + 1), :]
            acc_ref[SUBLANES * k:SUBLANES * (k + 1), :] += part
        return carry

    lax.fori_loop(0, nseq // CONV_CHUNK, step, 0)


def _to_seq(buf_ref, x_part, meta_part, seq):
    buf_ref[CONV_HIST:CONV_HIST + N_META, :] = meta_part
    buf_ref[CONV_HIST + N_META:CONV_HIST + N_META + seq, :] = x_part


def _zero_ends(buf_ref, nseq):
    zeros = jnp.zeros((CONV_HIST, buf_ref.shape[1]), F32)
    buf_ref[0:CONV_HIST, :] = zeros
    buf_ref[CONV_HIST + nseq:CONV_HIST + nseq + CONV_HIST, :] = zeros


def _mix_conv_fwd(hp5, wa, wb, bb):
    _, tp, wgrp = hp5.shape
    seq, nseq = _seq_rows(tp)
    sb = nseq + 2 * CONV_HIST
    ka, kb = wa.shape[0], wb.shape[0]
    xs, ms = slice(0, seq), slice(seq, seq + N_META)
    ox, om = slice(CONV_HIST + N_META, CONV_HIST + nseq), slice(CONV_HIST, CONV_HIST + N_META)

    def body(hp_ref, wa_ref, wb_ref, bb_ref, ya_ref, z_ref, s_ref, o_ref, sh_ref):
        _zero_ends(s_ref, nseq)
        _to_seq(s_ref, hp_ref[1, xs, :] * hp_ref[2, xs, :], hp_ref[1, ms, :] * hp_ref[2, ms, :], seq)
        _conv_taps(s_ref, sh_ref, wa_ref, o_ref, ka, nseq, False)
        ya_ref[xs, :] = (hp_ref[0, xs, :] * o_ref[ox, :]).astype(BF16)
        ya_ref[ms, :] = (hp_ref[0, ms, :] * o_ref[om, :]).astype(BF16)
        ya_ref[seq + N_META:tp, :] = jnp.zeros((tp - seq - N_META, LANES), BF16)
        _to_seq(s_ref, hp_ref[3, xs, :] * _sig(hp_ref[4, xs, :]), hp_ref[3, ms, :] * _sig(hp_ref[4, ms, :]), seq)
        _conv_taps(s_ref, sh_ref, wb_ref, o_ref, kb, nseq, False)
        z_ref[xs, :] = o_ref[ox, :] + bb_ref[...]
        z_ref[ms, :] = o_ref[om, :] + bb_ref[...]
        z_ref[seq + N_META:tp, :] = jnp.zeros((tp - seq - N_META, LANES), F32)

    col = lambda j: (0, j)
    return pl.pallas_call(
        body, name="mix_conv_fwd", grid=(wgrp // LANES,),
        in_specs=[pl.BlockSpec((5, tp, LANES), lambda j: (0, 0, j)), pl.BlockSpec((ka, LANES), col),
                  pl.BlockSpec((kb, LANES), col), pl.BlockSpec((1, LANES), col)],
        out_specs=[pl.BlockSpec((tp, LANES), col), pl.BlockSpec((tp, LANES), col)],
        out_shape=[_sds((tp, wgrp), BF16), _sds((tp, wgrp), F32)],
        scratch_shapes=[pltpu.VMEM((sb, LANES), F32), pltpu.VMEM((sb, LANES), F32),
                        pltpu.VMEM((SUBLANES - 1, sb, LANES), F32)],
        compiler_params=_params(("arbitrary",)),
    )(hp5, wa, wb, bb)


def _layer_norm_parts(z, lg, lb):
    mu = _mean(z)
    zc = z - mu
    rl = lax.rsqrt(_mean(zc * zc) + LN_EPS)
    zh = zc * rl
    return rl, zh, zh * lg + lb


def _mm_out(ya, z, h, wout, lg, lb, gpm, g2):
    tp, d = h.shape
    wa_ = ya.shape[1]
    tm = _row_tile(tp)

    def body(ya_ref, z_ref, h_ref, w_ref, lg_ref, lb_ref, gpm_ref, g2_ref, yb_ref, mix_ref, h1_ref, xn2_ref):
        for rs in _row_parts(tm):
            _, _, l = _layer_norm_parts(z_ref[rs, :], lg_ref[...], lb_ref[...])
            yb = (l * _sig(l)).astype(BF16)
            yb_ref[rs, :] = yb
            mix = _dot(ya_ref[rs, :], w_ref[0:wa_, :]) + _dot(yb, w_ref[wa_:d, :])
            mix_ref[rs, :] = mix
            rm = lax.rsqrt(_mean(mix * mix) + RMS_EPS)
            h1 = h_ref[rs, :] + mix * rm * gpm_ref[...]
            h1_ref[rs, :] = h1
            r2 = lax.rsqrt(_mean(h1 * h1) + RMS_EPS)
            xn2_ref[rs, :] = (h1 * r2 * g2_ref[...]).astype(BF16)

    row = lambda i: (i, 0)
    return pl.pallas_call(
        body, name="mm_out", grid=(tp // tm,),
        in_specs=[pl.BlockSpec((tm, wa_), row), pl.BlockSpec((tm, wa_), row), pl.BlockSpec((tm, d), row),
                  _resident(wout.shape), _full(lg.shape), _full(lb.shape), _full(gpm.shape), _full(g2.shape)],
        out_specs=[pl.BlockSpec((tm, wa_), row), pl.BlockSpec((tm, d), row), pl.BlockSpec((tm, d), row),
                   pl.BlockSpec((tm, d), row)],
        out_shape=[_sds((tp, wa_), BF16), _sds((tp, d), F32), _sds((tp, d), F32), _sds((tp, d), BF16)],
        compiler_params=_params(("arbitrary",)),
    )(ya, z, h, wout, lg, lb, gpm, g2)


def _ffn_up(xn2, wg, wu):
    tp, d = xn2.shape
    ff_dim = wg.shape[0]
    tm = _row_tile(tp)
    assert ff_dim % MXU_TILE == 0

    def body(xn_ref, wg_ref, wu_ref, p_ref, q_ref, f_ref):
        xn = xn_ref[...]
        for lo in range(0, ff_dim, MXU_TILE):
            cols = slice(lo, lo + MXU_TILE)
            a = _dot(xn, wg_ref[cols, :], NT)
            u = _dot(xn, wu_ref[cols, :], NT)
            s = _sig(a)
            q = a * s
            p_ref[:, cols] = (u * (s + q * (1.0 - s))).astype(BF16)
            q_ref[:, cols] = q.astype(BF16)
            f_ref[:, cols] = (q * u).astype(BF16)

    ospec = pl.BlockSpec((tm, ff_dim), lambda i: (i, 0))
    return pl.pallas_call(
        body, name="ffn_up", grid=(tp // tm,),
        in_specs=[pl.BlockSpec((tm, d), lambda i: (i, 0)), _resident(wg.shape), _resident(wu.shape)],
        out_specs=[ospec, ospec, ospec],
        out_shape=[_sds((tp, ff_dim), BF16)] * 3,
        compiler_params=_params(("arbitrary",)),
    )(xn2, wg, wu)


def _ffn_down(f, wd, h1, tgt, gpf):
    tp, ff_dim = f.shape
    d = h1.shape[1]
    tm = _row_tile(tp)
    seq, _ = _seq_rows(tp)

    def body(f_ref, w_ref, h1_ref, t_ref, gpf_ref, dff_ref, dh2_ref, loss_ref, dgpf_ref):
        i = pl.program_id(0)
        gpf_ = gpf_ref[...]

        @pl.when(i == 0)
        def _():
            loss_ref[...] = jnp.zeros(loss_ref.shape, F32)
            dgpf_ref[...] = jnp.zeros(dgpf_ref.shape, F32)

        for rs in _row_parts(tm):
            ff = _dot(f_ref[rs, :], w_ref[...])
            rf = lax.rsqrt(_mean(ff * ff) + RMS_EPS)
            nf = ff * rf
            h2 = h1_ref[rs, :] + nf * gpf_
            rows = i * tm + rs.start + lax.broadcasted_iota(jnp.int32, (rs.stop - rs.start, 1), 0)
            err = jnp.where(rows < seq, h2 - t_ref[rs, :], 0.0)
            dh2 = err * (1.0 / d)
            dh2_ref[rs, :] = dh2
            dn = dh2 * gpf_
            dff_ref[rs, :] = (rf * (dn - nf * _mean(dn * nf))).astype(BF16)
            loss_ref[...] += (0.5 / d) * jnp.sum(err * err, axis=(0, 1), keepdims=True)
            dgpf_ref[...] += jnp.sum(dh2 * nf, axis=0, keepdims=True)

    row = lambda i: (i, 0)
    return pl.pallas_call(
        body, name="ffn_down", grid=(tp // tm,),
        in_specs=[pl.BlockSpec((tm, ff_dim), row), _resident(wd.shape), pl.BlockSpec((tm, d), row),
                  pl.BlockSpec((tm, d), row), _full(gpf.shape)],
        out_specs=[pl.BlockSpec((tm, d), row), pl.BlockSpec((tm, d), row), _full((8, LANES)), _full((1, d))],
        out_shape=[_sds((tp, d), BF16), _sds((tp, d), F32), _sds((8, LANES), F32), _sds((1, d), F32)],
        compiler_params=_params(("arbitrary",)),
    )(f, wd, h1, tgt, gpf)


def _ffn_bwd_act(dff, wd, p, q):
    tp, d = dff.shape
    ff_dim = wd.shape[0]
    tm = _row_tile(tp)

    def body(dff_ref, w_ref, p_ref, q_ref, da_ref, du_ref):
        dffv = dff_ref[...]
        for lo in range(0, ff_dim, MXU_TILE):
            cols = slice(lo, lo + MXU_TILE)
            df = _dot(dffv, w_ref[cols, :], NT).astype(BF16)
            da_ref[:, cols] = df * p_ref[:, cols]
            du_ref[:, cols] = df * q_ref[:, cols]

    aspec = pl.BlockSpec((tm, ff_dim), lambda i: (i, 0))
    return pl.pallas_call(
        body, name="ffn_bwd_act", grid=(tp // tm,),
        in_specs=[pl.BlockSpec((tm, d), lambda i: (i, 0)), _resident(wd.shape), aspec, aspec],
        out_specs=[aspec, aspec],
        out_shape=[_sds((tp, ff_dim), BF16)] * 2,
        compiler_params=_params(("arbitrary",)),
    )(dff, wd, p, q)


def _grad_blocks(ff_dim):
    rows = ff_dim // 2
    assert rows % LANES == 0
    return rows


def _grad_w_down(f, dff):
    tp, ff_dim = f.shape
    d = dff.shape[1]
    rows = _grad_blocks(ff_dim)

    def body(f_ref, dff_ref, g_ref):
        g_ref[...] = _dot(f_ref[...], dff_ref[...], TN).astype(BF16)

    return pl.pallas_call(
        body, name="grad_w_down", grid=(ff_dim // rows,),
        in_specs=[pl.BlockSpec((tp, rows), lambda k: (0, k)), _resident(dff.shape)],
        out_specs=pl.BlockSpec((rows, d), lambda k: (k, 0)),
        out_shape=_sds((ff_dim, d), BF16),
        compiler_params=_params(("arbitrary",)),
    )(f, dff)


def _grad_w_gate_up(xn2, da, du):
    tp, ff_dim = da.shape
    d = xn2.shape[1]
    rows = _grad_blocks(ff_dim)

    def body(xn_ref, da_ref, du_ref, gg_ref, gu_ref):
        xn = xn_ref[...]
        gg_ref[...] = _dot(da_ref[...], xn, TN).astype(BF16)
        gu_ref[...] = _dot(du_ref[...], xn, TN).astype(BF16)

    aspec = pl.BlockSpec((tp, rows), lambda k: (0, k))
    gspec = pl.BlockSpec((rows, d), lambda k: (k, 0))
    return pl.pallas_call(
        body, name="grad_w_gate_up", grid=(ff_dim // rows,),
        in_specs=[_resident(xn2.shape), aspec, aspec],
        out_specs=[gspec, gspec],
        out_shape=[_sds((ff_dim, d), BF16)] * 2,
        compiler_params=_params(("arbitrary",)),
    )(xn2, da, du)


def _rms_bwd(dy, x, r, g):
    n = x * r
    dn = dy * g
    return r * (dn - n * _mean(dn * n)), dy * n


def _ffn_bwd_in(da, du, wg, wu, h1, mix, dh2, g2, gpm, after):
    tp, ff_dim = da.shape
    d = h1.shape[1]
    tm = _row_tile(tp)

    def body(da_ref, du_ref, wg_ref, wu_ref, h1_ref, mix_ref, dh2_ref, g2_ref, gpm_ref, after_ref,
             dh1_ref, dmix_ref, dg2_ref, dgpm_ref):
        i = pl.program_id(0)

        @pl.when(i == 0)
        def _():
            dg2_ref[...] = jnp.zeros(dg2_ref.shape, F32)
            dgpm_ref[...] = jnp.zeros(dgpm_ref.shape, F32)

        for rs in _row_parts(tm):
            dxn = _dot(da_ref[rs, :], wg_ref[...]) + _dot(du_ref[rs, :], wu_ref[...])
            h1v = h1_ref[rs, :]
            r2 = lax.rsqrt(_mean(h1v * h1v) + RMS_EPS)
            dres, dg2_rows = _rms_bwd(dxn, h1v, r2, g2_ref[...])
            dh1 = dh2_ref[rs, :] + dres
            dh1_ref[rs, :] = dh1
            mixv = mix_ref[rs, :]
            rm = lax.rsqrt(_mean(mixv * mixv) + RMS_EPS)
            dmix, dgpm_rows = _rms_bwd(dh1, mixv, rm, gpm_ref[...])
            dmix_ref[rs, :] = dmix.astype(BF16)
            dg2_ref[...] += jnp.sum(dg2_rows, axis=0, keepdims=True)
            dgpm_ref[...] += jnp.sum(dgpm_rows, axis=0, keepdims=True)

    aspec = pl.BlockSpec((tm, ff_dim), lambda i: (i, 0))
    row = pl.BlockSpec((tm, d), lambda i: (i, 0))
    return pl.pallas_call(
        body, name="ffn_bwd_in", grid=(tp // tm,),
        in_specs=[aspec, aspec, _resident(wg.shape), _resident(wu.shape), row, row, row, _full(g2.shape), _full(gpm.shape),
                  ANY],
        out_specs=[row, row, _full((1, d)), _full((1, d))],
        out_shape=[_sds((tp, d), F32), _sds((tp, d), BF16), _sds((1, d), F32), _sds((1, d), F32)],
        compiler_params=_params(("arbitrary",)),
    )(da, du, wg, wu, h1, mix, dh2, g2, gpm, after)


def _grad_w_out(ya, yb, dmix, after):
    tp, wa_ = ya.shape
    d = dmix.shape[1]

    def body(ya_ref, yb_ref, dmix_ref, after_ref, g_ref):
        dm = dmix_ref[...]
        g_ref[0:wa_, :] = _dot(ya_ref[...], dm, TN).astype(BF16)
        g_ref[wa_:2 * wa_, :] = _dot(yb_ref[...], dm, TN).astype(BF16)

    return pl.pallas_call(
        body, name="grad_w_out", grid=(1,),
        in_specs=[_full(ya.shape), _full(yb.shape), _full(dmix.shape), ANY],
        out_specs=_full((2 * wa_, d)),
        out_shape=_sds((2 * wa_, d), BF16),
        compiler_params=_params(("arbitrary",)),
    )(ya, yb, dmix, after)


def _mix_bwd_out(dmix, wout, z, lg, lb, after):
    tp, d = dmix.shape
    wa_ = z.shape[1]
    tm = _row_tile(tp)

    def body(dmix_ref, w_ref, z_ref, lg_ref, lb_ref, after_ref, dya_ref, dz_ref, dlg_ref, dlb_ref):
        i = pl.program_id(0)
        lg_ = lg_ref[...]

        @pl.when(i == 0)
        def _():
            dlg_ref[...] = jnp.zeros(dlg_ref.shape, F32)
            dlb_ref[...] = jnp.zeros(dlb_ref.shape, F32)

        for rs in _row_parts(tm):
            dm = dmix_ref[rs, :]
            dya_ref[rs, :] = _dot(dm, w_ref[0:wa_, :], NT)
            dyb = _dot(dm, w_ref[wa_:d, :], NT)
            rl, zh, l = _layer_norm_parts(z_ref[rs, :], lg_, lb_ref[...])
            sl = _sig(l)
            dl = dyb * (sl * (1.0 + l * (1.0 - sl)))
            dzh = dl * lg_
            dz_ref[rs, :] = rl * (dzh - _mean(dzh) - zh * _mean(dzh * zh))
            dlg_ref[...] += jnp.sum(dl * zh, axis=0, keepdims=True)
            dlb_ref[...] += jnp.sum(dl, axis=0, keepdims=True)

    row = lambda i: (i, 0)
    return pl.pallas_call(
        body, name="mix_bwd_out", grid=(tp // tm,),
        in_specs=[pl.BlockSpec((tm, d), row), _resident(wout.shape), pl.BlockSpec((tm, wa_), row), _full(lg.shape),
                  _full(lb.shape), ANY],
        out_specs=[pl.BlockSpec((tm, wa_), row), pl.BlockSpec((tm, wa_), row), _full((1, wa_)), _full((1, wa_))],
        out_shape=[_sds((tp, wa_), F32), _sds((tp, wa_), F32), _sds((1, wa_), F32), _sds((1, wa_), F32)],
        compiler_params=_params(("arbitrary",)),
    )(dmix, wout, z, lg, lb, after)


def _mix_conv_bwd(hp5, dya, dz, wa, wb):
    _, tp, wgrp = hp5.shape
    seq, nseq = _seq_rows(tp)
    sb = nseq + 2 * CONV_HIST
    ka, kb = wa.shape[0], wb.shape[0]
    xs, ms = slice(0, seq), slice(seq, seq + N_META)
    ox, om = slice(CONV_HIST + N_META, CONV_HIST + nseq), slice(CONV_HIST, CONV_HIST + N_META)
    n_tail = tp - seq - N_META

    def body(hp_ref, dya_ref, dz_ref, wa_ref, wb_ref, dhp_ref, dwa_ref, dwb_ref, dbb_ref, s_ref, d_ref, o_ref, acc_ref,
             shs_ref, shd_ref):
        _zero_ends(s_ref, nseq)
        _zero_ends(d_ref, nseq)

        def put(p, ox_val, om_val):
            dhp_ref[p, xs, :] = ox_val.astype(BF16)
            dhp_ref[p, ms, :] = om_val.astype(BF16)
            dhp_ref[p, seq + N_META:tp, :] = jnp.zeros((n_tail, LANES), BF16)

        def wgrad(dw_ref, width):
            for k in range(width):
                dw_ref[k:k + 1, :] = jnp.sum(acc_ref[8 * k:8 * k + 8, :], axis=0, keepdims=True)

        _to_seq(s_ref, hp_ref[1, xs, :] * hp_ref[2, xs, :], hp_ref[1, ms, :] * hp_ref[2, ms, :], seq)
        _conv_taps(s_ref, shs_ref, wa_ref, o_ref, ka, nseq, False)
        put(0, dya_ref[xs, :] * o_ref[ox, :], dya_ref[ms, :] * o_ref[om, :])
        _to_seq(d_ref, dya_ref[xs, :] * hp_ref[0, xs, :], dya_ref[ms, :] * hp_ref[0, ms, :], seq)
        _conv_wgrad(s_ref, shs_ref, d_ref, acc_ref, ka, nseq)
        wgrad(dwa_ref, ka)
        _conv_taps(d_ref, shd_ref, wa_ref, o_ref, ka, nseq, True)
        put(1, o_ref[ox, :] * hp_ref[2, xs, :], o_ref[om, :] * hp_ref[2, ms, :])
        put(2, o_ref[ox, :] * hp_ref[1, xs, :], o_ref[om, :] * hp_ref[1, ms, :])

        _to_seq(s_ref, hp_ref[3, xs, :] * _sig(hp_ref[4, xs, :]), hp_ref[3, ms, :] * _sig(hp_ref[4, ms, :]), seq)
        _to_seq(d_ref, dz_ref[xs, :], dz_ref[ms, :], seq)
        dbb_ref[...] = (jnp.sum(dz_ref[xs, :], axis=0, keepdims=True)
                        + jnp.sum(dz_ref[ms, :], axis=0, keepdims=True))
        _shift_copies(s_ref, shs_ref, kb, False)
        _conv_wgrad(s_ref, shs_ref, d_ref, acc_ref, kb, nseq)
        wgrad(dwb_ref, kb)
        _conv_taps(d_ref, shd_ref, wb_ref, o_ref, kb, nseq, True)
        sx, sm = _sig(hp_ref[4, xs, :]), _sig(hp_ref[4, ms, :])
        put(3, o_ref[ox, :] * sx, o_ref[om, :] * sm)
        put(4, o_ref[ox, :] * hp_ref[3, xs, :] * sx * (1.0 - sx), o_ref[om, :] * hp_ref[3, ms, :] * sm * (1.0 - sm))

    col = lambda j: (0, j)
    blk5 = pl.BlockSpec((5, tp, LANES), lambda j: (0, 0, j))
    return pl.pallas_call(
        body, name="mix_conv_bwd", grid=(wgrp // LANES,),
        in_specs=[blk5, pl.BlockSpec((tp, LANES), col), pl.BlockSpec((tp, LANES), col),
                  pl.BlockSpec((ka, LANES), col), pl.BlockSpec((kb, LANES), col)],
        out_specs=[blk5, pl.BlockSpec((ka, LANES), col), pl.BlockSpec((kb, LANES), col), pl.BlockSpec((1, LANES), col)],
        out_shape=[_sds((5, tp, wgrp), BF16), _sds((ka, wgrp), F32), _sds((kb, wgrp), F32), _sds((1, wgrp), F32)],
        scratch_shapes=[pltpu.VMEM((sb, LANES), F32), pltpu.VMEM((sb, LANES), F32), pltpu.VMEM((sb, LANES), F32),
                        pltpu.VMEM((SUBLANES * kb, LANES), F32), pltpu.VMEM((SUBLANES - 1, sb, LANES), F32),
                        pltpu.VMEM((SUBLANES - 1, sb, LANES), F32)],
        compiler_params=_params(("arbitrary",)),
    )(hp5, dya, dz, wa, wb)


def _grad_w_in(xn1, dhp5):
    n_p, tp, pw = dhp5.shape
    d = xn1.shape[1]

    def body(xn_ref, dhp_ref, g_ref):
        g_ref[...] = _dot(xn_ref[...], dhp_ref[0], TN).astype(BF16)

    return pl.pallas_call(
        body, name="grad_w_in", grid=(n_p,),
        in_specs=[_resident(xn1.shape), pl.BlockSpec((1, tp, pw), lambda p: (p, 0, 0))],
        out_specs=pl.BlockSpec((d, pw), lambda p: (0, p)),
        out_shape=_sds((d, n_p * pw), BF16),
        compiler_params=_params(("arbitrary",)),
    )(xn1, dhp5)


def _mix_bwd_in(dhp5, win4, h, dh1, g1, after):
    n_p, tp, pw = dhp5.shape
    d = h.shape[1]
    n_sh, _, csh = win4.shape
    tm = _row_tile(tp)
    pieces = _pieces(n_sh, csh, pw)

    seq, _ = _seq_rows(tp)
    last, meta_off = seq // tm, seq % tm
    assert last == tp // tm - 1
    assert any(rs.start <= meta_off and meta_off + N_META <= rs.stop for rs in _row_parts(tm))

    def body(dhp_ref, w_ref, h_ref, dh1_ref, g_ref, after_ref, gx_ref, dmeta_ref, dg1_ref):
        i = pl.program_id(0)

        @pl.when(i == 0)
        def _():
            dg1_ref[...] = jnp.zeros(dg1_ref.shape, F32)

        for rs in _row_parts(tm):
            dxn = None
            for k, klo, p, plo, w in pieces:
                t = _dot(dhp_ref[p, rs, plo:plo + w], w_ref[k, :, klo:klo + w], NT)
                dxn = t if dxn is None else dxn + t
            hh = h_ref[rs, :]
            r1 = lax.rsqrt(_mean(hh * hh) + RMS_EPS)
            dres, dg_rows = _rms_bwd(dxn, hh, r1, g_ref[...])
            dh = dh1_ref[rs, :] + dres
            gx_ref[rs, :] = dh
            dg1_ref[...] += jnp.sum(dg_rows, axis=0, keepdims=True)
            if rs.start <= meta_off and meta_off + N_META <= rs.stop:
                @pl.when(i == last)
                def _():
                    dmeta_ref[...] = dh[meta_off - rs.start:meta_off - rs.start + N_META, :]

    row = lambda i: (i, 0)
    return pl.pallas_call(
        body, name="mix_bwd_in", grid=(tp // tm,),
        in_specs=[pl.BlockSpec((n_p, tm, pw), lambda i: (0, i, 0)), _resident(win4.shape), pl.BlockSpec((tm, d), row),
                  pl.BlockSpec((tm, d), row), _full(g1.shape), ANY],
        out_specs=[pl.BlockSpec((tm, d), row), _full((N_META, d)), _full((1, d))],
        out_shape=[_sds((seq, d), F32), _sds((N_META, d), F32), _sds((1, d), F32)],
        compiler_params=_params(("arbitrary",)),
    )(dhp5, win4, h, dh1, g1, after)


def _other_chips(x, y):
    out = []
    for j in (1, 2, 3):
        px, py = _flip(x, j >> 1), _flip(y, j & 1)
        out.append((px, py, 2 * px + py))
    return out


PAIR_COLLECTIVE_ID = 0


def _pair_barrier(x, y, c):
    sem = pltpu.get_barrier_semaphore()
    pl.semaphore_signal(sem, inc=1, device_id=(x, y, 1 - c), device_id_type=MESH)
    pl.semaphore_wait(sem, 1)


def _pair_params():
    return pltpu.CompilerParams(collective_id=PAIR_COLLECTIVE_ID)


def _half_rows(c, rows_half):
    return pl.ds(pl.multiple_of(c * rows_half, 8), rows_half)


def _cast_place(w, q_arr, tag, after=None):
    rows, cols = w.shape
    tr = _row_tile(rows)
    extra = [] if after is None else [after]

    def body(q_ref, w_ref, *rest):
        rest[-1][0] = w_ref[...].astype(BF16)

    return pl.pallas_call(
        body, name="cast_place_" + tag,
        grid_spec=pltpu.PrefetchScalarGridSpec(
            num_scalar_prefetch=1, grid=(rows // tr,),
            in_specs=[pl.BlockSpec((tr, cols), lambda i, q: (i, 0))] + [ANY] * len(extra),
            out_specs=pl.BlockSpec((1, tr, cols), lambda i, q: (q[0], i, 0))),
        out_shape=_sds((N_CHIPS, rows, cols), BF16),
        compiler_params=_params(("arbitrary",)),
    )(q_arr, w, *extra)


def _gather_shards(fulls):
    n = len(fulls)
    halves = [a.shape[1] // 2 for a in fulls]

    def body(*refs):
        full = refs[n:2 * n]
        ssem, rsem = refs[2 * n:]
        x, y, c = _mesh_pos()
        chips = _other_chips(x, y)

        def remote(i, chip_no, half, to, s):
            part = full[i].at[chip_no, _half_rows(half, halves[i]), :]
            return pltpu.make_async_remote_copy(src_ref=part, dst_ref=part, send_sem=ssem.at[s], recv_sem=rsem.at[s],
                                                device_id=to, device_id_type=MESH)

        first = []
        for i in range(n):
            for j, (px, py, _) in enumerate(chips):
                cp = remote(i, 2 * x + y, c, (px, py, c), 3 * i + j)
                cp.start()
                first.append(cp)
        passed = []
        for i in range(n):
            for j, (_, _, qj) in enumerate(chips):
                remote(i, qj, c, (x, y, c), 3 * i + j).wait_recv()
                cp = remote(i, qj, c, (x, y, 1 - c), 3 * n + 3 * i + j)
                cp.start()
                passed.append(cp)
        for i in range(n):
            for j, (_, _, qj) in enumerate(chips):
                remote(i, qj, 1 - c, (x, y, c), 3 * n + 3 * i + j).wait_recv()
        for cp in first + passed:
            cp.wait_send()

    return pl.pallas_call(
        body, name="gather_shards",
        in_specs=[ANY] * n, out_specs=[ANY] * n,
        out_shape=[_sds(a.shape, a.dtype) for a in fulls],
        input_output_aliases={i: i for i in range(n)},
        scratch_shapes=[pltpu.SemaphoreType.DMA((6 * n,)), pltpu.SemaphoreType.DMA((6 * n,))],
    )(*fulls)


HBM = pl.BlockSpec(memory_space=pltpu.HBM)
SEM = pl.BlockSpec(memory_space=pltpu.SEMAPHORE)
EFFECT = pltpu.SideEffectType.DATAFLOW_SIDE_EFFECTING


def _in_hbm(a):
    return pltpu.with_memory_space_constraint(a, pltpu.HBM)


def _gather_start(fulls, after, tag):
    n = len(fulls)
    halves = [a.shape[1] // 2 for a in fulls]

    def body(*refs):
        land = refs[:n]
        ssem, rsem = refs[n + 1], refs[n + 2]
        token = refs[-1]
        x, y, c = _mesh_pos()
        q = 2 * x + y
        for i in range(n):
            for j, (px, py, _) in enumerate(_other_chips(x, y)):
                mine = land[i].at[q, _half_rows(c, halves[i]), :]
                pltpu.make_async_remote_copy(src_ref=mine, dst_ref=mine, send_sem=ssem.at[3 * i + j],
                                             recv_sem=rsem.at[3 * i + j], device_id=(px, py, c), device_id_type=MESH).start()
        token[...] = jnp.zeros(token.shape, F32)

    outs = pl.pallas_call(
        body, name="gather_start_" + tag,
        in_specs=[HBM] * n + [ANY], out_specs=[SEM, SEM] + [HBM] * n + [VMEM],
        out_shape=[pltpu.SemaphoreType.DMA((3 * n,)), pltpu.SemaphoreType.DMA((3 * n,))]
        + [pltpu.HBM(a.shape, a.dtype) for a in fulls] + [_sds((8, LANES), F32)],
        input_output_aliases={i: 2 + i for i in range(n)},
        compiler_params=pltpu.CompilerParams(has_side_effects=EFFECT),
    )(*[_in_hbm(a) for a in fulls], after)
    return outs[0], outs[1], list(outs[2:2 + n]), outs[-1]


def _gather_wait(which, ssem, rsem, lands, after, tag):
    m = len(which)
    halves = [a.shape[1] // 2 for a in lands]

    def body(*refs):
        land = refs[:m]
        ssem_, rsem_ = refs[m], refs[m + 1]
        x, y, c = _mesh_pos()
        for t, i in enumerate(which):
            for j, (px, py, qj) in enumerate(_other_chips(x, y)):
                rows = _half_rows(c, halves[t])
                cp = pltpu.make_async_remote_copy(src_ref=land[t].at[2 * x + y, rows, :], dst_ref=land[t].at[qj, rows, :],
                                                  send_sem=ssem_.at[3 * i + j], recv_sem=rsem_.at[3 * i + j],
                                                  device_id=(px, py, c), device_id_type=MESH)
                cp.wait_send()
                cp.wait_recv()

    outs = pl.pallas_call(
        body, name="gather_wait_" + tag,
        in_specs=[HBM] * m + [SEM, SEM, ANY], out_specs=[HBM] * m,
        out_shape=[pltpu.HBM(a.shape, a.dtype) for a in lands],
        input_output_aliases={i: i for i in range(m)},
        compiler_params=pltpu.CompilerParams(has_side_effects=EFFECT),
    )(*lands, ssem, rsem, after)
    return list(outs)


def _forward_pair(lands, tag):
    n = len(lands)
    halves = [a.shape[1] // 2 for a in lands]

    def body(*refs):
        full = refs[n:2 * n]
        ssem, rsem = refs[2 * n:]
        x, y, c = _mesh_pos()
        _pair_barrier(x, y, c)
        cps = []
        for i in range(n):
            for j, (_, _, qj) in enumerate(_other_chips(x, y)):
                part = full[i].at[qj, _half_rows(c, halves[i]), :]
                cp = pltpu.make_async_remote_copy(src_ref=part, dst_ref=part, send_sem=ssem.at[3 * i + j],
                                                  recv_sem=rsem.at[3 * i + j], device_id=(x, y, 1 - c), device_id_type=MESH)
                cp.start()
                cps.append(cp)
        for cp in cps:
            cp.wait()

    return pl.pallas_call(
        body, name="forward_pair_" + tag,
        in_specs=[ANY] * n, out_specs=[ANY] * n,
        out_shape=[_sds(a.shape, a.dtype) for a in lands],
        input_output_aliases={i: i for i in range(n)},
        scratch_shapes=[pltpu.SemaphoreType.DMA((3 * n,)), pltpu.SemaphoreType.DMA((3 * n,))],
        compiler_params=_pair_params(),
    )(*lands)


def _chip_exchange_start(parts, after, tag):
    n = len(parts)

    def body(*refs):
        src, land = refs[:n], refs[n:2 * n]
        ssem, rsem = refs[2 * n + 1], refs[2 * n + 2]
        token = refs[-1]
        x, y, c = _mesh_pos()
        for i in range(n):
            for j, (px, py, qj) in enumerate(_other_chips(x, y)):
                pltpu.make_async_remote_copy(src_ref=src[i].at[qj], dst_ref=land[i].at[j], send_sem=ssem.at[3 * i + j],
                                             recv_sem=rsem.at[3 * i + j], device_id=(px, py, c), device_id_type=MESH).start()
        token[...] = jnp.zeros(token.shape, F32)

    lands = [lax.empty((3,) + a.shape[1:], a.dtype) for a in parts]
    outs = pl.pallas_call(
        body, name="chip_exchange_start_" + tag,
        in_specs=[HBM] * (2 * n) + [ANY], out_specs=[SEM, SEM] + [HBM] * (2 * n) + [VMEM],
        out_shape=[pltpu.SemaphoreType.DMA((3 * n,)), pltpu.SemaphoreType.DMA((3 * n,))]
        + [pltpu.HBM(a.shape, a.dtype) for a in parts] + [pltpu.HBM(a.shape, a.dtype) for a in lands]
        + [_sds((8, LANES), F32)],
        input_output_aliases={i: 2 + i for i in range(2 * n)},
        compiler_params=pltpu.CompilerParams(has_side_effects=EFFECT),
    )(*[_in_hbm(a) for a in parts], *[_in_hbm(a) for a in lands], after)
    return outs[0], outs[1], list(outs[2:2 + n]), list(outs[2 + n:2 + 2 * n]), outs[-1]


def _chip_exchange_wait(ssem, rsem, parts, lands, after, tag):
    n = len(parts)

    def body(*refs):
        src, land = refs[:n], refs[n:2 * n]
        ssem_, rsem_ = refs[2 * n], refs[2 * n + 1]
        x, y, c = _mesh_pos()
        for i in range(n):
            for j, (px, py, qj) in enumerate(_other_chips(x, y)):
                cp = pltpu.make_async_remote_copy(src_ref=src[i].at[qj], dst_ref=land[i].at[j], send_sem=ssem_.at[3 * i + j],
                                                  recv_sem=rsem_.at[3 * i + j], device_id=(px, py, c), device_id_type=MESH)
                cp.wait_send()
                cp.wait_recv()

    outs = pl.pallas_call(
        body, name="chip_exchange_wait_" + tag,
        in_specs=[HBM] * (2 * n) + [SEM, SEM, ANY], out_specs=[HBM] * (2 * n),
        out_shape=[pltpu.HBM(a.shape, a.dtype) for a in parts] + [pltpu.HBM(a.shape, a.dtype) for a in lands],
        input_output_aliases={i: i for i in range(2 * n)},
        compiler_params=pltpu.CompilerParams(has_side_effects=EFFECT),
    )(*parts, *lands, ssem, rsem, after)
    return list(outs[:n]), list(outs[n:])


def _grad_half(ref, shape, axis, which):
    rows = shape[axis] // 2
    if axis == 0:
        return ref.at[_half_rows(which, rows), :]
    return ref.at[:, _half_rows(which, rows), :]


def _half_shape(a, axis):
    s = list(a.shape)
    s[axis] //= 2
    return tuple(s)


def _pair_exchange_start(grads, half_axis, after, tag):
    n = len(grads)

    def body(*refs):
        g, land = refs[:n], refs[n:2 * n]
        ssem, rsem = refs[2 * n + 1], refs[2 * n + 2]
        token = refs[-1]
        x, y, c = _mesh_pos()
        for i in range(n):
            pltpu.make_async_remote_copy(src_ref=_grad_half(g[i], grads[i].shape, half_axis[i], 1 - c), dst_ref=land[i],
                                         send_sem=ssem.at[i], recv_sem=rsem.at[i], device_id=(x, y, 1 - c),
                                         device_id_type=MESH).start()
        token[...] = jnp.zeros(token.shape, F32)

    lands = [lax.empty(_half_shape(a, half_axis[i]), a.dtype) for i, a in enumerate(grads)]
    outs = pl.pallas_call(
        body, name="pair_exchange_start_" + tag,
        in_specs=[HBM] * (2 * n) + [ANY], out_specs=[SEM, SEM] + [HBM] * (2 * n) + [VMEM],
        out_shape=[pltpu.SemaphoreType.DMA((n,)), pltpu.SemaphoreType.DMA((n,))]
        + [pltpu.HBM(a.shape, a.dtype) for a in grads] + [pltpu.HBM(a.shape, a.dtype) for a in lands]
        + [_sds((8, LANES), F32)],
        input_output_aliases={i: 2 + i for i in range(2 * n)},
        compiler_params=pltpu.CompilerParams(has_side_effects=EFFECT),
    )(*[_in_hbm(a) for a in grads], *[_in_hbm(a) for a in lands], after)
    return outs[0], outs[1], list(outs[2:2 + n]), list(outs[2 + n:2 + 2 * n]), outs[-1]


def _pair_exchange_wait(ssem, rsem, grads, lands, half_axis, after, tag):
    n = len(grads)

    def body(*refs):
        g, land = refs[:n], refs[n:2 * n]
        ssem_, rsem_ = refs[2 * n], refs[2 * n + 1]
        x, y, c = _mesh_pos()
        for i in range(n):
            cp = pltpu.make_async_remote_copy(src_ref=_grad_half(g[i], grads[i].shape, half_axis[i], 1 - c),
                                              dst_ref=land[i], send_sem=ssem_.at[i], recv_sem=rsem_.at[i],
                                              device_id=(x, y, 1 - c), device_id_type=MESH)
            cp.wait_send()
            cp.wait_recv()

    outs = pl.pallas_call(
        body, name="pair_exchange_wait_" + tag,
        in_specs=[HBM] * (2 * n) + [SEM, SEM, ANY], out_specs=[HBM] * (2 * n),
        out_shape=[pltpu.HBM(a.shape, a.dtype) for a in grads] + [pltpu.HBM(a.shape, a.dtype) for a in lands],
        input_output_aliases={i: i for i in range(2 * n)},
        compiler_params=pltpu.CompilerParams(has_side_effects=EFFECT),
    )(*grads, *lands, ssem, rsem, after)
    return list(outs[:n]), list(outs[n:])


def _pair_exchange_grads(grads, half_axis, tag):
    n = len(grads)

    def half_of(ref, i, which):
        rows = grads[i].shape[half_axis[i]] // 2
        if half_axis[i] == 0:
            return ref.at[_half_rows(which, rows), :]
        return ref.at[:, _half_rows(which, rows), :]

    def out_shape(i):
        s = list(grads[i].shape)
        s[half_axis[i]] //= 2
        return _sds(tuple(s), grads[i].dtype)

    def body(*refs):
        g, got = refs[:n], refs[n:2 * n]
        ssem, rsem = refs[2 * n:]
        x, y, c = _mesh_pos()
        _pair_barrier(x, y, c)
        cps = []
        for i in range(n):
            cp = pltpu.make_async_remote_copy(src_ref=half_of(g[i], i, 1 - c), dst_ref=got[i], send_sem=ssem.at[i],
                                              recv_sem=rsem.at[i], device_id=(x, y, 1 - c), device_id_type=MESH)
            cp.start()
            cps.append(cp)
        for cp in cps:
            cp.wait()

    return pl.pallas_call(
        body, name="pair_exchange_grads_" + tag,
        in_specs=[ANY] * n, out_specs=[ANY] * n,
        out_shape=[out_shape(i) for i in range(n)],
        scratch_shapes=[pltpu.SemaphoreType.DMA((n,)), pltpu.SemaphoreType.DMA((n,))],
        compiler_params=_pair_params(),
    )(*grads)


def _pair_sum(gs, gots, c_arr, col_sharded, tag):
    n = len(gs)
    if col_sharded:
        rows, cols = gs[0].shape
        rh, cs = rows // 2, cols // N_CHIPS
        g_spec = pl.BlockSpec((rh, cs), lambda k, c_ref: (c_ref[0], k))
        got_spec = pl.BlockSpec((rh, cs), lambda k, c_ref: (0, k))
    else:
        _, rows, cs = gs[0].shape
        rh = rows // 2
        g_spec = pl.BlockSpec((1, rh, cs), lambda k, c_ref: (k, c_ref[0], 0))
        got_spec = pl.BlockSpec((1, rh, cs), lambda k, c_ref: (k, 0, 0))

    def body(c_ref, *refs):
        for g_ref, got_ref, out_ref in zip(refs[:n], refs[n:2 * n], refs[2 * n:]):
            total = g_ref[...].astype(F32) + got_ref[...].astype(F32)
            out_ref[...] = total.astype(BF16).reshape(out_ref.shape)

    return list(pl.pallas_call(
        body, name="pair_sum_" + tag,
        grid_spec=pltpu.PrefetchScalarGridSpec(
            num_scalar_prefetch=1, grid=(N_CHIPS,), in_specs=[g_spec] * n + [got_spec] * n,
            out_specs=[pl.BlockSpec((1, rh, cs), lambda k, c_ref: (k, 0, 0))] * n),
        out_shape=[_sds((N_CHIPS, rh, cs), BF16)] * n,
        compiler_params=_params(("arbitrary",)),
    )(c_arr, *gs, *gots))


def _chip_sum(parts, gots, qc_arr, tag):
    n = len(parts)
    _, rh, cs = parts[0].shape
    steps = 2 if rh % 32 == 0 else 1
    rb = rh // steps

    def body(qc_ref, *refs):
        for part_ref, got_ref, out_ref in zip(refs[:n], refs[n:2 * n], refs[2 * n:]):
            total = part_ref[0].astype(F32)
            for j in range(3):
                total = total + got_ref[j].astype(F32)
            out_ref[...] = total

    return list(pl.pallas_call(
        body, name="chip_sum_" + tag,
        grid_spec=pltpu.PrefetchScalarGridSpec(
            num_scalar_prefetch=1, grid=(steps,),
            in_specs=[pl.BlockSpec((1, rb, cs), lambda i, qc: (qc[0], i, 0))] * n
            + [pl.BlockSpec((3, rb, cs), lambda i, qc: (0, i, 0))] * n,
            out_specs=[pl.BlockSpec((rb, cs), lambda i, qc: (qc[1] * steps + i, 0))] * n),
        out_shape=[_sds((2 * rh, cs), F32)] * n,
        compiler_params=_params(("arbitrary",)),
    )(qc_arr, *parts, *gots))


def _pair_share_grads(grads, tag):
    n = len(grads)

    def body(*refs):
        g = refs[n:2 * n]
        ssem, rsem = refs[2 * n:]
        x, y, c = _mesh_pos()
        _pair_barrier(x, y, c)
        cps = []
        for i in range(n):
            mine = g[i].at[_half_rows(c, grads[i].shape[0] // 2), :]
            cp = pltpu.make_async_remote_copy(src_ref=mine, dst_ref=mine, send_sem=ssem.at[i], recv_sem=rsem.at[i],
                                              device_id=(x, y, 1 - c), device_id_type=MESH)
            cp.start()
            cps.append(cp)
        for cp in cps:
            cp.wait()

    return pl.pallas_call(
        body, name="pair_share_grads_" + tag,
        in_specs=[ANY] * n, out_specs=[ANY] * n,
        out_shape=[_sds(a.shape, a.dtype) for a in grads],
        input_output_aliases={i: i for i in range(n)},
        scratch_shapes=[pltpu.SemaphoreType.DMA((n,)), pltpu.SemaphoreType.DMA((n,))],
        compiler_params=_pair_params(),
    )(*grads)


def _small_allreduce(parts, places, rows_total, width, after):
    n = len(parts)

    def body(*refs):
        ins, out_ref = refs[:n], refs[n + 1]
        pack, pair_got, chip_sum, got, ssem, rsem = refs[n + 2:]
        x, y, c = _mesh_pos()
        chip = 2 * x + y
        pack[...] = jnp.zeros(pack.shape, F32)
        for i in range(n):
            for row, col, src_row, rows in places[i]:
                w = parts[i].shape[1]
                pack[row:row + rows, col:col + w] = ins[i][src_row:src_row + rows, :]
        swap = pltpu.make_async_remote_copy(src_ref=pack, dst_ref=pair_got, send_sem=ssem.at[3], recv_sem=rsem.at[3],
                                            device_id=(x, y, 1 - c), device_id_type=MESH)
        swap.start()
        swap.wait()
        chip_sum[...] = pack[...] + pair_got[...]
        cps = []
        for j, (px, py, _) in enumerate(_other_chips(x, y)):
            cp = pltpu.make_async_remote_copy(src_ref=chip_sum, dst_ref=got.at[j], send_sem=ssem.at[j],
                                              recv_sem=rsem.at[j], device_id=(px, py, c), device_id_type=MESH)
            cp.start()
            cps.append(cp)
        for cp in cps:
            cp.wait()
        total = jnp.zeros(pack.shape, F32)
        for q in range(N_CHIPS):
            rel = jnp.bitwise_xor(chip, q)
            theirs = got[jnp.maximum(rel - 1, 0)]
            total = total + jnp.where(rel == 0, chip_sum[...], theirs)
        out_ref[...] = total

    return pl.pallas_call(
        body, name="small_allreduce",
        in_specs=[VMEM] * n + [ANY], out_specs=VMEM,
        out_shape=_sds((rows_total, width), F32),
        scratch_shapes=[pltpu.VMEM((rows_total, width), F32), pltpu.VMEM((rows_total, width), F32),
                        pltpu.VMEM((rows_total, width), F32), pltpu.VMEM((3, rows_total, width), F32),
                        pltpu.SemaphoreType.DMA((4,)), pltpu.SemaphoreType.DMA((4,))],
        compiler_params=_params(),
    )(*parts, after)


def _small_update(red, q_arr, takes, loss_at, ws, ms, vs):
    n_w = len(ws)

    def body(q_ref, red_ref, *refs):
        w_in, m_in, v_in = refs[0:n_w], refs[n_w:2 * n_w], refs[2 * n_w:3 * n_w]
        outs = refs[3 * n_w:]
        g_out, d_out, m_out, v_out = (outs[0:n_w], outs[n_w:2 * n_w], outs[2 * n_w:3 * n_w], outs[3 * n_w:4 * n_w])
        loss_ref = outs[4 * n_w]
        chip = q_ref[0]

        def take_own_columns(g_ref, d0, nr, s0, c0, w):
            for k in range(N_CHIPS):
                @pl.when(chip == k)
                def _():
                    g_ref[d0:d0 + nr, :] = red_ref[s0:s0 + nr, c0 + k * w:c0 + (k + 1) * w]

        for j in range(n_w):
            w = ws[j].shape[1]
            for d0, nr, s0, c0, sharded in takes[j]:
                if sharded:
                    take_own_columns(g_out[j], d0, nr, s0, c0, w)
                else:
                    g_out[j][d0:d0 + nr, :] = red_ref[s0:s0 + nr, c0:c0 + w]
            d_out[j][...], m_out[j][...], v_out[j][...] = _adamw_math(w_in[j][...], g_out[j][...], m_in[j][...], v_in[j][...])
        loss_ref[...] = red_ref[loss_at[0]:loss_at[0] + 1, loss_at[1]:loss_at[1] + LANES]

    shapes = [_sds(w.shape, F32) for w in ws]
    outs = pl.pallas_call(
        body, name="small_update",
        in_specs=[pl.BlockSpec(memory_space=pltpu.SMEM)] + [VMEM] * (1 + 3 * n_w), out_specs=[VMEM] * (4 * n_w + 1),
        out_shape=shapes * 4 + [_sds((1, LANES), F32)],
        compiler_params=_params(),
    )(q_arr, red, *ws, *ms, *vs)
    return outs[0:n_w], outs[n_w:2 * n_w], outs[2 * n_w:3 * n_w], outs[3 * n_w:4 * n_w], outs[4 * n_w]


def _adamw_math(w, g, m, v):
    m2 = ADAM_B1 * m + (1.0 - ADAM_B1) * g
    v2 = ADAM_B2 * v + (1.0 - ADAM_B2) * (g * g)
    m_hat = m2 / (1.0 - ADAM_B1 ** ADAM_STEP)
    v_hat = v2 / (1.0 - ADAM_B2 ** ADAM_STEP)
    delta = -ADAM_LR * (m_hat / (jnp.sqrt(v_hat) + ADAM_EPS) + ADAM_WD * w)
    return delta, m2, v2


ADAMW_BLOCK_BYTES = 3 * 2 ** 19


def _adamw_big(ws, gs, ms, vs, tag):
    n = len(ws)
    rows, cols = ws[0].shape
    tr = 16
    for t in range(16, rows + 1, 16):
        if rows % t == 0 and t * cols * 4 * n <= ADAMW_BLOCK_BYTES:
            tr = t

    def body(*refs):
        ins, outs = refs[:4 * n], refs[4 * n:]
        for i in range(n):
            w_ref, g_ref, m_ref, v_ref = ins[i], ins[n + i], ins[2 * n + i], ins[3 * n + i]
            gg = g_ref[...]
            outs[4 * i][...] = gg
            outs[4 * i + 1][...], outs[4 * i + 2][...], outs[4 * i + 3][...] = _adamw_math(
                w_ref[...], gg, m_ref[...], v_ref[...])

    spec = pl.BlockSpec((tr, cols), lambda i: (i, 0))
    outs = pl.pallas_call(
        body, name="adamw_" + tag, grid=(rows // tr,),
        in_specs=[spec] * (4 * n), out_specs=[spec] * (4 * n),
        out_shape=[_sds((rows, cols), F32)] * (4 * n),
        compiler_params=_params(("arbitrary",)),
    )(*ws, *gs, *ms, *vs)
    return [outs[4 * i:4 * i + 4] for i in range(n)]


SMALL_ROWS = 40
PACK_ROWS = 64


def kernel(x, meta_tokens, pre_mix_norm, w_in, conv_a_w, conv_b_w, conv_b_bias, ln_b_gain, ln_b_bias, w_out, post_mix_norm, pre_ffn_norm, w_gate, w_up, w_down, post_ffn_norm, loss_target, m_meta_tokens, m_pre_mix_norm, m_w_in, m_conv_a_w, m_conv_b_w, m_conv_b_bias, m_ln_b_gain, m_ln_b_bias, m_w_out, m_post_mix_norm, m_pre_ffn_norm, m_w_gate, m_w_up, m_w_down, m_post_ffn_norm, v_meta_tokens, v_pre_mix_norm, v_w_in, v_conv_a_w, v_conv_b_w, v_conv_b_bias, v_ln_b_gain, v_ln_b_bias, v_w_out, v_post_mix_norm, v_pre_ffn_norm, v_w_gate, v_w_up, v_w_down, v_post_ffn_norm):
    xq, yq, cq = lax.axis_index("x"), lax.axis_index("y"), lax.axis_index("c")
    chip = 2 * xq + yq
    c_arr = jnp.reshape(cq, (1,)).astype(jnp.int32)
    qc_arr = jnp.stack([chip, cq]).astype(jnp.int32)

    seq, d = x.shape[1], x.shape[2]
    x2, tgt2 = x[0], loss_target[0]
    tr = lambda a: jnp.swapaxes(a, 1, 2)[0]
    w_in2, w_out2, w_gate2, w_up2, w_down2 = w_in[0], w_out[0], tr(w_gate), tr(w_up), w_down[0]
    ka, wa_sh = conv_a_w.shape[1], conv_a_w.shape[2]
    kb = conv_b_w.shape[1]
    meta_sh = meta_tokens.shape[1]

    small = jnp.zeros((PACK_ROWS, meta_sh), F32)
    small = small.at[0:N_META, :].set(meta_tokens)
    small = small.at[16:16 + ka, 0:wa_sh].set(conv_a_w[0])
    small = small.at[24:24 + kb, 0:wa_sh].set(conv_b_w[0])
    q_arr = jnp.reshape(chip, (1,)).astype(jnp.int32)
    small_own = lax.dynamic_update_slice(jnp.zeros((N_CHIPS, PACK_ROWS, meta_sh), F32), small[None], (chip, 0, 0))
    i_ssem, i_rsem, first, i_token = _gather_start([_cast_place(w_in2, q_arr, "w_in"), small_own], pre_mix_norm, "in")
    rest = [_cast_place(w, q_arr, nm, i_token)
            for w, nm in ((w_out2, "w_out"), (w_gate2, "w_gate"), (w_up2, "w_up"), (w_down2, "w_down"))]
    g_ssem, g_rsem, lands, g_token = _gather_start(rest, i_token, "rest")
    win4, small4 = _forward_pair(_gather_wait([0, 1], i_ssem, i_rsem, first, g_token, "in"), "in")
    meta_f = jnp.concatenate([small4[k, 0:N_META, :] for k in range(N_CHIPS)], axis=1)
    wa_f = jnp.concatenate([small4[k, 16:16 + ka, 0:wa_sh] for k in range(N_CHIPS)], axis=1)
    wb_f = jnp.concatenate([small4[k, 24:24 + kb, 0:wa_sh] for k in range(N_CHIPS)], axis=1)

    tm = _row_tile(seq + TAIL_ROWS)
    tail = lax.dynamic_update_slice(jnp.zeros((tm, d), F32), meta_f, (seq % tm, 0))
    h, xn1, hp5 = _mm_in(x2, tail, win4, pre_mix_norm, g_token)
    ya, z = _mix_conv_fwd(hp5, wa_f, wb_f, conv_b_bias)
    (wout4,) = _forward_pair(_gather_wait([0], g_ssem, g_rsem, lands[0:1], z, "out"), "out")
    wout_f = wout4.reshape(N_CHIPS * wout4.shape[1], wout4.shape[2])
    yb, mix, h1, xn2 = _mm_out(ya, z, h, wout_f, ln_b_gain, ln_b_bias, post_mix_norm, pre_ffn_norm)
    wg4, wu4 = _forward_pair(_gather_wait([1, 2], g_ssem, g_rsem, lands[1:3], xn2, "gate_up"), "gate_up")
    stacked = lambda a: a.reshape(a.shape[0] * a.shape[1], a.shape[2])
    wg_f, wu_f = stacked(wg4), stacked(wu4)
    p_act, q_act, f_act = _ffn_up(xn2, wg_f, wu_f)
    (wd4,) = _forward_pair(_gather_wait([3], g_ssem, g_rsem, lands[3:4], f_act, "down"), "down")
    wd_f = stacked(wd4)
    dff, dh2, loss_blk, d_gpf = _ffn_down(f_act, wd_f, h1, tgt2, post_ffn_norm)

    da, du = _ffn_bwd_act(dff, wd_f, p_act, q_act)
    by_chip = lambda g: g.reshape(N_CHIPS, g.shape[0] // N_CHIPS, g.shape[1])
    g_down = by_chip(_grad_w_down(f_act, dff))
    g_gate, g_up = [by_chip(g) for g in _grad_w_gate_up(xn2, da, du)]
    ffn = [g_gate, g_up, g_down]
    p_ssem, p_rsem, ffn, p_lands, p_token = _pair_exchange_start(ffn, [1, 1, 1], dff, "ffn")
    dh1, dmix, d_g2, d_gpm = _ffn_bwd_in(da, du, wg_f, wu_f, h1, mix, dh2, pre_ffn_norm, post_mix_norm, p_token)
    ffn, got = _pair_exchange_wait(p_ssem, p_rsem, ffn, p_lands, [1, 1, 1], d_g2, "ffn")
    parts = _pair_sum(ffn, got, c_arr, False, "ffn")
    f_ssem, f_rsem, parts, f_lands, f_token = _chip_exchange_start(parts, dff, "ffn")
    g_out = _grad_w_out(ya, yb, dmix, f_token)
    dya, dz, d_lg, d_lb = _mix_bwd_out(dmix, wout_f, z, ln_b_gain, ln_b_bias, f_token)
    dhp5, d_wa, d_wb, d_bb = _mix_conv_bwd(hp5, dya, dz, wa_f, wb_f)
    g_in = _grad_w_in(xn1, dhp5)

    g_out4 = g_out.reshape(N_CHIPS, g_out.shape[0] // N_CHIPS, g_out.shape[1])
    mixw = [g_in, g_out4]
    got2 = _pair_exchange_grads(mixw, [0, 1], "mix")
    parts2 = _pair_sum(mixw[0:1], got2[0:1], c_arr, True, "in") + _pair_sum(mixw[1:2], got2[1:2], c_arr, False, "out")
    m_ssem, m_rsem, parts2, m_lands, m_token = _chip_exchange_start(parts2, dhp5, "mix")
    grad_x2, d_meta, d_g1 = _mix_bwd_in(dhp5, win4, h, dh1, pre_mix_norm, m_token)
    grad_x = grad_x2[None]

    parts, f_recv = _chip_exchange_wait(f_ssem, f_rsem, parts, f_lands, d_g1, "ffn")
    halves = _chip_sum(parts, f_recv, qc_arr, "ffn")
    gsum_ffn = _pair_share_grads(halves, "ffn")

    names_big = ["w_in", "w_out", "w_gate", "w_up", "w_down"]
    w_big = dict(zip(names_big, [w_in2, w_out2, w_gate2, w_up2, w_down2]))
    m_big = dict(zip(names_big, [m_w_in[0], m_w_out[0], tr(m_w_gate), tr(m_w_up), m_w_down[0]]))
    v_big = dict(zip(names_big, [v_w_in[0], v_w_out[0], tr(v_w_gate), tr(v_w_up), v_w_down[0]]))
    grads, deltas, new_m, new_v = {}, {}, {}, {}

    def update(names, gs, tag):
        res = _adamw_big([w_big[k] for k in names], gs, [m_big[k] for k in names], [v_big[k] for k in names], tag)
        for nm, outs in zip(names, res):
            if nm in ("w_gate", "w_up"):
                outs = [jnp.swapaxes(o[None], 1, 2) for o in outs]
            else:
                outs = [o[None] for o in outs]
            grads[nm], deltas[nm], new_m[nm], new_v[nm] = outs
        return res[-1][1]

    last = update(["w_gate", "w_up", "w_down"], list(gsum_ffn), "ffn")

    hw = d // 2
    assert d_wa.shape == (3, hw) and d_wb.shape == (31, hw) and d_bb.shape == (1, hw)
    small_parts = [d_meta, d_g1, d_gpm, d_g2, d_gpf, d_bb, d_lg, d_lb, loss_blk[0:1, :], d_wa, d_wb]
    places = [[(0, 0, 0, N_META)], [(16, 0, 0, 1)], [(17, 0, 0, 1)], [(18, 0, 0, 1)], [(19, 0, 0, 1)],
              [(20, 0, 0, 1)], [(20, hw, 0, 1)], [(21, 0, 0, 1)], [(21, hw, 0, 1)], [(22, 0, 0, 3)],
              [(22, hw, 0, 3), (25, 0, 3, 14), (25, hw, 17, 14)]]
    names_small = ["meta_tokens", "pre_mix_norm", "conv_a_w", "conv_b_w", "conv_b_bias", "ln_b_gain", "ln_b_bias",
                   "post_mix_norm", "pre_ffn_norm", "post_ffn_norm"]
    takes = [[(0, N_META, 0, 0, True)], [(0, 1, 16, 0, False)], [(0, 3, 22, 0, True)],
             [(0, 3, 22, hw, True), (3, 14, 25, 0, True), (17, 14, 25, hw, True)], [(0, 1, 20, 0, False)],
             [(0, 1, 20, hw, False)], [(0, 1, 21, 0, False)], [(0, 1, 17, 0, False)], [(0, 1, 18, 0, False)],
             [(0, 1, 19, 0, False)]]
    w_small = [meta_tokens, pre_mix_norm, conv_a_w[0], conv_b_w[0], conv_b_bias, ln_b_gain, ln_b_bias, post_mix_norm,
               pre_ffn_norm, post_ffn_norm]
    m_small = [m_meta_tokens, m_pre_mix_norm, m_conv_a_w[0], m_conv_b_w[0], m_conv_b_bias, m_ln_b_gain, m_ln_b_bias,
               m_post_mix_norm, m_pre_ffn_norm, m_post_ffn_norm]
    v_small = [v_meta_tokens, v_pre_mix_norm, v_conv_a_w[0], v_conv_b_w[0], v_conv_b_bias, v_ln_b_gain, v_ln_b_bias,
               v_post_mix_norm, v_pre_ffn_norm, v_post_ffn_norm]
    red = _small_allreduce(small_parts, places, SMALL_ROWS, d, last)
    g_s, d_s, m_s, v_s, loss_row = _small_update(red, q_arr, takes, (21, hw), w_small, m_small, v_small)
    loss = loss_row[0, 0]
    for i, nm in enumerate(names_small):
        lead = nm in ("conv_a_w", "conv_b_w")
        fix = (lambda a: a[None]) if lead else (lambda a: a)
        grads[nm], deltas[nm], new_m[nm], new_v[nm] = fix(g_s[i]), fix(d_s[i]), fix(m_s[i]), fix(v_s[i])

    parts2, m_recv = _chip_exchange_wait(m_ssem, m_rsem, parts2, m_lands, loss_row, "mix")
    halves2 = _chip_sum(parts2[0:1], m_recv[0:1], qc_arr, "in") + _chip_sum(parts2[1:2], m_recv[1:2], qc_arr, "out")
    gsum_mix = _pair_share_grads(halves2, "mix")
    update(["w_in"], [gsum_mix[0]], "w_in")
    update(["w_out"], [gsum_mix[1]], "w_out")

    order = ["meta_tokens", "pre_mix_norm", "w_in", "conv_a_w", "conv_b_w", "conv_b_bias", "ln_b_gain", "ln_b_bias", "w_out",
             "post_mix_norm", "pre_ffn_norm", "w_gate", "w_up", "w_down", "post_ffn_norm"]
    return (loss, grad_x, *[grads[k] for k in order], *[deltas[k] for k in order], *[new_m[k] for k in order],
            *[new_v[k] for k in order])
```

```python
import jax
import jax.numpy as jnp
from jax import lax
from jax.experimental import pallas as pl
from jax.experimental.pallas import tpu as pltpu

F32 = jnp.float32
BF16 = jnp.bfloat16
MESH = pl.DeviceIdType.MESH

N_META = 16
TAIL_ROWS = 128
RMS_EPS = 1e-6
LN_EPS = 1e-5
ADAM_LR = 0.001
ADAM_B1 = 0.9
ADAM_B2 = 0.999
ADAM_EPS = 1e-08
ADAM_WD = 0.01
ADAM_STEP = 10

N_CHIPS = 4
LANES = 128
MXU_TILE = 256
CONV_CHUNK = 48
CONV_HIST = 32
ROW_TILE_CAP = 640
VMEM_LIMIT = 56 * 1024 * 1024

NN = (((1,), (0,)), ((), ()))
NT = (((1,), (1,)), ((), ()))
TN = (((0,), (0,)), ((), ()))


def _dot(a, b, dims=NN):
    return lax.dot_general(a, b, dims, preferred_element_type=F32)


def _sig(v):
    return 1.0 / (1.0 + jnp.exp(-v))


def _mean(v):
    return jnp.mean(v, axis=-1, keepdims=True)


def _row_tile(rows):
    best = 16
    for t in range(16, min(rows, ROW_TILE_CAP) + 1, 16):
        if rows % t == 0:
            best = t
    assert rows % best == 0
    return best


def _row_parts(tm):
    if tm % 32:
        return [slice(0, tm)]
    return [slice(0, tm // 2), slice(tm // 2, tm)]


def _concat_shards(w_ref, wcat_ref):
    n_sh, _, csh = w_ref.shape

    @pl.when(pl.program_id(0) == 0)
    def _():
        for k in range(n_sh):
            wcat_ref[:, k * csh:(k + 1) * csh] = w_ref[k]


def _params(semantics=None):
    kw = dict(vmem_limit_bytes=VMEM_LIMIT)
    if semantics is not None:
        kw["dimension_semantics"] = semantics
    return pltpu.CompilerParams(**kw)


def _full(shape):
    nd = len(shape)
    return pl.BlockSpec(shape, lambda *_: (0,) * nd)


def _resident(shape):
    nd = len(shape)
    return pl.BlockSpec(shape, lambda *_: (0,) * nd, pipeline_mode=pl.Buffered(1))


def _sds(shape, dtype):
    return jax.ShapeDtypeStruct(shape, dtype)


ANY = pl.BlockSpec(memory_space=pl.ANY)
VMEM = pl.BlockSpec(memory_space=pltpu.VMEM)


def _mesh_pos():
    return lax.axis_index("x"), lax.axis_index("y"), lax.axis_index("c")


def _flip(v, bit):
    return 1 - v if bit else v


def _mm_in(x, tail, win4, g1, after):
    seq, d = x.shape
    tp = seq + TAIL_ROWS
    tm = _row_tile(tp)
    n_sh, _, csh = win4.shape
    pw = n_sh * csh // 5

    def body(x_ref, tail_ref, w_ref, g_ref, after_ref, h_ref, xn_ref, hp_ref, wcat_ref):
        _concat_shards(w_ref, wcat_ref)
        rows = pl.program_id(0) * tm + lax.broadcasted_iota(jnp.int32, (tm, 1), 0)
        hh = jnp.where(rows < seq, x_ref[...], tail_ref[...])
        h_ref[...] = hh
        r = lax.rsqrt(_mean(hh * hh) + RMS_EPS)
        xn = (hh * r * g_ref[...]).astype(BF16)
        xn_ref[...] = xn
        for p in range(5):
            hp_ref[p] = _dot(xn, wcat_ref[:, p * pw:(p + 1) * pw])

    row = pl.BlockSpec((tm, d), lambda i: (i, 0))
    return pl.pallas_call(
        body, name="mm_in", grid=(tp // tm,),
        in_specs=[row, _full(tail.shape), _resident(win4.shape), _full(g1.shape), ANY],
        out_specs=[row, row, pl.BlockSpec((5, tm, pw), lambda i: (0, i, 0))],
        out_shape=[_sds((tp, d), F32), _sds((tp, d), BF16), _sds((5, tp, pw), F32)],
        scratch_shapes=[pltpu.VMEM((d, n_sh * csh), BF16)],
        compiler_params=_params(("arbitrary",)),
    )(x, tail, win4, g1, after)


def _seq_rows(tp):
    seq = tp - TAIL_ROWS
    nseq = seq + N_META
    assert nseq % CONV_CHUNK == 0 and seq % 16 == 0
    return seq, nseq


SUBLANES = 8


def _conv_offsets(width, transpose):
    return [(width - 1 - k) if transpose else (CONV_HIST - (width - 1) + k) for k in range(width)]


def _shift_copies(src_ref, sh_ref, width, transpose):
    n = src_ref.shape[0] - SUBLANES
    for s in sorted({o % SUBLANES for o in _conv_offsets(width, transpose)} - {0}):
        sh_ref[s - 1, 0:n, :] = src_ref[s:s + n, :]


def _tap_rows(src_ref, sh_ref, base, off):
    start = pl.multiple_of(base + (off // SUBLANES) * SUBLANES, SUBLANES)
    if off % SUBLANES == 0:
        return src_ref[pl.ds(start, CONV_CHUNK), :]
    return sh_ref[off % SUBLANES - 1, pl.ds(start, CONV_CHUNK), :]


def _conv_taps(src_ref, sh_ref, w_ref, dst_ref, width, nseq, transpose):
    w = w_ref[...]
    offs = _conv_offsets(width, transpose)
    _shift_copies(src_ref, sh_ref, width, transpose)

    def step(n, carry):
        out0 = pl.multiple_of(CONV_HIST + n * CONV_CHUNK, SUBLANES)
        base = out0 if transpose else n * CONV_CHUNK
        acc = jnp.zeros((CONV_CHUNK, w.shape[1]), F32)
        for k, off in enumerate(offs):
            acc = acc + w[k:k + 1, :] * _tap_rows(src_ref, sh_ref, base, off)
        dst_ref[pl.ds(out0, CONV_CHUNK), :] = acc
        return carry

    lax.fori_loop(0, nseq // CONV_CHUNK, step, 0)


def _conv_wgrad(src_ref, sh_ref, dz_ref, acc_ref, width, nseq):
    acc_ref[...] = jnp.zeros(acc_ref.shape, F32)
    offs = _conv_offsets(width, False)

    def step(n, carry):
        dzc = dz_ref[pl.ds(pl.multiple_of(CONV_HIST + n * CONV_CHUNK, SUBLANES), CONV_CHUNK), :]
        for k, off in enumerate(offs):
            prod = dzc * _tap_rows(src_ref, sh_ref, n * CONV_CHUNK, off)
            part = prod[0:SUBLANES, :]
            for s in range(1, CONV_CHUNK // SUBLANES):
                part = part + prod[SUBLANES * s:SUBLANES * (s + 1), :]
            acc_ref[SUBLANES * k:SUBLANES * (k + 1), :] += part
        return carry

    lax.fori_loop(0, nseq // CONV_CHUNK, step, 0)


def _to_seq(buf_ref, x_part, meta_part, seq):
    buf_ref[CONV_HIST:CONV_HIST + N_META, :] = meta_part
    buf_ref[CONV_HIST + N_META:CONV_HIST + N_META + seq, :] = x_part


def _zero_ends(buf_ref, nseq):
    zeros = jnp.zeros((CONV_HIST, buf_ref.shape[1]), F32)
    buf_ref[0:CONV_HIST, :] = zeros
    buf_ref[CONV_HIST + nseq:CONV_HIST + nseq + CONV_HIST, :] = zeros


def _mix_conv_fwd(hp5, wa, wb, bb):
    _, tp, wgrp = hp5.shape
    seq, nseq = _seq_rows(tp)
    sb = nseq + 2 * CONV_HIST
    ka, kb = wa.shape[0], wb.shape[0]
    xs, ms = slice(0, seq), slice(seq, seq + N_META)
    ox, om = slice(CONV_HIST + N_META, CONV_HIST + nseq), slice(CONV_HIST, CONV_HIST + N_META)

    def body(hp_ref, wa_ref, wb_ref, bb_ref, ya_ref, z_ref, s_ref, o_ref, sh_ref):
        _zero_ends(s_ref, nseq)
        _to_seq(s_ref, hp_ref[1, xs, :] * hp_ref[2, xs, :], hp_ref[1, ms, :] * hp_ref[2, ms, :], seq)
        _conv_taps(s_ref, sh_ref, wa_ref, o_ref, ka, nseq, False)
        ya_ref[xs, :] = (hp_ref[0, xs, :] * o_ref[ox, :]).astype(BF16)
        ya_ref[ms, :] = (hp_ref[0, ms, :] * o_ref[om, :]).astype(BF16)
        ya_ref[seq + N_META:tp, :] = jnp.zeros((tp - seq - N_META, LANES), BF16)
        _to_seq(s_ref, hp_ref[3, xs, :] * _sig(hp_ref[4, xs, :]), hp_ref[3, ms, :] * _sig(hp_ref[4, ms, :]), seq)
        _conv_taps(s_ref, sh_ref, wb_ref, o_ref, kb, nseq, False)
        z_ref[xs, :] = o_ref[ox, :] + bb_ref[...]
        z_ref[ms, :] = o_ref[om, :] + bb_ref[...]
        z_ref[seq + N_META:tp, :] = jnp.zeros((tp - seq - N_META, LANES), F32)

    col = lambda j: (0, j)
    return pl.pallas_call(
        body, name="mix_conv_fwd", grid=(wgrp // LANES,),
        in_specs=[pl.BlockSpec((5, tp, LANES), lambda j: (0, 0, j)), pl.BlockSpec((ka, LANES), col),
                  pl.BlockSpec((kb, LANES), col), pl.BlockSpec((1, LANES), col)],
        out_specs=[pl.BlockSpec((tp, LANES), col), pl.BlockSpec((tp, LANES), col)],
        out_shape=[_sds((tp, wgrp), BF16), _sds((tp, wgrp), F32)],
        scratch_shapes=[pltpu.VMEM((sb, LANES), F32), pltpu.VMEM((sb, LANES), F32),
                        pltpu.VMEM((SUBLANES - 1, sb, LANES), F32)],
        compiler_params=_params(("arbitrary",)),
    )(hp5, wa, wb, bb)


def _layer_norm_parts(z, lg, lb):
    mu = _mean(z)
    zc = z - mu
    rl = lax.rsqrt(_mean(zc * zc) + LN_EPS)
    zh = zc * rl
    return rl, zh, zh * lg + lb


def _mm_out(ya, z, h, wout, lg, lb, gpm, g2):
    tp, d = h.shape
    wa_ = ya.shape[1]
    tm = _row_tile(tp)

    def body(ya_ref, z_ref, h_ref, w_ref, lg_ref, lb_ref, gpm_ref, g2_ref, yb_ref, mix_ref, h1_ref, xn2_ref):
        for rs in _row_parts(tm):
            _, _, l = _layer_norm_parts(z_ref[rs, :], lg_ref[...], lb_ref[...])
            yb = (l * _sig(l)).astype(BF16)
            yb_ref[rs, :] = yb
            mix = _dot(ya_ref[rs, :], w_ref[0:wa_, :]) + _dot(yb, w_ref[wa_:d, :])
            mix_ref[rs, :] = mix
            rm = lax.rsqrt(_mean(mix * mix) + RMS_EPS)
            h1 = h_ref[rs, :] + mix * rm * gpm_ref[...]
            h1_ref[rs, :] = h1
            r2 = lax.rsqrt(_mean(h1 * h1) + RMS_EPS)
            xn2_ref[rs, :] = (h1 * r2 * g2_ref[...]).astype(BF16)

    row = lambda i: (i, 0)
    return pl.pallas_call(
        body, name="mm_out", grid=(tp // tm,),
        in_specs=[pl.BlockSpec((tm, wa_), row), pl.BlockSpec((tm, wa_), row), pl.BlockSpec((tm, d), row),
                  _resident(wout.shape), _full(lg.shape), _full(lb.shape), _full(gpm.shape), _full(g2.shape)],
        out_specs=[pl.BlockSpec((tm, wa_), row), pl.BlockSpec((tm, d), row), pl.BlockSpec((tm, d), row),
                   pl.BlockSpec((tm, d), row)],
        out_shape=[_sds((tp, wa_), BF16), _sds((tp, d), F32), _sds((tp, d), F32), _sds((tp, d), BF16)],
        compiler_params=_params(("arbitrary",)),
    )(ya, z, h, wout, lg, lb, gpm, g2)


def _ffn_up(xn2, wg, wu):
    tp, d = xn2.shape
    ff_dim = wg.shape[0]
    tm = _row_tile(tp)
    assert ff_dim % MXU_TILE == 0

    def body(xn_ref, wg_ref, wu_ref, p_ref, q_ref, f_ref):
        xn = xn_ref[...]
        for lo in range(0, ff_dim, MXU_TILE):
            cols = slice(lo, lo + MXU_TILE)
            a = _dot(xn, wg_ref[cols, :], NT)
            u = _dot(xn, wu_ref[cols, :], NT)
            s = _sig(a)
            q = a * s
            p_ref[:, cols] = (u * (s + q * (1.0 - s))).astype(BF16)
            q_ref[:, cols] = q.astype(BF16)
            f_ref[:, cols] = (q * u).astype(BF16)

    ospec = pl.BlockSpec((tm, ff_dim), lambda i: (i, 0))
    return pl.pallas_call(
        body, name="ffn_up", grid=(tp // tm,),
        in_specs=[pl.BlockSpec((tm, d), lambda i: (i, 0)), _resident(wg.shape), _resident(wu.shape)],
        out_specs=[ospec, ospec, ospec],
        out_shape=[_sds((tp, ff_dim), BF16)] * 3,
        compiler_params=_params(("arbitrary",)),
    )(xn2, wg, wu)


def _ffn_down(f, wd, h1, tgt, gpf):
    tp, ff_dim = f.shape
    d = h1.shape[1]
    tm = _row_tile(tp)
    seq, _ = _seq_rows(tp)

    def body(f_ref, w_ref, h1_ref, t_ref, gpf_ref, dff_ref, dh2_ref, loss_ref, dgpf_ref):
        i = pl.program_id(0)
        gpf_ = gpf_ref[...]

        @pl.when(i == 0)
        def _():
            loss_ref[...] = jnp.zeros(loss_ref.shape, F32)
            dgpf_ref[...] = jnp.zeros(dgpf_ref.shape, F32)

        for rs in _row_parts(tm):
            ff = _dot(f_ref[rs, :], w_ref[...])
            rf = lax.rsqrt(_mean(ff * ff) + RMS_EPS)
            nf = ff * rf
            h2 = h1_ref[rs, :] + nf * gpf_
            rows = i * tm + rs.start + lax.broadcasted_iota(jnp.int32, (rs.stop - rs.start, 1), 0)
            err = jnp.where(rows < seq, h2 - t_ref[rs, :], 0.0)
            dh2 = err * (1.0 / d)
            dh2_ref[rs, :] = dh2
            dn = dh2 * gpf_
            dff_ref[rs, :] = (rf * (dn - nf * _mean(dn * nf))).astype(BF16)
            loss_ref[...] += (0.5 / d) * jnp.sum(err * err, axis=(0, 1), keepdims=True)
            dgpf_ref[...] += jnp.sum(dh2 * nf, axis=0, keepdims=True)

    row = lambda i: (i, 0)
    return pl.pallas_call(
        body, name="ffn_down", grid=(tp // tm,),
        in_specs=[pl.BlockSpec((tm, ff_dim), row), _resident(wd.shape), pl.BlockSpec((tm, d), row),
                  pl.BlockSpec((tm, d), row), _full(gpf.shape)],
        out_specs=[pl.BlockSpec((tm, d), row), pl.BlockSpec((tm, d), row), _full((8, LANES)), _full((1, d))],
        out_shape=[_sds((tp, d), BF16), _sds((tp, d), F32), _sds((8, LANES), F32), _sds((1, d), F32)],
        compiler_params=_params(("arbitrary",)),
    )(f, wd, h1, tgt, gpf)


def _ffn_bwd_act(dff, wd, p, q):
    tp, d = dff.shape
    ff_dim = wd.shape[0]
    tm = _row_tile(tp)

    def body(dff_ref, w_ref, p_ref, q_ref, da_ref, du_ref):
        dffv = dff_ref[...]
        for lo in range(0, ff_dim, MXU_TILE):
            cols = slice(lo, lo + MXU_TILE)
            df = _dot(dffv, w_ref[cols, :], NT).astype(BF16)
            da_ref[:, cols] = df * p_ref[:, cols]
            du_ref[:, cols] = df * q_ref[:, cols]

    aspec = pl.BlockSpec((tm, ff_dim), lambda i: (i, 0))
    return pl.pallas_call(
        body, name="ffn_bwd_act", grid=(tp // tm,),
        in_specs=[pl.BlockSpec((tm, d), lambda i: (i, 0)), _resident(wd.shape), aspec, aspec],
        out_specs=[aspec, aspec],
        out_shape=[_sds((tp, ff_dim), BF16)] * 2,
        compiler_params=_params(("arbitrary",)),
    )(dff, wd, p, q)


def _grad_blocks(ff_dim):
    rows = ff_dim // 2
    assert rows % LANES == 0
    return rows


def _grad_w_down(f, dff):
    tp, ff_dim = f.shape
    d = dff.shape[1]
    rows = _grad_blocks(ff_dim)

    def body(f_ref, dff_ref, g_ref):
        g_ref[...] = _dot(f_ref[...], dff_ref[...], TN).astype(BF16)

    return pl.pallas_call(
        body, name="grad_w_down", grid=(ff_dim // rows,),
        in_specs=[pl.BlockSpec((tp, rows), lambda k: (0, k)), _resident(dff.shape)],
        out_specs=pl.BlockSpec((rows, d), lambda k: (k, 0)),
        out_shape=_sds((ff_dim, d), BF16),
        compiler_params=_params(("arbitrary",)),
    )(f, dff)


def _grad_w_gate_up(xn2, da, du):
    tp, ff_dim = da.shape
    d = xn2.shape[1]
    rows = _grad_blocks(ff_dim)

    def body(xn_ref, da_ref, du_ref, gg_ref, gu_ref):
        xn = xn_ref[...]
        gg_ref[...] = _dot(da_ref[...], xn, TN).astype(BF16)
        gu_ref[...] = _dot(du_ref[...], xn, TN).astype(BF16)

    aspec = pl.BlockSpec((tp, rows), lambda k: (0, k))
    gspec = pl.BlockSpec((rows, d), lambda k: (k, 0))
    return pl.pallas_call(
        body, name="grad_w_gate_up", grid=(ff_dim // rows,),
        in_specs=[_resident(xn2.shape), aspec, aspec],
        out_specs=[gspec, gspec],
        out_shape=[_sds((ff_dim, d), BF16)] * 2,
        compiler_params=_params(("arbitrary",)),
    )(xn2, da, du)


def _rms_bwd(dy, x, r, g):
    n = x * r
    dn = dy * g
    return r * (dn - n * _mean(dn * n)), dy * n


def _ffn_bwd_in(da, du, wg, wu, h1, mix, dh2, g2, gpm, after):
    tp, ff_dim = da.shape
    d = h1.shape[1]
    tm = _row_tile(tp)

    def body(da_ref, du_ref, wg_ref, wu_ref, h1_ref, mix_ref, dh2_ref, g2_ref, gpm_ref, after_ref,
             dh1_ref, dmix_ref, dg2_ref, dgpm_ref):
        i = pl.program_id(0)

        @pl.when(i == 0)
        def _():
            dg2_ref[...] = jnp.zeros(dg2_ref.shape, F32)
            dgpm_ref[...] = jnp.zeros(dgpm_ref.shape, F32)

        for rs in _row_parts(tm):
            dxn = _dot(da_ref[rs, :], wg_ref[...]) + _dot(du_ref[rs, :], wu_ref[...])
            h1v = h1_ref[rs, :]
            r2 = lax.rsqrt(_mean(h1v * h1v) + RMS_EPS)
            dres, dg2_rows = _rms_bwd(dxn, h1v, r2, g2_ref[...])
            dh1 = dh2_ref[rs, :] + dres
            dh1_ref[rs, :] = dh1
            mixv = mix_ref[rs, :]
            rm = lax.rsqrt(_mean(mixv * mixv) + RMS_EPS)
            dmix, dgpm_rows = _rms_bwd(dh1, mixv, rm, gpm_ref[...])
            dmix_ref[rs, :] = dmix.astype(BF16)
            dg2_ref[...] += jnp.sum(dg2_rows, axis=0, keepdims=True)
            dgpm_ref[...] += jnp.sum(dgpm_rows, axis=0, keepdims=True)

    aspec = pl.BlockSpec((tm, ff_dim), lambda i: (i, 0))
    row = pl.BlockSpec((tm, d), lambda i: (i, 0))
    return pl.pallas_call(
        body, name="ffn_bwd_in", grid=(tp // tm,),
        in_specs=[aspec, aspec, _resident(wg.shape), _resident(wu.shape), row, row, row, _full(g2.shape), _full(gpm.shape),
                  ANY],
        out_specs=[row, row, _full((1, d)), _full((1, d))],
        out_shape=[_sds((tp, d), F32), _sds((tp, d), BF16), _sds((1, d), F32), _sds((1, d), F32)],
        compiler_params=_params(("arbitrary",)),
    )(da, du, wg, wu, h1, mix, dh2, g2, gpm, after)


def _grad_w_out(ya, yb, dmix, after):
    tp, wa_ = ya.shape
    d = dmix.shape[1]

    def body(ya_ref, yb_ref, dmix_ref, after_ref, g_ref):
        dm = dmix_ref[...]
        g_ref[0:wa_, :] = _dot(ya_ref[...], dm, TN).astype(BF16)
        g_ref[wa_:2 * wa_, :] = _dot(yb_ref[...], dm, TN).astype(BF16)

    return pl.pallas_call(
        body, name="grad_w_out", grid=(1,),
        in_specs=[_full(ya.shape), _full(yb.shape), _full(dmix.shape), ANY],
        out_specs=_full((2 * wa_, d)),
        out_shape=_sds((2 * wa_, d), BF16),
        compiler_params=_params(("arbitrary",)),
    )(ya, yb, dmix, after)


def _mix_bwd_out(dmix, wout, z, lg, lb, after):
    tp, d = dmix.shape
    wa_ = z.shape[1]
    tm = _row_tile(tp)

    def body(dmix_ref, w_ref, z_ref, lg_ref, lb_ref, after_ref, dya_ref, dz_ref, dlg_ref, dlb_ref):
        i = pl.program_id(0)
        lg_ = lg_ref[...]

        @pl.when(i == 0)
        def _():
            dlg_ref[...] = jnp.zeros(dlg_ref.shape, F32)
            dlb_ref[...] = jnp.zeros(dlb_ref.shape, F32)

        for rs in _row_parts(tm):
            dm = dmix_ref[rs, :]
            dya_ref[rs, :] = _dot(dm, w_ref[0:wa_, :], NT)
            dyb = _dot(dm, w_ref[wa_:d, :], NT)
            rl, zh, l = _layer_norm_parts(z_ref[rs, :], lg_, lb_ref[...])
            sl = _sig(l)
            dl = dyb * (sl * (1.0 + l * (1.0 - sl)))
            dzh = dl * lg_
            dz_ref[rs, :] = rl * (dzh - _mean(dzh) - zh * _mean(dzh * zh))
            dlg_ref[...] += jnp.sum(dl * zh, axis=0, keepdims=True)
            dlb_ref[...] += jnp.sum(dl, axis=0, keepdims=True)

    row = lambda i: (i, 0)
    return pl.pallas_call(
        body, name="mix_bwd_out", grid=(tp // tm,),
        in_specs=[pl.BlockSpec((tm, d), row), _resident(wout.shape), pl.BlockSpec((tm, wa_), row), _full(lg.shape),
                  _full(lb.shape), ANY],
        out_specs=[pl.BlockSpec((tm, wa_), row), pl.BlockSpec((tm, wa_), row), _full((1, wa_)), _full((1, wa_))],
        out_shape=[_sds((tp, wa_), F32), _sds((tp, wa_), F32), _sds((1, wa_), F32), _sds((1, wa_), F32)],
        compiler_params=_params(("arbitrary",)),
    )(dmix, wout, z, lg, lb, after)


def _mix_conv_bwd(hp5, dya, dz, wa, wb):
    _, tp, wgrp = hp5.shape
    seq, nseq = _seq_rows(tp)
    sb = nseq + 2 * CONV_HIST
    ka, kb = wa.shape[0], wb.shape[0]
    xs, ms = slice(0, seq), slice(seq, seq + N_META)
    ox, om = slice(CONV_HIST + N_META, CONV_HIST + nseq), slice(CONV_HIST, CONV_HIST + N_META)
    n_tail = tp - seq - N_META

    def body(hp_ref, dya_ref, dz_ref, wa_ref, wb_ref, dhp_ref, dwa_ref, dwb_ref, dbb_ref, s_ref, d_ref, o_ref, acc_ref,
             shs_ref, shd_ref):
        _zero_ends(s_ref, nseq)
        _zero_ends(d_ref, nseq)

        def put(p, ox_val, om_val):
            dhp_ref[p, xs, :] = ox_val.astype(BF16)
            dhp_ref[p, ms, :] = om_val.astype(BF16)
            dhp_ref[p, seq + N_META:tp, :] = jnp.zeros((n_tail, LANES), BF16)

        def wgrad(dw_ref, width):
            for k in range(width):
                dw_ref[k:k + 1, :] = jnp.sum(acc_ref[8 * k:8 * k + 8, :], axis=0, keepdims=True)

        _to_seq(s_ref, hp_ref[1, xs, :] * hp_ref[2, xs, :], hp_ref[1, ms, :] * hp_ref[2, ms, :], seq)
        _conv_taps(s_ref, shs_ref, wa_ref, o_ref, ka, nseq, False)
        put(0, dya_ref[xs, :] * o_ref[ox, :], dya_ref[ms, :] * o_ref[om, :])
        _to_seq(d_ref, dya_ref[xs, :] * hp_ref[0, xs, :], dya_ref[ms, :] * hp_ref[0, ms, :], seq)
        _conv_wgrad(s_ref, shs_ref, d_ref, acc_ref, ka, nseq)
        wgrad(dwa_ref, ka)
        _conv_taps(d_ref, shd_ref, wa_ref, o_ref, ka, nseq, True)
        put(1, o_ref[ox, :] * hp_ref[2, xs, :], o_ref[om, :] * hp_ref[2, ms, :])
        put(2, o_ref[ox, :] * hp_ref[1, xs, :], o_ref[om, :] * hp_ref[1, ms, :])

        _to_seq(s_ref, hp_ref[3, xs, :] * _sig(hp_ref[4, xs, :]), hp_ref[3, ms, :] * _sig(hp_ref[4, ms, :]), seq)
        _to_seq(d_ref, dz_ref[xs, :], dz_ref[ms, :], seq)
        dbb_ref[...] = (jnp.sum(dz_ref[xs, :], axis=0, keepdims=True)
                        + jnp.sum(dz_ref[ms, :], axis=0, keepdims=True))
        _shift_copies(s_ref, shs_ref, kb, False)
        _conv_wgrad(s_ref, shs_ref, d_ref, acc_ref, kb, nseq)
        wgrad(dwb_ref, kb)
        _conv_taps(d_ref, shd_ref, wb_ref, o_ref, kb, nseq, True)
        sx, sm = _sig(hp_ref[4, xs, :]), _sig(hp_ref[4, ms, :])
        put(3, o_ref[ox, :] * sx, o_ref[om, :] * sm)
        put(4, o_ref[ox, :] * hp_ref[3, xs, :] * sx * (1.0 - sx), o_ref[om, :] * hp_ref[3, ms, :] * sm * (1.0 - sm))

    col = lambda j: (0, j)
    blk5 = pl.BlockSpec((5, tp, LANES), lambda j: (0, 0, j))
    return pl.pallas_call(
        body, name="mix_conv_bwd", grid=(wgrp // LANES,),
        in_specs=[blk5, pl.BlockSpec((tp, LANES), col), pl.BlockSpec((tp, LANES), col),
                  pl.BlockSpec((ka, LANES), col), pl.BlockSpec((kb, LANES), col)],
        out_specs=[blk5, pl.BlockSpec((ka, LANES), col), pl.BlockSpec((kb, LANES), col), pl.BlockSpec((1, LANES), col)],
        out_shape=[_sds((5, tp, wgrp), BF16), _sds((ka, wgrp), F32), _sds((kb, wgrp), F32), _sds((1, wgrp), F32)],
        scratch_shapes=[pltpu.VMEM((sb, LANES), F32), pltpu.VMEM((sb, LANES), F32), pltpu.VMEM((sb, LANES), F32),
                        pltpu.VMEM((SUBLANES * kb, LANES), F32), pltpu.VMEM((SUBLANES - 1, sb, LANES), F32),
                        pltpu.VMEM((SUBLANES - 1, sb, LANES), F32)],
        compiler_params=_params(("arbitrary",)),
    )(hp5, dya, dz, wa, wb)


def _grad_w_in(xn1, dhp5):
    n_p, tp, pw = dhp5.shape
    d = xn1.shape[1]

    def body(xn_ref, dhp_ref, g_ref):
        g_ref[...] = _dot(xn_ref[...], dhp_ref[0], TN).astype(BF16)

    return pl.pallas_call(
        body, name="grad_w_in", grid=(n_p,),
        in_specs=[_resident(xn1.shape), pl.BlockSpec((1, tp, pw), lambda p: (p, 0, 0))],
        out_specs=pl.BlockSpec((d, pw), lambda p: (0, p)),
        out_shape=_sds((d, n_p * pw), BF16),
        compiler_params=_params(("arbitrary",)),
    )(xn1, dhp5)


def _mix_bwd_in(dhp5, win4, h, dh1, g1, after):
    n_p, tp, pw = dhp5.shape
    d = h.shape[1]
    n_sh, _, csh = win4.shape
    tm = _row_tile(tp)

    seq, _ = _seq_rows(tp)
    last, meta_off = seq // tm, seq % tm
    assert last == tp // tm - 1
    assert any(rs.start <= meta_off and meta_off + N_META <= rs.stop for rs in _row_parts(tm))

    def body(dhp_ref, w_ref, h_ref, dh1_ref, g_ref, after_ref, gx_ref, dmeta_ref, dg1_ref, wcat_ref):
        i = pl.program_id(0)
        _concat_shards(w_ref, wcat_ref)

        @pl.when(i == 0)
        def _():
            dg1_ref[...] = jnp.zeros(dg1_ref.shape, F32)

        for rs in _row_parts(tm):
            dxn = _dot(dhp_ref[0, rs, :], wcat_ref[:, 0:pw], NT)
            for p in range(1, n_p):
                dxn = dxn + _dot(dhp_ref[p, rs, :], wcat_ref[:, p * pw:(p + 1) * pw], NT)
            hh = h_ref[rs, :]
            r1 = lax.rsqrt(_mean(hh * hh) + RMS_EPS)
            dres, dg_rows = _rms_bwd(dxn, hh, r1, g_ref[...])
            dh = dh1_ref[rs, :] + dres
            gx_ref[rs, :] = dh
            dg1_ref[...] += jnp.sum(dg_rows, axis=0, keepdims=True)
            if rs.start <= meta_off and meta_off + N_META <= rs.stop:
                @pl.when(i == last)
                def _():
                    dmeta_ref[...] = dh[meta_off - rs.start:meta_off - rs.start + N_META, :]

    row = lambda i: (i, 0)
    return pl.pallas_call(
        body, name="mix_bwd_in", grid=(tp // tm,),
        in_specs=[pl.BlockSpec((n_p, tm, pw), lambda i: (0, i, 0)), _resident(win4.shape), pl.BlockSpec((tm, d), row),
                  pl.BlockSpec((tm, d), row), _full(g1.shape), ANY],
        out_specs=[pl.BlockSpec((tm, d), row), _full((N_META, d)), _full((1, d))],
        out_shape=[_sds((seq, d), F32), _sds((N_META, d), F32), _sds((1, d), F32)],
        scratch_shapes=[pltpu.VMEM((d, n_sh * csh), BF16)],
        compiler_params=_params(("arbitrary",)),
    )(dhp5, win4, h, dh1, g1, after)


def _other_chips(x, y):
    out = []
    for j in (1, 2, 3):
        px, py = _flip(x, j >> 1), _flip(y, j & 1)
        out.append((px, py, 2 * px + py))
    return out


PAIR_COLLECTIVE_ID = 0


def _pair_barrier(x, y, c):
    sem = pltpu.get_barrier_semaphore()
    pl.semaphore_signal(sem, inc=1, device_id=(x, y, 1 - c), device_id_type=MESH)
    pl.semaphore_wait(sem, 1)


def _pair_params():
    return pltpu.CompilerParams(collective_id=PAIR_COLLECTIVE_ID)


def _half_rows(c, rows_half):
    return pl.ds(pl.multiple_of(c * rows_half, 8), rows_half)


def _cast_place(ws, q_arr, tag, after=None):
    n = len(ws)
    rows, cols = ws[0].shape
    tr = _row_tile(rows)
    extra = [] if after is None else [after]

    def body(q_ref, *refs):
        for w_ref, out_ref in zip(refs[:n], refs[n + len(extra):]):
            out_ref[0] = w_ref[...].astype(BF16)

    return list(pl.pallas_call(
        body, name="cast_place_" + tag,
        grid_spec=pltpu.PrefetchScalarGridSpec(
            num_scalar_prefetch=1, grid=(rows // tr,),
            in_specs=[pl.BlockSpec((tr, cols), lambda i, q: (i, 0))] * n + [ANY] * len(extra),
            out_specs=[pl.BlockSpec((1, tr, cols), lambda i, q: (q[0], i, 0))] * n),
        out_shape=[_sds((N_CHIPS, rows, cols), BF16)] * n,
        compiler_params=_params(("arbitrary",)),
    )(q_arr, *ws, *extra))


HBM = pl.BlockSpec(memory_space=pltpu.HBM)
SEM = pl.BlockSpec(memory_space=pltpu.SEMAPHORE)
EFFECT = pltpu.SideEffectType.DATAFLOW_SIDE_EFFECTING


def _in_hbm(a):
    return pltpu.with_memory_space_constraint(a, pltpu.HBM)


def _gather_start(fulls, after, tag):
    n = len(fulls)
    halves = [a.shape[1] // 2 for a in fulls]

    def body(*refs):
        land = refs[:n]
        ssem, rsem = refs[n + 1], refs[n + 2]
        token = refs[-1]
        x, y, c = _mesh_pos()
        q = 2 * x + y
        for i in range(n):
            for j, (px, py, _) in enumerate(_other_chips(x, y)):
                mine = land[i].at[q, _half_rows(c, halves[i]), :]
                pltpu.make_async_remote_copy(src_ref=mine, dst_ref=mine, send_sem=ssem.at[3 * i + j],
                                             recv_sem=rsem.at[3 * i + j], device_id=(px, py, c), device_id_type=MESH).start()
        token[...] = jnp.zeros(token.shape, F32)

    outs = pl.pallas_call(
        body, name="gather_start_" + tag,
        in_specs=[HBM] * n + [ANY], out_specs=[SEM, SEM] + [HBM] * n + [VMEM],
        out_shape=[pltpu.SemaphoreType.DMA((3 * n,)), pltpu.SemaphoreType.DMA((3 * n,))]
        + [pltpu.HBM(a.shape, a.dtype) for a in fulls] + [_sds((8, LANES), F32)],
        input_output_aliases={i: 2 + i for i in range(n)},
        compiler_params=pltpu.CompilerParams(has_side_effects=EFFECT),
    )(*[_in_hbm(a) for a in fulls], after)
    return outs[0], outs[1], list(outs[2:2 + n]), outs[-1]


def _gather_wait(which, ssem, rsem, lands, after, tag):
    m = len(which)
    halves = [a.shape[1] // 2 for a in lands]

    def body(*refs):
        land = refs[:m]
        ssem_, rsem_ = refs[m], refs[m + 1]
        x, y, c = _mesh_pos()
        for t, i in enumerate(which):
            for j, (px, py, qj) in enumerate(_other_chips(x, y)):
                rows = _half_rows(c, halves[t])
                cp = pltpu.make_async_remote_copy(src_ref=land[t].at[2 * x + y, rows, :], dst_ref=land[t].at[qj, rows, :],
                                                  send_sem=ssem_.at[3 * i + j], recv_sem=rsem_.at[3 * i + j],
                                                  device_id=(px, py, c), device_id_type=MESH)
                cp.wait_send()
                cp.wait_recv()

    outs = pl.pallas_call(
        body, name="gather_wait_" + tag,
        in_specs=[HBM] * m + [SEM, SEM, ANY], out_specs=[HBM] * m,
        out_shape=[pltpu.HBM(a.shape, a.dtype) for a in lands],
        input_output_aliases={i: i for i in range(m)},
        compiler_params=pltpu.CompilerParams(has_side_effects=EFFECT),
    )(*lands, ssem, rsem, after)
    return list(outs)


def _forward_pair(lands, tag):
    n = len(lands)
    halves = [a.shape[1] // 2 for a in lands]

    def body(*refs):
        full = refs[n:2 * n]
        ssem, rsem = refs[2 * n:]
        x, y, c = _mesh_pos()
        _pair_barrier(x, y, c)
        cps = []
        for i in range(n):
            for j, (_, _, qj) in enumerate(_other_chips(x, y)):
                part = full[i].at[qj, _half_rows(c, halves[i]), :]
                cp = pltpu.make_async_remote_copy(src_ref=part, dst_ref=part, send_sem=ssem.at[3 * i + j],
                                                  recv_sem=rsem.at[3 * i + j], device_id=(x, y, 1 - c), device_id_type=MESH)
                cp.start()
                cps.append(cp)
        for cp in cps:
            cp.wait()

    return pl.pallas_call(
        body, name="forward_pair_" + tag,
        in_specs=[ANY] * n, out_specs=[ANY] * n,
        out_shape=[_sds(a.shape, a.dtype) for a in lands],
        input_output_aliases={i: i for i in range(n)},
        scratch_shapes=[pltpu.SemaphoreType.DMA((3 * n,)), pltpu.SemaphoreType.DMA((3 * n,))],
        compiler_params=_pair_params(),
    )(*lands)


def _chip_exchange_start(parts, after, tag):
    n = len(parts)

    def body(*refs):
        src, land = refs[:n], refs[n:2 * n]
        ssem, rsem = refs[2 * n + 1], refs[2 * n + 2]
        token = refs[-1]
        x, y, c = _mesh_pos()
        for i in range(n):
            for j, (px, py, qj) in enumerate(_other_chips(x, y)):
                pltpu.make_async_remote_copy(src_ref=src[i].at[qj], dst_ref=land[i].at[j], send_sem=ssem.at[3 * i + j],
                                             recv_sem=rsem.at[3 * i + j], device_id=(px, py, c), device_id_type=MESH).start()
        token[...] = jnp.zeros(token.shape, F32)

    lands = [lax.empty((3,) + a.shape[1:], a.dtype) for a in parts]
    outs = pl.pallas_call(
        body, name="chip_exchange_start_" + tag,
        in_specs=[HBM] * (2 * n) + [ANY], out_specs=[SEM, SEM] + [HBM] * (2 * n) + [VMEM],
        out_shape=[pltpu.SemaphoreType.DMA((3 * n,)), pltpu.SemaphoreType.DMA((3 * n,))]
        + [pltpu.HBM(a.shape, a.dtype) for a in parts] + [pltpu.HBM(a.shape, a.dtype) for a in lands]
        + [_sds((8, LANES), F32)],
        input_output_aliases={i: 2 + i for i in range(2 * n)},
        compiler_params=pltpu.CompilerParams(has_side_effects=EFFECT),
    )(*[_in_hbm(a) for a in parts], *[_in_hbm(a) for a in lands], after)
    return outs[0], outs[1], list(outs[2:2 + n]), list(outs[2 + n:2 + 2 * n]), outs[-1]


def _chip_exchange_wait(ssem, rsem, parts, lands, after, tag):
    n = len(parts)

    def body(*refs):
        src, land = refs[:n], refs[n:2 * n]
        ssem_, rsem_ = refs[2 * n], refs[2 * n + 1]
        x, y, c = _mesh_pos()
        for i in range(n):
            for j, (px, py, qj) in enumerate(_other_chips(x, y)):
                cp = pltpu.make_async_remote_copy(src_ref=src[i].at[qj], dst_ref=land[i].at[j], send_sem=ssem_.at[3 * i + j],
                                                  recv_sem=rsem_.at[3 * i + j], device_id=(px, py, c), device_id_type=MESH)
                cp.wait_send()
                cp.wait_recv()

    outs = pl.pallas_call(
        body, name="chip_exchange_wait_" + tag,
        in_specs=[HBM] * (2 * n) + [SEM, SEM, ANY], out_specs=[HBM] * (2 * n),
        out_shape=[pltpu.HBM(a.shape, a.dtype) for a in parts] + [pltpu.HBM(a.shape, a.dtype) for a in lands],
        input_output_aliases={i: i for i in range(2 * n)},
        compiler_params=pltpu.CompilerParams(has_side_effects=EFFECT),
    )(*parts, *lands, ssem, rsem, after)
    return list(outs[:n]), list(outs[n:])


def _grad_half(ref, shape, axis, which):
    rows = shape[axis] // 2
    if axis == 0:
        return ref.at[_half_rows(which, rows), :]
    return ref.at[:, _half_rows(which, rows), :]


def _half_shape(a, axis):
    s = list(a.shape)
    s[axis] //= 2
    return tuple(s)


def _pair_exchange_start(grads, half_axis, after, tag):
    n = len(grads)

    def body(*refs):
        g, land = refs[:n], refs[n:2 * n]
        ssem, rsem = refs[2 * n + 1], refs[2 * n + 2]
        token = refs[-1]
        x, y, c = _mesh_pos()
        for i in range(n):
            pltpu.make_async_remote_copy(src_ref=_grad_half(g[i], grads[i].shape, half_axis[i], 1 - c), dst_ref=land[i],
                                         send_sem=ssem.at[i], recv_sem=rsem.at[i], device_id=(x, y, 1 - c),
                                         device_id_type=MESH).start()
        token[...] = jnp.zeros(token.shape, F32)

    lands = [lax.empty(_half_shape(a, half_axis[i]), a.dtype) for i, a in enumerate(grads)]
    outs = pl.pallas_call(
        body, name="pair_exchange_start_" + tag,
        in_specs=[HBM] * (2 * n) + [ANY], out_specs=[SEM, SEM] + [HBM] * (2 * n) + [VMEM],
        out_shape=[pltpu.SemaphoreType.DMA((n,)), pltpu.SemaphoreType.DMA((n,))]
        + [pltpu.HBM(a.shape, a.dtype) for a in grads] + [pltpu.HBM(a.shape, a.dtype) for a in lands]
        + [_sds((8, LANES), F32)],
        input_output_aliases={i: 2 + i for i in range(2 * n)},
        compiler_params=pltpu.CompilerParams(has_side_effects=EFFECT),
    )(*[_in_hbm(a) for a in grads], *[_in_hbm(a) for a in lands], after)
    return outs[0], outs[1], list(outs[2:2 + n]), list(outs[2 + n:2 + 2 * n]), outs[-1]


def _pair_exchange_wait(ssem, rsem, grads, lands, half_axis, after, tag):
    n = len(grads)

    def body(*refs):
        g, land = refs[:n], refs[n:2 * n]
        ssem_, rsem_ = refs[2 * n], refs[2 * n + 1]
        x, y, c = _mesh_pos()
        for i in range(n):
            cp = pltpu.make_async_remote_copy(src_ref=_grad_half(g[i], grads[i].shape, half_axis[i], 1 - c),
                                              dst_ref=land[i], send_sem=ssem_.at[i], recv_sem=rsem_.at[i],
                                              device_id=(x, y, 1 - c), device_id_type=MESH)
            cp.wait_send()
            cp.wait_recv()

    outs = pl.pallas_call(
        body, name="pair_exchange_wait_" + tag,
        in_specs=[HBM] * (2 * n) + [SEM, SEM, ANY], out_specs=[HBM] * (2 * n),
        out_shape=[pltpu.HBM(a.shape, a.dtype) for a in grads] + [pltpu.HBM(a.shape, a.dtype) for a in lands],
        input_output_aliases={i: i for i in range(2 * n)},
        compiler_params=pltpu.CompilerParams(has_side_effects=EFFECT),
    )(*grads, *lands, ssem, rsem, after)
    return list(outs[:n]), list(outs[n:])


def _pair_exchange_grads(grads, half_axis, tag):
    n = len(grads)

    def body(*refs):
        g, got = refs[:n], refs[n:2 * n]
        ssem, rsem = refs[2 * n:]
        x, y, c = _mesh_pos()
        _pair_barrier(x, y, c)
        cps = []
        for i in range(n):
            cp = pltpu.make_async_remote_copy(src_ref=_grad_half(g[i], grads[i].shape, half_axis[i], 1 - c),
                                              dst_ref=got[i], send_sem=ssem.at[i], recv_sem=rsem.at[i],
                                              device_id=(x, y, 1 - c), device_id_type=MESH)
            cp.start()
            cps.append(cp)
        for cp in cps:
            cp.wait()

    return pl.pallas_call(
        body, name="pair_exchange_grads_" + tag,
        in_specs=[ANY] * n, out_specs=[ANY] * n,
        out_shape=[_sds(_half_shape(a, half_axis[i]), a.dtype) for i, a in enumerate(grads)],
        scratch_shapes=[pltpu.SemaphoreType.DMA((n,)), pltpu.SemaphoreType.DMA((n,))],
        compiler_params=_pair_params(),
    )(*grads)


def _pair_sum(gs, gots, c_arr, col_sharded, tag):
    n = len(gs)
    if col_sharded:
        rows, cols = gs[0].shape
        rh, cs = rows // 2, cols // N_CHIPS
        g_spec = pl.BlockSpec((rh, cs), lambda k, c_ref: (c_ref[0], k))
        got_spec = pl.BlockSpec((rh, cs), lambda k, c_ref: (0, k))
    else:
        _, rows, cs = gs[0].shape
        rh = rows // 2
        g_spec = pl.BlockSpec((1, rh, cs), lambda k, c_ref: (k, c_ref[0], 0))
        got_spec = pl.BlockSpec((1, rh, cs), lambda k, c_ref: (k, 0, 0))

    def body(c_ref, *refs):
        for g_ref, got_ref, out_ref in zip(refs[:n], refs[n:2 * n], refs[2 * n:]):
            total = g_ref[...].astype(F32) + got_ref[...].astype(F32)
            out_ref[...] = total.astype(BF16).reshape(out_ref.shape)

    return list(pl.pallas_call(
        body, name="pair_sum_" + tag,
        grid_spec=pltpu.PrefetchScalarGridSpec(
            num_scalar_prefetch=1, grid=(N_CHIPS,), in_specs=[g_spec] * n + [got_spec] * n,
            out_specs=[pl.BlockSpec((1, rh, cs), lambda k, c_ref: (k, 0, 0))] * n),
        out_shape=[_sds((N_CHIPS, rh, cs), BF16)] * n,
        compiler_params=_params(("arbitrary",)),
    )(c_arr, *gs, *gots))


def _chip_sum(parts, gots, qc_arr, tag):
    n = len(parts)
    _, rh, cs = parts[0].shape
    steps = 2 if rh % 32 == 0 else 1
    rb = rh // steps

    def body(qc_ref, *refs):
        for part_ref, got_ref, out_ref in zip(refs[:n], refs[n:2 * n], refs[2 * n:]):
            total = part_ref[0].astype(F32)
            for j in range(3):
                total = total + got_ref[j].astype(F32)
            out_ref[...] = total

    return list(pl.pallas_call(
        body, name="chip_sum_" + tag,
        grid_spec=pltpu.PrefetchScalarGridSpec(
            num_scalar_prefetch=1, grid=(steps,),
            in_specs=[pl.BlockSpec((1, rb, cs), lambda i, qc: (qc[0], i, 0))] * n
            + [pl.BlockSpec((3, rb, cs), lambda i, qc: (0, i, 0))] * n,
            out_specs=[pl.BlockSpec((rb, cs), lambda i, qc: (qc[1] * steps + i, 0))] * n),
        out_shape=[_sds((2 * rh, cs), F32)] * n,
        compiler_params=_params(("arbitrary",)),
    )(qc_arr, *parts, *gots))


def _pair_share_grads(grads, tag):
    n = len(grads)

    def body(*refs):
        g = refs[n:2 * n]
        ssem, rsem = refs[2 * n:]
        x, y, c = _mesh_pos()
        _pair_barrier(x, y, c)
        cps = []
        for i in range(n):
            mine = g[i].at[_half_rows(c, grads[i].shape[0] // 2), :]
            cp = pltpu.make_async_remote_copy(src_ref=mine, dst_ref=mine, send_sem=ssem.at[i], recv_sem=rsem.at[i],
                                              device_id=(x, y, 1 - c), device_id_type=MESH)
            cp.start()
            cps.append(cp)
        for cp in cps:
            cp.wait()

    return pl.pallas_call(
        body, name="pair_share_grads_" + tag,
        in_specs=[ANY] * n, out_specs=[ANY] * n,
        out_shape=[_sds(a.shape, a.dtype) for a in grads],
        input_output_aliases={i: i for i in range(n)},
        scratch_shapes=[pltpu.SemaphoreType.DMA((n,)), pltpu.SemaphoreType.DMA((n,))],
        compiler_params=_pair_params(),
    )(*grads)


def _small_allreduce(parts, places, rows_total, width, after):
    n = len(parts)

    def body(*refs):
        ins, out_ref = refs[:n], refs[n + 1]
        pack, pair_got, chip_sum, got, ssem, rsem = refs[n + 2:]
        x, y, c = _mesh_pos()
        chip = 2 * x + y
        pack[...] = jnp.zeros(pack.shape, F32)
        for i in range(n):
            for row, col, src_row, rows in places[i]:
                w = parts[i].shape[1]
                pack[row:row + rows, col:col + w] = ins[i][src_row:src_row + rows, :]
        swap = pltpu.make_async_remote_copy(src_ref=pack, dst_ref=pair_got, send_sem=ssem.at[3], recv_sem=rsem.at[3],
                                            device_id=(x, y, 1 - c), device_id_type=MESH)
        swap.start()
        swap.wait()
        chip_sum[...] = pack[...] + pair_got[...]
        cps = []
        for j, (px, py, _) in enumerate(_other_chips(x, y)):
            cp = pltpu.make_async_remote_copy(src_ref=chip_sum, dst_ref=got.at[j], send_sem=ssem.at[j],
                                              recv_sem=rsem.at[j], device_id=(px, py, c), device_id_type=MESH)
            cp.start()
            cps.append(cp)
        for cp in cps:
            cp.wait()
        total = jnp.zeros(pack.shape, F32)
        for q in range(N_CHIPS):
            rel = jnp.bitwise_xor(chip, q)
            theirs = got[jnp.maximum(rel - 1, 0)]
            total = total + jnp.where(rel == 0, chip_sum[...], theirs)
        out_ref[...] = total

    return pl.pallas_call(
        body, name="small_allreduce",
        in_specs=[VMEM] * n + [ANY], out_specs=VMEM,
        out_shape=_sds((rows_total, width), F32),
        scratch_shapes=[pltpu.VMEM((rows_total, width), F32), pltpu.VMEM((rows_total, width), F32),
                        pltpu.VMEM((rows_total, width), F32), pltpu.VMEM((3, rows_total, width), F32),
                        pltpu.SemaphoreType.DMA((4,)), pltpu.SemaphoreType.DMA((4,))],
        compiler_params=_params(),
    )(*parts, after)


def _small_update(red, q_arr, takes, loss_at, ws, ms, vs):
    n_w = len(ws)

    def body(q_ref, red_ref, *refs):
        w_in, m_in, v_in = refs[0:n_w], refs[n_w:2 * n_w], refs[2 * n_w:3 * n_w]
        outs = refs[3 * n_w:]
        g_out, d_out, m_out, v_out = (outs[0:n_w], outs[n_w:2 * n_w], outs[2 * n_w:3 * n_w], outs[3 * n_w:4 * n_w])
        loss_ref = outs[4 * n_w]
        chip = q_ref[0]

        def take_own_columns(g_ref, d0, nr, s0, c0, w):
            for k in range(N_CHIPS):
                @pl.when(chip == k)
                def _():
                    g_ref[d0:d0 + nr, :] = red_ref[s0:s0 + nr, c0 + k * w:c0 + (k + 1) * w]

        for j in range(n_w):
            w = ws[j].shape[1]
            for d0, nr, s0, c0, sharded in takes[j]:
                if sharded:
                    take_own_columns(g_out[j], d0, nr, s0, c0, w)
                else:
                    g_out[j][d0:d0 + nr, :] = red_ref[s0:s0 + nr, c0:c0 + w]
            d_out[j][...], m_out[j][...], v_out[j][...] = _adamw_math(w_in[j][...], g_out[j][...], m_in[j][...], v_in[j][...])
        loss_ref[...] = red_ref[loss_at[0]:loss_at[0] + 1, loss_at[1]:loss_at[1] + LANES]

    shapes = [_sds(w.shape, F32) for w in ws]
    outs = pl.pallas_call(
        body, name="small_update",
        in_specs=[pl.BlockSpec(memory_space=pltpu.SMEM)] + [VMEM] * (1 + 3 * n_w), out_specs=[VMEM] * (4 * n_w + 1),
        out_shape=shapes * 4 + [_sds((1, LANES), F32)],
        compiler_params=_params(),
    )(q_arr, red, *ws, *ms, *vs)
    return outs[0:n_w], outs[n_w:2 * n_w], outs[2 * n_w:3 * n_w], outs[3 * n_w:4 * n_w], outs[4 * n_w]


def _adamw_math(w, g, m, v):
    m2 = ADAM_B1 * m + (1.0 - ADAM_B1) * g
    v2 = ADAM_B2 * v + (1.0 - ADAM_B2) * (g * g)
    m_hat = m2 / (1.0 - ADAM_B1 ** ADAM_STEP)
    v_hat = v2 / (1.0 - ADAM_B2 ** ADAM_STEP)
    delta = -ADAM_LR * (m_hat / (jnp.sqrt(v_hat) + ADAM_EPS) + ADAM_WD * w)
    return delta, m2, v2


ADAMW_BLOCK_BYTES = 3 * 2 ** 19


def _adamw_big(ws, gs, ms, vs, tag):
    n = len(ws)
    rows, cols = ws[0].shape
    tr = 16
    for t in range(16, rows + 1, 16):
        if rows % t == 0 and t * cols * 4 * n <= ADAMW_BLOCK_BYTES:
            tr = t

    def body(*refs):
        ins, outs = refs[:4 * n], refs[4 * n:]
        for i in range(n):
            w_ref, g_ref, m_ref, v_ref = ins[i], ins[n + i], ins[2 * n + i], ins[3 * n + i]
            gg = g_ref[...]
            outs[4 * i][...] = gg
            outs[4 * i + 1][...], outs[4 * i + 2][...], outs[4 * i + 3][...] = _adamw_math(
                w_ref[...], gg, m_ref[...], v_ref[...])

    spec = pl.BlockSpec((tr, cols), lambda i: (i, 0))
    outs = pl.pallas_call(
        body, name="adamw_" + tag, grid=(rows // tr,),
        in_specs=[spec] * (4 * n), out_specs=[spec] * (4 * n),
        out_shape=[_sds((rows, cols), F32)] * (4 * n),
        compiler_params=_params(("arbitrary",)),
    )(*ws, *gs, *ms, *vs)
    return [outs[4 * i:4 * i + 4] for i in range(n)]


SMALL_ROWS = 40
PACK_ROWS = 64


def kernel(x, meta_tokens, pre_mix_norm, w_in, conv_a_w, conv_b_w, conv_b_bias, ln_b_gain, ln_b_bias, w_out, post_mix_norm, pre_ffn_norm, w_gate, w_up, w_down, post_ffn_norm, loss_target, m_meta_tokens, m_pre_mix_norm, m_w_in, m_conv_a_w, m_conv_b_w, m_conv_b_bias, m_ln_b_gain, m_ln_b_bias, m_w_out, m_post_mix_norm, m_pre_ffn_norm, m_w_gate, m_w_up, m_w_down, m_post_ffn_norm, v_meta_tokens, v_pre_mix_norm, v_w_in, v_conv_a_w, v_conv_b_w, v_conv_b_bias, v_ln_b_gain, v_ln_b_bias, v_w_out, v_post_mix_norm, v_pre_ffn_norm, v_w_gate, v_w_up, v_w_down, v_post_ffn_norm):
    xq, yq, cq = lax.axis_index("x"), lax.axis_index("y"), lax.axis_index("c")
    chip = 2 * xq + yq
    c_arr = jnp.reshape(cq, (1,)).astype(jnp.int32)
    qc_arr = jnp.stack([chip, cq]).astype(jnp.int32)

    seq, d = x.shape[1], x.shape[2]
    x2, tgt2 = x[0], loss_target[0]
    tr = lambda a: jnp.swapaxes(a, 1, 2)[0]
    w_in2, w_out2, w_gate2, w_up2, w_down2 = w_in[0], w_out[0], tr(w_gate), tr(w_up), w_down[0]
    ka, wa_sh = conv_a_w.shape[1], conv_a_w.shape[2]
    kb = conv_b_w.shape[1]
    meta_sh = meta_tokens.shape[1]

    small = jnp.zeros((PACK_ROWS, meta_sh), F32)
    small = small.at[0:N_META, :].set(meta_tokens)
    small = small.at[16:16 + ka, 0:wa_sh].set(conv_a_w[0])
    small = small.at[24:24 + kb, 0:wa_sh].set(conv_b_w[0])
    q_arr = jnp.reshape(chip, (1,)).astype(jnp.int32)
    small_own = lax.dynamic_update_slice(jnp.zeros((N_CHIPS, PACK_ROWS, meta_sh), F32), small[None], (chip, 0, 0))
    i_ssem, i_rsem, first, i_token = _gather_start(_cast_place([w_in2], q_arr, "w_in") + [small_own], pre_mix_norm, "in")
    rest = _cast_place([w_out2], q_arr, "w_out", i_token) + _cast_place([w_gate2, w_up2, w_down2], q_arr, "ffn", i_token)
    g_ssem, g_rsem, lands, g_token = _gather_start(rest, i_token, "rest")
    win4, small4 = _forward_pair(_gather_wait([0, 1], i_ssem, i_rsem, first, g_token, "in"), "in")
    meta_f = jnp.concatenate([small4[k, 0:N_META, :] for k in range(N_CHIPS)], axis=1)
    wa_f = jnp.concatenate([small4[k, 16:16 + ka, 0:wa_sh] for k in range(N_CHIPS)], axis=1)
    wb_f = jnp.concatenate([small4[k, 24:24 + kb, 0:wa_sh] for k in range(N_CHIPS)], axis=1)

    tm = _row_tile(seq + TAIL_ROWS)
    tail = lax.dynamic_update_slice(jnp.zeros((tm, d), F32), meta_f, (seq % tm, 0))
    h, xn1, hp5 = _mm_in(x2, tail, win4, pre_mix_norm, g_token)
    ya, z = _mix_conv_fwd(hp5, wa_f, wb_f, conv_b_bias)
    (wout4,) = _forward_pair(_gather_wait([0], g_ssem, g_rsem, lands[0:1], z, "out"), "out")
    wout_f = wout4.reshape(N_CHIPS * wout4.shape[1], wout4.shape[2])
    yb, mix, h1, xn2 = _mm_out(ya, z, h, wout_f, ln_b_gain, ln_b_bias, post_mix_norm, pre_ffn_norm)
    wg4, wu4 = _forward_pair(_gather_wait([1, 2], g_ssem, g_rsem, lands[1:3], xn2, "gate_up"), "gate_up")
    stacked = lambda a: a.reshape(a.shape[0] * a.shape[1], a.shape[2])
    wg_f, wu_f = stacked(wg4), stacked(wu4)
    p_act, q_act, f_act = _ffn_up(xn2, wg_f, wu_f)
    (wd4,) = _forward_pair(_gather_wait([3], g_ssem, g_rsem, lands[3:4], f_act, "down"), "down")
    wd_f = stacked(wd4)
    dff, dh2, loss_blk, d_gpf = _ffn_down(f_act, wd_f, h1, tgt2, post_ffn_norm)

    da, du = _ffn_bwd_act(dff, wd_f, p_act, q_act)
    by_chip = lambda g: g.reshape(N_CHIPS, g.shape[0] // N_CHIPS, g.shape[1])
    g_down = by_chip(_grad_w_down(f_act, dff))
    g_gate, g_up = [by_chip(g) for g in _grad_w_gate_up(xn2, da, du)]
    ffn = [g_gate, g_up, g_down]
    p_ssem, p_rsem, ffn, p_lands, p_token = _pair_exchange_start(ffn, [1, 1, 1], dff, "ffn")
    dh1, dmix, d_g2, d_gpm = _ffn_bwd_in(da, du, wg_f, wu_f, h1, mix, dh2, pre_ffn_norm, post_mix_norm, p_token)
    ffn, got = _pair_exchange_wait(p_ssem, p_rsem, ffn, p_lands, [1, 1, 1], d_g2, "ffn")
    parts = _pair_sum(ffn, got, c_arr, False, "ffn")
    f_ssem, f_rsem, parts, f_lands, f_token = _chip_exchange_start(parts, dff, "ffn")
    g_out = _grad_w_out(ya, yb, dmix, f_token)
    dya, dz, d_lg, d_lb = _mix_bwd_out(dmix, wout_f, z, ln_b_gain, ln_b_bias, f_token)
    dhp5, d_wa, d_wb, d_bb = _mix_conv_bwd(hp5, dya, dz, wa_f, wb_f)
    g_in = _grad_w_in(xn1, dhp5)

    g_out4 = g_out.reshape(N_CHIPS, g_out.shape[0] // N_CHIPS, g_out.shape[1])
    mixw = [g_in, g_out4]
    got2 = _pair_exchange_grads(mixw, [0, 1], "mix")
    parts2 = _pair_sum(mixw[0:1], got2[0:1], c_arr, True, "in") + _pair_sum(mixw[1:2], got2[1:2], c_arr, False, "out")
    m_ssem, m_rsem, parts2, m_lands, m_token = _chip_exchange_start(parts2, dhp5, "mix")
    grad_x2, d_meta, d_g1 = _mix_bwd_in(dhp5, win4, h, dh1, pre_mix_norm, m_token)
    grad_x = grad_x2[None]

    parts, f_recv = _chip_exchange_wait(f_ssem, f_rsem, parts, f_lands, d_g1, "ffn")
    halves = _chip_sum(parts, f_recv, qc_arr, "ffn")
    gsum_ffn = _pair_share_grads(halves, "ffn")

    names_big = ["w_in", "w_out", "w_gate", "w_up", "w_down"]
    w_big = dict(zip(names_big, [w_in2, w_out2, w_gate2, w_up2, w_down2]))
    m_big = dict(zip(names_big, [m_w_in[0], m_w_out[0], tr(m_w_gate), tr(m_w_up), m_w_down[0]]))
    v_big = dict(zip(names_big, [v_w_in[0], v_w_out[0], tr(v_w_gate), tr(v_w_up), v_w_down[0]]))
    grads, deltas, new_m, new_v = {}, {}, {}, {}

    def update(names, gs, tag):
        res = _adamw_big([w_big[k] for k in names], gs, [m_big[k] for k in names], [v_big[k] for k in names], tag)
        for nm, outs in zip(names, res):
            if nm in ("w_gate", "w_up"):
                outs = [jnp.swapaxes(o[None], 1, 2) for o in outs]
            else:
                outs = [o[None] for o in outs]
            grads[nm], deltas[nm], new_m[nm], new_v[nm] = outs
        return res[-1][1]

    last = update(["w_gate", "w_up", "w_down"], list(gsum_ffn), "ffn")

    hw = d // 2
    assert d_wa.shape == (3, hw) and d_wb.shape == (31, hw) and d_bb.shape == (1, hw)
    small_parts = [d_meta, d_g1, d_gpm, d_g2, d_gpf, d_bb, d_lg, d_lb, loss_blk[0:1, :], d_wa, d_wb]
    places = [[(0, 0, 0, N_META)], [(16, 0, 0, 1)], [(17, 0, 0, 1)], [(18, 0, 0, 1)], [(19, 0, 0, 1)],
              [(20, 0, 0, 1)], [(20, hw, 0, 1)], [(21, 0, 0, 1)], [(21, hw, 0, 1)], [(22, 0, 0, 3)],
              [(22, hw, 0, 3), (25, 0, 3, 14), (25, hw, 17, 14)]]
    names_small = ["meta_tokens", "pre_mix_norm", "conv_a_w", "conv_b_w", "conv_b_bias", "ln_b_gain", "ln_b_bias",
                   "post_mix_norm", "pre_ffn_norm", "post_ffn_norm"]
    takes = [[(0, N_META, 0, 0, True)], [(0, 1, 16, 0, False)], [(0, 3, 22, 0, True)],
             [(0, 3, 22, hw, True), (3, 14, 25, 0, True), (17, 14, 25, hw, True)], [(0, 1, 20, 0, False)],
             [(0, 1, 20, hw, False)], [(0, 1, 21, 0, False)], [(0, 1, 17, 0, False)], [(0, 1, 18, 0, False)],
             [(0, 1, 19, 0, False)]]
    w_small = [meta_tokens, pre_mix_norm, conv_a_w[0], conv_b_w[0], conv_b_bias, ln_b_gain, ln_b_bias, post_mix_norm,
               pre_ffn_norm, post_ffn_norm]
    m_small = [m_meta_tokens, m_pre_mix_norm, m_conv_a_w[0], m_conv_b_w[0], m_conv_b_bias, m_ln_b_gain, m_ln_b_bias,
               m_post_mix_norm, m_pre_ffn_norm, m_post_ffn_norm]
    v_small = [v_meta_tokens, v_pre_mix_norm, v_conv_a_w[0], v_conv_b_w[0], v_conv_b_bias, v_ln_b_gain, v_ln_b_bias,
               v_post_mix_norm, v_pre_ffn_norm, v_post_ffn_norm]
    red = _small_allreduce(small_parts, places, SMALL_ROWS, d, last)
    g_s, d_s, m_s, v_s, loss_row = _small_update(red, q_arr, takes, (21, hw), w_small, m_small, v_small)
    loss = loss_row[0, 0]
    for i, nm in enumerate(names_small):
        lead = nm in ("conv_a_w", "conv_b_w")
        fix = (lambda a: a[None]) if lead else (lambda a: a)
        grads[nm], deltas[nm], new_m[nm], new_v[nm] = fix(g_s[i]), fix(d_s[i]), fix(m_s[i]), fix(v_s[i])

    parts2, m_recv = _chip_exchange_wait(m_ssem, m_rsem, parts2, m_lands, loss_row, "mix")
    halves2 = _chip_sum(parts2[0:1], m_recv[0:1], qc_arr, "in") + _chip_sum(parts2[1:2], m_recv[1:2], qc_arr, "out")
    gsum_mix = _pair_share_grads(halves2, "mix")
    update(["w_in"], [gsum_mix[0]], "w_in")
    update(["w_out"], [gsum_mix[1]], "w_out")

    order = ["meta_tokens", "pre_mix_norm", "w_in", "conv_a_w", "conv_b_w", "conv_b_bias", "ln_b_gain", "ln_b_bias", "w_out",
             "post_mix_norm", "pre_ffn_norm", "w_gate", "w_up", "w_down", "post_ffn_norm"]
    return (loss, grad_x, *[grads[k] for k in order], *[deltas[k] for k in order], *[new_m[k] for k in order],
            *[new_v[k] for k in order])
```

```python
import jax
import jax.numpy as jnp
from jax import lax
from jax.experimental import pallas as pl
from jax.experimental.pallas import tpu as pltpu
from jax.experimental.pallas import tpu_sc as plsc

F32 = jnp.float32
BF16 = jnp.bfloat16
MESH = pl.DeviceIdType.MESH

N_META = 16
TAIL_ROWS = 128
RMS_EPS = 1e-6
LN_EPS = 1e-5
ADAM_LR = 0.001
ADAM_B1 = 0.9
ADAM_B2 = 0.999
ADAM_EPS = 1e-08
ADAM_WD = 0.01
ADAM_STEP = 10

N_CHIPS = 4
LANES = 128
MXU_TILE = 256
CONV_CHUNK = 48
CONV_HIST = 32
ROW_TILE_CAP = 640
VMEM_LIMIT = 56 * 1024 * 1024

NN = (((1,), (0,)), ((), ()))
NT = (((1,), (1,)), ((), ()))
TN = (((0,), (0,)), ((), ()))


def _dot(a, b, dims=NN):
    return lax.dot_general(a, b, dims, preferred_element_type=F32)


def _sig(v):
    return 1.0 / (1.0 + jnp.exp(-v))


def _mean(v):
    return jnp.mean(v, axis=-1, keepdims=True)


def _row_tile(rows):
    best = 16
    for t in range(16, min(rows, ROW_TILE_CAP) + 1, 16):
        if rows % t == 0:
            best = t
    assert rows % best == 0
    return best


def _row_parts(tm):
    if tm % 32:
        return [slice(0, tm)]
    return [slice(0, tm // 2), slice(tm // 2, tm)]


def _concat_shards(w_ref, wcat_ref):
    n_sh, _, csh = w_ref.shape

    @pl.when(pl.program_id(0) == 0)
    def _():
        for k in range(n_sh):
            wcat_ref[:, k * csh:(k + 1) * csh] = w_ref[k]


def _params(semantics=None):
    kw = dict(vmem_limit_bytes=VMEM_LIMIT)
    if semantics is not None:
        kw["dimension_semantics"] = semantics
    return pltpu.CompilerParams(**kw)


def _full(shape):
    nd = len(shape)
    return pl.BlockSpec(shape, lambda *_: (0,) * nd)


def _resident(shape):
    nd = len(shape)
    return pl.BlockSpec(shape, lambda *_: (0,) * nd, pipeline_mode=pl.Buffered(1))


def _sds(shape, dtype):
    return jax.ShapeDtypeStruct(shape, dtype)


ANY = pl.BlockSpec(memory_space=pl.ANY)
VMEM = pl.BlockSpec(memory_space=pltpu.VMEM)


def _mesh_pos():
    return lax.axis_index("x"), lax.axis_index("y"), lax.axis_index("c")


def _flip(v, bit):
    return 1 - v if bit else v


def _mm_in(x, tail, win4, g1, after):
    seq, d = x.shape
    tp = seq + TAIL_ROWS
    tm = _row_tile(tp)
    n_sh, _, csh = win4.shape
    pw = n_sh * csh // 5

    def body(x_ref, tail_ref, w_ref, g_ref, after_ref, h_ref, xn_ref, hp_ref, wcat_ref):
        _concat_shards(w_ref, wcat_ref)
        rows = pl.program_id(0) * tm + lax.broadcasted_iota(jnp.int32, (tm, 1), 0)
        hh = jnp.where(rows < seq, x_ref[...], tail_ref[...])
        h_ref[...] = hh
        r = lax.rsqrt(_mean(hh * hh) + RMS_EPS)
        xn = (hh * r * g_ref[...]).astype(BF16)
        xn_ref[...] = xn
        for p in range(5):
            hp_ref[p] = _dot(xn, wcat_ref[:, p * pw:(p + 1) * pw])

    row = pl.BlockSpec((tm, d), lambda i: (i, 0))
    return pl.pallas_call(
        body, name="mm_in", grid=(tp // tm,),
        in_specs=[row, _full(tail.shape), _resident(win4.shape), _full(g1.shape), ANY],
        out_specs=[row, row, pl.BlockSpec((5, tm, pw), lambda i: (0, i, 0))],
        out_shape=[_sds((tp, d), F32), _sds((tp, d), BF16), _sds((5, tp, pw), F32)],
        scratch_shapes=[pltpu.VMEM((d, n_sh * csh), BF16)],
        compiler_params=_params(("arbitrary",)),
    )(x, tail, win4, g1, after)


def _seq_rows(tp):
    seq = tp - TAIL_ROWS
    nseq = seq + N_META
    assert nseq % CONV_CHUNK == 0 and seq % 16 == 0
    return seq, nseq


SUBLANES = 8


def _conv_offsets(width, transpose):
    return [(width - 1 - k) if transpose else (CONV_HIST - (width - 1) + k) for k in range(width)]


def _shift_copies(src_ref, sh_ref, width, transpose):
    n = src_ref.shape[0] - SUBLANES
    for s in sorted({o % SUBLANES for o in _conv_offsets(width, transpose)} - {0}):
        sh_ref[s - 1, 0:n, :] = src_ref[s:s + n, :]


def _tap_rows(src_ref, sh_ref, base, off):
    start = pl.multiple_of(base + (off // SUBLANES) * SUBLANES, SUBLANES)
    if off % SUBLANES == 0:
        return src_ref[pl.ds(start, CONV_CHUNK), :]
    return sh_ref[off % SUBLANES - 1, pl.ds(start, CONV_CHUNK), :]


def _conv_taps(src_ref, sh_ref, w_ref, dst_ref, width, nseq, transpose):
    w = w_ref[...]
    offs = _conv_offsets(width, transpose)
    _shift_copies(src_ref, sh_ref, width, transpose)

    def step(n, carry):
        out0 = pl.multiple_of(CONV_HIST + n * CONV_CHUNK, SUBLANES)
        base = out0 if transpose else n * CONV_CHUNK
        acc = jnp.zeros((CONV_CHUNK, w.shape[1]), F32)
        for k, off in enumerate(offs):
            acc = acc + w[k:k + 1, :] * _tap_rows(src_ref, sh_ref, base, off)
        dst_ref[pl.ds(out0, CONV_CHUNK), :] = acc
        return carry

    lax.fori_loop(0, nseq // CONV_CHUNK, step, 0)


def _conv_wgrad(src_ref, sh_ref, dz_ref, acc_ref, width, nseq):
    acc_ref[...] = jnp.zeros(acc_ref.shape, F32)
    offs = _conv_offsets(width, False)

    def step(n, carry):
        dzc = dz_ref[pl.ds(pl.multiple_of(CONV_HIST + n * CONV_CHUNK, SUBLANES), CONV_CHUNK), :]
        for k, off in enumerate(offs):
            prod = dzc * _tap_rows(src_ref, sh_ref, n * CONV_CHUNK, off)
            part = prod[0:SUBLANES, :]
            for s in range(1, CONV_CHUNK // SUBLANES):
                part = part + prod[SUBLANES * s:SUBLANES * (s + 1), :]
            acc_ref[SUBLANES * k:SUBLANES * (k + 1), :] += part
        return carry

    lax.fori_loop(0, nseq // CONV_CHUNK, step, 0)


def _to_seq(buf_ref, x_part, meta_part, seq):
    buf_ref[CONV_HIST:CONV_HIST + N_META, :] = meta_part
    buf_ref[CONV_HIST + N_META:CONV_HIST + N_META + seq, :] = x_part


def _zero_ends(buf_ref, nseq):
    zeros = jnp.zeros((CONV_HIST, buf_ref.shape[1]), F32)
    buf_ref[0:CONV_HIST, :] = zeros
    buf_ref[CONV_HIST + nseq:CONV_HIST + nseq + CONV_HIST, :] = zeros


def _mix_conv_fwd(hp5, wa, wb, bb):
    _, tp, wgrp = hp5.shape
    seq, nseq = _seq_rows(tp)
    sb = nseq + 2 * CONV_HIST
    ka, kb = wa.shape[0], wb.shape[0]
    xs, ms = slice(0, seq), slice(seq, seq + N_META)
    ox, om = slice(CONV_HIST + N_META, CONV_HIST + nseq), slice(CONV_HIST, CONV_HIST + N_META)

    def body(hp_ref, wa_ref, wb_ref, bb_ref, ya_ref, z_ref, s_ref, o_ref, sh_ref):
        _zero_ends(s_ref, nseq)
        _to_seq(s_ref, hp_ref[1, xs, :] * hp_ref[2, xs, :], hp_ref[1, ms, :] * hp_ref[2, ms, :], seq)
        _conv_taps(s_ref, sh_ref, wa_ref, o_ref, ka, nseq, False)
        ya_ref[xs, :] = (hp_ref[0, xs, :] * o_ref[ox, :]).astype(BF16)
        ya_ref[ms, :] = (hp_ref[0, ms, :] * o_ref[om, :]).astype(BF16)
        ya_ref[seq + N_META:tp, :] = jnp.zeros((tp - seq - N_META, LANES), BF16)
        _to_seq(s_ref, hp_ref[3, xs, :] * _sig(hp_ref[4, xs, :]), hp_ref[3, ms, :] * _sig(hp_ref[4, ms, :]), seq)
        _conv_taps(s_ref, sh_ref, wb_ref, o_ref, kb, nseq, False)
        z_ref[xs, :] = o_ref[ox, :] + bb_ref[...]
        z_ref[ms, :] = o_ref[om, :] + bb_ref[...]
        z_ref[seq + N_META:tp, :] = jnp.zeros((tp - seq - N_META, LANES), F32)

    col = lambda j: (0, j)
    return pl.pallas_call(
        body, name="mix_conv_fwd", grid=(wgrp // LANES,),
        in_specs=[pl.BlockSpec((5, tp, LANES), lambda j: (0, 0, j)), pl.BlockSpec((ka, LANES), col),
                  pl.BlockSpec((kb, LANES), col), pl.BlockSpec((1, LANES), col)],
        out_specs=[pl.BlockSpec((tp, LANES), col), pl.BlockSpec((tp, LANES), col)],
        out_shape=[_sds((tp, wgrp), BF16), _sds((tp, wgrp), F32)],
        scratch_shapes=[pltpu.VMEM((sb, LANES), F32), pltpu.VMEM((sb, LANES), F32),
                        pltpu.VMEM((SUBLANES - 1, sb, LANES), F32)],
        compiler_params=_params(("arbitrary",)),
    )(hp5, wa, wb, bb)


def _layer_norm_parts(z, lg, lb):
    mu = _mean(z)
    zc = z - mu
    rl = lax.rsqrt(_mean(zc * zc) + LN_EPS)
    zh = zc * rl
    return rl, zh, zh * lg + lb


def _mm_out(ya, z, h, wout, lg, lb, gpm, g2):
    tp, d = h.shape
    wa_ = ya.shape[1]
    tm = _row_tile(tp)

    def body(ya_ref, z_ref, h_ref, w_ref, lg_ref, lb_ref, gpm_ref, g2_ref, yb_ref, mix_ref, h1_ref, xn2_ref):
        for rs in _row_parts(tm):
            _, _, l = _layer_norm_parts(z_ref[rs, :], lg_ref[...], lb_ref[...])
            yb = (l * _sig(l)).astype(BF16)
            yb_ref[rs, :] = yb
            mix = _dot(ya_ref[rs, :], w_ref[0:wa_, :]) + _dot(yb, w_ref[wa_:d, :])
            mix_ref[rs, :] = mix
            rm = lax.rsqrt(_mean(mix * mix) + RMS_EPS)
            h1 = h_ref[rs, :] + mix * rm * gpm_ref[...]
            h1_ref[rs, :] = h1
            r2 = lax.rsqrt(_mean(h1 * h1) + RMS_EPS)
            xn2_ref[rs, :] = (h1 * r2 * g2_ref[...]).astype(BF16)

    row = lambda i: (i, 0)
    return pl.pallas_call(
        body, name="mm_out", grid=(tp // tm,),
        in_specs=[pl.BlockSpec((tm, wa_), row), pl.BlockSpec((tm, wa_), row), pl.BlockSpec((tm, d), row),
                  _resident(wout.shape), _full(lg.shape), _full(lb.shape), _full(gpm.shape), _full(g2.shape)],
        out_specs=[pl.BlockSpec((tm, wa_), row), pl.BlockSpec((tm, d), row), pl.BlockSpec((tm, d), row),
                   pl.BlockSpec((tm, d), row)],
        out_shape=[_sds((tp, wa_), BF16), _sds((tp, d), F32), _sds((tp, d), F32), _sds((tp, d), BF16)],
        compiler_params=_params(("arbitrary",)),
    )(ya, z, h, wout, lg, lb, gpm, g2)


def _ffn_up(xn2, wg, wu):
    tp, d = xn2.shape
    ff_dim = wg.shape[0]
    tm = _row_tile(tp)
    assert ff_dim % MXU_TILE == 0

    def body(xn_ref, wg_ref, wu_ref, p_ref, q_ref, f_ref):
        xn = xn_ref[...]
        for lo in range(0, ff_dim, MXU_TILE):
            cols = slice(lo, lo + MXU_TILE)
            a = _dot(xn, wg_ref[cols, :], NT)
            u = _dot(xn, wu_ref[cols, :], NT)
            s = _sig(a)
            q = a * s
            p_ref[:, cols] = (u * (s + q * (1.0 - s))).astype(BF16)
            q_ref[:, cols] = q.astype(BF16)
            f_ref[:, cols] = (q * u).astype(BF16)

    ospec = pl.BlockSpec((tm, ff_dim), lambda i: (i, 0))
    return pl.pallas_call(
        body, name="ffn_up", grid=(tp // tm,),
        in_specs=[pl.BlockSpec((tm, d), lambda i: (i, 0)), _resident(wg.shape), _resident(wu.shape)],
        out_specs=[ospec, ospec, ospec],
        out_shape=[_sds((tp, ff_dim), BF16)] * 3,
        compiler_params=_params(("arbitrary",)),
    )(xn2, wg, wu)


def _ffn_down(f, wd, h1, tgt, gpf):
    tp, ff_dim = f.shape
    d = h1.shape[1]
    tm = _row_tile(tp)
    seq, _ = _seq_rows(tp)

    def body(f_ref, w_ref, h1_ref, t_ref, gpf_ref, dff_ref, dh2_ref, loss_ref, dgpf_ref):
        i = pl.program_id(0)
        gpf_ = gpf_ref[...]

        @pl.when(i == 0)
        def _():
            loss_ref[...] = jnp.zeros(loss_ref.shape, F32)
            dgpf_ref[...] = jnp.zeros(dgpf_ref.shape, F32)

        for rs in _row_parts(tm):
            ff = _dot(f_ref[rs, :], w_ref[...])
            rf = lax.rsqrt(_mean(ff * ff) + RMS_EPS)
            nf = ff * rf
            h2 = h1_ref[rs, :] + nf * gpf_
            rows = i * tm + rs.start + lax.broadcasted_iota(jnp.int32, (rs.stop - rs.start, 1), 0)
            err = jnp.where(rows < seq, h2 - t_ref[rs, :], 0.0)
            dh2 = err * (1.0 / d)
            dh2_ref[rs, :] = dh2
            dn = dh2 * gpf_
            dff_ref[rs, :] = (rf * (dn - nf * _mean(dn * nf))).astype(BF16)
            loss_ref[...] += (0.5 / d) * jnp.sum(err * err, axis=(0, 1), keepdims=True)
            dgpf_ref[...] += jnp.sum(dh2 * nf, axis=0, keepdims=True)

    row = lambda i: (i, 0)
    return pl.pallas_call(
        body, name="ffn_down", grid=(tp // tm,),
        in_specs=[pl.BlockSpec((tm, ff_dim), row), _resident(wd.shape), pl.BlockSpec((tm, d), row),
                  pl.BlockSpec((tm, d), row), _full(gpf.shape)],
        out_specs=[pl.BlockSpec((tm, d), row), pl.BlockSpec((tm, d), row), _full((8, LANES)), _full((1, d))],
        out_shape=[_sds((tp, d), BF16), _sds((tp, d), F32), _sds((8, LANES), F32), _sds((1, d), F32)],
        compiler_params=_params(("arbitrary",)),
    )(f, wd, h1, tgt, gpf)


def _ffn_bwd_act(dff, wd, p, q):
    tp, d = dff.shape
    ff_dim = wd.shape[0]
    tm = _row_tile(tp)

    def body(dff_ref, w_ref, p_ref, q_ref, da_ref, du_ref):
        dffv = dff_ref[...]
        for lo in range(0, ff_dim, MXU_TILE):
            cols = slice(lo, lo + MXU_TILE)
            df = _dot(dffv, w_ref[cols, :], NT).astype(BF16)
            da_ref[:, cols] = df * p_ref[:, cols]
            du_ref[:, cols] = df * q_ref[:, cols]

    aspec = pl.BlockSpec((tm, ff_dim), lambda i: (i, 0))
    return pl.pallas_call(
        body, name="ffn_bwd_act", grid=(tp // tm,),
        in_specs=[pl.BlockSpec((tm, d), lambda i: (i, 0)), _resident(wd.shape), aspec, aspec],
        out_specs=[aspec, aspec],
        out_shape=[_sds((tp, ff_dim), BF16)] * 2,
        compiler_params=_params(("arbitrary",)),
    )(dff, wd, p, q)


def _grad_blocks(ff_dim):
    rows = ff_dim // 2
    assert rows % LANES == 0
    return rows


def _grad_w_down(f, dff):
    tp, ff_dim = f.shape
    d = dff.shape[1]
    rows = _grad_blocks(ff_dim)

    def body(f_ref, dff_ref, g_ref):
        g_ref[...] = _dot(f_ref[...], dff_ref[...], TN).astype(BF16)

    return pl.pallas_call(
        body, name="grad_w_down", grid=(ff_dim // rows,),
        in_specs=[pl.BlockSpec((tp, rows), lambda k: (0, k)), _resident(dff.shape)],
        out_specs=pl.BlockSpec((rows, d), lambda k: (k, 0)),
        out_shape=_sds((ff_dim, d), BF16),
        compiler_params=_params(("arbitrary",)),
    )(f, dff)


def _grad_w_gate_up(xn2, da, du):
    tp, ff_dim = da.shape
    d = xn2.shape[1]
    rows = _grad_blocks(ff_dim)

    def body(xn_ref, da_ref, du_ref, gg_ref, gu_ref):
        xn = xn_ref[...]
        gg_ref[...] = _dot(da_ref[...], xn, TN).astype(BF16)
        gu_ref[...] = _dot(du_ref[...], xn, TN).astype(BF16)

    aspec = pl.BlockSpec((tp, rows), lambda k: (0, k))
    gspec = pl.BlockSpec((rows, d), lambda k: (k, 0))
    return pl.pallas_call(
        body, name="grad_w_gate_up", grid=(ff_dim // rows,),
        in_specs=[_resident(xn2.shape), aspec, aspec],
        out_specs=[gspec, gspec],
        out_shape=[_sds((ff_dim, d), BF16)] * 2,
        compiler_params=_params(("arbitrary",)),
    )(xn2, da, du)


def _rms_bwd(dy, x, r, g):
    n = x * r
    dn = dy * g
    return r * (dn - n * _mean(dn * n)), dy * n


def _ffn_bwd_in(da, du, wg, wu, h1, mix, dh2, g2, gpm, after):
    tp, ff_dim = da.shape
    d = h1.shape[1]
    tm = _row_tile(tp)

    def body(da_ref, du_ref, wg_ref, wu_ref, h1_ref, mix_ref, dh2_ref, g2_ref, gpm_ref, after_ref,
             dh1_ref, dmix_ref, dg2_ref, dgpm_ref):
        i = pl.program_id(0)

        @pl.when(i == 0)
        def _():
            dg2_ref[...] = jnp.zeros(dg2_ref.shape, F32)
            dgpm_ref[...] = jnp.zeros(dgpm_ref.shape, F32)

        for rs in _row_parts(tm):
            dxn = _dot(da_ref[rs, :], wg_ref[...]) + _dot(du_ref[rs, :], wu_ref[...])
            h1v = h1_ref[rs, :]
            r2 = lax.rsqrt(_mean(h1v * h1v) + RMS_EPS)
            dres, dg2_rows = _rms_bwd(dxn, h1v, r2, g2_ref[...])
            dh1 = dh2_ref[rs, :] + dres
            dh1_ref[rs, :] = dh1
            mixv = mix_ref[rs, :]
            rm = lax.rsqrt(_mean(mixv * mixv) + RMS_EPS)
            dmix, dgpm_rows = _rms_bwd(dh1, mixv, rm, gpm_ref[...])
            dmix_ref[rs, :] = dmix.astype(BF16)
            dg2_ref[...] += jnp.sum(dg2_rows, axis=0, keepdims=True)
            dgpm_ref[...] += jnp.sum(dgpm_rows, axis=0, keepdims=True)

    aspec = pl.BlockSpec((tm, ff_dim), lambda i: (i, 0))
    row = pl.BlockSpec((tm, d), lambda i: (i, 0))
    return pl.pallas_call(
        body, name="ffn_bwd_in", grid=(tp // tm,),
        in_specs=[aspec, aspec, _resident(wg.shape), _resident(wu.shape), row, row, row, _full(g2.shape), _full(gpm.shape),
                  ANY],
        out_specs=[row, row, _full((1, d)), _full((1, d))],
        out_shape=[_sds((tp, d), F32), _sds((tp, d), BF16), _sds((1, d), F32), _sds((1, d), F32)],
        compiler_params=_params(("arbitrary",)),
    )(da, du, wg, wu, h1, mix, dh2, g2, gpm, after)


def _grad_w_out(ya, yb, dmix, after):
    tp, wa_ = ya.shape
    d = dmix.shape[1]

    def body(ya_ref, yb_ref, dmix_ref, after_ref, g_ref):
        dm = dmix_ref[...]
        g_ref[0:wa_, :] = _dot(ya_ref[...], dm, TN).astype(BF16)
        g_ref[wa_:2 * wa_, :] = _dot(yb_ref[...], dm, TN).astype(BF16)

    return pl.pallas_call(
        body, name="grad_w_out", grid=(1,),
        in_specs=[_full(ya.shape), _full(yb.shape), _full(dmix.shape), ANY],
        out_specs=_full((2 * wa_, d)),
        out_shape=_sds((2 * wa_, d), BF16),
        compiler_params=_params(("arbitrary",)),
    )(ya, yb, dmix, after)


def _mix_bwd_out(dmix, wout, z, lg, lb, after):
    tp, d = dmix.shape
    wa_ = z.shape[1]
    tm = _row_tile(tp)

    def body(dmix_ref, w_ref, z_ref, lg_ref, lb_ref, after_ref, dya_ref, dz_ref, dlg_ref, dlb_ref):
        i = pl.program_id(0)
        lg_ = lg_ref[...]

        @pl.when(i == 0)
        def _():
            dlg_ref[...] = jnp.zeros(dlg_ref.shape, F32)
            dlb_ref[...] = jnp.zeros(dlb_ref.shape, F32)

        for rs in _row_parts(tm):
            dm = dmix_ref[rs, :]
            dya_ref[rs, :] = _dot(dm, w_ref[0:wa_, :], NT)
            dyb = _dot(dm, w_ref[wa_:d, :], NT)
            rl, zh, l = _layer_norm_parts(z_ref[rs, :], lg_, lb_ref[...])
            sl = _sig(l)
            dl = dyb * (sl * (1.0 + l * (1.0 - sl)))
            dzh = dl * lg_
            dz_ref[rs, :] = rl * (dzh - _mean(dzh) - zh * _mean(dzh * zh))
            dlg_ref[...] += jnp.sum(dl * zh, axis=0, keepdims=True)
            dlb_ref[...] += jnp.sum(dl, axis=0, keepdims=True)

    row = lambda i: (i, 0)
    return pl.pallas_call(
        body, name="mix_bwd_out", grid=(tp // tm,),
        in_specs=[pl.BlockSpec((tm, d), row), _resident(wout.shape), pl.BlockSpec((tm, wa_), row), _full(lg.shape),
                  _full(lb.shape), ANY],
        out_specs=[pl.BlockSpec((tm, wa_), row), pl.BlockSpec((tm, wa_), row), _full((1, wa_)), _full((1, wa_))],
        out_shape=[_sds((tp, wa_), F32), _sds((tp, wa_), F32), _sds((1, wa_), F32), _sds((1, wa_), F32)],
        compiler_params=_params(("arbitrary",)),
    )(dmix, wout, z, lg, lb, after)


def _mix_conv_bwd(hp5, dya, dz, wa, wb):
    _, tp, wgrp = hp5.shape
    seq, nseq = _seq_rows(tp)
    sb = nseq + 2 * CONV_HIST
    ka, kb = wa.shape[0], wb.shape[0]
    xs, ms = slice(0, seq), slice(seq, seq + N_META)
    ox, om = slice(CONV_HIST + N_META, CONV_HIST + nseq), slice(CONV_HIST, CONV_HIST + N_META)
    n_tail = tp - seq - N_META

    def body(hp_ref, dya_ref, dz_ref, wa_ref, wb_ref, dhp_ref, dwa_ref, dwb_ref, dbb_ref, s_ref, d_ref, o_ref, acc_ref,
             shs_ref, shd_ref):
        _zero_ends(s_ref, nseq)
        _zero_ends(d_ref, nseq)

        def put(p, ox_val, om_val):
            dhp_ref[p, xs, :] = ox_val.astype(BF16)
            dhp_ref[p, ms, :] = om_val.astype(BF16)
            dhp_ref[p, seq + N_META:tp, :] = jnp.zeros((n_tail, LANES), BF16)

        def wgrad(dw_ref, width):
            for k in range(width):
                dw_ref[k:k + 1, :] = jnp.sum(acc_ref[8 * k:8 * k + 8, :], axis=0, keepdims=True)

        _to_seq(s_ref, hp_ref[1, xs, :] * hp_ref[2, xs, :], hp_ref[1, ms, :] * hp_ref[2, ms, :], seq)
        _conv_taps(s_ref, shs_ref, wa_ref, o_ref, ka, nseq, False)
        put(0, dya_ref[xs, :] * o_ref[ox, :], dya_ref[ms, :] * o_ref[om, :])
        _to_seq(d_ref, dya_ref[xs, :] * hp_ref[0, xs, :], dya_ref[ms, :] * hp_ref[0, ms, :], seq)
        _conv_wgrad(s_ref, shs_ref, d_ref, acc_ref, ka, nseq)
        wgrad(dwa_ref, ka)
        _conv_taps(d_ref, shd_ref, wa_ref, o_ref, ka, nseq, True)
        put(1, o_ref[ox, :] * hp_ref[2, xs, :], o_ref[om, :] * hp_ref[2, ms, :])
        put(2, o_ref[ox, :] * hp_ref[1, xs, :], o_ref[om, :] * hp_ref[1, ms, :])

        _to_seq(s_ref, hp_ref[3, xs, :] * _sig(hp_ref[4, xs, :]), hp_ref[3, ms, :] * _sig(hp_ref[4, ms, :]), seq)
        _to_seq(d_ref, dz_ref[xs, :], dz_ref[ms, :], seq)
        dbb_ref[...] = (jnp.sum(dz_ref[xs, :], axis=0, keepdims=True)
                        + jnp.sum(dz_ref[ms, :], axis=0, keepdims=True))
        _shift_copies(s_ref, shs_ref, kb, False)
        _conv_wgrad(s_ref, shs_ref, d_ref, acc_ref, kb, nseq)
        wgrad(dwb_ref, kb)
        _conv_taps(d_ref, shd_ref, wb_ref, o_ref, kb, nseq, True)
        sx, sm = _sig(hp_ref[4, xs, :]), _sig(hp_ref[4, ms, :])
        put(3, o_ref[ox, :] * sx, o_ref[om, :] * sm)
        put(4, o_ref[ox, :] * hp_ref[3, xs, :] * sx * (1.0 - sx), o_ref[om, :] * hp_ref[3, ms, :] * sm * (1.0 - sm))

    col = lambda j: (0, j)
    blk5 = pl.BlockSpec((5, tp, LANES), lambda j: (0, 0, j))
    return pl.pallas_call(
        body, name="mix_conv_bwd", grid=(wgrp // LANES,),
        in_specs=[blk5, pl.BlockSpec((tp, LANES), col), pl.BlockSpec((tp, LANES), col),
                  pl.BlockSpec((ka, LANES), col), pl.BlockSpec((kb, LANES), col)],
        out_specs=[blk5, pl.BlockSpec((ka, LANES), col), pl.BlockSpec((kb, LANES), col), pl.BlockSpec((1, LANES), col)],
        out_shape=[_sds((5, tp, wgrp), BF16), _sds((ka, wgrp), F32), _sds((kb, wgrp), F32), _sds((1, wgrp), F32)],
        scratch_shapes=[pltpu.VMEM((sb, LANES), F32), pltpu.VMEM((sb, LANES), F32), pltpu.VMEM((sb, LANES), F32),
                        pltpu.VMEM((SUBLANES * kb, LANES), F32), pltpu.VMEM((SUBLANES - 1, sb, LANES), F32),
                        pltpu.VMEM((SUBLANES - 1, sb, LANES), F32)],
        compiler_params=_params(("arbitrary",)),
    )(hp5, dya, dz, wa, wb)


def _grad_w_in(xn1, dhp5):
    n_p, tp, pw = dhp5.shape
    d = xn1.shape[1]

    def body(xn_ref, dhp_ref, g_ref):
        g_ref[...] = _dot(xn_ref[...], dhp_ref[0], TN).astype(BF16)

    return pl.pallas_call(
        body, name="grad_w_in", grid=(n_p,),
        in_specs=[_resident(xn1.shape), pl.BlockSpec((1, tp, pw), lambda p: (p, 0, 0))],
        out_specs=pl.BlockSpec((d, pw), lambda p: (0, p)),
        out_shape=_sds((d, n_p * pw), BF16),
        compiler_params=_params(("arbitrary",)),
    )(xn1, dhp5)


def _mix_bwd_in(dhp5, win4, h, dh1, g1, after):
    n_p, tp, pw = dhp5.shape
    d = h.shape[1]
    n_sh, _, csh = win4.shape
    tm = _row_tile(tp)

    seq, _ = _seq_rows(tp)
    last, meta_off = seq // tm, seq % tm
    assert last == tp // tm - 1
    assert any(rs.start <= meta_off and meta_off + N_META <= rs.stop for rs in _row_parts(tm))

    def body(dhp_ref, w_ref, h_ref, dh1_ref, g_ref, after_ref, gx_ref, dmeta_ref, dg1_ref, wcat_ref):
        i = pl.program_id(0)
        _concat_shards(w_ref, wcat_ref)

        @pl.when(i == 0)
        def _():
            dg1_ref[...] = jnp.zeros(dg1_ref.shape, F32)

        for rs in _row_parts(tm):
            dxn = _dot(dhp_ref[0, rs, :], wcat_ref[:, 0:pw], NT)
            for p in range(1, n_p):
                dxn = dxn + _dot(dhp_ref[p, rs, :], wcat_ref[:, p * pw:(p + 1) * pw], NT)
            hh = h_ref[rs, :]
            r1 = lax.rsqrt(_mean(hh * hh) + RMS_EPS)
            dres, dg_rows = _rms_bwd(dxn, hh, r1, g_ref[...])
            dh = dh1_ref[rs, :] + dres
            gx_ref[rs, :] = dh
            dg1_ref[...] += jnp.sum(dg_rows, axis=0, keepdims=True)
            if rs.start <= meta_off and meta_off + N_META <= rs.stop:
                @pl.when(i == last)
                def _():
                    dmeta_ref[...] = dh[meta_off - rs.start:meta_off - rs.start + N_META, :]

    row = lambda i: (i, 0)
    return pl.pallas_call(
        body, name="mix_bwd_in", grid=(tp // tm,),
        in_specs=[pl.BlockSpec((n_p, tm, pw), lambda i: (0, i, 0)), _resident(win4.shape), pl.BlockSpec((tm, d), row),
                  pl.BlockSpec((tm, d), row), _full(g1.shape), ANY],
        out_specs=[pl.BlockSpec((tm, d), row), _full((N_META, d)), _full((1, d))],
        out_shape=[_sds((seq, d), F32), _sds((N_META, d), F32), _sds((1, d), F32)],
        scratch_shapes=[pltpu.VMEM((d, n_sh * csh), BF16)],
        compiler_params=_params(("arbitrary",)),
    )(dhp5, win4, h, dh1, g1, after)


def _other_chips(x, y):
    out = []
    for j in (1, 2, 3):
        px, py = _flip(x, j >> 1), _flip(y, j & 1)
        out.append((px, py, 2 * px + py))
    return out


PAIR_COLLECTIVE_ID = 0


def _pair_barrier(x, y, c):
    sem = pltpu.get_barrier_semaphore()
    pl.semaphore_signal(sem, inc=1, device_id=(x, y, 1 - c), device_id_type=MESH)
    pl.semaphore_wait(sem, 1)


def _pair_params():
    return pltpu.CompilerParams(collective_id=PAIR_COLLECTIVE_ID)


def _half_rows(c, rows_half):
    return pl.ds(pl.multiple_of(c * rows_half, 8), rows_half)


def _cast_place(ws, q_arr, tag, after=None):
    n = len(ws)
    rows, cols = ws[0].shape
    tr = _row_tile(rows)
    extra = [] if after is None else [after]

    def body(q_ref, *refs):
        for w_ref, out_ref in zip(refs[:n], refs[n + len(extra):]):
            out_ref[0] = w_ref[...].astype(BF16)

    return list(pl.pallas_call(
        body, name="cast_place_" + tag,
        grid_spec=pltpu.PrefetchScalarGridSpec(
            num_scalar_prefetch=1, grid=(rows // tr,),
            in_specs=[pl.BlockSpec((tr, cols), lambda i, q: (i, 0))] * n + [ANY] * len(extra),
            out_specs=[pl.BlockSpec((1, tr, cols), lambda i, q: (q[0], i, 0))] * n),
        out_shape=[_sds((N_CHIPS, rows, cols), BF16)] * n,
        compiler_params=_params(("arbitrary",)),
    )(q_arr, *ws, *extra))


HBM = pl.BlockSpec(memory_space=pltpu.HBM)
SEM = pl.BlockSpec(memory_space=pltpu.SEMAPHORE)
EFFECT = pltpu.SideEffectType.DATAFLOW_SIDE_EFFECTING


def _in_hbm(a):
    return pltpu.with_memory_space_constraint(a, pltpu.HBM)


def _gather_start(fulls, after, tag):
    n = len(fulls)
    halves = [a.shape[1] // 2 for a in fulls]

    def body(*refs):
        land = refs[:n]
        ssem, rsem = refs[n + 1], refs[n + 2]
        token = refs[-1]
        x, y, c = _mesh_pos()
        q = 2 * x + y
        for i in range(n):
            for j, (px, py, _) in enumerate(_other_chips(x, y)):
                mine = land[i].at[q, _half_rows(c, halves[i]), :]
                pltpu.make_async_remote_copy(src_ref=mine, dst_ref=mine, send_sem=ssem.at[3 * i + j],
                                             recv_sem=rsem.at[3 * i + j], device_id=(px, py, c), device_id_type=MESH).start()
        token[...] = jnp.zeros(token.shape, F32)

    outs = pl.pallas_call(
        body, name="gather_start_" + tag,
        in_specs=[HBM] * n + [ANY], out_specs=[SEM, SEM] + [HBM] * n + [VMEM],
        out_shape=[pltpu.SemaphoreType.DMA((3 * n,)), pltpu.SemaphoreType.DMA((3 * n,))]
        + [pltpu.HBM(a.shape, a.dtype) for a in fulls] + [_sds((8, LANES), F32)],
        input_output_aliases={i: 2 + i for i in range(n)},
        compiler_params=pltpu.CompilerParams(has_side_effects=EFFECT),
    )(*[_in_hbm(a) for a in fulls], after)
    return outs[0], outs[1], list(outs[2:2 + n]), outs[-1]


def _gather_wait(which, ssem, rsem, lands, after, tag):
    m = len(which)
    halves = [a.shape[1] // 2 for a in lands]

    def body(*refs):
        land = refs[:m]
        ssem_, rsem_ = refs[m], refs[m + 1]
        x, y, c = _mesh_pos()
        for t, i in enumerate(which):
            for j, (px, py, qj) in enumerate(_other_chips(x, y)):
                rows = _half_rows(c, halves[t])
                cp = pltpu.make_async_remote_copy(src_ref=land[t].at[2 * x + y, rows, :], dst_ref=land[t].at[qj, rows, :],
                                                  send_sem=ssem_.at[3 * i + j], recv_sem=rsem_.at[3 * i + j],
                                                  device_id=(px, py, c), device_id_type=MESH)
                cp.wait_send()
                cp.wait_recv()

    outs = pl.pallas_call(
        body, name="gather_wait_" + tag,
        in_specs=[HBM] * m + [SEM, SEM, ANY], out_specs=[HBM] * m,
        out_shape=[pltpu.HBM(a.shape, a.dtype) for a in lands],
        input_output_aliases={i: i for i in range(m)},
        compiler_params=pltpu.CompilerParams(has_side_effects=EFFECT),
    )(*lands, ssem, rsem, after)
    return list(outs)


def _forward_pair(lands, tag):
    n = len(lands)
    halves = [a.shape[1] // 2 for a in lands]

    def body(*refs):
        full = refs[n:2 * n]
        ssem, rsem = refs[2 * n:]
        x, y, c = _mesh_pos()
        _pair_barrier(x, y, c)
        cps = []
        for i in range(n):
            for j, (_, _, qj) in enumerate(_other_chips(x, y)):
                part = full[i].at[qj, _half_rows(c, halves[i]), :]
                cp = pltpu.make_async_remote_copy(src_ref=part, dst_ref=part, send_sem=ssem.at[3 * i + j],
                                                  recv_sem=rsem.at[3 * i + j], device_id=(x, y, 1 - c), device_id_type=MESH)
                cp.start()
                cps.append(cp)
        for cp in cps:
            cp.wait()

    return pl.pallas_call(
        body, name="forward_pair_" + tag,
        in_specs=[ANY] * n, out_specs=[ANY] * n,
        out_shape=[_sds(a.shape, a.dtype) for a in lands],
        input_output_aliases={i: i for i in range(n)},
        scratch_shapes=[pltpu.SemaphoreType.DMA((3 * n,)), pltpu.SemaphoreType.DMA((3 * n,))],
        compiler_params=_pair_params(),
    )(*lands)


def _chip_exchange_start(parts, after, tag):
    n = len(parts)

    def body(*refs):
        src, land = refs[:n], refs[n:2 * n]
        ssem, rsem = refs[2 * n + 1], refs[2 * n + 2]
        token = refs[-1]
        x, y, c = _mesh_pos()
        for i in range(n):
            for j, (px, py, qj) in enumerate(_other_chips(x, y)):
                pltpu.make_async_remote_copy(src_ref=src[i].at[qj], dst_ref=land[i].at[j], send_sem=ssem.at[3 * i + j],
                                             recv_sem=rsem.at[3 * i + j], device_id=(px, py, c), device_id_type=MESH).start()
        token[...] = jnp.zeros(token.shape, F32)

    lands = [lax.empty((3,) + a.shape[1:], a.dtype) for a in parts]
    outs = pl.pallas_call(
        body, name="chip_exchange_start_" + tag,
        in_specs=[HBM] * (2 * n) + [ANY], out_specs=[SEM, SEM] + [HBM] * (2 * n) + [VMEM],
        out_shape=[pltpu.SemaphoreType.DMA((3 * n,)), pltpu.SemaphoreType.DMA((3 * n,))]
        + [pltpu.HBM(a.shape, a.dtype) for a in parts] + [pltpu.HBM(a.shape, a.dtype) for a in lands]
        + [_sds((8, LANES), F32)],
        input_output_aliases={i: 2 + i for i in range(2 * n)},
        compiler_params=pltpu.CompilerParams(has_side_effects=EFFECT),
    )(*[_in_hbm(a) for a in parts], *[_in_hbm(a) for a in lands], after)
    return outs[0], outs[1], list(outs[2:2 + n]), list(outs[2 + n:2 + 2 * n]), outs[-1]


def _chip_exchange_wait(ssem, rsem, parts, lands, after, tag):
    n = len(parts)

    def body(*refs):
        src, land = refs[:n], refs[n:2 * n]
        ssem_, rsem_ = refs[2 * n], refs[2 * n + 1]
        x, y, c = _mesh_pos()
        for i in range(n):
            for j, (px, py, qj) in enumerate(_other_chips(x, y)):
                cp = pltpu.make_async_remote_copy(src_ref=src[i].at[qj], dst_ref=land[i].at[j], send_sem=ssem_.at[3 * i + j],
                                                  recv_sem=rsem_.at[3 * i + j], device_id=(px, py, c), device_id_type=MESH)
                cp.wait_send()
                cp.wait_recv()

    outs = pl.pallas_call(
        body, name="chip_exchange_wait_" + tag,
        in_specs=[HBM] * (2 * n) + [SEM, SEM, ANY], out_specs=[HBM] * (2 * n),
        out_shape=[pltpu.HBM(a.shape, a.dtype) for a in parts] + [pltpu.HBM(a.shape, a.dtype) for a in lands],
        input_output_aliases={i: i for i in range(2 * n)},
        compiler_params=pltpu.CompilerParams(has_side_effects=EFFECT),
    )(*parts, *lands, ssem, rsem, after)
    return list(outs[:n]), list(outs[n:])


def _grad_half(ref, shape, axis, which):
    rows = shape[axis] // 2
    if axis == 0:
        return ref.at[_half_rows(which, rows), :]
    return ref.at[:, _half_rows(which, rows), :]


def _half_shape(a, axis):
    s = list(a.shape)
    s[axis] //= 2
    return tuple(s)


def _pair_exchange_start(grads, half_axis, after, tag):
    n = len(grads)

    def body(*refs):
        g, land = refs[:n], refs[n:2 * n]
        ssem, rsem = refs[2 * n + 1], refs[2 * n + 2]
        token = refs[-1]
        x, y, c = _mesh_pos()
        for i in range(n):
            pltpu.make_async_remote_copy(src_ref=_grad_half(g[i], grads[i].shape, half_axis[i], 1 - c), dst_ref=land[i],
                                         send_sem=ssem.at[i], recv_sem=rsem.at[i], device_id=(x, y, 1 - c),
                                         device_id_type=MESH).start()
        token[...] = jnp.zeros(token.shape, F32)

    lands = [lax.empty(_half_shape(a, half_axis[i]), a.dtype) for i, a in enumerate(grads)]
    outs = pl.pallas_call(
        body, name="pair_exchange_start_" + tag,
        in_specs=[HBM] * (2 * n) + [ANY], out_specs=[SEM, SEM] + [HBM] * (2 * n) + [VMEM],
        out_shape=[pltpu.SemaphoreType.DMA((n,)), pltpu.SemaphoreType.DMA((n,))]
        + [pltpu.HBM(a.shape, a.dtype) for a in grads] + [pltpu.HBM(a.shape, a.dtype) for a in lands]
        + [_sds((8, LANES), F32)],
        input_output_aliases={i: 2 + i for i in range(2 * n)},
        compiler_params=pltpu.CompilerParams(has_side_effects=EFFECT),
    )(*[_in_hbm(a) for a in grads], *[_in_hbm(a) for a in lands], after)
    return outs[0], outs[1], list(outs[2:2 + n]), list(outs[2 + n:2 + 2 * n]), outs[-1]


def _pair_exchange_wait(ssem, rsem, grads, lands, half_axis, after, tag):
    n = len(grads)

    def body(*refs):
        g, land = refs[:n], refs[n:2 * n]
        ssem_, rsem_ = refs[2 * n], refs[2 * n + 1]
        x, y, c = _mesh_pos()
        for i in range(n):
            cp = pltpu.make_async_remote_copy(src_ref=_grad_half(g[i], grads[i].shape, half_axis[i], 1 - c),
                                              dst_ref=land[i], send_sem=ssem_.at[i], recv_sem=rsem_.at[i],
                                              device_id=(x, y, 1 - c), device_id_type=MESH)
            cp.wait_send()
            cp.wait_recv()

    outs = pl.pallas_call(
        body, name="pair_exchange_wait_" + tag,
        in_specs=[HBM] * (2 * n) + [SEM, SEM, ANY], out_specs=[HBM] * (2 * n),
        out_shape=[pltpu.HBM(a.shape, a.dtype) for a in grads] + [pltpu.HBM(a.shape, a.dtype) for a in lands],
        input_output_aliases={i: i for i in range(2 * n)},
        compiler_params=pltpu.CompilerParams(has_side_effects=EFFECT),
    )(*grads, *lands, ssem, rsem, after)
    return list(outs[:n]), list(outs[n:])


def _pair_exchange_grads(grads, half_axis, tag):
    n = len(grads)

    def body(*refs):
        g, got = refs[:n], refs[n:2 * n]
        ssem, rsem = refs[2 * n:]
        x, y, c = _mesh_pos()
        _pair_barrier(x, y, c)
        cps = []
        for i in range(n):
            cp = pltpu.make_async_remote_copy(src_ref=_grad_half(g[i], grads[i].shape, half_axis[i], 1 - c),
                                              dst_ref=got[i], send_sem=ssem.at[i], recv_sem=rsem.at[i],
                                              device_id=(x, y, 1 - c), device_id_type=MESH)
            cp.start()
            cps.append(cp)
        for cp in cps:
            cp.wait()

    return pl.pallas_call(
        body, name="pair_exchange_grads_" + tag,
        in_specs=[ANY] * n, out_specs=[ANY] * n,
        out_shape=[_sds(_half_shape(a, half_axis[i]), a.dtype) for i, a in enumerate(grads)],
        scratch_shapes=[pltpu.SemaphoreType.DMA((n,)), pltpu.SemaphoreType.DMA((n,))],
        compiler_params=_pair_params(),
    )(*grads)


def _pair_sum(gs, gots, c_arr, col_sharded, tag):
    n = len(gs)
    if col_sharded:
        rows, cols = gs[0].shape
        rh, cs = rows // 2, cols // N_CHIPS
        g_spec = pl.BlockSpec((rh, cs), lambda k, c_ref: (c_ref[0], k))
        got_spec = pl.BlockSpec((rh, cs), lambda k, c_ref: (0, k))
    else:
        _, rows, cs = gs[0].shape
        rh = rows // 2
        g_spec = pl.BlockSpec((1, rh, cs), lambda k, c_ref: (k, c_ref[0], 0))
        got_spec = pl.BlockSpec((1, rh, cs), lambda k, c_ref: (k, 0, 0))

    def body(c_ref, *refs):
        for g_ref, got_ref, out_ref in zip(refs[:n], refs[n:2 * n], refs[2 * n:]):
            total = g_ref[...].astype(F32) + got_ref[...].astype(F32)
            out_ref[...] = total.astype(BF16).reshape(out_ref.shape)

    return list(pl.pallas_call(
        body, name="pair_sum_" + tag,
        grid_spec=pltpu.PrefetchScalarGridSpec(
            num_scalar_prefetch=1, grid=(N_CHIPS,), in_specs=[g_spec] * n + [got_spec] * n,
            out_specs=[pl.BlockSpec((1, rh, cs), lambda k, c_ref: (k, 0, 0))] * n),
        out_shape=[_sds((N_CHIPS, rh, cs), BF16)] * n,
        compiler_params=_params(("arbitrary",)),
    )(c_arr, *gs, *gots))


def _chip_sum(parts, gots, qc_arr, tag):
    n = len(parts)
    _, rh, cs = parts[0].shape
    steps = 2 if rh % 32 == 0 else 1
    rb = rh // steps

    def body(qc_ref, *refs):
        for part_ref, got_ref, out_ref in zip(refs[:n], refs[n:2 * n], refs[2 * n:]):
            total = part_ref[0].astype(F32)
            for j in range(3):
                total = total + got_ref[j].astype(F32)
            out_ref[...] = total

    return list(pl.pallas_call(
        body, name="chip_sum_" + tag,
        grid_spec=pltpu.PrefetchScalarGridSpec(
            num_scalar_prefetch=1, grid=(steps,),
            in_specs=[pl.BlockSpec((1, rb, cs), lambda i, qc: (qc[0], i, 0))] * n
            + [pl.BlockSpec((3, rb, cs), lambda i, qc: (0, i, 0))] * n,
            out_specs=[pl.BlockSpec((rb, cs), lambda i, qc: (qc[1] * steps + i, 0))] * n),
        out_shape=[_sds((2 * rh, cs), F32)] * n,
        compiler_params=_params(("arbitrary",)),
    )(qc_arr, *parts, *gots))


def _pair_share_grads(grads, tag):
    n = len(grads)

    def body(*refs):
        g = refs[n:2 * n]
        ssem, rsem = refs[2 * n:]
        x, y, c = _mesh_pos()
        _pair_barrier(x, y, c)
        cps = []
        for i in range(n):
            mine = g[i].at[_half_rows(c, grads[i].shape[0] // 2), :]
            cp = pltpu.make_async_remote_copy(src_ref=mine, dst_ref=mine, send_sem=ssem.at[i], recv_sem=rsem.at[i],
                                              device_id=(x, y, 1 - c), device_id_type=MESH)
            cp.start()
            cps.append(cp)
        for cp in cps:
            cp.wait()

    return pl.pallas_call(
        body, name="pair_share_grads_" + tag,
        in_specs=[ANY] * n, out_specs=[ANY] * n,
        out_shape=[_sds(a.shape, a.dtype) for a in grads],
        input_output_aliases={i: i for i in range(n)},
        scratch_shapes=[pltpu.SemaphoreType.DMA((n,)), pltpu.SemaphoreType.DMA((n,))],
        compiler_params=_pair_params(),
    )(*grads)


def _small_allreduce(parts, places, rows_total, width, after):
    n = len(parts)

    def body(*refs):
        ins, out_ref = refs[:n], refs[n + 1]
        pack, pair_got, chip_sum, got, ssem, rsem = refs[n + 2:]
        x, y, c = _mesh_pos()
        chip = 2 * x + y
        pack[...] = jnp.zeros(pack.shape, F32)
        for i in range(n):
            for row, col, src_row, rows in places[i]:
                w = parts[i].shape[1]
                pack[row:row + rows, col:col + w] = ins[i][src_row:src_row + rows, :]
        swap = pltpu.make_async_remote_copy(src_ref=pack, dst_ref=pair_got, send_sem=ssem.at[3], recv_sem=rsem.at[3],
                                            device_id=(x, y, 1 - c), device_id_type=MESH)
        swap.start()
        swap.wait()
        chip_sum[...] = pack[...] + pair_got[...]
        cps = []
        for j, (px, py, _) in enumerate(_other_chips(x, y)):
            cp = pltpu.make_async_remote_copy(src_ref=chip_sum, dst_ref=got.at[j], send_sem=ssem.at[j],
                                              recv_sem=rsem.at[j], device_id=(px, py, c), device_id_type=MESH)
            cp.start()
            cps.append(cp)
        for cp in cps:
            cp.wait()
        total = jnp.zeros(pack.shape, F32)
        for q in range(N_CHIPS):
            rel = jnp.bitwise_xor(chip, q)
            theirs = got[jnp.maximum(rel - 1, 0)]
            total = total + jnp.where(rel == 0, chip_sum[...], theirs)
        out_ref[...] = total

    return pl.pallas_call(
        body, name="small_allreduce",
        in_specs=[VMEM] * n + [ANY], out_specs=VMEM,
        out_shape=_sds((rows_total, width), F32),
        scratch_shapes=[pltpu.VMEM((rows_total, width), F32), pltpu.VMEM((rows_total, width), F32),
                        pltpu.VMEM((rows_total, width), F32), pltpu.VMEM((3, rows_total, width), F32),
                        pltpu.SemaphoreType.DMA((4,)), pltpu.SemaphoreType.DMA((4,))],
        compiler_params=_params(),
    )(*parts, after)


def _small_update(red, q_arr, takes, loss_at, ws, ms, vs):
    n_w = len(ws)

    def body(q_ref, red_ref, *refs):
        w_in, m_in, v_in = refs[0:n_w], refs[n_w:2 * n_w], refs[2 * n_w:3 * n_w]
        outs = refs[3 * n_w:]
        g_out, d_out, m_out, v_out = (outs[0:n_w], outs[n_w:2 * n_w], outs[2 * n_w:3 * n_w], outs[3 * n_w:4 * n_w])
        loss_ref = outs[4 * n_w]
        chip = q_ref[0]

        def take_own_columns(g_ref, d0, nr, s0, c0, w):
            for k in range(N_CHIPS):
                @pl.when(chip == k)
                def _():
                    g_ref[d0:d0 + nr, :] = red_ref[s0:s0 + nr, c0 + k * w:c0 + (k + 1) * w]

        for j in range(n_w):
            w = ws[j].shape[1]
            for d0, nr, s0, c0, sharded in takes[j]:
                if sharded:
                    take_own_columns(g_out[j], d0, nr, s0, c0, w)
                else:
                    g_out[j][d0:d0 + nr, :] = red_ref[s0:s0 + nr, c0:c0 + w]
            d_out[j][...], m_out[j][...], v_out[j][...] = _adamw_math(w_in[j][...], g_out[j][...], m_in[j][...], v_in[j][...])
        loss_ref[...] = red_ref[loss_at[0]:loss_at[0] + 1, loss_at[1]:loss_at[1] + LANES]

    shapes = [_sds(w.shape, F32) for w in ws]
    outs = pl.pallas_call(
        body, name="small_update",
        in_specs=[pl.BlockSpec(memory_space=pltpu.SMEM)] + [VMEM] * (1 + 3 * n_w), out_specs=[VMEM] * (4 * n_w + 1),
        out_shape=shapes * 4 + [_sds((1, LANES), F32)],
        compiler_params=_params(),
    )(q_arr, red, *ws, *ms, *vs)
    return outs[0:n_w], outs[n_w:2 * n_w], outs[2 * n_w:3 * n_w], outs[3 * n_w:4 * n_w], outs[4 * n_w]


def _adamw_math(w, g, m, v):
    m2 = ADAM_B1 * m + (1.0 - ADAM_B1) * g
    v2 = ADAM_B2 * v + (1.0 - ADAM_B2) * (g * g)
    m_hat = m2 / (1.0 - ADAM_B1 ** ADAM_STEP)
    v_hat = v2 / (1.0 - ADAM_B2 ** ADAM_STEP)
    delta = -ADAM_LR * (m_hat / (jnp.sqrt(v_hat) + ADAM_EPS) + ADAM_WD * w)
    return delta, m2, v2


SC_LANES = 16
SC_BLOCK = (8, 256)


def _adamw_sparsecore(ws, gs, ms, vs):
    n = len(ws)
    rows, cols = ws[0].shape
    br, bc = SC_BLOCK
    assert rows % br == 0 and cols % bc == 0

    def tile_body(*refs):
        ins, outs = refs[:4 * n], refs[4 * n:]

        def block(w_v, g_v, m_v, v_v, go_v, d_v, m2_v, v2_v):
            @pl.loop(0, br)
            def _(r):
                @pl.loop(0, bc, step=SC_LANES)
                def _(col):
                    at = (pl.ds(r, 1), pl.ds(col, SC_LANES))
                    gg = g_v.at[*at][...]
                    dd, mm, vv = _adamw_math(w_v.at[*at][...], gg, m_v.at[*at][...], v_v.at[*at][...])
                    go_v.at[*at][...] = gg
                    d_v.at[*at][...] = dd
                    m2_v.at[*at][...] = mm
                    v2_v.at[*at][...] = vv

        spec = pl.BlockSpec(block_shape=SC_BLOCK, index_map=lambda i, j: (i, j))
        for k in range(n):
            pltpu.emit_pipeline(
                block, grid=(rows // br, cols // bc), in_specs=[spec] * 4, out_specs=[spec] * 4,
                core_axis_name=("sc_core", "sc_tile"), dimension_semantics=(pltpu.PARALLEL, pltpu.PARALLEL),
            )(ins[k], ins[n + k], ins[2 * n + k], ins[3 * n + k], *outs[4 * k:4 * k + 4])

    outs = pl.kernel(
        tile_body, name="adamw_sparsecore",
        out_type=[_sds((rows, cols), F32)] * (4 * n),
        mesh=plsc.VectorSubcoreMesh(core_axis_name="sc_core", subcore_axis_name="sc_tile"),
        scratch_types=[],
    )(*ws, *gs, *ms, *vs)
    return [outs[4 * i:4 * i + 4] for i in range(n)]


ADAMW_BLOCK_BYTES = 3 * 2 ** 19


def _adamw_big(ws, gs, ms, vs, tag):
    n = len(ws)
    rows, cols = ws[0].shape
    tr = 16
    for t in range(16, rows + 1, 16):
        if rows % t == 0 and t * cols * 4 * n <= ADAMW_BLOCK_BYTES:
            tr = t

    def body(*refs):
        ins, outs = refs[:4 * n], refs[4 * n:]
        for i in range(n):
            w_ref, g_ref, m_ref, v_ref = ins[i], ins[n + i], ins[2 * n + i], ins[3 * n + i]
            gg = g_ref[...]
            outs[4 * i][...] = gg
            outs[4 * i + 1][...], outs[4 * i + 2][...], outs[4 * i + 3][...] = _adamw_math(
                w_ref[...], gg, m_ref[...], v_ref[...])

    spec = pl.BlockSpec((tr, cols), lambda i: (i, 0))
    outs = pl.pallas_call(
        body, name="adamw_" + tag, grid=(rows // tr,),
        in_specs=[spec] * (4 * n), out_specs=[spec] * (4 * n),
        out_shape=[_sds((rows, cols), F32)] * (4 * n),
        compiler_params=_params(("arbitrary",)),
    )(*ws, *gs, *ms, *vs)
    return [outs[4 * i:4 * i + 4] for i in range(n)]


SMALL_ROWS = 40
PACK_ROWS = 64


def kernel(x, meta_tokens, pre_mix_norm, w_in, conv_a_w, conv_b_w, conv_b_bias, ln_b_gain, ln_b_bias, w_out, post_mix_norm, pre_ffn_norm, w_gate, w_up, w_down, post_ffn_norm, loss_target, m_meta_tokens, m_pre_mix_norm, m_w_in, m_conv_a_w, m_conv_b_w, m_conv_b_bias, m_ln_b_gain, m_ln_b_bias, m_w_out, m_post_mix_norm, m_pre_ffn_norm, m_w_gate, m_w_up, m_w_down, m_post_ffn_norm, v_meta_tokens, v_pre_mix_norm, v_w_in, v_conv_a_w, v_conv_b_w, v_conv_b_bias, v_ln_b_gain, v_ln_b_bias, v_w_out, v_post_mix_norm, v_pre_ffn_norm, v_w_gate, v_w_up, v_w_down, v_post_ffn_norm):
    xq, yq, cq = lax.axis_index("x"), lax.axis_index("y"), lax.axis_index("c")
    chip = 2 * xq + yq
    c_arr = jnp.reshape(cq, (1,)).astype(jnp.int32)
    qc_arr = jnp.stack([chip, cq]).astype(jnp.int32)

    seq, d = x.shape[1], x.shape[2]
    x2, tgt2 = x[0], loss_target[0]
    tr = lambda a: jnp.swapaxes(a, 1, 2)[0]
    w_in2, w_out2, w_gate2, w_up2, w_down2 = w_in[0], w_out[0], tr(w_gate), tr(w_up), w_down[0]
    ka, wa_sh = conv_a_w.shape[1], conv_a_w.shape[2]
    kb = conv_b_w.shape[1]
    meta_sh = meta_tokens.shape[1]

    small = jnp.zeros((PACK_ROWS, meta_sh), F32)
    small = small.at[0:N_META, :].set(meta_tokens)
    small = small.at[16:16 + ka, 0:wa_sh].set(conv_a_w[0])
    small = small.at[24:24 + kb, 0:wa_sh].set(conv_b_w[0])
    q_arr = jnp.reshape(chip, (1,)).astype(jnp.int32)
    small_own = lax.dynamic_update_slice(jnp.zeros((N_CHIPS, PACK_ROWS, meta_sh), F32), small[None], (chip, 0, 0))
    i_ssem, i_rsem, first, i_token = _gather_start(_cast_place([w_in2], q_arr, "w_in") + [small_own], pre_mix_norm, "in")
    rest = _cast_place([w_out2], q_arr, "w_out", i_token) + _cast_place([w_gate2, w_up2, w_down2], q_arr, "ffn", i_token)
    g_ssem, g_rsem, lands, g_token = _gather_start(rest, i_token, "rest")
    win4, small4 = _forward_pair(_gather_wait([0, 1], i_ssem, i_rsem, first, g_token, "in"), "in")
    meta_f = jnp.concatenate([small4[k, 0:N_META, :] for k in range(N_CHIPS)], axis=1)
    wa_f = jnp.concatenate([small4[k, 16:16 + ka, 0:wa_sh] for k in range(N_CHIPS)], axis=1)
    wb_f = jnp.concatenate([small4[k, 24:24 + kb, 0:wa_sh] for k in range(N_CHIPS)], axis=1)

    tm = _row_tile(seq + TAIL_ROWS)
    tail = lax.dynamic_update_slice(jnp.zeros((tm, d), F32), meta_f, (seq % tm, 0))
    h, xn1, hp5 = _mm_in(x2, tail, win4, pre_mix_norm, g_token)
    ya, z = _mix_conv_fwd(hp5, wa_f, wb_f, conv_b_bias)
    (wout4,) = _forward_pair(_gather_wait([0], g_ssem, g_rsem, lands[0:1], z, "out"), "out")
    wout_f = wout4.reshape(N_CHIPS * wout4.shape[1], wout4.shape[2])
    yb, mix, h1, xn2 = _mm_out(ya, z, h, wout_f, ln_b_gain, ln_b_bias, post_mix_norm, pre_ffn_norm)
    wg4, wu4 = _forward_pair(_gather_wait([1, 2], g_ssem, g_rsem, lands[1:3], xn2, "gate_up"), "gate_up")
    stacked = lambda a: a.reshape(a.shape[0] * a.shape[1], a.shape[2])
    wg_f, wu_f = stacked(wg4), stacked(wu4)
    p_act, q_act, f_act = _ffn_up(xn2, wg_f, wu_f)
    (wd4,) = _forward_pair(_gather_wait([3], g_ssem, g_rsem, lands[3:4], f_act, "down"), "down")
    wd_f = stacked(wd4)
    dff, dh2, loss_blk, d_gpf = _ffn_down(f_act, wd_f, h1, tgt2, post_ffn_norm)

    da, du = _ffn_bwd_act(dff, wd_f, p_act, q_act)
    by_chip = lambda g: g.reshape(N_CHIPS, g.shape[0] // N_CHIPS, g.shape[1])
    g_down = by_chip(_grad_w_down(f_act, dff))
    g_gate, g_up = [by_chip(g) for g in _grad_w_gate_up(xn2, da, du)]
    ffn = [g_gate, g_up, g_down]
    p_ssem, p_rsem, ffn, p_lands, p_token = _pair_exchange_start(ffn, [1, 1, 1], dff, "ffn")
    dh1, dmix, d_g2, d_gpm = _ffn_bwd_in(da, du, wg_f, wu_f, h1, mix, dh2, pre_ffn_norm, post_mix_norm, p_token)
    ffn, got = _pair_exchange_wait(p_ssem, p_rsem, ffn, p_lands, [1, 1, 1], d_g2, "ffn")
    parts = _pair_sum(ffn, got, c_arr, False, "ffn")
    f_ssem, f_rsem, parts, f_lands, f_token = _chip_exchange_start(parts, dff, "ffn")
    g_out = _grad_w_out(ya, yb, dmix, f_token)
    dya, dz, d_lg, d_lb = _mix_bwd_out(dmix, wout_f, z, ln_b_gain, ln_b_bias, f_token)
    dhp5, d_wa, d_wb, d_bb = _mix_conv_bwd(hp5, dya, dz, wa_f, wb_f)
    g_in = _grad_w_in(xn1, dhp5)

    g_out4 = g_out.reshape(N_CHIPS, g_out.shape[0] // N_CHIPS, g_out.shape[1])
    mixw = [g_in, g_out4]
    got2 = _pair_exchange_grads(mixw, [0, 1], "mix")
    parts2 = _pair_sum(mixw[0:1], got2[0:1], c_arr, True, "in") + _pair_sum(mixw[1:2], got2[1:2], c_arr, False, "out")
    m_ssem, m_rsem, parts2, m_lands, m_token = _chip_exchange_start(parts2, dhp5, "mix")
    grad_x2, d_meta, d_g1 = _mix_bwd_in(dhp5, win4, h, dh1, pre_mix_norm, m_token)
    grad_x = grad_x2[None]

    parts, f_recv = _chip_exchange_wait(f_ssem, f_rsem, parts, f_lands, d_g1, "ffn")
    halves = _chip_sum(parts, f_recv, qc_arr, "ffn")
    gsum_ffn = _pair_share_grads(halves, "ffn")

    names_big = ["w_in", "w_out", "w_gate", "w_up", "w_down"]
    w_big = dict(zip(names_big, [w_in2, w_out2, w_gate2, w_up2, w_down2]))
    m_big = dict(zip(names_big, [m_w_in[0], m_w_out[0], tr(m_w_gate), tr(m_w_up), m_w_down[0]]))
    v_big = dict(zip(names_big, [v_w_in[0], v_w_out[0], tr(v_w_gate), tr(v_w_up), v_w_down[0]]))
    grads, deltas, new_m, new_v = {}, {}, {}, {}

    def update(names, gs, tag):
        operands = ([w_big[k] for k in names], gs, [m_big[k] for k in names], [v_big[k] for k in names])
        res = _adamw_sparsecore(*operands) if tag == "ffn" else _adamw_big(*operands, tag)
        for nm, outs in zip(names, res):
            if nm in ("w_gate", "w_up"):
                outs = [jnp.swapaxes(o[None], 1, 2) for o in outs]
            else:
                outs = [o[None] for o in outs]
            grads[nm], deltas[nm], new_m[nm], new_v[nm] = outs
        return res[-1][1]

    update(["w_gate", "w_up", "w_down"], list(gsum_ffn), "ffn")
    last = gsum_ffn[2]

    hw = d // 2
    assert d_wa.shape == (3, hw) and d_wb.shape == (31, hw) and d_bb.shape == (1, hw)
    small_parts = [d_meta, d_g1, d_gpm, d_g2, d_gpf, d_bb, d_lg, d_lb, loss_blk[0:1, :], d_wa, d_wb]
    places = [[(0, 0, 0, N_META)], [(16, 0, 0, 1)], [(17, 0, 0, 1)], [(18, 0, 0, 1)], [(19, 0, 0, 1)],
              [(20, 0, 0, 1)], [(20, hw, 0, 1)], [(21, 0, 0, 1)], [(21, hw, 0, 1)], [(22, 0, 0, 3)],
              [(22, hw, 0, 3), (25, 0, 3, 14), (25, hw, 17, 14)]]
    names_small = ["meta_tokens", "pre_mix_norm", "conv_a_w", "conv_b_w", "conv_b_bias", "ln_b_gain", "ln_b_bias",
                   "post_mix_norm", "pre_ffn_norm", "post_ffn_norm"]
    takes = [[(0, N_META, 0, 0, True)], [(0, 1, 16, 0, False)], [(0, 3, 22, 0, True)],
             [(0, 3, 22, hw, True), (3, 14, 25, 0, True), (17, 14, 25, hw, True)], [(0, 1, 20, 0, False)],
             [(0, 1, 20, hw, False)], [(0, 1, 21, 0, False)], [(0, 1, 17, 0, False)], [(0, 1, 18, 0, False)],
             [(0, 1, 19, 0, False)]]
    w_small = [meta_tokens, pre_mix_norm, conv_a_w[0], conv_b_w[0], conv_b_bias, ln_b_gain, ln_b_bias, post_mix_norm,
               pre_ffn_norm, post_ffn_norm]
    m_small = [m_meta_tokens, m_pre_mix_norm, m_conv_a_w[0], m_conv_b_w[0], m_conv_b_bias, m_ln_b_gain, m_ln_b_bias,
               m_post_mix_norm, m_pre_ffn_norm, m_post_ffn_norm]
    v_small = [v_meta_tokens, v_pre_mix_norm, v_conv_a_w[0], v_conv_b_w[0], v_conv_b_bias, v_ln_b_gain, v_ln_b_bias,
               v_post_mix_norm, v_pre_ffn_norm, v_post_ffn_norm]
    red = _small_allreduce(small_parts, places, SMALL_ROWS, d, last)
    g_s, d_s, m_s, v_s, loss_row = _small_update(red, q_arr, takes, (21, hw), w_small, m_small, v_small)
    loss = loss_row[0, 0]
    for i, nm in enumerate(names_small):
        lead = nm in ("conv_a_w", "conv_b_w")
        fix = (lambda a: a[None]) if lead else (lambda a: a)
        grads[nm], deltas[nm], new_m[nm], new_v[nm] = fix(g_s[i]), fix(d_s[i]), fix(m_s[i]), fix(v_s[i])

    parts2, m_recv = _chip_exchange_wait(m_ssem, m_rsem, parts2, m_lands, loss_row, "mix")
    halves2 = _chip_sum(parts2[0:1], m_recv[0:1], qc_arr, "in") + _chip_sum(parts2[1:2], m_recv[1:2], qc_arr, "out")
    gsum_mix = _pair_share_grads(halves2, "mix")
    update(["w_in"], [gsum_mix[0]], "w_in")
    update(["w_out"], [gsum_mix[1]], "w_out")

    order = ["meta_tokens", "pre_mix_norm", "w_in", "conv_a_w", "conv_b_w", "conv_b_bias", "ln_b_gain", "ln_b_bias", "w_out",
             "post_mix_norm", "pre_ffn_norm", "w_gate", "w_up", "w_down", "post_ffn_norm"]
    return (loss, grad_x, *[grads[k] for k in order], *[deltas[k] for k in order], *[new_m[k] for k in order],
            *[new_v[k] for k in order])
```

```python
import jax
import jax.numpy as jnp
from jax import lax
from jax.experimental import pallas as pl
from jax.experimental.pallas import tpu as pltpu
from jax.experimental.pallas import tpu_sc as plsc

F32 = jnp.float32
BF16 = jnp.bfloat16
MESH = pl.DeviceIdType.MESH

N_META = 16
TAIL_ROWS = 128
RMS_EPS = 1e-6
LN_EPS = 1e-5
ADAM_LR = 0.001
ADAM_B1 = 0.9
ADAM_B2 = 0.999
ADAM_EPS = 1e-08
ADAM_WD = 0.01
ADAM_STEP = 10

N_CHIPS = 4
LANES = 128
MXU_TILE = 256
CONV_CHUNK = 48
CONV_HIST = 32
ROW_TILE_CAP = 640
VMEM_LIMIT = 56 * 1024 * 1024

NN = (((1,), (0,)), ((), ()))
NT = (((1,), (1,)), ((), ()))
TN = (((0,), (0,)), ((), ()))


def _dot(a, b, dims=NN):
    return lax.dot_general(a, b, dims, preferred_element_type=F32)


def _sig(v):
    return 1.0 / (1.0 + jnp.exp(-v))


def _mean(v):
    return jnp.mean(v, axis=-1, keepdims=True)


def _row_tile(rows):
    best = 16
    for t in range(16, min(rows, ROW_TILE_CAP) + 1, 16):
        if rows % t == 0:
            best = t
    assert rows % best == 0
    return best


def _row_parts(tm):
    if tm % 32:
        return [slice(0, tm)]
    return [slice(0, tm // 2), slice(tm // 2, tm)]


def _concat_shards(w_ref, wcat_ref):
    n_sh, _, csh = w_ref.shape

    @pl.when(pl.program_id(0) == 0)
    def _():
        for k in range(n_sh):
            wcat_ref[:, k * csh:(k + 1) * csh] = w_ref[k]


def _params(semantics=None):
    kw = dict(vmem_limit_bytes=VMEM_LIMIT)
    if semantics is not None:
        kw["dimension_semantics"] = semantics
    return pltpu.CompilerParams(**kw)


def _full(shape):
    nd = len(shape)
    return pl.BlockSpec(shape, lambda *_: (0,) * nd)


def _resident(shape):
    nd = len(shape)
    return pl.BlockSpec(shape, lambda *_: (0,) * nd, pipeline_mode=pl.Buffered(1))


def _sds(shape, dtype):
    return jax.ShapeDtypeStruct(shape, dtype)


ANY = pl.BlockSpec(memory_space=pl.ANY)
VMEM = pl.BlockSpec(memory_space=pltpu.VMEM)


def _mesh_pos():
    return lax.axis_index("x"), lax.axis_index("y"), lax.axis_index("c")


def _flip(v, bit):
    return 1 - v if bit else v


def _mm_in(x, tail, win4, g1, after):
    seq, d = x.shape
    tp = seq + TAIL_ROWS
    tm = _row_tile(tp)
    n_sh, _, csh = win4.shape
    pw = n_sh * csh // 5

    def body(x_ref, tail_ref, w_ref, g_ref, after_ref, h_ref, xn_ref, hp_ref, wcat_ref):
        _concat_shards(w_ref, wcat_ref)
        rows = pl.program_id(0) * tm + lax.broadcasted_iota(jnp.int32, (tm, 1), 0)
        hh = jnp.where(rows < seq, x_ref[...], tail_ref[...])
        h_ref[...] = hh
        r = lax.rsqrt(_mean(hh * hh) + RMS_EPS)
        xn = (hh * r * g_ref[...]).astype(BF16)
        xn_ref[...] = xn
        for p in range(5):
            hp_ref[p] = _dot(xn, wcat_ref[:, p * pw:(p + 1) * pw])

    row = pl.BlockSpec((tm, d), lambda i: (i, 0))
    return pl.pallas_call(
        body, name="mm_in", grid=(tp // tm,),
        in_specs=[row, _full(tail.shape), _resident(win4.shape), _full(g1.shape), ANY],
        out_specs=[row, row, pl.BlockSpec((5, tm, pw), lambda i: (0, i, 0))],
        out_shape=[_sds((tp, d), F32), _sds((tp, d), BF16), _sds((5, tp, pw), F32)],
        scratch_shapes=[pltpu.VMEM((d, n_sh * csh), BF16)],
        compiler_params=_params(("arbitrary",)),
    )(x, tail, win4, g1, after)


def _seq_rows(tp):
    seq = tp - TAIL_ROWS
    nseq = seq + N_META
    assert nseq % CONV_CHUNK == 0 and seq % 16 == 0
    return seq, nseq


SUBLANES = 8


def _conv_offsets(width, transpose):
    return [(width - 1 - k) if transpose else (CONV_HIST - (width - 1) + k) for k in range(width)]


def _shift_copies(src_ref, sh_ref, width, transpose):
    n = src_ref.shape[0] - SUBLANES
    for s in sorted({o % SUBLANES for o in _conv_offsets(width, transpose)} - {0}):
        sh_ref[s - 1, 0:n, :] = src_ref[s:s + n, :]


def _tap_rows(src_ref, sh_ref, base, off):
    start = pl.multiple_of(base + (off // SUBLANES) * SUBLANES, SUBLANES)
    if off % SUBLANES == 0:
        return src_ref[pl.ds(start, CONV_CHUNK), :]
    return sh_ref[off % SUBLANES - 1, pl.ds(start, CONV_CHUNK), :]


def _conv_taps(src_ref, sh_ref, w_ref, dst_ref, width, nseq, transpose):
    w = w_ref[...]
    offs = _conv_offsets(width, transpose)
    _shift_copies(src_ref, sh_ref, width, transpose)

    def step(n, carry):
        out0 = pl.multiple_of(CONV_HIST + n * CONV_CHUNK, SUBLANES)
        base = out0 if transpose else n * CONV_CHUNK
        acc = jnp.zeros((CONV_CHUNK, w.shape[1]), F32)
        for k, off in enumerate(offs):
            acc = acc + w[k:k + 1, :] * _tap_rows(src_ref, sh_ref, base, off)
        dst_ref[pl.ds(out0, CONV_CHUNK), :] = acc
        return carry

    lax.fori_loop(0, nseq // CONV_CHUNK, step, 0)


def _conv_wgrad(src_ref, sh_ref, dz_ref, acc_ref, width, nseq):
    acc_ref[...] = jnp.zeros(acc_ref.shape, F32)
    offs = _conv_offsets(width, False)

    def step(n, carry):
        dzc = dz_ref[pl.ds(pl.multiple_of(CONV_HIST + n * CONV_CHUNK, SUBLANES), CONV_CHUNK), :]
        for k, off in enumerate(offs):
            prod = dzc * _tap_rows(src_ref, sh_ref, n * CONV_CHUNK, off)
            part = prod[0:SUBLANES, :]
            for s in range(1, CONV_CHUNK // SUBLANES):
                part = part + prod[SUBLANES * s:SUBLANES * (s + 1), :]
            acc_ref[SUBLANES * k:SUBLANES * (k + 1), :] += part
        return carry

    lax.fori_loop(0, nseq // CONV_CHUNK, step, 0)


def _to_seq(buf_ref, x_part, meta_part, seq):
    buf_ref[CONV_HIST:CONV_HIST + N_META, :] = meta_part
    buf_ref[CONV_HIST + N_META:CONV_HIST + N_META + seq, :] = x_part


def _zero_ends(buf_ref, nseq):
    zeros = jnp.zeros((CONV_HIST, buf_ref.shape[1]), F32)
    buf_ref[0:CONV_HIST, :] = zeros
    buf_ref[CONV_HIST + nseq:CONV_HIST + nseq + CONV_HIST, :] = zeros


def _mix_conv_fwd(hp5, wa, wb, bb):
    _, tp, wgrp = hp5.shape
    seq, nseq = _seq_rows(tp)
    sb = nseq + 2 * CONV_HIST
    ka, kb = wa.shape[0], wb.shape[0]
    xs, ms = slice(0, seq), slice(seq, seq + N_META)
    ox, om = slice(CONV_HIST + N_META, CONV_HIST + nseq), slice(CONV_HIST, CONV_HIST + N_META)

    def body(hp_ref, wa_ref, wb_ref, bb_ref, ya_ref, z_ref, s_ref, o_ref, sh_ref):
        _zero_ends(s_ref, nseq)
        _to_seq(s_ref, hp_ref[1, xs, :] * hp_ref[2, xs, :], hp_ref[1, ms, :] * hp_ref[2, ms, :], seq)
        _conv_taps(s_ref, sh_ref, wa_ref, o_ref, ka, nseq, False)
        ya_ref[xs, :] = (hp_ref[0, xs, :] * o_ref[ox, :]).astype(BF16)
        ya_ref[ms, :] = (hp_ref[0, ms, :] * o_ref[om, :]).astype(BF16)
        ya_ref[seq + N_META:tp, :] = jnp.zeros((tp - seq - N_META, LANES), BF16)
        _to_seq(s_ref, hp_ref[3, xs, :] * _sig(hp_ref[4, xs, :]), hp_ref[3, ms, :] * _sig(hp_ref[4, ms, :]), seq)
        _conv_taps(s_ref, sh_ref, wb_ref, o_ref, kb, nseq, False)
        z_ref[xs, :] = o_ref[ox, :] + bb_ref[...]
        z_ref[ms, :] = o_ref[om, :] + bb_ref[...]
        z_ref[seq + N_META:tp, :] = jnp.zeros((tp - seq - N_META, LANES), F32)

    col = lambda j: (0, j)
    return pl.pallas_call(
        body, name="mix_conv_fwd", grid=(wgrp // LANES,),
        in_specs=[pl.BlockSpec((5, tp, LANES), lambda j: (0, 0, j)), pl.BlockSpec((ka, LANES), col),
                  pl.BlockSpec((kb, LANES), col), pl.BlockSpec((1, LANES), col)],
        out_specs=[pl.BlockSpec((tp, LANES), col), pl.BlockSpec((tp, LANES), col)],
        out_shape=[_sds((tp, wgrp), BF16), _sds((tp, wgrp), F32)],
        scratch_shapes=[pltpu.VMEM((sb, LANES), F32), pltpu.VMEM((sb, LANES), F32),
                        pltpu.VMEM((SUBLANES - 1, sb, LANES), F32)],
        compiler_params=_params(("arbitrary",)),
    )(hp5, wa, wb, bb)


def _layer_norm_parts(z, lg, lb):
    mu = _mean(z)
    zc = z - mu
    rl = lax.rsqrt(_mean(zc * zc) + LN_EPS)
    zh = zc * rl
    return rl, zh, zh * lg + lb


def _mm_out(ya, z, h, wout, lg, lb, gpm, g2):
    tp, d = h.shape
    wa_ = ya.shape[1]
    tm = _row_tile(tp)

    def body(ya_ref, z_ref, h_ref, w_ref, lg_ref, lb_ref, gpm_ref, g2_ref, yb_ref, mix_ref, h1_ref, xn2_ref):
        for rs in _row_parts(tm):
            _, _, l = _layer_norm_parts(z_ref[rs, :], lg_ref[...], lb_ref[...])
            yb = (l * _sig(l)).astype(BF16)
            yb_ref[rs, :] = yb
            mix = _dot(ya_ref[rs, :], w_ref[0:wa_, :]) + _dot(yb, w_ref[wa_:d, :])
            mix_ref[rs, :] = mix
            rm = lax.rsqrt(_mean(mix * mix) + RMS_EPS)
            h1 = h_ref[rs, :] + mix * rm * gpm_ref[...]
            h1_ref[rs, :] = h1
            r2 = lax.rsqrt(_mean(h1 * h1) + RMS_EPS)
            xn2_ref[rs, :] = (h1 * r2 * g2_ref[...]).astype(BF16)

    row = lambda i: (i, 0)
    return pl.pallas_call(
        body, name="mm_out", grid=(tp // tm,),
        in_specs=[pl.BlockSpec((tm, wa_), row), pl.BlockSpec((tm, wa_), row), pl.BlockSpec((tm, d), row),
                  _resident(wout.shape), _full(lg.shape), _full(lb.shape), _full(gpm.shape), _full(g2.shape)],
        out_specs=[pl.BlockSpec((tm, wa_), row), pl.BlockSpec((tm, d), row), pl.BlockSpec((tm, d), row),
                   pl.BlockSpec((tm, d), row)],
        out_shape=[_sds((tp, wa_), BF16), _sds((tp, d), F32), _sds((tp, d), F32), _sds((tp, d), BF16)],
        compiler_params=_params(("arbitrary",)),
    )(ya, z, h, wout, lg, lb, gpm, g2)


def _ffn_up(xn2, wg, wu):
    tp, d = xn2.shape
    ff_dim = wg.shape[0]
    tm = _row_tile(tp)
    assert ff_dim % MXU_TILE == 0

    def body(xn_ref, wg_ref, wu_ref, p_ref, q_ref, f_ref):
        xn = xn_ref[...]
        for lo in range(0, ff_dim, MXU_TILE):
            cols = slice(lo, lo + MXU_TILE)
            a = _dot(xn, wg_ref[cols, :], NT)
            u = _dot(xn, wu_ref[cols, :], NT)
            s = _sig(a)
            q = a * s
            p_ref[:, cols] = (u * (s + q * (1.0 - s))).astype(BF16)
            q_ref[:, cols] = q.astype(BF16)
            f_ref[:, cols] = (q * u).astype(BF16)

    ospec = pl.BlockSpec((tm, ff_dim), lambda i: (i, 0))
    return pl.pallas_call(
        body, name="ffn_up", grid=(tp // tm,),
        in_specs=[pl.BlockSpec((tm, d), lambda i: (i, 0)), _resident(wg.shape), _resident(wu.shape)],
        out_specs=[ospec, ospec, ospec],
        out_shape=[_sds((tp, ff_dim), BF16)] * 3,
        compiler_params=_params(("arbitrary",)),
    )(xn2, wg, wu)


def _ffn_down(f, wd, h1, tgt, gpf):
    tp, ff_dim = f.shape
    d = h1.shape[1]
    tm = _row_tile(tp)
    seq, _ = _seq_rows(tp)

    def body(f_ref, w_ref, h1_ref, t_ref, gpf_ref, dff_ref, dh2_ref, loss_ref, dgpf_ref):
        i = pl.program_id(0)
        gpf_ = gpf_ref[...]

        @pl.when(i == 0)
        def _():
            loss_ref[...] = jnp.zeros(loss_ref.shape, F32)
            dgpf_ref[...] = jnp.zeros(dgpf_ref.shape, F32)

        for rs in _row_parts(tm):
            ff = _dot(f_ref[rs, :], w_ref[...])
            rf = lax.rsqrt(_mean(ff * ff) + RMS_EPS)
            nf = ff * rf
            h2 = h1_ref[rs, :] + nf * gpf_
            rows = i * tm + rs.start + lax.broadcasted_iota(jnp.int32, (rs.stop - rs.start, 1), 0)
            err = jnp.where(rows < seq, h2 - t_ref[rs, :], 0.0)
            dh2 = err * (1.0 / d)
            dh2_ref[rs, :] = dh2
            dn = dh2 * gpf_
            dff_ref[rs, :] = (rf * (dn - nf * _mean(dn * nf))).astype(BF16)
            loss_ref[...] += (0.5 / d) * jnp.sum(err * err, axis=(0, 1), keepdims=True)
            dgpf_ref[...] += jnp.sum(dh2 * nf, axis=0, keepdims=True)

    row = lambda i: (i, 0)
    return pl.pallas_call(
        body, name="ffn_down", grid=(tp // tm,),
        in_specs=[pl.BlockSpec((tm, ff_dim), row), _resident(wd.shape), pl.BlockSpec((tm, d), row),
                  pl.BlockSpec((tm, d), row), _full(gpf.shape)],
        out_specs=[pl.BlockSpec((tm, d), row), pl.BlockSpec((tm, d), row), _full((8, LANES)), _full((1, d))],
        out_shape=[_sds((tp, d), BF16), _sds((tp, d), F32), _sds((8, LANES), F32), _sds((1, d), F32)],
        compiler_params=_params(("arbitrary",)),
    )(f, wd, h1, tgt, gpf)


def _ffn_bwd_act(dff, wd, p, q):
    tp, d = dff.shape
    ff_dim = wd.shape[0]
    tm = _row_tile(tp)

    def body(dff_ref, w_ref, p_ref, q_ref, da_ref, du_ref):
        dffv = dff_ref[...]
        for lo in range(0, ff_dim, MXU_TILE):
            cols = slice(lo, lo + MXU_TILE)
            df = _dot(dffv, w_ref[cols, :], NT).astype(BF16)
            da_ref[:, cols] = df * p_ref[:, cols]
            du_ref[:, cols] = df * q_ref[:, cols]

    aspec = pl.BlockSpec((tm, ff_dim), lambda i: (i, 0))
    return pl.pallas_call(
        body, name="ffn_bwd_act", grid=(tp // tm,),
        in_specs=[pl.BlockSpec((tm, d), lambda i: (i, 0)), _resident(wd.shape), aspec, aspec],
        out_specs=[aspec, aspec],
        out_shape=[_sds((tp, ff_dim), BF16)] * 2,
        compiler_params=_params(("arbitrary",)),
    )(dff, wd, p, q)


def _grad_blocks(ff_dim):
    rows = ff_dim // 2
    assert rows % LANES == 0
    return rows


def _grad_w_down(f, dff):
    tp, ff_dim = f.shape
    d = dff.shape[1]
    rows = _grad_blocks(ff_dim)

    def body(f_ref, dff_ref, g_ref):
        g_ref[...] = _dot(f_ref[...], dff_ref[...], TN).astype(BF16)

    return pl.pallas_call(
        body, name="grad_w_down", grid=(ff_dim // rows,),
        in_specs=[pl.BlockSpec((tp, rows), lambda k: (0, k)), _resident(dff.shape)],
        out_specs=pl.BlockSpec((rows, d), lambda k: (k, 0)),
        out_shape=_sds((ff_dim, d), BF16),
        compiler_params=_params(("arbitrary",)),
    )(f, dff)


def _grad_w_gate_up(xn2, da, du):
    tp, ff_dim = da.shape
    d = xn2.shape[1]
    rows = _grad_blocks(ff_dim)

    def body(xn_ref, da_ref, du_ref, gg_ref, gu_ref):
        xn = xn_ref[...]
        gg_ref[...] = _dot(da_ref[...], xn, TN).astype(BF16)
        gu_ref[...] = _dot(du_ref[...], xn, TN).astype(BF16)

    aspec = pl.BlockSpec((tp, rows), lambda k: (0, k))
    gspec = pl.BlockSpec((rows, d), lambda k: (k, 0))
    return pl.pallas_call(
        body, name="grad_w_gate_up", grid=(ff_dim // rows,),
        in_specs=[_resident(xn2.shape), aspec, aspec],
        out_specs=[gspec, gspec],
        out_shape=[_sds((ff_dim, d), BF16)] * 2,
        compiler_params=_params(("arbitrary",)),
    )(xn2, da, du)


def _rms_bwd(dy, x, r, g):
    n = x * r
    dn = dy * g
    return r * (dn - n * _mean(dn * n)), dy * n


def _ffn_bwd_in(da, du, wg, wu, h1, mix, dh2, g2, gpm, after):
    tp, ff_dim = da.shape
    d = h1.shape[1]
    tm = _row_tile(tp)

    def body(da_ref, du_ref, wg_ref, wu_ref, h1_ref, mix_ref, dh2_ref, g2_ref, gpm_ref, after_ref,
             dh1_ref, dmix_ref, dg2_ref, dgpm_ref):
        i = pl.program_id(0)

        @pl.when(i == 0)
        def _():
            dg2_ref[...] = jnp.zeros(dg2_ref.shape, F32)
            dgpm_ref[...] = jnp.zeros(dgpm_ref.shape, F32)

        for rs in _row_parts(tm):
            dxn = _dot(da_ref[rs, :], wg_ref[...]) + _dot(du_ref[rs, :], wu_ref[...])
            h1v = h1_ref[rs, :]
            r2 = lax.rsqrt(_mean(h1v * h1v) + RMS_EPS)
            dres, dg2_rows = _rms_bwd(dxn, h1v, r2, g2_ref[...])
            dh1 = dh2_ref[rs, :] + dres
            dh1_ref[rs, :] = dh1
            mixv = mix_ref[rs, :]
            rm = lax.rsqrt(_mean(mixv * mixv) + RMS_EPS)
            dmix, dgpm_rows = _rms_bwd(dh1, mixv, rm, gpm_ref[...])
            dmix_ref[rs, :] = dmix.astype(BF16)
            dg2_ref[...] += jnp.sum(dg2_rows, axis=0, keepdims=True)
            dgpm_ref[...] += jnp.sum(dgpm_rows, axis=0, keepdims=True)

    aspec = pl.BlockSpec((tm, ff_dim), lambda i: (i, 0))
    row = pl.BlockSpec((tm, d), lambda i: (i, 0))
    return pl.pallas_call(
        body, name="ffn_bwd_in", grid=(tp // tm,),
        in_specs=[aspec, aspec, _resident(wg.shape), _resident(wu.shape), row, row, row, _full(g2.shape), _full(gpm.shape),
                  ANY],
        out_specs=[row, row, _full((1, d)), _full((1, d))],
        out_shape=[_sds((tp, d), F32), _sds((tp, d), BF16), _sds((1, d), F32), _sds((1, d), F32)],
        compiler_params=_params(("arbitrary",)),
    )(da, du, wg, wu, h1, mix, dh2, g2, gpm, after)


def _grad_w_out(ya, yb, dmix, after):
    tp, wa_ = ya.shape
    d = dmix.shape[1]

    def body(ya_ref, yb_ref, dmix_ref, after_ref, g_ref):
        dm = dmix_ref[...]
        g_ref[0:wa_, :] = _dot(ya_ref[...], dm, TN).astype(BF16)
        g_ref[wa_:2 * wa_, :] = _dot(yb_ref[...], dm, TN).astype(BF16)

    return pl.pallas_call(
        body, name="grad_w_out", grid=(1,),
        in_specs=[_full(ya.shape), _full(yb.shape), _full(dmix.shape), ANY],
        out_specs=_full((2 * wa_, d)),
        out_shape=_sds((2 * wa_, d), BF16),
        compiler_params=_params(("arbitrary",)),
    )(ya, yb, dmix, after)


def _mix_bwd_out(dmix, wout, z, lg, lb, after):
    tp, d = dmix.shape
    wa_ = z.shape[1]
    tm = _row_tile(tp)

    def body(dmix_ref, w_ref, z_ref, lg_ref, lb_ref, after_ref, dya_ref, dz_ref, dlg_ref, dlb_ref):
        i = pl.program_id(0)
        lg_ = lg_ref[...]

        @pl.when(i == 0)
        def _():
            dlg_ref[...] = jnp.zeros(dlg_ref.shape, F32)
            dlb_ref[...] = jnp.zeros(dlb_ref.shape, F32)

        for rs in _row_parts(tm):
            dm = dmix_ref[rs, :]
            dya_ref[rs, :] = _dot(dm, w_ref[0:wa_, :], NT)
            dyb = _dot(dm, w_ref[wa_:d, :], NT)
            rl, zh, l = _layer_norm_parts(z_ref[rs, :], lg_, lb_ref[...])
            sl = _sig(l)
            dl = dyb * (sl * (1.0 + l * (1.0 - sl)))
            dzh = dl * lg_
            dz_ref[rs, :] = rl * (dzh - _mean(dzh) - zh * _mean(dzh * zh))
            dlg_ref[...] += jnp.sum(dl * zh, axis=0, keepdims=True)
            dlb_ref[...] += jnp.sum(dl, axis=0, keepdims=True)

    row = lambda i: (i, 0)
    return pl.pallas_call(
        body, name="mix_bwd_out", grid=(tp // tm,),
        in_specs=[pl.BlockSpec((tm, d), row), _resident(wout.shape), pl.BlockSpec((tm, wa_), row), _full(lg.shape),
                  _full(lb.shape), ANY],
        out_specs=[pl.BlockSpec((tm, wa_), row), pl.BlockSpec((tm, wa_), row), _full((1, wa_)), _full((1, wa_))],
        out_shape=[_sds((tp, wa_), F32), _sds((tp, wa_), F32), _sds((1, wa_), F32), _sds((1, wa_), F32)],
        compiler_params=_params(("arbitrary",)),
    )(dmix, wout, z, lg, lb, after)


def _mix_conv_bwd(hp5, dya, dz, wa, wb):
    _, tp, wgrp = hp5.shape
    seq, nseq = _seq_rows(tp)
    sb = nseq + 2 * CONV_HIST
    ka, kb = wa.shape[0], wb.shape[0]
    xs, ms = slice(0, seq), slice(seq, seq + N_META)
    ox, om = slice(CONV_HIST + N_META, CONV_HIST + nseq), slice(CONV_HIST, CONV_HIST + N_META)
    n_tail = tp - seq - N_META

    def body(hp_ref, dya_ref, dz_ref, wa_ref, wb_ref, dhp_ref, dwa_ref, dwb_ref, dbb_ref, s_ref, d_ref, o_ref, acc_ref,
             shs_ref, shd_ref):
        _zero_ends(s_ref, nseq)
        _zero_ends(d_ref, nseq)

        def put(p, ox_val, om_val):
            dhp_ref[p, xs, :] = ox_val.astype(BF16)
            dhp_ref[p, ms, :] = om_val.astype(BF16)
            dhp_ref[p, seq + N_META:tp, :] = jnp.zeros((n_tail, LANES), BF16)

        def wgrad(dw_ref, width):
            for k in range(width):
                dw_ref[k:k + 1, :] = jnp.sum(acc_ref[8 * k:8 * k + 8, :], axis=0, keepdims=True)

        _to_seq(s_ref, hp_ref[1, xs, :] * hp_ref[2, xs, :], hp_ref[1, ms, :] * hp_ref[2, ms, :], seq)
        _conv_taps(s_ref, shs_ref, wa_ref, o_ref, ka, nseq, False)
        put(0, dya_ref[xs, :] * o_ref[ox, :], dya_ref[ms, :] * o_ref[om, :])
        _to_seq(d_ref, dya_ref[xs, :] * hp_ref[0, xs, :], dya_ref[ms, :] * hp_ref[0, ms, :], seq)
        _conv_wgrad(s_ref, shs_ref, d_ref, acc_ref, ka, nseq)
        wgrad(dwa_ref, ka)
        _conv_taps(d_ref, shd_ref, wa_ref, o_ref, ka, nseq, True)
        put(1, o_ref[ox, :] * hp_ref[2, xs, :], o_ref[om, :] * hp_ref[2, ms, :])
        put(2, o_ref[ox, :] * hp_ref[1, xs, :], o_ref[om, :] * hp_ref[1, ms, :])

        _to_seq(s_ref, hp_ref[3, xs, :] * _sig(hp_ref[4, xs, :]), hp_ref[3, ms, :] * _sig(hp_ref[4, ms, :]), seq)
        _to_seq(d_ref, dz_ref[xs, :], dz_ref[ms, :], seq)
        dbb_ref[...] = (jnp.sum(dz_ref[xs, :], axis=0, keepdims=True)
                        + jnp.sum(dz_ref[ms, :], axis=0, keepdims=True))
        _shift_copies(s_ref, shs_ref, kb, False)
        _conv_wgrad(s_ref, shs_ref, d_ref, acc_ref, kb, nseq)
        wgrad(dwb_ref, kb)
        _conv_taps(d_ref, shd_ref, wb_ref, o_ref, kb, nseq, True)
        sx, sm = _sig(hp_ref[4, xs, :]), _sig(hp_ref[4, ms, :])
        put(3, o_ref[ox, :] * sx, o_ref[om, :] * sm)
        put(4, o_ref[ox, :] * hp_ref[3, xs, :] * sx * (1.0 - sx), o_ref[om, :] * hp_ref[3, ms, :] * sm * (1.0 - sm))

    col = lambda j: (0, j)
    blk5 = pl.BlockSpec((5, tp, LANES), lambda j: (0, 0, j))
    return pl.pallas_call(
        body, name="mix_conv_bwd", grid=(wgrp // LANES,),
        in_specs=[blk5, pl.BlockSpec((tp, LANES), col), pl.BlockSpec((tp, LANES), col),
                  pl.BlockSpec((ka, LANES), col), pl.BlockSpec((kb, LANES), col)],
        out_specs=[blk5, pl.BlockSpec((ka, LANES), col), pl.BlockSpec((kb, LANES), col), pl.BlockSpec((1, LANES), col)],
        out_shape=[_sds((5, tp, wgrp), BF16), _sds((ka, wgrp), F32), _sds((kb, wgrp), F32), _sds((1, wgrp), F32)],
        scratch_shapes=[pltpu.VMEM((sb, LANES), F32), pltpu.VMEM((sb, LANES), F32), pltpu.VMEM((sb, LANES), F32),
                        pltpu.VMEM((SUBLANES * kb, LANES), F32), pltpu.VMEM((SUBLANES - 1, sb, LANES), F32),
                        pltpu.VMEM((SUBLANES - 1, sb, LANES), F32)],
        compiler_params=_params(("arbitrary",)),
    )(hp5, dya, dz, wa, wb)


def _grad_w_in(xn1, dhp5):
    n_p, tp, pw = dhp5.shape
    d = xn1.shape[1]

    def body(xn_ref, dhp_ref, g_ref):
        g_ref[...] = _dot(xn_ref[...], dhp_ref[0], TN).astype(BF16)

    return pl.pallas_call(
        body, name="grad_w_in", grid=(n_p,),
        in_specs=[_resident(xn1.shape), pl.BlockSpec((1, tp, pw), lambda p: (p, 0, 0))],
        out_specs=pl.BlockSpec((d, pw), lambda p: (0, p)),
        out_shape=_sds((d, n_p * pw), BF16),
        compiler_params=_params(("arbitrary",)),
    )(xn1, dhp5)


def _mix_bwd_in(dhp5, win4, h, dh1, g1, after):
    n_p, tp, pw = dhp5.shape
    d = h.shape[1]
    n_sh, _, csh = win4.shape
    tm = _row_tile(tp)

    seq, _ = _seq_rows(tp)
    last, meta_off = seq // tm, seq % tm
    assert last == tp // tm - 1
    assert any(rs.start <= meta_off and meta_off + N_META <= rs.stop for rs in _row_parts(tm))

    def body(dhp_ref, w_ref, h_ref, dh1_ref, g_ref, after_ref, gx_ref, dmeta_ref, dg1_ref, wcat_ref):
        i = pl.program_id(0)
        _concat_shards(w_ref, wcat_ref)

        @pl.when(i == 0)
        def _():
            dg1_ref[...] = jnp.zeros(dg1_ref.shape, F32)

        for rs in _row_parts(tm):
            dxn = _dot(dhp_ref[0, rs, :], wcat_ref[:, 0:pw], NT)
            for p in range(1, n_p):
                dxn = dxn + _dot(dhp_ref[p, rs, :], wcat_ref[:, p * pw:(p + 1) * pw], NT)
            hh = h_ref[rs, :]
            r1 = lax.rsqrt(_mean(hh * hh) + RMS_EPS)
            dres, dg_rows = _rms_bwd(dxn, hh, r1, g_ref[...])
            dh = dh1_ref[rs, :] + dres
            gx_ref[rs, :] = dh
            dg1_ref[...] += jnp.sum(dg_rows, axis=0, keepdims=True)
            if rs.start <= meta_off and meta_off + N_META <= rs.stop:
                @pl.when(i == last)
                def _():
                    dmeta_ref[...] = dh[meta_off - rs.start:meta_off - rs.start + N_META, :]

    row = lambda i: (i, 0)
    return pl.pallas_call(
        body, name="mix_bwd_in", grid=(tp // tm,),
        in_specs=[pl.BlockSpec((n_p, tm, pw), lambda i: (0, i, 0)), _resident(win4.shape), pl.BlockSpec((tm, d), row),
                  pl.BlockSpec((tm, d), row), _full(g1.shape), ANY],
        out_specs=[pl.BlockSpec((tm, d), row), _full((N_META, d)), _full((1, d))],
        out_shape=[_sds((seq, d), F32), _sds((N_META, d), F32), _sds((1, d), F32)],
        scratch_shapes=[pltpu.VMEM((d, n_sh * csh), BF16)],
        compiler_params=_params(("arbitrary",)),
    )(dhp5, win4, h, dh1, g1, after)


def _other_chips(x, y):
    out = []
    for j in (1, 2, 3):
        px, py = _flip(x, j >> 1), _flip(y, j & 1)
        out.append((px, py, 2 * px + py))
    return out


PAIR_COLLECTIVE_ID = 0


def _pair_barrier(x, y, c):
    sem = pltpu.get_barrier_semaphore()
    pl.semaphore_signal(sem, inc=1, device_id=(x, y, 1 - c), device_id_type=MESH)
    pl.semaphore_wait(sem, 1)


def _pair_params():
    return pltpu.CompilerParams(collective_id=PAIR_COLLECTIVE_ID)


def _half_rows(c, rows_half):
    return pl.ds(pl.multiple_of(c * rows_half, 8), rows_half)


def _cast_place(ws, q_arr, tag, after=None):
    n = len(ws)
    rows, cols = ws[0].shape
    tr = _row_tile(rows)
    extra = [] if after is None else [after]

    def body(q_ref, *refs):
        for w_ref, out_ref in zip(refs[:n], refs[n + len(extra):]):
            out_ref[0] = w_ref[...].astype(BF16)

    return list(pl.pallas_call(
        body, name="cast_place_" + tag,
        grid_spec=pltpu.PrefetchScalarGridSpec(
            num_scalar_prefetch=1, grid=(rows // tr,),
            in_specs=[pl.BlockSpec((tr, cols), lambda i, q: (i, 0))] * n + [ANY] * len(extra),
            out_specs=[pl.BlockSpec((1, tr, cols), lambda i, q: (q[0], i, 0))] * n),
        out_shape=[_sds((N_CHIPS, rows, cols), BF16)] * n,
        compiler_params=_params(("arbitrary",)),
    )(q_arr, *ws, *extra))


HBM = pl.BlockSpec(memory_space=pltpu.HBM)
SEM = pl.BlockSpec(memory_space=pltpu.SEMAPHORE)
EFFECT = pltpu.SideEffectType.DATAFLOW_SIDE_EFFECTING


def _in_hbm(a):
    return pltpu.with_memory_space_constraint(a, pltpu.HBM)


def _gather_start(fulls, after, tag):
    n = len(fulls)
    halves = [a.shape[1] // 2 for a in fulls]

    def body(*refs):
        land = refs[:n]
        ssem, rsem = refs[n + 1], refs[n + 2]
        token = refs[-1]
        x, y, c = _mesh_pos()
        q = 2 * x + y
        for i in range(n):
            for j, (px, py, _) in enumerate(_other_chips(x, y)):
                mine = land[i].at[q, _half_rows(c, halves[i]), :]
                pltpu.make_async_remote_copy(src_ref=mine, dst_ref=mine, send_sem=ssem.at[3 * i + j],
                                             recv_sem=rsem.at[3 * i + j], device_id=(px, py, c), device_id_type=MESH).start()
        token[...] = jnp.zeros(token.shape, F32)

    outs = pl.pallas_call(
        body, name="gather_start_" + tag,
        in_specs=[HBM] * n + [ANY], out_specs=[SEM, SEM] + [HBM] * n + [VMEM],
        out_shape=[pltpu.SemaphoreType.DMA((3 * n,)), pltpu.SemaphoreType.DMA((3 * n,))]
        + [pltpu.HBM(a.shape, a.dtype) for a in fulls] + [_sds((8, LANES), F32)],
        input_output_aliases={i: 2 + i for i in range(n)},
        compiler_params=pltpu.CompilerParams(has_side_effects=EFFECT),
    )(*[_in_hbm(a) for a in fulls], after)
    return outs[0], outs[1], list(outs[2:2 + n]), outs[-1]


def _gather_wait(which, ssem, rsem, lands, after, tag):
    m = len(which)
    halves = [a.shape[1] // 2 for a in lands]

    def body(*refs):
        land = refs[:m]
        ssem_, rsem_ = refs[m], refs[m + 1]
        x, y, c = _mesh_pos()
        for t, i in enumerate(which):
            for j, (px, py, qj) in enumerate(_other_chips(x, y)):
                rows = _half_rows(c, halves[t])
                cp = pltpu.make_async_remote_copy(src_ref=land[t].at[2 * x + y, rows, :], dst_ref=land[t].at[qj, rows, :],
                                                  send_sem=ssem_.at[3 * i + j], recv_sem=rsem_.at[3 * i + j],
                                                  device_id=(px, py, c), device_id_type=MESH)
                cp.wait_send()
                cp.wait_recv()

    outs = pl.pallas_call(
        body, name="gather_wait_" + tag,
        in_specs=[HBM] * m + [SEM, SEM, ANY], out_specs=[HBM] * m,
        out_shape=[pltpu.HBM(a.shape, a.dtype) for a in lands],
        input_output_aliases={i: i for i in range(m)},
        compiler_params=pltpu.CompilerParams(has_side_effects=EFFECT),
    )(*lands, ssem, rsem, after)
    return list(outs)


def _forward_pair(lands, tag):
    n = len(lands)
    halves = [a.shape[1] // 2 for a in lands]

    def body(*refs):
        full = refs[n:2 * n]
        ssem, rsem = refs[2 * n:]
        x, y, c = _mesh_pos()
        _pair_barrier(x, y, c)
        cps = []
        for i in range(n):
            for j, (_, _, qj) in enumerate(_other_chips(x, y)):
                part = full[i].at[qj, _half_rows(c, halves[i]), :]
                cp = pltpu.make_async_remote_copy(src_ref=part, dst_ref=part, send_sem=ssem.at[3 * i + j],
                                                  recv_sem=rsem.at[3 * i + j], device_id=(x, y, 1 - c), device_id_type=MESH)
                cp.start()
                cps.append(cp)
        for cp in cps:
            cp.wait()

    return pl.pallas_call(
        body, name="forward_pair_" + tag,
        in_specs=[ANY] * n, out_specs=[ANY] * n,
        out_shape=[_sds(a.shape, a.dtype) for a in lands],
        input_output_aliases={i: i for i in range(n)},
        scratch_shapes=[pltpu.SemaphoreType.DMA((3 * n,)), pltpu.SemaphoreType.DMA((3 * n,))],
        compiler_params=_pair_params(),
    )(*lands)


def _chip_exchange_start(parts, after, tag):
    n = len(parts)

    def body(*refs):
        src, land = refs[:n], refs[n:2 * n]
        ssem, rsem = refs[2 * n + 1], refs[2 * n + 2]
        token = refs[-1]
        x, y, c = _mesh_pos()
        for i in range(n):
            for j, (px, py, qj) in enumerate(_other_chips(x, y)):
                pltpu.make_async_remote_copy(src_ref=src[i].at[qj], dst_ref=land[i].at[j], send_sem=ssem.at[3 * i + j],
                                             recv_sem=rsem.at[3 * i + j], device_id=(px, py, c), device_id_type=MESH).start()
        token[...] = jnp.zeros(token.shape, F32)

    lands = [lax.empty((3,) + a.shape[1:], a.dtype) for a in parts]
    outs = pl.pallas_call(
        body, name="chip_exchange_start_" + tag,
        in_specs=[HBM] * (2 * n) + [ANY], out_specs=[SEM, SEM] + [HBM] * (2 * n) + [VMEM],
        out_shape=[pltpu.SemaphoreType.DMA((3 * n,)), pltpu.SemaphoreType.DMA((3 * n,))]
        + [pltpu.HBM(a.shape, a.dtype) for a in parts] + [pltpu.HBM(a.shape, a.dtype) for a in lands]
        + [_sds((8, LANES), F32)],
        input_output_aliases={i: 2 + i for i in range(2 * n)},
        compiler_params=pltpu.CompilerParams(has_side_effects=EFFECT),
    )(*[_in_hbm(a) for a in parts], *[_in_hbm(a) for a in lands], after)
    return outs[0], outs[1], list(outs[2:2 + n]), list(outs[2 + n:2 + 2 * n]), outs[-1]


def _chip_exchange_wait(ssem, rsem, parts, lands, after, tag):
    n = len(parts)

    def body(*refs):
        src, land = refs[:n], refs[n:2 * n]
        ssem_, rsem_ = refs[2 * n], refs[2 * n + 1]
        x, y, c = _mesh_pos()
        for i in range(n):
            for j, (px, py, qj) in enumerate(_other_chips(x, y)):
                cp = pltpu.make_async_remote_copy(src_ref=src[i].at[qj], dst_ref=land[i].at[j], send_sem=ssem_.at[3 * i + j],
                                                  recv_sem=rsem_.at[3 * i + j], device_id=(px, py, c), device_id_type=MESH)
                cp.wait_send()
                cp.wait_recv()

    outs = pl.pallas_call(
        body, name="chip_exchange_wait_" + tag,
        in_specs=[HBM] * (2 * n) + [SEM, SEM, ANY], out_specs=[HBM] * (2 * n),
        out_shape=[pltpu.HBM(a.shape, a.dtype) for a in parts] + [pltpu.HBM(a.shape, a.dtype) for a in lands],
        input_output_aliases={i: i for i in range(2 * n)},
        compiler_params=pltpu.CompilerParams(has_side_effects=EFFECT),
    )(*parts, *lands, ssem, rsem, after)
    return list(outs[:n]), list(outs[n:])


def _grad_half(ref, shape, axis, which):
    rows = shape[axis] // 2
    if axis == 0:
        return ref.at[_half_rows(which, rows), :]
    return ref.at[:, _half_rows(which, rows), :]


def _half_shape(a, axis):
    s = list(a.shape)
    s[axis] //= 2
    return tuple(s)


def _pair_exchange_start(grads, half_axis, after, tag):
    n = len(grads)

    def body(*refs):
        g, land = refs[:n], refs[n:2 * n]
        ssem, rsem = refs[2 * n + 1], refs[2 * n + 2]
        token = refs[-1]
        x, y, c = _mesh_pos()
        for i in range(n):
            pltpu.make_async_remote_copy(src_ref=_grad_half(g[i], grads[i].shape, half_axis[i], 1 - c), dst_ref=land[i],
                                         send_sem=ssem.at[i], recv_sem=rsem.at[i], device_id=(x, y, 1 - c),
                                         device_id_type=MESH).start()
        token[...] = jnp.zeros(token.shape, F32)

    lands = [lax.empty(_half_shape(a, half_axis[i]), a.dtype) for i, a in enumerate(grads)]
    outs = pl.pallas_call(
        body, name="pair_exchange_start_" + tag,
        in_specs=[HBM] * (2 * n) + [ANY], out_specs=[SEM, SEM] + [HBM] * (2 * n) + [VMEM],
        out_shape=[pltpu.SemaphoreType.DMA((n,)), pltpu.SemaphoreType.DMA((n,))]
        + [pltpu.HBM(a.shape, a.dtype) for a in grads] + [pltpu.HBM(a.shape, a.dtype) for a in lands]
        + [_sds((8, LANES), F32)],
        input_output_aliases={i: 2 + i for i in range(2 * n)},
        compiler_params=pltpu.CompilerParams(has_side_effects=EFFECT),
    )(*[_in_hbm(a) for a in grads], *[_in_hbm(a) for a in lands], after)
    return outs[0], outs[1], list(outs[2:2 + n]), list(outs[2 + n:2 + 2 * n]), outs[-1]


def _pair_exchange_wait(ssem, rsem, grads, lands, half_axis, after, tag):
    n = len(grads)

    def body(*refs):
        g, land = refs[:n], refs[n:2 * n]
        ssem_, rsem_ = refs[2 * n], refs[2 * n + 1]
        x, y, c = _mesh_pos()
        for i in range(n):
            cp = pltpu.make_async_remote_copy(src_ref=_grad_half(g[i], grads[i].shape, half_axis[i], 1 - c),
                                              dst_ref=land[i], send_sem=ssem_.at[i], recv_sem=rsem_.at[i],
                                              device_id=(x, y, 1 - c), device_id_type=MESH)
            cp.wait_send()
            cp.wait_recv()

    outs = pl.pallas_call(
        body, name="pair_exchange_wait_" + tag,
        in_specs=[HBM] * (2 * n) + [SEM, SEM, ANY], out_specs=[HBM] * (2 * n),
        out_shape=[pltpu.HBM(a.shape, a.dtype) for a in grads] + [pltpu.HBM(a.shape, a.dtype) for a in lands],
        input_output_aliases={i: i for i in range(2 * n)},
        compiler_params=pltpu.CompilerParams(has_side_effects=EFFECT),
    )(*grads, *lands, ssem, rsem, after)
    return list(outs[:n]), list(outs[n:])


def _pair_exchange_grads(grads, half_axis, tag):
    n = len(grads)

    def body(*refs):
        g, got = refs[:n], refs[n:2 * n]
        ssem, rsem = refs[2 * n:]
        x, y, c = _mesh_pos()
        _pair_barrier(x, y, c)
        cps = []
        for i in range(n):
            cp = pltpu.make_async_remote_copy(src_ref=_grad_half(g[i], grads[i].shape, half_axis[i], 1 - c),
                                              dst_ref=got[i], send_sem=ssem.at[i], recv_sem=rsem.at[i],
                                              device_id=(x, y, 1 - c), device_id_type=MESH)
            cp.start()
            cps.append(cp)
        for cp in cps:
            cp.wait()

    return pl.pallas_call(
        body, name="pair_exchange_grads_" + tag,
        in_specs=[ANY] * n, out_specs=[ANY] * n,
        out_shape=[_sds(_half_shape(a, half_axis[i]), a.dtype) for i, a in enumerate(grads)],
        scratch_shapes=[pltpu.SemaphoreType.DMA((n,)), pltpu.SemaphoreType.DMA((n,))],
        compiler_params=_pair_params(),
    )(*grads)


def _pair_sum(gs, gots, c_arr, col_sharded, tag):
    n = len(gs)
    if col_sharded:
        rows, cols = gs[0].shape
        rh, cs = rows // 2, cols // N_CHIPS
        g_spec = pl.BlockSpec((rh, cs), lambda k, c_ref: (c_ref[0], k))
        got_spec = pl.BlockSpec((rh, cs), lambda k, c_ref: (0, k))
    else:
        _, rows, cs = gs[0].shape
        rh = rows // 2
        g_spec = pl.BlockSpec((1, rh, cs), lambda k, c_ref: (k, c_ref[0], 0))
        got_spec = pl.BlockSpec((1, rh, cs), lambda k, c_ref: (k, 0, 0))

    def body(c_ref, *refs):
        for g_ref, got_ref, out_ref in zip(refs[:n], refs[n:2 * n], refs[2 * n:]):
            total = g_ref[...].astype(F32) + got_ref[...].astype(F32)
            out_ref[...] = total.astype(BF16).reshape(out_ref.shape)

    return list(pl.pallas_call(
        body, name="pair_sum_" + tag,
        grid_spec=pltpu.PrefetchScalarGridSpec(
            num_scalar_prefetch=1, grid=(N_CHIPS,), in_specs=[g_spec] * n + [got_spec] * n,
            out_specs=[pl.BlockSpec((1, rh, cs), lambda k, c_ref: (k, 0, 0))] * n),
        out_shape=[_sds((N_CHIPS, rh, cs), BF16)] * n,
        compiler_params=_params(("arbitrary",)),
    )(c_arr, *gs, *gots))


def _chip_sum(parts, gots, qc_arr, tag):
    n = len(parts)
    _, rh, cs = parts[0].shape
    steps = 2 if rh % 32 == 0 else 1
    rb = rh // steps

    def body(qc_ref, *refs):
        for part_ref, got_ref, out_ref in zip(refs[:n], refs[n:2 * n], refs[2 * n:]):
            total = part_ref[0].astype(F32)
            for j in range(3):
                total = total + got_ref[j].astype(F32)
            out_ref[...] = total

    return list(pl.pallas_call(
        body, name="chip_sum_" + tag,
        grid_spec=pltpu.PrefetchScalarGridSpec(
            num_scalar_prefetch=1, grid=(steps,),
            in_specs=[pl.BlockSpec((1, rb, cs), lambda i, qc: (qc[0], i, 0))] * n
            + [pl.BlockSpec((3, rb, cs), lambda i, qc: (0, i, 0))] * n,
            out_specs=[pl.BlockSpec((rb, cs), lambda i, qc: (qc[1] * steps + i, 0))] * n),
        out_shape=[_sds((2 * rh, cs), F32)] * n,
        compiler_params=_params(("arbitrary",)),
    )(qc_arr, *parts, *gots))


def _pair_share_grads(grads, tag):
    n = len(grads)

    def body(*refs):
        g = refs[n:2 * n]
        ssem, rsem = refs[2 * n:]
        x, y, c = _mesh_pos()
        _pair_barrier(x, y, c)
        cps = []
        for i in range(n):
            mine = g[i].at[_half_rows(c, grads[i].shape[0] // 2), :]
            cp = pltpu.make_async_remote_copy(src_ref=mine, dst_ref=mine, send_sem=ssem.at[i], recv_sem=rsem.at[i],
                                              device_id=(x, y, 1 - c), device_id_type=MESH)
            cp.start()
            cps.append(cp)
        for cp in cps:
            cp.wait()

    return pl.pallas_call(
        body, name="pair_share_grads_" + tag,
        in_specs=[ANY] * n, out_specs=[ANY] * n,
        out_shape=[_sds(a.shape, a.dtype) for a in grads],
        input_output_aliases={i: i for i in range(n)},
        scratch_shapes=[pltpu.SemaphoreType.DMA((n,)), pltpu.SemaphoreType.DMA((n,))],
        compiler_params=_pair_params(),
    )(*grads)


def _small_allreduce(parts, places, rows_total, width, after):
    n = len(parts)

    def body(*refs):
        ins, out_ref = refs[:n], refs[n + 1]
        pack, pair_got, chip_sum, got, ssem, rsem = refs[n + 2:]
        x, y, c = _mesh_pos()
        chip = 2 * x + y
        pack[...] = jnp.zeros(pack.shape, F32)
        for i in range(n):
            for row, col, src_row, rows in places[i]:
                w = parts[i].shape[1]
                pack[row:row + rows, col:col + w] = ins[i][src_row:src_row + rows, :]
        swap = pltpu.make_async_remote_copy(src_ref=pack, dst_ref=pair_got, send_sem=ssem.at[3], recv_sem=rsem.at[3],
                                            device_id=(x, y, 1 - c), device_id_type=MESH)
        swap.start()
        swap.wait()
        chip_sum[...] = pack[...] + pair_got[...]
        cps = []
        for j, (px, py, _) in enumerate(_other_chips(x, y)):
            cp = pltpu.make_async_remote_copy(src_ref=chip_sum, dst_ref=got.at[j], send_sem=ssem.at[j],
                                              recv_sem=rsem.at[j], device_id=(px, py, c), device_id_type=MESH)
            cp.start()
            cps.append(cp)
        for cp in cps:
            cp.wait()
        total = jnp.zeros(pack.shape, F32)
        for q in range(N_CHIPS):
            rel = jnp.bitwise_xor(chip, q)
            theirs = got[jnp.maximum(rel - 1, 0)]
            total = total + jnp.where(rel == 0, chip_sum[...], theirs)
        out_ref[...] = total

    return pl.pallas_call(
        body, name="small_allreduce",
        in_specs=[VMEM] * n + [ANY], out_specs=VMEM,
        out_shape=_sds((rows_total, width), F32),
        scratch_shapes=[pltpu.VMEM((rows_total, width), F32), pltpu.VMEM((rows_total, width), F32),
                        pltpu.VMEM((rows_total, width), F32), pltpu.VMEM((3, rows_total, width), F32),
                        pltpu.SemaphoreType.DMA((4,)), pltpu.SemaphoreType.DMA((4,))],
        compiler_params=_params(),
    )(*parts, after)


def _small_update(red, q_arr, takes, loss_at, ws, ms, vs):
    n_w = len(ws)

    def body(q_ref, red_ref, *refs):
        w_in, m_in, v_in = refs[0:n_w], refs[n_w:2 * n_w], refs[2 * n_w:3 * n_w]
        outs = refs[3 * n_w:]
        g_out, d_out, m_out, v_out = (outs[0:n_w], outs[n_w:2 * n_w], outs[2 * n_w:3 * n_w], outs[3 * n_w:4 * n_w])
        loss_ref = outs[4 * n_w]
        chip = q_ref[0]

        def take_own_columns(g_ref, d0, nr, s0, c0, w):
            for k in range(N_CHIPS):
                @pl.when(chip == k)
                def _():
                    g_ref[d0:d0 + nr, :] = red_ref[s0:s0 + nr, c0 + k * w:c0 + (k + 1) * w]

        for j in range(n_w):
            w = ws[j].shape[1]
            for d0, nr, s0, c0, sharded in takes[j]:
                if sharded:
                    take_own_columns(g_out[j], d0, nr, s0, c0, w)
                else:
                    g_out[j][d0:d0 + nr, :] = red_ref[s0:s0 + nr, c0:c0 + w]
            d_out[j][...], m_out[j][...], v_out[j][...] = _adamw_math(w_in[j][...], g_out[j][...], m_in[j][...], v_in[j][...])
        loss_ref[...] = red_ref[loss_at[0]:loss_at[0] + 1, loss_at[1]:loss_at[1] + LANES]

    shapes = [_sds(w.shape, F32) for w in ws]
    outs = pl.pallas_call(
        body, name="small_update",
        in_specs=[pl.BlockSpec(memory_space=pltpu.SMEM)] + [VMEM] * (1 + 3 * n_w), out_specs=[VMEM] * (4 * n_w + 1),
        out_shape=shapes * 4 + [_sds((1, LANES), F32)],
        compiler_params=_params(),
    )(q_arr, red, *ws, *ms, *vs)
    return outs[0:n_w], outs[n_w:2 * n_w], outs[2 * n_w:3 * n_w], outs[3 * n_w:4 * n_w], outs[4 * n_w]


def _adamw_math(w, g, m, v):
    m2 = ADAM_B1 * m + (1.0 - ADAM_B1) * g
    v2 = ADAM_B2 * v + (1.0 - ADAM_B2) * (g * g)
    m_hat = m2 * (1.0 / (1.0 - ADAM_B1 ** ADAM_STEP))
    v_hat = v2 * (1.0 / (1.0 - ADAM_B2 ** ADAM_STEP))
    delta = -ADAM_LR * (m_hat / (jnp.sqrt(v_hat) + ADAM_EPS) + ADAM_WD * w)
    return delta, m2, v2


SC_LANES = 16
SC_BLOCK = (8, 256)


def _adamw_sparsecore(ws, gs, ms, vs):
    n = len(ws)
    rows, cols = ws[0].shape
    br, bc = SC_BLOCK
    assert rows % br == 0 and cols % bc == 0

    def tile_body(*refs):
        ins, outs = refs[:4 * n], refs[4 * n:]

        def block(w_v, g_v, m_v, v_v, go_v, d_v, m2_v, v2_v):
            @pl.loop(0, br)
            def _(r):
                @pl.loop(0, bc, step=SC_LANES)
                def _(col):
                    at = (pl.ds(r, 1), pl.ds(col, SC_LANES))
                    gg = g_v.at[*at][...]
                    dd, mm, vv = _adamw_math(w_v.at[*at][...], gg, m_v.at[*at][...], v_v.at[*at][...])
                    go_v.at[*at][...] = gg
                    d_v.at[*at][...] = dd
                    m2_v.at[*at][...] = mm
                    v2_v.at[*at][...] = vv

        spec = pl.BlockSpec(block_shape=SC_BLOCK, index_map=lambda i, j: (i, j))
        for k in range(n):
            pltpu.emit_pipeline(
                block, grid=(rows // br, cols // bc), in_specs=[spec] * 4, out_specs=[spec] * 4,
                core_axis_name=("sc_core", "sc_tile"), dimension_semantics=(pltpu.PARALLEL, pltpu.PARALLEL),
            )(ins[k], ins[n + k], ins[2 * n + k], ins[3 * n + k], *outs[4 * k:4 * k + 4])

    outs = pl.kernel(
        tile_body, name="adamw_sparsecore",
        out_type=[_sds((rows, cols), F32)] * (4 * n),
        mesh=plsc.VectorSubcoreMesh(core_axis_name="sc_core", subcore_axis_name="sc_tile"),
        scratch_types=[],
    )(*ws, *gs, *ms, *vs)
    return [outs[4 * i:4 * i + 4] for i in range(n)]


ADAMW_BLOCK_BYTES = 3 * 2 ** 19


def _adamw_big(ws, gs, ms, vs, tag):
    n = len(ws)
    rows, cols = ws[0].shape
    tr = 16
    for t in range(16, rows + 1, 16):
        if rows % t == 0 and t * cols * 4 * n <= ADAMW_BLOCK_BYTES:
            tr = t

    def body(*refs):
        ins, outs = refs[:4 * n], refs[4 * n:]
        for i in range(n):
            w_ref, g_ref, m_ref, v_ref = ins[i], ins[n + i], ins[2 * n + i], ins[3 * n + i]
            gg = g_ref[...]
            outs[4 * i][...] = gg
            outs[4 * i + 1][...], outs[4 * i + 2][...], outs[4 * i + 3][...] = _adamw_math(
                w_ref[...], gg, m_ref[...], v_ref[...])

    spec = pl.BlockSpec((tr, cols), lambda i: (i, 0))
    outs = pl.pallas_call(
        body, name="adamw_" + tag, grid=(rows // tr,),
        in_specs=[spec] * (4 * n), out_specs=[spec] * (4 * n),
        out_shape=[_sds((rows, cols), F32)] * (4 * n),
        compiler_params=_params(("arbitrary",)),
    )(*ws, *gs, *ms, *vs)
    return [outs[4 * i:4 * i + 4] for i in range(n)]


SMALL_ROWS = 40
PACK_ROWS = 64


def kernel(x, meta_tokens, pre_mix_norm, w_in, conv_a_w, conv_b_w, conv_b_bias, ln_b_gain, ln_b_bias, w_out, post_mix_norm, pre_ffn_norm, w_gate, w_up, w_down, post_ffn_norm, loss_target, m_meta_tokens, m_pre_mix_norm, m_w_in, m_conv_a_w, m_conv_b_w, m_conv_b_bias, m_ln_b_gain, m_ln_b_bias, m_w_out, m_post_mix_norm, m_pre_ffn_norm, m_w_gate, m_w_up, m_w_down, m_post_ffn_norm, v_meta_tokens, v_pre_mix_norm, v_w_in, v_conv_a_w, v_conv_b_w, v_conv_b_bias, v_ln_b_gain, v_ln_b_bias, v_w_out, v_post_mix_norm, v_pre_ffn_norm, v_w_gate, v_w_up, v_w_down, v_post_ffn_norm):
    xq, yq, cq = lax.axis_index("x"), lax.axis_index("y"), lax.axis_index("c")
    chip = 2 * xq + yq
    c_arr = jnp.reshape(cq, (1,)).astype(jnp.int32)
    qc_arr = jnp.stack([chip, cq]).astype(jnp.int32)

    seq, d = x.shape[1], x.shape[2]
    x2, tgt2 = x[0], loss_target[0]
    tr = lambda a: jnp.swapaxes(a, 1, 2)[0]
    w_in2, w_out2, w_gate2, w_up2, w_down2 = w_in[0], w_out[0], tr(w_gate), tr(w_up), w_down[0]
    ka, wa_sh = conv_a_w.shape[1], conv_a_w.shape[2]
    kb = conv_b_w.shape[1]
    meta_sh = meta_tokens.shape[1]

    small = jnp.zeros((PACK_ROWS, meta_sh), F32)
    small = small.at[0:N_META, :].set(meta_tokens)
    small = small.at[16:16 + ka, 0:wa_sh].set(conv_a_w[0])
    small = small.at[24:24 + kb, 0:wa_sh].set(conv_b_w[0])
    q_arr = jnp.reshape(chip, (1,)).astype(jnp.int32)
    small_own = lax.dynamic_update_slice(jnp.zeros((N_CHIPS, PACK_ROWS, meta_sh), F32), small[None], (chip, 0, 0))
    i_ssem, i_rsem, first, i_token = _gather_start(_cast_place([w_in2], q_arr, "w_in") + [small_own], pre_mix_norm, "in")
    rest = _cast_place([w_out2], q_arr, "w_out", i_token) + _cast_place([w_gate2, w_up2, w_down2], q_arr, "ffn", i_token)
    g_ssem, g_rsem, lands, g_token = _gather_start(rest, i_token, "rest")
    win4, small4 = _forward_pair(_gather_wait([0, 1], i_ssem, i_rsem, first, g_token, "in"), "in")
    meta_f = jnp.concatenate([small4[k, 0:N_META, :] for k in range(N_CHIPS)], axis=1)
    wa_f = jnp.concatenate([small4[k, 16:16 + ka, 0:wa_sh] for k in range(N_CHIPS)], axis=1)
    wb_f = jnp.concatenate([small4[k, 24:24 + kb, 0:wa_sh] for k in range(N_CHIPS)], axis=1)

    tm = _row_tile(seq + TAIL_ROWS)
    tail = lax.dynamic_update_slice(jnp.zeros((tm, d), F32), meta_f, (seq % tm, 0))
    h, xn1, hp5 = _mm_in(x2, tail, win4, pre_mix_norm, g_token)
    ya, z = _mix_conv_fwd(hp5, wa_f, wb_f, conv_b_bias)
    (wout4,) = _forward_pair(_gather_wait([0], g_ssem, g_rsem, lands[0:1], z, "out"), "out")
    wout_f = wout4.reshape(N_CHIPS * wout4.shape[1], wout4.shape[2])
    yb, mix, h1, xn2 = _mm_out(ya, z, h, wout_f, ln_b_gain, ln_b_bias, post_mix_norm, pre_ffn_norm)
    wg4, wu4 = _forward_pair(_gather_wait([1, 2], g_ssem, g_rsem, lands[1:3], xn2, "gate_up"), "gate_up")
    stacked = lambda a: a.reshape(a.shape[0] * a.shape[1], a.shape[2])
    wg_f, wu_f = stacked(wg4), stacked(wu4)
    p_act, q_act, f_act = _ffn_up(xn2, wg_f, wu_f)
    (wd4,) = _forward_pair(_gather_wait([3], g_ssem, g_rsem, lands[3:4], f_act, "down"), "down")
    wd_f = stacked(wd4)
    dff, dh2, loss_blk, d_gpf = _ffn_down(f_act, wd_f, h1, tgt2, post_ffn_norm)

    da, du = _ffn_bwd_act(dff, wd_f, p_act, q_act)
    by_chip = lambda g: g.reshape(N_CHIPS, g.shape[0] // N_CHIPS, g.shape[1])
    g_down = by_chip(_grad_w_down(f_act, dff))
    g_gate, g_up = [by_chip(g) for g in _grad_w_gate_up(xn2, da, du)]
    ffn = [g_gate, g_up, g_down]
    p_ssem, p_rsem, ffn, p_lands, p_token = _pair_exchange_start(ffn, [1, 1, 1], dff, "ffn")
    dh1, dmix, d_g2, d_gpm = _ffn_bwd_in(da, du, wg_f, wu_f, h1, mix, dh2, pre_ffn_norm, post_mix_norm, p_token)
    ffn, got = _pair_exchange_wait(p_ssem, p_rsem, ffn, p_lands, [1, 1, 1], d_g2, "ffn")
    parts = _pair_sum(ffn, got, c_arr, False, "ffn")
    f_ssem, f_rsem, parts, f_lands, f_token = _chip_exchange_start(parts, dff, "ffn")
    g_out = _grad_w_out(ya, yb, dmix, f_token)
    dya, dz, d_lg, d_lb = _mix_bwd_out(dmix, wout_f, z, ln_b_gain, ln_b_bias, f_token)
    dhp5, d_wa, d_wb, d_bb = _mix_conv_bwd(hp5, dya, dz, wa_f, wb_f)
    g_in = _grad_w_in(xn1, dhp5)

    g_out4 = g_out.reshape(N_CHIPS, g_out.shape[0] // N_CHIPS, g_out.shape[1])
    mixw = [g_in, g_out4]
    got2 = _pair_exchange_grads(mixw, [0, 1], "mix")
    parts2 = _pair_sum(mixw[0:1], got2[0:1], c_arr, True, "in") + _pair_sum(mixw[1:2], got2[1:2], c_arr, False, "out")
    m_ssem, m_rsem, parts2, m_lands, m_token = _chip_exchange_start(parts2, dhp5, "mix")
    grad_x2, d_meta, d_g1 = _mix_bwd_in(dhp5, win4, h, dh1, pre_mix_norm, m_token)
    grad_x = grad_x2[None]

    parts, f_recv = _chip_exchange_wait(f_ssem, f_rsem, parts, f_lands, d_g1, "ffn")
    halves = _chip_sum(parts, f_recv, qc_arr, "ffn")
    gsum_ffn = _pair_share_grads(halves, "ffn")

    names_big = ["w_in", "w_out", "w_gate", "w_up", "w_down"]
    w_big = dict(zip(names_big, [w_in2, w_out2, w_gate2, w_up2, w_down2]))
    m_big = dict(zip(names_big, [m_w_in[0], m_w_out[0], tr(m_w_gate), tr(m_w_up), m_w_down[0]]))
    v_big = dict(zip(names_big, [v_w_in[0], v_w_out[0], tr(v_w_gate), tr(v_w_up), v_w_down[0]]))
    grads, deltas, new_m, new_v = {}, {}, {}, {}

    def update(names, gs, tag):
        operands = ([w_big[k] for k in names], gs, [m_big[k] for k in names], [v_big[k] for k in names])
        res = _adamw_sparsecore(*operands) if tag == "sparsecore" else _adamw_big(*operands, tag)
        for nm, outs in zip(names, res):
            if nm in ("w_gate", "w_up"):
                outs = [jnp.swapaxes(o[None], 1, 2) for o in outs]
            else:
                outs = [o[None] for o in outs]
            grads[nm], deltas[nm], new_m[nm], new_v[nm] = outs
        return res[-1][1]

    update(["w_down"], [gsum_ffn[2]], "sparsecore")
    last = update(["w_gate", "w_up"], list(gsum_ffn[0:2]), "ffn")

    hw = d // 2
    assert d_wa.shape == (3, hw) and d_wb.shape == (31, hw) and d_bb.shape == (1, hw)
    small_parts = [d_meta, d_g1, d_gpm, d_g2, d_gpf, d_bb, d_lg, d_lb, loss_blk[0:1, :], d_wa, d_wb]
    places = [[(0, 0, 0, N_META)], [(16, 0, 0, 1)], [(17, 0, 0, 1)], [(18, 0, 0, 1)], [(19, 0, 0, 1)],
              [(20, 0, 0, 1)], [(20, hw, 0, 1)], [(21, 0, 0, 1)], [(21, hw, 0, 1)], [(22, 0, 0, 3)],
              [(22, hw, 0, 3), (25, 0, 3, 14), (25, hw, 17, 14)]]
    names_small = ["meta_tokens", "pre_mix_norm", "conv_a_w", "conv_b_w", "conv_b_bias", "ln_b_gain", "ln_b_bias",
                   "post_mix_norm", "pre_ffn_norm", "post_ffn_norm"]
    takes = [[(0, N_META, 0, 0, True)], [(0, 1, 16, 0, False)], [(0, 3, 22, 0, True)],
             [(0, 3, 22, hw, True), (3, 14, 25, 0, True), (17, 14, 25, hw, True)], [(0, 1, 20, 0, False)],
             [(0, 1, 20, hw, False)], [(0, 1, 21, 0, False)], [(0, 1, 17, 0, False)], [(0, 1, 18, 0, False)],
             [(0, 1, 19, 0, False)]]
    w_small = [meta_tokens, pre_mix_norm, conv_a_w[0], conv_b_w[0], conv_b_bias, ln_b_gain, ln_b_bias, post_mix_norm,
               pre_ffn_norm, post_ffn_norm]
    m_small = [m_meta_tokens, m_pre_mix_norm, m_conv_a_w[0], m_conv_b_w[0], m_conv_b_bias, m_ln_b_gain, m_ln_b_bias,
               m_post_mix_norm, m_pre_ffn_norm, m_post_ffn_norm]
    v_small = [v_meta_tokens, v_pre_mix_norm, v_conv_a_w[0], v_conv_b_w[0], v_conv_b_bias, v_ln_b_gain, v_ln_b_bias,
               v_post_mix_norm, v_pre_ffn_norm, v_post_ffn_norm]
    red = _small_allreduce(small_parts, places, SMALL_ROWS, d, last)
    g_s, d_s, m_s, v_s, loss_row = _small_update(red, q_arr, takes, (21, hw), w_small, m_small, v_small)
    loss = loss_row[0, 0]
    for i, nm in enumerate(names_small):
        lead = nm in ("conv_a_w", "conv_b_w")
        fix = (lambda a: a[None]) if lead else (lambda a: a)
        grads[nm], deltas[nm], new_m[nm], new_v[nm] = fix(g_s[i]), fix(d_s[i]), fix(m_s[i]), fix(v_s[i])

    parts2, m_recv = _chip_exchange_wait(m_ssem, m_rsem, parts2, m_lands, loss_row, "mix")
    halves2 = _chip_sum(parts2[0:1], m_recv[0:1], qc_arr, "in") + _chip_sum(parts2[1:2], m_recv[1:2], qc_arr, "out")
    gsum_mix = _pair_share_grads(halves2, "mix")
    update(["w_in"], [gsum_mix[0]], "w_in")
    update(["w_out"], [gsum_mix[1]], "w_out")

    order = ["meta_tokens", "pre_mix_norm", "w_in", "conv_a_w", "conv_b_w", "conv_b_bias", "ln_b_gain", "ln_b_bias", "w_out",
             "post_mix_norm", "pre_ffn_norm", "w_gate", "w_up", "w_down", "post_ffn_norm"]
    return (loss, grad_x, *[grads[k] for k in order], *[deltas[k] for k in order], *[new_m[k] for k in order],
            *[new_v[k] for k in order])
```

```python
import jax
import jax.numpy as jnp
from jax import lax
from jax.experimental import pallas as pl
from jax.experimental.pallas import tpu as pltpu

F32 = jnp.float32
BF16 = jnp.bfloat16
MESH = pl.DeviceIdType.MESH

N_META = 16
TAIL_ROWS = 128
RMS_EPS = 1e-6
LN_EPS = 1e-5
ADAM_LR = 0.001
ADAM_B1 = 0.9
ADAM_B2 = 0.999
ADAM_EPS = 1e-08
ADAM_WD = 0.01
ADAM_STEP = 10

N_CHIPS = 4
LANES = 128
MXU_TILE = 256
CONV_CHUNK = 48
CONV_HIST = 32
ROW_TILE_CAP = 640
VMEM_LIMIT = 56 * 1024 * 1024

NN = (((1,), (0,)), ((), ()))
NT = (((1,), (1,)), ((), ()))
TN = (((0,), (0,)), ((), ()))


def _dot(a, b, dims=NN):
    return lax.dot_general(a, b, dims, preferred_element_type=F32)


def _sig(v):
    return 1.0 / (1.0 + jnp.exp(-v))


def _mean(v):
    return jnp.mean(v, axis=-1, keepdims=True)


def _row_tile(rows):
    best = 16
    for t in range(16, min(rows, ROW_TILE_CAP) + 1, 16):
        if rows % t == 0:
            best = t
    assert rows % best == 0
    return best


def _row_parts(tm):
    if tm % 32:
        return [slice(0, tm)]
    return [slice(0, tm // 2), slice(tm // 2, tm)]


def _concat_shards(w_ref, wcat_ref):
    n_sh, _, csh = w_ref.shape

    @pl.when(pl.program_id(0) == 0)
    def _():
        for k in range(n_sh):
            wcat_ref[:, k * csh:(k + 1) * csh] = w_ref[k]


def _params(semantics=None):
    kw = dict(vmem_limit_bytes=VMEM_LIMIT)
    if semantics is not None:
        kw["dimension_semantics"] = semantics
    return pltpu.CompilerParams(**kw)


def _full(shape):
    nd = len(shape)
    return pl.BlockSpec(shape, lambda *_: (0,) * nd)


def _resident(shape):
    nd = len(shape)
    return pl.BlockSpec(shape, lambda *_: (0,) * nd, pipeline_mode=pl.Buffered(1))


def _sds(shape, dtype):
    return jax.ShapeDtypeStruct(shape, dtype)


ANY = pl.BlockSpec(memory_space=pl.ANY)
VMEM = pl.BlockSpec(memory_space=pltpu.VMEM)


def _mesh_pos():
    return lax.axis_index("x"), lax.axis_index("y"), lax.axis_index("c")


def _flip(v, bit):
    return 1 - v if bit else v


def _mm_in(x, tail, win4, g1, after):
    seq, d = x.shape
    tp = seq + TAIL_ROWS
    tm = _row_tile(tp)
    n_sh, _, csh = win4.shape
    pw = n_sh * csh // 5

    def body(x_ref, tail_ref, w_ref, g_ref, after_ref, h_ref, xn_ref, hp_ref, wcat_ref):
        _concat_shards(w_ref, wcat_ref)
        rows = pl.program_id(0) * tm + lax.broadcasted_iota(jnp.int32, (tm, 1), 0)
        hh = jnp.where(rows < seq, x_ref[...], tail_ref[...])
        h_ref[...] = hh
        r = lax.rsqrt(_mean(hh * hh) + RMS_EPS)
        xn = (hh * r * g_ref[...]).astype(BF16)
        xn_ref[...] = xn
        for p in range(5):
            hp_ref[p] = _dot(xn, wcat_ref[:, p * pw:(p + 1) * pw])

    row = pl.BlockSpec((tm, d), lambda i: (i, 0))
    return pl.pallas_call(
        body, name="mm_in", grid=(tp // tm,),
        in_specs=[row, _full(tail.shape), _resident(win4.shape), _full(g1.shape), ANY],
        out_specs=[row, row, pl.BlockSpec((5, tm, pw), lambda i: (0, i, 0))],
        out_shape=[_sds((tp, d), F32), _sds((tp, d), BF16), _sds((5, tp, pw), F32)],
        scratch_shapes=[pltpu.VMEM((d, n_sh * csh), BF16)],
        compiler_params=_params(("arbitrary",)),
    )(x, tail, win4, g1, after)


def _seq_rows(tp):
    seq = tp - TAIL_ROWS
    nseq = seq + N_META
    assert nseq % CONV_CHUNK == 0 and seq % 16 == 0
    return seq, nseq


SUBLANES = 8


def _conv_offsets(width, transpose):
    return [(width - 1 - k) if transpose else (CONV_HIST - (width - 1) + k) for k in range(width)]


def _shift_copies(src_ref, sh_ref, width, transpose):
    n = src_ref.shape[0] - SUBLANES
    for s in sorted({o % SUBLANES for o in _conv_offsets(width, transpose)} - {0}):
        sh_ref[s - 1, 0:n, :] = src_ref[s:s + n, :]


def _tap_rows(src_ref, sh_ref, base, off):
    start = pl.multiple_of(base + (off // SUBLANES) * SUBLANES, SUBLANES)
    if off % SUBLANES == 0:
        return src_ref[pl.ds(start, CONV_CHUNK), :]
    return sh_ref[off % SUBLANES - 1, pl.ds(start, CONV_CHUNK), :]


def _conv_taps(src_ref, sh_ref, w_ref, dst_ref, width, nseq, transpose):
    w = w_ref[...]
    offs = _conv_offsets(width, transpose)
    _shift_copies(src_ref, sh_ref, width, transpose)

    def step(n, carry):
        out0 = pl.multiple_of(CONV_HIST + n * CONV_CHUNK, SUBLANES)
        base = out0 if transpose else n * CONV_CHUNK
        acc = jnp.zeros((CONV_CHUNK, w.shape[1]), F32)
        for k, off in enumerate(offs):
            acc = acc + w[k:k + 1, :] * _tap_rows(src_ref, sh_ref, base, off)
        dst_ref[pl.ds(out0, CONV_CHUNK), :] = acc
        return carry

    lax.fori_loop(0, nseq // CONV_CHUNK, step, 0)


def _conv_wgrad(src_ref, sh_ref, dz_ref, acc_ref, width, nseq):
    acc_ref[...] = jnp.zeros(acc_ref.shape, F32)
    offs = _conv_offsets(width, False)

    def step(n, carry):
        dzc = dz_ref[pl.ds(pl.multiple_of(CONV_HIST + n * CONV_CHUNK, SUBLANES), CONV_CHUNK), :]
        for k, off in enumerate(offs):
            prod = dzc * _tap_rows(src_ref, sh_ref, n * CONV_CHUNK, off)
            part = prod[0:SUBLANES, :]
            for s in range(1, CONV_CHUNK // SUBLANES):
                part = part + prod[SUBLANES * s:SUBLANES * (s + 1), :]
            acc_ref[SUBLANES * k:SUBLANES * (k + 1), :] += part
        return carry

    lax.fori_loop(0, nseq // CONV_CHUNK, step, 0)


def _to_seq(buf_ref, x_part, meta_part, seq):
    buf_ref[CONV_HIST:CONV_HIST + N_META, :] = meta_part
    buf_ref[CONV_HIST + N_META:CONV_HIST + N_META + seq, :] = x_part


def _zero_ends(buf_ref, nseq):
    zeros = jnp.zeros((CONV_HIST, buf_ref.shape[1]), F32)
    buf_ref[0:CONV_HIST, :] = zeros
    buf_ref[CONV_HIST + nseq:CONV_HIST + nseq + CONV_HIST, :] = zeros


def _mix_conv_fwd(hp5, wa, wb, bb):
    _, tp, wgrp = hp5.shape
    seq, nseq = _seq_rows(tp)
    sb = nseq + 2 * CONV_HIST
    ka, kb = wa.shape[0], wb.shape[0]
    xs, ms = slice(0, seq), slice(seq, seq + N_META)
    ox, om = slice(CONV_HIST + N_META, CONV_HIST + nseq), slice(CONV_HIST, CONV_HIST + N_META)

    def body(hp_ref, wa_ref, wb_ref, bb_ref, ya_ref, z_ref, s_ref, o_ref, sh_ref):
        _zero_ends(s_ref, nseq)
        _to_seq(s_ref, hp_ref[1, xs, :] * hp_ref[2, xs, :], hp_ref[1, ms, :] * hp_ref[2, ms, :], seq)
        _conv_taps(s_ref, sh_ref, wa_ref, o_ref, ka, nseq, False)
        ya_ref[xs, :] = (hp_ref[0, xs, :] * o_ref[ox, :]).astype(BF16)
        ya_ref[ms, :] = (hp_ref[0, ms, :] * o_ref[om, :]).astype(BF16)
        ya_ref[seq + N_META:tp, :] = jnp.zeros((tp - seq - N_META, LANES), BF16)
        _to_seq(s_ref, hp_ref[3, xs, :] * _sig(hp_ref[4, xs, :]), hp_ref[3, ms, :] * _sig(hp_ref[4, ms, :]), seq)
        _conv_taps(s_ref, sh_ref, wb_ref, o_ref, kb, nseq, False)
        z_ref[xs, :] = o_ref[ox, :] + bb_ref[...]
        z_ref[ms, :] = o_ref[om, :] + bb_ref[...]
        z_ref[seq + N_META:tp, :] = jnp.zeros((tp - seq - N_META, LANES), F32)

    col = lambda j: (0, j)
    return pl.pallas_call(
        body, name="mix_conv_fwd", grid=(wgrp // LANES,),
        in_specs=[pl.BlockSpec((5, tp, LANES), lambda j: (0, 0, j)), pl.BlockSpec((ka, LANES), col),
                  pl.BlockSpec((kb, LANES), col), pl.BlockSpec((1, LANES), col)],
        out_specs=[pl.BlockSpec((tp, LANES), col), pl.BlockSpec((tp, LANES), col)],
        out_shape=[_sds((tp, wgrp), BF16), _sds((tp, wgrp), F32)],
        scratch_shapes=[pltpu.VMEM((sb, LANES), F32), pltpu.VMEM((sb, LANES), F32),
                        pltpu.VMEM((SUBLANES - 1, sb, LANES), F32)],
        compiler_params=_params(("arbitrary",)),
    )(hp5, wa, wb, bb)


def _layer_norm_parts(z, lg, lb):
    mu = _mean(z)
    zc = z - mu
    rl = lax.rsqrt(_mean(zc * zc) + LN_EPS)
    zh = zc * rl
    return rl, zh, zh * lg + lb


def _mm_out(ya, z, h, wout, lg, lb, gpm, g2):
    tp, d = h.shape
    wa_ = ya.shape[1]
    tm = _row_tile(tp)

    def body(ya_ref, z_ref, h_ref, w_ref, lg_ref, lb_ref, gpm_ref, g2_ref, yb_ref, mix_ref, h1_ref, xn2_ref):
        for rs in _row_parts(tm):
            _, _, l = _layer_norm_parts(z_ref[rs, :], lg_ref[...], lb_ref[...])
            yb = (l * _sig(l)).astype(BF16)
            yb_ref[rs, :] = yb
            mix = _dot(ya_ref[rs, :], w_ref[0:wa_, :]) + _dot(yb, w_ref[wa_:d, :])
            mix_ref[rs, :] = mix
            rm = lax.rsqrt(_mean(mix * mix) + RMS_EPS)
            h1 = h_ref[rs, :] + mix * rm * gpm_ref[...]
            h1_ref[rs, :] = h1
            r2 = lax.rsqrt(_mean(h1 * h1) + RMS_EPS)
            xn2_ref[rs, :] = (h1 * r2 * g2_ref[...]).astype(BF16)

    row = lambda i: (i, 0)
    return pl.pallas_call(
        body, name="mm_out", grid=(tp // tm,),
        in_specs=[pl.BlockSpec((tm, wa_), row), pl.BlockSpec((tm, wa_), row), pl.BlockSpec((tm, d), row),
                  _resident(wout.shape), _full(lg.shape), _full(lb.shape), _full(gpm.shape), _full(g2.shape)],
        out_specs=[pl.BlockSpec((tm, wa_), row), pl.BlockSpec((tm, d), row), pl.BlockSpec((tm, d), row),
                   pl.BlockSpec((tm, d), row)],
        out_shape=[_sds((tp, wa_), BF16), _sds((tp, d), F32), _sds((tp, d), F32), _sds((tp, d), BF16)],
        compiler_params=_params(("arbitrary",)),
    )(ya, z, h, wout, lg, lb, gpm, g2)


def _ffn_up(xn2, wg, wu):
    tp, d = xn2.shape
    ff_dim = wg.shape[0]
    tm = _row_tile(tp)
    assert ff_dim % MXU_TILE == 0

    def body(xn_ref, wg_ref, wu_ref, p_ref, q_ref, f_ref):
        xn = xn_ref[...]
        for lo in range(0, ff_dim, MXU_TILE):
            cols = slice(lo, lo + MXU_TILE)
            a = _dot(xn, wg_ref[cols, :], NT)
            u = _dot(xn, wu_ref[cols, :], NT)
            s = _sig(a)
            q = a * s
            p_ref[:, cols] = (u * (s + q * (1.0 - s))).astype(BF16)
            q_ref[:, cols] = q.astype(BF16)
            f_ref[:, cols] = (q * u).astype(BF16)

    ospec = pl.BlockSpec((tm, ff_dim), lambda i: (i, 0))
    return pl.pallas_call(
        body, name="ffn_up", grid=(tp // tm,),
        in_specs=[pl.BlockSpec((tm, d), lambda i: (i, 0)), _resident(wg.shape), _resident(wu.shape)],
        out_specs=[ospec, ospec, ospec],
        out_shape=[_sds((tp, ff_dim), BF16)] * 3,
        compiler_params=_params(("arbitrary",)),
    )(xn2, wg, wu)


def _ffn_down(f, wd, h1, tgt, gpf):
    tp, ff_dim = f.shape
    d = h1.shape[1]
    tm = _row_tile(tp)
    n = tp // tm
    seq, _ = _seq_rows(tp)

    def body(f_ref, w_ref, h1_ref, t_ref, gpf_ref, dff_ref, dh2_ref, loss_ref, dgpf_ref, ff_ref):
        i = pl.program_id(0)

        def matmul():
            return _dot(f_ref[...], w_ref[...])

        def epilogue(ff):
            gpf_ = gpf_ref[...]
            rf = lax.rsqrt(_mean(ff * ff) + RMS_EPS)
            nf = ff * rf
            h2 = h1_ref[...] + nf * gpf_
            rows = (i - 1) * tm + lax.broadcasted_iota(jnp.int32, (tm, 1), 0)
            err = jnp.where(rows < seq, h2 - t_ref[...], 0.0)
            dh2 = err * (1.0 / d)
            dh2_ref[...] = dh2
            dn = dh2 * gpf_
            dff_ref[...] = (rf * (dn - nf * _mean(dn * nf))).astype(BF16)
            loss_ref[...] += (0.5 / d) * jnp.sum(err * err, axis=(0, 1), keepdims=True)
            dgpf_ref[...] += jnp.sum(dh2 * nf, axis=0, keepdims=True)

        @pl.when(i == 0)
        def _():
            loss_ref[...] = jnp.zeros(loss_ref.shape, F32)
            dgpf_ref[...] = jnp.zeros(dgpf_ref.shape, F32)
            ff_ref[0] = matmul()

        @pl.when((i > 0) & (i < n))
        def _():
            before = ff_ref[(i - 1) % 2]
            ff_ref[i % 2] = matmul()
            epilogue(before)

        @pl.when(i == n)
        def _():
            epilogue(ff_ref[(n - 1) % 2])

    ahead = lambda i: (jnp.minimum(i, n - 1), 0)
    behind = lambda i: (jnp.maximum(i - 1, 0), 0)
    return pl.pallas_call(
        body, name="ffn_down", grid=(n + 1,),
        in_specs=[pl.BlockSpec((tm, ff_dim), ahead), _resident(wd.shape), pl.BlockSpec((tm, d), behind),
                  pl.BlockSpec((tm, d), behind), _full(gpf.shape)],
        out_specs=[pl.BlockSpec((tm, d), behind), pl.BlockSpec((tm, d), behind), _full((8, LANES)), _full((1, d))],
        out_shape=[_sds((tp, d), BF16), _sds((tp, d), F32), _sds((8, LANES), F32), _sds((1, d), F32)],
        scratch_shapes=[pltpu.VMEM((2, tm, d), F32)],
        compiler_params=_params(("arbitrary",)),
    )(f, wd, h1, tgt, gpf)


def _ffn_bwd_act(dff, wd, p, q):
    tp, d = dff.shape
    ff_dim = wd.shape[0]
    tm = _row_tile(tp)

    def body(dff_ref, w_ref, p_ref, q_ref, da_ref, du_ref):
        dffv = dff_ref[...]
        for lo in range(0, ff_dim, MXU_TILE):
            cols = slice(lo, lo + MXU_TILE)
            df = _dot(dffv, w_ref[cols, :], NT).astype(BF16)
            da_ref[:, cols] = df * p_ref[:, cols]
            du_ref[:, cols] = df * q_ref[:, cols]

    aspec = pl.BlockSpec((tm, ff_dim), lambda i: (i, 0))
    return pl.pallas_call(
        body, name="ffn_bwd_act", grid=(tp // tm,),
        in_specs=[pl.BlockSpec((tm, d), lambda i: (i, 0)), _resident(wd.shape), aspec, aspec],
        out_specs=[aspec, aspec],
        out_shape=[_sds((tp, ff_dim), BF16)] * 2,
        compiler_params=_params(("arbitrary",)),
    )(dff, wd, p, q)


def _grad_blocks(ff_dim):
    rows = ff_dim // 2
    assert rows % LANES == 0
    return rows


def _grad_w_down(f, dff):
    tp, ff_dim = f.shape
    d = dff.shape[1]
    rows = _grad_blocks(ff_dim)

    def body(f_ref, dff_ref, g_ref):
        g_ref[...] = _dot(f_ref[...], dff_ref[...], TN).astype(BF16)

    return pl.pallas_call(
        body, name="grad_w_down", grid=(ff_dim // rows,),
        in_specs=[pl.BlockSpec((tp, rows), lambda k: (0, k)), _resident(dff.shape)],
        out_specs=pl.BlockSpec((rows, d), lambda k: (k, 0)),
        out_shape=_sds((ff_dim, d), BF16),
        compiler_params=_params(("arbitrary",)),
    )(f, dff)


def _grad_w_gate_up(xn2, da, du):
    tp, ff_dim = da.shape
    d = xn2.shape[1]
    rows = _grad_blocks(ff_dim)

    def body(xn_ref, da_ref, du_ref, gg_ref, gu_ref):
        xn = xn_ref[...]
        gg_ref[...] = _dot(da_ref[...], xn, TN).astype(BF16)
        gu_ref[...] = _dot(du_ref[...], xn, TN).astype(BF16)

    aspec = pl.BlockSpec((tp, rows), lambda k: (0, k))
    gspec = pl.BlockSpec((rows, d), lambda k: (k, 0))
    return pl.pallas_call(
        body, name="grad_w_gate_up", grid=(ff_dim // rows,),
        in_specs=[_resident(xn2.shape), aspec, aspec],
        out_specs=[gspec, gspec],
        out_shape=[_sds((ff_dim, d), BF16)] * 2,
        compiler_params=_params(("arbitrary",)),
    )(xn2, da, du)


def _rms_bwd(dy, x, r, g):
    n = x * r
    dn = dy * g
    return r * (dn - n * _mean(dn * n)), dy * n


def _ffn_bwd_in(da, du, wg, wu, h1, mix, dh2, g2, gpm, after):
    tp, ff_dim = da.shape
    d = h1.shape[1]
    tm = _row_tile(tp)
    n = tp // tm

    def body(da_ref, du_ref, wg_ref, wu_ref, h1_ref, mix_ref, dh2_ref, g2_ref, gpm_ref, after_ref,
             dh1_ref, dmix_ref, dg2_ref, dgpm_ref, dxn_ref):
        i = pl.program_id(0)

        def matmul():
            return _dot(da_ref[...], wg_ref[...]) + _dot(du_ref[...], wu_ref[...])

        def epilogue(dxn):
            h1v = h1_ref[...]
            r2 = lax.rsqrt(_mean(h1v * h1v) + RMS_EPS)
            dres, dg2_rows = _rms_bwd(dxn, h1v, r2, g2_ref[...])
            dh1 = dh2_ref[...] + dres
            dh1_ref[...] = dh1
            mixv = mix_ref[...]
            rm = lax.rsqrt(_mean(mixv * mixv) + RMS_EPS)
            dmix, dgpm_rows = _rms_bwd(dh1, mixv, rm, gpm_ref[...])
            dmix_ref[...] = dmix.astype(BF16)
            dg2_ref[...] += jnp.sum(dg2_rows, axis=0, keepdims=True)
            dgpm_ref[...] += jnp.sum(dgpm_rows, axis=0, keepdims=True)

        @pl.when(i == 0)
        def _():
            dg2_ref[...] = jnp.zeros(dg2_ref.shape, F32)
            dgpm_ref[...] = jnp.zeros(dgpm_ref.shape, F32)
            dxn_ref[0] = matmul()

        @pl.when((i > 0) & (i < n))
        def _():
            before = dxn_ref[(i - 1) % 2]
            dxn_ref[i % 2] = matmul()
            epilogue(before)

        @pl.when(i == n)
        def _():
            epilogue(dxn_ref[(n - 1) % 2])

    aspec = pl.BlockSpec((tm, ff_dim), lambda i: (jnp.minimum(i, n - 1), 0))
    row = pl.BlockSpec((tm, d), lambda i: (jnp.maximum(i - 1, 0), 0))
    return pl.pallas_call(
        body, name="ffn_bwd_in", grid=(n + 1,),
        in_specs=[aspec, aspec, _resident(wg.shape), _resident(wu.shape), row, row, row, _full(g2.shape), _full(gpm.shape),
                  ANY],
        out_specs=[row, row, _full((1, d)), _full((1, d))],
        out_shape=[_sds((tp, d), F32), _sds((tp, d), BF16), _sds((1, d), F32), _sds((1, d), F32)],
        scratch_shapes=[pltpu.VMEM((2, tm, d), F32)],
        compiler_params=_params(("arbitrary",)),
    )(da, du, wg, wu, h1, mix, dh2, g2, gpm, after)


def _grad_w_out(ya, yb, dmix, after):
    tp, wa_ = ya.shape
    d = dmix.shape[1]

    def body(ya_ref, yb_ref, dmix_ref, after_ref, g_ref):
        dm = dmix_ref[...]
        g_ref[0:wa_, :] = _dot(ya_ref[...], dm, TN).astype(BF16)
        g_ref[wa_:2 * wa_, :] = _dot(yb_ref[...], dm, TN).astype(BF16)

    return pl.pallas_call(
        body, name="grad_w_out", grid=(1,),
        in_specs=[_full(ya.shape), _full(yb.shape), _full(dmix.shape), ANY],
        out_specs=_full((2 * wa_, d)),
        out_shape=_sds((2 * wa_, d), BF16),
        compiler_params=_params(("arbitrary",)),
    )(ya, yb, dmix, after)


def _mix_bwd_out(dmix, wout, z, lg, lb, after):
    tp, d = dmix.shape
    wa_ = z.shape[1]
    tm = _row_tile(tp)

    def body(dmix_ref, w_ref, z_ref, lg_ref, lb_ref, after_ref, dya_ref, dz_ref, dlg_ref, dlb_ref):
        i = pl.program_id(0)
        lg_ = lg_ref[...]

        @pl.when(i == 0)
        def _():
            dlg_ref[...] = jnp.zeros(dlg_ref.shape, F32)
            dlb_ref[...] = jnp.zeros(dlb_ref.shape, F32)

        for rs in _row_parts(tm):
            dm = dmix_ref[rs, :]
            dya_ref[rs, :] = _dot(dm, w_ref[0:wa_, :], NT)
            dyb = _dot(dm, w_ref[wa_:d, :], NT)
            rl, zh, l = _layer_norm_parts(z_ref[rs, :], lg_, lb_ref[...])
            sl = _sig(l)
            dl = dyb * (sl * (1.0 + l * (1.0 - sl)))
            dzh = dl * lg_
            dz_ref[rs, :] = rl * (dzh - _mean(dzh) - zh * _mean(dzh * zh))
            dlg_ref[...] += jnp.sum(dl * zh, axis=0, keepdims=True)
            dlb_ref[...] += jnp.sum(dl, axis=0, keepdims=True)

    row = lambda i: (i, 0)
    return pl.pallas_call(
        body, name="mix_bwd_out", grid=(tp // tm,),
        in_specs=[pl.BlockSpec((tm, d), row), _resident(wout.shape), pl.BlockSpec((tm, wa_), row), _full(lg.shape),
                  _full(lb.shape), ANY],
        out_specs=[pl.BlockSpec((tm, wa_), row), pl.BlockSpec((tm, wa_), row), _full((1, wa_)), _full((1, wa_))],
        out_shape=[_sds((tp, wa_), F32), _sds((tp, wa_), F32), _sds((1, wa_), F32), _sds((1, wa_), F32)],
        compiler_params=_params(("arbitrary",)),
    )(dmix, wout, z, lg, lb, after)


def _mix_conv_bwd(hp5, dya, dz, wa, wb):
    _, tp, wgrp = hp5.shape
    seq, nseq = _seq_rows(tp)
    sb = nseq + 2 * CONV_HIST
    ka, kb = wa.shape[0], wb.shape[0]
    xs, ms = slice(0, seq), slice(seq, seq + N_META)
    ox, om = slice(CONV_HIST + N_META, CONV_HIST + nseq), slice(CONV_HIST, CONV_HIST + N_META)
    n_tail = tp - seq - N_META

    def body(hp_ref, dya_ref, dz_ref, wa_ref, wb_ref, dhp_ref, dwa_ref, dwb_ref, dbb_ref, s_ref, d_ref, o_ref, acc_ref,
             shs_ref, shd_ref):
        _zero_ends(s_ref, nseq)
        _zero_ends(d_ref, nseq)

        def put(p, ox_val, om_val):
            dhp_ref[p, xs, :] = ox_val.astype(BF16)
            dhp_ref[p, ms, :] = om_val.astype(BF16)
            dhp_ref[p, seq + N_META:tp, :] = jnp.zeros((n_tail, LANES), BF16)

        def wgrad(dw_ref, width):
            for k in range(width):
                dw_ref[k:k + 1, :] = jnp.sum(acc_ref[8 * k:8 * k + 8, :], axis=0, keepdims=True)

        _to_seq(s_ref, hp_ref[1, xs, :] * hp_ref[2, xs, :], hp_ref[1, ms, :] * hp_ref[2, ms, :], seq)
        _conv_taps(s_ref, shs_ref, wa_ref, o_ref, ka, nseq, False)
        put(0, dya_ref[xs, :] * o_ref[ox, :], dya_ref[ms, :] * o_ref[om, :])
        _to_seq(d_ref, dya_ref[xs, :] * hp_ref[0, xs, :], dya_ref[ms, :] * hp_ref[0, ms, :], seq)
        _conv_wgrad(s_ref, shs_ref, d_ref, acc_ref, ka, nseq)
        wgrad(dwa_ref, ka)
        _conv_taps(d_ref, shd_ref, wa_ref, o_ref, ka, nseq, True)
        put(1, o_ref[ox, :] * hp_ref[2, xs, :], o_ref[om, :] * hp_ref[2, ms, :])
        put(2, o_ref[ox, :] * hp_ref[1, xs, :], o_ref[om, :] * hp_ref[1, ms, :])

        _to_seq(s_ref, hp_ref[3, xs, :] * _sig(hp_ref[4, xs, :]), hp_ref[3, ms, :] * _sig(hp_ref[4, ms, :]), seq)
        _to_seq(d_ref, dz_ref[xs, :], dz_ref[ms, :], seq)
        dbb_ref[...] = (jnp.sum(dz_ref[xs, :], axis=0, keepdims=True)
                        + jnp.sum(dz_ref[ms, :], axis=0, keepdims=True))
        _shift_copies(s_ref, shs_ref, kb, False)
        _conv_wgrad(s_ref, shs_ref, d_ref, acc_ref, kb, nseq)
        wgrad(dwb_ref, kb)
        _conv_taps(d_ref, shd_ref, wb_ref, o_ref, kb, nseq, True)
        sx, sm = _sig(hp_ref[4, xs, :]), _sig(hp_ref[4, ms, :])
        put(3, o_ref[ox, :] * sx, o_ref[om, :] * sm)
        put(4, o_ref[ox, :] * hp_ref[3, xs, :] * sx * (1.0 - sx), o_ref[om, :] * hp_ref[3, ms, :] * sm * (1.0 - sm))

    col = lambda j: (0, j)
    blk5 = pl.BlockSpec((5, tp, LANES), lambda j: (0, 0, j))
    return pl.pallas_call(
        body, name="mix_conv_bwd", grid=(wgrp // LANES,),
        in_specs=[blk5, pl.BlockSpec((tp, LANES), col), pl.BlockSpec((tp, LANES), col),
                  pl.BlockSpec((ka, LANES), col), pl.BlockSpec((kb, LANES), col)],
        out_specs=[blk5, pl.BlockSpec((ka, LANES), col), pl.BlockSpec((kb, LANES), col), pl.BlockSpec((1, LANES), col)],
        out_shape=[_sds((5, tp, wgrp), BF16), _sds((ka, wgrp), F32), _sds((kb, wgrp), F32), _sds((1, wgrp), F32)],
        scratch_shapes=[pltpu.VMEM((sb, LANES), F32), pltpu.VMEM((sb, LANES), F32), pltpu.VMEM((sb, LANES), F32),
                        pltpu.VMEM((SUBLANES * kb, LANES), F32), pltpu.VMEM((SUBLANES - 1, sb, LANES), F32),
                        pltpu.VMEM((SUBLANES - 1, sb, LANES), F32)],
        compiler_params=_params(("arbitrary",)),
    )(hp5, dya, dz, wa, wb)


def _grad_w_in(xn1, dhp5):
    n_p, tp, pw = dhp5.shape
    d = xn1.shape[1]

    def body(xn_ref, dhp_ref, g_ref):
        g_ref[...] = _dot(xn_ref[...], dhp_ref[0], TN).astype(BF16)

    return pl.pallas_call(
        body, name="grad_w_in", grid=(n_p,),
        in_specs=[_resident(xn1.shape), pl.BlockSpec((1, tp, pw), lambda p: (p, 0, 0))],
        out_specs=pl.BlockSpec((d, pw), lambda p: (0, p)),
        out_shape=_sds((d, n_p * pw), BF16),
        compiler_params=_params(("arbitrary",)),
    )(xn1, dhp5)


def _mix_bwd_in(dhp5, win4, h, dh1, g1, after):
    n_p, tp, pw = dhp5.shape
    d = h.shape[1]
    n_sh, _, csh = win4.shape
    tm = _row_tile(tp)

    seq, _ = _seq_rows(tp)
    last, meta_off = seq // tm, seq % tm
    assert last == tp // tm - 1
    assert any(rs.start <= meta_off and meta_off + N_META <= rs.stop for rs in _row_parts(tm))

    def body(dhp_ref, w_ref, h_ref, dh1_ref, g_ref, after_ref, gx_ref, dmeta_ref, dg1_ref, wcat_ref):
        i = pl.program_id(0)
        _concat_shards(w_ref, wcat_ref)

        @pl.when(i == 0)
        def _():
            dg1_ref[...] = jnp.zeros(dg1_ref.shape, F32)

        for rs in _row_parts(tm):
            dxn = _dot(dhp_ref[0, rs, :], wcat_ref[:, 0:pw], NT)
            for p in range(1, n_p):
                dxn = dxn + _dot(dhp_ref[p, rs, :], wcat_ref[:, p * pw:(p + 1) * pw], NT)
            hh = h_ref[rs, :]
            r1 = lax.rsqrt(_mean(hh * hh) + RMS_EPS)
            dres, dg_rows = _rms_bwd(dxn, hh, r1, g_ref[...])
            dh = dh1_ref[rs, :] + dres
            gx_ref[rs, :] = dh
            dg1_ref[...] += jnp.sum(dg_rows, axis=0, keepdims=True)
            if rs.start <= meta_off and meta_off + N_META <= rs.stop:
                @pl.when(i == last)
                def _():
                    dmeta_ref[...] = dh[meta_off - rs.start:meta_off - rs.start + N_META, :]

    row = lambda i: (i, 0)
    return pl.pallas_call(
        body, name="mix_bwd_in", grid=(tp // tm,),
        in_specs=[pl.BlockSpec((n_p, tm, pw), lambda i: (0, i, 0)), _resident(win4.shape), pl.BlockSpec((tm, d), row),
                  pl.BlockSpec((tm, d), row), _full(g1.shape), ANY],
        out_specs=[pl.BlockSpec((tm, d), row), _full((N_META, d)), _full((1, d))],
        out_shape=[_sds((seq, d), F32), _sds((N_META, d), F32), _sds((1, d), F32)],
        scratch_shapes=[pltpu.VMEM((d, n_sh * csh), BF16)],
        compiler_params=_params(("arbitrary",)),
    )(dhp5, win4, h, dh1, g1, after)


def _other_chips(x, y):
    out = []
    for j in (1, 2, 3):
        px, py = _flip(x, j >> 1), _flip(y, j & 1)
        out.append((px, py, 2 * px + py))
    return out


PAIR_COLLECTIVE_ID = 0


def _pair_barrier(x, y, c):
    sem = pltpu.get_barrier_semaphore()
    pl.semaphore_signal(sem, inc=1, device_id=(x, y, 1 - c), device_id_type=MESH)
    pl.semaphore_wait(sem, 1)


def _pair_params():
    return pltpu.CompilerParams(collective_id=PAIR_COLLECTIVE_ID)


def _half_rows(c, rows_half):
    return pl.ds(pl.multiple_of(c * rows_half, 8), rows_half)


def _cast_place(ws, q_arr, tag, after=None):
    n = len(ws)
    rows, cols = ws[0].shape
    tr = _row_tile(rows)
    extra = [] if after is None else [after]

    def body(q_ref, *refs):
        for w_ref, out_ref in zip(refs[:n], refs[n + len(extra):]):
            out_ref[0] = w_ref[...].astype(BF16)

    return list(pl.pallas_call(
        body, name="cast_place_" + tag,
        grid_spec=pltpu.PrefetchScalarGridSpec(
            num_scalar_prefetch=1, grid=(rows // tr,),
            in_specs=[pl.BlockSpec((tr, cols), lambda i, q: (i, 0))] * n + [ANY] * len(extra),
            out_specs=[pl.BlockSpec((1, tr, cols), lambda i, q: (q[0], i, 0))] * n),
        out_shape=[_sds((N_CHIPS, rows, cols), BF16)] * n,
        compiler_params=_params(("arbitrary",)),
    )(q_arr, *ws, *extra))


HBM = pl.BlockSpec(memory_space=pltpu.HBM)
SEM = pl.BlockSpec(memory_space=pltpu.SEMAPHORE)
EFFECT = pltpu.SideEffectType.DATAFLOW_SIDE_EFFECTING


def _in_hbm(a):
    return pltpu.with_memory_space_constraint(a, pltpu.HBM)


def _gather_start(fulls, after, tag):
    n = len(fulls)
    halves = [a.shape[1] // 2 for a in fulls]

    def body(*refs):
        land = refs[:n]
        ssem, rsem = refs[n + 1], refs[n + 2]
        token = refs[-1]
        x, y, c = _mesh_pos()
        q = 2 * x + y
        for i in range(n):
            for j, (px, py, _) in enumerate(_other_chips(x, y)):
                mine = land[i].at[q, _half_rows(c, halves[i]), :]
                pltpu.make_async_remote_copy(src_ref=mine, dst_ref=mine, send_sem=ssem.at[3 * i + j],
                                             recv_sem=rsem.at[3 * i + j], device_id=(px, py, c), device_id_type=MESH).start()
        token[...] = jnp.zeros(token.shape, F32)

    outs = pl.pallas_call(
        body, name="gather_start_" + tag,
        in_specs=[HBM] * n + [ANY], out_specs=[SEM, SEM] + [HBM] * n + [VMEM],
        out_shape=[pltpu.SemaphoreType.DMA((3 * n,)), pltpu.SemaphoreType.DMA((3 * n,))]
        + [pltpu.HBM(a.shape, a.dtype) for a in fulls] + [_sds((8, LANES), F32)],
        input_output_aliases={i: 2 + i for i in range(n)},
        compiler_params=pltpu.CompilerParams(has_side_effects=EFFECT),
    )(*[_in_hbm(a) for a in fulls], after)
    return outs[0], outs[1], list(outs[2:2 + n]), outs[-1]


def _gather_wait(which, ssem, rsem, lands, after, tag):
    m = len(which)
    halves = [a.shape[1] // 2 for a in lands]

    def body(*refs):
        land = refs[:m]
        ssem_, rsem_ = refs[m], refs[m + 1]
        x, y, c = _mesh_pos()
        for t, i in enumerate(which):
            for j, (px, py, qj) in enumerate(_other_chips(x, y)):
                rows = _half_rows(c, halves[t])
                cp = pltpu.make_async_remote_copy(src_ref=land[t].at[2 * x + y, rows, :], dst_ref=land[t].at[qj, rows, :],
                                                  send_sem=ssem_.at[3 * i + j], recv_sem=rsem_.at[3 * i + j],
                                                  device_id=(px, py, c), device_id_type=MESH)
                cp.wait_send()
                cp.wait_recv()

    outs = pl.pallas_call(
        body, name="gather_wait_" + tag,
        in_specs=[HBM] * m + [SEM, SEM, ANY], out_specs=[HBM] * m,
        out_shape=[pltpu.HBM(a.shape, a.dtype) for a in lands],
        input_output_aliases={i: i for i in range(m)},
        compiler_params=pltpu.CompilerParams(has_side_effects=EFFECT),
    )(*lands, ssem, rsem, after)
    return list(outs)


def _forward_pair(lands, tag):
    n = len(lands)
    halves = [a.shape[1] // 2 for a in lands]

    def body(*refs):
        full = refs[n:2 * n]
        ssem, rsem = refs[2 * n:]
        x, y, c = _mesh_pos()
        _pair_barrier(x, y, c)
        cps = []
        for i in range(n):
            for j, (_, _, qj) in enumerate(_other_chips(x, y)):
                part = full[i].at[qj, _half_rows(c, halves[i]), :]
                cp = pltpu.make_async_remote_copy(src_ref=part, dst_ref=part, send_sem=ssem.at[3 * i + j],
                                                  recv_sem=rsem.at[3 * i + j], device_id=(x, y, 1 - c), device_id_type=MESH)
                cp.start()
                cps.append(cp)
        for cp in cps:
            cp.wait()

    return pl.pallas_call(
        body, name="forward_pair_" + tag,
        in_specs=[ANY] * n, out_specs=[ANY] * n,
        out_shape=[_sds(a.shape, a.dtype) for a in lands],
        input_output_aliases={i: i for i in range(n)},
        scratch_shapes=[pltpu.SemaphoreType.DMA((3 * n,)), pltpu.SemaphoreType.DMA((3 * n,))],
        compiler_params=_pair_params(),
    )(*lands)


def _chip_exchange_start(parts, after, tag):
    n = len(parts)

    def body(*refs):
        src, land = refs[:n], refs[n:2 * n]
        ssem, rsem = refs[2 * n + 1], refs[2 * n + 2]
        token = refs[-1]
        x, y, c = _mesh_pos()
        for i in range(n):
            for j, (px, py, qj) in enumerate(_other_chips(x, y)):
                pltpu.make_async_remote_copy(src_ref=src[i].at[qj], dst_ref=land[i].at[j], send_sem=ssem.at[3 * i + j],
                                             recv_sem=rsem.at[3 * i + j], device_id=(px, py, c), device_id_type=MESH).start()
        token[...] = jnp.zeros(token.shape, F32)

    lands = [lax.empty((3,) + a.shape[1:], a.dtype) for a in parts]
    outs = pl.pallas_call(
        body, name="chip_exchange_start_" + tag,
        in_specs=[HBM] * (2 * n) + [ANY], out_specs=[SEM, SEM] + [HBM] * (2 * n) + [VMEM],
        out_shape=[pltpu.SemaphoreType.DMA((3 * n,)), pltpu.SemaphoreType.DMA((3 * n,))]
        + [pltpu.HBM(a.shape, a.dtype) for a in parts] + [pltpu.HBM(a.shape, a.dtype) for a in lands]
        + [_sds((8, LANES), F32)],
        input_output_aliases={i: 2 + i for i in range(2 * n)},
        compiler_params=pltpu.CompilerParams(has_side_effects=EFFECT),
    )(*[_in_hbm(a) for a in parts], *[_in_hbm(a) for a in lands], after)
    return outs[0], outs[1], list(outs[2:2 + n]), list(outs[2 + n:2 + 2 * n]), outs[-1]


def _chip_exchange_wait(ssem, rsem, parts, lands, after, tag):
    n = len(parts)

    def body(*refs):
        src, land = refs[:n], refs[n:2 * n]
        ssem_, rsem_ = refs[2 * n], refs[2 * n + 1]
        x, y, c = _mesh_pos()
        for i in range(n):
            for j, (px, py, qj) in enumerate(_other_chips(x, y)):
                cp = pltpu.make_async_remote_copy(src_ref=src[i].at[qj], dst_ref=land[i].at[j], send_sem=ssem_.at[3 * i + j],
                                                  recv_sem=rsem_.at[3 * i + j], device_id=(px, py, c), device_id_type=MESH)
                cp.wait_send()
                cp.wait_recv()

    outs = pl.pallas_call(
        body, name="chip_exchange_wait_" + tag,
        in_specs=[HBM] * (2 * n) + [SEM, SEM, ANY], out_specs=[HBM] * (2 * n),
        out_shape=[pltpu.HBM(a.shape, a.dtype) for a in parts] + [pltpu.HBM(a.shape, a.dtype) for a in lands],
        input_output_aliases={i: i for i in range(2 * n)},
        compiler_params=pltpu.CompilerParams(has_side_effects=EFFECT),
    )(*parts, *lands, ssem, rsem, after)
    return list(outs[:n]), list(outs[n:])


def _grad_half(ref, shape, axis, which):
    rows = shape[axis] // 2
    if axis == 0:
        return ref.at[_half_rows(which, rows), :]
    return ref.at[:, _half_rows(which, rows), :]


def _half_shape(a, axis):
    s = list(a.shape)
    s[axis] //= 2
    return tuple(s)


def _pair_exchange_start(grads, half_axis, after, tag):
    n = len(grads)

    def body(*refs):
        g, land = refs[:n], refs[n:2 * n]
        ssem, rsem = refs[2 * n + 1], refs[2 * n + 2]
        token = refs[-1]
        x, y, c = _mesh_pos()
        for i in range(n):
            pltpu.make_async_remote_copy(src_ref=_grad_half(g[i], grads[i].shape, half_axis[i], 1 - c), dst_ref=land[i],
                                         send_sem=ssem.at[i], recv_sem=rsem.at[i], device_id=(x, y, 1 - c),
                                         device_id_type=MESH).start()
        token[...] = jnp.zeros(token.shape, F32)

    lands = [lax.empty(_half_shape(a, half_axis[i]), a.dtype) for i, a in enumerate(grads)]
    outs = pl.pallas_call(
        body, name="pair_exchange_start_" + tag,
        in_specs=[HBM] * (2 * n) + [ANY], out_specs=[SEM, SEM] + [HBM] * (2 * n) + [VMEM],
        out_shape=[pltpu.SemaphoreType.DMA((n,)), pltpu.SemaphoreType.DMA((n,))]
        + [pltpu.HBM(a.shape, a.dtype) for a in grads] + [pltpu.HBM(a.shape, a.dtype) for a in lands]
        + [_sds((8, LANES), F32)],
        input_output_aliases={i: 2 + i for i in range(2 * n)},
        compiler_params=pltpu.CompilerParams(has_side_effects=EFFECT),
    )(*[_in_hbm(a) for a in grads], *[_in_hbm(a) for a in lands], after)
    return outs[0], outs[1], list(outs[2:2 + n]), list(outs[2 + n:2 + 2 * n]), outs[-1]


def _pair_exchange_wait(ssem, rsem, grads, lands, half_axis, after, tag):
    n = len(grads)

    def body(*refs):
        g, land = refs[:n], refs[n:2 * n]
        ssem_, rsem_ = refs[2 * n], refs[2 * n + 1]
        x, y, c = _mesh_pos()
        for i in range(n):
            cp = pltpu.make_async_remote_copy(src_ref=_grad_half(g[i], grads[i].shape, half_axis[i], 1 - c),
                                              dst_ref=land[i], send_sem=ssem_.at[i], recv_sem=rsem_.at[i],
                                              device_id=(x, y, 1 - c), device_id_type=MESH)
            cp.wait_send()
            cp.wait_recv()

    outs = pl.pallas_call(
        body, name="pair_exchange_wait_" + tag,
        in_specs=[HBM] * (2 * n) + [SEM, SEM, ANY], out_specs=[HBM] * (2 * n),
        out_shape=[pltpu.HBM(a.shape, a.dtype) for a in grads] + [pltpu.HBM(a.shape, a.dtype) for a in lands],
        input_output_aliases={i: i for i in range(2 * n)},
        compiler_params=pltpu.CompilerParams(has_side_effects=EFFECT),
    )(*grads, *lands, ssem, rsem, after)
    return list(outs[:n]), list(outs[n:])


def _pair_exchange_grads(grads, half_axis, tag):
    n = len(grads)

    def body(*refs):
        g, got = refs[:n], refs[n:2 * n]
        ssem, rsem = refs[2 * n:]
        x, y, c = _mesh_pos()
        _pair_barrier(x, y, c)
        cps = []
        for i in range(n):
            cp = pltpu.make_async_remote_copy(src_ref=_grad_half(g[i], grads[i].shape, half_axis[i], 1 - c),
                                              dst_ref=got[i], send_sem=ssem.at[i], recv_sem=rsem.at[i],
                                              device_id=(x, y, 1 - c), device_id_type=MESH)
            cp.start()
            cps.append(cp)
        for cp in cps:
            cp.wait()

    return pl.pallas_call(
        body, name="pair_exchange_grads_" + tag,
        in_specs=[ANY] * n, out_specs=[ANY] * n,
        out_shape=[_sds(_half_shape(a, half_axis[i]), a.dtype) for i, a in enumerate(grads)],
        scratch_shapes=[pltpu.SemaphoreType.DMA((n,)), pltpu.SemaphoreType.DMA((n,))],
        compiler_params=_pair_params(),
    )(*grads)


def _pair_sum(gs, gots, c_arr, col_sharded, tag):
    n = len(gs)
    if col_sharded:
        rows, cols = gs[0].shape
        rh, cs = rows // 2, cols // N_CHIPS
        g_spec = pl.BlockSpec((rh, cs), lambda k, c_ref: (c_ref[0], k))
        got_spec = pl.BlockSpec((rh, cs), lambda k, c_ref: (0, k))
    else:
        _, rows, cs = gs[0].shape
        rh = rows // 2
        g_spec = pl.BlockSpec((1, rh, cs), lambda k, c_ref: (k, c_ref[0], 0))
        got_spec = pl.BlockSpec((1, rh, cs), lambda k, c_ref: (k, 0, 0))

    def body(c_ref, *refs):
        for g_ref, got_ref, out_ref in zip(refs[:n], refs[n:2 * n], refs[2 * n:]):
            total = g_ref[...].astype(F32) + got_ref[...].astype(F32)
            out_ref[...] = total.astype(BF16).reshape(out_ref.shape)

    return list(pl.pallas_call(
        body, name="pair_sum_" + tag,
        grid_spec=pltpu.PrefetchScalarGridSpec(
            num_scalar_prefetch=1, grid=(N_CHIPS,), in_specs=[g_spec] * n + [got_spec] * n,
            out_specs=[pl.BlockSpec((1, rh, cs), lambda k, c_ref: (k, 0, 0))] * n),
        out_shape=[_sds((N_CHIPS, rh, cs), BF16)] * n,
        compiler_params=_params(("arbitrary",)),
    )(c_arr, *gs, *gots))


def _chip_sum(parts, gots, qc_arr, tag):
    n = len(parts)
    _, rh, cs = parts[0].shape
    steps = 2 if rh % 32 == 0 else 1
    rb = rh // steps

    def body(qc_ref, *refs):
        for part_ref, got_ref, out_ref in zip(refs[:n], refs[n:2 * n], refs[2 * n:]):
            total = part_ref[0].astype(F32)
            for j in range(3):
                total = total + got_ref[j].astype(F32)
            out_ref[...] = total

    return list(pl.pallas_call(
        body, name="chip_sum_" + tag,
        grid_spec=pltpu.PrefetchScalarGridSpec(
            num_scalar_prefetch=1, grid=(steps,),
            in_specs=[pl.BlockSpec((1, rb, cs), lambda i, qc: (qc[0], i, 0))] * n
            + [pl.BlockSpec((3, rb, cs), lambda i, qc: (0, i, 0))] * n,
            out_specs=[pl.BlockSpec((rb, cs), lambda i, qc: (qc[1] * steps + i, 0))] * n),
        out_shape=[_sds((2 * rh, cs), F32)] * n,
        compiler_params=_params(("arbitrary",)),
    )(qc_arr, *parts, *gots))


def _pair_share_grads(grads, tag):
    n = len(grads)

    def body(*refs):
        g = refs[n:2 * n]
        ssem, rsem = refs[2 * n:]
        x, y, c = _mesh_pos()
        _pair_barrier(x, y, c)
        cps = []
        for i in range(n):
            mine = g[i].at[_half_rows(c, grads[i].shape[0] // 2), :]
            cp = pltpu.make_async_remote_copy(src_ref=mine, dst_ref=mine, send_sem=ssem.at[i], recv_sem=rsem.at[i],
                                              device_id=(x, y, 1 - c), device_id_type=MESH)
            cp.start()
            cps.append(cp)
        for cp in cps:
            cp.wait()

    return pl.pallas_call(
        body, name="pair_share_grads_" + tag,
        in_specs=[ANY] * n, out_specs=[ANY] * n,
        out_shape=[_sds(a.shape, a.dtype) for a in grads],
        input_output_aliases={i: i for i in range(n)},
        scratch_shapes=[pltpu.SemaphoreType.DMA((n,)), pltpu.SemaphoreType.DMA((n,))],
        compiler_params=_pair_params(),
    )(*grads)


def _small_allreduce(parts, places, rows_total, width, after):
    n = len(parts)

    def body(*refs):
        ins, out_ref = refs[:n], refs[n + 1]
        pack, pair_got, chip_sum, got, ssem, rsem = refs[n + 2:]
        x, y, c = _mesh_pos()
        chip = 2 * x + y
        pack[...] = jnp.zeros(pack.shape, F32)
        for i in range(n):
            for row, col, src_row, rows in places[i]:
                w = parts[i].shape[1]
                pack[row:row + rows, col:col + w] = ins[i][src_row:src_row + rows, :]
        swap = pltpu.make_async_remote_copy(src_ref=pack, dst_ref=pair_got, send_sem=ssem.at[3], recv_sem=rsem.at[3],
                                            device_id=(x, y, 1 - c), device_id_type=MESH)
        swap.start()
        swap.wait()
        chip_sum[...] = pack[...] + pair_got[...]
        cps = []
        for j, (px, py, _) in enumerate(_other_chips(x, y)):
            cp = pltpu.make_async_remote_copy(src_ref=chip_sum, dst_ref=got.at[j], send_sem=ssem.at[j],
                                              recv_sem=rsem.at[j], device_id=(px, py, c), device_id_type=MESH)
            cp.start()
            cps.append(cp)
        for cp in cps:
            cp.wait()
        total = jnp.zeros(pack.shape, F32)
        for q in range(N_CHIPS):
            rel = jnp.bitwise_xor(chip, q)
            theirs = got[jnp.maximum(rel - 1, 0)]
            total = total + jnp.where(rel == 0, chip_sum[...], theirs)
        out_ref[...] = total

    return pl.pallas_call(
        body, name="small_allreduce",
        in_specs=[VMEM] * n + [ANY], out_specs=VMEM,
        out_shape=_sds((rows_total, width), F32),
        scratch_shapes=[pltpu.VMEM((rows_total, width), F32), pltpu.VMEM((rows_total, width), F32),
                        pltpu.VMEM((rows_total, width), F32), pltpu.VMEM((3, rows_total, width), F32),
                        pltpu.SemaphoreType.DMA((4,)), pltpu.SemaphoreType.DMA((4,))],
        compiler_params=_params(),
    )(*parts, after)


def _small_update(red, q_arr, takes, loss_at, ws, ms, vs):
    n_w = len(ws)

    def body(q_ref, red_ref, *refs):
        w_in, m_in, v_in = refs[0:n_w], refs[n_w:2 * n_w], refs[2 * n_w:3 * n_w]
        outs = refs[3 * n_w:]
        g_out, d_out, m_out, v_out = (outs[0:n_w], outs[n_w:2 * n_w], outs[2 * n_w:3 * n_w], outs[3 * n_w:4 * n_w])
        loss_ref = outs[4 * n_w]
        chip = q_ref[0]

        def take_own_columns(g_ref, d0, nr, s0, c0, w):
            for k in range(N_CHIPS):
                @pl.when(chip == k)
                def _():
                    g_ref[d0:d0 + nr, :] = red_ref[s0:s0 + nr, c0 + k * w:c0 + (k + 1) * w]

        for j in range(n_w):
            w = ws[j].shape[1]
            for d0, nr, s0, c0, sharded in takes[j]:
                if sharded:
                    take_own_columns(g_out[j], d0, nr, s0, c0, w)
                else:
                    g_out[j][d0:d0 + nr, :] = red_ref[s0:s0 + nr, c0:c0 + w]
            d_out[j][...], m_out[j][...], v_out[j][...] = _adamw_math(w_in[j][...], g_out[j][...], m_in[j][...], v_in[j][...])
        loss_ref[...] = red_ref[loss_at[0]:loss_at[0] + 1, loss_at[1]:loss_at[1] + LANES]

    shapes = [_sds(w.shape, F32) for w in ws]
    outs = pl.pallas_call(
        body, name="small_update",
        in_specs=[pl.BlockSpec(memory_space=pltpu.SMEM)] + [VMEM] * (1 + 3 * n_w), out_specs=[VMEM] * (4 * n_w + 1),
        out_shape=shapes * 4 + [_sds((1, LANES), F32)],
        compiler_params=_params(),
    )(q_arr, red, *ws, *ms, *vs)
    return outs[0:n_w], outs[n_w:2 * n_w], outs[2 * n_w:3 * n_w], outs[3 * n_w:4 * n_w], outs[4 * n_w]


def _adamw_math(w, g, m, v):
    m2 = ADAM_B1 * m + (1.0 - ADAM_B1) * g
    v2 = ADAM_B2 * v + (1.0 - ADAM_B2) * (g * g)
    m_hat = m2 * (1.0 / (1.0 - ADAM_B1 ** ADAM_STEP))
    v_hat = v2 * (1.0 / (1.0 - ADAM_B2 ** ADAM_STEP))
    delta = -ADAM_LR * (m_hat / (jnp.sqrt(v_hat) + ADAM_EPS) + ADAM_WD * w)
    return delta, m2, v2


ADAMW_BLOCK_BYTES = 3 * 2 ** 19


def _adamw_big(ws, gs, ms, vs, tag):
    n = len(ws)
    rows, cols = ws[0].shape
    tr = 16
    for t in range(16, rows + 1, 16):
        if rows % t == 0 and t * cols * 4 * n <= ADAMW_BLOCK_BYTES:
            tr = t

    def body(*refs):
        ins, outs = refs[:4 * n], refs[4 * n:]
        for i in range(n):
            w_ref, g_ref, m_ref, v_ref = ins[i], ins[n + i], ins[2 * n + i], ins[3 * n + i]
            gg = g_ref[...]
            outs[4 * i][...] = gg
            outs[4 * i + 1][...], outs[4 * i + 2][...], outs[4 * i + 3][...] = _adamw_math(
                w_ref[...], gg, m_ref[...], v_ref[...])

    spec = pl.BlockSpec((tr, cols), lambda i: (i, 0))
    outs = pl.pallas_call(
        body, name="adamw_" + tag, grid=(rows // tr,),
        in_specs=[spec] * (4 * n), out_specs=[spec] * (4 * n),
        out_shape=[_sds((rows, cols), F32)] * (4 * n),
        compiler_params=_params(("arbitrary",)),
    )(*ws, *gs, *ms, *vs)
    return [outs[4 * i:4 * i + 4] for i in range(n)]


SMALL_ROWS = 40
PACK_ROWS = 64


def kernel(x, meta_tokens, pre_mix_norm, w_in, conv_a_w, conv_b_w, conv_b_bias, ln_b_gain, ln_b_bias, w_out, post_mix_norm, pre_ffn_norm, w_gate, w_up, w_down, post_ffn_norm, loss_target, m_meta_tokens, m_pre_mix_norm, m_w_in, m_conv_a_w, m_conv_b_w, m_conv_b_bias, m_ln_b_gain, m_ln_b_bias, m_w_out, m_post_mix_norm, m_pre_ffn_norm, m_w_gate, m_w_up, m_w_down, m_post_ffn_norm, v_meta_tokens, v_pre_mix_norm, v_w_in, v_conv_a_w, v_conv_b_w, v_conv_b_bias, v_ln_b_gain, v_ln_b_bias, v_w_out, v_post_mix_norm, v_pre_ffn_norm, v_w_gate, v_w_up, v_w_down, v_post_ffn_norm):
    xq, yq, cq = lax.axis_index("x"), lax.axis_index("y"), lax.axis_index("c")
    chip = 2 * xq + yq
    c_arr = jnp.reshape(cq, (1,)).astype(jnp.int32)
    qc_arr = jnp.stack([chip, cq]).astype(jnp.int32)

    seq, d = x.shape[1], x.shape[2]
    x2, tgt2 = x[0], loss_target[0]
    tr = lambda a: jnp.swapaxes(a, 1, 2)[0]
    w_in2, w_out2, w_gate2, w_up2, w_down2 = w_in[0], w_out[0], tr(w_gate), tr(w_up), w_down[0]
    ka, wa_sh = conv_a_w.shape[1], conv_a_w.shape[2]
    kb = conv_b_w.shape[1]
    meta_sh = meta_tokens.shape[1]

    small = jnp.zeros((PACK_ROWS, meta_sh), F32)
    small = small.at[0:N_META, :].set(meta_tokens)
    small = small.at[16:16 + ka, 0:wa_sh].set(conv_a_w[0])
    small = small.at[24:24 + kb, 0:wa_sh].set(conv_b_w[0])
    q_arr = jnp.reshape(chip, (1,)).astype(jnp.int32)
    small_own = lax.dynamic_update_slice(jnp.zeros((N_CHIPS, PACK_ROWS, meta_sh), F32), small[None], (chip, 0, 0))
    i_ssem, i_rsem, first, i_token = _gather_start(_cast_place([w_in2], q_arr, "w_in") + [small_own], pre_mix_norm, "in")
    rest = _cast_place([w_out2], q_arr, "w_out", i_token) + _cast_place([w_gate2, w_up2, w_down2], q_arr, "ffn", i_token)
    g_ssem, g_rsem, lands, g_token = _gather_start(rest, i_token, "rest")
    win4, small4 = _forward_pair(_gather_wait([0, 1], i_ssem, i_rsem, first, g_token, "in"), "in")
    meta_f = jnp.concatenate([small4[k, 0:N_META, :] for k in range(N_CHIPS)], axis=1)
    wa_f = jnp.concatenate([small4[k, 16:16 + ka, 0:wa_sh] for k in range(N_CHIPS)], axis=1)
    wb_f = jnp.concatenate([small4[k, 24:24 + kb, 0:wa_sh] for k in range(N_CHIPS)], axis=1)

    tm = _row_tile(seq + TAIL_ROWS)
    tail = lax.dynamic_update_slice(jnp.zeros((tm, d), F32), meta_f, (seq % tm, 0))
    h, xn1, hp5 = _mm_in(x2, tail, win4, pre_mix_norm, g_token)
    ya, z = _mix_conv_fwd(hp5, wa_f, wb_f, conv_b_bias)
    (wout4,) = _forward_pair(_gather_wait([0], g_ssem, g_rsem, lands[0:1], z, "out"), "out")
    wout_f = wout4.reshape(N_CHIPS * wout4.shape[1], wout4.shape[2])
    yb, mix, h1, xn2 = _mm_out(ya, z, h, wout_f, ln_b_gain, ln_b_bias, post_mix_norm, pre_ffn_norm)
    wg4, wu4 = _forward_pair(_gather_wait([1, 2], g_ssem, g_rsem, lands[1:3], xn2, "gate_up"), "gate_up")
    stacked = lambda a: a.reshape(a.shape[0] * a.shape[1], a.shape[2])
    wg_f, wu_f = stacked(wg4), stacked(wu4)
    p_act, q_act, f_act = _ffn_up(xn2, wg_f, wu_f)
    (wd4,) = _forward_pair(_gather_wait([3], g_ssem, g_rsem, lands[3:4], f_act, "down"), "down")
    wd_f = stacked(wd4)
    dff, dh2, loss_blk, d_gpf = _ffn_down(f_act, wd_f, h1, tgt2, post_ffn_norm)

    da, du = _ffn_bwd_act(dff, wd_f, p_act, q_act)
    by_chip = lambda g: g.reshape(N_CHIPS, g.shape[0] // N_CHIPS, g.shape[1])
    g_down = by_chip(_grad_w_down(f_act, dff))
    g_gate, g_up = [by_chip(g) for g in _grad_w_gate_up(xn2, da, du)]
    ffn = [g_gate, g_up, g_down]
    p_ssem, p_rsem, ffn, p_lands, p_token = _pair_exchange_start(ffn, [1, 1, 1], dff, "ffn")
    dh1, dmix, d_g2, d_gpm = _ffn_bwd_in(da, du, wg_f, wu_f, h1, mix, dh2, pre_ffn_norm, post_mix_norm, p_token)
    ffn, got = _pair_exchange_wait(p_ssem, p_rsem, ffn, p_lands, [1, 1, 1], d_g2, "ffn")
    parts = _pair_sum(ffn, got, c_arr, False, "ffn")
    f_ssem, f_rsem, parts, f_lands, f_token = _chip_exchange_start(parts, dff, "ffn")
    g_out = _grad_w_out(ya, yb, dmix, f_token)
    dya, dz, d_lg, d_lb = _mix_bwd_out(dmix, wout_f, z, ln_b_gain, ln_b_bias, f_token)
    dhp5, d_wa, d_wb, d_bb = _mix_conv_bwd(hp5, dya, dz, wa_f, wb_f)
    g_in = _grad_w_in(xn1, dhp5)

    g_out4 = g_out.reshape(N_CHIPS, g_out.shape[0] // N_CHIPS, g_out.shape[1])
    mixw = [g_in, g_out4]
    got2 = _pair_exchange_grads(mixw, [0, 1], "mix")
    parts2 = _pair_sum(mixw[0:1], got2[0:1], c_arr, True, "in") + _pair_sum(mixw[1:2], got2[1:2], c_arr, False, "out")
    m_ssem, m_rsem, parts2, m_lands, m_token = _chip_exchange_start(parts2, dhp5, "mix")
    grad_x2, d_meta, d_g1 = _mix_bwd_in(dhp5, win4, h, dh1, pre_mix_norm, m_token)
    grad_x = grad_x2[None]

    parts, f_recv = _chip_exchange_wait(f_ssem, f_rsem, parts, f_lands, d_g1, "ffn")
    halves = _chip_sum(parts, f_recv, qc_arr, "ffn")
    gsum_ffn = _pair_share_grads(halves, "ffn")

    names_big = ["w_in", "w_out", "w_gate", "w_up", "w_down"]
    w_big = dict(zip(names_big, [w_in2, w_out2, w_gate2, w_up2, w_down2]))
    m_big = dict(zip(names_big, [m_w_in[0], m_w_out[0], tr(m_w_gate), tr(m_w_up), m_w_down[0]]))
    v_big = dict(zip(names_big, [v_w_in[0], v_w_out[0], tr(v_w_gate), tr(v_w_up), v_w_down[0]]))
    grads, deltas, new_m, new_v = {}, {}, {}, {}

    def update(names, gs, tag):
        res = _adamw_big([w_big[k] for k in names], gs, [m_big[k] for k in names], [v_big[k] for k in names], tag)
        for nm, outs in zip(names, res):
            if nm in ("w_gate", "w_up"):
                outs = [jnp.swapaxes(o[None], 1, 2) for o in outs]
            else:
                outs = [o[None] for o in outs]
            grads[nm], deltas[nm], new_m[nm], new_v[nm] = outs
        return res[-1][1]

    last = update(["w_gate", "w_up", "w_down"], list(gsum_ffn), "ffn")

    hw = d // 2
    assert d_wa.shape == (3, hw) and d_wb.shape == (31, hw) and d_bb.shape == (1, hw)
    small_parts = [d_meta, d_g1, d_gpm, d_g2, d_gpf, d_bb, d_lg, d_lb, loss_blk[0:1, :], d_wa, d_wb]
    places = [[(0, 0, 0, N_META)], [(16, 0, 0, 1)], [(17, 0, 0, 1)], [(18, 0, 0, 1)], [(19, 0, 0, 1)],
              [(20, 0, 0, 1)], [(20, hw, 0, 1)], [(21, 0, 0, 1)], [(21, hw, 0, 1)], [(22, 0, 0, 3)],
              [(22, hw, 0, 3), (25, 0, 3, 14), (25, hw, 17, 14)]]
    names_small = ["meta_tokens", "pre_mix_norm", "conv_a_w", "conv_b_w", "conv_b_bias", "ln_b_gain", "ln_b_bias",
                   "post_mix_norm", "pre_ffn_norm", "post_ffn_norm"]
    takes = [[(0, N_META, 0, 0, True)], [(0, 1, 16, 0, False)], [(0, 3, 22, 0, True)],
             [(0, 3, 22, hw, True), (3, 14, 25, 0, True), (17, 14, 25, hw, True)], [(0, 1, 20, 0, False)],
             [(0, 1, 20, hw, False)], [(0, 1, 21, 0, False)], [(0, 1, 17, 0, False)], [(0, 1, 18, 0, False)],
             [(0, 1, 19, 0, False)]]
    w_small = [meta_tokens, pre_mix_norm, conv_a_w[0], conv_b_w[0], conv_b_bias, ln_b_gain, ln_b_bias, post_mix_norm,
               pre_ffn_norm, post_ffn_norm]
    m_small = [m_meta_tokens, m_pre_mix_norm, m_conv_a_w[0], m_conv_b_w[0], m_conv_b_bias, m_ln_b_gain, m_ln_b_bias,
               m_post_mix_norm, m_pre_ffn_norm, m_post_ffn_norm]
    v_small = [v_meta_tokens, v_pre_mix_norm, v_conv_a_w[0], v_conv_b_w[0], v_conv_b_bias, v_ln_b_gain, v_ln_b_bias,
               v_post_mix_norm, v_pre_ffn_norm, v_post_ffn_norm]
    red = _small_allreduce(small_parts, places, SMALL_ROWS, d, last)
    g_s, d_s, m_s, v_s, loss_row = _small_update(red, q_arr, takes, (21, hw), w_small, m_small, v_small)
    loss = loss_row[0, 0]
    for i, nm in enumerate(names_small):
        lead = nm in ("conv_a_w", "conv_b_w")
        fix = (lambda a: a[None]) if lead else (lambda a: a)
        grads[nm], deltas[nm], new_m[nm], new_v[nm] = fix(g_s[i]), fix(d_s[i]), fix(m_s[i]), fix(v_s[i])

    parts2, m_recv = _chip_exchange_wait(m_ssem, m_rsem, parts2, m_lands, loss_row, "mix")
    halves2 = _chip_sum(parts2[0:1], m_recv[0:1], qc_arr, "in") + _chip_sum(parts2[1:2], m_recv[1:2], qc_arr, "out")
    gsum_mix = _pair_share_grads(halves2, "mix")
    update(["w_in"], [gsum_mix[0]], "w_in")
    update(["w_out"], [gsum_mix[1]], "w_out")

    order = ["meta_tokens", "pre_mix_norm", "w_in", "conv_a_w", "conv_b_w", "conv_b_bias", "ln_b_gain", "ln_b_bias", "w_out",
             "post_mix_norm", "pre_ffn_norm", "w_gate", "w_up", "w_down", "post_ffn_norm"]
    return (loss, grad_x, *[grads[k] for k in order], *[deltas[k] for k in order], *[new_m[k] for k in order],
            *[new_v[k] for k in order])
```

```python
import jax
import jax.numpy as jnp
from jax import lax
from jax.experimental import pallas as pl
from jax.experimental.pallas import tpu as pltpu

F32 = jnp.float32
BF16 = jnp.bfloat16
MESH = pl.DeviceIdType.MESH

N_META = 16
TAIL_ROWS = 128
RMS_EPS = 1e-6
LN_EPS = 1e-5
ADAM_LR = 0.001
ADAM_B1 = 0.9
ADAM_B2 = 0.999
ADAM_EPS = 1e-08
ADAM_WD = 0.01
ADAM_STEP = 10

N_CHIPS = 4
LANES = 128
MXU_TILE = 256
CONV_CHUNK = 48
CONV_HIST = 32
ROW_TILE_CAP = 640
VMEM_LIMIT = 56 * 1024 * 1024

NN = (((1,), (0,)), ((), ()))
NT = (((1,), (1,)), ((), ()))
TN = (((0,), (0,)), ((), ()))


def _dot(a, b, dims=NN):
    return lax.dot_general(a, b, dims, preferred_element_type=F32)


def _sig(v):
    return 1.0 / (1.0 + jnp.exp(-v))


def _mean(v):
    return jnp.mean(v, axis=-1, keepdims=True)


def _row_tile(rows):
    best = 16
    for t in range(16, min(rows, ROW_TILE_CAP) + 1, 16):
        if rows % t == 0:
            best = t
    assert rows % best == 0
    return best


def _row_parts(tm):
    if tm % 32:
        return [slice(0, tm)]
    return [slice(0, tm // 2), slice(tm // 2, tm)]


def _concat_shards(w_ref, wcat_ref):
    n_sh, _, csh = w_ref.shape

    @pl.when(pl.program_id(0) == 0)
    def _():
        for k in range(n_sh):
            wcat_ref[:, k * csh:(k + 1) * csh] = w_ref[k]


def _params(semantics=None):
    kw = dict(vmem_limit_bytes=VMEM_LIMIT)
    if semantics is not None:
        kw["dimension_semantics"] = semantics
    return pltpu.CompilerParams(**kw)


def _full(shape):
    nd = len(shape)
    return pl.BlockSpec(shape, lambda *_: (0,) * nd)


def _resident(shape):
    nd = len(shape)
    return pl.BlockSpec(shape, lambda *_: (0,) * nd, pipeline_mode=pl.Buffered(1))


def _sds(shape, dtype):
    return jax.ShapeDtypeStruct(shape, dtype)


ANY = pl.BlockSpec(memory_space=pl.ANY)
VMEM = pl.BlockSpec(memory_space=pltpu.VMEM)


def _mesh_pos():
    return lax.axis_index("x"), lax.axis_index("y"), lax.axis_index("c")


def _flip(v, bit):
    return 1 - v if bit else v


def _mm_in(x, tail, win4, g1, after):
    seq, d = x.shape
    tp = seq + TAIL_ROWS
    tm = _row_tile(tp)
    n_sh, _, csh = win4.shape
    pw = n_sh * csh // 5

    def body(x_ref, tail_ref, w_ref, g_ref, after_ref, h_ref, xn_ref, hp_ref, wcat_ref):
        _concat_shards(w_ref, wcat_ref)
        rows = pl.program_id(0) * tm + lax.broadcasted_iota(jnp.int32, (tm, 1), 0)
        hh = jnp.where(rows < seq, x_ref[...], tail_ref[...])
        h_ref[...] = hh
        r = lax.rsqrt(_mean(hh * hh) + RMS_EPS)
        xn = (hh * r * g_ref[...]).astype(BF16)
        xn_ref[...] = xn
        for p in range(5):
            hp_ref[p] = _dot(xn, wcat_ref[:, p * pw:(p + 1) * pw])

    row = pl.BlockSpec((tm, d), lambda i: (i, 0))
    return pl.pallas_call(
        body, name="mm_in", grid=(tp // tm,),
        in_specs=[row, _full(tail.shape), _resident(win4.shape), _full(g1.shape), ANY],
        out_specs=[row, row, pl.BlockSpec((5, tm, pw), lambda i: (0, i, 0))],
        out_shape=[_sds((tp, d), F32), _sds((tp, d), BF16), _sds((5, tp, pw), F32)],
        scratch_shapes=[pltpu.VMEM((d, n_sh * csh), BF16)],
        compiler_params=_params(("arbitrary",)),
    )(x, tail, win4, g1, after)


def _seq_rows(tp):
    seq = tp - TAIL_ROWS
    nseq = seq + N_META
    assert nseq % CONV_CHUNK == 0 and seq % 16 == 0
    return seq, nseq


SUBLANES = 8


def _conv_offsets(width, transpose):
    return [(width - 1 - k) if transpose else (CONV_HIST - (width - 1) + k) for k in range(width)]


def _shift_copies(src_ref, sh_ref, width, transpose):
    n = src_ref.shape[0] - SUBLANES
    for s in sorted({o % SUBLANES for o in _conv_offsets(width, transpose)} - {0}):
        sh_ref[s - 1, 0:n, :] = src_ref[s:s + n, :]


def _tap_rows(src_ref, sh_ref, base, off):
    start = pl.multiple_of(base + (off // SUBLANES) * SUBLANES, SUBLANES)
    if off % SUBLANES == 0:
        return src_ref[pl.ds(start, CONV_CHUNK), :]
    return sh_ref[off % SUBLANES - 1, pl.ds(start, CONV_CHUNK), :]


def _conv_taps(src_ref, sh_ref, w_ref, dst_ref, width, nseq, transpose):
    w = w_ref[...]
    offs = _conv_offsets(width, transpose)
    _shift_copies(src_ref, sh_ref, width, transpose)

    def step(n, carry):
        out0 = pl.multiple_of(CONV_HIST + n * CONV_CHUNK, SUBLANES)
        base = out0 if transpose else n * CONV_CHUNK
        acc = jnp.zeros((CONV_CHUNK, w.shape[1]), F32)
        for k, off in enumerate(offs):
            acc = acc + w[k:k + 1, :] * _tap_rows(src_ref, sh_ref, base, off)
        dst_ref[pl.ds(out0, CONV_CHUNK), :] = acc
        return carry

    lax.fori_loop(0, nseq // CONV_CHUNK, step, 0)


def _conv_wgrad(src_ref, sh_ref, dz_ref, acc_ref, width, nseq):
    acc_ref[...] = jnp.zeros(acc_ref.shape, F32)
    offs = _conv_offsets(width, False)

    def step(n, carry):
        dzc = dz_ref[pl.ds(pl.multiple_of(CONV_HIST + n * CONV_CHUNK, SUBLANES), CONV_CHUNK), :]
        for k, off in enumerate(offs):
            prod = dzc * _tap_rows(src_ref, sh_ref, n * CONV_CHUNK, off)
            part = prod[0:SUBLANES, :]
            for s in range(1, CONV_CHUNK // SUBLANES):
                part = part + prod[SUBLANES * s:SUBLANES * (s + 1), :]
            acc_ref[SUBLANES * k:SUBLANES * (k + 1), :] += part
        return carry

    lax.fori_loop(0, nseq // CONV_CHUNK, step, 0)


def _to_seq(buf_ref, x_part, meta_part, seq):
    buf_ref[CONV_HIST:CONV_HIST + N_META, :] = meta_part
    buf_ref[CONV_HIST + N_META:CONV_HIST + N_META + seq, :] = x_part


def _zero_ends(buf_ref, nseq):
    zeros = jnp.zeros((CONV_HIST, buf_ref.shape[1]), F32)
    buf_ref[0:CONV_HIST, :] = zeros
    buf_ref[CONV_HIST + nseq:CONV_HIST + nseq + CONV_HIST, :] = zeros


def _mix_conv_fwd(hp5, wa, wb, bb):
    _, tp, wgrp = hp5.shape
    seq, nseq = _seq_rows(tp)
    sb = nseq + 2 * CONV_HIST
    ka, kb = wa.shape[0], wb.shape[0]
    xs, ms = slice(0, seq), slice(seq, seq + N_META)
    ox, om = slice(CONV_HIST + N_META, CONV_HIST + nseq), slice(CONV_HIST, CONV_HIST + N_META)

    def body(hp_ref, wa_ref, wb_ref, bb_ref, ya_ref, z_ref, s_ref, o_ref, sh_ref):
        _zero_ends(s_ref, nseq)
        _to_seq(s_ref, hp_ref[1, xs, :] * hp_ref[2, xs, :], hp_ref[1, ms, :] * hp_ref[2, ms, :], seq)
        _conv_taps(s_ref, sh_ref, wa_ref, o_ref, ka, nseq, False)
        ya_ref[xs, :] = (hp_ref[0, xs, :] * o_ref[ox, :]).astype(BF16)
        ya_ref[ms, :] = (hp_ref[0, ms, :] * o_ref[om, :]).astype(BF16)
        ya_ref[seq + N_META:tp, :] = jnp.zeros((tp - seq - N_META, LANES), BF16)
        _to_seq(s_ref, hp_ref[3, xs, :] * _sig(hp_ref[4, xs, :]), hp_ref[3, ms, :] * _sig(hp_ref[4, ms, :]), seq)
        _conv_taps(s_ref, sh_ref, wb_ref, o_ref, kb, nseq, False)
        z_ref[xs, :] = o_ref[ox, :] + bb_ref[...]
        z_ref[ms, :] = o_ref[om, :] + bb_ref[...]
        z_ref[seq + N_META:tp, :] = jnp.zeros((tp - seq - N_META, LANES), F32)

    col = lambda j: (0, j)
    return pl.pallas_call(
        body, name="mix_conv_fwd", grid=(wgrp // LANES,),
        in_specs=[pl.BlockSpec((5, tp, LANES), lambda j: (0, 0, j)), pl.BlockSpec((ka, LANES), col),
                  pl.BlockSpec((kb, LANES), col), pl.BlockSpec((1, LANES), col)],
        out_specs=[pl.BlockSpec((tp, LANES), col), pl.BlockSpec((tp, LANES), col)],
        out_shape=[_sds((tp, wgrp), BF16), _sds((tp, wgrp), F32)],
        scratch_shapes=[pltpu.VMEM((sb, LANES), F32), pltpu.VMEM((sb, LANES), F32),
                        pltpu.VMEM((SUBLANES - 1, sb, LANES), F32)],
        compiler_params=_params(("arbitrary",)),
    )(hp5, wa, wb, bb)


def _layer_norm_parts(z, lg, lb):
    mu = _mean(z)
    zc = z - mu
    rl = lax.rsqrt(_mean(zc * zc) + LN_EPS)
    zh = zc * rl
    return rl, zh, zh * lg + lb


def _mm_out(ya, z, h, wout, lg, lb, gpm, g2):
    tp, d = h.shape
    wa_ = ya.shape[1]
    tm = _row_tile(tp)

    def body(ya_ref, z_ref, h_ref, w_ref, lg_ref, lb_ref, gpm_ref, g2_ref, yb_ref, mix_ref, h1_ref, xn2_ref):
        for rs in _row_parts(tm):
            _, _, l = _layer_norm_parts(z_ref[rs, :], lg_ref[...], lb_ref[...])
            yb = (l * _sig(l)).astype(BF16)
            yb_ref[rs, :] = yb
            mix = _dot(ya_ref[rs, :], w_ref[0:wa_, :]) + _dot(yb, w_ref[wa_:d, :])
            mix_ref[rs, :] = mix
            rm = lax.rsqrt(_mean(mix * mix) + RMS_EPS)
            h1 = h_ref[rs, :] + mix * rm * gpm_ref[...]
            h1_ref[rs, :] = h1
            r2 = lax.rsqrt(_mean(h1 * h1) + RMS_EPS)
            xn2_ref[rs, :] = (h1 * r2 * g2_ref[...]).astype(BF16)

    row = lambda i: (i, 0)
    return pl.pallas_call(
        body, name="mm_out", grid=(tp // tm,),
        in_specs=[pl.BlockSpec((tm, wa_), row), pl.BlockSpec((tm, wa_), row), pl.BlockSpec((tm, d), row),
                  _resident(wout.shape), _full(lg.shape), _full(lb.shape), _full(gpm.shape), _full(g2.shape)],
        out_specs=[pl.BlockSpec((tm, wa_), row), pl.BlockSpec((tm, d), row), pl.BlockSpec((tm, d), row),
                   pl.BlockSpec((tm, d), row)],
        out_shape=[_sds((tp, wa_), BF16), _sds((tp, d), F32), _sds((tp, d), F32), _sds((tp, d), BF16)],
        compiler_params=_params(("arbitrary",)),
    )(ya, z, h, wout, lg, lb, gpm, g2)


def _ffn_up(xn2, wg, wu):
    tp, d = xn2.shape
    ff_dim = wg.shape[0]
    tm = _row_tile(tp)
    assert ff_dim % MXU_TILE == 0

    def body(xn_ref, wg_ref, wu_ref, p_ref, q_ref, f_ref):
        xn = xn_ref[...]
        for lo in range(0, ff_dim, MXU_TILE):
            cols = slice(lo, lo + MXU_TILE)
            a = _dot(xn, wg_ref[cols, :], NT)
            u = _dot(xn, wu_ref[cols, :], NT)
            s = _sig(a)
            q = a * s
            p_ref[:, cols] = (u * (s + q * (1.0 - s))).astype(BF16)
            q_ref[:, cols] = q.astype(BF16)
            f_ref[:, cols] = (q * u).astype(BF16)

    ospec = pl.BlockSpec((tm, ff_dim), lambda i: (i, 0))
    return pl.pallas_call(
        body, name="ffn_up", grid=(tp // tm,),
        in_specs=[pl.BlockSpec((tm, d), lambda i: (i, 0)), _resident(wg.shape), _resident(wu.shape)],
        out_specs=[ospec, ospec, ospec],
        out_shape=[_sds((tp, ff_dim), BF16)] * 3,
        compiler_params=_params(("arbitrary",)),
    )(xn2, wg, wu)


def _ffn_down(f, wd, h1, tgt, gpf):
    tp, ff_dim = f.shape
    d = h1.shape[1]
    tm = _row_tile(tp)
    seq, _ = _seq_rows(tp)

    def body(f_ref, w_ref, h1_ref, t_ref, gpf_ref, dff_ref, dh2_ref, loss_ref, dgpf_ref):
        i = pl.program_id(0)
        gpf_ = gpf_ref[...]

        @pl.when(i == 0)
        def _():
            loss_ref[...] = jnp.zeros(loss_ref.shape, F32)
            dgpf_ref[...] = jnp.zeros(dgpf_ref.shape, F32)

        for rs in _row_parts(tm):
            ff = _dot(f_ref[rs, :], w_ref[...])
            rf = lax.rsqrt(_mean(ff * ff) + RMS_EPS)
            nf = ff * rf
            h2 = h1_ref[rs, :] + nf * gpf_
            rows = i * tm + rs.start + lax.broadcasted_iota(jnp.int32, (rs.stop - rs.start, 1), 0)
            err = jnp.where(rows < seq, h2 - t_ref[rs, :], 0.0)
            dh2 = err * (1.0 / d)
            dh2_ref[rs, :] = dh2
            dn = dh2 * gpf_
            dff_ref[rs, :] = (rf * (dn - nf * _mean(dn * nf))).astype(BF16)
            loss_ref[...] += (0.5 / d) * jnp.sum(err * err, axis=(0, 1), keepdims=True)
            dgpf_ref[...] += jnp.sum(dh2 * nf, axis=0, keepdims=True)

    row = lambda i: (i, 0)
    return pl.pallas_call(
        body, name="ffn_down", grid=(tp // tm,),
        in_specs=[pl.BlockSpec((tm, ff_dim), row), _resident(wd.shape), pl.BlockSpec((tm, d), row),
                  pl.BlockSpec((tm, d), row), _full(gpf.shape)],
        out_specs=[pl.BlockSpec((tm, d), row), pl.BlockSpec((tm, d), row), _full((8, LANES)), _full((1, d))],
        out_shape=[_sds((tp, d), BF16), _sds((tp, d), F32), _sds((8, LANES), F32), _sds((1, d), F32)],
        compiler_params=_params(("arbitrary",)),
    )(f, wd, h1, tgt, gpf)


def _ffn_bwd_act(dff, wd, p, q):
    tp, d = dff.shape
    ff_dim = wd.shape[0]
    tm = _row_tile(tp)

    def body(dff_ref, w_ref, p_ref, q_ref, da_ref, du_ref):
        dffv = dff_ref[...]
        for lo in range(0, ff_dim, MXU_TILE):
            cols = slice(lo, lo + MXU_TILE)
            df = _dot(dffv, w_ref[cols, :], NT).astype(BF16)
            da_ref[:, cols] = df * p_ref[:, cols]
            du_ref[:, cols] = df * q_ref[:, cols]

    aspec = pl.BlockSpec((tm, ff_dim), lambda i: (i, 0))
    return pl.pallas_call(
        body, name="ffn_bwd_act", grid=(tp // tm,),
        in_specs=[pl.BlockSpec((tm, d), lambda i: (i, 0)), _resident(wd.shape), aspec, aspec],
        out_specs=[aspec, aspec],
        out_shape=[_sds((tp, ff_dim), BF16)] * 2,
        compiler_params=_params(("arbitrary",)),
    )(dff, wd, p, q)


def _grad_blocks(ff_dim):
    rows = ff_dim // 2
    assert rows % LANES == 0
    return rows


def _grad_w_down(f, dff):
    tp, ff_dim = f.shape
    d = dff.shape[1]
    rows = _grad_blocks(ff_dim)

    def body(f_ref, dff_ref, g_ref):
        g_ref[...] = _dot(f_ref[...], dff_ref[...], TN).astype(BF16)

    return pl.pallas_call(
        body, name="grad_w_down", grid=(ff_dim // rows,),
        in_specs=[pl.BlockSpec((tp, rows), lambda k: (0, k)), _resident(dff.shape)],
        out_specs=pl.BlockSpec((rows, d), lambda k: (k, 0)),
        out_shape=_sds((ff_dim, d), BF16),
        compiler_params=_params(("arbitrary",)),
    )(f, dff)


def _grad_w_gate_up(xn2, da, du):
    tp, ff_dim = da.shape
    d = xn2.shape[1]
    rows = _grad_blocks(ff_dim)

    def body(xn_ref, da_ref, du_ref, gg_ref, gu_ref):
        xn = xn_ref[...]
        gg_ref[...] = _dot(da_ref[...], xn, TN).astype(BF16)
        gu_ref[...] = _dot(du_ref[...], xn, TN).astype(BF16)

    aspec = pl.BlockSpec((tp, rows), lambda k: (0, k))
    gspec = pl.BlockSpec((rows, d), lambda k: (k, 0))
    return pl.pallas_call(
        body, name="grad_w_gate_up", grid=(ff_dim // rows,),
        in_specs=[_resident(xn2.shape), aspec, aspec],
        out_specs=[gspec, gspec],
        out_shape=[_sds((ff_dim, d), BF16)] * 2,
        compiler_params=_params(("arbitrary",)),
    )(xn2, da, du)


def _rms_bwd(dy, x, r, g):
    n = x * r
    dn = dy * g
    return r * (dn - n * _mean(dn * n)), dy * n


def _ffn_bwd_in(da, du, wg, wu, h1, mix, dh2, g2, gpm, after):
    tp, ff_dim = da.shape
    d = h1.shape[1]
    tm = _row_tile(tp)

    def body(da_ref, du_ref, wg_ref, wu_ref, h1_ref, mix_ref, dh2_ref, g2_ref, gpm_ref, after_ref,
             dh1_ref, dmix_ref, dg2_ref, dgpm_ref):
        i = pl.program_id(0)

        @pl.when(i == 0)
        def _():
            dg2_ref[...] = jnp.zeros(dg2_ref.shape, F32)
            dgpm_ref[...] = jnp.zeros(dgpm_ref.shape, F32)

        for rs in _row_parts(tm):
            dxn = _dot(da_ref[rs, :], wg_ref[...]) + _dot(du_ref[rs, :], wu_ref[...])
            h1v = h1_ref[rs, :]
            r2 = lax.rsqrt(_mean(h1v * h1v) + RMS_EPS)
            dres, dg2_rows = _rms_bwd(dxn, h1v, r2, g2_ref[...])
            dh1 = dh2_ref[rs, :] + dres
            dh1_ref[rs, :] = dh1
            mixv = mix_ref[rs, :]
            rm = lax.rsqrt(_mean(mixv * mixv) + RMS_EPS)
            dmix, dgpm_rows = _rms_bwd(dh1, mixv, rm, gpm_ref[...])
            dmix_ref[rs, :] = dmix.astype(BF16)
            dg2_ref[...] += jnp.sum(dg2_rows, axis=0, keepdims=True)
            dgpm_ref[...] += jnp.sum(dgpm_rows, axis=0, keepdims=True)

    aspec = pl.BlockSpec((tm, ff_dim), lambda i: (i, 0))
    row = pl.BlockSpec((tm, d), lambda i: (i, 0))
    return pl.pallas_call(
        body, name="ffn_bwd_in", grid=(tp // tm,),
        in_specs=[aspec, aspec, _resident(wg.shape), _resident(wu.shape), row, row, row, _full(g2.shape), _full(gpm.shape),
                  ANY],
        out_specs=[row, row, _full((1, d)), _full((1, d))],
        out_shape=[_sds((tp, d), F32), _sds((tp, d), BF16), _sds((1, d), F32), _sds((1, d), F32)],
        compiler_params=_params(("arbitrary",)),
    )(da, du, wg, wu, h1, mix, dh2, g2, gpm, after)


def _grad_w_out(ya, yb, dmix, after):
    tp, wa_ = ya.shape
    d = dmix.shape[1]

    def body(ya_ref, yb_ref, dmix_ref, after_ref, g_ref):
        dm = dmix_ref[...]
        g_ref[0:wa_, :] = _dot(ya_ref[...], dm, TN).astype(BF16)
        g_ref[wa_:2 * wa_, :] = _dot(yb_ref[...], dm, TN).astype(BF16)

    return pl.pallas_call(
        body, name="grad_w_out", grid=(1,),
        in_specs=[_full(ya.shape), _full(yb.shape), _full(dmix.shape), ANY],
        out_specs=_full((2 * wa_, d)),
        out_shape=_sds((2 * wa_, d), BF16),
        compiler_params=_params(("arbitrary",)),
    )(ya, yb, dmix, after)


def _mix_bwd_out(dmix, wout, z, lg, lb, after):
    tp, d = dmix.shape
    wa_ = z.shape[1]
    tm = _row_tile(tp)

    def body(dmix_ref, w_ref, z_ref, lg_ref, lb_ref, after_ref, dya_ref, dz_ref, dlg_ref, dlb_ref):
        i = pl.program_id(0)
        lg_ = lg_ref[...]

        @pl.when(i == 0)
        def _():
            dlg_ref[...] = jnp.zeros(dlg_ref.shape, F32)
            dlb_ref[...] = jnp.zeros(dlb_ref.shape, F32)

        for rs in _row_parts(tm):
            dm = dmix_ref[rs, :]
            dya_ref[rs, :] = _dot(dm, w_ref[0:wa_, :], NT)
            dyb = _dot(dm, w_ref[wa_:d, :], NT)
            rl, zh, l = _layer_norm_parts(z_ref[rs, :], lg_, lb_ref[...])
            sl = _sig(l)
            dl = dyb * (sl * (1.0 + l * (1.0 - sl)))
            dzh = dl * lg_
            dz_ref[rs, :] = rl * (dzh - _mean(dzh) - zh * _mean(dzh * zh))
            dlg_ref[...] += jnp.sum(dl * zh, axis=0, keepdims=True)
            dlb_ref[...] += jnp.sum(dl, axis=0, keepdims=True)

    row = lambda i: (i, 0)
    return pl.pallas_call(
        body, name="mix_bwd_out", grid=(tp // tm,),
        in_specs=[pl.BlockSpec((tm, d), row), _resident(wout.shape), pl.BlockSpec((tm, wa_), row), _full(lg.shape),
                  _full(lb.shape), ANY],
        out_specs=[pl.BlockSpec((tm, wa_), row), pl.BlockSpec((tm, wa_), row), _full((1, wa_)), _full((1, wa_))],
        out_shape=[_sds((tp, wa_), F32), _sds((tp, wa_), F32), _sds((1, wa_), F32), _sds((1, wa_), F32)],
        compiler_params=_params(("arbitrary",)),
    )(dmix, wout, z, lg, lb, after)


def _mix_conv_bwd(hp5, dya, dz, wa, wb):
    _, tp, wgrp = hp5.shape
    seq, nseq = _seq_rows(tp)
    sb = nseq + 2 * CONV_HIST
    ka, kb = wa.shape[0], wb.shape[0]
    xs, ms = slice(0, seq), slice(seq, seq + N_META)
    ox, om = slice(CONV_HIST + N_META, CONV_HIST + nseq), slice(CONV_HIST, CONV_HIST + N_META)
    n_tail = tp - seq - N_META

    def body(hp_ref, dya_ref, dz_ref, wa_ref, wb_ref, dhp_ref, dwa_ref, dwb_ref, dbb_ref, s_ref, d_ref, o_ref, acc_ref,
             shs_ref, shd_ref):
        _zero_ends(s_ref, nseq)
        _zero_ends(d_ref, nseq)

        def put(p, ox_val, om_val):
            dhp_ref[p, xs, :] = ox_val.astype(BF16)
            dhp_ref[p, ms, :] = om_val.astype(BF16)
            dhp_ref[p, seq + N_META:tp, :] = jnp.zeros((n_tail, LANES), BF16)

        def wgrad(dw_ref, width):
            for k in range(width):
                dw_ref[k:k + 1, :] = jnp.sum(acc_ref[8 * k:8 * k + 8, :], axis=0, keepdims=True)

        _to_seq(s_ref, hp_ref[1, xs, :] * hp_ref[2, xs, :], hp_ref[1, ms, :] * hp_ref[2, ms, :], seq)
        _conv_taps(s_ref, shs_ref, wa_ref, o_ref, ka, nseq, False)
        put(0, dya_ref[xs, :] * o_ref[ox, :], dya_ref[ms, :] * o_ref[om, :])
        _to_seq(d_ref, dya_ref[xs, :] * hp_ref[0, xs, :], dya_ref[ms, :] * hp_ref[0, ms, :], seq)
        _conv_wgrad(s_ref, shs_ref, d_ref, acc_ref, ka, nseq)
        wgrad(dwa_ref, ka)
        _conv_taps(d_ref, shd_ref, wa_ref, o_ref, ka, nseq, True)
        put(1, o_ref[ox, :] * hp_ref[2, xs, :], o_ref[om, :] * hp_ref[2, ms, :])
        put(2, o_ref[ox, :] * hp_ref[1, xs, :], o_ref[om, :] * hp_ref[1, ms, :])

        _to_seq(s_ref, hp_ref[3, xs, :] * _sig(hp_ref[4, xs, :]), hp_ref[3, ms, :] * _sig(hp_ref[4, ms, :]), seq)
        _to_seq(d_ref, dz_ref[xs, :], dz_ref[ms, :], seq)
        dbb_ref[...] = (jnp.sum(dz_ref[xs, :], axis=0, keepdims=True)
                        + jnp.sum(dz_ref[ms, :], axis=0, keepdims=True))
        _shift_copies(s_ref, shs_ref, kb, False)
        _conv_wgrad(s_ref, shs_ref, d_ref, acc_ref, kb, nseq)
        wgrad(dwb_ref, kb)
        _conv_taps(d_ref, shd_ref, wb_ref, o_ref, kb, nseq, True)
        sx, sm = _sig(hp_ref[4, xs, :]), _sig(hp_ref[4, ms, :])
        put(3, o_ref[ox, :] * sx, o_ref[om, :] * sm)
        put(4, o_ref[ox, :] * hp_ref[3, xs, :] * sx * (1.0 - sx), o_ref[om, :] * hp_ref[3, ms, :] * sm * (1.0 - sm))

    col = lambda j: (0, j)
    blk5 = pl.BlockSpec((5, tp, LANES), lambda j: (0, 0, j))
    return pl.pallas_call(
        body, name="mix_conv_bwd", grid=(wgrp // LANES,),
        in_specs=[blk5, pl.BlockSpec((tp, LANES), col), pl.BlockSpec((tp, LANES), col),
                  pl.BlockSpec((ka, LANES), col), pl.BlockSpec((kb, LANES), col)],
        out_specs=[blk5, pl.BlockSpec((ka, LANES), col), pl.BlockSpec((kb, LANES), col), pl.BlockSpec((1, LANES), col)],
        out_shape=[_sds((5, tp, wgrp), BF16), _sds((ka, wgrp), F32), _sds((kb, wgrp), F32), _sds((1, wgrp), F32)],
        scratch_shapes=[pltpu.VMEM((sb, LANES), F32), pltpu.VMEM((sb, LANES), F32), pltpu.VMEM((sb, LANES), F32),
                        pltpu.VMEM((SUBLANES * kb, LANES), F32), pltpu.VMEM((SUBLANES - 1, sb, LANES), F32),
                        pltpu.VMEM((SUBLANES - 1, sb, LANES), F32)],
        compiler_params=_params(("arbitrary",)),
    )(hp5, dya, dz, wa, wb)


def _grad_w_in(xn1, dhp5):
    n_p, tp, pw = dhp5.shape
    d = xn1.shape[1]

    def body(xn_ref, dhp_ref, g_ref):
        g_ref[...] = _dot(xn_ref[...], dhp_ref[0], TN).astype(BF16)

    return pl.pallas_call(
        body, name="grad_w_in", grid=(n_p,),
        in_specs=[_resident(xn1.shape), pl.BlockSpec((1, tp, pw), lambda p: (p, 0, 0))],
        out_specs=pl.BlockSpec((d, pw), lambda p: (0, p)),
        out_shape=_sds((d, n_p * pw), BF16),
        compiler_params=_params(("arbitrary",)),
    )(xn1, dhp5)


def _mix_bwd_in(dhp5, win4, h, dh1, g1, after):
    n_p, tp, pw = dhp5.shape
    d = h.shape[1]
    n_sh, _, csh = win4.shape
    tm = _row_tile(tp)

    seq, _ = _seq_rows(tp)
    last, meta_off = seq // tm, seq % tm
    assert last == tp // tm - 1
    assert any(rs.start <= meta_off and meta_off + N_META <= rs.stop for rs in _row_parts(tm))

    def body(dhp_ref, w_ref, h_ref, dh1_ref, g_ref, after_ref, gx_ref, dmeta_ref, dg1_ref, wcat_ref):
        i = pl.program_id(0)
        _concat_shards(w_ref, wcat_ref)

        @pl.when(i == 0)
        def _():
            dg1_ref[...] = jnp.zeros(dg1_ref.shape, F32)

        for rs in _row_parts(tm):
            dxn = _dot(dhp_ref[0, rs, :], wcat_ref[:, 0:pw], NT)
            for p in range(1, n_p):
                dxn = dxn + _dot(dhp_ref[p, rs, :], wcat_ref[:, p * pw:(p + 1) * pw], NT)
            hh = h_ref[rs, :]
            r1 = lax.rsqrt(_mean(hh * hh) + RMS_EPS)
            dres, dg_rows = _rms_bwd(dxn, hh, r1, g_ref[...])
            dh = dh1_ref[rs, :] + dres
            gx_ref[rs, :] = dh
            dg1_ref[...] += jnp.sum(dg_rows, axis=0, keepdims=True)
            if rs.start <= meta_off and meta_off + N_META <= rs.stop:
                @pl.when(i == last)
                def _():
                    dmeta_ref[...] = dh[meta_off - rs.start:meta_off - rs.start + N_META, :]

    row = lambda i: (i, 0)
    return pl.pallas_call(
        body, name="mix_bwd_in", grid=(tp // tm,),
        in_specs=[pl.BlockSpec((n_p, tm, pw), lambda i: (0, i, 0)), _resident(win4.shape), pl.BlockSpec((tm, d), row),
                  pl.BlockSpec((tm, d), row), _full(g1.shape), ANY],
        out_specs=[pl.BlockSpec((tm, d), row), _full((N_META, d)), _full((1, d))],
        out_shape=[_sds((seq, d), F32), _sds((N_META, d), F32), _sds((1, d), F32)],
        scratch_shapes=[pltpu.VMEM((d, n_sh * csh), BF16)],
        compiler_params=_params(("arbitrary",)),
    )(dhp5, win4, h, dh1, g1, after)


def _other_chips(x, y):
    out = []
    for j in (1, 2, 3):
        px, py = _flip(x, j >> 1), _flip(y, j & 1)
        out.append((px, py, 2 * px + py))
    return out


PAIR_COLLECTIVE_ID = 0


def _pair_barrier(x, y, c):
    sem = pltpu.get_barrier_semaphore()
    pl.semaphore_signal(sem, inc=1, device_id=(x, y, 1 - c), device_id_type=MESH)
    pl.semaphore_wait(sem, 1)


def _pair_params():
    return pltpu.CompilerParams(collective_id=PAIR_COLLECTIVE_ID)


def _half_rows(c, rows_half):
    return pl.ds(pl.multiple_of(c * rows_half, 8), rows_half)


def _cast_place(ws, q_arr, tag, after=None):
    n = len(ws)
    rows, cols = ws[0].shape
    tr = _row_tile(rows)
    extra = [] if after is None else [after]

    def body(q_ref, *refs):
        for w_ref, out_ref in zip(refs[:n], refs[n + len(extra):]):
            out_ref[0] = w_ref[...].astype(BF16)

    return list(pl.pallas_call(
        body, name="cast_place_" + tag,
        grid_spec=pltpu.PrefetchScalarGridSpec(
            num_scalar_prefetch=1, grid=(rows // tr,),
            in_specs=[pl.BlockSpec((tr, cols), lambda i, q: (i, 0))] * n + [ANY] * len(extra),
            out_specs=[pl.BlockSpec((1, tr, cols), lambda i, q: (q[0], i, 0))] * n),
        out_shape=[_sds((N_CHIPS, rows, cols), BF16)] * n,
        compiler_params=_params(("arbitrary",)),
    )(q_arr, *ws, *extra))


HBM = pl.BlockSpec(memory_space=pltpu.HBM)
SEM = pl.BlockSpec(memory_space=pltpu.SEMAPHORE)
EFFECT = pltpu.SideEffectType.DATAFLOW_SIDE_EFFECTING


def _in_hbm(a):
    return pltpu.with_memory_space_constraint(a, pltpu.HBM)


def _gather_start(fulls, after, tag):
    n = len(fulls)
    halves = [a.shape[1] // 2 for a in fulls]

    def body(*refs):
        land = refs[:n]
        ssem, rsem = refs[n + 1], refs[n + 2]
        token = refs[-1]
        x, y, c = _mesh_pos()
        q = 2 * x + y
        for i in range(n):
            for j, (px, py, _) in enumerate(_other_chips(x, y)):
                mine = land[i].at[q, _half_rows(c, halves[i]), :]
                pltpu.make_async_remote_copy(src_ref=mine, dst_ref=mine, send_sem=ssem.at[3 * i + j],
                                             recv_sem=rsem.at[3 * i + j], device_id=(px, py, c), device_id_type=MESH).start()
        token[...] = jnp.zeros(token.shape, F32)

    outs = pl.pallas_call(
        body, name="gather_start_" + tag,
        in_specs=[HBM] * n + [ANY], out_specs=[SEM, SEM] + [HBM] * n + [VMEM],
        out_shape=[pltpu.SemaphoreType.DMA((3 * n,)), pltpu.SemaphoreType.DMA((3 * n,))]
        + [pltpu.HBM(a.shape, a.dtype) for a in fulls] + [_sds((8, LANES), F32)],
        input_output_aliases={i: 2 + i for i in range(n)},
        compiler_params=pltpu.CompilerParams(has_side_effects=EFFECT),
    )(*[_in_hbm(a) for a in fulls], after)
    return outs[0], outs[1], list(outs[2:2 + n]), outs[-1]


def _gather_wait(which, ssem, rsem, lands, after, tag):
    m = len(which)
    halves = [a.shape[1] // 2 for a in lands]

    def body(*refs):
        land = refs[:m]
        ssem_, rsem_ = refs[m], refs[m + 1]
        x, y, c = _mesh_pos()
        for t, i in enumerate(which):
            for j, (px, py, qj) in enumerate(_other_chips(x, y)):
                rows = _half_rows(c, halves[t])
                cp = pltpu.make_async_remote_copy(src_ref=land[t].at[2 * x + y, rows, :], dst_ref=land[t].at[qj, rows, :],
                                                  send_sem=ssem_.at[3 * i + j], recv_sem=rsem_.at[3 * i + j],
                                                  device_id=(px, py, c), device_id_type=MESH)
                cp.wait_send()
                cp.wait_recv()

    outs = pl.pallas_call(
        body, name="gather_wait_" + tag,
        in_specs=[HBM] * m + [SEM, SEM, ANY], out_specs=[HBM] * m,
        out_shape=[pltpu.HBM(a.shape, a.dtype) for a in lands],
        input_output_aliases={i: i for i in range(m)},
        compiler_params=pltpu.CompilerParams(has_side_effects=EFFECT),
    )(*lands, ssem, rsem, after)
    return list(outs)


def _forward_pair(lands, tag):
    n = len(lands)
    halves = [a.shape[1] // 2 for a in lands]

    def body(*refs):
        full = refs[n:2 * n]
        ssem, rsem = refs[2 * n:]
        x, y, c = _mesh_pos()
        _pair_barrier(x, y, c)
        cps = []
        for i in range(n):
            for j, (_, _, qj) in enumerate(_other_chips(x, y)):
                part = full[i].at[qj, _half_rows(c, halves[i]), :]
                cp = pltpu.make_async_remote_copy(src_ref=part, dst_ref=part, send_sem=ssem.at[3 * i + j],
                                                  recv_sem=rsem.at[3 * i + j], device_id=(x, y, 1 - c), device_id_type=MESH)
                cp.start()
                cps.append(cp)
        for cp in cps:
            cp.wait()

    return pl.pallas_call(
        body, name="forward_pair_" + tag,
        in_specs=[ANY] * n, out_specs=[ANY] * n,
        out_shape=[_sds(a.shape, a.dtype) for a in lands],
        input_output_aliases={i: i for i in range(n)},
        scratch_shapes=[pltpu.SemaphoreType.DMA((3 * n,)), pltpu.SemaphoreType.DMA((3 * n,))],
        compiler_params=_pair_params(),
    )(*lands)


def _chip_exchange_start(parts, after, tag):
    n = len(parts)

    def body(*refs):
        src, land = refs[:n], refs[n:2 * n]
        ssem, rsem = refs[2 * n + 1], refs[2 * n + 2]
        token = refs[-1]
        x, y, c = _mesh_pos()
        for i in range(n):
            for j, (px, py, qj) in enumerate(_other_chips(x, y)):
                pltpu.make_async_remote_copy(src_ref=src[i].at[qj], dst_ref=land[i].at[j], send_sem=ssem.at[3 * i + j],
                                             recv_sem=rsem.at[3 * i + j], device_id=(px, py, c), device_id_type=MESH).start()
        token[...] = jnp.zeros(token.shape, F32)

    lands = [lax.empty((3,) + a.shape[1:], a.dtype) for a in parts]
    outs = pl.pallas_call(
        body, name="chip_exchange_start_" + tag,
        in_specs=[HBM] * (2 * n) + [ANY], out_specs=[SEM, SEM] + [HBM] * (2 * n) + [VMEM],
        out_shape=[pltpu.SemaphoreType.DMA((3 * n,)), pltpu.SemaphoreType.DMA((3 * n,))]
        + [pltpu.HBM(a.shape, a.dtype) for a in parts] + [pltpu.HBM(a.shape, a.dtype) for a in lands]
        + [_sds((8, LANES), F32)],
        input_output_aliases={i: 2 + i for i in range(2 * n)},
        compiler_params=pltpu.CompilerParams(has_side_effects=EFFECT),
    )(*[_in_hbm(a) for a in parts], *[_in_hbm(a) for a in lands], after)
    return outs[0], outs[1], list(outs[2:2 + n]), list(outs[2 + n:2 + 2 * n]), outs[-1]


def _chip_exchange_wait(ssem, rsem, parts, lands, after, tag):
    n = len(parts)

    def body(*refs):
        src, land = refs[:n], refs[n:2 * n]
        ssem_, rsem_ = refs[2 * n], refs[2 * n + 1]
        x, y, c = _mesh_pos()
        for i in range(n):
            for j, (px, py, qj) in enumerate(_other_chips(x, y)):
                cp = pltpu.make_async_remote_copy(src_ref=src[i].at[qj], dst_ref=land[i].at[j], send_sem=ssem_.at[3 * i + j],
                                                  recv_sem=rsem_.at[3 * i + j], device_id=(px, py, c), device_id_type=MESH)
                cp.wait_send()
                cp.wait_recv()

    outs = pl.pallas_call(
        body, name="chip_exchange_wait_" + tag,
        in_specs=[HBM] * (2 * n) + [SEM, SEM, ANY], out_specs=[HBM] * (2 * n),
        out_shape=[pltpu.HBM(a.shape, a.dtype) for a in parts] + [pltpu.HBM(a.shape, a.dtype) for a in lands],
        input_output_aliases={i: i for i in range(2 * n)},
        compiler_params=pltpu.CompilerParams(has_side_effects=EFFECT),
    )(*parts, *lands, ssem, rsem, after)
    return list(outs[:n]), list(outs[n:])


def _grad_half(ref, shape, axis, which):
    rows = shape[axis] // 2
    if axis == 0:
        return ref.at[_half_rows(which, rows), :]
    return ref.at[:, _half_rows(which, rows), :]


def _half_shape(a, axis):
    s = list(a.shape)
    s[axis] //= 2
    return tuple(s)


def _pair_exchange_start(grads, half_axis, after, tag):
    n = len(grads)

    def body(*refs):
        g, land = refs[:n], refs[n:2 * n]
        ssem, rsem = refs[2 * n + 1], refs[2 * n + 2]
        token = refs[-1]
        x, y, c = _mesh_pos()
        for i in range(n):
            pltpu.make_async_remote_copy(src_ref=_grad_half(g[i], grads[i].shape, half_axis[i], 1 - c), dst_ref=land[i],
                                         send_sem=ssem.at[i], recv_sem=rsem.at[i], device_id=(x, y, 1 - c),
                                         device_id_type=MESH).start()
        token[...] = jnp.zeros(token.shape, F32)

    lands = [lax.empty(_half_shape(a, half_axis[i]), a.dtype) for i, a in enumerate(grads)]
    outs = pl.pallas_call(
        body, name="pair_exchange_start_" + tag,
        in_specs=[HBM] * (2 * n) + [ANY], out_specs=[SEM, SEM] + [HBM] * (2 * n) + [VMEM],
        out_shape=[pltpu.SemaphoreType.DMA((n,)), pltpu.SemaphoreType.DMA((n,))]
        + [pltpu.HBM(a.shape, a.dtype) for a in grads] + [pltpu.HBM(a.shape, a.dtype) for a in lands]
        + [_sds((8, LANES), F32)],
        input_output_aliases={i: 2 + i for i in range(2 * n)},
        compiler_params=pltpu.CompilerParams(has_side_effects=EFFECT),
    )(*[_in_hbm(a) for a in grads], *[_in_hbm(a) for a in lands], after)
    return outs[0], outs[1], list(outs[2:2 + n]), list(outs[2 + n:2 + 2 * n]), outs[-1]


def _pair_exchange_wait(ssem, rsem, grads, lands, half_axis, after, tag):
    n = len(grads)

    def body(*refs):
        g, land = refs[:n], refs[n:2 * n]
        ssem_, rsem_ = refs[2 * n], refs[2 * n + 1]
        x, y, c = _mesh_pos()
        for i in range(n):
            cp = pltpu.make_async_remote_copy(src_ref=_grad_half(g[i], grads[i].shape, half_axis[i], 1 - c),
                                              dst_ref=land[i], send_sem=ssem_.at[i], recv_sem=rsem_.at[i],
                                              device_id=(x, y, 1 - c), device_id_type=MESH)
            cp.wait_send()
            cp.wait_recv()

    outs = pl.pallas_call(
        body, name="pair_exchange_wait_" + tag,
        in_specs=[HBM] * (2 * n) + [SEM, SEM, ANY], out_specs=[HBM] * (2 * n),
        out_shape=[pltpu.HBM(a.shape, a.dtype) for a in grads] + [pltpu.HBM(a.shape, a.dtype) for a in lands],
        input_output_aliases={i: i for i in range(2 * n)},
        compiler_params=pltpu.CompilerParams(has_side_effects=EFFECT),
    )(*grads, *lands, ssem, rsem, after)
    return list(outs[:n]), list(outs[n:])


def _pair_exchange_grads(grads, half_axis, tag):
    n = len(grads)

    def body(*refs):
        g, got = refs[:n], refs[n:2 * n]
        ssem, rsem = refs[2 * n:]
        x, y, c = _mesh_pos()
        _pair_barrier(x, y, c)
        cps = []
        for i in range(n):
            cp = pltpu.make_async_remote_copy(src_ref=_grad_half(g[i], grads[i].shape, half_axis[i], 1 - c),
                                              dst_ref=got[i], send_sem=ssem.at[i], recv_sem=rsem.at[i],
                                              device_id=(x, y, 1 - c), device_id_type=MESH)
            cp.start()
            cps.append(cp)
        for cp in cps:
            cp.wait()

    return pl.pallas_call(
        body, name="pair_exchange_grads_" + tag,
        in_specs=[ANY] * n, out_specs=[ANY] * n,
        out_shape=[_sds(_half_shape(a, half_axis[i]), a.dtype) for i, a in enumerate(grads)],
        scratch_shapes=[pltpu.SemaphoreType.DMA((n,)), pltpu.SemaphoreType.DMA((n,))],
        compiler_params=_pair_params(),
    )(*grads)


def _pair_sum(gs, gots, c_arr, col_sharded, tag):
    n = len(gs)
    g_specs, got_specs, out_specs, out_shapes = [], [], [], []
    for g, by_cols in zip(gs, col_sharded):
        if by_cols:
            rows, cols = g.shape
            rh, cs = rows // 2, cols // N_CHIPS
            g_specs.append(pl.BlockSpec((rh, cs), lambda k, c_ref: (c_ref[0], k)))
            got_specs.append(pl.BlockSpec((rh, cs), lambda k, c_ref: (0, k)))
        else:
            _, rows, cs = g.shape
            rh = rows // 2
            g_specs.append(pl.BlockSpec((1, rh, cs), lambda k, c_ref: (k, c_ref[0], 0)))
            got_specs.append(pl.BlockSpec((1, rh, cs), lambda k, c_ref: (k, 0, 0)))
        out_specs.append(pl.BlockSpec((1, rh, cs), lambda k, c_ref: (k, 0, 0)))
        out_shapes.append(_sds((N_CHIPS, rh, cs), BF16))

    def body(c_ref, *refs):
        for g_ref, got_ref, out_ref in zip(refs[:n], refs[n:2 * n], refs[2 * n:]):
            total = g_ref[...].astype(F32) + got_ref[...].astype(F32)
            out_ref[...] = total.astype(BF16).reshape(out_ref.shape)

    return list(pl.pallas_call(
        body, name="pair_sum_" + tag,
        grid_spec=pltpu.PrefetchScalarGridSpec(
            num_scalar_prefetch=1, grid=(N_CHIPS,), in_specs=g_specs + got_specs, out_specs=out_specs),
        out_shape=out_shapes,
        compiler_params=_params(("arbitrary",)),
    )(c_arr, *gs, *gots))


def _chip_sum(parts, gots, qc_arr, tag):
    n = len(parts)
    steps = 2 if all(p.shape[1] % 32 == 0 for p in parts) else 1
    part_specs, got_specs, out_specs, out_shapes = [], [], [], []
    for p in parts:
        _, rh, cs = p.shape
        rb = rh // steps
        part_specs.append(pl.BlockSpec((1, rb, cs), lambda i, qc: (qc[0], i, 0)))
        got_specs.append(pl.BlockSpec((3, rb, cs), lambda i, qc: (0, i, 0)))
        out_specs.append(pl.BlockSpec((rb, cs), lambda i, qc: (qc[1] * steps + i, 0)))
        out_shapes.append(_sds((2 * rh, cs), F32))

    def body(qc_ref, *refs):
        for part_ref, got_ref, out_ref in zip(refs[:n], refs[n:2 * n], refs[2 * n:]):
            total = part_ref[0].astype(F32)
            for j in range(3):
                total = total + got_ref[j].astype(F32)
            out_ref[...] = total

    return list(pl.pallas_call(
        body, name="chip_sum_" + tag,
        grid_spec=pltpu.PrefetchScalarGridSpec(
            num_scalar_prefetch=1, grid=(steps,), in_specs=part_specs + got_specs, out_specs=out_specs),
        out_shape=out_shapes,
        compiler_params=_params(("arbitrary",)),
    )(qc_arr, *parts, *gots))


def _pair_share_grads(grads, tag):
    n = len(grads)

    def body(*refs):
        g = refs[n:2 * n]
        ssem, rsem = refs[2 * n:]
        x, y, c = _mesh_pos()
        _pair_barrier(x, y, c)
        cps = []
        for i in range(n):
            mine = g[i].at[_half_rows(c, grads[i].shape[0] // 2), :]
            cp = pltpu.make_async_remote_copy(src_ref=mine, dst_ref=mine, send_sem=ssem.at[i], recv_sem=rsem.at[i],
                                              device_id=(x, y, 1 - c), device_id_type=MESH)
            cp.start()
            cps.append(cp)
        for cp in cps:
            cp.wait()

    return pl.pallas_call(
        body, name="pair_share_grads_" + tag,
        in_specs=[ANY] * n, out_specs=[ANY] * n,
        out_shape=[_sds(a.shape, a.dtype) for a in grads],
        input_output_aliases={i: i for i in range(n)},
        scratch_shapes=[pltpu.SemaphoreType.DMA((n,)), pltpu.SemaphoreType.DMA((n,))],
        compiler_params=_pair_params(),
    )(*grads)


def _small_allreduce(parts, places, rows_total, width, after):
    n = len(parts)

    def body(*refs):
        ins, out_ref = refs[:n], refs[n + 1]
        pack, pair_got, chip_sum, got, ssem, rsem = refs[n + 2:]
        x, y, c = _mesh_pos()
        chip = 2 * x + y
        pack[...] = jnp.zeros(pack.shape, F32)
        for i in range(n):
            for row, col, src_row, rows in places[i]:
                w = parts[i].shape[1]
                pack[row:row + rows, col:col + w] = ins[i][src_row:src_row + rows, :]
        swap = pltpu.make_async_remote_copy(src_ref=pack, dst_ref=pair_got, send_sem=ssem.at[3], recv_sem=rsem.at[3],
                                            device_id=(x, y, 1 - c), device_id_type=MESH)
        swap.start()
        swap.wait()
        chip_sum[...] = pack[...] + pair_got[...]
        cps = []
        for j, (px, py, _) in enumerate(_other_chips(x, y)):
            cp = pltpu.make_async_remote_copy(src_ref=chip_sum, dst_ref=got.at[j], send_sem=ssem.at[j],
                                              recv_sem=rsem.at[j], device_id=(px, py, c), device_id_type=MESH)
            cp.start()
            cps.append(cp)
        for cp in cps:
            cp.wait()
        total = jnp.zeros(pack.shape, F32)
        for q in range(N_CHIPS):
            rel = jnp.bitwise_xor(chip, q)
            theirs = got[jnp.maximum(rel - 1, 0)]
            total = total + jnp.where(rel == 0, chip_sum[...], theirs)
        out_ref[...] = total

    return pl.pallas_call(
        body, name="small_allreduce",
        in_specs=[VMEM] * n + [ANY], out_specs=VMEM,
        out_shape=_sds((rows_total, width), F32),
        scratch_shapes=[pltpu.VMEM((rows_total, width), F32), pltpu.VMEM((rows_total, width), F32),
                        pltpu.VMEM((rows_total, width), F32), pltpu.VMEM((3, rows_total, width), F32),
                        pltpu.SemaphoreType.DMA((4,)), pltpu.SemaphoreType.DMA((4,))],
        compiler_params=_params(),
    )(*parts, after)


def _small_update(red, q_arr, takes, loss_at, ws, ms, vs):
    n_w = len(ws)

    def body(q_ref, red_ref, *refs):
        w_in, m_in, v_in = refs[0:n_w], refs[n_w:2 * n_w], refs[2 * n_w:3 * n_w]
        outs = refs[3 * n_w:]
        g_out, d_out, m_out, v_out = (outs[0:n_w], outs[n_w:2 * n_w], outs[2 * n_w:3 * n_w], outs[3 * n_w:4 * n_w])
        loss_ref = outs[4 * n_w]
        chip = q_ref[0]

        def take_own_columns(g_ref, d0, nr, s0, c0, w):
            for k in range(N_CHIPS):
                @pl.when(chip == k)
                def _():
                    g_ref[d0:d0 + nr, :] = red_ref[s0:s0 + nr, c0 + k * w:c0 + (k + 1) * w]

        for j in range(n_w):
            w = ws[j].shape[1]
            for d0, nr, s0, c0, sharded in takes[j]:
                if sharded:
                    take_own_columns(g_out[j], d0, nr, s0, c0, w)
                else:
                    g_out[j][d0:d0 + nr, :] = red_ref[s0:s0 + nr, c0:c0 + w]
            d_out[j][...], m_out[j][...], v_out[j][...] = _adamw_math(w_in[j][...], g_out[j][...], m_in[j][...], v_in[j][...])
        loss_ref[...] = red_ref[loss_at[0]:loss_at[0] + 1, loss_at[1]:loss_at[1] + LANES]

    shapes = [_sds(w.shape, F32) for w in ws]
    outs = pl.pallas_call(
        body, name="small_update",
        in_specs=[pl.BlockSpec(memory_space=pltpu.SMEM)] + [VMEM] * (1 + 3 * n_w), out_specs=[VMEM] * (4 * n_w + 1),
        out_shape=shapes * 4 + [_sds((1, LANES), F32)],
        compiler_params=_params(),
    )(q_arr, red, *ws, *ms, *vs)
    return outs[0:n_w], outs[n_w:2 * n_w], outs[2 * n_w:3 * n_w], outs[3 * n_w:4 * n_w], outs[4 * n_w]


def _adamw_math(w, g, m, v):
    m2 = ADAM_B1 * m + (1.0 - ADAM_B1) * g
    v2 = ADAM_B2 * v + (1.0 - ADAM_B2) * (g * g)
    m_hat = m2 / (1.0 - ADAM_B1 ** ADAM_STEP)
    v_hat = v2 / (1.0 - ADAM_B2 ** ADAM_STEP)
    delta = -ADAM_LR * (m_hat / (jnp.sqrt(v_hat) + ADAM_EPS) + ADAM_WD * w)
    return delta, m2, v2


ADAMW_BLOCK_BYTES = 3 * 2 ** 19


def _adamw_big(ws, gs, ms, vs, tag):
    n = len(ws)

    def fits(steps):
        return all(w.shape[0] % steps == 0 and (w.shape[0] // steps) % SUBLANES == 0
                   and (w.shape[0] // steps) * w.shape[1] * 4 * n <= ADAMW_BLOCK_BYTES for w in ws)

    steps = next(s for s in range(1, min(w.shape[0] for w in ws) + 1) if fits(s))
    specs = [pl.BlockSpec((w.shape[0] // steps, w.shape[1]), lambda i: (i, 0)) for w in ws]

    def body(*refs):
        ins, outs = refs[:4 * n], refs[4 * n:]
        for i in range(n):
            w_ref, g_ref, m_ref, v_ref = ins[i], ins[n + i], ins[2 * n + i], ins[3 * n + i]
            gg = g_ref[...]
            outs[4 * i][...] = gg
            outs[4 * i + 1][...], outs[4 * i + 2][...], outs[4 * i + 3][...] = _adamw_math(
                w_ref[...], gg, m_ref[...], v_ref[...])

    outs = pl.pallas_call(
        body, name="adamw_" + tag, grid=(steps,),
        in_specs=specs * 4, out_specs=[s for s in specs for _ in range(4)],
        out_shape=[_sds(w.shape, F32) for w in ws for _ in range(4)],
        compiler_params=_params(("arbitrary",)),
    )(*ws, *gs, *ms, *vs)
    return [outs[4 * i:4 * i + 4] for i in range(n)]


SMALL_ROWS = 40
PACK_ROWS = 64


def kernel(x, meta_tokens, pre_mix_norm, w_in, conv_a_w, conv_b_w, conv_b_bias, ln_b_gain, ln_b_bias, w_out, post_mix_norm, pre_ffn_norm, w_gate, w_up, w_down, post_ffn_norm, loss_target, m_meta_tokens, m_pre_mix_norm, m_w_in, m_conv_a_w, m_conv_b_w, m_conv_b_bias, m_ln_b_gain, m_ln_b_bias, m_w_out, m_post_mix_norm, m_pre_ffn_norm, m_w_gate, m_w_up, m_w_down, m_post_ffn_norm, v_meta_tokens, v_pre_mix_norm, v_w_in, v_conv_a_w, v_conv_b_w, v_conv_b_bias, v_ln_b_gain, v_ln_b_bias, v_w_out, v_post_mix_norm, v_pre_ffn_norm, v_w_gate, v_w_up, v_w_down, v_post_ffn_norm):
    xq, yq, cq = lax.axis_index("x"), lax.axis_index("y"), lax.axis_index("c")
    chip = 2 * xq + yq
    c_arr = jnp.reshape(cq, (1,)).astype(jnp.int32)
    qc_arr = jnp.stack([chip, cq]).astype(jnp.int32)

    seq, d = x.shape[1], x.shape[2]
    x2, tgt2 = x[0], loss_target[0]
    tr = lambda a: jnp.swapaxes(a, 1, 2)[0]
    w_in2, w_out2, w_gate2, w_up2, w_down2 = w_in[0], w_out[0], tr(w_gate), tr(w_up), w_down[0]
    ka, wa_sh = conv_a_w.shape[1], conv_a_w.shape[2]
    kb = conv_b_w.shape[1]
    meta_sh = meta_tokens.shape[1]

    small = jnp.zeros((PACK_ROWS, meta_sh), F32)
    small = small.at[0:N_META, :].set(meta_tokens)
    small = small.at[16:16 + ka, 0:wa_sh].set(conv_a_w[0])
    small = small.at[24:24 + kb, 0:wa_sh].set(conv_b_w[0])
    q_arr = jnp.reshape(chip, (1,)).astype(jnp.int32)
    small_own = lax.dynamic_update_slice(jnp.zeros((N_CHIPS, PACK_ROWS, meta_sh), F32), small[None], (chip, 0, 0))
    i_ssem, i_rsem, first, i_token = _gather_start(_cast_place([w_in2], q_arr, "w_in") + [small_own], pre_mix_norm, "in")
    rest = _cast_place([w_out2], q_arr, "w_out", i_token) + _cast_place([w_gate2, w_up2, w_down2], q_arr, "ffn", i_token)
    g_ssem, g_rsem, lands, g_token = _gather_start(rest, i_token, "rest")
    win4, small4 = _forward_pair(_gather_wait([0, 1], i_ssem, i_rsem, first, g_token, "in"), "in")
    meta_f = jnp.concatenate([small4[k, 0:N_META, :] for k in range(N_CHIPS)], axis=1)
    wa_f = jnp.concatenate([small4[k, 16:16 + ka, 0:wa_sh] for k in range(N_CHIPS)], axis=1)
    wb_f = jnp.concatenate([small4[k, 24:24 + kb, 0:wa_sh] for k in range(N_CHIPS)], axis=1)

    tm = _row_tile(seq + TAIL_ROWS)
    tail = lax.dynamic_update_slice(jnp.zeros((tm, d), F32), meta_f, (seq % tm, 0))
    h, xn1, hp5 = _mm_in(x2, tail, win4, pre_mix_norm, g_token)
    ya, z = _mix_conv_fwd(hp5, wa_f, wb_f, conv_b_bias)
    (wout4,) = _forward_pair(_gather_wait([0], g_ssem, g_rsem, lands[0:1], z, "out"), "out")
    wout_f = wout4.reshape(N_CHIPS * wout4.shape[1], wout4.shape[2])
    yb, mix, h1, xn2 = _mm_out(ya, z, h, wout_f, ln_b_gain, ln_b_bias, post_mix_norm, pre_ffn_norm)
    wg4, wu4 = _forward_pair(_gather_wait([1, 2], g_ssem, g_rsem, lands[1:3], xn2, "gate_up"), "gate_up")
    stacked = lambda a: a.reshape(a.shape[0] * a.shape[1], a.shape[2])
    wg_f, wu_f = stacked(wg4), stacked(wu4)
    p_act, q_act, f_act = _ffn_up(xn2, wg_f, wu_f)
    (wd4,) = _forward_pair(_gather_wait([3], g_ssem, g_rsem, lands[3:4], f_act, "down"), "down")
    wd_f = stacked(wd4)
    dff, dh2, loss_blk, d_gpf = _ffn_down(f_act, wd_f, h1, tgt2, post_ffn_norm)

    da, du = _ffn_bwd_act(dff, wd_f, p_act, q_act)
    by_chip = lambda g: g.reshape(N_CHIPS, g.shape[0] // N_CHIPS, g.shape[1])
    g_down = by_chip(_grad_w_down(f_act, dff))
    g_gate, g_up = [by_chip(g) for g in _grad_w_gate_up(xn2, da, du)]
    ffn = [g_gate, g_up, g_down]
    p_ssem, p_rsem, ffn, p_lands, p_token = _pair_exchange_start(ffn, [1, 1, 1], dff, "ffn")
    dh1, dmix, d_g2, d_gpm = _ffn_bwd_in(da, du, wg_f, wu_f, h1, mix, dh2, pre_ffn_norm, post_mix_norm, p_token)
    ffn, got = _pair_exchange_wait(p_ssem, p_rsem, ffn, p_lands, [1, 1, 1], d_g2, "ffn")
    parts = _pair_sum(ffn, got, c_arr, [False] * 3, "ffn")
    f_ssem, f_rsem, parts, f_lands, f_token = _chip_exchange_start(parts, dff, "ffn")
    g_out = _grad_w_out(ya, yb, dmix, f_token)
    dya, dz, d_lg, d_lb = _mix_bwd_out(dmix, wout_f, z, ln_b_gain, ln_b_bias, f_token)
    dhp5, d_wa, d_wb, d_bb = _mix_conv_bwd(hp5, dya, dz, wa_f, wb_f)
    g_in = _grad_w_in(xn1, dhp5)

    g_out4 = g_out.reshape(N_CHIPS, g_out.shape[0] // N_CHIPS, g_out.shape[1])
    mixw = [g_in, g_out4]
    got2 = _pair_exchange_grads(mixw, [0, 1], "mix")
    parts2 = _pair_sum(mixw, got2, c_arr, [True, False], "mix")
    m_ssem, m_rsem, parts2, m_lands, m_token = _chip_exchange_start(parts2, dhp5, "mix")
    grad_x2, d_meta, d_g1 = _mix_bwd_in(dhp5, win4, h, dh1, pre_mix_norm, m_token)
    grad_x = grad_x2[None]

    parts, f_recv = _chip_exchange_wait(f_ssem, f_rsem, parts, f_lands, d_g1, "ffn")
    halves = _chip_sum(parts, f_recv, qc_arr, "ffn")
    gsum_ffn = _pair_share_grads(halves, "ffn")

    names_big = ["w_in", "w_out", "w_gate", "w_up", "w_down"]
    w_big = dict(zip(names_big, [w_in2, w_out2, w_gate2, w_up2, w_down2]))
    m_big = dict(zip(names_big, [m_w_in[0], m_w_out[0], tr(m_w_gate), tr(m_w_up), m_w_down[0]]))
    v_big = dict(zip(names_big, [v_w_in[0], v_w_out[0], tr(v_w_gate), tr(v_w_up), v_w_down[0]]))
    grads, deltas, new_m, new_v = {}, {}, {}, {}

    def update(names, gs, tag):
        res = _adamw_big([w_big[k] for k in names], gs, [m_big[k] for k in names], [v_big[k] for k in names], tag)
        for nm, outs in zip(names, res):
            if nm in ("w_gate", "w_up"):
                outs = [jnp.swapaxes(o[None], 1, 2) for o in outs]
            else:
                outs = [o[None] for o in outs]
            grads[nm], deltas[nm], new_m[nm], new_v[nm] = outs
        return res[-1][1]

    last = update(["w_gate", "w_up", "w_down"], list(gsum_ffn), "ffn")

    hw = d // 2
    assert d_wa.shape == (3, hw) and d_wb.shape == (31, hw) and d_bb.shape == (1, hw)
    small_parts = [d_meta, d_g1, d_gpm, d_g2, d_gpf, d_bb, d_lg, d_lb, loss_blk[0:1, :], d_wa, d_wb]
    places = [[(0, 0, 0, N_META)], [(16, 0, 0, 1)], [(17, 0, 0, 1)], [(18, 0, 0, 1)], [(19, 0, 0, 1)],
              [(20, 0, 0, 1)], [(20, hw, 0, 1)], [(21, 0, 0, 1)], [(21, hw, 0, 1)], [(22, 0, 0, 3)],
              [(22, hw, 0, 3), (25, 0, 3, 14), (25, hw, 17, 14)]]
    names_small = ["meta_tokens", "pre_mix_norm", "conv_a_w", "conv_b_w", "conv_b_bias", "ln_b_gain", "ln_b_bias",
                   "post_mix_norm", "pre_ffn_norm", "post_ffn_norm"]
    takes = [[(0, N_META, 0, 0, True)], [(0, 1, 16, 0, False)], [(0, 3, 22, 0, True)],
             [(0, 3, 22, hw, True), (3, 14, 25, 0, True), (17, 14, 25, hw, True)], [(0, 1, 20, 0, False)],
             [(0, 1, 20, hw, False)], [(0, 1, 21, 0, False)], [(0, 1, 17, 0, False)], [(0, 1, 18, 0, False)],
             [(0, 1, 19, 0, False)]]
    w_small = [meta_tokens, pre_mix_norm, conv_a_w[0], conv_b_w[0], conv_b_bias, ln_b_gain, ln_b_bias, post_mix_norm,
               pre_ffn_norm, post_ffn_norm]
    m_small = [m_meta_tokens, m_pre_mix_norm, m_conv_a_w[0], m_conv_b_w[0], m_conv_b_bias, m_ln_b_gain, m_ln_b_bias,
               m_post_mix_norm, m_pre_ffn_norm, m_post_ffn_norm]
    v_small = [v_meta_tokens, v_pre_mix_norm, v_conv_a_w[0], v_conv_b_w[0], v_conv_b_bias, v_ln_b_gain, v_ln_b_bias,
               v_post_mix_norm, v_pre_ffn_norm, v_post_ffn_norm]
    red = _small_allreduce(small_parts, places, SMALL_ROWS, d, last)
    g_s, d_s, m_s, v_s, loss_row = _small_update(red, q_arr, takes, (21, hw), w_small, m_small, v_small)
    loss = loss_row[0, 0]
    for i, nm in enumerate(names_small):
        lead = nm in ("conv_a_w", "conv_b_w")
        fix = (lambda a: a[None]) if lead else (lambda a: a)
        grads[nm], deltas[nm], new_m[nm], new_v[nm] = fix(g_s[i]), fix(d_s[i]), fix(m_s[i]), fix(v_s[i])

    parts2, m_recv = _chip_exchange_wait(m_ssem, m_rsem, parts2, m_lands, loss_row, "mix")
    halves2 = _chip_sum(parts2, m_recv, qc_arr, "mix")
    gsum_mix = _pair_share_grads(halves2, "mix")
    update(["w_in", "w_out"], list(gsum_mix), "mix")

    order = ["meta_tokens", "pre_mix_norm", "w_in", "conv_a_w", "conv_b_w", "conv_b_bias", "ln_b_gain", "ln_b_bias", "w_out",
             "post_mix_norm", "pre_ffn_norm", "w_gate", "w_up", "w_down", "post_ffn_norm"]
    return (loss, grad_x, *[grads[k] for k in order], *[deltas[k] for k in order], *[new_m[k] for k in order],
            *[new_v[k] for k in order])
```

```python
import jax
import jax.numpy as jnp
from jax import lax
from jax.experimental import pallas as pl
from jax.experimental.pallas import tpu as pltpu

F32 = jnp.float32
BF16 = jnp.bfloat16
MESH = pl.DeviceIdType.MESH

N_META = 16
TAIL_ROWS = 128
RMS_EPS = 1e-6
LN_EPS = 1e-5
ADAM_LR = 0.001
ADAM_B1 = 0.9
ADAM_B2 = 0.999
ADAM_EPS = 1e-08
ADAM_WD = 0.01
ADAM_STEP = 10

N_CHIPS = 4
LANES = 128
SUBLANES = 8
BF16_ROWS = 16
MXU_TILE = 256
CONV_CHUNK = 48
CONV_HIST = 32
ROW_TILE_CAP = 640
VMEM_LIMIT = 56 * 1024 * 1024

NN = (((1,), (0,)), ((), ()))
NT = (((1,), (1,)), ((), ()))
TN = (((0,), (0,)), ((), ()))


def _dot(a, b, dims=NN):
    return lax.dot_general(a, b, dims, preferred_element_type=F32)


def _sig(v):
    return 1.0 / (1.0 + jnp.exp(-v))


def _mean(v):
    return jnp.mean(v, axis=-1, keepdims=True)


def _row_tile(rows):
    best = BF16_ROWS
    for t in range(BF16_ROWS, min(rows, ROW_TILE_CAP) + 1, BF16_ROWS):
        if rows % t == 0:
            best = t
    assert rows % best == 0
    return best


def _row_parts(tm):
    if tm % (2 * BF16_ROWS):
        return [slice(0, tm)]
    return [slice(0, tm // 2), slice(tm // 2, tm)]


def _concat_shards(w_ref, wcat_ref):
    n_sh, _, csh = w_ref.shape

    @pl.when(pl.program_id(0) == 0)
    def _():
        for k in range(n_sh):
            wcat_ref[:, k * csh:(k + 1) * csh] = w_ref[k]


def _params(semantics=None):
    kw = dict(vmem_limit_bytes=VMEM_LIMIT)
    if semantics is not None:
        kw["dimension_semantics"] = semantics
    return pltpu.CompilerParams(**kw)


def _full(shape):
    nd = len(shape)
    return pl.BlockSpec(shape, lambda *_: (0,) * nd)


def _resident(shape):
    nd = len(shape)
    return pl.BlockSpec(shape, lambda *_: (0,) * nd, pipeline_mode=pl.Buffered(1))


def _sds(shape, dtype):
    return jax.ShapeDtypeStruct(shape, dtype)


ANY = pl.BlockSpec(memory_space=pl.ANY)
VMEM = pl.BlockSpec(memory_space=pltpu.VMEM)


def _mesh_pos():
    return lax.axis_index("x"), lax.axis_index("y"), lax.axis_index("c")


def _flip(v, bit):
    return 1 - v if bit else v


def _mm_in(x, tail, win4, g1, after):
    seq, d = x.shape
    tp = seq + TAIL_ROWS
    tm = _row_tile(tp)
    n_sh, _, csh = win4.shape
    pw = n_sh * csh // 5

    def body(x_ref, tail_ref, w_ref, g_ref, after_ref, h_ref, xn_ref, hp_ref, wcat_ref):
        _concat_shards(w_ref, wcat_ref)
        rows = pl.program_id(0) * tm + lax.broadcasted_iota(jnp.int32, (tm, 1), 0)
        hh = jnp.where(rows < seq, x_ref[...], tail_ref[...])
        h_ref[...] = hh
        r = lax.rsqrt(_mean(hh * hh) + RMS_EPS)
        xn = (hh * r * g_ref[...]).astype(BF16)
        xn_ref[...] = xn
        for p in range(5):
            hp_ref[p] = _dot(xn, wcat_ref[:, p * pw:(p + 1) * pw])

    row = pl.BlockSpec((tm, d), lambda i: (i, 0))
    return pl.pallas_call(
        body, name="mm_in", grid=(tp // tm,),
        in_specs=[row, _full(tail.shape), _resident(win4.shape), _full(g1.shape), ANY],
        out_specs=[row, row, pl.BlockSpec((5, tm, pw), lambda i: (0, i, 0))],
        out_shape=[_sds((tp, d), F32), _sds((tp, d), BF16), _sds((5, tp, pw), F32)],
        scratch_shapes=[pltpu.VMEM((d, n_sh * csh), BF16)],
        compiler_params=_params(("arbitrary",)),
    )(x, tail, win4, g1, after)


def _seq_rows(tp):
    seq = tp - TAIL_ROWS
    nseq = seq + N_META
    assert nseq % CONV_CHUNK == 0 and seq % BF16_ROWS == 0
    return seq, nseq


def _conv_offsets(width, transpose):
    return [(width - 1 - k) if transpose else (CONV_HIST - (width - 1) + k) for k in range(width)]


def _shift_copies(src_ref, sh_ref, width, transpose):
    n = src_ref.shape[0] - SUBLANES
    for s in sorted({o % SUBLANES for o in _conv_offsets(width, transpose)} - {0}):
        sh_ref[s - 1, 0:n, :] = src_ref[s:s + n, :]


def _tap_rows(src_ref, sh_ref, base, off):
    start = pl.multiple_of(base + (off // SUBLANES) * SUBLANES, SUBLANES)
    if off % SUBLANES == 0:
        return src_ref[pl.ds(start, CONV_CHUNK), :]
    return sh_ref[off % SUBLANES - 1, pl.ds(start, CONV_CHUNK), :]


def _conv_taps(src_ref, sh_ref, w_ref, dst_ref, width, nseq, transpose):
    w = w_ref[...]
    offs = _conv_offsets(width, transpose)
    _shift_copies(src_ref, sh_ref, width, transpose)

    def step(n, carry):
        out0 = pl.multiple_of(CONV_HIST + n * CONV_CHUNK, SUBLANES)
        base = out0 if transpose else n * CONV_CHUNK
        acc = jnp.zeros((CONV_CHUNK, w.shape[1]), F32)
        for k, off in enumerate(offs):
            acc = acc + w[k:k + 1, :] * _tap_rows(src_ref, sh_ref, base, off)
        dst_ref[pl.ds(out0, CONV_CHUNK), :] = acc
        return carry

    lax.fori_loop(0, nseq // CONV_CHUNK, step, 0)


def _conv_wgrad(src_ref, sh_ref, dz_ref, acc_ref, width, nseq):
    acc_ref[...] = jnp.zeros(acc_ref.shape, F32)
    offs = _conv_offsets(width, False)

    def step(n, carry):
        dzc = dz_ref[pl.ds(pl.multiple_of(CONV_HIST + n * CONV_CHUNK, SUBLANES), CONV_CHUNK), :]
        for k, off in enumerate(offs):
            prod = dzc * _tap_rows(src_ref, sh_ref, n * CONV_CHUNK, off)
            part = prod[0:SUBLANES, :]
            for s in range(1, CONV_CHUNK // SUBLANES):
                part = part + prod[SUBLANES * s:SUBLANES * (s + 1), :]
            acc_ref[SUBLANES * k:SUBLANES * (k + 1), :] += part
        return carry

    lax.fori_loop(0, nseq // CONV_CHUNK, step, 0)


def _to_seq(buf_ref, x_part, meta_part, seq):
    buf_ref[CONV_HIST:CONV_HIST + N_META, :] = meta_part
    buf_ref[CONV_HIST + N_META:CONV_HIST + N_META + seq, :] = x_part


def _zero_ends(buf_ref, nseq):
    zeros = jnp.zeros((CONV_HIST, buf_ref.shape[1]), F32)
    buf_ref[0:CONV_HIST, :] = zeros
    buf_ref[CONV_HIST + nseq:CONV_HIST + nseq + CONV_HIST, :] = zeros


def _mix_conv_fwd(hp5, wa, wb, bb):
    _, tp, wgrp = hp5.shape
    seq, nseq = _seq_rows(tp)
    sb = nseq + 2 * CONV_HIST
    ka, kb = wa.shape[0], wb.shape[0]
    xs, ms = slice(0, seq), slice(seq, seq + N_META)
    ox, om = slice(CONV_HIST + N_META, CONV_HIST + nseq), slice(CONV_HIST, CONV_HIST + N_META)

    def body(hp_ref, wa_ref, wb_ref, bb_ref, ya_ref, z_ref, s_ref, o_ref, sh_ref):
        _zero_ends(s_ref, nseq)
        _to_seq(s_ref, hp_ref[1, xs, :] * hp_ref[2, xs, :], hp_ref[1, ms, :] * hp_ref[2, ms, :], seq)
        _conv_taps(s_ref, sh_ref, wa_ref, o_ref, ka, nseq, False)
        ya_ref[xs, :] = (hp_ref[0, xs, :] * o_ref[ox, :]).astype(BF16)
        ya_ref[ms, :] = (hp_ref[0, ms, :] * o_ref[om, :]).astype(BF16)
        ya_ref[seq + N_META:tp, :] = jnp.zeros((tp - seq - N_META, LANES), BF16)
        _to_seq(s_ref, hp_ref[3, xs, :] * _sig(hp_ref[4, xs, :]), hp_ref[3, ms, :] * _sig(hp_ref[4, ms, :]), seq)
        _conv_taps(s_ref, sh_ref, wb_ref, o_ref, kb, nseq, False)
        z_ref[xs, :] = o_ref[ox, :] + bb_ref[...]
        z_ref[ms, :] = o_ref[om, :] + bb_ref[...]
        z_ref[seq + N_META:tp, :] = jnp.zeros((tp - seq - N_META, LANES), F32)

    col = lambda j: (0, j)
    return pl.pallas_call(
        body, name="mix_conv_fwd", grid=(wgrp // LANES,),
        in_specs=[pl.BlockSpec((5, tp, LANES), lambda j: (0, 0, j)), pl.BlockSpec((ka, LANES), col),
                  pl.BlockSpec((kb, LANES), col), pl.BlockSpec((1, LANES), col)],
        out_specs=[pl.BlockSpec((tp, LANES), col), pl.BlockSpec((tp, LANES), col)],
        out_shape=[_sds((tp, wgrp), BF16), _sds((tp, wgrp), F32)],
        scratch_shapes=[pltpu.VMEM((sb, LANES), F32), pltpu.VMEM((sb, LANES), F32),
                        pltpu.VMEM((SUBLANES - 1, sb, LANES), F32)],
        compiler_params=_params(("arbitrary",)),
    )(hp5, wa, wb, bb)


def _layer_norm_parts(z, lg, lb):
    mu = _mean(z)
    zc = z - mu
    rl = lax.rsqrt(_mean(zc * zc) + LN_EPS)
    zh = zc * rl
    return rl, zh, zh * lg + lb


def _mm_out(ya, z, h, wout, lg, lb, gpm, g2):
    tp, d = h.shape
    wa_ = ya.shape[1]
    tm = _row_tile(tp)

    def body(ya_ref, z_ref, h_ref, w_ref, lg_ref, lb_ref, gpm_ref, g2_ref, yb_ref, mix_ref, h1_ref, xn2_ref):
        for rs in _row_parts(tm):
            _, _, l = _layer_norm_parts(z_ref[rs, :], lg_ref[...], lb_ref[...])
            yb = (l * _sig(l)).astype(BF16)
            yb_ref[rs, :] = yb
            mix = _dot(ya_ref[rs, :], w_ref[0:wa_, :]) + _dot(yb, w_ref[wa_:d, :])
            mix_ref[rs, :] = mix
            rm = lax.rsqrt(_mean(mix * mix) + RMS_EPS)
            h1 = h_ref[rs, :] + mix * rm * gpm_ref[...]
            h1_ref[rs, :] = h1
            r2 = lax.rsqrt(_mean(h1 * h1) + RMS_EPS)
            xn2_ref[rs, :] = (h1 * r2 * g2_ref[...]).astype(BF16)

    row = lambda i: (i, 0)
    return pl.pallas_call(
        body, name="mm_out", grid=(tp // tm,),
        in_specs=[pl.BlockSpec((tm, wa_), row), pl.BlockSpec((tm, wa_), row), pl.BlockSpec((tm, d), row),
                  _resident(wout.shape), _full(lg.shape), _full(lb.shape), _full(gpm.shape), _full(g2.shape)],
        out_specs=[pl.BlockSpec((tm, wa_), row), pl.BlockSpec((tm, d), row), pl.BlockSpec((tm, d), row),
                   pl.BlockSpec((tm, d), row)],
        out_shape=[_sds((tp, wa_), BF16), _sds((tp, d), F32), _sds((tp, d), F32), _sds((tp, d), BF16)],
        compiler_params=_params(("arbitrary",)),
    )(ya, z, h, wout, lg, lb, gpm, g2)


def _ffn_up(xn2, wg, wu):
    tp, d = xn2.shape
    ff_dim = wg.shape[0]
    tm = _row_tile(tp)
    assert ff_dim % MXU_TILE == 0

    def body(xn_ref, wg_ref, wu_ref, p_ref, q_ref, f_ref):
        xn = xn_ref[...]
        for lo in range(0, ff_dim, MXU_TILE):
            cols = slice(lo, lo + MXU_TILE)
            a = _dot(xn, wg_ref[cols, :], NT)
            u = _dot(xn, wu_ref[cols, :], NT)
            s = _sig(a)
            q = a * s
            p_ref[:, cols] = (u * (s + q * (1.0 - s))).astype(BF16)
            q_ref[:, cols] = q.astype(BF16)
            f_ref[:, cols] = (q * u).astype(BF16)

    ospec = pl.BlockSpec((tm, ff_dim), lambda i: (i, 0))
    return pl.pallas_call(
        body, name="ffn_up", grid=(tp // tm,),
        in_specs=[pl.BlockSpec((tm, d), lambda i: (i, 0)), _resident(wg.shape), _resident(wu.shape)],
        out_specs=[ospec, ospec, ospec],
        out_shape=[_sds((tp, ff_dim), BF16)] * 3,
        compiler_params=_params(("arbitrary",)),
    )(xn2, wg, wu)


def _ffn_down(f, wd, h1, tgt, gpf):
    tp, ff_dim = f.shape
    d = h1.shape[1]
    tm = _row_tile(tp)
    seq, _ = _seq_rows(tp)

    def body(f_ref, w_ref, h1_ref, t_ref, gpf_ref, dff_ref, dh2_ref, loss_ref, dgpf_ref):
        i = pl.program_id(0)
        gpf_ = gpf_ref[...]

        @pl.when(i == 0)
        def _():
            loss_ref[...] = jnp.zeros(loss_ref.shape, F32)
            dgpf_ref[...] = jnp.zeros(dgpf_ref.shape, F32)

        for rs in _row_parts(tm):
            ff = _dot(f_ref[rs, :], w_ref[...])
            rf = lax.rsqrt(_mean(ff * ff) + RMS_EPS)
            nf = ff * rf
            h2 = h1_ref[rs, :] + nf * gpf_
            rows = i * tm + rs.start + lax.broadcasted_iota(jnp.int32, (rs.stop - rs.start, 1), 0)
            err = jnp.where(rows < seq, h2 - t_ref[rs, :], 0.0)
            dh2 = err * (1.0 / d)
            dh2_ref[rs, :] = dh2
            dn = dh2 * gpf_
            dff_ref[rs, :] = (rf * (dn - nf * _mean(dn * nf))).astype(BF16)
            loss_ref[...] += (0.5 / d) * jnp.sum(err * err, axis=(0, 1), keepdims=True)
            dgpf_ref[...] += jnp.sum(dh2 * nf, axis=0, keepdims=True)

    row = lambda i: (i, 0)
    return pl.pallas_call(
        body, name="ffn_down", grid=(tp // tm,),
        in_specs=[pl.BlockSpec((tm, ff_dim), row), _resident(wd.shape), pl.BlockSpec((tm, d), row),
                  pl.BlockSpec((tm, d), row), _full(gpf.shape)],
        out_specs=[pl.BlockSpec((tm, d), row), pl.BlockSpec((tm, d), row), _full((SUBLANES, LANES)), _full((1, d))],
        out_shape=[_sds((tp, d), BF16), _sds((tp, d), F32), _sds((SUBLANES, LANES), F32), _sds((1, d), F32)],
        compiler_params=_params(("arbitrary",)),
    )(f, wd, h1, tgt, gpf)


def _ffn_bwd_act(dff, wd, p, q):
    tp, d = dff.shape
    ff_dim = wd.shape[0]
    tm = _row_tile(tp)

    def body(dff_ref, w_ref, p_ref, q_ref, da_ref, du_ref):
        dffv = dff_ref[...]
        for lo in range(0, ff_dim, MXU_TILE):
            cols = slice(lo, lo + MXU_TILE)
            df = _dot(dffv, w_ref[cols, :], NT).astype(BF16)
            da_ref[:, cols] = df * p_ref[:, cols]
            du_ref[:, cols] = df * q_ref[:, cols]

    aspec = pl.BlockSpec((tm, ff_dim), lambda i: (i, 0))
    return pl.pallas_call(
        body, name="ffn_bwd_act", grid=(tp // tm,),
        in_specs=[pl.BlockSpec((tm, d), lambda i: (i, 0)), _resident(wd.shape), aspec, aspec],
        out_specs=[aspec, aspec],
        out_shape=[_sds((tp, ff_dim), BF16)] * 2,
        compiler_params=_params(("arbitrary",)),
    )(dff, wd, p, q)


def _grad_blocks(ff_dim):
    rows = ff_dim // 2
    assert rows % LANES == 0
    return rows


def _grad_w_down(f, dff):
    tp, ff_dim = f.shape
    d = dff.shape[1]
    rows = _grad_blocks(ff_dim)

    def body(f_ref, dff_ref, g_ref):
        g_ref[...] = _dot(f_ref[...], dff_ref[...], TN).astype(BF16)

    return pl.pallas_call(
        body, name="grad_w_down", grid=(ff_dim // rows,),
        in_specs=[pl.BlockSpec((tp, rows), lambda k: (0, k)), _resident(dff.shape)],
        out_specs=pl.BlockSpec((rows, d), lambda k: (k, 0)),
        out_shape=_sds((ff_dim, d), BF16),
        compiler_params=_params(("arbitrary",)),
    )(f, dff)


def _grad_w_gate_up(xn2, da, du):
    tp, ff_dim = da.shape
    d = xn2.shape[1]
    rows = _grad_blocks(ff_dim)

    def body(xn_ref, da_ref, du_ref, gg_ref, gu_ref):
        xn = xn_ref[...]
        gg_ref[...] = _dot(da_ref[...], xn, TN).astype(BF16)
        gu_ref[...] = _dot(du_ref[...], xn, TN).astype(BF16)

    aspec = pl.BlockSpec((tp, rows), lambda k: (0, k))
    gspec = pl.BlockSpec((rows, d), lambda k: (k, 0))
    return pl.pallas_call(
        body, name="grad_w_gate_up", grid=(ff_dim // rows,),
        in_specs=[_resident(xn2.shape), aspec, aspec],
        out_specs=[gspec, gspec],
        out_shape=[_sds((ff_dim, d), BF16)] * 2,
        compiler_params=_params(("arbitrary",)),
    )(xn2, da, du)


def _rms_bwd(dy, x, r, g):
    n = x * r
    dn = dy * g
    return r * (dn - n * _mean(dn * n)), dy * n


def _ffn_bwd_in(da, du, wg, wu, h1, mix, dh2, g2, gpm, after):
    tp, ff_dim = da.shape
    d = h1.shape[1]
    tm = _row_tile(tp)

    def body(da_ref, du_ref, wg_ref, wu_ref, h1_ref, mix_ref, dh2_ref, g2_ref, gpm_ref, after_ref,
             dh1_ref, dmix_ref, dg2_ref, dgpm_ref):
        i = pl.program_id(0)

        @pl.when(i == 0)
        def _():
            dg2_ref[...] = jnp.zeros(dg2_ref.shape, F32)
            dgpm_ref[...] = jnp.zeros(dgpm_ref.shape, F32)

        for rs in _row_parts(tm):
            dxn = _dot(da_ref[rs, :], wg_ref[...]) + _dot(du_ref[rs, :], wu_ref[...])
            h1v = h1_ref[rs, :]
            r2 = lax.rsqrt(_mean(h1v * h1v) + RMS_EPS)
            dres, dg2_rows = _rms_bwd(dxn, h1v, r2, g2_ref[...])
            dh1 = dh2_ref[rs, :] + dres
            dh1_ref[rs, :] = dh1
            mixv = mix_ref[rs, :]
            rm = lax.rsqrt(_mean(mixv * mixv) + RMS_EPS)
            dmix, dgpm_rows = _rms_bwd(dh1, mixv, rm, gpm_ref[...])
            dmix_ref[rs, :] = dmix.astype(BF16)
            dg2_ref[...] += jnp.sum(dg2_rows, axis=0, keepdims=True)
            dgpm_ref[...] += jnp.sum(dgpm_rows, axis=0, keepdims=True)

    aspec = pl.BlockSpec((tm, ff_dim), lambda i: (i, 0))
    row = pl.BlockSpec((tm, d), lambda i: (i, 0))
    return pl.pallas_call(
        body, name="ffn_bwd_in", grid=(tp // tm,),
        in_specs=[aspec, aspec, _resident(wg.shape), _resident(wu.shape), row, row, row, _full(g2.shape), _full(gpm.shape),
                  ANY],
        out_specs=[row, row, _full((1, d)), _full((1, d))],
        out_shape=[_sds((tp, d), F32), _sds((tp, d), BF16), _sds((1, d), F32), _sds((1, d), F32)],
        compiler_params=_params(("arbitrary",)),
    )(da, du, wg, wu, h1, mix, dh2, g2, gpm, after)


def _grad_w_out(ya, yb, dmix, after):
    tp, wa_ = ya.shape
    d = dmix.shape[1]

    def body(ya_ref, yb_ref, dmix_ref, after_ref, g_ref):
        dm = dmix_ref[...]
        g_ref[0:wa_, :] = _dot(ya_ref[...], dm, TN).astype(BF16)
        g_ref[wa_:2 * wa_, :] = _dot(yb_ref[...], dm, TN).astype(BF16)

    return pl.pallas_call(
        body, name="grad_w_out", grid=(1,),
        in_specs=[_full(ya.shape), _full(yb.shape), _full(dmix.shape), ANY],
        out_specs=_full((2 * wa_, d)),
        out_shape=_sds((2 * wa_, d), BF16),
        compiler_params=_params(("arbitrary",)),
    )(ya, yb, dmix, after)


def _mix_bwd_out(dmix, wout, z, lg, lb, after):
    tp, d = dmix.shape
    wa_ = z.shape[1]
    tm = _row_tile(tp)

    def body(dmix_ref, w_ref, z_ref, lg_ref, lb_ref, after_ref, dya_ref, dz_ref, dlg_ref, dlb_ref):
        i = pl.program_id(0)
        lg_ = lg_ref[...]

        @pl.when(i == 0)
        def _():
            dlg_ref[...] = jnp.zeros(dlg_ref.shape, F32)
            dlb_ref[...] = jnp.zeros(dlb_ref.shape, F32)

        for rs in _row_parts(tm):
            dm = dmix_ref[rs, :]
            dya_ref[rs, :] = _dot(dm, w_ref[0:wa_, :], NT)
            dyb = _dot(dm, w_ref[wa_:d, :], NT)
            rl, zh, l = _layer_norm_parts(z_ref[rs, :], lg_, lb_ref[...])
            sl = _sig(l)
            dl = dyb * (sl * (1.0 + l * (1.0 - sl)))
            dzh = dl * lg_
            dz_ref[rs, :] = rl * (dzh - _mean(dzh) - zh * _mean(dzh * zh))
            dlg_ref[...] += jnp.sum(dl * zh, axis=0, keepdims=True)
            dlb_ref[...] += jnp.sum(dl, axis=0, keepdims=True)

    row = lambda i: (i, 0)
    return pl.pallas_call(
        body, name="mix_bwd_out", grid=(tp // tm,),
        in_specs=[pl.BlockSpec((tm, d), row), _resident(wout.shape), pl.BlockSpec((tm, wa_), row), _full(lg.shape),
                  _full(lb.shape), ANY],
        out_specs=[pl.BlockSpec((tm, wa_), row), pl.BlockSpec((tm, wa_), row), _full((1, wa_)), _full((1, wa_))],
        out_shape=[_sds((tp, wa_), F32), _sds((tp, wa_), F32), _sds((1, wa_), F32), _sds((1, wa_), F32)],
        compiler_params=_params(("arbitrary",)),
    )(dmix, wout, z, lg, lb, after)


def _mix_conv_bwd(hp5, dya, dz, wa, wb):
    _, tp, wgrp = hp5.shape
    seq, nseq = _seq_rows(tp)
    sb = nseq + 2 * CONV_HIST
    ka, kb = wa.shape[0], wb.shape[0]
    xs, ms = slice(0, seq), slice(seq, seq + N_META)
    ox, om = slice(CONV_HIST + N_META, CONV_HIST + nseq), slice(CONV_HIST, CONV_HIST + N_META)
    n_tail = tp - seq - N_META

    def body(hp_ref, dya_ref, dz_ref, wa_ref, wb_ref, dhp_ref, dwa_ref, dwb_ref, dbb_ref, s_ref, d_ref, o_ref, acc_ref,
             shs_ref, shd_ref):
        _zero_ends(s_ref, nseq)
        _zero_ends(d_ref, nseq)

        def put(p, ox_val, om_val):
            dhp_ref[p, xs, :] = ox_val.astype(BF16)
            dhp_ref[p, ms, :] = om_val.astype(BF16)
            dhp_ref[p, seq + N_META:tp, :] = jnp.zeros((n_tail, LANES), BF16)

        def wgrad(dw_ref, width):
            for k in range(width):
                dw_ref[k:k + 1, :] = jnp.sum(acc_ref[8 * k:8 * k + 8, :], axis=0, keepdims=True)

        _to_seq(s_ref, hp_ref[1, xs, :] * hp_ref[2, xs, :], hp_ref[1, ms, :] * hp_ref[2, ms, :], seq)
        _conv_taps(s_ref, shs_ref, wa_ref, o_ref, ka, nseq, False)
        put(0, dya_ref[xs, :] * o_ref[ox, :], dya_ref[ms, :] * o_ref[om, :])
        _to_seq(d_ref, dya_ref[xs, :] * hp_ref[0, xs, :], dya_ref[ms, :] * hp_ref[0, ms, :], seq)
        _conv_wgrad(s_ref, shs_ref, d_ref, acc_ref, ka, nseq)
        wgrad(dwa_ref, ka)
        _conv_taps(d_ref, shd_ref, wa_ref, o_ref, ka, nseq, True)
        put(1, o_ref[ox, :] * hp_ref[2, xs, :], o_ref[om, :] * hp_ref[2, ms, :])
        put(2, o_ref[ox, :] * hp_ref[1, xs, :], o_ref[om, :] * hp_ref[1, ms, :])

        _to_seq(s_ref, hp_ref[3, xs, :] * _sig(hp_ref[4, xs, :]), hp_ref[3, ms, :] * _sig(hp_ref[4, ms, :]), seq)
        _to_seq(d_ref, dz_ref[xs, :], dz_ref[ms, :], seq)
        dbb_ref[...] = (jnp.sum(dz_ref[xs, :], axis=0, keepdims=True)
                        + jnp.sum(dz_ref[ms, :], axis=0, keepdims=True))
        _shift_copies(s_ref, shs_ref, kb, False)
        _conv_wgrad(s_ref, shs_ref, d_ref, acc_ref, kb, nseq)
        wgrad(dwb_ref, kb)
        _conv_taps(d_ref, shd_ref, wb_ref, o_ref, kb, nseq, True)
        sx, sm = _sig(hp_ref[4, xs, :]), _sig(hp_ref[4, ms, :])
        put(3, o_ref[ox, :] * sx, o_ref[om, :] * sm)
        put(4, o_ref[ox, :] * hp_ref[3, xs, :] * sx * (1.0 - sx), o_ref[om, :] * hp_ref[3, ms, :] * sm * (1.0 - sm))

    col = lambda j: (0, j)
    blk5 = pl.BlockSpec((5, tp, LANES), lambda j: (0, 0, j))
    return pl.pallas_call(
        body, name="mix_conv_bwd", grid=(wgrp // LANES,),
        in_specs=[blk5, pl.BlockSpec((tp, LANES), col), pl.BlockSpec((tp, LANES), col),
                  pl.BlockSpec((ka, LANES), col), pl.BlockSpec((kb, LANES), col)],
        out_specs=[blk5, pl.BlockSpec((ka, LANES), col), pl.BlockSpec((kb, LANES), col), pl.BlockSpec((1, LANES), col)],
        out_shape=[_sds((5, tp, wgrp), BF16), _sds((ka, wgrp), F32), _sds((kb, wgrp), F32), _sds((1, wgrp), F32)],
        scratch_shapes=[pltpu.VMEM((sb, LANES), F32), pltpu.VMEM((sb, LANES), F32), pltpu.VMEM((sb, LANES), F32),
                        pltpu.VMEM((SUBLANES * kb, LANES), F32), pltpu.VMEM((SUBLANES - 1, sb, LANES), F32),
                        pltpu.VMEM((SUBLANES - 1, sb, LANES), F32)],
        compiler_params=_params(("arbitrary",)),
    )(hp5, dya, dz, wa, wb)


def _grad_w_in(xn1, dhp5):
    n_p, tp, pw = dhp5.shape
    d = xn1.shape[1]

    def body(xn_ref, dhp_ref, g_ref):
        g_ref[...] = _dot(xn_ref[...], dhp_ref[0], TN).astype(BF16)

    return pl.pallas_call(
        body, name="grad_w_in", grid=(n_p,),
        in_specs=[_resident(xn1.shape), pl.BlockSpec((1, tp, pw), lambda p: (p, 0, 0))],
        out_specs=pl.BlockSpec((d, pw), lambda p: (0, p)),
        out_shape=_sds((d, n_p * pw), BF16),
        compiler_params=_params(("arbitrary",)),
    )(xn1, dhp5)


def _mix_bwd_in(dhp5, win4, h, dh1, g1, after):
    n_p, tp, pw = dhp5.shape
    d = h.shape[1]
    n_sh, _, csh = win4.shape
    tm = _row_tile(tp)

    seq, _ = _seq_rows(tp)
    last, meta_off = seq // tm, seq % tm
    assert last == tp // tm - 1
    assert any(rs.start <= meta_off and meta_off + N_META <= rs.stop for rs in _row_parts(tm))

    def body(dhp_ref, w_ref, h_ref, dh1_ref, g_ref, after_ref, gx_ref, dmeta_ref, dg1_ref, wcat_ref):
        i = pl.program_id(0)
        _concat_shards(w_ref, wcat_ref)

        @pl.when(i == 0)
        def _():
            dg1_ref[...] = jnp.zeros(dg1_ref.shape, F32)

        for rs in _row_parts(tm):
            dxn = _dot(dhp_ref[0, rs, :], wcat_ref[:, 0:pw], NT)
            for p in range(1, n_p):
                dxn = dxn + _dot(dhp_ref[p, rs, :], wcat_ref[:, p * pw:(p + 1) * pw], NT)
            hh = h_ref[rs, :]
            r1 = lax.rsqrt(_mean(hh * hh) + RMS_EPS)
            dres, dg_rows = _rms_bwd(dxn, hh, r1, g_ref[...])
            dh = dh1_ref[rs, :] + dres
            gx_ref[rs, :] = dh
            dg1_ref[...] += jnp.sum(dg_rows, axis=0, keepdims=True)
            if rs.start <= meta_off and meta_off + N_META <= rs.stop:
                @pl.when(i == last)
                def _():
                    dmeta_ref[...] = dh[meta_off - rs.start:meta_off - rs.start + N_META, :]

    row = lambda i: (i, 0)
    return pl.pallas_call(
        body, name="mix_bwd_in", grid=(tp // tm,),
        in_specs=[pl.BlockSpec((n_p, tm, pw), lambda i: (0, i, 0)), _resident(win4.shape), pl.BlockSpec((tm, d), row),
                  pl.BlockSpec((tm, d), row), _full(g1.shape), ANY],
        out_specs=[pl.BlockSpec((tm, d), row), _full((N_META, d)), _full((1, d))],
        out_shape=[_sds((seq, d), F32), _sds((N_META, d), F32), _sds((1, d), F32)],
        scratch_shapes=[pltpu.VMEM((d, n_sh * csh), BF16)],
        compiler_params=_params(("arbitrary",)),
    )(dhp5, win4, h, dh1, g1, after)


def _other_chips(x, y):
    out = []
    for j in (1, 2, 3):
        px, py = _flip(x, j >> 1), _flip(y, j & 1)
        out.append((px, py, 2 * px + py))
    return out


PAIR_COLLECTIVE_ID = 0


def _pair_barrier(x, y, c):
    sem = pltpu.get_barrier_semaphore()
    pl.semaphore_signal(sem, inc=1, device_id=(x, y, 1 - c), device_id_type=MESH)
    pl.semaphore_wait(sem, 1)


def _pair_params():
    return pltpu.CompilerParams(collective_id=PAIR_COLLECTIVE_ID)


def _half_rows(c, rows_half):
    return pl.ds(pl.multiple_of(c * rows_half, SUBLANES), rows_half)


def _cast_place(ws, q_arr, tag, after=None):
    n = len(ws)
    extra = [] if after is None else [after]

    def fits(steps):
        return all(w.shape[0] % steps == 0 and (w.shape[0] // steps) % BF16_ROWS == 0
                   and w.shape[0] // steps <= ROW_TILE_CAP for w in ws)

    steps = next(s for s in range(1, min(w.shape[0] for w in ws) + 1) if fits(s))

    def body(q_ref, *refs):
        for w_ref, out_ref in zip(refs[:n], refs[n + len(extra):]):
            out_ref[0] = w_ref[...].astype(BF16)

    return list(pl.pallas_call(
        body, name="cast_place_" + tag,
        grid_spec=pltpu.PrefetchScalarGridSpec(
            num_scalar_prefetch=1, grid=(steps,),
            in_specs=[pl.BlockSpec((w.shape[0] // steps, w.shape[1]), lambda i, q: (i, 0)) for w in ws] + [ANY] * len(extra),
            out_specs=[pl.BlockSpec((1, w.shape[0] // steps, w.shape[1]), lambda i, q: (q[0], i, 0)) for w in ws]),
        out_shape=[_sds((N_CHIPS,) + w.shape, BF16) for w in ws],
        compiler_params=_params(("arbitrary",)),
    )(q_arr, *ws, *extra))


HBM = pl.BlockSpec(memory_space=pltpu.HBM)
SEM = pl.BlockSpec(memory_space=pltpu.SEMAPHORE)
EFFECT = pltpu.SideEffectType.DATAFLOW_SIDE_EFFECTING


def _in_hbm(a):
    return pltpu.with_memory_space_constraint(a, pltpu.HBM)


def _gather_start(fulls, after, tag):
    n = len(fulls)
    halves = [a.shape[1] // 2 for a in fulls]

    def body(*refs):
        land = refs[:n]
        ssem, rsem = refs[n + 1], refs[n + 2]
        token = refs[-1]
        x, y, c = _mesh_pos()
        q = 2 * x + y
        for i in range(n):
            for j, (px, py, _) in enumerate(_other_chips(x, y)):
                mine = land[i].at[q, _half_rows(c, halves[i]), :]
                pltpu.make_async_remote_copy(src_ref=mine, dst_ref=mine, send_sem=ssem.at[3 * i + j],
                                             recv_sem=rsem.at[3 * i + j], device_id=(px, py, c), device_id_type=MESH).start()
        token[...] = jnp.zeros(token.shape, F32)

    outs = pl.pallas_call(
        body, name="gather_start_" + tag,
        in_specs=[HBM] * n + [ANY], out_specs=[SEM, SEM] + [HBM] * n + [VMEM],
        out_shape=[pltpu.SemaphoreType.DMA((3 * n,)), pltpu.SemaphoreType.DMA((3 * n,))]
        + [pltpu.HBM(a.shape, a.dtype) for a in fulls] + [_sds((SUBLANES, LANES), F32)],
        input_output_aliases={i: 2 + i for i in range(n)},
        compiler_params=pltpu.CompilerParams(has_side_effects=EFFECT),
    )(*[_in_hbm(a) for a in fulls], after)
    return outs[0], outs[1], list(outs[2:2 + n]), outs[-1]


def _gather_wait(which, ssem, rsem, lands, after, tag):
    m = len(which)
    halves = [a.shape[1] // 2 for a in lands]

    def body(*refs):
        land = refs[:m]
        ssem_, rsem_ = refs[m], refs[m + 1]
        x, y, c = _mesh_pos()
        for t, i in enumerate(which):
            for j, (px, py, qj) in enumerate(_other_chips(x, y)):
                rows = _half_rows(c, halves[t])
                cp = pltpu.make_async_remote_copy(src_ref=land[t].at[2 * x + y, rows, :], dst_ref=land[t].at[qj, rows, :],
                                                  send_sem=ssem_.at[3 * i + j], recv_sem=rsem_.at[3 * i + j],
                                                  device_id=(px, py, c), device_id_type=MESH)
                cp.wait_send()
                cp.wait_recv()

    outs = pl.pallas_call(
        body, name="gather_wait_" + tag,
        in_specs=[HBM] * m + [SEM, SEM, ANY], out_specs=[HBM] * m,
        out_shape=[pltpu.HBM(a.shape, a.dtype) for a in lands],
        input_output_aliases={i: i for i in range(m)},
        compiler_params=pltpu.CompilerParams(has_side_effects=EFFECT),
    )(*lands, ssem, rsem, after)
    return list(outs)


def _forward_pair(lands, tag):
    n = len(lands)
    halves = [a.shape[1] // 2 for a in lands]

    def body(*refs):
        full = refs[n:2 * n]
        ssem, rsem = refs[2 * n:]
        x, y, c = _mesh_pos()
        _pair_barrier(x, y, c)
        cps = []
        for i in range(n):
            for j, (_, _, qj) in enumerate(_other_chips(x, y)):
                part = full[i].at[qj, _half_rows(c, halves[i]), :]
                cp = pltpu.make_async_remote_copy(src_ref=part, dst_ref=part, send_sem=ssem.at[3 * i + j],
                                                  recv_sem=rsem.at[3 * i + j], device_id=(x, y, 1 - c), device_id_type=MESH)
                cp.start()
                cps.append(cp)
        for cp in cps:
            cp.wait()

    return pl.pallas_call(
        body, name="forward_pair_" + tag,
        in_specs=[ANY] * n, out_specs=[ANY] * n,
        out_shape=[_sds(a.shape, a.dtype) for a in lands],
        input_output_aliases={i: i for i in range(n)},
        scratch_shapes=[pltpu.SemaphoreType.DMA((3 * n,)), pltpu.SemaphoreType.DMA((3 * n,))],
        compiler_params=_pair_params(),
    )(*lands)


def _chip_exchange_start(parts, after, tag):
    n = len(parts)

    def body(*refs):
        src, land = refs[:n], refs[n:2 * n]
        ssem, rsem = refs[2 * n + 1], refs[2 * n + 2]
        token = refs[-1]
        x, y, c = _mesh_pos()
        for i in range(n):
            for j, (px, py, qj) in enumerate(_other_chips(x, y)):
                pltpu.make_async_remote_copy(src_ref=src[i].at[qj], dst_ref=land[i].at[j], send_sem=ssem.at[3 * i + j],
                                             recv_sem=rsem.at[3 * i + j], device_id=(px, py, c), device_id_type=MESH).start()
        token[...] = jnp.zeros(token.shape, F32)

    lands = [lax.empty((3,) + a.shape[1:], a.dtype) for a in parts]
    outs = pl.pallas_call(
        body, name="chip_exchange_start_" + tag,
        in_specs=[HBM] * (2 * n) + [ANY], out_specs=[SEM, SEM] + [HBM] * (2 * n) + [VMEM],
        out_shape=[pltpu.SemaphoreType.DMA((3 * n,)), pltpu.SemaphoreType.DMA((3 * n,))]
        + [pltpu.HBM(a.shape, a.dtype) for a in parts] + [pltpu.HBM(a.shape, a.dtype) for a in lands]
        + [_sds((SUBLANES, LANES), F32)],
        input_output_aliases={i: 2 + i for i in range(2 * n)},
        compiler_params=pltpu.CompilerParams(has_side_effects=EFFECT),
    )(*[_in_hbm(a) for a in parts], *[_in_hbm(a) for a in lands], after)
    return outs[0], outs[1], list(outs[2:2 + n]), list(outs[2 + n:2 + 2 * n]), outs[-1]


def _chip_exchange_wait(ssem, rsem, parts, lands, after, tag):
    n = len(parts)

    def body(*refs):
        src, land = refs[:n], refs[n:2 * n]
        ssem_, rsem_ = refs[2 * n], refs[2 * n + 1]
        x, y, c = _mesh_pos()
        for i in range(n):
            for j, (px, py, qj) in enumerate(_other_chips(x, y)):
                cp = pltpu.make_async_remote_copy(src_ref=src[i].at[qj], dst_ref=land[i].at[j], send_sem=ssem_.at[3 * i + j],
                                                  recv_sem=rsem_.at[3 * i + j], device_id=(px, py, c), device_id_type=MESH)
                cp.wait_send()
                cp.wait_recv()

    outs = pl.pallas_call(
        body, name="chip_exchange_wait_" + tag,
        in_specs=[HBM] * (2 * n) + [SEM, SEM, ANY], out_specs=[HBM] * (2 * n),
        out_shape=[pltpu.HBM(a.shape, a.dtype) for a in parts] + [pltpu.HBM(a.shape, a.dtype) for a in lands],
        input_output_aliases={i: i for i in range(2 * n)},
        compiler_params=pltpu.CompilerParams(has_side_effects=EFFECT),
    )(*parts, *lands, ssem, rsem, after)
    return list(outs[:n]), list(outs[n:])


def _grad_half(ref, shape, axis, which):
    rows = shape[axis] // 2
    if axis == 0:
        return ref.at[_half_rows(which, rows), :]
    return ref.at[:, _half_rows(which, rows), :]


def _half_shape(a, axis):
    s = list(a.shape)
    s[axis] //= 2
    return tuple(s)


def _pair_exchange_start(grads, half_axis, after, tag):
    n = len(grads)

    def body(*refs):
        g, land = refs[:n], refs[n:2 * n]
        ssem, rsem = refs[2 * n + 1], refs[2 * n + 2]
        token = refs[-1]
        x, y, c = _mesh_pos()
        for i in range(n):
            pltpu.make_async_remote_copy(src_ref=_grad_half(g[i], grads[i].shape, half_axis[i], 1 - c), dst_ref=land[i],
                                         send_sem=ssem.at[i], recv_sem=rsem.at[i], device_id=(x, y, 1 - c),
                                         device_id_type=MESH).start()
        token[...] = jnp.zeros(token.shape, F32)

    lands = [lax.empty(_half_shape(a, half_axis[i]), a.dtype) for i, a in enumerate(grads)]
    outs = pl.pallas_call(
        body, name="pair_exchange_start_" + tag,
        in_specs=[HBM] * (2 * n) + [ANY], out_specs=[SEM, SEM] + [HBM] * (2 * n) + [VMEM],
        out_shape=[pltpu.SemaphoreType.DMA((n,)), pltpu.SemaphoreType.DMA((n,))]
        + [pltpu.HBM(a.shape, a.dtype) for a in grads] + [pltpu.HBM(a.shape, a.dtype) for a in lands]
        + [_sds((SUBLANES, LANES), F32)],
        input_output_aliases={i: 2 + i for i in range(2 * n)},
        compiler_params=pltpu.CompilerParams(has_side_effects=EFFECT),
    )(*[_in_hbm(a) for a in grads], *[_in_hbm(a) for a in lands], after)
    return outs[0], outs[1], list(outs[2:2 + n]), list(outs[2 + n:2 + 2 * n]), outs[-1]


def _pair_exchange_wait(ssem, rsem, grads, lands, half_axis, after, tag):
    n = len(grads)

    def body(*refs):
        g, land = refs[:n], refs[n:2 * n]
        ssem_, rsem_ = refs[2 * n], refs[2 * n + 1]
        x, y, c = _mesh_pos()
        for i in range(n):
            cp = pltpu.make_async_remote_copy(src_ref=_grad_half(g[i], grads[i].shape, half_axis[i], 1 - c),
                                              dst_ref=land[i], send_sem=ssem_.at[i], recv_sem=rsem_.at[i],
                                              device_id=(x, y, 1 - c), device_id_type=MESH)
            cp.wait_send()
            cp.wait_recv()

    outs = pl.pallas_call(
        body, name="pair_exchange_wait_" + tag,
        in_specs=[HBM] * (2 * n) + [SEM, SEM, ANY], out_specs=[HBM] * (2 * n),
        out_shape=[pltpu.HBM(a.shape, a.dtype) for a in grads] + [pltpu.HBM(a.shape, a.dtype) for a in lands],
        input_output_aliases={i: i for i in range(2 * n)},
        compiler_params=pltpu.CompilerParams(has_side_effects=EFFECT),
    )(*grads, *lands, ssem, rsem, after)
    return list(outs[:n]), list(outs[n:])


def _pair_exchange_grads(grads, half_axis, tag):
    n = len(grads)

    def body(*refs):
        g, got = refs[:n], refs[n:2 * n]
        ssem, rsem = refs[2 * n:]
        x, y, c = _mesh_pos()
        _pair_barrier(x, y, c)
        cps = []
        for i in range(n):
            cp = pltpu.make_async_remote_copy(src_ref=_grad_half(g[i], grads[i].shape, half_axis[i], 1 - c),
                                              dst_ref=got[i], send_sem=ssem.at[i], recv_sem=rsem.at[i],
                                              device_id=(x, y, 1 - c), device_id_type=MESH)
            cp.start()
            cps.append(cp)
        for cp in cps:
            cp.wait()

    return pl.pallas_call(
        body, name="pair_exchange_grads_" + tag,
        in_specs=[ANY] * n, out_specs=[ANY] * n,
        out_shape=[_sds(_half_shape(a, half_axis[i]), a.dtype) for i, a in enumerate(grads)],
        scratch_shapes=[pltpu.SemaphoreType.DMA((n,)), pltpu.SemaphoreType.DMA((n,))],
        compiler_params=_pair_params(),
    )(*grads)


def _pair_sum(gs, gots, c_arr, col_sharded, tag):
    n = len(gs)
    g_specs, got_specs, out_specs, out_shapes = [], [], [], []
    for g, by_cols in zip(gs, col_sharded):
        if by_cols:
            rows, cols = g.shape
            rh, cs = rows // 2, cols // N_CHIPS
            g_specs.append(pl.BlockSpec((rh, cs), lambda k, c_ref: (c_ref[0], k)))
            got_specs.append(pl.BlockSpec((rh, cs), lambda k, c_ref: (0, k)))
        else:
            _, rows, cs = g.shape
            rh = rows // 2
            g_specs.append(pl.BlockSpec((1, rh, cs), lambda k, c_ref: (k, c_ref[0], 0)))
            got_specs.append(pl.BlockSpec((1, rh, cs), lambda k, c_ref: (k, 0, 0)))
        out_specs.append(pl.BlockSpec((1, rh, cs), lambda k, c_ref: (k, 0, 0)))
        out_shapes.append(_sds((N_CHIPS, rh, cs), BF16))

    def body(c_ref, *refs):
        for g_ref, got_ref, out_ref in zip(refs[:n], refs[n:2 * n], refs[2 * n:]):
            total = g_ref[...].astype(F32) + got_ref[...].astype(F32)
            out_ref[...] = total.astype(BF16).reshape(out_ref.shape)

    return list(pl.pallas_call(
        body, name="pair_sum_" + tag,
        grid_spec=pltpu.PrefetchScalarGridSpec(
            num_scalar_prefetch=1, grid=(N_CHIPS,), in_specs=g_specs + got_specs, out_specs=out_specs),
        out_shape=out_shapes,
        compiler_params=_params(("arbitrary",)),
    )(c_arr, *gs, *gots))


def _chip_sum(parts, gots, qc_arr, tag):
    n = len(parts)
    steps = 2 if all(p.shape[1] % 32 == 0 for p in parts) else 1
    part_specs, got_specs, out_specs, out_shapes = [], [], [], []
    for p in parts:
        _, rh, cs = p.shape
        rb = rh // steps
        part_specs.append(pl.BlockSpec((1, rb, cs), lambda i, qc: (qc[0], i, 0)))
        got_specs.append(pl.BlockSpec((3, rb, cs), lambda i, qc: (0, i, 0)))
        out_specs.append(pl.BlockSpec((rb, cs), lambda i, qc: (qc[1] * steps + i, 0)))
        out_shapes.append(_sds((2 * rh, cs), F32))

    def body(qc_ref, *refs):
        for part_ref, got_ref, out_ref in zip(refs[:n], refs[n:2 * n], refs[2 * n:]):
            total = part_ref[0].astype(F32)
            for j in range(3):
                total = total + got_ref[j].astype(F32)
            out_ref[...] = total

    return list(pl.pallas_call(
        body, name="chip_sum_" + tag,
        grid_spec=pltpu.PrefetchScalarGridSpec(
            num_scalar_prefetch=1, grid=(steps,), in_specs=part_specs + got_specs, out_specs=out_specs),
        out_shape=out_shapes,
        compiler_params=_params(("arbitrary",)),
    )(qc_arr, *parts, *gots))


def _pair_share_grads(grads, tag):
    n = len(grads)

    def body(*refs):
        g = refs[n:2 * n]
        ssem, rsem = refs[2 * n:]
        x, y, c = _mesh_pos()
        _pair_barrier(x, y, c)
        cps = []
        for i in range(n):
            mine = g[i].at[_half_rows(c, grads[i].shape[0] // 2), :]
            cp = pltpu.make_async_remote_copy(src_ref=mine, dst_ref=mine, send_sem=ssem.at[i], recv_sem=rsem.at[i],
                                              device_id=(x, y, 1 - c), device_id_type=MESH)
            cp.start()
            cps.append(cp)
        for cp in cps:
            cp.wait()

    return pl.pallas_call(
        body, name="pair_share_grads_" + tag,
        in_specs=[ANY] * n, out_specs=[ANY] * n,
        out_shape=[_sds(a.shape, a.dtype) for a in grads],
        input_output_aliases={i: i for i in range(n)},
        scratch_shapes=[pltpu.SemaphoreType.DMA((n,)), pltpu.SemaphoreType.DMA((n,))],
        compiler_params=_pair_params(),
    )(*grads)


def _small_allreduce(parts, places, rows_total, width, after):
    n = len(parts)

    def body(*refs):
        ins, out_ref = refs[:n], refs[n + 1]
        pack, pair_got, chip_sum, got, ssem, rsem = refs[n + 2:]
        x, y, c = _mesh_pos()
        chip = 2 * x + y
        pack[...] = jnp.zeros(pack.shape, F32)
        for i in range(n):
            for row, col, src_row, rows in places[i]:
                w = parts[i].shape[1]
                pack[row:row + rows, col:col + w] = ins[i][src_row:src_row + rows, :]
        swap = pltpu.make_async_remote_copy(src_ref=pack, dst_ref=pair_got, send_sem=ssem.at[3], recv_sem=rsem.at[3],
                                            device_id=(x, y, 1 - c), device_id_type=MESH)
        swap.start()
        swap.wait()
        chip_sum[...] = pack[...] + pair_got[...]
        cps = []
        for j, (px, py, _) in enumerate(_other_chips(x, y)):
            cp = pltpu.make_async_remote_copy(src_ref=chip_sum, dst_ref=got.at[j], send_sem=ssem.at[j],
                                              recv_sem=rsem.at[j], device_id=(px, py, c), device_id_type=MESH)
            cp.start()
            cps.append(cp)
        for cp in cps:
            cp.wait()
        total = jnp.zeros(pack.shape, F32)
        for q in range(N_CHIPS):
            rel = jnp.bitwise_xor(chip, q)
            theirs = got[jnp.maximum(rel - 1, 0)]
            total = total + jnp.where(rel == 0, chip_sum[...], theirs)
        out_ref[...] = total

    return pl.pallas_call(
        body, name="small_allreduce",
        in_specs=[VMEM] * n + [ANY], out_specs=VMEM,
        out_shape=_sds((rows_total, width), F32),
        scratch_shapes=[pltpu.VMEM((rows_total, width), F32), pltpu.VMEM((rows_total, width), F32),
                        pltpu.VMEM((rows_total, width), F32), pltpu.VMEM((3, rows_total, width), F32),
                        pltpu.SemaphoreType.DMA((4,)), pltpu.SemaphoreType.DMA((4,))],
        compiler_params=_params(),
    )(*parts, after)


def _small_update(red, q_arr, takes, loss_at, ws, ms, vs):
    n_w = len(ws)

    def body(q_ref, red_ref, *refs):
        w_in, m_in, v_in = refs[0:n_w], refs[n_w:2 * n_w], refs[2 * n_w:3 * n_w]
        outs = refs[3 * n_w:]
        g_out, d_out, m_out, v_out = (outs[0:n_w], outs[n_w:2 * n_w], outs[2 * n_w:3 * n_w], outs[3 * n_w:4 * n_w])
        loss_ref = outs[4 * n_w]
        chip = q_ref[0]

        def put(g_ref, d0, nr, s0, lo, w):
            if len(g_ref.shape) == 3:
                for r in range(nr):
                    g_ref[d0 + r] = red_ref[s0 + r:s0 + r + 1, lo:lo + w]
            else:
                g_ref[d0:d0 + nr, :] = red_ref[s0:s0 + nr, lo:lo + w]

        def take_own_columns(g_ref, d0, nr, s0, c0, w):
            for k in range(N_CHIPS):
                @pl.when(chip == k)
                def _():
                    put(g_ref, d0, nr, s0, c0 + k * w, w)

        for j in range(n_w):
            w = ws[j].shape[-1]
            for d0, nr, s0, c0, sharded in takes[j]:
                if sharded:
                    take_own_columns(g_out[j], d0, nr, s0, c0, w)
                else:
                    put(g_out[j], d0, nr, s0, c0, w)
            d_out[j][...], m_out[j][...], v_out[j][...] = _adamw_math(w_in[j][...], g_out[j][...], m_in[j][...], v_in[j][...])
        loss_ref[...] = red_ref[loss_at[0]:loss_at[0] + 1, loss_at[1]:loss_at[1] + LANES]

    shapes = [_sds(w.shape, F32) for w in ws]
    outs = pl.pallas_call(
        body, name="small_update",
        in_specs=[pl.BlockSpec(memory_space=pltpu.SMEM)] + [VMEM] * (1 + 3 * n_w), out_specs=[VMEM] * (4 * n_w + 1),
        out_shape=shapes * 4 + [_sds((1, LANES), F32)],
        compiler_params=_params(),
    )(q_arr, red, *ws, *ms, *vs)
    return outs[0:n_w], outs[n_w:2 * n_w], outs[2 * n_w:3 * n_w], outs[3 * n_w:4 * n_w], outs[4 * n_w]


def _adamw_math(w, g, m, v):
    m2 = ADAM_B1 * m + (1.0 - ADAM_B1) * g
    v2 = ADAM_B2 * v + (1.0 - ADAM_B2) * (g * g)
    m_hat = m2 / (1.0 - ADAM_B1 ** ADAM_STEP)
    v_hat = v2 / (1.0 - ADAM_B2 ** ADAM_STEP)
    delta = -ADAM_LR * (m_hat / (jnp.sqrt(v_hat) + ADAM_EPS) + ADAM_WD * w)
    return delta, m2, v2


ADAMW_BLOCK_BYTES = 3 * 2 ** 19


def _adamw_big(ws, gs, ms, vs, tag):
    n = len(ws)

    def fits(steps):
        return all(w.shape[0] % steps == 0 and (w.shape[0] // steps) % SUBLANES == 0
                   and (w.shape[0] // steps) * w.shape[1] * 4 * n <= ADAMW_BLOCK_BYTES for w in ws)

    steps = next(s for s in range(1, min(w.shape[0] for w in ws) + 1) if fits(s))
    specs = [pl.BlockSpec((w.shape[0] // steps, w.shape[1]), lambda i: (i, 0)) for w in ws]

    def body(*refs):
        ins, outs = refs[:4 * n], refs[4 * n:]
        for i in range(n):
            w_ref, g_ref, m_ref, v_ref = ins[i], ins[n + i], ins[2 * n + i], ins[3 * n + i]
            gg = g_ref[...]
            outs[4 * i][...] = gg
            outs[4 * i + 1][...], outs[4 * i + 2][...], outs[4 * i + 3][...] = _adamw_math(
                w_ref[...], gg, m_ref[...], v_ref[...])

    outs = pl.pallas_call(
        body, name="adamw_" + tag, grid=(steps,),
        in_specs=specs * 4, out_specs=[s for s in specs for _ in range(4)],
        out_shape=[_sds(w.shape, F32) for w in ws for _ in range(4)],
        compiler_params=_params(("arbitrary",)),
    )(*ws, *gs, *ms, *vs)
    return [outs[4 * i:4 * i + 4] for i in range(n)]


SMALL_ROWS = 40
PACK_ROWS = 64


def kernel(x, meta_tokens, pre_mix_norm, w_in, conv_a_w, conv_b_w, conv_b_bias, ln_b_gain, ln_b_bias, w_out, post_mix_norm, pre_ffn_norm, w_gate, w_up, w_down, post_ffn_norm, loss_target, m_meta_tokens, m_pre_mix_norm, m_w_in, m_conv_a_w, m_conv_b_w, m_conv_b_bias, m_ln_b_gain, m_ln_b_bias, m_w_out, m_post_mix_norm, m_pre_ffn_norm, m_w_gate, m_w_up, m_w_down, m_post_ffn_norm, v_meta_tokens, v_pre_mix_norm, v_w_in, v_conv_a_w, v_conv_b_w, v_conv_b_bias, v_ln_b_gain, v_ln_b_bias, v_w_out, v_post_mix_norm, v_pre_ffn_norm, v_w_gate, v_w_up, v_w_down, v_post_ffn_norm):
    xq, yq, cq = lax.axis_index("x"), lax.axis_index("y"), lax.axis_index("c")
    chip = 2 * xq + yq
    c_arr = jnp.reshape(cq, (1,)).astype(jnp.int32)
    qc_arr = jnp.stack([chip, cq]).astype(jnp.int32)

    seq, d = x.shape[1], x.shape[2]
    x2, tgt2 = x[0], loss_target[0]
    tr = lambda a: jnp.swapaxes(a, 1, 2)[0]
    w_in2, w_out2, w_gate2, w_up2, w_down2 = w_in[0], w_out[0], tr(w_gate), tr(w_up), w_down[0]
    ka, wa_sh = conv_a_w.shape[1], conv_a_w.shape[2]
    kb = conv_b_w.shape[1]
    meta_sh = meta_tokens.shape[1]

    small = jnp.zeros((PACK_ROWS, meta_sh), F32)
    small = small.at[0:N_META, :].set(meta_tokens)
    small = small.at[16:16 + ka, 0:wa_sh].set(conv_a_w[0])
    small = small.at[24:24 + kb, 0:wa_sh].set(conv_b_w[0])
    q_arr = jnp.reshape(chip, (1,)).astype(jnp.int32)
    small_own = lax.dynamic_update_slice(jnp.zeros((N_CHIPS, PACK_ROWS, meta_sh), F32), small[None], (chip, 0, 0))
    i_ssem, i_rsem, first, i_token = _gather_start(_cast_place([w_in2], q_arr, "w_in") + [small_own], pre_mix_norm, "in")
    rest = _cast_place([w_out2, w_gate2, w_up2, w_down2], q_arr, "rest", i_token)
    g_ssem, g_rsem, lands, g_token = _gather_start(rest, i_token, "rest")
    win4, small4 = _forward_pair(_gather_wait([0, 1], i_ssem, i_rsem, first, g_token, "in"), "in")
    meta_f = jnp.concatenate([small4[k, 0:N_META, :] for k in range(N_CHIPS)], axis=1)
    wa_f = jnp.concatenate([small4[k, 16:16 + ka, 0:wa_sh] for k in range(N_CHIPS)], axis=1)
    wb_f = jnp.concatenate([small4[k, 24:24 + kb, 0:wa_sh] for k in range(N_CHIPS)], axis=1)

    tm = _row_tile(seq + TAIL_ROWS)
    tail = lax.dynamic_update_slice(jnp.zeros((tm, d), F32), meta_f, (seq % tm, 0))
    h, xn1, hp5 = _mm_in(x2, tail, win4, pre_mix_norm, g_token)
    ya, z = _mix_conv_fwd(hp5, wa_f, wb_f, conv_b_bias)
    (wout4,) = _forward_pair(_gather_wait([0], g_ssem, g_rsem, lands[0:1], z, "out"), "out")
    wout_f = wout4.reshape(N_CHIPS * wout4.shape[1], wout4.shape[2])
    yb, mix, h1, xn2 = _mm_out(ya, z, h, wout_f, ln_b_gain, ln_b_bias, post_mix_norm, pre_ffn_norm)
    wg4, wu4 = _forward_pair(_gather_wait([1, 2], g_ssem, g_rsem, lands[1:3], xn2, "gate_up"), "gate_up")
    stacked = lambda a: a.reshape(a.shape[0] * a.shape[1], a.shape[2])
    wg_f, wu_f = stacked(wg4), stacked(wu4)
    p_act, q_act, f_act = _ffn_up(xn2, wg_f, wu_f)
    (wd4,) = _forward_pair(_gather_wait([3], g_ssem, g_rsem, lands[3:4], f_act, "down"), "down")
    wd_f = stacked(wd4)
    dff, dh2, loss_blk, d_gpf = _ffn_down(f_act, wd_f, h1, tgt2, post_ffn_norm)

    da, du = _ffn_bwd_act(dff, wd_f, p_act, q_act)
    by_chip = lambda g: g.reshape(N_CHIPS, g.shape[0] // N_CHIPS, g.shape[1])
    g_down = by_chip(_grad_w_down(f_act, dff))
    g_gate, g_up = [by_chip(g) for g in _grad_w_gate_up(xn2, da, du)]
    ffn = [g_gate, g_up, g_down]
    p_ssem, p_rsem, ffn, p_lands, p_token = _pair_exchange_start(ffn, [1, 1, 1], dff, "ffn")
    dh1, dmix, d_g2, d_gpm = _ffn_bwd_in(da, du, wg_f, wu_f, h1, mix, dh2, pre_ffn_norm, post_mix_norm, p_token)
    ffn, got = _pair_exchange_wait(p_ssem, p_rsem, ffn, p_lands, [1, 1, 1], d_g2, "ffn")
    parts = _pair_sum(ffn, got, c_arr, [False] * 3, "ffn")
    f_ssem, f_rsem, parts, f_lands, f_token = _chip_exchange_start(parts, dff, "ffn")
    g_out = _grad_w_out(ya, yb, dmix, f_token)
    dya, dz, d_lg, d_lb = _mix_bwd_out(dmix, wout_f, z, ln_b_gain, ln_b_bias, f_token)
    dhp5, d_wa, d_wb, d_bb = _mix_conv_bwd(hp5, dya, dz, wa_f, wb_f)
    g_in = _grad_w_in(xn1, dhp5)

    g_out4 = g_out.reshape(N_CHIPS, g_out.shape[0] // N_CHIPS, g_out.shape[1])
    mixw = [g_in, g_out4]
    got2 = _pair_exchange_grads(mixw, [0, 1], "mix")
    parts2 = _pair_sum(mixw, got2, c_arr, [True, False], "mix")
    m_ssem, m_rsem, parts2, m_lands, m_token = _chip_exchange_start(parts2, dhp5, "mix")
    grad_x2, d_meta, d_g1 = _mix_bwd_in(dhp5, win4, h, dh1, pre_mix_norm, m_token)
    grad_x = grad_x2[None]

    parts, f_recv = _chip_exchange_wait(f_ssem, f_rsem, parts, f_lands, d_g1, "ffn")
    halves = _chip_sum(parts, f_recv, qc_arr, "ffn")
    gsum_ffn = _pair_share_grads(halves, "ffn")

    names_big = ["w_in", "w_out", "w_gate", "w_up", "w_down"]
    w_big = dict(zip(names_big, [w_in2, w_out2, w_gate2, w_up2, w_down2]))
    m_big = dict(zip(names_big, [m_w_in[0], m_w_out[0], tr(m_w_gate), tr(m_w_up), m_w_down[0]]))
    v_big = dict(zip(names_big, [v_w_in[0], v_w_out[0], tr(v_w_gate), tr(v_w_up), v_w_down[0]]))
    grads, deltas, new_m, new_v = {}, {}, {}, {}

    def update(names, gs, tag):
        res = _adamw_big([w_big[k] for k in names], gs, [m_big[k] for k in names], [v_big[k] for k in names], tag)
        for nm, outs in zip(names, res):
            if nm in ("w_gate", "w_up"):
                outs = [jnp.swapaxes(o[None], 1, 2) for o in outs]
            else:
                outs = [o[None] for o in outs]
            grads[nm], deltas[nm], new_m[nm], new_v[nm] = outs
        return res[-1][1]

    last = update(["w_gate", "w_up", "w_down"], list(gsum_ffn), "ffn")

    hw = d // 2
    assert d_wa.shape == (3, hw) and d_wb.shape == (31, hw) and d_bb.shape == (1, hw)
    small_parts = [d_meta, d_g1, d_gpm, d_g2, d_gpf, d_bb, d_lg, d_lb, loss_blk[0:1, :], d_wa, d_wb]
    places = [[(0, 0, 0, N_META)], [(16, 0, 0, 1)], [(17, 0, 0, 1)], [(18, 0, 0, 1)], [(19, 0, 0, 1)],
              [(20, 0, 0, 1)], [(20, hw, 0, 1)], [(21, 0, 0, 1)], [(21, hw, 0, 1)], [(22, 0, 0, 3)],
              [(22, hw, 0, 3), (25, 0, 3, 14), (25, hw, 17, 14)]]
    names_small = ["meta_tokens", "pre_mix_norm", "conv_a_w", "conv_b_w", "conv_b_bias", "ln_b_gain", "ln_b_bias",
                   "post_mix_norm", "pre_ffn_norm", "post_ffn_norm"]
    takes = [[(0, N_META, 0, 0, True)], [(0, 1, 16, 0, False)], [(0, 3, 22, 0, True)],
             [(0, 3, 22, hw, True), (3, 14, 25, 0, True), (17, 14, 25, hw, True)], [(0, 1, 20, 0, False)],
             [(0, 1, 20, hw, False)], [(0, 1, 21, 0, False)], [(0, 1, 17, 0, False)], [(0, 1, 18, 0, False)],
             [(0, 1, 19, 0, False)]]
    taps = lambda a: jnp.swapaxes(a, 0, 1)
    w_small = [meta_tokens, pre_mix_norm, taps(conv_a_w), taps(conv_b_w), conv_b_bias, ln_b_gain, ln_b_bias, post_mix_norm,
               pre_ffn_norm, post_ffn_norm]
    m_small = [m_meta_tokens, m_pre_mix_norm, taps(m_conv_a_w), taps(m_conv_b_w), m_conv_b_bias, m_ln_b_gain, m_ln_b_bias,
               m_post_mix_norm, m_pre_ffn_norm, m_post_ffn_norm]
    v_small = [v_meta_tokens, v_pre_mix_norm, taps(v_conv_a_w), taps(v_conv_b_w), v_conv_b_bias, v_ln_b_gain, v_ln_b_bias,
               v_post_mix_norm, v_pre_ffn_norm, v_post_ffn_norm]
    red = _small_allreduce(small_parts, places, SMALL_ROWS, d, last)
    g_s, d_s, m_s, v_s, loss_row = _small_update(red, q_arr, takes, (21, hw), w_small, m_small, v_small)
    loss = loss_row[0, 0]
    for i, nm in enumerate(names_small):
        fix = taps if nm in ("conv_a_w", "conv_b_w") else (lambda a: a)
        grads[nm], deltas[nm], new_m[nm], new_v[nm] = fix(g_s[i]), fix(d_s[i]), fix(m_s[i]), fix(v_s[i])

    parts2, m_recv = _chip_exchange_wait(m_ssem, m_rsem, parts2, m_lands, loss_row, "mix")
    halves2 = _chip_sum(parts2, m_recv, qc_arr, "mix")
    gsum_mix = _pair_share_grads(halves2, "mix")
    update(["w_in", "w_out"], list(gsum_mix), "mix")

    order = ["meta_tokens", "pre_mix_norm", "w_in", "conv_a_w", "conv_b_w", "conv_b_bias", "ln_b_gain", "ln_b_bias", "w_out",
             "post_mix_norm", "pre_ffn_norm", "w_gate", "w_up", "w_down", "post_ffn_norm"]
    return (loss, grad_x, *[grads[k] for k in order], *[deltas[k] for k in order], *[new_m[k] for k in order],
            *[new_v[k] for k in order])
```

```python
import jax
import jax.numpy as jnp
from jax import lax
from jax.experimental import pallas as pl
from jax.experimental.pallas import tpu as pltpu

F32 = jnp.float32
BF16 = jnp.bfloat16
MESH = pl.DeviceIdType.MESH

N_META = 16
TAIL_ROWS = 128
RMS_EPS = 1e-6
LN_EPS = 1e-5
ADAM_LR = 0.001
ADAM_B1 = 0.9
ADAM_B2 = 0.999
ADAM_EPS = 1e-08
ADAM_WD = 0.01
ADAM_STEP = 10

N_CHIPS = 4
LANES = 128
SUBLANES = 8
BF16_ROWS = 16
MXU_TILE = 256
CONV_CHUNK = 48
CONV_HIST = 32
ROW_TILE_CAP = 640
VMEM_LIMIT = 56 * 1024 * 1024

NN = (((1,), (0,)), ((), ()))
NT = (((1,), (1,)), ((), ()))
TN = (((0,), (0,)), ((), ()))


def _dot(a, b, dims=NN):
    return lax.dot_general(a, b, dims, preferred_element_type=F32)


def _sig(v):
    return 1.0 / (1.0 + jnp.exp(-v))


def _mean(v):
    return jnp.mean(v, axis=-1, keepdims=True)


def _row_tile(rows):
    best = BF16_ROWS
    for t in range(BF16_ROWS, min(rows, ROW_TILE_CAP) + 1, BF16_ROWS):
        if rows % t == 0:
            best = t
    assert rows % best == 0
    return best


def _row_parts(tm):
    if tm % (2 * BF16_ROWS):
        return [slice(0, tm)]
    return [slice(0, tm // 2), slice(tm // 2, tm)]


def _concat_shards(w_ref, wcat_ref):
    n_sh, _, csh = w_ref.shape

    @pl.when(pl.program_id(0) == 0)
    def _():
        for k in range(n_sh):
            wcat_ref[:, k * csh:(k + 1) * csh] = w_ref[k]


def _stream_shards(piece, n_pieces, w_hbm, wcat_ref, sem):
    n_sh, _, csh = w_hbm.shape
    pw = n_sh * csh // n_pieces
    copies = [pltpu.make_async_copy(w_hbm.at[k], wcat_ref.at[:, pl.ds(k * csh, csh)], sem.at[k]) for k in range(n_sh)]

    @pl.when(pl.program_id(0) == 0)
    def _():
        for cp in copies:
            cp.start()
        landed = 0
        for p in range(n_pieces):
            while landed < n_sh and landed * csh < (p + 1) * pw:
                copies[landed].wait()
                landed += 1
            piece(p)

    @pl.when(pl.program_id(0) > 0)
    def _():
        for p in range(n_pieces):
            piece(p)


def _params(semantics=None):
    kw = dict(vmem_limit_bytes=VMEM_LIMIT)
    if semantics is not None:
        kw["dimension_semantics"] = semantics
    return pltpu.CompilerParams(**kw)


def _full(shape):
    nd = len(shape)
    return pl.BlockSpec(shape, lambda *_: (0,) * nd)


def _resident(shape):
    nd = len(shape)
    return pl.BlockSpec(shape, lambda *_: (0,) * nd, pipeline_mode=pl.Buffered(1))


def _sds(shape, dtype):
    return jax.ShapeDtypeStruct(shape, dtype)


ANY = pl.BlockSpec(memory_space=pl.ANY)
VMEM = pl.BlockSpec(memory_space=pltpu.VMEM)


def _mesh_pos():
    return lax.axis_index("x"), lax.axis_index("y"), lax.axis_index("c")


def _flip(v, bit):
    return 1 - v if bit else v


def _mm_in(x, tail, win4, g1, after):
    seq, d = x.shape
    tp = seq + TAIL_ROWS
    tm = _row_tile(tp)
    n_sh, _, csh = win4.shape
    pw = n_sh * csh // 5

    def body(x_ref, tail_ref, w_hbm, g_ref, after_ref, h_ref, xn_ref, hp_ref, wcat_ref, wsem):
        rows = pl.program_id(0) * tm + lax.broadcasted_iota(jnp.int32, (tm, 1), 0)
        hh = jnp.where(rows < seq, x_ref[...], tail_ref[...])
        h_ref[...] = hh
        r = lax.rsqrt(_mean(hh * hh) + RMS_EPS)
        xn_ref[...] = (hh * r * g_ref[...]).astype(BF16)

        def piece(p):
            hp_ref[p] = _dot(xn_ref[...], wcat_ref[:, p * pw:(p + 1) * pw])

        _stream_shards(piece, 5, w_hbm, wcat_ref, wsem)

    row = pl.BlockSpec((tm, d), lambda i: (i, 0))
    return pl.pallas_call(
        body, name="mm_in", grid=(tp // tm,),
        in_specs=[row, _full(tail.shape), ANY, _full(g1.shape), ANY],
        out_specs=[row, row, pl.BlockSpec((5, tm, pw), lambda i: (0, i, 0))],
        out_shape=[_sds((tp, d), F32), _sds((tp, d), BF16), _sds((5, tp, pw), F32)],
        scratch_shapes=[pltpu.VMEM((d, n_sh * csh), BF16), pltpu.SemaphoreType.DMA((n_sh,))],
        compiler_params=_params(("arbitrary",)),
    )(x, tail, win4, g1, after)


def _seq_rows(tp):
    seq = tp - TAIL_ROWS
    nseq = seq + N_META
    assert nseq % CONV_CHUNK == 0 and seq % BF16_ROWS == 0
    return seq, nseq


def _conv_offsets(width, transpose):
    return [(width - 1 - k) if transpose else (CONV_HIST - (width - 1) + k) for k in range(width)]


def _shift_copies(src_ref, sh_ref, width, transpose):
    n = src_ref.shape[0] - SUBLANES
    for s in sorted({o % SUBLANES for o in _conv_offsets(width, transpose)} - {0}):
        sh_ref[s - 1, 0:n, :] = src_ref[s:s + n, :]


def _tap_rows(src_ref, sh_ref, base, off):
    start = pl.multiple_of(base + (off // SUBLANES) * SUBLANES, SUBLANES)
    if off % SUBLANES == 0:
        return src_ref[pl.ds(start, CONV_CHUNK), :]
    return sh_ref[off % SUBLANES - 1, pl.ds(start, CONV_CHUNK), :]


def _conv_taps(src_ref, sh_ref, w_ref, dst_ref, width, nseq, transpose):
    w = w_ref[...]
    offs = _conv_offsets(width, transpose)
    _shift_copies(src_ref, sh_ref, width, transpose)

    def step(n, carry):
        out0 = pl.multiple_of(CONV_HIST + n * CONV_CHUNK, SUBLANES)
        base = out0 if transpose else n * CONV_CHUNK
        acc = jnp.zeros((CONV_CHUNK, w.shape[1]), F32)
        for k, off in enumerate(offs):
            acc = acc + w[k:k + 1, :] * _tap_rows(src_ref, sh_ref, base, off)
        dst_ref[pl.ds(out0, CONV_CHUNK), :] = acc
        return carry

    lax.fori_loop(0, nseq // CONV_CHUNK, step, 0)


def _conv_wgrad(src_ref, sh_ref, dz_ref, acc_ref, width, nseq):
    acc_ref[...] = jnp.zeros(acc_ref.shape, F32)
    offs = _conv_offsets(width, False)

    def step(n, carry):
        dzc = dz_ref[pl.ds(pl.multiple_of(CONV_HIST + n * CONV_CHUNK, SUBLANES), CONV_CHUNK), :]
        for k, off in enumerate(offs):
            prod = dzc * _tap_rows(src_ref, sh_ref, n * CONV_CHUNK, off)
            part = prod[0:SUBLANES, :]
            for s in range(1, CONV_CHUNK // SUBLANES):
                part = part + prod[SUBLANES * s:SUBLANES * (s + 1), :]
            acc_ref[SUBLANES * k:SUBLANES * (k + 1), :] += part
        return carry

    lax.fori_loop(0, nseq // CONV_CHUNK, step, 0)


def _to_seq(buf_ref, x_part, meta_part, seq):
    buf_ref[CONV_HIST:CONV_HIST + N_META, :] = meta_part
    buf_ref[CONV_HIST + N_META:CONV_HIST + N_META + seq, :] = x_part


def _zero_ends(buf_ref, nseq):
    zeros = jnp.zeros((CONV_HIST, buf_ref.shape[1]), F32)
    buf_ref[0:CONV_HIST, :] = zeros
    buf_ref[CONV_HIST + nseq:CONV_HIST + nseq + CONV_HIST, :] = zeros


def _mix_conv_fwd(hp5, wa, wb, bb):
    _, tp, wgrp = hp5.shape
    seq, nseq = _seq_rows(tp)
    sb = nseq + 2 * CONV_HIST
    ka, kb = wa.shape[0], wb.shape[0]
    xs, ms = slice(0, seq), slice(seq, seq + N_META)
    ox, om = slice(CONV_HIST + N_META, CONV_HIST + nseq), slice(CONV_HIST, CONV_HIST + N_META)

    def body(hp_ref, wa_ref, wb_ref, bb_ref, ya_ref, z_ref, s_ref, o_ref, sh_ref):
        _zero_ends(s_ref, nseq)
        _to_seq(s_ref, hp_ref[1, xs, :] * hp_ref[2, xs, :], hp_ref[1, ms, :] * hp_ref[2, ms, :], seq)
        _conv_taps(s_ref, sh_ref, wa_ref, o_ref, ka, nseq, False)
        ya_ref[xs, :] = (hp_ref[0, xs, :] * o_ref[ox, :]).astype(BF16)
        ya_ref[ms, :] = (hp_ref[0, ms, :] * o_ref[om, :]).astype(BF16)
        ya_ref[seq + N_META:tp, :] = jnp.zeros((tp - seq - N_META, LANES), BF16)
        _to_seq(s_ref, hp_ref[3, xs, :] * _sig(hp_ref[4, xs, :]), hp_ref[3, ms, :] * _sig(hp_ref[4, ms, :]), seq)
        _conv_taps(s_ref, sh_ref, wb_ref, o_ref, kb, nseq, False)
        z_ref[xs, :] = o_ref[ox, :] + bb_ref[...]
        z_ref[ms, :] = o_ref[om, :] + bb_ref[...]
        z_ref[seq + N_META:tp, :] = jnp.zeros((tp - seq - N_META, LANES), F32)

    col = lambda j: (0, j)
    return pl.pallas_call(
        body, name="mix_conv_fwd", grid=(wgrp // LANES,),
        in_specs=[pl.BlockSpec((5, tp, LANES), lambda j: (0, 0, j)), pl.BlockSpec((ka, LANES), col),
                  pl.BlockSpec((kb, LANES), col), pl.BlockSpec((1, LANES), col)],
        out_specs=[pl.BlockSpec((tp, LANES), col), pl.BlockSpec((tp, LANES), col)],
        out_shape=[_sds((tp, wgrp), BF16), _sds((tp, wgrp), F32)],
        scratch_shapes=[pltpu.VMEM((sb, LANES), F32), pltpu.VMEM((sb, LANES), F32),
                        pltpu.VMEM((SUBLANES - 1, sb, LANES), F32)],
        compiler_params=_params(("arbitrary",)),
    )(hp5, wa, wb, bb)


def _layer_norm_parts(z, lg, lb):
    mu = _mean(z)
    zc = z - mu
    rl = lax.rsqrt(_mean(zc * zc) + LN_EPS)
    zh = zc * rl
    return rl, zh, zh * lg + lb


def _mm_out(ya, z, h, wout, lg, lb, gpm, g2):
    tp, d = h.shape
    wa_ = ya.shape[1]
    tm = _row_tile(tp)

    def body(ya_ref, z_ref, h_ref, w_ref, lg_ref, lb_ref, gpm_ref, g2_ref, yb_ref, mix_ref, h1_ref, xn2_ref):
        for rs in _row_parts(tm):
            _, _, l = _layer_norm_parts(z_ref[rs, :], lg_ref[...], lb_ref[...])
            yb = (l * _sig(l)).astype(BF16)
            yb_ref[rs, :] = yb
            mix = _dot(ya_ref[rs, :], w_ref[0:wa_, :]) + _dot(yb, w_ref[wa_:d, :])
            mix_ref[rs, :] = mix
            rm = lax.rsqrt(_mean(mix * mix) + RMS_EPS)
            h1 = h_ref[rs, :] + mix * rm * gpm_ref[...]
            h1_ref[rs, :] = h1
            r2 = lax.rsqrt(_mean(h1 * h1) + RMS_EPS)
            xn2_ref[rs, :] = (h1 * r2 * g2_ref[...]).astype(BF16)

    row = lambda i: (i, 0)
    return pl.pallas_call(
        body, name="mm_out", grid=(tp // tm,),
        in_specs=[pl.BlockSpec((tm, wa_), row), pl.BlockSpec((tm, wa_), row), pl.BlockSpec((tm, d), row),
                  _resident(wout.shape), _full(lg.shape), _full(lb.shape), _full(gpm.shape), _full(g2.shape)],
        out_specs=[pl.BlockSpec((tm, wa_), row), pl.BlockSpec((tm, d), row), pl.BlockSpec((tm, d), row),
                   pl.BlockSpec((tm, d), row)],
        out_shape=[_sds((tp, wa_), BF16), _sds((tp, d), F32), _sds((tp, d), F32), _sds((tp, d), BF16)],
        compiler_params=_params(("arbitrary",)),
    )(ya, z, h, wout, lg, lb, gpm, g2)


def _stream_pieces(piece, n_pieces, weights):
    def copies(j):
        rows = pl.ds(j * MXU_TILE, MXU_TILE)
        return [pltpu.make_async_copy(hbm.at[rows, :], vmem.at[rows, :], sem.at[j]) for hbm, vmem, sem in weights]

    @pl.when(pl.program_id(0) == 0)
    def _():
        for j in range(n_pieces):
            for cp in copies(j):
                cp.start()
        for j in range(n_pieces):
            for cp in copies(j):
                cp.wait()
            piece(j)

    @pl.when(pl.program_id(0) > 0)
    def _():
        for j in range(n_pieces):
            piece(j)


def _ffn_up(xn2, wg, wu):
    tp, d = xn2.shape
    ff_dim = wg.shape[0]
    tm = _row_tile(tp)
    assert ff_dim % MXU_TILE == 0
    n_pieces = ff_dim // MXU_TILE

    def body(xn_ref, wg_hbm, wu_hbm, p_ref, q_ref, f_ref, wg_ref, wu_ref, gsem, usem):
        def piece(j):
            cols = slice(j * MXU_TILE, (j + 1) * MXU_TILE)
            xn = xn_ref[...]
            a = _dot(xn, wg_ref[cols, :], NT)
            u = _dot(xn, wu_ref[cols, :], NT)
            s = _sig(a)
            q = a * s
            p_ref[:, cols] = (u * (s + q * (1.0 - s))).astype(BF16)
            q_ref[:, cols] = q.astype(BF16)
            f_ref[:, cols] = (q * u).astype(BF16)

        _stream_pieces(piece, n_pieces, [(wg_hbm, wg_ref, gsem), (wu_hbm, wu_ref, usem)])

    ospec = pl.BlockSpec((tm, ff_dim), lambda i: (i, 0))
    return pl.pallas_call(
        body, name="ffn_up", grid=(tp // tm,),
        in_specs=[pl.BlockSpec((tm, d), lambda i: (i, 0)), ANY, ANY],
        out_specs=[ospec, ospec, ospec],
        out_shape=[_sds((tp, ff_dim), BF16)] * 3,
        scratch_shapes=[pltpu.VMEM(wg.shape, BF16), pltpu.VMEM(wu.shape, BF16),
                        pltpu.SemaphoreType.DMA((n_pieces,)), pltpu.SemaphoreType.DMA((n_pieces,))],
        compiler_params=_params(("arbitrary",)),
    )(xn2, wg, wu)


def _ffn_down(f, wd, h1, tgt, gpf):
    tp, ff_dim = f.shape
    d = h1.shape[1]
    tm = _row_tile(tp)
    seq, _ = _seq_rows(tp)

    def body(f_ref, w_ref, h1_ref, t_ref, gpf_ref, dff_ref, dh2_ref, loss_ref, dgpf_ref):
        i = pl.program_id(0)
        gpf_ = gpf_ref[...]

        @pl.when(i == 0)
        def _():
            loss_ref[...] = jnp.zeros(loss_ref.shape, F32)
            dgpf_ref[...] = jnp.zeros(dgpf_ref.shape, F32)

        for rs in _row_parts(tm):
            ff = _dot(f_ref[rs, :], w_ref[...])
            rf = lax.rsqrt(_mean(ff * ff) + RMS_EPS)
            nf = ff * rf
            h2 = h1_ref[rs, :] + nf * gpf_
            rows = i * tm + rs.start + lax.broadcasted_iota(jnp.int32, (rs.stop - rs.start, 1), 0)
            err = jnp.where(rows < seq, h2 - t_ref[rs, :], 0.0)
            dh2 = err * (1.0 / d)
            dh2_ref[rs, :] = dh2
            dn = dh2 * gpf_
            dff_ref[rs, :] = (rf * (dn - nf * _mean(dn * nf))).astype(BF16)
            loss_ref[...] += (0.5 / d) * jnp.sum(err * err, axis=(0, 1), keepdims=True)
            dgpf_ref[...] += jnp.sum(dh2 * nf, axis=0, keepdims=True)

    row = lambda i: (i, 0)
    return pl.pallas_call(
        body, name="ffn_down", grid=(tp // tm,),
        in_specs=[pl.BlockSpec((tm, ff_dim), row), _resident(wd.shape), pl.BlockSpec((tm, d), row),
                  pl.BlockSpec((tm, d), row), _full(gpf.shape)],
        out_specs=[pl.BlockSpec((tm, d), row), pl.BlockSpec((tm, d), row), _full((SUBLANES, LANES)), _full((1, d))],
        out_shape=[_sds((tp, d), BF16), _sds((tp, d), F32), _sds((SUBLANES, LANES), F32), _sds((1, d), F32)],
        compiler_params=_params(("arbitrary",)),
    )(f, wd, h1, tgt, gpf)


def _ffn_bwd_act(dff, wd, p, q):
    tp, d = dff.shape
    ff_dim = wd.shape[0]
    tm = _row_tile(tp)
    n_pieces = ff_dim // MXU_TILE

    def body(dff_ref, w_hbm, p_ref, q_ref, da_ref, du_ref, w_ref, wsem):
        def piece(j):
            cols = slice(j * MXU_TILE, (j + 1) * MXU_TILE)
            df = _dot(dff_ref[...], w_ref[cols, :], NT).astype(BF16)
            da_ref[:, cols] = df * p_ref[:, cols]
            du_ref[:, cols] = df * q_ref[:, cols]

        _stream_pieces(piece, n_pieces, [(w_hbm, w_ref, wsem)])

    aspec = pl.BlockSpec((tm, ff_dim), lambda i: (i, 0))
    return pl.pallas_call(
        body, name="ffn_bwd_act", grid=(tp // tm,),
        in_specs=[pl.BlockSpec((tm, d), lambda i: (i, 0)), ANY, aspec, aspec],
        out_specs=[aspec, aspec],
        out_shape=[_sds((tp, ff_dim), BF16)] * 2,
        scratch_shapes=[pltpu.VMEM(wd.shape, BF16), pltpu.SemaphoreType.DMA((n_pieces,))],
        compiler_params=_params(("arbitrary",)),
    )(dff, wd, p, q)


def _grad_blocks(ff_dim):
    rows = ff_dim // 2
    assert rows % LANES == 0
    return rows


def _grad_w_down(f, dff):
    tp, ff_dim = f.shape
    d = dff.shape[1]
    rows = _grad_blocks(ff_dim)

    def body(f_ref, dff_ref, g_ref):
        g_ref[...] = _dot(f_ref[...], dff_ref[...], TN).astype(BF16)

    return pl.pallas_call(
        body, name="grad_w_down", grid=(ff_dim // rows,),
        in_specs=[pl.BlockSpec((tp, rows), lambda k: (0, k)), _resident(dff.shape)],
        out_specs=pl.BlockSpec((rows, d), lambda k: (k, 0)),
        out_shape=_sds((ff_dim, d), BF16),
        compiler_params=_params(("arbitrary",)),
    )(f, dff)


def _grad_w_gate_up(xn2, da, du):
    tp, ff_dim = da.shape
    d = xn2.shape[1]
    rows = _grad_blocks(ff_dim)

    def body(xn_ref, da_ref, du_ref, gg_ref, gu_ref):
        xn = xn_ref[...]
        gg_ref[...] = _dot(da_ref[...], xn, TN).astype(BF16)
        gu_ref[...] = _dot(du_ref[...], xn, TN).astype(BF16)

    aspec = pl.BlockSpec((tp, rows), lambda k: (0, k))
    gspec = pl.BlockSpec((rows, d), lambda k: (k, 0))
    return pl.pallas_call(
        body, name="grad_w_gate_up", grid=(ff_dim // rows,),
        in_specs=[_resident(xn2.shape), aspec, aspec],
        out_specs=[gspec, gspec],
        out_shape=[_sds((ff_dim, d), BF16)] * 2,
        compiler_params=_params(("arbitrary",)),
    )(xn2, da, du)


def _rms_bwd(dy, x, r, g):
    n = x * r
    dn = dy * g
    return r * (dn - n * _mean(dn * n)), dy * n


def _ffn_bwd_in(da, du, wg, wu, h1, mix, dh2, g2, gpm, after):
    tp, ff_dim = da.shape
    d = h1.shape[1]
    tm = _row_tile(tp)

    def body(da_ref, du_ref, wg_ref, wu_ref, h1_ref, mix_ref, dh2_ref, g2_ref, gpm_ref, after_ref,
             dh1_ref, dmix_ref, dg2_ref, dgpm_ref):
        i = pl.program_id(0)

        @pl.when(i == 0)
        def _():
            dg2_ref[...] = jnp.zeros(dg2_ref.shape, F32)
            dgpm_ref[...] = jnp.zeros(dgpm_ref.shape, F32)

        for rs in _row_parts(tm):
            dxn = _dot(da_ref[rs, :], wg_ref[...]) + _dot(du_ref[rs, :], wu_ref[...])
            h1v = h1_ref[rs, :]
            r2 = lax.rsqrt(_mean(h1v * h1v) + RMS_EPS)
            dres, dg2_rows = _rms_bwd(dxn, h1v, r2, g2_ref[...])
            dh1 = dh2_ref[rs, :] + dres
            dh1_ref[rs, :] = dh1
            mixv = mix_ref[rs, :]
            rm = lax.rsqrt(_mean(mixv * mixv) + RMS_EPS)
            dmix, dgpm_rows = _rms_bwd(dh1, mixv, rm, gpm_ref[...])
            dmix_ref[rs, :] = dmix.astype(BF16)
            dg2_ref[...] += jnp.sum(dg2_rows, axis=0, keepdims=True)
            dgpm_ref[...] += jnp.sum(dgpm_rows, axis=0, keepdims=True)

    aspec = pl.BlockSpec((tm, ff_dim), lambda i: (i, 0))
    row = pl.BlockSpec((tm, d), lambda i: (i, 0))
    return pl.pallas_call(
        body, name="ffn_bwd_in", grid=(tp // tm,),
        in_specs=[aspec, aspec, _resident(wg.shape), _resident(wu.shape), row, row, row, _full(g2.shape), _full(gpm.shape),
                  ANY],
        out_specs=[row, row, _full((1, d)), _full((1, d))],
        out_shape=[_sds((tp, d), F32), _sds((tp, d), BF16), _sds((1, d), F32), _sds((1, d), F32)],
        compiler_params=_params(("arbitrary",)),
    )(da, du, wg, wu, h1, mix, dh2, g2, gpm, after)


def _grad_w_out(ya, yb, dmix, after):
    tp, wa_ = ya.shape
    d = dmix.shape[1]

    def body(ya_ref, yb_ref, dmix_ref, after_ref, g_ref):
        dm = dmix_ref[...]
        g_ref[0:wa_, :] = _dot(ya_ref[...], dm, TN).astype(BF16)
        g_ref[wa_:2 * wa_, :] = _dot(yb_ref[...], dm, TN).astype(BF16)

    return pl.pallas_call(
        body, name="grad_w_out", grid=(1,),
        in_specs=[_full(ya.shape), _full(yb.shape), _full(dmix.shape), ANY],
        out_specs=_full((2 * wa_, d)),
        out_shape=_sds((2 * wa_, d), BF16),
        compiler_params=_params(("arbitrary",)),
    )(ya, yb, dmix, after)


def _mix_bwd_out(dmix, wout, z, lg, lb, after):
    tp, d = dmix.shape
    wa_ = z.shape[1]
    tm = _row_tile(tp)

    def body(dmix_ref, w_ref, z_ref, lg_ref, lb_ref, after_ref, dya_ref, dz_ref, dlg_ref, dlb_ref):
        i = pl.program_id(0)
        lg_ = lg_ref[...]

        @pl.when(i == 0)
        def _():
            dlg_ref[...] = jnp.zeros(dlg_ref.shape, F32)
            dlb_ref[...] = jnp.zeros(dlb_ref.shape, F32)

        for rs in _row_parts(tm):
            dm = dmix_ref[rs, :]
            dya_ref[rs, :] = _dot(dm, w_ref[0:wa_, :], NT)
            dyb = _dot(dm, w_ref[wa_:d, :], NT)
            rl, zh, l = _layer_norm_parts(z_ref[rs, :], lg_, lb_ref[...])
            sl = _sig(l)
            dl = dyb * (sl * (1.0 + l * (1.0 - sl)))
            dzh = dl * lg_
            dz_ref[rs, :] = rl * (dzh - _mean(dzh) - zh * _mean(dzh * zh))
            dlg_ref[...] += jnp.sum(dl * zh, axis=0, keepdims=True)
            dlb_ref[...] += jnp.sum(dl, axis=0, keepdims=True)

    row = lambda i: (i, 0)
    return pl.pallas_call(
        body, name="mix_bwd_out", grid=(tp // tm,),
        in_specs=[pl.BlockSpec((tm, d), row), _resident(wout.shape), pl.BlockSpec((tm, wa_), row), _full(lg.shape),
                  _full(lb.shape), ANY],
        out_specs=[pl.BlockSpec((tm, wa_), row), pl.BlockSpec((tm, wa_), row), _full((1, wa_)), _full((1, wa_))],
        out_shape=[_sds((tp, wa_), F32), _sds((tp, wa_), F32), _sds((1, wa_), F32), _sds((1, wa_), F32)],
        compiler_params=_params(("arbitrary",)),
    )(dmix, wout, z, lg, lb, after)


def _mix_conv_bwd(hp5, dya, dz, wa, wb):
    _, tp, wgrp = hp5.shape
    seq, nseq = _seq_rows(tp)
    sb = nseq + 2 * CONV_HIST
    ka, kb = wa.shape[0], wb.shape[0]
    xs, ms = slice(0, seq), slice(seq, seq + N_META)
    ox, om = slice(CONV_HIST + N_META, CONV_HIST + nseq), slice(CONV_HIST, CONV_HIST + N_META)
    n_tail = tp - seq - N_META

    def body(hp_ref, dya_ref, dz_ref, wa_ref, wb_ref, dhp_ref, dwa_ref, dwb_ref, dbb_ref, s_ref, d_ref, o_ref, acc_ref,
             shs_ref, shd_ref):
        _zero_ends(s_ref, nseq)
        _zero_ends(d_ref, nseq)

        def put(p, ox_val, om_val):
            dhp_ref[p, xs, :] = ox_val.astype(BF16)
            dhp_ref[p, ms, :] = om_val.astype(BF16)
            dhp_ref[p, seq + N_META:tp, :] = jnp.zeros((n_tail, LANES), BF16)

        def wgrad(dw_ref, width):
            for k in range(width):
                dw_ref[k:k + 1, :] = jnp.sum(acc_ref[8 * k:8 * k + 8, :], axis=0, keepdims=True)

        _to_seq(s_ref, hp_ref[1, xs, :] * hp_ref[2, xs, :], hp_ref[1, ms, :] * hp_ref[2, ms, :], seq)
        _conv_taps(s_ref, shs_ref, wa_ref, o_ref, ka, nseq, False)
        put(0, dya_ref[xs, :] * o_ref[ox, :], dya_ref[ms, :] * o_ref[om, :])
        _to_seq(d_ref, dya_ref[xs, :] * hp_ref[0, xs, :], dya_ref[ms, :] * hp_ref[0, ms, :], seq)
        _conv_wgrad(s_ref, shs_ref, d_ref, acc_ref, ka, nseq)
        wgrad(dwa_ref, ka)
        _conv_taps(d_ref, shd_ref, wa_ref, o_ref, ka, nseq, True)
        put(1, o_ref[ox, :] * hp_ref[2, xs, :], o_ref[om, :] * hp_ref[2, ms, :])
        put(2, o_ref[ox, :] * hp_ref[1, xs, :], o_ref[om, :] * hp_ref[1, ms, :])

        _to_seq(s_ref, hp_ref[3, xs, :] * _sig(hp_ref[4, xs, :]), hp_ref[3, ms, :] * _sig(hp_ref[4, ms, :]), seq)
        _to_seq(d_ref, dz_ref[xs, :], dz_ref[ms, :], seq)
        dbb_ref[...] = (jnp.sum(dz_ref[xs, :], axis=0, keepdims=True)
                        + jnp.sum(dz_ref[ms, :], axis=0, keepdims=True))
        _shift_copies(s_ref, shs_ref, kb, False)
        _conv_wgrad(s_ref, shs_ref, d_ref, acc_ref, kb, nseq)
        wgrad(dwb_ref, kb)
        _conv_taps(d_ref, shd_ref, wb_ref, o_ref, kb, nseq, True)
        sx, sm = _sig(hp_ref[4, xs, :]), _sig(hp_ref[4, ms, :])
        put(3, o_ref[ox, :] * sx, o_ref[om, :] * sm)
        put(4, o_ref[ox, :] * hp_ref[3, xs, :] * sx * (1.0 - sx), o_ref[om, :] * hp_ref[3, ms, :] * sm * (1.0 - sm))

    col = lambda j: (0, j)
    blk5 = pl.BlockSpec((5, tp, LANES), lambda j: (0, 0, j))
    return pl.pallas_call(
        body, name="mix_conv_bwd", grid=(wgrp // LANES,),
        in_specs=[blk5, pl.BlockSpec((tp, LANES), col), pl.BlockSpec((tp, LANES), col),
                  pl.BlockSpec((ka, LANES), col), pl.BlockSpec((kb, LANES), col)],
        out_specs=[blk5, pl.BlockSpec((ka, LANES), col), pl.BlockSpec((kb, LANES), col), pl.BlockSpec((1, LANES), col)],
        out_shape=[_sds((5, tp, wgrp), BF16), _sds((ka, wgrp), F32), _sds((kb, wgrp), F32), _sds((1, wgrp), F32)],
        scratch_shapes=[pltpu.VMEM((sb, LANES), F32), pltpu.VMEM((sb, LANES), F32), pltpu.VMEM((sb, LANES), F32),
                        pltpu.VMEM((SUBLANES * kb, LANES), F32), pltpu.VMEM((SUBLANES - 1, sb, LANES), F32),
                        pltpu.VMEM((SUBLANES - 1, sb, LANES), F32)],
        compiler_params=_params(("arbitrary",)),
    )(hp5, dya, dz, wa, wb)


def _grad_w_in(xn1, dhp5):
    n_p, tp, pw = dhp5.shape
    d = xn1.shape[1]

    def body(xn_ref, dhp_ref, g_ref):
        g_ref[...] = _dot(xn_ref[...], dhp_ref[0], TN).astype(BF16)

    return pl.pallas_call(
        body, name="grad_w_in", grid=(n_p,),
        in_specs=[_resident(xn1.shape), pl.BlockSpec((1, tp, pw), lambda p: (p, 0, 0))],
        out_specs=pl.BlockSpec((d, pw), lambda p: (0, p)),
        out_shape=_sds((d, n_p * pw), BF16),
        compiler_params=_params(("arbitrary",)),
    )(xn1, dhp5)


def _mix_bwd_in(dhp5, win4, h, dh1, g1, after):
    n_p, tp, pw = dhp5.shape
    d = h.shape[1]
    n_sh, _, csh = win4.shape
    tm = _row_tile(tp)

    seq, _ = _seq_rows(tp)
    last, meta_off = seq // tm, seq % tm
    assert last == tp // tm - 1
    assert any(rs.start <= meta_off and meta_off + N_META <= rs.stop for rs in _row_parts(tm))

    def body(dhp_ref, w_ref, h_ref, dh1_ref, g_ref, after_ref, gx_ref, dmeta_ref, dg1_ref, wcat_ref):
        i = pl.program_id(0)
        _concat_shards(w_ref, wcat_ref)

        @pl.when(i == 0)
        def _():
            dg1_ref[...] = jnp.zeros(dg1_ref.shape, F32)

        for rs in _row_parts(tm):
            dxn = _dot(dhp_ref[0, rs, :], wcat_ref[:, 0:pw], NT)
            for p in range(1, n_p):
                dxn = dxn + _dot(dhp_ref[p, rs, :], wcat_ref[:, p * pw:(p + 1) * pw], NT)
            hh = h_ref[rs, :]
            r1 = lax.rsqrt(_mean(hh * hh) + RMS_EPS)
            dres, dg_rows = _rms_bwd(dxn, hh, r1, g_ref[...])
            dh = dh1_ref[rs, :] + dres
            gx_ref[rs, :] = dh
            dg1_ref[...] += jnp.sum(dg_rows, axis=0, keepdims=True)
            if rs.start <= meta_off and meta_off + N_META <= rs.stop:
                @pl.when(i == last)
                def _():
                    dmeta_ref[...] = dh[meta_off - rs.start:meta_off - rs.start + N_META, :]

    row = lambda i: (i, 0)
    return pl.pallas_call(
        body, name="mix_bwd_in", grid=(tp // tm,),
        in_specs=[pl.BlockSpec((n_p, tm, pw), lambda i: (0, i, 0)), _resident(win4.shape), pl.BlockSpec((tm, d), row),
                  pl.BlockSpec((tm, d), row), _full(g1.shape), ANY],
        out_specs=[pl.BlockSpec((tm, d), row), _full((N_META, d)), _full((1, d))],
        out_shape=[_sds((seq, d), F32), _sds((N_META, d), F32), _sds((1, d), F32)],
        scratch_shapes=[pltpu.VMEM((d, n_sh * csh), BF16)],
        compiler_params=_params(("arbitrary",)),
    )(dhp5, win4, h, dh1, g1, after)


def _other_chips(x, y):
    out = []
    for j in (1, 2, 3):
        px, py = _flip(x, j >> 1), _flip(y, j & 1)
        out.append((px, py, 2 * px + py))
    return out


PAIR_COLLECTIVE_ID = 0


def _pair_barrier(x, y, c):
    sem = pltpu.get_barrier_semaphore()
    pl.semaphore_signal(sem, inc=1, device_id=(x, y, 1 - c), device_id_type=MESH)
    pl.semaphore_wait(sem, 1)


def _pair_params():
    return pltpu.CompilerParams(collective_id=PAIR_COLLECTIVE_ID)


def _half_rows(c, rows_half):
    return pl.ds(pl.multiple_of(c * rows_half, SUBLANES), rows_half)


def _cast_place(ws, q_arr, tag, after=None):
    n = len(ws)
    extra = [] if after is None else [after]

    def fits(steps):
        return all(w.shape[0] % steps == 0 and (w.shape[0] // steps) % BF16_ROWS == 0
                   and w.shape[0] // steps <= ROW_TILE_CAP for w in ws)

    steps = next(s for s in range(1, min(w.shape[0] for w in ws) + 1) if fits(s))

    def body(q_ref, *refs):
        for w_ref, out_ref in zip(refs[:n], refs[n + len(extra):]):
            out_ref[0] = w_ref[...].astype(BF16)

    return list(pl.pallas_call(
        body, name="cast_place_" + tag,
        grid_spec=pltpu.PrefetchScalarGridSpec(
            num_scalar_prefetch=1, grid=(steps,),
            in_specs=[pl.BlockSpec((w.shape[0] // steps, w.shape[1]), lambda i, q: (i, 0)) for w in ws] + [ANY] * len(extra),
            out_specs=[pl.BlockSpec((1, w.shape[0] // steps, w.shape[1]), lambda i, q: (q[0], i, 0)) for w in ws]),
        out_shape=[_sds((N_CHIPS,) + w.shape, BF16) for w in ws],
        compiler_params=_params(("arbitrary",)),
    )(q_arr, *ws, *extra))


HBM = pl.BlockSpec(memory_space=pltpu.HBM)
SEM = pl.BlockSpec(memory_space=pltpu.SEMAPHORE)
EFFECT = pltpu.SideEffectType.DATAFLOW_SIDE_EFFECTING


def _in_hbm(a):
    return pltpu.with_memory_space_constraint(a, pltpu.HBM)


def _gather_start(fulls, after, tag):
    n = len(fulls)
    halves = [a.shape[1] // 2 for a in fulls]

    def body(*refs):
        land = refs[:n]
        ssem, rsem = refs[n + 1], refs[n + 2]
        token = refs[-1]
        x, y, c = _mesh_pos()
        q = 2 * x + y
        for i in range(n):
            for j, (px, py, _) in enumerate(_other_chips(x, y)):
                mine = land[i].at[q, _half_rows(c, halves[i]), :]
                pltpu.make_async_remote_copy(src_ref=mine, dst_ref=mine, send_sem=ssem.at[3 * i + j],
                                             recv_sem=rsem.at[3 * i + j], device_id=(px, py, c), device_id_type=MESH).start()
        token[...] = jnp.zeros(token.shape, F32)

    outs = pl.pallas_call(
        body, name="gather_start_" + tag,
        in_specs=[HBM] * n + [ANY], out_specs=[SEM, SEM] + [HBM] * n + [VMEM],
        out_shape=[pltpu.SemaphoreType.DMA((3 * n,)), pltpu.SemaphoreType.DMA((3 * n,))]
        + [pltpu.HBM(a.shape, a.dtype) for a in fulls] + [_sds((SUBLANES, LANES), F32)],
        input_output_aliases={i: 2 + i for i in range(n)},
        compiler_params=pltpu.CompilerParams(has_side_effects=EFFECT),
    )(*[_in_hbm(a) for a in fulls], after)
    return outs[0], outs[1], list(outs[2:2 + n]), outs[-1]


def _gather_wait(which, ssem, rsem, lands, after, tag):
    m = len(which)
    halves = [a.shape[1] // 2 for a in lands]

    def body(*refs):
        land = refs[:m]
        ssem_, rsem_ = refs[m], refs[m + 1]
        x, y, c = _mesh_pos()
        for t, i in enumerate(which):
            for j, (px, py, qj) in enumerate(_other_chips(x, y)):
                rows = _half_rows(c, halves[t])
                cp = pltpu.make_async_remote_copy(src_ref=land[t].at[2 * x + y, rows, :], dst_ref=land[t].at[qj, rows, :],
                                                  send_sem=ssem_.at[3 * i + j], recv_sem=rsem_.at[3 * i + j],
                                                  device_id=(px, py, c), device_id_type=MESH)
                cp.wait_send()
                cp.wait_recv()

    outs = pl.pallas_call(
        body, name="gather_wait_" + tag,
        in_specs=[HBM] * m + [SEM, SEM, ANY], out_specs=[HBM] * m,
        out_shape=[pltpu.HBM(a.shape, a.dtype) for a in lands],
        input_output_aliases={i: i for i in range(m)},
        compiler_params=pltpu.CompilerParams(has_side_effects=EFFECT),
    )(*lands, ssem, rsem, after)
    return list(outs)


def _forward_pair(lands, tag):
    n = len(lands)
    halves = [a.shape[1] // 2 for a in lands]

    def body(*refs):
        full = refs[n:2 * n]
        ssem, rsem = refs[2 * n:]
        x, y, c = _mesh_pos()
        _pair_barrier(x, y, c)
        cps = []
        for i in range(n):
            for j, (_, _, qj) in enumerate(_other_chips(x, y)):
                part = full[i].at[qj, _half_rows(c, halves[i]), :]
                cp = pltpu.make_async_remote_copy(src_ref=part, dst_ref=part, send_sem=ssem.at[3 * i + j],
                                                  recv_sem=rsem.at[3 * i + j], device_id=(x, y, 1 - c), device_id_type=MESH)
                cp.start()
                cps.append(cp)
        for cp in cps:
            cp.wait()

    return pl.pallas_call(
        body, name="forward_pair_" + tag,
        in_specs=[ANY] * n, out_specs=[ANY] * n,
        out_shape=[_sds(a.shape, a.dtype) for a in lands],
        input_output_aliases={i: i for i in range(n)},
        scratch_shapes=[pltpu.SemaphoreType.DMA((3 * n,)), pltpu.SemaphoreType.DMA((3 * n,))],
        compiler_params=_pair_params(),
    )(*lands)


def _chip_exchange_start(parts, after, tag):
    n = len(parts)

    def body(*refs):
        src, land = refs[:n], refs[n:2 * n]
        ssem, rsem = refs[2 * n + 1], refs[2 * n + 2]
        token = refs[-1]
        x, y, c = _mesh_pos()
        for i in range(n):
            for j, (px, py, qj) in enumerate(_other_chips(x, y)):
                pltpu.make_async_remote_copy(src_ref=src[i].at[qj], dst_ref=land[i].at[j], send_sem=ssem.at[3 * i + j],
                                             recv_sem=rsem.at[3 * i + j], device_id=(px, py, c), device_id_type=MESH).start()
        token[...] = jnp.zeros(token.shape, F32)

    lands = [lax.empty((3,) + a.shape[1:], a.dtype) for a in parts]
    outs = pl.pallas_call(
        body, name="chip_exchange_start_" + tag,
        in_specs=[HBM] * (2 * n) + [ANY], out_specs=[SEM, SEM] + [HBM] * (2 * n) + [VMEM],
        out_shape=[pltpu.SemaphoreType.DMA((3 * n,)), pltpu.SemaphoreType.DMA((3 * n,))]
        + [pltpu.HBM(a.shape, a.dtype) for a in parts] + [pltpu.HBM(a.shape, a.dtype) for a in lands]
        + [_sds((SUBLANES, LANES), F32)],
        input_output_aliases={i: 2 + i for i in range(2 * n)},
        compiler_params=pltpu.CompilerParams(has_side_effects=EFFECT),
    )(*[_in_hbm(a) for a in parts], *[_in_hbm(a) for a in lands], after)
    return outs[0], outs[1], list(outs[2:2 + n]), list(outs[2 + n:2 + 2 * n]), outs[-1]


def _chip_exchange_wait(ssem, rsem, parts, lands, after, tag):
    n = len(parts)

    def body(*refs):
        src, land = refs[:n], refs[n:2 * n]
        ssem_, rsem_ = refs[2 * n], refs[2 * n + 1]
        x, y, c = _mesh_pos()
        for i in range(n):
            for j, (px, py, qj) in enumerate(_other_chips(x, y)):
                cp = pltpu.make_async_remote_copy(src_ref=src[i].at[qj], dst_ref=land[i].at[j], send_sem=ssem_.at[3 * i + j],
                                                  recv_sem=rsem_.at[3 * i + j], device_id=(px, py, c), device_id_type=MESH)
                cp.wait_send()
                cp.wait_recv()

    outs = pl.pallas_call(
        body, name="chip_exchange_wait_" + tag,
        in_specs=[HBM] * (2 * n) + [SEM, SEM, ANY], out_specs=[HBM] * (2 * n),
        out_shape=[pltpu.HBM(a.shape, a.dtype) for a in parts] + [pltpu.HBM(a.shape, a.dtype) for a in lands],
        input_output_aliases={i: i for i in range(2 * n)},
        compiler_params=pltpu.CompilerParams(has_side_effects=EFFECT),
    )(*parts, *lands, ssem, rsem, after)
    return list(outs[:n]), list(outs[n:])


def _grad_half(ref, shape, axis, which):
    rows = shape[axis] // 2
    if axis == 0:
        return ref.at[_half_rows(which, rows), :]
    return ref.at[:, _half_rows(which, rows), :]


def _half_shape(a, axis):
    s = list(a.shape)
    s[axis] //= 2
    return tuple(s)


def _pair_exchange_start(grads, half_axis, after, tag):
    n = len(grads)

    def body(*refs):
        g, land = refs[:n], refs[n:2 * n]
        ssem, rsem = refs[2 * n + 1], refs[2 * n + 2]
        token = refs[-1]
        x, y, c = _mesh_pos()
        for i in range(n):
            pltpu.make_async_remote_copy(src_ref=_grad_half(g[i], grads[i].shape, half_axis[i], 1 - c), dst_ref=land[i],
                                         send_sem=ssem.at[i], recv_sem=rsem.at[i], device_id=(x, y, 1 - c),
                                         device_id_type=MESH).start()
        token[...] = jnp.zeros(token.shape, F32)

    lands = [lax.empty(_half_shape(a, half_axis[i]), a.dtype) for i, a in enumerate(grads)]
    outs = pl.pallas_call(
        body, name="pair_exchange_start_" + tag,
        in_specs=[HBM] * (2 * n) + [ANY], out_specs=[SEM, SEM] + [HBM] * (2 * n) + [VMEM],
        out_shape=[pltpu.SemaphoreType.DMA((n,)), pltpu.SemaphoreType.DMA((n,))]
        + [pltpu.HBM(a.shape, a.dtype) for a in grads] + [pltpu.HBM(a.shape, a.dtype) for a in lands]
        + [_sds((SUBLANES, LANES), F32)],
        input_output_aliases={i: 2 + i for i in range(2 * n)},
        compiler_params=pltpu.CompilerParams(has_side_effects=EFFECT),
    )(*[_in_hbm(a) for a in grads], *[_in_hbm(a) for a in lands], after)
    return outs[0], outs[1], list(outs[2:2 + n]), list(outs[2 + n:2 + 2 * n]), outs[-1]


def _pair_exchange_wait(ssem, rsem, grads, lands, half_axis, after, tag):
    n = len(grads)

    def body(*refs):
        g, land = refs[:n], refs[n:2 * n]
        ssem_, rsem_ = refs[2 * n], refs[2 * n + 1]
        x, y, c = _mesh_pos()
        for i in range(n):
            cp = pltpu.make_async_remote_copy(src_ref=_grad_half(g[i], grads[i].shape, half_axis[i], 1 - c),
                                              dst_ref=land[i], send_sem=ssem_.at[i], recv_sem=rsem_.at[i],
                                              device_id=(x, y, 1 - c), device_id_type=MESH)
            cp.wait_send()
            cp.wait_recv()

    outs = pl.pallas_call(
        body, name="pair_exchange_wait_" + tag,
        in_specs=[HBM] * (2 * n) + [SEM, SEM, ANY], out_specs=[HBM] * (2 * n),
        out_shape=[pltpu.HBM(a.shape, a.dtype) for a in grads] + [pltpu.HBM(a.shape, a.dtype) for a in lands],
        input_output_aliases={i: i for i in range(2 * n)},
        compiler_params=pltpu.CompilerParams(has_side_effects=EFFECT),
    )(*grads, *lands, ssem, rsem, after)
    return list(outs[:n]), list(outs[n:])


def _pair_exchange_grads(grads, half_axis, tag):
    n = len(grads)

    def body(*refs):
        g, got = refs[:n], refs[n:2 * n]
        ssem, rsem = refs[2 * n:]
        x, y, c = _mesh_pos()
        _pair_barrier(x, y, c)
        cps = []
        for i in range(n):
            cp = pltpu.make_async_remote_copy(src_ref=_grad_half(g[i], grads[i].shape, half_axis[i], 1 - c),
                                              dst_ref=got[i], send_sem=ssem.at[i], recv_sem=rsem.at[i],
                                              device_id=(x, y, 1 - c), device_id_type=MESH)
            cp.start()
            cps.append(cp)
        for cp in cps:
            cp.wait()

    return pl.pallas_call(
        body, name="pair_exchange_grads_" + tag,
        in_specs=[ANY] * n, out_specs=[ANY] * n,
        out_shape=[_sds(_half_shape(a, half_axis[i]), a.dtype) for i, a in enumerate(grads)],
        scratch_shapes=[pltpu.SemaphoreType.DMA((n,)), pltpu.SemaphoreType.DMA((n,))],
        compiler_params=_pair_params(),
    )(*grads)


def _pair_sum(gs, gots, c_arr, col_sharded, tag):
    n = len(gs)
    g_specs, got_specs, out_specs, out_shapes = [], [], [], []
    for g, by_cols in zip(gs, col_sharded):
        if by_cols:
            rows, cols = g.shape
            rh, cs = rows // 2, cols // N_CHIPS
            g_specs.append(pl.BlockSpec((rh, cs), lambda k, c_ref: (c_ref[0], k)))
            got_specs.append(pl.BlockSpec((rh, cs), lambda k, c_ref: (0, k)))
        else:
            _, rows, cs = g.shape
            rh = rows // 2
            g_specs.append(pl.BlockSpec((1, rh, cs), lambda k, c_ref: (k, c_ref[0], 0)))
            got_specs.append(pl.BlockSpec((1, rh, cs), lambda k, c_ref: (k, 0, 0)))
        out_specs.append(pl.BlockSpec((1, rh, cs), lambda k, c_ref: (k, 0, 0)))
        out_shapes.append(_sds((N_CHIPS, rh, cs), BF16))

    def body(c_ref, *refs):
        for g_ref, got_ref, out_ref in zip(refs[:n], refs[n:2 * n], refs[2 * n:]):
            total = g_ref[...].astype(F32) + got_ref[...].astype(F32)
            out_ref[...] = total.astype(BF16).reshape(out_ref.shape)

    return list(pl.pallas_call(
        body, name="pair_sum_" + tag,
        grid_spec=pltpu.PrefetchScalarGridSpec(
            num_scalar_prefetch=1, grid=(N_CHIPS,), in_specs=g_specs + got_specs, out_specs=out_specs),
        out_shape=out_shapes,
        compiler_params=_params(("arbitrary",)),
    )(c_arr, *gs, *gots))


def _chip_sum(parts, gots, qc_arr, tag):
    n = len(parts)
    steps = 2 if all(p.shape[1] % 32 == 0 for p in parts) else 1
    part_specs, got_specs, out_specs, out_shapes = [], [], [], []
    for p in parts:
        _, rh, cs = p.shape
        rb = rh // steps
        part_specs.append(pl.BlockSpec((1, rb, cs), lambda i, qc: (qc[0], i, 0)))
        got_specs.append(pl.BlockSpec((3, rb, cs), lambda i, qc: (0, i, 0)))
        out_specs.append(pl.BlockSpec((rb, cs), lambda i, qc: (qc[1] * steps + i, 0)))
        out_shapes.append(_sds((2 * rh, cs), F32))

    def body(qc_ref, *refs):
        for part_ref, got_ref, out_ref in zip(refs[:n], refs[n:2 * n], refs[2 * n:]):
            total = part_ref[0].astype(F32)
            for j in range(3):
                total = total + got_ref[j].astype(F32)
            out_ref[...] = total

    return list(pl.pallas_call(
        body, name="chip_sum_" + tag,
        grid_spec=pltpu.PrefetchScalarGridSpec(
            num_scalar_prefetch=1, grid=(steps,), in_specs=part_specs + got_specs, out_specs=out_specs),
        out_shape=out_shapes,
        compiler_params=_params(("arbitrary",)),
    )(qc_arr, *parts, *gots))


def _pair_share_grads(grads, tag):
    n = len(grads)

    def body(*refs):
        g = refs[n:2 * n]
        ssem, rsem = refs[2 * n:]
        x, y, c = _mesh_pos()
        _pair_barrier(x, y, c)
        cps = []
        for i in range(n):
            mine = g[i].at[_half_rows(c, grads[i].shape[0] // 2), :]
            cp = pltpu.make_async_remote_copy(src_ref=mine, dst_ref=mine, send_sem=ssem.at[i], recv_sem=rsem.at[i],
                                              device_id=(x, y, 1 - c), device_id_type=MESH)
            cp.start()
            cps.append(cp)
        for cp in cps:
            cp.wait()

    return pl.pallas_call(
        body, name="pair_share_grads_" + tag,
        in_specs=[ANY] * n, out_specs=[ANY] * n,
        out_shape=[_sds(a.shape, a.dtype) for a in grads],
        input_output_aliases={i: i for i in range(n)},
        scratch_shapes=[pltpu.SemaphoreType.DMA((n,)), pltpu.SemaphoreType.DMA((n,))],
        compiler_params=_pair_params(),
    )(*grads)


def _small_allreduce(parts, places, rows_total, width, after):
    n = len(parts)

    def body(*refs):
        ins, out_ref = refs[:n], refs[n + 1]
        pack, pair_got, chip_sum, got, ssem, rsem = refs[n + 2:]
        x, y, c = _mesh_pos()
        chip = 2 * x + y
        pack[...] = jnp.zeros(pack.shape, F32)
        for i in range(n):
            for row, col, src_row, rows in places[i]:
                w = parts[i].shape[1]
                pack[row:row + rows, col:col + w] = ins[i][src_row:src_row + rows, :]
        swap = pltpu.make_async_remote_copy(src_ref=pack, dst_ref=pair_got, send_sem=ssem.at[3], recv_sem=rsem.at[3],
                                            device_id=(x, y, 1 - c), device_id_type=MESH)
        swap.start()
        swap.wait()
        chip_sum[...] = pack[...] + pair_got[...]
        cps = []
        for j, (px, py, _) in enumerate(_other_chips(x, y)):
            cp = pltpu.make_async_remote_copy(src_ref=chip_sum, dst_ref=got.at[j], send_sem=ssem.at[j],
                                              recv_sem=rsem.at[j], device_id=(px, py, c), device_id_type=MESH)
            cp.start()
            cps.append(cp)
        for cp in cps:
            cp.wait()
        total = jnp.zeros(pack.shape, F32)
        for q in range(N_CHIPS):
            rel = jnp.bitwise_xor(chip, q)
            theirs = got[jnp.maximum(rel - 1, 0)]
            total = total + jnp.where(rel == 0, chip_sum[...], theirs)
        out_ref[...] = total

    return pl.pallas_call(
        body, name="small_allreduce",
        in_specs=[VMEM] * n + [ANY], out_specs=VMEM,
        out_shape=_sds((rows_total, width), F32),
        scratch_shapes=[pltpu.VMEM((rows_total, width), F32), pltpu.VMEM((rows_total, width), F32),
                        pltpu.VMEM((rows_total, width), F32), pltpu.VMEM((3, rows_total, width), F32),
                        pltpu.SemaphoreType.DMA((4,)), pltpu.SemaphoreType.DMA((4,))],
        compiler_params=_params(),
    )(*parts, after)


def _small_update(red, q_arr, takes, loss_at, ws, ms, vs):
    n_w = len(ws)

    def body(q_ref, red_ref, *refs):
        w_in, m_in, v_in = refs[0:n_w], refs[n_w:2 * n_w], refs[2 * n_w:3 * n_w]
        outs = refs[3 * n_w:]
        g_out, d_out, m_out, v_out = (outs[0:n_w], outs[n_w:2 * n_w], outs[2 * n_w:3 * n_w], outs[3 * n_w:4 * n_w])
        loss_ref = outs[4 * n_w]
        chip = q_ref[0]

        def put(g_ref, d0, nr, s0, lo, w):
            if len(g_ref.shape) == 3:
                for r in range(nr):
                    g_ref[d0 + r] = red_ref[s0 + r:s0 + r + 1, lo:lo + w]
            else:
                g_ref[d0:d0 + nr, :] = red_ref[s0:s0 + nr, lo:lo + w]

        def take_own_columns(g_ref, d0, nr, s0, c0, w):
            for k in range(N_CHIPS):
                @pl.when(chip == k)
                def _():
                    put(g_ref, d0, nr, s0, c0 + k * w, w)

        for j in range(n_w):
            w = ws[j].shape[-1]
            for d0, nr, s0, c0, sharded in takes[j]:
                if sharded:
                    take_own_columns(g_out[j], d0, nr, s0, c0, w)
                else:
                    put(g_out[j], d0, nr, s0, c0, w)
            d_out[j][...], m_out[j][...], v_out[j][...] = _adamw_math(w_in[j][...], g_out[j][...], m_in[j][...], v_in[j][...])
        loss_ref[...] = red_ref[loss_at[0]:loss_at[0] + 1, loss_at[1]:loss_at[1] + LANES]

    shapes = [_sds(w.shape, F32) for w in ws]
    outs = pl.pallas_call(
        body, name="small_update",
        in_specs=[pl.BlockSpec(memory_space=pltpu.SMEM)] + [VMEM] * (1 + 3 * n_w), out_specs=[VMEM] * (4 * n_w + 1),
        out_shape=shapes * 4 + [_sds((1, LANES), F32)],
        compiler_params=_params(),
    )(q_arr, red, *ws, *ms, *vs)
    return outs[0:n_w], outs[n_w:2 * n_w], outs[2 * n_w:3 * n_w], outs[3 * n_w:4 * n_w], outs[4 * n_w]


def _adamw_math(w, g, m, v):
    m2 = ADAM_B1 * m + (1.0 - ADAM_B1) * g
    v2 = ADAM_B2 * v + (1.0 - ADAM_B2) * (g * g)
    m_hat = m2 / (1.0 - ADAM_B1 ** ADAM_STEP)
    v_hat = v2 / (1.0 - ADAM_B2 ** ADAM_STEP)
    delta = -ADAM_LR * (m_hat / (jnp.sqrt(v_hat) + ADAM_EPS) + ADAM_WD * w)
    return delta, m2, v2


ADAMW_BLOCK_BYTES = 3 * 2 ** 19


def _adamw_big(ws, gs, ms, vs, tag):
    n = len(ws)

    def fits(steps):
        return all(w.shape[0] % steps == 0 and (w.shape[0] // steps) % SUBLANES == 0
                   and (w.shape[0] // steps) * w.shape[1] * 4 * n <= ADAMW_BLOCK_BYTES for w in ws)

    steps = next(s for s in range(1, min(w.shape[0] for w in ws) + 1) if fits(s))
    specs = [pl.BlockSpec((w.shape[0] // steps, w.shape[1]), lambda i: (i, 0)) for w in ws]

    def body(*refs):
        ins, outs = refs[:4 * n], refs[4 * n:]
        for i in range(n):
            w_ref, g_ref, m_ref, v_ref = ins[i], ins[n + i], ins[2 * n + i], ins[3 * n + i]
            gg = g_ref[...]
            outs[4 * i][...] = gg
            outs[4 * i + 1][...], outs[4 * i + 2][...], outs[4 * i + 3][...] = _adamw_math(
                w_ref[...], gg, m_ref[...], v_ref[...])

    outs = pl.pallas_call(
        body, name="adamw_" + tag, grid=(steps,),
        in_specs=specs * 4, out_specs=[s for s in specs for _ in range(4)],
        out_shape=[_sds(w.shape, F32) for w in ws for _ in range(4)],
        compiler_params=_params(("arbitrary",)),
    )(*ws, *gs, *ms, *vs)
    return [outs[4 * i:4 * i + 4] for i in range(n)]


SMALL_ROWS = 40
PACK_ROWS = 64


def kernel(x, meta_tokens, pre_mix_norm, w_in, conv_a_w, conv_b_w, conv_b_bias, ln_b_gain, ln_b_bias, w_out, post_mix_norm, pre_ffn_norm, w_gate, w_up, w_down, post_ffn_norm, loss_target, m_meta_tokens, m_pre_mix_norm, m_w_in, m_conv_a_w, m_conv_b_w, m_conv_b_bias, m_ln_b_gain, m_ln_b_bias, m_w_out, m_post_mix_norm, m_pre_ffn_norm, m_w_gate, m_w_up, m_w_down, m_post_ffn_norm, v_meta_tokens, v_pre_mix_norm, v_w_in, v_conv_a_w, v_conv_b_w, v_conv_b_bias, v_ln_b_gain, v_ln_b_bias, v_w_out, v_post_mix_norm, v_pre_ffn_norm, v_w_gate, v_w_up, v_w_down, v_post_ffn_norm):
    xq, yq, cq = lax.axis_index("x"), lax.axis_index("y"), lax.axis_index("c")
    chip = 2 * xq + yq
    c_arr = jnp.reshape(cq, (1,)).astype(jnp.int32)
    qc_arr = jnp.stack([chip, cq]).astype(jnp.int32)

    seq, d = x.shape[1], x.shape[2]
    x2, tgt2 = x[0], loss_target[0]
    tr = lambda a: jnp.swapaxes(a, 1, 2)[0]
    w_in2, w_out2, w_gate2, w_up2, w_down2 = w_in[0], w_out[0], tr(w_gate), tr(w_up), w_down[0]
    ka, wa_sh = conv_a_w.shape[1], conv_a_w.shape[2]
    kb = conv_b_w.shape[1]
    meta_sh = meta_tokens.shape[1]

    small = jnp.zeros((PACK_ROWS, meta_sh), F32)
    small = small.at[0:N_META, :].set(meta_tokens)
    small = small.at[16:16 + ka, 0:wa_sh].set(conv_a_w[0])
    small = small.at[24:24 + kb, 0:wa_sh].set(conv_b_w[0])
    q_arr = jnp.reshape(chip, (1,)).astype(jnp.int32)
    small_own = lax.dynamic_update_slice(jnp.zeros((N_CHIPS, PACK_ROWS, meta_sh), F32), small[None], (chip, 0, 0))
    i_ssem, i_rsem, first, i_token = _gather_start(_cast_place([w_in2], q_arr, "w_in") + [small_own], pre_mix_norm, "in")
    rest = _cast_place([w_out2, w_gate2, w_up2, w_down2], q_arr, "rest", i_token)
    g_ssem, g_rsem, lands, g_token = _gather_start(rest, i_token, "rest")
    win4, small4 = _forward_pair(_gather_wait([0, 1], i_ssem, i_rsem, first, g_token, "in"), "in")
    meta_f = jnp.concatenate([small4[k, 0:N_META, :] for k in range(N_CHIPS)], axis=1)
    wa_f = jnp.concatenate([small4[k, 16:16 + ka, 0:wa_sh] for k in range(N_CHIPS)], axis=1)
    wb_f = jnp.concatenate([small4[k, 24:24 + kb, 0:wa_sh] for k in range(N_CHIPS)], axis=1)

    tm = _row_tile(seq + TAIL_ROWS)
    tail = lax.dynamic_update_slice(jnp.zeros((tm, d), F32), meta_f, (seq % tm, 0))
    h, xn1, hp5 = _mm_in(x2, tail, win4, pre_mix_norm, g_token)
    ya, z = _mix_conv_fwd(hp5, wa_f, wb_f, conv_b_bias)
    (wout4,) = _forward_pair(_gather_wait([0], g_ssem, g_rsem, lands[0:1], z, "out"), "out")
    wout_f = wout4.reshape(N_CHIPS * wout4.shape[1], wout4.shape[2])
    yb, mix, h1, xn2 = _mm_out(ya, z, h, wout_f, ln_b_gain, ln_b_bias, post_mix_norm, pre_ffn_norm)
    wg4, wu4 = _forward_pair(_gather_wait([1, 2], g_ssem, g_rsem, lands[1:3], xn2, "gate_up"), "gate_up")
    stacked = lambda a: a.reshape(a.shape[0] * a.shape[1], a.shape[2])
    wg_f, wu_f = stacked(wg4), stacked(wu4)
    p_act, q_act, f_act = _ffn_up(xn2, wg_f, wu_f)
    (wd4,) = _forward_pair(_gather_wait([3], g_ssem, g_rsem, lands[3:4], f_act, "down"), "down")
    wd_f = stacked(wd4)
    dff, dh2, loss_blk, d_gpf = _ffn_down(f_act, wd_f, h1, tgt2, post_ffn_norm)

    da, du = _ffn_bwd_act(dff, wd_f, p_act, q_act)
    by_chip = lambda g: g.reshape(N_CHIPS, g.shape[0] // N_CHIPS, g.shape[1])
    g_down = by_chip(_grad_w_down(f_act, dff))
    g_gate, g_up = [by_chip(g) for g in _grad_w_gate_up(xn2, da, du)]
    ffn = [g_gate, g_up, g_down]
    p_ssem, p_rsem, ffn, p_lands, p_token = _pair_exchange_start(ffn, [1, 1, 1], dff, "ffn")
    dh1, dmix, d_g2, d_gpm = _ffn_bwd_in(da, du, wg_f, wu_f, h1, mix, dh2, pre_ffn_norm, post_mix_norm, p_token)
    ffn, got = _pair_exchange_wait(p_ssem, p_rsem, ffn, p_lands, [1, 1, 1], d_g2, "ffn")
    parts = _pair_sum(ffn, got, c_arr, [False] * 3, "ffn")
    f_ssem, f_rsem, parts, f_lands, f_token = _chip_exchange_start(parts, dff, "ffn")
    g_out = _grad_w_out(ya, yb, dmix, f_token)
    dya, dz, d_lg, d_lb = _mix_bwd_out(dmix, wout_f, z, ln_b_gain, ln_b_bias, f_token)
    dhp5, d_wa, d_wb, d_bb = _mix_conv_bwd(hp5, dya, dz, wa_f, wb_f)
    g_in = _grad_w_in(xn1, dhp5)

    g_out4 = g_out.reshape(N_CHIPS, g_out.shape[0] // N_CHIPS, g_out.shape[1])
    mixw = [g_in, g_out4]
    got2 = _pair_exchange_grads(mixw, [0, 1], "mix")
    parts2 = _pair_sum(mixw, got2, c_arr, [True, False], "mix")
    m_ssem, m_rsem, parts2, m_lands, m_token = _chip_exchange_start(parts2, dhp5, "mix")
    grad_x2, d_meta, d_g1 = _mix_bwd_in(dhp5, win4, h, dh1, pre_mix_norm, m_token)
    grad_x = grad_x2[None]

    parts, f_recv = _chip_exchange_wait(f_ssem, f_rsem, parts, f_lands, d_g1, "ffn")
    halves = _chip_sum(parts, f_recv, qc_arr, "ffn")
    gsum_ffn = _pair_share_grads(halves, "ffn")

    names_big = ["w_in", "w_out", "w_gate", "w_up", "w_down"]
    w_big = dict(zip(names_big, [w_in2, w_out2, w_gate2, w_up2, w_down2]))
    m_big = dict(zip(names_big, [m_w_in[0], m_w_out[0], tr(m_w_gate), tr(m_w_up), m_w_down[0]]))
    v_big = dict(zip(names_big, [v_w_in[0], v_w_out[0], tr(v_w_gate), tr(v_w_up), v_w_down[0]]))
    grads, deltas, new_m, new_v = {}, {}, {}, {}

    def update(names, gs, tag):
        res = _adamw_big([w_big[k] for k in names], gs, [m_big[k] for k in names], [v_big[k] for k in names], tag)
        for nm, outs in zip(names, res):
            if nm in ("w_gate", "w_up"):
                outs = [jnp.swapaxes(o[None], 1, 2) for o in outs]
            else:
                outs = [o[None] for o in outs]
            grads[nm], deltas[nm], new_m[nm], new_v[nm] = outs
        return res[-1][1]

    last = update(["w_gate", "w_up", "w_down"], list(gsum_ffn), "ffn")

    hw = d // 2
    assert d_wa.shape == (3, hw) and d_wb.shape == (31, hw) and d_bb.shape == (1, hw)
    small_parts = [d_meta, d_g1, d_gpm, d_g2, d_gpf, d_bb, d_lg, d_lb, loss_blk[0:1, :], d_wa, d_wb]
    places = [[(0, 0, 0, N_META)], [(16, 0, 0, 1)], [(17, 0, 0, 1)], [(18, 0, 0, 1)], [(19, 0, 0, 1)],
              [(20, 0, 0, 1)], [(20, hw, 0, 1)], [(21, 0, 0, 1)], [(21, hw, 0, 1)], [(22, 0, 0, 3)],
              [(22, hw, 0, 3), (25, 0, 3, 14), (25, hw, 17, 14)]]
    names_small = ["meta_tokens", "pre_mix_norm", "conv_a_w", "conv_b_w", "conv_b_bias", "ln_b_gain", "ln_b_bias",
                   "post_mix_norm", "pre_ffn_norm", "post_ffn_norm"]
    takes = [[(0, N_META, 0, 0, True)], [(0, 1, 16, 0, False)], [(0, 3, 22, 0, True)],
             [(0, 3, 22, hw, True), (3, 14, 25, 0, True), (17, 14, 25, hw, True)], [(0, 1, 20, 0, False)],
             [(0, 1, 20, hw, False)], [(0, 1, 21, 0, False)], [(0, 1, 17, 0, False)], [(0, 1, 18, 0, False)],
             [(0, 1, 19, 0, False)]]
    taps = lambda a: jnp.swapaxes(a, 0, 1)
    w_small = [meta_tokens, pre_mix_norm, taps(conv_a_w), taps(conv_b_w), conv_b_bias, ln_b_gain, ln_b_bias, post_mix_norm,
               pre_ffn_norm, post_ffn_norm]
    m_small = [m_meta_tokens, m_pre_mix_norm, taps(m_conv_a_w), taps(m_conv_b_w), m_conv_b_bias, m_ln_b_gain, m_ln_b_bias,
               m_post_mix_norm, m_pre_ffn_norm, m_post_ffn_norm]
    v_small = [v_meta_tokens, v_pre_mix_norm, taps(v_conv_a_w), taps(v_conv_b_w), v_conv_b_bias, v_ln_b_gain, v_ln_b_bias,
               v_post_mix_norm, v_pre_ffn_norm, v_post_ffn_norm]
    red = _small_allreduce(small_parts, places, SMALL_ROWS, d, last)
    g_s, d_s, m_s, v_s, loss_row = _small_update(red, q_arr, takes, (21, hw), w_small, m_small, v_small)
    loss = loss_row[0, 0]
    for i, nm in enumerate(names_small):
        fix = taps if nm in ("conv_a_w", "conv_b_w") else (lambda a: a)
        grads[nm], deltas[nm], new_m[nm], new_v[nm] = fix(g_s[i]), fix(d_s[i]), fix(m_s[i]), fix(v_s[i])

    parts2, m_recv = _chip_exchange_wait(m_ssem, m_rsem, parts2, m_lands, loss_row, "mix")
    halves2 = _chip_sum(parts2, m_recv, qc_arr, "mix")
    gsum_mix = _pair_share_grads(halves2, "mix")
    update(["w_in", "w_out"], list(gsum_mix), "mix")

    order = ["meta_tokens", "pre_mix_norm", "w_in", "conv_a_w", "conv_b_w", "conv_b_bias", "ln_b_gain", "ln_b_bias", "w_out",
             "post_mix_norm", "pre_ffn_norm", "w_gate", "w_up", "w_down", "post_ffn_norm"]
    return (loss, grad_x, *[grads[k] for k in order], *[deltas[k] for k in order], *[new_m[k] for k in order],
            *[new_v[k] for k in order])
```

```python
import jax
import jax.numpy as jnp
from jax import lax
from jax.experimental import pallas as pl
from jax.experimental.pallas import tpu as pltpu

F32 = jnp.float32
BF16 = jnp.bfloat16
MESH = pl.DeviceIdType.MESH

N_META = 16
TAIL_ROWS = 128
RMS_EPS = 1e-6
LN_EPS = 1e-5
ADAM_LR = 0.001
ADAM_B1 = 0.9
ADAM_B2 = 0.999
ADAM_EPS = 1e-08
ADAM_WD = 0.01
ADAM_STEP = 10

N_CHIPS = 4
LANES = 128
SUBLANES = 8
BF16_ROWS = 16
MXU_TILE = 256
CONV_CHUNK = 48
CONV_HIST = 32
ROW_TILE_CAP = 640
VMEM_LIMIT = 56 * 1024 * 1024

NN = (((1,), (0,)), ((), ()))
NT = (((1,), (1,)), ((), ()))
TN = (((0,), (0,)), ((), ()))


def _dot(a, b, dims=NN):
    return lax.dot_general(a, b, dims, preferred_element_type=F32)


def _sig(v):
    return 1.0 / (1.0 + jnp.exp(-v))


def _mean(v):
    return jnp.mean(v, axis=-1, keepdims=True)


def _row_tile(rows):
    best = BF16_ROWS
    for t in range(BF16_ROWS, min(rows, ROW_TILE_CAP) + 1, BF16_ROWS):
        if rows % t == 0:
            best = t
    assert rows % best == 0
    return best


def _row_parts(tm, parts=2):
    units = tm // BF16_ROWS
    if tm % BF16_ROWS or units < parts:
        return [slice(0, tm)]
    cuts = [BF16_ROWS * ((units * k + parts - 1) // parts) for k in range(parts + 1)]
    return [slice(lo, hi) for lo, hi in zip(cuts[:-1], cuts[1:])]


def _concat_shards(w_ref, wcat_ref):
    n_sh, _, csh = w_ref.shape

    @pl.when(pl.program_id(0) == 0)
    def _():
        for k in range(n_sh):
            wcat_ref[:, k * csh:(k + 1) * csh] = w_ref[k]


def _params(semantics=None):
    kw = dict(vmem_limit_bytes=VMEM_LIMIT)
    if semantics is not None:
        kw["dimension_semantics"] = semantics
    return pltpu.CompilerParams(**kw)


def _full(shape):
    nd = len(shape)
    return pl.BlockSpec(shape, lambda *_: (0,) * nd)


def _resident(shape):
    nd = len(shape)
    return pl.BlockSpec(shape, lambda *_: (0,) * nd, pipeline_mode=pl.Buffered(1))


def _sds(shape, dtype):
    return jax.ShapeDtypeStruct(shape, dtype)


ANY = pl.BlockSpec(memory_space=pl.ANY)
VMEM = pl.BlockSpec(memory_space=pltpu.VMEM)


def _mesh_pos():
    return lax.axis_index("x"), lax.axis_index("y"), lax.axis_index("c")


def _flip(v, bit):
    return 1 - v if bit else v


def _mm_in(x, tail, win4, g1, after):
    seq, d = x.shape
    tp = seq + TAIL_ROWS
    tm = _row_tile(tp)
    n_sh, _, csh = win4.shape
    pw = n_sh * csh // 5

    def body(x_ref, tail_ref, w_ref, g_ref, after_ref, h_ref, xn_ref, hp_ref, wcat_ref):
        _concat_shards(w_ref, wcat_ref)
        rows = pl.program_id(0) * tm + lax.broadcasted_iota(jnp.int32, (tm, 1), 0)
        hh = jnp.where(rows < seq, x_ref[...], tail_ref[...])
        h_ref[...] = hh
        r = lax.rsqrt(_mean(hh * hh) + RMS_EPS)
        xn = (hh * r * g_ref[...]).astype(BF16)
        xn_ref[...] = xn
        for p in range(5):
            hp_ref[p] = _dot(xn, wcat_ref[:, p * pw:(p + 1) * pw])

    row = pl.BlockSpec((tm, d), lambda i: (i, 0))
    return pl.pallas_call(
        body, name="mm_in", grid=(tp // tm,),
        in_specs=[row, _full(tail.shape), _resident(win4.shape), _full(g1.shape), ANY],
        out_specs=[row, row, pl.BlockSpec((5, tm, pw), lambda i: (0, i, 0))],
        out_shape=[_sds((tp, d), F32), _sds((tp, d), BF16), _sds((5, tp, pw), F32)],
        scratch_shapes=[pltpu.VMEM((d, n_sh * csh), BF16)],
        compiler_params=_params(("arbitrary",)),
    )(x, tail, win4, g1, after)


def _seq_rows(tp):
    seq = tp - TAIL_ROWS
    nseq = seq + N_META
    assert nseq % CONV_CHUNK == 0 and seq % BF16_ROWS == 0
    return seq, nseq


def _conv_offsets(width, transpose):
    return [(width - 1 - k) if transpose else (CONV_HIST - (width - 1) + k) for k in range(width)]


def _shift_copies(src_ref, sh_ref, width, transpose):
    n = src_ref.shape[0] - SUBLANES
    for s in sorted({o % SUBLANES for o in _conv_offsets(width, transpose)} - {0}):
        sh_ref[s - 1, 0:n, :] = src_ref[s:s + n, :]


def _tap_rows(src_ref, sh_ref, base, off):
    start = pl.multiple_of(base + (off // SUBLANES) * SUBLANES, SUBLANES)
    if off % SUBLANES == 0:
        return src_ref[pl.ds(start, CONV_CHUNK), :]
    return sh_ref[off % SUBLANES - 1, pl.ds(start, CONV_CHUNK), :]


def _conv_taps(src_ref, sh_ref, w_ref, dst_ref, width, nseq, transpose):
    w = w_ref[...]
    offs = _conv_offsets(width, transpose)
    _shift_copies(src_ref, sh_ref, width, transpose)

    def step(n, carry):
        out0 = pl.multiple_of(CONV_HIST + n * CONV_CHUNK, SUBLANES)
        base = out0 if transpose else n * CONV_CHUNK
        acc = jnp.zeros((CONV_CHUNK, w.shape[1]), F32)
        for k, off in enumerate(offs):
            acc = acc + w[k:k + 1, :] * _tap_rows(src_ref, sh_ref, base, off)
        dst_ref[pl.ds(out0, CONV_CHUNK), :] = acc
        return carry

    lax.fori_loop(0, nseq // CONV_CHUNK, step, 0)


def _conv_wgrad(src_ref, sh_ref, dz_ref, acc_ref, width, nseq):
    acc_ref[...] = jnp.zeros(acc_ref.shape, F32)
    offs = _conv_offsets(width, False)

    def step(n, carry):
        dzc = dz_ref[pl.ds(pl.multiple_of(CONV_HIST + n * CONV_CHUNK, SUBLANES), CONV_CHUNK), :]
        for k, off in enumerate(offs):
            prod = dzc * _tap_rows(src_ref, sh_ref, n * CONV_CHUNK, off)
            part = prod[0:SUBLANES, :]
            for s in range(1, CONV_CHUNK // SUBLANES):
                part = part + prod[SUBLANES * s:SUBLANES * (s + 1), :]
            acc_ref[SUBLANES * k:SUBLANES * (k + 1), :] += part
        return carry

    lax.fori_loop(0, nseq // CONV_CHUNK, step, 0)


def _to_seq(buf_ref, x_part, meta_part, seq):
    buf_ref[CONV_HIST:CONV_HIST + N_META, :] = meta_part
    buf_ref[CONV_HIST + N_META:CONV_HIST + N_META + seq, :] = x_part


def _zero_ends(buf_ref, nseq):
    zeros = jnp.zeros((CONV_HIST, buf_ref.shape[1]), F32)
    buf_ref[0:CONV_HIST, :] = zeros
    buf_ref[CONV_HIST + nseq:CONV_HIST + nseq + CONV_HIST, :] = zeros


def _mix_conv_fwd(hp5, wa, wb, bb):
    _, tp, wgrp = hp5.shape
    seq, nseq = _seq_rows(tp)
    sb = nseq + 2 * CONV_HIST
    ka, kb = wa.shape[0], wb.shape[0]
    xs, ms = slice(0, seq), slice(seq, seq + N_META)
    ox, om = slice(CONV_HIST + N_META, CONV_HIST + nseq), slice(CONV_HIST, CONV_HIST + N_META)

    def body(hp_ref, wa_ref, wb_ref, bb_ref, ya_ref, z_ref, s_ref, o_ref, sh_ref):
        _zero_ends(s_ref, nseq)
        _to_seq(s_ref, hp_ref[1, xs, :] * hp_ref[2, xs, :], hp_ref[1, ms, :] * hp_ref[2, ms, :], seq)
        _conv_taps(s_ref, sh_ref, wa_ref, o_ref, ka, nseq, False)
        ya_ref[xs, :] = (hp_ref[0, xs, :] * o_ref[ox, :]).astype(BF16)
        ya_ref[ms, :] = (hp_ref[0, ms, :] * o_ref[om, :]).astype(BF16)
        ya_ref[seq + N_META:tp, :] = jnp.zeros((tp - seq - N_META, LANES), BF16)
        _to_seq(s_ref, hp_ref[3, xs, :] * _sig(hp_ref[4, xs, :]), hp_ref[3, ms, :] * _sig(hp_ref[4, ms, :]), seq)
        _conv_taps(s_ref, sh_ref, wb_ref, o_ref, kb, nseq, False)
        z_ref[xs, :] = o_ref[ox, :] + bb_ref[...]
        z_ref[ms, :] = o_ref[om, :] + bb_ref[...]
        z_ref[seq + N_META:tp, :] = jnp.zeros((tp - seq - N_META, LANES), F32)

    col = lambda j: (0, j)
    return pl.pallas_call(
        body, name="mix_conv_fwd", grid=(wgrp // LANES,),
        in_specs=[pl.BlockSpec((5, tp, LANES), lambda j: (0, 0, j)), pl.BlockSpec((ka, LANES), col),
                  pl.BlockSpec((kb, LANES), col), pl.BlockSpec((1, LANES), col)],
        out_specs=[pl.BlockSpec((tp, LANES), col), pl.BlockSpec((tp, LANES), col)],
        out_shape=[_sds((tp, wgrp), BF16), _sds((tp, wgrp), F32)],
        scratch_shapes=[pltpu.VMEM((sb, LANES), F32), pltpu.VMEM((sb, LANES), F32),
                        pltpu.VMEM((SUBLANES - 1, sb, LANES), F32)],
        compiler_params=_params(("arbitrary",)),
    )(hp5, wa, wb, bb)


def _layer_norm_parts(z, lg, lb):
    mu = _mean(z)
    zc = z - mu
    rl = lax.rsqrt(_mean(zc * zc) + LN_EPS)
    zh = zc * rl
    return rl, zh, zh * lg + lb


def _mm_out(ya, z, h, wout, lg, lb, gpm, g2):
    tp, d = h.shape
    wa_ = ya.shape[1]
    tm = _row_tile(tp)

    def body(ya_ref, z_ref, h_ref, w_ref, lg_ref, lb_ref, gpm_ref, g2_ref, yb_ref, mix_ref, h1_ref, xn2_ref):
        for rs in _row_parts(tm, 3):
            _, _, l = _layer_norm_parts(z_ref[rs, :], lg_ref[...], lb_ref[...])
            yb = (l * _sig(l)).astype(BF16)
            yb_ref[rs, :] = yb
            mix = _dot(ya_ref[rs, :], w_ref[0:wa_, :]) + _dot(yb, w_ref[wa_:d, :])
            mix_ref[rs, :] = mix
            rm = lax.rsqrt(_mean(mix * mix) + RMS_EPS)
            h1 = h_ref[rs, :] + mix * rm * gpm_ref[...]
            h1_ref[rs, :] = h1
            r2 = lax.rsqrt(_mean(h1 * h1) + RMS_EPS)
            xn2_ref[rs, :] = (h1 * r2 * g2_ref[...]).astype(BF16)

    row = lambda i: (i, 0)
    return pl.pallas_call(
        body, name="mm_out", grid=(tp // tm,),
        in_specs=[pl.BlockSpec((tm, wa_), row), pl.BlockSpec((tm, wa_), row), pl.BlockSpec((tm, d), row),
                  _resident(wout.shape), _full(lg.shape), _full(lb.shape), _full(gpm.shape), _full(g2.shape)],
        out_specs=[pl.BlockSpec((tm, wa_), row), pl.BlockSpec((tm, d), row), pl.BlockSpec((tm, d), row),
                   pl.BlockSpec((tm, d), row)],
        out_shape=[_sds((tp, wa_), BF16), _sds((tp, d), F32), _sds((tp, d), F32), _sds((tp, d), BF16)],
        compiler_params=_params(("arbitrary",)),
    )(ya, z, h, wout, lg, lb, gpm, g2)


def _ffn_up(xn2, wg, wu):
    tp, d = xn2.shape
    ff_dim = wg.shape[0]
    tm = _row_tile(tp)
    assert ff_dim % MXU_TILE == 0

    def body(xn_ref, wg_ref, wu_ref, p_ref, q_ref, f_ref):
        xn = xn_ref[...]
        for lo in range(0, ff_dim, MXU_TILE):
            cols = slice(lo, lo + MXU_TILE)
            a = _dot(xn, wg_ref[cols, :], NT)
            u = _dot(xn, wu_ref[cols, :], NT)
            s = _sig(a)
            q = a * s
            p_ref[:, cols] = (u * (s + q * (1.0 - s))).astype(BF16)
            q_ref[:, cols] = q.astype(BF16)
            f_ref[:, cols] = (q * u).astype(BF16)

    ospec = pl.BlockSpec((tm, ff_dim), lambda i: (i, 0))
    return pl.pallas_call(
        body, name="ffn_up", grid=(tp // tm,),
        in_specs=[pl.BlockSpec((tm, d), lambda i: (i, 0)), _resident(wg.shape), _resident(wu.shape)],
        out_specs=[ospec, ospec, ospec],
        out_shape=[_sds((tp, ff_dim), BF16)] * 3,
        compiler_params=_params(("arbitrary",)),
    )(xn2, wg, wu)


def _ffn_down(f, wd, h1, tgt, gpf):
    tp, ff_dim = f.shape
    d = h1.shape[1]
    tm = _row_tile(tp)
    seq, _ = _seq_rows(tp)

    def body(f_ref, w_ref, h1_ref, t_ref, gpf_ref, dff_ref, dh2_ref, loss_ref, dgpf_ref):
        i = pl.program_id(0)
        gpf_ = gpf_ref[...]

        @pl.when(i == 0)
        def _():
            loss_ref[...] = jnp.zeros(loss_ref.shape, F32)
            dgpf_ref[...] = jnp.zeros(dgpf_ref.shape, F32)

        for rs in _row_parts(tm):
            ff = _dot(f_ref[rs, :], w_ref[...])
            rf = lax.rsqrt(_mean(ff * ff) + RMS_EPS)
            nf = ff * rf
            h2 = h1_ref[rs, :] + nf * gpf_
            rows = i * tm + rs.start + lax.broadcasted_iota(jnp.int32, (rs.stop - rs.start, 1), 0)
            err = jnp.where(rows < seq, h2 - t_ref[rs, :], 0.0)
            dh2 = err * (1.0 / d)
            dh2_ref[rs, :] = dh2
            dn = dh2 * gpf_
            dff_ref[rs, :] = (rf * (dn - nf * _mean(dn * nf))).astype(BF16)
            loss_ref[...] += (0.5 / d) * jnp.sum(err * err, axis=(0, 1), keepdims=True)
            dgpf_ref[...] += jnp.sum(dh2 * nf, axis=0, keepdims=True)

    row = lambda i: (i, 0)
    return pl.pallas_call(
        body, name="ffn_down", grid=(tp // tm,),
        in_specs=[pl.BlockSpec((tm, ff_dim), row), _resident(wd.shape), pl.BlockSpec((tm, d), row),
                  pl.BlockSpec((tm, d), row), _full(gpf.shape)],
        out_specs=[pl.BlockSpec((tm, d), row), pl.BlockSpec((tm, d), row), _full((SUBLANES, LANES)), _full((1, d))],
        out_shape=[_sds((tp, d), BF16), _sds((tp, d), F32), _sds((SUBLANES, LANES), F32), _sds((1, d), F32)],
        compiler_params=_params(("arbitrary",)),
    )(f, wd, h1, tgt, gpf)


def _ffn_bwd_act(dff, wd, p, q):
    tp, d = dff.shape
    ff_dim = wd.shape[0]
    tm = _row_tile(tp)

    def body(dff_ref, w_ref, p_ref, q_ref, da_ref, du_ref):
        dffv = dff_ref[...]
        for lo in range(0, ff_dim, MXU_TILE):
            cols = slice(lo, lo + MXU_TILE)
            df = _dot(dffv, w_ref[cols, :], NT).astype(BF16)
            da_ref[:, cols] = df * p_ref[:, cols]
            du_ref[:, cols] = df * q_ref[:, cols]

    aspec = pl.BlockSpec((tm, ff_dim), lambda i: (i, 0))
    return pl.pallas_call(
        body, name="ffn_bwd_act", grid=(tp // tm,),
        in_specs=[pl.BlockSpec((tm, d), lambda i: (i, 0)), _resident(wd.shape), aspec, aspec],
        out_specs=[aspec, aspec],
        out_shape=[_sds((tp, ff_dim), BF16)] * 2,
        compiler_params=_params(("arbitrary",)),
    )(dff, wd, p, q)


def _grad_blocks(ff_dim):
    rows = ff_dim // 2
    assert rows % LANES == 0
    return rows


def _grad_w_down(f, dff):
    tp, ff_dim = f.shape
    d = dff.shape[1]
    rows = _grad_blocks(ff_dim)

    def body(f_ref, dff_ref, g_ref):
        g_ref[...] = _dot(f_ref[...], dff_ref[...], TN).astype(BF16)

    return pl.pallas_call(
        body, name="grad_w_down", grid=(ff_dim // rows,),
        in_specs=[pl.BlockSpec((tp, rows), lambda k: (0, k)), _resident(dff.shape)],
        out_specs=pl.BlockSpec((rows, d), lambda k: (k, 0)),
        out_shape=_sds((ff_dim, d), BF16),
        compiler_params=_params(("arbitrary",)),
    )(f, dff)


def _grad_w_gate_up(xn2, da, du):
    tp, ff_dim = da.shape
    d = xn2.shape[1]
    rows = _grad_blocks(ff_dim)

    def body(xn_ref, da_ref, du_ref, gg_ref, gu_ref):
        xn = xn_ref[...]
        gg_ref[...] = _dot(da_ref[...], xn, TN).astype(BF16)
        gu_ref[...] = _dot(du_ref[...], xn, TN).astype(BF16)

    aspec = pl.BlockSpec((tp, rows), lambda k: (0, k))
    gspec = pl.BlockSpec((rows, d), lambda k: (k, 0))
    return pl.pallas_call(
        body, name="grad_w_gate_up", grid=(ff_dim // rows,),
        in_specs=[_resident(xn2.shape), aspec, aspec],
        out_specs=[gspec, gspec],
        out_shape=[_sds((ff_dim, d), BF16)] * 2,
        compiler_params=_params(("arbitrary",)),
    )(xn2, da, du)


def _rms_bwd(dy, x, r, g):
    n = x * r
    dn = dy * g
    return r * (dn - n * _mean(dn * n)), dy * n


def _ffn_bwd_in(da, du, wg, wu, h1, mix, dh2, g2, gpm, after):
    tp, ff_dim = da.shape
    d = h1.shape[1]
    tm = _row_tile(tp)

    def body(da_ref, du_ref, wg_ref, wu_ref, h1_ref, mix_ref, dh2_ref, g2_ref, gpm_ref, after_ref,
             dh1_ref, dmix_ref, dg2_ref, dgpm_ref):
        i = pl.program_id(0)

        @pl.when(i == 0)
        def _():
            dg2_ref[...] = jnp.zeros(dg2_ref.shape, F32)
            dgpm_ref[...] = jnp.zeros(dgpm_ref.shape, F32)

        for rs in _row_parts(tm, 3):
            dxn = _dot(da_ref[rs, :], wg_ref[...]) + _dot(du_ref[rs, :], wu_ref[...])
            h1v = h1_ref[rs, :]
            r2 = lax.rsqrt(_mean(h1v * h1v) + RMS_EPS)
            dres, dg2_rows = _rms_bwd(dxn, h1v, r2, g2_ref[...])
            dh1 = dh2_ref[rs, :] + dres
            dh1_ref[rs, :] = dh1
            mixv = mix_ref[rs, :]
            rm = lax.rsqrt(_mean(mixv * mixv) + RMS_EPS)
            dmix, dgpm_rows = _rms_bwd(dh1, mixv, rm, gpm_ref[...])
            dmix_ref[rs, :] = dmix.astype(BF16)
            dg2_ref[...] += jnp.sum(dg2_rows, axis=0, keepdims=True)
            dgpm_ref[...] += jnp.sum(dgpm_rows, axis=0, keepdims=True)

    aspec = pl.BlockSpec((tm, ff_dim), lambda i: (i, 0))
    row = pl.BlockSpec((tm, d), lambda i: (i, 0))
    return pl.pallas_call(
        body, name="ffn_bwd_in", grid=(tp // tm,),
        in_specs=[aspec, aspec, _resident(wg.shape), _resident(wu.shape), row, row, row, _full(g2.shape), _full(gpm.shape),
                  ANY],
        out_specs=[row, row, _full((1, d)), _full((1, d))],
        out_shape=[_sds((tp, d), F32), _sds((tp, d), BF16), _sds((1, d), F32), _sds((1, d), F32)],
        compiler_params=_params(("arbitrary",)),
    )(da, du, wg, wu, h1, mix, dh2, g2, gpm, after)


def _grad_w_out(ya, yb, dmix, after):
    tp, wa_ = ya.shape
    d = dmix.shape[1]

    def body(ya_ref, yb_ref, dmix_ref, after_ref, g_ref):
        dm = dmix_ref[...]
        g_ref[0:wa_, :] = _dot(ya_ref[...], dm, TN).astype(BF16)
        g_ref[wa_:2 * wa_, :] = _dot(yb_ref[...], dm, TN).astype(BF16)

    return pl.pallas_call(
        body, name="grad_w_out", grid=(1,),
        in_specs=[_full(ya.shape), _full(yb.shape), _full(dmix.shape), ANY],
        out_specs=_full((2 * wa_, d)),
        out_shape=_sds((2 * wa_, d), BF16),
        compiler_params=_params(("arbitrary",)),
    )(ya, yb, dmix, after)


def _mix_bwd_out(dmix, wout, z, lg, lb, after):
    tp, d = dmix.shape
    wa_ = z.shape[1]
    tm = _row_tile(tp)

    def body(dmix_ref, w_ref, z_ref, lg_ref, lb_ref, after_ref, dya_ref, dz_ref, dlg_ref, dlb_ref):
        i = pl.program_id(0)
        lg_ = lg_ref[...]

        @pl.when(i == 0)
        def _():
            dlg_ref[...] = jnp.zeros(dlg_ref.shape, F32)
            dlb_ref[...] = jnp.zeros(dlb_ref.shape, F32)

        for rs in _row_parts(tm):
            dm = dmix_ref[rs, :]
            dya_ref[rs, :] = _dot(dm, w_ref[0:wa_, :], NT)
            dyb = _dot(dm, w_ref[wa_:d, :], NT)
            rl, zh, l = _layer_norm_parts(z_ref[rs, :], lg_, lb_ref[...])
            sl = _sig(l)
            dl = dyb * (sl * (1.0 + l * (1.0 - sl)))
            dzh = dl * lg_
            dz_ref[rs, :] = rl * (dzh - _mean(dzh) - zh * _mean(dzh * zh))
            dlg_ref[...] += jnp.sum(dl * zh, axis=0, keepdims=True)
            dlb_ref[...] += jnp.sum(dl, axis=0, keepdims=True)

    row = lambda i: (i, 0)
    return pl.pallas_call(
        body, name="mix_bwd_out", grid=(tp // tm,),
        in_specs=[pl.BlockSpec((tm, d), row), _resident(wout.shape), pl.BlockSpec((tm, wa_), row), _full(lg.shape),
                  _full(lb.shape), ANY],
        out_specs=[pl.BlockSpec((tm, wa_), row), pl.BlockSpec((tm, wa_), row), _full((1, wa_)), _full((1, wa_))],
        out_shape=[_sds((tp, wa_), F32), _sds((tp, wa_), F32), _sds((1, wa_), F32), _sds((1, wa_), F32)],
        compiler_params=_params(("arbitrary",)),
    )(dmix, wout, z, lg, lb, after)


def _mix_conv_bwd(hp5, dya, dz, wa, wb):
    _, tp, wgrp = hp5.shape
    seq, nseq = _seq_rows(tp)
    sb = nseq + 2 * CONV_HIST
    ka, kb = wa.shape[0], wb.shape[0]
    xs, ms = slice(0, seq), slice(seq, seq + N_META)
    ox, om = slice(CONV_HIST + N_META, CONV_HIST + nseq), slice(CONV_HIST, CONV_HIST + N_META)
    n_tail = tp - seq - N_META

    def body(hp_ref, dya_ref, dz_ref, wa_ref, wb_ref, dhp_ref, dwa_ref, dwb_ref, dbb_ref, s_ref, d_ref, o_ref, acc_ref,
             shs_ref, shd_ref):
        _zero_ends(s_ref, nseq)
        _zero_ends(d_ref, nseq)

        def put(p, ox_val, om_val):
            dhp_ref[p, xs, :] = ox_val.astype(BF16)
            dhp_ref[p, ms, :] = om_val.astype(BF16)
            dhp_ref[p, seq + N_META:tp, :] = jnp.zeros((n_tail, LANES), BF16)

        def wgrad(dw_ref, width):
            for k in range(width):
                dw_ref[k:k + 1, :] = jnp.sum(acc_ref[8 * k:8 * k + 8, :], axis=0, keepdims=True)

        _to_seq(s_ref, hp_ref[1, xs, :] * hp_ref[2, xs, :], hp_ref[1, ms, :] * hp_ref[2, ms, :], seq)
        _conv_taps(s_ref, shs_ref, wa_ref, o_ref, ka, nseq, False)
        put(0, dya_ref[xs, :] * o_ref[ox, :], dya_ref[ms, :] * o_ref[om, :])
        _to_seq(d_ref, dya_ref[xs, :] * hp_ref[0, xs, :], dya_ref[ms, :] * hp_ref[0, ms, :], seq)
        _conv_wgrad(s_ref, shs_ref, d_ref, acc_ref, ka, nseq)
        wgrad(dwa_ref, ka)
        _conv_taps(d_ref, shd_ref, wa_ref, o_ref, ka, nseq, True)
        put(1, o_ref[ox, :] * hp_ref[2, xs, :], o_ref[om, :] * hp_ref[2, ms, :])
        put(2, o_ref[ox, :] * hp_ref[1, xs, :], o_ref[om, :] * hp_ref[1, ms, :])

        _to_seq(s_ref, hp_ref[3, xs, :] * _sig(hp_ref[4, xs, :]), hp_ref[3, ms, :] * _sig(hp_ref[4, ms, :]), seq)
        _to_seq(d_ref, dz_ref[xs, :], dz_ref[ms, :], seq)
        dbb_ref[...] = (jnp.sum(dz_ref[xs, :], axis=0, keepdims=True)
                        + jnp.sum(dz_ref[ms, :], axis=0, keepdims=True))
        _shift_copies(s_ref, shs_ref, kb, False)
        _conv_wgrad(s_ref, shs_ref, d_ref, acc_ref, kb, nseq)
        wgrad(dwb_ref, kb)
        _conv_taps(d_ref, shd_ref, wb_ref, o_ref, kb, nseq, True)
        sx, sm = _sig(hp_ref[4, xs, :]), _sig(hp_ref[4, ms, :])
        put(3, o_ref[ox, :] * sx, o_ref[om, :] * sm)
        put(4, o_ref[ox, :] * hp_ref[3, xs, :] * sx * (1.0 - sx), o_ref[om, :] * hp_ref[3, ms, :] * sm * (1.0 - sm))

    col = lambda j: (0, j)
    blk5 = pl.BlockSpec((5, tp, LANES), lambda j: (0, 0, j))
    return pl.pallas_call(
        body, name="mix_conv_bwd", grid=(wgrp // LANES,),
        in_specs=[blk5, pl.BlockSpec((tp, LANES), col), pl.BlockSpec((tp, LANES), col),
                  pl.BlockSpec((ka, LANES), col), pl.BlockSpec((kb, LANES), col)],
        out_specs=[blk5, pl.BlockSpec((ka, LANES), col), pl.BlockSpec((kb, LANES), col), pl.BlockSpec((1, LANES), col)],
        out_shape=[_sds((5, tp, wgrp), BF16), _sds((ka, wgrp), F32), _sds((kb, wgrp), F32), _sds((1, wgrp), F32)],
        scratch_shapes=[pltpu.VMEM((sb, LANES), F32), pltpu.VMEM((sb, LANES), F32), pltpu.VMEM((sb, LANES), F32),
                        pltpu.VMEM((SUBLANES * kb, LANES), F32), pltpu.VMEM((SUBLANES - 1, sb, LANES), F32),
                        pltpu.VMEM((SUBLANES - 1, sb, LANES), F32)],
        compiler_params=_params(("arbitrary",)),
    )(hp5, dya, dz, wa, wb)


def _grad_w_in(xn1, dhp5):
    n_p, tp, pw = dhp5.shape
    d = xn1.shape[1]

    def body(xn_ref, dhp_ref, g_ref):
        g_ref[...] = _dot(xn_ref[...], dhp_ref[0], TN).astype(BF16)

    return pl.pallas_call(
        body, name="grad_w_in", grid=(n_p,),
        in_specs=[_resident(xn1.shape), pl.BlockSpec((1, tp, pw), lambda p: (p, 0, 0))],
        out_specs=pl.BlockSpec((d, pw), lambda p: (0, p)),
        out_shape=_sds((d, n_p * pw), BF16),
        compiler_params=_params(("arbitrary",)),
    )(xn1, dhp5)


def _mix_bwd_in(dhp5, win4, h, dh1, g1, after):
    n_p, tp, pw = dhp5.shape
    d = h.shape[1]
    n_sh, _, csh = win4.shape
    tm = _row_tile(tp)

    seq, _ = _seq_rows(tp)
    last, meta_off = seq // tm, seq % tm
    assert last == tp // tm - 1
    assert any(rs.start <= meta_off and meta_off + N_META <= rs.stop for rs in _row_parts(tm))

    def body(dhp_ref, w_ref, h_ref, dh1_ref, g_ref, after_ref, gx_ref, dmeta_ref, dg1_ref, wcat_ref):
        i = pl.program_id(0)
        _concat_shards(w_ref, wcat_ref)

        @pl.when(i == 0)
        def _():
            dg1_ref[...] = jnp.zeros(dg1_ref.shape, F32)

        for rs in _row_parts(tm):
            dxn = _dot(dhp_ref[0, rs, :], wcat_ref[:, 0:pw], NT)
            for p in range(1, n_p):
                dxn = dxn + _dot(dhp_ref[p, rs, :], wcat_ref[:, p * pw:(p + 1) * pw], NT)
            hh = h_ref[rs, :]
            r1 = lax.rsqrt(_mean(hh * hh) + RMS_EPS)
            dres, dg_rows = _rms_bwd(dxn, hh, r1, g_ref[...])
            dh = dh1_ref[rs, :] + dres
            gx_ref[rs, :] = dh
            dg1_ref[...] += jnp.sum(dg_rows, axis=0, keepdims=True)
            if rs.start <= meta_off and meta_off + N_META <= rs.stop:
                @pl.when(i == last)
                def _():
                    dmeta_ref[...] = dh[meta_off - rs.start:meta_off - rs.start + N_META, :]

    row = lambda i: (i, 0)
    return pl.pallas_call(
        body, name="mix_bwd_in", grid=(tp // tm,),
        in_specs=[pl.BlockSpec((n_p, tm, pw), lambda i: (0, i, 0)), _resident(win4.shape), pl.BlockSpec((tm, d), row),
                  pl.BlockSpec((tm, d), row), _full(g1.shape), ANY],
        out_specs=[pl.BlockSpec((tm, d), row), _full((N_META, d)), _full((1, d))],
        out_shape=[_sds((seq, d), F32), _sds((N_META, d), F32), _sds((1, d), F32)],
        scratch_shapes=[pltpu.VMEM((d, n_sh * csh), BF16)],
        compiler_params=_params(("arbitrary",)),
    )(dhp5, win4, h, dh1, g1, after)


def _other_chips(x, y):
    out = []
    for j in (1, 2, 3):
        px, py = _flip(x, j >> 1), _flip(y, j & 1)
        out.append((px, py, 2 * px + py))
    return out


PAIR_COLLECTIVE_ID = 0


def _pair_barrier(x, y, c):
    sem = pltpu.get_barrier_semaphore()
    pl.semaphore_signal(sem, inc=1, device_id=(x, y, 1 - c), device_id_type=MESH)
    pl.semaphore_wait(sem, 1)


def _pair_params():
    return pltpu.CompilerParams(collective_id=PAIR_COLLECTIVE_ID)


def _half_rows(c, rows_half):
    return pl.ds(pl.multiple_of(c * rows_half, SUBLANES), rows_half)


def _cast_place(ws, q_arr, tag, after=None):
    n = len(ws)
    extra = [] if after is None else [after]

    def fits(steps):
        return all(w.shape[0] % steps == 0 and (w.shape[0] // steps) % BF16_ROWS == 0
                   and w.shape[0] // steps <= ROW_TILE_CAP for w in ws)

    steps = next(s for s in range(1, min(w.shape[0] for w in ws) + 1) if fits(s))

    def body(q_ref, *refs):
        for w_ref, out_ref in zip(refs[:n], refs[n + len(extra):]):
            out_ref[0] = w_ref[...].astype(BF16)

    return list(pl.pallas_call(
        body, name="cast_place_" + tag,
        grid_spec=pltpu.PrefetchScalarGridSpec(
            num_scalar_prefetch=1, grid=(steps,),
            in_specs=[pl.BlockSpec((w.shape[0] // steps, w.shape[1]), lambda i, q: (i, 0)) for w in ws] + [ANY] * len(extra),
            out_specs=[pl.BlockSpec((1, w.shape[0] // steps, w.shape[1]), lambda i, q: (q[0], i, 0)) for w in ws]),
        out_shape=[_sds((N_CHIPS,) + w.shape, BF16) for w in ws],
        compiler_params=_params(("arbitrary",)),
    )(q_arr, *ws, *extra))


HBM = pl.BlockSpec(memory_space=pltpu.HBM)
SEM = pl.BlockSpec(memory_space=pltpu.SEMAPHORE)
EFFECT = pltpu.SideEffectType.DATAFLOW_SIDE_EFFECTING


def _in_hbm(a):
    return pltpu.with_memory_space_constraint(a, pltpu.HBM)


def _gather_start(fulls, after, tag):
    n = len(fulls)
    halves = [a.shape[1] // 2 for a in fulls]

    def body(*refs):
        land = refs[:n]
        ssem, rsem = refs[n + 1], refs[n + 2]
        token = refs[-1]
        x, y, c = _mesh_pos()
        q = 2 * x + y
        for i in range(n):
            for j, (px, py, _) in enumerate(_other_chips(x, y)):
                mine = land[i].at[q, _half_rows(c, halves[i]), :]
                pltpu.make_async_remote_copy(src_ref=mine, dst_ref=mine, send_sem=ssem.at[3 * i + j],
                                             recv_sem=rsem.at[3 * i + j], device_id=(px, py, c), device_id_type=MESH).start()
        token[...] = jnp.zeros(token.shape, F32)

    outs = pl.pallas_call(
        body, name="gather_start_" + tag,
        in_specs=[HBM] * n + [ANY], out_specs=[SEM, SEM] + [HBM] * n + [VMEM],
        out_shape=[pltpu.SemaphoreType.DMA((3 * n,)), pltpu.SemaphoreType.DMA((3 * n,))]
        + [pltpu.HBM(a.shape, a.dtype) for a in fulls] + [_sds((SUBLANES, LANES), F32)],
        input_output_aliases={i: 2 + i for i in range(n)},
        compiler_params=pltpu.CompilerParams(has_side_effects=EFFECT),
    )(*[_in_hbm(a) for a in fulls], after)
    return outs[0], outs[1], list(outs[2:2 + n]), outs[-1]


def _gather_wait(which, ssem, rsem, lands, after, tag):
    m = len(which)
    halves = [a.shape[1] // 2 for a in lands]

    def body(*refs):
        land = refs[:m]
        ssem_, rsem_ = refs[m], refs[m + 1]
        x, y, c = _mesh_pos()
        for t, i in enumerate(which):
            for j, (px, py, qj) in enumerate(_other_chips(x, y)):
                rows = _half_rows(c, halves[t])
                cp = pltpu.make_async_remote_copy(src_ref=land[t].at[2 * x + y, rows, :], dst_ref=land[t].at[qj, rows, :],
                                                  send_sem=ssem_.at[3 * i + j], recv_sem=rsem_.at[3 * i + j],
                                                  device_id=(px, py, c), device_id_type=MESH)
                cp.wait_send()
                cp.wait_recv()

    outs = pl.pallas_call(
        body, name="gather_wait_" + tag,
        in_specs=[HBM] * m + [SEM, SEM, ANY], out_specs=[HBM] * m,
        out_shape=[pltpu.HBM(a.shape, a.dtype) for a in lands],
        input_output_aliases={i: i for i in range(m)},
        compiler_params=pltpu.CompilerParams(has_side_effects=EFFECT),
    )(*lands, ssem, rsem, after)
    return list(outs)


def _forward_pair(lands, tag):
    n = len(lands)
    halves = [a.shape[1] // 2 for a in lands]

    def body(*refs):
        full = refs[n:2 * n]
        ssem, rsem = refs[2 * n:]
        x, y, c = _mesh_pos()
        _pair_barrier(x, y, c)
        cps = []
        for i in range(n):
            for j, (_, _, qj) in enumerate(_other_chips(x, y)):
                part = full[i].at[qj, _half_rows(c, halves[i]), :]
                cp = pltpu.make_async_remote_copy(src_ref=part, dst_ref=part, send_sem=ssem.at[3 * i + j],
                                                  recv_sem=rsem.at[3 * i + j], device_id=(x, y, 1 - c), device_id_type=MESH)
                cp.start()
                cps.append(cp)
        for cp in cps:
            cp.wait()

    return pl.pallas_call(
        body, name="forward_pair_" + tag,
        in_specs=[ANY] * n, out_specs=[ANY] * n,
        out_shape=[_sds(a.shape, a.dtype) for a in lands],
        input_output_aliases={i: i for i in range(n)},
        scratch_shapes=[pltpu.SemaphoreType.DMA((3 * n,)), pltpu.SemaphoreType.DMA((3 * n,))],
        compiler_params=_pair_params(),
    )(*lands)


def _chip_exchange_start(parts, after, tag):
    n = len(parts)

    def body(*refs):
        src, land = refs[:n], refs[n:2 * n]
        ssem, rsem = refs[2 * n + 1], refs[2 * n + 2]
        token = refs[-1]
        x, y, c = _mesh_pos()
        for i in range(n):
            for j, (px, py, qj) in enumerate(_other_chips(x, y)):
                pltpu.make_async_remote_copy(src_ref=src[i].at[qj], dst_ref=land[i].at[j], send_sem=ssem.at[3 * i + j],
                                             recv_sem=rsem.at[3 * i + j], device_id=(px, py, c), device_id_type=MESH).start()
        token[...] = jnp.zeros(token.shape, F32)

    lands = [lax.empty((3,) + a.shape[1:], a.dtype) for a in parts]
    outs = pl.pallas_call(
        body, name="chip_exchange_start_" + tag,
        in_specs=[HBM] * (2 * n) + [ANY], out_specs=[SEM, SEM] + [HBM] * (2 * n) + [VMEM],
        out_shape=[pltpu.SemaphoreType.DMA((3 * n,)), pltpu.SemaphoreType.DMA((3 * n,))]
        + [pltpu.HBM(a.shape, a.dtype) for a in parts] + [pltpu.HBM(a.shape, a.dtype) for a in lands]
        + [_sds((SUBLANES, LANES), F32)],
        input_output_aliases={i: 2 + i for i in range(2 * n)},
        compiler_params=pltpu.CompilerParams(has_side_effects=EFFECT),
    )(*[_in_hbm(a) for a in parts], *[_in_hbm(a) for a in lands], after)
    return outs[0], outs[1], list(outs[2:2 + n]), list(outs[2 + n:2 + 2 * n]), outs[-1]


def _chip_exchange_wait(ssem, rsem, parts, lands, after, tag):
    n = len(parts)

    def body(*refs):
        src, land = refs[:n], refs[n:2 * n]
        ssem_, rsem_ = refs[2 * n], refs[2 * n + 1]
        x, y, c = _mesh_pos()
        for i in range(n):
            for j, (px, py, qj) in enumerate(_other_chips(x, y)):
                cp = pltpu.make_async_remote_copy(src_ref=src[i].at[qj], dst_ref=land[i].at[j], send_sem=ssem_.at[3 * i + j],
                                                  recv_sem=rsem_.at[3 * i + j], device_id=(px, py, c), device_id_type=MESH)
                cp.wait_send()
                cp.wait_recv()

    outs = pl.pallas_call(
        body, name="chip_exchange_wait_" + tag,
        in_specs=[HBM] * (2 * n) + [SEM, SEM, ANY], out_specs=[HBM] * (2 * n),
        out_shape=[pltpu.HBM(a.shape, a.dtype) for a in parts] + [pltpu.HBM(a.shape, a.dtype) for a in lands],
        input_output_aliases={i: i for i in range(2 * n)},
        compiler_params=pltpu.CompilerParams(has_side_effects=EFFECT),
    )(*parts, *lands, ssem, rsem, after)
    return list(outs[:n]), list(outs[n:])


def _grad_half(ref, shape, axis, which):
    rows = shape[axis] // 2
    if axis == 0:
        return ref.at[_half_rows(which, rows), :]
    return ref.at[:, _half_rows(which, rows), :]


def _half_shape(a, axis):
    s = list(a.shape)
    s[axis] //= 2
    return tuple(s)


def _pair_exchange_start(grads, half_axis, after, tag):
    n = len(grads)

    def body(*refs):
        g, land = refs[:n], refs[n:2 * n]
        ssem, rsem = refs[2 * n + 1], refs[2 * n + 2]
        token = refs[-1]
        x, y, c = _mesh_pos()
        for i in range(n):
            pltpu.make_async_remote_copy(src_ref=_grad_half(g[i], grads[i].shape, half_axis[i], 1 - c), dst_ref=land[i],
                                         send_sem=ssem.at[i], recv_sem=rsem.at[i], device_id=(x, y, 1 - c),
                                         device_id_type=MESH).start()
        token[...] = jnp.zeros(token.shape, F32)

    lands = [lax.empty(_half_shape(a, half_axis[i]), a.dtype) for i, a in enumerate(grads)]
    outs = pl.pallas_call(
        body, name="pair_exchange_start_" + tag,
        in_specs=[HBM] * (2 * n) + [ANY], out_specs=[SEM, SEM] + [HBM] * (2 * n) + [VMEM],
        out_shape=[pltpu.SemaphoreType.DMA((n,)), pltpu.SemaphoreType.DMA((n,))]
        + [pltpu.HBM(a.shape, a.dtype) for a in grads] + [pltpu.HBM(a.shape, a.dtype) for a in lands]
        + [_sds((SUBLANES, LANES), F32)],
        input_output_aliases={i: 2 + i for i in range(2 * n)},
        compiler_params=pltpu.CompilerParams(has_side_effects=EFFECT),
    )(*[_in_hbm(a) for a in grads], *[_in_hbm(a) for a in lands], after)
    return outs[0], outs[1], list(outs[2:2 + n]), list(outs[2 + n:2 + 2 * n]), outs[-1]


def _pair_exchange_wait(ssem, rsem, grads, lands, half_axis, after, tag):
    n = len(grads)

    def body(*refs):
        g, land = refs[:n], refs[n:2 * n]
        ssem_, rsem_ = refs[2 * n], refs[2 * n + 1]
        x, y, c = _mesh_pos()
        for i in range(n):
            cp = pltpu.make_async_remote_copy(src_ref=_grad_half(g[i], grads[i].shape, half_axis[i], 1 - c),
                                              dst_ref=land[i], send_sem=ssem_.at[i], recv_sem=rsem_.at[i],
                                              device_id=(x, y, 1 - c), device_id_type=MESH)
            cp.wait_send()
            cp.wait_recv()

    outs = pl.pallas_call(
        body, name="pair_exchange_wait_" + tag,
        in_specs=[HBM] * (2 * n) + [SEM, SEM, ANY], out_specs=[HBM] * (2 * n),
        out_shape=[pltpu.HBM(a.shape, a.dtype) for a in grads] + [pltpu.HBM(a.shape, a.dtype) for a in lands],
        input_output_aliases={i: i for i in range(2 * n)},
        compiler_params=pltpu.CompilerParams(has_side_effects=EFFECT),
    )(*grads, *lands, ssem, rsem, after)
    return list(outs[:n]), list(outs[n:])


def _pair_exchange_grads(grads, half_axis, tag):
    n = len(grads)

    def body(*refs):
        g, got = refs[:n], refs[n:2 * n]
        ssem, rsem = refs[2 * n:]
        x, y, c = _mesh_pos()
        _pair_barrier(x, y, c)
        cps = []
        for i in range(n):
            cp = pltpu.make_async_remote_copy(src_ref=_grad_half(g[i], grads[i].shape, half_axis[i], 1 - c),
                                              dst_ref=got[i], send_sem=ssem.at[i], recv_sem=rsem.at[i],
                                              device_id=(x, y, 1 - c), device_id_type=MESH)
            cp.start()
            cps.append(cp)
        for cp in cps:
            cp.wait()

    return pl.pallas_call(
        body, name="pair_exchange_grads_" + tag,
        in_specs=[ANY] * n, out_specs=[ANY] * n,
        out_shape=[_sds(_half_shape(a, half_axis[i]), a.dtype) for i, a in enumerate(grads)],
        scratch_shapes=[pltpu.SemaphoreType.DMA((n,)), pltpu.SemaphoreType.DMA((n,))],
        compiler_params=_pair_params(),
    )(*grads)


def _pair_sum(gs, gots, c_arr, col_sharded, tag):
    n = len(gs)
    g_specs, got_specs, out_specs, out_shapes = [], [], [], []
    for g, by_cols in zip(gs, col_sharded):
        if by_cols:
            rows, cols = g.shape
            rh, cs = rows // 2, cols // N_CHIPS
            g_specs.append(pl.BlockSpec((rh, cs), lambda k, c_ref: (c_ref[0], k)))
            got_specs.append(pl.BlockSpec((rh, cs), lambda k, c_ref: (0, k)))
        else:
            _, rows, cs = g.shape
            rh = rows // 2
            g_specs.append(pl.BlockSpec((1, rh, cs), lambda k, c_ref: (k, c_ref[0], 0)))
            got_specs.append(pl.BlockSpec((1, rh, cs), lambda k, c_ref: (k, 0, 0)))
        out_specs.append(pl.BlockSpec((1, rh, cs), lambda k, c_ref: (k, 0, 0)))
        out_shapes.append(_sds((N_CHIPS, rh, cs), BF16))

    def body(c_ref, *refs):
        for g_ref, got_ref, out_ref in zip(refs[:n], refs[n:2 * n], refs[2 * n:]):
            total = g_ref[...].astype(F32) + got_ref[...].astype(F32)
            out_ref[...] = total.astype(BF16).reshape(out_ref.shape)

    return list(pl.pallas_call(
        body, name="pair_sum_" + tag,
        grid_spec=pltpu.PrefetchScalarGridSpec(
            num_scalar_prefetch=1, grid=(N_CHIPS,), in_specs=g_specs + got_specs, out_specs=out_specs),
        out_shape=out_shapes,
        compiler_params=_params(("arbitrary",)),
    )(c_arr, *gs, *gots))


def _chip_sum(parts, gots, qc_arr, tag):
    n = len(parts)
    steps = 2 if all(p.shape[1] % 32 == 0 for p in parts) else 1
    part_specs, got_specs, out_specs, out_shapes = [], [], [], []
    for p in parts:
        _, rh, cs = p.shape
        rb = rh // steps
        part_specs.append(pl.BlockSpec((1, rb, cs), lambda i, qc: (qc[0], i, 0)))
        got_specs.append(pl.BlockSpec((3, rb, cs), lambda i, qc: (0, i, 0)))
        out_specs.append(pl.BlockSpec((rb, cs), lambda i, qc: (qc[1] * steps + i, 0)))
        out_shapes.append(_sds((2 * rh, cs), F32))

    def body(qc_ref, *refs):
        for part_ref, got_ref, out_ref in zip(refs[:n], refs[n:2 * n], refs[2 * n:]):
            total = part_ref[0].astype(F32)
            for j in range(3):
                total = total + got_ref[j].astype(F32)
            out_ref[...] = total

    return list(pl.pallas_call(
        body, name="chip_sum_" + tag,
        grid_spec=pltpu.PrefetchScalarGridSpec(
            num_scalar_prefetch=1, grid=(steps,), in_specs=part_specs + got_specs, out_specs=out_specs),
        out_shape=out_shapes,
        compiler_params=_params(("arbitrary",)),
    )(qc_arr, *parts, *gots))


def _pair_share_grads(grads, tag):
    n = len(grads)

    def body(*refs):
        g = refs[n:2 * n]
        ssem, rsem = refs[2 * n:]
        x, y, c = _mesh_pos()
        _pair_barrier(x, y, c)
        cps = []
        for i in range(n):
            mine = g[i].at[_half_rows(c, grads[i].shape[0] // 2), :]
            cp = pltpu.make_async_remote_copy(src_ref=mine, dst_ref=mine, send_sem=ssem.at[i], recv_sem=rsem.at[i],
                                              device_id=(x, y, 1 - c), device_id_type=MESH)
            cp.start()
            cps.append(cp)
        for cp in cps:
            cp.wait()

    return pl.pallas_call(
        body, name="pair_share_grads_" + tag,
        in_specs=[ANY] * n, out_specs=[ANY] * n,
        out_shape=[_sds(a.shape, a.dtype) for a in grads],
        input_output_aliases={i: i for i in range(n)},
        scratch_shapes=[pltpu.SemaphoreType.DMA((n,)), pltpu.SemaphoreType.DMA((n,))],
        compiler_params=_pair_params(),
    )(*grads)


def _small_allreduce(parts, places, rows_total, width, after):
    n = len(parts)

    def body(*refs):
        ins, out_ref = refs[:n], refs[n + 1]
        pack, pair_got, chip_sum, got, ssem, rsem = refs[n + 2:]
        x, y, c = _mesh_pos()
        chip = 2 * x + y
        pack[...] = jnp.zeros(pack.shape, F32)
        for i in range(n):
            for row, col, src_row, rows in places[i]:
                w = parts[i].shape[1]
                pack[row:row + rows, col:col + w] = ins[i][src_row:src_row + rows, :]
        swap = pltpu.make_async_remote_copy(src_ref=pack, dst_ref=pair_got, send_sem=ssem.at[3], recv_sem=rsem.at[3],
                                            device_id=(x, y, 1 - c), device_id_type=MESH)
        swap.start()
        swap.wait()
        chip_sum[...] = pack[...] + pair_got[...]
        cps = []
        for j, (px, py, _) in enumerate(_other_chips(x, y)):
            cp = pltpu.make_async_remote_copy(src_ref=chip_sum, dst_ref=got.at[j], send_sem=ssem.at[j],
                                              recv_sem=rsem.at[j], device_id=(px, py, c), device_id_type=MESH)
            cp.start()
            cps.append(cp)
        for cp in cps:
            cp.wait()
        total = jnp.zeros(pack.shape, F32)
        for q in range(N_CHIPS):
            rel = jnp.bitwise_xor(chip, q)
            theirs = got[jnp.maximum(rel - 1, 0)]
            total = total + jnp.where(rel == 0, chip_sum[...], theirs)
        out_ref[...] = total

    return pl.pallas_call(
        body, name="small_allreduce",
        in_specs=[VMEM] * n + [ANY], out_specs=VMEM,
        out_shape=_sds((rows_total, width), F32),
        scratch_shapes=[pltpu.VMEM((rows_total, width), F32), pltpu.VMEM((rows_total, width), F32),
                        pltpu.VMEM((rows_total, width), F32), pltpu.VMEM((3, rows_total, width), F32),
                        pltpu.SemaphoreType.DMA((4,)), pltpu.SemaphoreType.DMA((4,))],
        compiler_params=_params(),
    )(*parts, after)


def _small_update(red, q_arr, takes, loss_at, ws, ms, vs):
    n_w = len(ws)

    def body(q_ref, red_ref, *refs):
        w_in, m_in, v_in = refs[0:n_w], refs[n_w:2 * n_w], refs[2 * n_w:3 * n_w]
        outs = refs[3 * n_w:]
        g_out, d_out, m_out, v_out = (outs[0:n_w], outs[n_w:2 * n_w], outs[2 * n_w:3 * n_w], outs[3 * n_w:4 * n_w])
        loss_ref = outs[4 * n_w]
        chip = q_ref[0]

        def put(g_ref, d0, nr, s0, lo, w):
            if len(g_ref.shape) == 3:
                for r in range(nr):
                    g_ref[d0 + r] = red_ref[s0 + r:s0 + r + 1, lo:lo + w]
            else:
                g_ref[d0:d0 + nr, :] = red_ref[s0:s0 + nr, lo:lo + w]

        def take_own_columns(g_ref, d0, nr, s0, c0, w):
            for k in range(N_CHIPS):
                @pl.when(chip == k)
                def _():
                    put(g_ref, d0, nr, s0, c0 + k * w, w)

        for j in range(n_w):
            w = ws[j].shape[-1]
            for d0, nr, s0, c0, sharded in takes[j]:
                if sharded:
                    take_own_columns(g_out[j], d0, nr, s0, c0, w)
                else:
                    put(g_out[j], d0, nr, s0, c0, w)
            d_out[j][...], m_out[j][...], v_out[j][...] = _adamw_math(w_in[j][...], g_out[j][...], m_in[j][...], v_in[j][...])
        loss_ref[...] = red_ref[loss_at[0]:loss_at[0] + 1, loss_at[1]:loss_at[1] + LANES]

    shapes = [_sds(w.shape, F32) for w in ws]
    outs = pl.pallas_call(
        body, name="small_update",
        in_specs=[pl.BlockSpec(memory_space=pltpu.SMEM)] + [VMEM] * (1 + 3 * n_w), out_specs=[VMEM] * (4 * n_w + 1),
        out_shape=shapes * 4 + [_sds((1, LANES), F32)],
        compiler_params=_params(),
    )(q_arr, red, *ws, *ms, *vs)
    return outs[0:n_w], outs[n_w:2 * n_w], outs[2 * n_w:3 * n_w], outs[3 * n_w:4 * n_w], outs[4 * n_w]


def _adamw_math(w, g, m, v):
    m2 = ADAM_B1 * m + (1.0 - ADAM_B1) * g
    v2 = ADAM_B2 * v + (1.0 - ADAM_B2) * (g * g)
    m_hat = m2 / (1.0 - ADAM_B1 ** ADAM_STEP)
    v_hat = v2 / (1.0 - ADAM_B2 ** ADAM_STEP)
    delta = -ADAM_LR * (m_hat / (jnp.sqrt(v_hat) + ADAM_EPS) + ADAM_WD * w)
    return delta, m2, v2


ADAMW_BLOCK_BYTES = 3 * 2 ** 19


def _adamw_big(ws, gs, ms, vs, tag):
    n = len(ws)

    def fits(steps):
        return all(w.shape[0] % steps == 0 and (w.shape[0] // steps) % SUBLANES == 0
                   and (w.shape[0] // steps) * w.shape[1] * 4 * n <= ADAMW_BLOCK_BYTES for w in ws)

    steps = next(s for s in range(1, min(w.shape[0] for w in ws) + 1) if fits(s))
    specs = [pl.BlockSpec((w.shape[0] // steps, w.shape[1]), lambda i: (i, 0)) for w in ws]

    def body(*refs):
        ins, outs = refs[:4 * n], refs[4 * n:]
        for i in range(n):
            w_ref, g_ref, m_ref, v_ref = ins[i], ins[n + i], ins[2 * n + i], ins[3 * n + i]
            gg = g_ref[...]
            outs[4 * i][...] = gg
            outs[4 * i + 1][...], outs[4 * i + 2][...], outs[4 * i + 3][...] = _adamw_math(
                w_ref[...], gg, m_ref[...], v_ref[...])

    outs = pl.pallas_call(
        body, name="adamw_" + tag, grid=(steps,),
        in_specs=specs * 4, out_specs=[s for s in specs for _ in range(4)],
        out_shape=[_sds(w.shape, F32) for w in ws for _ in range(4)],
        compiler_params=_params(("arbitrary",)),
    )(*ws, *gs, *ms, *vs)
    return [outs[4 * i:4 * i + 4] for i in range(n)]


SMALL_ROWS = 40
PACK_ROWS = 64


def kernel(x, meta_tokens, pre_mix_norm, w_in, conv_a_w, conv_b_w, conv_b_bias, ln_b_gain, ln_b_bias, w_out, post_mix_norm, pre_ffn_norm, w_gate, w_up, w_down, post_ffn_norm, loss_target, m_meta_tokens, m_pre_mix_norm, m_w_in, m_conv_a_w, m_conv_b_w, m_conv_b_bias, m_ln_b_gain, m_ln_b_bias, m_w_out, m_post_mix_norm, m_pre_ffn_norm, m_w_gate, m_w_up, m_w_down, m_post_ffn_norm, v_meta_tokens, v_pre_mix_norm, v_w_in, v_conv_a_w, v_conv_b_w, v_conv_b_bias, v_ln_b_gain, v_ln_b_bias, v_w_out, v_post_mix_norm, v_pre_ffn_norm, v_w_gate, v_w_up, v_w_down, v_post_ffn_norm):
    xq, yq, cq = lax.axis_index("x"), lax.axis_index("y"), lax.axis_index("c")
    chip = 2 * xq + yq
    c_arr = jnp.reshape(cq, (1,)).astype(jnp.int32)
    qc_arr = jnp.stack([chip, cq]).astype(jnp.int32)

    seq, d = x.shape[1], x.shape[2]
    x2, tgt2 = x[0], loss_target[0]
    tr = lambda a: jnp.swapaxes(a, 1, 2)[0]
    w_in2, w_out2, w_gate2, w_up2, w_down2 = w_in[0], w_out[0], tr(w_gate), tr(w_up), w_down[0]
    ka, wa_sh = conv_a_w.shape[1], conv_a_w.shape[2]
    kb = conv_b_w.shape[1]
    meta_sh = meta_tokens.shape[1]

    small = jnp.zeros((PACK_ROWS, meta_sh), F32)
    small = small.at[0:N_META, :].set(meta_tokens)
    small = small.at[16:16 + ka, 0:wa_sh].set(conv_a_w[0])
    small = small.at[24:24 + kb, 0:wa_sh].set(conv_b_w[0])
    q_arr = jnp.reshape(chip, (1,)).astype(jnp.int32)
    small_own = lax.dynamic_update_slice(jnp.zeros((N_CHIPS, PACK_ROWS, meta_sh), F32), small[None], (chip, 0, 0))
    i_ssem, i_rsem, first, i_token = _gather_start(_cast_place([w_in2], q_arr, "w_in") + [small_own], pre_mix_norm, "in")
    rest = _cast_place([w_out2, w_gate2, w_up2, w_down2], q_arr, "rest", i_token)
    g_ssem, g_rsem, lands, g_token = _gather_start(rest, i_token, "rest")
    win4, small4 = _forward_pair(_gather_wait([0, 1], i_ssem, i_rsem, first, g_token, "in"), "in")
    meta_f = jnp.concatenate([small4[k, 0:N_META, :] for k in range(N_CHIPS)], axis=1)
    wa_f = jnp.concatenate([small4[k, 16:16 + ka, 0:wa_sh] for k in range(N_CHIPS)], axis=1)
    wb_f = jnp.concatenate([small4[k, 24:24 + kb, 0:wa_sh] for k in range(N_CHIPS)], axis=1)

    tm = _row_tile(seq + TAIL_ROWS)
    tail = lax.dynamic_update_slice(jnp.zeros((tm, d), F32), meta_f, (seq % tm, 0))
    h, xn1, hp5 = _mm_in(x2, tail, win4, pre_mix_norm, g_token)
    ya, z = _mix_conv_fwd(hp5, wa_f, wb_f, conv_b_bias)
    (wout4,) = _forward_pair(_gather_wait([0], g_ssem, g_rsem, lands[0:1], z, "out"), "out")
    wout_f = wout4.reshape(N_CHIPS * wout4.shape[1], wout4.shape[2])
    yb, mix, h1, xn2 = _mm_out(ya, z, h, wout_f, ln_b_gain, ln_b_bias, post_mix_norm, pre_ffn_norm)
    wg4, wu4 = _forward_pair(_gather_wait([1, 2], g_ssem, g_rsem, lands[1:3], xn2, "gate_up"), "gate_up")
    stacked = lambda a: a.reshape(a.shape[0] * a.shape[1], a.shape[2])
    wg_f, wu_f = stacked(wg4), stacked(wu4)
    p_act, q_act, f_act = _ffn_up(xn2, wg_f, wu_f)
    (wd4,) = _forward_pair(_gather_wait([3], g_ssem, g_rsem, lands[3:4], f_act, "down"), "down")
    wd_f = stacked(wd4)
    dff, dh2, loss_blk, d_gpf = _ffn_down(f_act, wd_f, h1, tgt2, post_ffn_norm)

    da, du = _ffn_bwd_act(dff, wd_f, p_act, q_act)
    by_chip = lambda g: g.reshape(N_CHIPS, g.shape[0] // N_CHIPS, g.shape[1])
    g_down = by_chip(_grad_w_down(f_act, dff))
    g_gate, g_up = [by_chip(g) for g in _grad_w_gate_up(xn2, da, du)]
    ffn = [g_gate, g_up, g_down]
    p_ssem, p_rsem, ffn, p_lands, p_token = _pair_exchange_start(ffn, [1, 1, 1], dff, "ffn")
    dh1, dmix, d_g2, d_gpm = _ffn_bwd_in(da, du, wg_f, wu_f, h1, mix, dh2, pre_ffn_norm, post_mix_norm, p_token)
    ffn, got = _pair_exchange_wait(p_ssem, p_rsem, ffn, p_lands, [1, 1, 1], d_g2, "ffn")
    parts = _pair_sum(ffn, got, c_arr, [False] * 3, "ffn")
    f_ssem, f_rsem, parts, f_lands, f_token = _chip_exchange_start(parts, dff, "ffn")
    g_out = _grad_w_out(ya, yb, dmix, f_token)
    dya, dz, d_lg, d_lb = _mix_bwd_out(dmix, wout_f, z, ln_b_gain, ln_b_bias, f_token)
    dhp5, d_wa, d_wb, d_bb = _mix_conv_bwd(hp5, dya, dz, wa_f, wb_f)
    g_in = _grad_w_in(xn1, dhp5)

    g_out4 = g_out.reshape(N_CHIPS, g_out.shape[0] // N_CHIPS, g_out.shape[1])
    mixw = [g_in, g_out4]
    got2 = _pair_exchange_grads(mixw, [0, 1], "mix")
    parts2 = _pair_sum(mixw, got2, c_arr, [True, False], "mix")
    m_ssem, m_rsem, parts2, m_lands, m_token = _chip_exchange_start(parts2, dhp5, "mix")
    grad_x2, d_meta, d_g1 = _mix_bwd_in(dhp5, win4, h, dh1, pre_mix_norm, m_token)
    grad_x = grad_x2[None]

    parts, f_recv = _chip_exchange_wait(f_ssem, f_rsem, parts, f_lands, d_g1, "ffn")
    halves = _chip_sum(parts, f_recv, qc_arr, "ffn")
    gsum_ffn = _pair_share_grads(halves, "ffn")

    names_big = ["w_in", "w_out", "w_gate", "w_up", "w_down"]
    w_big = dict(zip(names_big, [w_in2, w_out2, w_gate2, w_up2, w_down2]))
    m_big = dict(zip(names_big, [m_w_in[0], m_w_out[0], tr(m_w_gate), tr(m_w_up), m_w_down[0]]))
    v_big = dict(zip(names_big, [v_w_in[0], v_w_out[0], tr(v_w_gate), tr(v_w_up), v_w_down[0]]))
    grads, deltas, new_m, new_v = {}, {}, {}, {}

    def update(names, gs, tag):
        res = _adamw_big([w_big[k] for k in names], gs, [m_big[k] for k in names], [v_big[k] for k in names], tag)
        for nm, outs in zip(names, res):
            if nm in ("w_gate", "w_up"):
                outs = [jnp.swapaxes(o[None], 1, 2) for o in outs]
            else:
                outs = [o[None] for o in outs]
            grads[nm], deltas[nm], new_m[nm], new_v[nm] = outs
        return res[-1][1]

    last = update(["w_gate", "w_up", "w_down"], list(gsum_ffn), "ffn")

    hw = d // 2
    assert d_wa.shape == (3, hw) and d_wb.shape == (31, hw) and d_bb.shape == (1, hw)
    small_parts = [d_meta, d_g1, d_gpm, d_g2, d_gpf, d_bb, d_lg, d_lb, loss_blk[0:1, :], d_wa, d_wb]
    places = [[(0, 0, 0, N_META)], [(16, 0, 0, 1)], [(17, 0, 0, 1)], [(18, 0, 0, 1)], [(19, 0, 0, 1)],
              [(20, 0, 0, 1)], [(20, hw, 0, 1)], [(21, 0, 0, 1)], [(21, hw, 0, 1)], [(22, 0, 0, 3)],
              [(22, hw, 0, 3), (25, 0, 3, 14), (25, hw, 17, 14)]]
    names_small = ["meta_tokens", "pre_mix_norm", "conv_a_w", "conv_b_w", "conv_b_bias", "ln_b_gain", "ln_b_bias",
                   "post_mix_norm", "pre_ffn_norm", "post_ffn_norm"]
    takes = [[(0, N_META, 0, 0, True)], [(0, 1, 16, 0, False)], [(0, 3, 22, 0, True)],
             [(0, 3, 22, hw, True), (3, 14, 25, 0, True), (17, 14, 25, hw, True)], [(0, 1, 20, 0, False)],
             [(0, 1, 20, hw, False)], [(0, 1, 21, 0, False)], [(0, 1, 17, 0, False)], [(0, 1, 18, 0, False)],
             [(0, 1, 19, 0, False)]]
    taps = lambda a: jnp.swapaxes(a, 0, 1)
    w_small = [meta_tokens, pre_mix_norm, taps(conv_a_w), taps(conv_b_w), conv_b_bias, ln_b_gain, ln_b_bias, post_mix_norm,
               pre_ffn_norm, post_ffn_norm]
    m_small = [m_meta_tokens, m_pre_mix_norm, taps(m_conv_a_w), taps(m_conv_b_w), m_conv_b_bias, m_ln_b_gain, m_ln_b_bias,
               m_post_mix_norm, m_pre_ffn_norm, m_post_ffn_norm]
    v_small = [v_meta_tokens, v_pre_mix_norm, taps(v_conv_a_w), taps(v_conv_b_w), v_conv_b_bias, v_ln_b_gain, v_ln_b_bias,
               v_post_mix_norm, v_pre_ffn_norm, v_post_ffn_norm]
    red = _small_allreduce(small_parts, places, SMALL_ROWS, d, last)
    g_s, d_s, m_s, v_s, loss_row = _small_update(red, q_arr, takes, (21, hw), w_small, m_small, v_small)
    loss = loss_row[0, 0]
    for i, nm in enumerate(names_small):
        fix = taps if nm in ("conv_a_w", "conv_b_w") else (lambda a: a)
        grads[nm], deltas[nm], new_m[nm], new_v[nm] = fix(g_s[i]), fix(d_s[i]), fix(m_s[i]), fix(v_s[i])

    parts2, m_recv = _chip_exchange_wait(m_ssem, m_rsem, parts2, m_lands, loss_row, "mix")
    halves2 = _chip_sum(parts2, m_recv, qc_arr, "mix")
    gsum_mix = _pair_share_grads(halves2, "mix")
    update(["w_in", "w_out"], list(gsum_mix), "mix")

    order = ["meta_tokens", "pre_mix_norm", "w_in", "conv_a_w", "conv_b_w", "conv_b_bias", "ln_b_gain", "ln_b_bias", "w_out",
             "post_mix_norm", "pre_ffn_norm", "w_gate", "w_up", "w_down", "post_ffn_norm"]
    return (loss, grad_x, *[grads[k] for k in order], *[deltas[k] for k in order], *[new_m[k] for k in order],
            *[new_v[k] for k in order])
```

```python
import jax
import jax.numpy as jnp
from jax import lax
from jax.experimental import pallas as pl
from jax.experimental.pallas import tpu as pltpu

F32 = jnp.float32
BF16 = jnp.bfloat16
MESH = pl.DeviceIdType.MESH

N_META = 16
TAIL_ROWS = 128
RMS_EPS = 1e-6
LN_EPS = 1e-5
ADAM_LR = 0.001
ADAM_B1 = 0.9
ADAM_B2 = 0.999
ADAM_EPS = 1e-08
ADAM_WD = 0.01
ADAM_STEP = 10

N_CHIPS = 4
LANES = 128
SUBLANES = 8
BF16_ROWS = 16
MXU_TILE = 256
CONV_CHUNK = 48
CONV_HIST = 32
ROW_TILE_CAP = 640
VMEM_LIMIT = 56 * 1024 * 1024

NN = (((1,), (0,)), ((), ()))
NT = (((1,), (1,)), ((), ()))
TN = (((0,), (0,)), ((), ()))


def _dot(a, b, dims=NN):
    return lax.dot_general(a, b, dims, preferred_element_type=F32)


def _sig(v):
    return 1.0 / (1.0 + jnp.exp(-v))


def _mean(v):
    return jnp.mean(v, axis=-1, keepdims=True)


def _row_tile(rows):
    best = BF16_ROWS
    for t in range(BF16_ROWS, min(rows, ROW_TILE_CAP) + 1, BF16_ROWS):
        if rows % t == 0:
            best = t
    assert rows % best == 0
    return best


def _row_parts(tm, parts=2):
    units = tm // BF16_ROWS
    if tm % BF16_ROWS or units < parts:
        return [slice(0, tm)]
    cuts = [BF16_ROWS * ((units * k + parts - 1) // parts) for k in range(parts + 1)]
    return [slice(lo, hi) for lo, hi in zip(cuts[:-1], cuts[1:])]


def _concat_shards(w_ref, wcat_ref):
    n_sh, _, csh = w_ref.shape

    @pl.when(pl.program_id(0) == 0)
    def _():
        for k in range(n_sh):
            wcat_ref[:, k * csh:(k + 1) * csh] = w_ref[k]


def _params(semantics=None):
    kw = dict(vmem_limit_bytes=VMEM_LIMIT)
    if semantics is not None:
        kw["dimension_semantics"] = semantics
    return pltpu.CompilerParams(**kw)


def _full(shape):
    nd = len(shape)
    return pl.BlockSpec(shape, lambda *_: (0,) * nd)


def _resident(shape):
    nd = len(shape)
    return pl.BlockSpec(shape, lambda *_: (0,) * nd, pipeline_mode=pl.Buffered(1))


def _sds(shape, dtype):
    return jax.ShapeDtypeStruct(shape, dtype)


ANY = pl.BlockSpec(memory_space=pl.ANY)
VMEM = pl.BlockSpec(memory_space=pltpu.VMEM)


def _mesh_pos():
    return lax.axis_index("x"), lax.axis_index("y"), lax.axis_index("c")


def _flip(v, bit):
    return 1 - v if bit else v


def _mm_in(x, tail, win4, g1, after):
    seq, d = x.shape
    tp = seq + TAIL_ROWS
    tm = _row_tile(tp)
    n_sh, _, csh = win4.shape
    pw = n_sh * csh // 5

    def body(x_ref, tail_ref, w_ref, g_ref, after_ref, h_ref, xn_ref, hp_ref, wcat_ref):
        _concat_shards(w_ref, wcat_ref)
        rows = pl.program_id(0) * tm + lax.broadcasted_iota(jnp.int32, (tm, 1), 0)
        hh = jnp.where(rows < seq, x_ref[...], tail_ref[...])
        h_ref[...] = hh
        r = lax.rsqrt(_mean(hh * hh) + RMS_EPS)
        xn = (hh * r * g_ref[...]).astype(BF16)
        xn_ref[...] = xn
        for p in range(5):
            hp_ref[p] = _dot(xn, wcat_ref[:, p * pw:(p + 1) * pw])

    row = pl.BlockSpec((tm, d), lambda i: (i, 0))
    return pl.pallas_call(
        body, name="mm_in", grid=(tp // tm,),
        in_specs=[row, _full(tail.shape), _resident(win4.shape), _full(g1.shape), ANY],
        out_specs=[row, row, pl.BlockSpec((5, tm, pw), lambda i: (0, i, 0))],
        out_shape=[_sds((tp, d), F32), _sds((tp, d), BF16), _sds((5, tp, pw), F32)],
        scratch_shapes=[pltpu.VMEM((d, n_sh * csh), BF16)],
        compiler_params=_params(("arbitrary",)),
    )(x, tail, win4, g1, after)


def _seq_rows(tp):
    seq = tp - TAIL_ROWS
    nseq = seq + N_META
    assert nseq % CONV_CHUNK == 0 and seq % BF16_ROWS == 0
    return seq, nseq


def _conv_offsets(width, transpose):
    return [(width - 1 - k) if transpose else (CONV_HIST - (width - 1) + k) for k in range(width)]


def _shift_copies(src_ref, sh_ref, width, transpose):
    n = src_ref.shape[0] - SUBLANES
    for s in sorted({o % SUBLANES for o in _conv_offsets(width, transpose)} - {0}):
        sh_ref[s - 1, 0:n, :] = src_ref[s:s + n, :]


def _tap_rows(src_ref, sh_ref, base, off):
    start = pl.multiple_of(base + (off // SUBLANES) * SUBLANES, SUBLANES)
    if off % SUBLANES == 0:
        return src_ref[pl.ds(start, CONV_CHUNK), :]
    return sh_ref[off % SUBLANES - 1, pl.ds(start, CONV_CHUNK), :]


def _conv_taps(src_ref, sh_ref, w_ref, dst_ref, width, nseq, transpose):
    w = w_ref[...]
    offs = _conv_offsets(width, transpose)
    _shift_copies(src_ref, sh_ref, width, transpose)

    def step(n, carry):
        out0 = pl.multiple_of(CONV_HIST + n * CONV_CHUNK, SUBLANES)
        base = out0 if transpose else n * CONV_CHUNK
        acc = jnp.zeros((CONV_CHUNK, w.shape[1]), F32)
        for k, off in enumerate(offs):
            acc = acc + w[k:k + 1, :] * _tap_rows(src_ref, sh_ref, base, off)
        dst_ref[pl.ds(out0, CONV_CHUNK), :] = acc
        return carry

    lax.fori_loop(0, nseq // CONV_CHUNK, step, 0)


def _conv_wgrad(src_ref, sh_ref, dz_ref, acc_ref, width, nseq):
    acc_ref[...] = jnp.zeros(acc_ref.shape, F32)
    offs = _conv_offsets(width, False)

    def step(n, carry):
        dzc = dz_ref[pl.ds(pl.multiple_of(CONV_HIST + n * CONV_CHUNK, SUBLANES), CONV_CHUNK), :]
        for k, off in enumerate(offs):
            prod = dzc * _tap_rows(src_ref, sh_ref, n * CONV_CHUNK, off)
            part = prod[0:SUBLANES, :]
            for s in range(1, CONV_CHUNK // SUBLANES):
                part = part + prod[SUBLANES * s:SUBLANES * (s + 1), :]
            acc_ref[SUBLANES * k:SUBLANES * (k + 1), :] += part
        return carry

    lax.fori_loop(0, nseq // CONV_CHUNK, step, 0)


def _to_seq(buf_ref, x_part, meta_part, seq):
    buf_ref[CONV_HIST:CONV_HIST + N_META, :] = meta_part
    buf_ref[CONV_HIST + N_META:CONV_HIST + N_META + seq, :] = x_part


def _zero_ends(buf_ref, nseq):
    zeros = jnp.zeros((CONV_HIST, buf_ref.shape[1]), F32)
    buf_ref[0:CONV_HIST, :] = zeros
    buf_ref[CONV_HIST + nseq:CONV_HIST + nseq + CONV_HIST, :] = zeros


def _mix_conv_fwd(hp5, wa, wb, bb):
    _, tp, wgrp = hp5.shape
    seq, nseq = _seq_rows(tp)
    sb = nseq + 2 * CONV_HIST
    ka, kb = wa.shape[0], wb.shape[0]
    xs, ms = slice(0, seq), slice(seq, seq + N_META)
    ox, om = slice(CONV_HIST + N_META, CONV_HIST + nseq), slice(CONV_HIST, CONV_HIST + N_META)

    def body(hp_ref, wa_ref, wb_ref, bb_ref, ya_ref, z_ref, s_ref, o_ref, sh_ref):
        _zero_ends(s_ref, nseq)
        _to_seq(s_ref, hp_ref[1, xs, :] * hp_ref[2, xs, :], hp_ref[1, ms, :] * hp_ref[2, ms, :], seq)
        _conv_taps(s_ref, sh_ref, wa_ref, o_ref, ka, nseq, False)
        ya_ref[xs, :] = (hp_ref[0, xs, :] * o_ref[ox, :]).astype(BF16)
        ya_ref[ms, :] = (hp_ref[0, ms, :] * o_ref[om, :]).astype(BF16)
        ya_ref[seq + N_META:tp, :] = jnp.zeros((tp - seq - N_META, LANES), BF16)
        _to_seq(s_ref, hp_ref[3, xs, :] * _sig(hp_ref[4, xs, :]), hp_ref[3, ms, :] * _sig(hp_ref[4, ms, :]), seq)
        _conv_taps(s_ref, sh_ref, wb_ref, o_ref, kb, nseq, False)
        z_ref[xs, :] = o_ref[ox, :] + bb_ref[...]
        z_ref[ms, :] = o_ref[om, :] + bb_ref[...]
        z_ref[seq + N_META:tp, :] = jnp.zeros((tp - seq - N_META, LANES), F32)

    col = lambda j: (0, j)
    return pl.pallas_call(
        body, name="mix_conv_fwd", grid=(wgrp // LANES,),
        in_specs=[pl.BlockSpec((5, tp, LANES), lambda j: (0, 0, j)), pl.BlockSpec((ka, LANES), col),
                  pl.BlockSpec((kb, LANES), col), pl.BlockSpec((1, LANES), col)],
        out_specs=[pl.BlockSpec((tp, LANES), col), pl.BlockSpec((tp, LANES), col)],
        out_shape=[_sds((tp, wgrp), BF16), _sds((tp, wgrp), F32)],
        scratch_shapes=[pltpu.VMEM((sb, LANES), F32), pltpu.VMEM((sb, LANES), F32),
                        pltpu.VMEM((SUBLANES - 1, sb, LANES), F32)],
        compiler_params=_params(("arbitrary",)),
    )(hp5, wa, wb, bb)


def _layer_norm_parts(z, lg, lb):
    mu = _mean(z)
    zc = z - mu
    rl = lax.rsqrt(_mean(zc * zc) + LN_EPS)
    zh = zc * rl
    return rl, zh, zh * lg + lb


def _mm_out(ya, z, h, wout, lg, lb, gpm, g2):
    tp, d = h.shape
    wa_ = ya.shape[1]
    tm = _row_tile(tp)

    def body(ya_ref, z_ref, h_ref, w_ref, lg_ref, lb_ref, gpm_ref, g2_ref, yb_ref, mix_ref, h1_ref, xn2_ref):
        for rs in _row_parts(tm, 3):
            _, _, l = _layer_norm_parts(z_ref[rs, :], lg_ref[...], lb_ref[...])
            yb = (l * _sig(l)).astype(BF16)
            yb_ref[rs, :] = yb
            mix = _dot(ya_ref[rs, :], w_ref[0:wa_, :]) + _dot(yb, w_ref[wa_:d, :])
            mix_ref[rs, :] = mix
            rm = lax.rsqrt(_mean(mix * mix) + RMS_EPS)
            h1 = h_ref[rs, :] + mix * rm * gpm_ref[...]
            h1_ref[rs, :] = h1
            r2 = lax.rsqrt(_mean(h1 * h1) + RMS_EPS)
            xn2_ref[rs, :] = (h1 * r2 * g2_ref[...]).astype(BF16)

    row = lambda i: (i, 0)
    return pl.pallas_call(
        body, name="mm_out", grid=(tp // tm,),
        in_specs=[pl.BlockSpec((tm, wa_), row), pl.BlockSpec((tm, wa_), row), pl.BlockSpec((tm, d), row),
                  _resident(wout.shape), _full(lg.shape), _full(lb.shape), _full(gpm.shape), _full(g2.shape)],
        out_specs=[pl.BlockSpec((tm, wa_), row), pl.BlockSpec((tm, d), row), pl.BlockSpec((tm, d), row),
                   pl.BlockSpec((tm, d), row)],
        out_shape=[_sds((tp, wa_), BF16), _sds((tp, d), F32), _sds((tp, d), F32), _sds((tp, d), BF16)],
        compiler_params=_params(("arbitrary",)),
    )(ya, z, h, wout, lg, lb, gpm, g2)


def _ffn_up(xn2, wg, wu):
    tp, d = xn2.shape
    ff_dim = wg.shape[0]
    tm = _row_tile(tp)
    assert ff_dim % MXU_TILE == 0

    def body(xn_ref, wg_ref, wu_ref, p_ref, q_ref, f_ref):
        xn = xn_ref[...]
        for lo in range(0, ff_dim, MXU_TILE):
            cols = slice(lo, lo + MXU_TILE)
            a = _dot(xn, wg_ref[cols, :], NT)
            u = _dot(xn, wu_ref[cols, :], NT)
            s = _sig(a)
            q = a * s
            p_ref[:, cols] = (u * (s + q * (1.0 - s))).astype(BF16)
            q_ref[:, cols] = q.astype(BF16)
            f_ref[:, cols] = (q * u).astype(BF16)

    ospec = pl.BlockSpec((tm, ff_dim), lambda i: (i, 0))
    return pl.pallas_call(
        body, name="ffn_up", grid=(tp // tm,),
        in_specs=[pl.BlockSpec((tm, d), lambda i: (i, 0)), _resident(wg.shape), _resident(wu.shape)],
        out_specs=[ospec, ospec, ospec],
        out_shape=[_sds((tp, ff_dim), BF16)] * 3,
        compiler_params=_params(("arbitrary",)),
    )(xn2, wg, wu)


def _ffn_down(f, wd, h1, tgt, gpf):
    tp, ff_dim = f.shape
    d = h1.shape[1]
    tm = _row_tile(tp)
    seq, _ = _seq_rows(tp)

    def body(f_ref, w_ref, h1_ref, t_ref, gpf_ref, dff_ref, dh2_ref, loss_ref, dgpf_ref):
        i = pl.program_id(0)
        gpf_ = gpf_ref[...]

        @pl.when(i == 0)
        def _():
            loss_ref[...] = jnp.zeros(loss_ref.shape, F32)
            dgpf_ref[...] = jnp.zeros(dgpf_ref.shape, F32)

        for rs in _row_parts(tm):
            ff = _dot(f_ref[rs, :], w_ref[...])
            rf = lax.rsqrt(_mean(ff * ff) + RMS_EPS)
            nf = ff * rf
            h2 = h1_ref[rs, :] + nf * gpf_
            rows = i * tm + rs.start + lax.broadcasted_iota(jnp.int32, (rs.stop - rs.start, 1), 0)
            err = jnp.where(rows < seq, h2 - t_ref[rs, :], 0.0)
            dh2 = err * (1.0 / d)
            dh2_ref[rs, :] = dh2
            dn = dh2 * gpf_
            dff_ref[rs, :] = (rf * (dn - nf * _mean(dn * nf))).astype(BF16)
            loss_ref[...] += (0.5 / d) * jnp.sum(err * err, axis=(0, 1), keepdims=True)
            dgpf_ref[...] += jnp.sum(dh2 * nf, axis=0, keepdims=True)

    row = lambda i: (i, 0)
    return pl.pallas_call(
        body, name="ffn_down", grid=(tp // tm,),
        in_specs=[pl.BlockSpec((tm, ff_dim), row), _resident(wd.shape), pl.BlockSpec((tm, d), row),
                  pl.BlockSpec((tm, d), row), _full(gpf.shape)],
        out_specs=[pl.BlockSpec((tm, d), row), pl.BlockSpec((tm, d), row), _full((SUBLANES, LANES)), _full((1, d))],
        out_shape=[_sds((tp, d), BF16), _sds((tp, d), F32), _sds((SUBLANES, LANES), F32), _sds((1, d), F32)],
        compiler_params=_params(("arbitrary",)),
    )(f, wd, h1, tgt, gpf)


def _ffn_bwd_act(dff, wd, p, q):
    tp, d = dff.shape
    ff_dim = wd.shape[0]
    tm = _row_tile(tp)

    def body(dff_ref, w_ref, p_ref, q_ref, da_ref, du_ref):
        dffv = dff_ref[...]
        for lo in range(0, ff_dim, MXU_TILE):
            cols = slice(lo, lo + MXU_TILE)
            df = _dot(dffv, w_ref[cols, :], NT).astype(BF16)
            da_ref[:, cols] = df * p_ref[:, cols]
            du_ref[:, cols] = df * q_ref[:, cols]

    aspec = pl.BlockSpec((tm, ff_dim), lambda i: (i, 0))
    return pl.pallas_call(
        body, name="ffn_bwd_act", grid=(tp // tm,),
        in_specs=[pl.BlockSpec((tm, d), lambda i: (i, 0)), _resident(wd.shape), aspec, aspec],
        out_specs=[aspec, aspec],
        out_shape=[_sds((tp, ff_dim), BF16)] * 2,
        compiler_params=_params(("arbitrary",)),
    )(dff, wd, p, q)


def _grad_blocks(ff_dim):
    rows = ff_dim // 2
    assert rows % LANES == 0
    return rows


def _grad_w_down(f, dff):
    tp, ff_dim = f.shape
    d = dff.shape[1]
    rows = _grad_blocks(ff_dim)

    def body(f_ref, dff_ref, g_ref):
        g_ref[...] = _dot(f_ref[...], dff_ref[...], TN).astype(BF16)

    return pl.pallas_call(
        body, name="grad_w_down", grid=(ff_dim // rows,),
        in_specs=[pl.BlockSpec((tp, rows), lambda k: (0, k)), _resident(dff.shape)],
        out_specs=pl.BlockSpec((rows, d), lambda k: (k, 0)),
        out_shape=_sds((ff_dim, d), BF16),
        compiler_params=_params(("arbitrary",)),
    )(f, dff)


def _grad_w_gate_up(xn2, da, du):
    tp, ff_dim = da.shape
    d = xn2.shape[1]
    rows = _grad_blocks(ff_dim)

    def body(xn_ref, da_ref, du_ref, gg_ref, gu_ref):
        xn = xn_ref[...]
        gg_ref[...] = _dot(da_ref[...], xn, TN).astype(BF16)
        gu_ref[...] = _dot(du_ref[...], xn, TN).astype(BF16)

    aspec = pl.BlockSpec((tp, rows), lambda k: (0, k))
    gspec = pl.BlockSpec((rows, d), lambda k: (k, 0))
    return pl.pallas_call(
        body, name="grad_w_gate_up", grid=(ff_dim // rows,),
        in_specs=[_resident(xn2.shape), aspec, aspec],
        out_specs=[gspec, gspec],
        out_shape=[_sds((ff_dim, d), BF16)] * 2,
        compiler_params=_params(("arbitrary",)),
    )(xn2, da, du)


def _rms_bwd(dy, x, r, g):
    n = x * r
    dn = dy * g
    return r * (dn - n * _mean(dn * n)), dy * n


def _ffn_bwd_in(da, du, wg, wu, h1, mix, dh2, g2, gpm, after):
    tp, ff_dim = da.shape
    d = h1.shape[1]
    tm = _row_tile(tp)

    def body(da_ref, du_ref, wg_ref, wu_ref, h1_ref, mix_ref, dh2_ref, g2_ref, gpm_ref, after_ref,
             dh1_ref, dmix_ref, dg2_ref, dgpm_ref):
        i = pl.program_id(0)

        @pl.when(i == 0)
        def _():
            dg2_ref[...] = jnp.zeros(dg2_ref.shape, F32)
            dgpm_ref[...] = jnp.zeros(dgpm_ref.shape, F32)

        for rs in _row_parts(tm, 3):
            dxn = _dot(da_ref[rs, :], wg_ref[...]) + _dot(du_ref[rs, :], wu_ref[...])
            h1v = h1_ref[rs, :]
            r2 = lax.rsqrt(_mean(h1v * h1v) + RMS_EPS)
            dres, dg2_rows = _rms_bwd(dxn, h1v, r2, g2_ref[...])
            dh1 = dh2_ref[rs, :] + dres
            dh1_ref[rs, :] = dh1
            mixv = mix_ref[rs, :]
            rm = lax.rsqrt(_mean(mixv * mixv) + RMS_EPS)
            dmix, dgpm_rows = _rms_bwd(dh1, mixv, rm, gpm_ref[...])
            dmix_ref[rs, :] = dmix.astype(BF16)
            dg2_ref[...] += jnp.sum(dg2_rows, axis=0, keepdims=True)
            dgpm_ref[...] += jnp.sum(dgpm_rows, axis=0, keepdims=True)

    aspec = pl.BlockSpec((tm, ff_dim), lambda i: (i, 0))
    row = pl.BlockSpec((tm, d), lambda i: (i, 0))
    return pl.pallas_call(
        body, name="ffn_bwd_in", grid=(tp // tm,),
        in_specs=[aspec, aspec, _resident(wg.shape), _resident(wu.shape), row, row, row, _full(g2.shape), _full(gpm.shape),
                  ANY],
        out_specs=[row, row, _full((1, d)), _full((1, d))],
        out_shape=[_sds((tp, d), F32), _sds((tp, d), BF16), _sds((1, d), F32), _sds((1, d), F32)],
        compiler_params=_params(("arbitrary",)),
    )(da, du, wg, wu, h1, mix, dh2, g2, gpm, after)


def _grad_w_out(ya, yb, dmix, after):
    tp, wa_ = ya.shape
    d = dmix.shape[1]

    def body(ya_ref, yb_ref, dmix_ref, after_ref, g_ref):
        dm = dmix_ref[...]
        g_ref[0:wa_, :] = _dot(ya_ref[...], dm, TN).astype(BF16)
        g_ref[wa_:2 * wa_, :] = _dot(yb_ref[...], dm, TN).astype(BF16)

    return pl.pallas_call(
        body, name="grad_w_out", grid=(1,),
        in_specs=[_full(ya.shape), _full(yb.shape), _full(dmix.shape), ANY],
        out_specs=_full((2 * wa_, d)),
        out_shape=_sds((2 * wa_, d), BF16),
        compiler_params=_params(("arbitrary",)),
    )(ya, yb, dmix, after)


def _mix_bwd_out(dmix, wout, z, lg, lb, after):
    tp, d = dmix.shape
    wa_ = z.shape[1]
    tm = _row_tile(tp)

    def body(dmix_ref, w_ref, z_ref, lg_ref, lb_ref, after_ref, dya_ref, dz_ref, dlg_ref, dlb_ref):
        i = pl.program_id(0)
        lg_ = lg_ref[...]

        @pl.when(i == 0)
        def _():
            dlg_ref[...] = jnp.zeros(dlg_ref.shape, F32)
            dlb_ref[...] = jnp.zeros(dlb_ref.shape, F32)

        for rs in _row_parts(tm):
            dm = dmix_ref[rs, :]
            dya_ref[rs, :] = _dot(dm, w_ref[0:wa_, :], NT)
            dyb = _dot(dm, w_ref[wa_:d, :], NT)
            rl, zh, l = _layer_norm_parts(z_ref[rs, :], lg_, lb_ref[...])
            sl = _sig(l)
            dl = dyb * (sl * (1.0 + l * (1.0 - sl)))
            dzh = dl * lg_
            dz_ref[rs, :] = rl * (dzh - _mean(dzh) - zh * _mean(dzh * zh))
            dlg_ref[...] += jnp.sum(dl * zh, axis=0, keepdims=True)
            dlb_ref[...] += jnp.sum(dl, axis=0, keepdims=True)

    row = lambda i: (i, 0)
    return pl.pallas_call(
        body, name="mix_bwd_out", grid=(tp // tm,),
        in_specs=[pl.BlockSpec((tm, d), row), _resident(wout.shape), pl.BlockSpec((tm, wa_), row), _full(lg.shape),
                  _full(lb.shape), ANY],
        out_specs=[pl.BlockSpec((tm, wa_), row), pl.BlockSpec((tm, wa_), row), _full((1, wa_)), _full((1, wa_))],
        out_shape=[_sds((tp, wa_), F32), _sds((tp, wa_), F32), _sds((1, wa_), F32), _sds((1, wa_), F32)],
        compiler_params=_params(("arbitrary",)),
    )(dmix, wout, z, lg, lb, after)


def _mix_conv_bwd(hp5, dya, dz, wa, wb):
    _, tp, wgrp = hp5.shape
    seq, nseq = _seq_rows(tp)
    sb = nseq + 2 * CONV_HIST
    ka, kb = wa.shape[0], wb.shape[0]
    xs, ms = slice(0, seq), slice(seq, seq + N_META)
    ox, om = slice(CONV_HIST + N_META, CONV_HIST + nseq), slice(CONV_HIST, CONV_HIST + N_META)
    n_tail = tp - seq - N_META

    def body(hp_ref, dya_ref, dz_ref, wa_ref, wb_ref, dhp_ref, dwa_ref, dwb_ref, dbb_ref, s_ref, d_ref, o_ref, acc_ref,
             shs_ref, shd_ref):
        _zero_ends(s_ref, nseq)
        _zero_ends(d_ref, nseq)

        def put(p, ox_val, om_val):
            dhp_ref[p, xs, :] = ox_val.astype(BF16)
            dhp_ref[p, ms, :] = om_val.astype(BF16)
            dhp_ref[p, seq + N_META:tp, :] = jnp.zeros((n_tail, LANES), BF16)

        def wgrad(dw_ref, width):
            for k in range(width):
                dw_ref[k:k + 1, :] = jnp.sum(acc_ref[8 * k:8 * k + 8, :], axis=0, keepdims=True)

        _to_seq(s_ref, hp_ref[1, xs, :] * hp_ref[2, xs, :], hp_ref[1, ms, :] * hp_ref[2, ms, :], seq)
        _conv_taps(s_ref, shs_ref, wa_ref, o_ref, ka, nseq, False)
        put(0, dya_ref[xs, :] * o_ref[ox, :], dya_ref[ms, :] * o_ref[om, :])
        _to_seq(d_ref, dya_ref[xs, :] * hp_ref[0, xs, :], dya_ref[ms, :] * hp_ref[0, ms, :], seq)
        _conv_wgrad(s_ref, shs_ref, d_ref, acc_ref, ka, nseq)
        wgrad(dwa_ref, ka)
        _conv_taps(d_ref, shd_ref, wa_ref, o_ref, ka, nseq, True)
        put(1, o_ref[ox, :] * hp_ref[2, xs, :], o_ref[om, :] * hp_ref[2, ms, :])
        put(2, o_ref[ox, :] * hp_ref[1, xs, :], o_ref[om, :] * hp_ref[1, ms, :])

        _to_seq(s_ref, hp_ref[3, xs, :] * _sig(hp_ref[4, xs, :]), hp_ref[3, ms, :] * _sig(hp_ref[4, ms, :]), seq)
        _to_seq(d_ref, dz_ref[xs, :], dz_ref[ms, :], seq)
        dbb_ref[...] = (jnp.sum(dz_ref[xs, :], axis=0, keepdims=True)
                        + jnp.sum(dz_ref[ms, :], axis=0, keepdims=True))
        _shift_copies(s_ref, shs_ref, kb, False)
        _conv_wgrad(s_ref, shs_ref, d_ref, acc_ref, kb, nseq)
        wgrad(dwb_ref, kb)
        _conv_taps(d_ref, shd_ref, wb_ref, o_ref, kb, nseq, True)
        sx, sm = _sig(hp_ref[4, xs, :]), _sig(hp_ref[4, ms, :])
        put(3, o_ref[ox, :] * sx, o_ref[om, :] * sm)
        put(4, o_ref[ox, :] * hp_ref[3, xs, :] * sx * (1.0 - sx), o_ref[om, :] * hp_ref[3, ms, :] * sm * (1.0 - sm))

    col = lambda j: (0, j)
    blk5 = pl.BlockSpec((5, tp, LANES), lambda j: (0, 0, j))
    return pl.pallas_call(
        body, name="mix_conv_bwd", grid=(wgrp // LANES,),
        in_specs=[blk5, pl.BlockSpec((tp, LANES), col), pl.BlockSpec((tp, LANES), col),
                  pl.BlockSpec((ka, LANES), col), pl.BlockSpec((kb, LANES), col)],
        out_specs=[blk5, pl.BlockSpec((ka, LANES), col), pl.BlockSpec((kb, LANES), col), pl.BlockSpec((1, LANES), col)],
        out_shape=[_sds((5, tp, wgrp), BF16), _sds((ka, wgrp), F32), _sds((kb, wgrp), F32), _sds((1, wgrp), F32)],
        scratch_shapes=[pltpu.VMEM((sb, LANES), F32), pltpu.VMEM((sb, LANES), F32), pltpu.VMEM((sb, LANES), F32),
                        pltpu.VMEM((SUBLANES * kb, LANES), F32), pltpu.VMEM((SUBLANES - 1, sb, LANES), F32),
                        pltpu.VMEM((SUBLANES - 1, sb, LANES), F32)],
        compiler_params=_params(("arbitrary",)),
    )(hp5, dya, dz, wa, wb)


def _grad_w_in(xn1, dhp5):
    n_p, tp, pw = dhp5.shape
    d = xn1.shape[1]

    def body(xn_ref, dhp_ref, g_ref):
        g_ref[...] = _dot(xn_ref[...], dhp_ref[0], TN).astype(BF16)

    return pl.pallas_call(
        body, name="grad_w_in", grid=(n_p,),
        in_specs=[_resident(xn1.shape), pl.BlockSpec((1, tp, pw), lambda p: (p, 0, 0))],
        out_specs=pl.BlockSpec((d, pw), lambda p: (0, p)),
        out_shape=_sds((d, n_p * pw), BF16),
        compiler_params=_params(("arbitrary",)),
    )(xn1, dhp5)


def _mix_bwd_in(dhp5, win4, h, dh1, g1, after):
    n_p, tp, pw = dhp5.shape
    d = h.shape[1]
    n_sh, _, csh = win4.shape
    tm = _row_tile(tp)

    seq, _ = _seq_rows(tp)
    last, meta_off = seq // tm, seq % tm
    assert last == tp // tm - 1
    assert any(rs.start <= meta_off and meta_off + N_META <= rs.stop for rs in _row_parts(tm))

    def body(dhp_ref, w_ref, h_ref, dh1_ref, g_ref, after_ref, gx_ref, dmeta_ref, dg1_ref, wcat_ref):
        i = pl.program_id(0)
        _concat_shards(w_ref, wcat_ref)

        @pl.when(i == 0)
        def _():
            dg1_ref[...] = jnp.zeros(dg1_ref.shape, F32)

        for rs in _row_parts(tm):
            dxn = _dot(dhp_ref[0, rs, :], wcat_ref[:, 0:pw], NT)
            for p in range(1, n_p):
                dxn = dxn + _dot(dhp_ref[p, rs, :], wcat_ref[:, p * pw:(p + 1) * pw], NT)
            hh = h_ref[rs, :]
            r1 = lax.rsqrt(_mean(hh * hh) + RMS_EPS)
            dres, dg_rows = _rms_bwd(dxn, hh, r1, g_ref[...])
            dh = dh1_ref[rs, :] + dres
            gx_ref[rs, :] = dh
            dg1_ref[...] += jnp.sum(dg_rows, axis=0, keepdims=True)
            if rs.start <= meta_off and meta_off + N_META <= rs.stop:
                @pl.when(i == last)
                def _():
                    dmeta_ref[...] = dh[meta_off - rs.start:meta_off - rs.start + N_META, :]

    row = lambda i: (i, 0)
    return pl.pallas_call(
        body, name="mix_bwd_in", grid=(tp // tm,),
        in_specs=[pl.BlockSpec((n_p, tm, pw), lambda i: (0, i, 0)), _resident(win4.shape), pl.BlockSpec((tm, d), row),
                  pl.BlockSpec((tm, d), row), _full(g1.shape), ANY],
        out_specs=[pl.BlockSpec((tm, d), row), _full((N_META, d)), _full((1, d))],
        out_shape=[_sds((seq, d), F32), _sds((N_META, d), F32), _sds((1, d), F32)],
        scratch_shapes=[pltpu.VMEM((d, n_sh * csh), BF16)],
        compiler_params=_params(("arbitrary",)),
    )(dhp5, win4, h, dh1, g1, after)


def _other_chips(x, y):
    out = []
    for j in (1, 2, 3):
        px, py = _flip(x, j >> 1), _flip(y, j & 1)
        out.append((px, py, 2 * px + py))
    return out


PAIR_COLLECTIVE_ID = 0


def _pair_barrier(x, y, c):
    sem = pltpu.get_barrier_semaphore()
    pl.semaphore_signal(sem, inc=1, device_id=(x, y, 1 - c), device_id_type=MESH)
    pl.semaphore_wait(sem, 1)


def _pair_params():
    return pltpu.CompilerParams(collective_id=PAIR_COLLECTIVE_ID)


def _half_rows(c, rows_half):
    return pl.ds(pl.multiple_of(c * rows_half, SUBLANES), rows_half)


def _cast_place(ws, q_arr, tag, after=None):
    n = len(ws)
    extra = [] if after is None else [after]

    def fits(steps):
        return all(w.shape[0] % steps == 0 and (w.shape[0] // steps) % BF16_ROWS == 0
                   and w.shape[0] // steps <= ROW_TILE_CAP for w in ws)

    steps = next(s for s in range(1, min(w.shape[0] for w in ws) + 1) if fits(s))

    def body(q_ref, *refs):
        for w_ref, out_ref in zip(refs[:n], refs[n + len(extra):]):
            out_ref[0] = w_ref[...].astype(BF16)

    return list(pl.pallas_call(
        body, name="cast_place_" + tag,
        grid_spec=pltpu.PrefetchScalarGridSpec(
            num_scalar_prefetch=1, grid=(steps,),
            in_specs=[pl.BlockSpec((w.shape[0] // steps, w.shape[1]), lambda i, q: (i, 0)) for w in ws] + [ANY] * len(extra),
            out_specs=[pl.BlockSpec((1, w.shape[0] // steps, w.shape[1]), lambda i, q: (q[0], i, 0)) for w in ws]),
        out_shape=[_sds((N_CHIPS,) + w.shape, BF16) for w in ws],
        compiler_params=_params(("arbitrary",)),
    )(q_arr, *ws, *extra))


HBM = pl.BlockSpec(memory_space=pltpu.HBM)
SEM = pl.BlockSpec(memory_space=pltpu.SEMAPHORE)
EFFECT = pltpu.SideEffectType.DATAFLOW_SIDE_EFFECTING


def _in_hbm(a):
    return pltpu.with_memory_space_constraint(a, pltpu.HBM)


def _gather_start(fulls, after, tag):
    n = len(fulls)
    halves = [a.shape[1] // 2 for a in fulls]

    def body(*refs):
        land = refs[:n]
        ssem, rsem = refs[n + 1], refs[n + 2]
        token = refs[-1]
        x, y, c = _mesh_pos()
        q = 2 * x + y
        for i in range(n):
            for j, (px, py, _) in enumerate(_other_chips(x, y)):
                mine = land[i].at[q, _half_rows(c, halves[i]), :]
                pltpu.make_async_remote_copy(src_ref=mine, dst_ref=mine, send_sem=ssem.at[3 * i + j],
                                             recv_sem=rsem.at[3 * i + j], device_id=(px, py, c), device_id_type=MESH).start()
        token[...] = jnp.zeros(token.shape, F32)

    outs = pl.pallas_call(
        body, name="gather_start_" + tag,
        in_specs=[HBM] * n + [ANY], out_specs=[SEM, SEM] + [HBM] * n + [VMEM],
        out_shape=[pltpu.SemaphoreType.DMA((3 * n,)), pltpu.SemaphoreType.DMA((3 * n,))]
        + [pltpu.HBM(a.shape, a.dtype) for a in fulls] + [_sds((SUBLANES, LANES), F32)],
        input_output_aliases={i: 2 + i for i in range(n)},
        compiler_params=pltpu.CompilerParams(has_side_effects=EFFECT),
    )(*[_in_hbm(a) for a in fulls], after)
    return outs[0], outs[1], list(outs[2:2 + n]), outs[-1]


def _gather_wait(which, ssem, rsem, lands, after, tag):
    m = len(which)
    halves = [a.shape[1] // 2 for a in lands]

    def body(*refs):
        land = refs[:m]
        ssem_, rsem_ = refs[m], refs[m + 1]
        x, y, c = _mesh_pos()
        for t, i in enumerate(which):
            for j, (px, py, qj) in enumerate(_other_chips(x, y)):
                rows = _half_rows(c, halves[t])
                cp = pltpu.make_async_remote_copy(src_ref=land[t].at[2 * x + y, rows, :], dst_ref=land[t].at[qj, rows, :],
                                                  send_sem=ssem_.at[3 * i + j], recv_sem=rsem_.at[3 * i + j],
                                                  device_id=(px, py, c), device_id_type=MESH)
                cp.wait_send()
                cp.wait_recv()

    outs = pl.pallas_call(
        body, name="gather_wait_" + tag,
        in_specs=[HBM] * m + [SEM, SEM, ANY], out_specs=[HBM] * m,
        out_shape=[pltpu.HBM(a.shape, a.dtype) for a in lands],
        input_output_aliases={i: i for i in range(m)},
        compiler_params=pltpu.CompilerParams(has_side_effects=EFFECT),
    )(*lands, ssem, rsem, after)
    return list(outs)


def _forward_pair(lands, tag):
    n = len(lands)
    halves = [a.shape[1] // 2 for a in lands]

    def body(*refs):
        full = refs[n:2 * n]
        ssem, rsem = refs[2 * n:]
        x, y, c = _mesh_pos()
        _pair_barrier(x, y, c)
        cps = []
        for i in range(n):
            for j, (_, _, qj) in enumerate(_other_chips(x, y)):
                part = full[i].at[qj, _half_rows(c, halves[i]), :]
                cp = pltpu.make_async_remote_copy(src_ref=part, dst_ref=part, send_sem=ssem.at[3 * i + j],
                                                  recv_sem=rsem.at[3 * i + j], device_id=(x, y, 1 - c), device_id_type=MESH)
                cp.start()
                cps.append(cp)
        for cp in cps:
            cp.wait()

    return pl.pallas_call(
        body, name="forward_pair_" + tag,
        in_specs=[ANY] * n, out_specs=[ANY] * n,
        out_shape=[_sds(a.shape, a.dtype) for a in lands],
        input_output_aliases={i: i for i in range(n)},
        scratch_shapes=[pltpu.SemaphoreType.DMA((3 * n,)), pltpu.SemaphoreType.DMA((3 * n,))],
        compiler_params=_pair_params(),
    )(*lands)


def _chip_exchange_start(parts, after, tag):
    n = len(parts)

    def body(*refs):
        src, land = refs[:n], refs[n:2 * n]
        ssem, rsem = refs[2 * n + 1], refs[2 * n + 2]
        token = refs[-1]
        x, y, c = _mesh_pos()
        for i in range(n):
            for j, (px, py, qj) in enumerate(_other_chips(x, y)):
                pltpu.make_async_remote_copy(src_ref=src[i].at[qj], dst_ref=land[i].at[j], send_sem=ssem.at[3 * i + j],
                                             recv_sem=rsem.at[3 * i + j], device_id=(px, py, c), device_id_type=MESH).start()
        token[...] = jnp.zeros(token.shape, F32)

    lands = [lax.empty((3,) + a.shape[1:], a.dtype) for a in parts]
    outs = pl.pallas_call(
        body, name="chip_exchange_start_" + tag,
        in_specs=[HBM] * (2 * n) + [ANY], out_specs=[SEM, SEM] + [HBM] * (2 * n) + [VMEM],
        out_shape=[pltpu.SemaphoreType.DMA((3 * n,)), pltpu.SemaphoreType.DMA((3 * n,))]
        + [pltpu.HBM(a.shape, a.dtype) for a in parts] + [pltpu.HBM(a.shape, a.dtype) for a in lands]
        + [_sds((SUBLANES, LANES), F32)],
        input_output_aliases={i: 2 + i for i in range(2 * n)},
        compiler_params=pltpu.CompilerParams(has_side_effects=EFFECT),
    )(*[_in_hbm(a) for a in parts], *[_in_hbm(a) for a in lands], after)
    return outs[0], outs[1], list(outs[2:2 + n]), list(outs[2 + n:2 + 2 * n]), outs[-1]


def _chip_exchange_wait(ssem, rsem, parts, lands, after, tag):
    n = len(parts)

    def body(*refs):
        src, land = refs[:n], refs[n:2 * n]
        ssem_, rsem_ = refs[2 * n], refs[2 * n + 1]
        x, y, c = _mesh_pos()
        for i in range(n):
            for j, (px, py, qj) in enumerate(_other_chips(x, y)):
                cp = pltpu.make_async_remote_copy(src_ref=src[i].at[qj], dst_ref=land[i].at[j], send_sem=ssem_.at[3 * i + j],
                                                  recv_sem=rsem_.at[3 * i + j], device_id=(px, py, c), device_id_type=MESH)
                cp.wait_send()
                cp.wait_recv()

    outs = pl.pallas_call(
        body, name="chip_exchange_wait_" + tag,
        in_specs=[HBM] * (2 * n) + [SEM, SEM, ANY], out_specs=[HBM] * (2 * n),
        out_shape=[pltpu.HBM(a.shape, a.dtype) for a in parts] + [pltpu.HBM(a.shape, a.dtype) for a in lands],
        input_output_aliases={i: i for i in range(2 * n)},
        compiler_params=pltpu.CompilerParams(has_side_effects=EFFECT),
    )(*parts, *lands, ssem, rsem, after)
    return list(outs[:n]), list(outs[n:])


def _grad_half(ref, shape, axis, which):
    rows = shape[axis] // 2
    if axis == 0:
        return ref.at[_half_rows(which, rows), :]
    return ref.at[:, _half_rows(which, rows), :]


def _half_shape(a, axis):
    s = list(a.shape)
    s[axis] //= 2
    return tuple(s)


def _pair_exchange_start(grads, half_axis, after, tag):
    n = len(grads)

    def body(*refs):
        g, land = refs[:n], refs[n:2 * n]
        ssem, rsem = refs[2 * n + 1], refs[2 * n + 2]
        token = refs[-1]
        x, y, c = _mesh_pos()
        for i in range(n):
            pltpu.make_async_remote_copy(src_ref=_grad_half(g[i], grads[i].shape, half_axis[i], 1 - c), dst_ref=land[i],
                                         send_sem=ssem.at[i], recv_sem=rsem.at[i], device_id=(x, y, 1 - c),
                                         device_id_type=MESH).start()
        token[...] = jnp.zeros(token.shape, F32)

    lands = [lax.empty(_half_shape(a, half_axis[i]), a.dtype) for i, a in enumerate(grads)]
    outs = pl.pallas_call(
        body, name="pair_exchange_start_" + tag,
        in_specs=[HBM] * (2 * n) + [ANY], out_specs=[SEM, SEM] + [HBM] * (2 * n) + [VMEM],
        out_shape=[pltpu.SemaphoreType.DMA((n,)), pltpu.SemaphoreType.DMA((n,))]
        + [pltpu.HBM(a.shape, a.dtype) for a in grads] + [pltpu.HBM(a.shape, a.dtype) for a in lands]
        + [_sds((SUBLANES, LANES), F32)],
        input_output_aliases={i: 2 + i for i in range(2 * n)},
        compiler_params=pltpu.CompilerParams(has_side_effects=EFFECT),
    )(*[_in_hbm(a) for a in grads], *[_in_hbm(a) for a in lands], after)
    return outs[0], outs[1], list(outs[2:2 + n]), list(outs[2 + n:2 + 2 * n]), outs[-1]


def _pair_exchange_wait(ssem, rsem, grads, lands, half_axis, after, tag):
    n = len(grads)

    def body(*refs):
        g, land = refs[:n], refs[n:2 * n]
        ssem_, rsem_ = refs[2 * n], refs[2 * n + 1]
        x, y, c = _mesh_pos()
        for i in range(n):
            cp = pltpu.make_async_remote_copy(src_ref=_grad_half(g[i], grads[i].shape, half_axis[i], 1 - c),
                                              dst_ref=land[i], send_sem=ssem_.at[i], recv_sem=rsem_.at[i],
                                              device_id=(x, y, 1 - c), device_id_type=MESH)
            cp.wait_send()
            cp.wait_recv()

    outs = pl.pallas_call(
        body, name="pair_exchange_wait_" + tag,
        in_specs=[HBM] * (2 * n) + [SEM, SEM, ANY], out_specs=[HBM] * (2 * n),
        out_shape=[pltpu.HBM(a.shape, a.dtype) for a in grads] + [pltpu.HBM(a.shape, a.dtype) for a in lands],
        input_output_aliases={i: i for i in range(2 * n)},
        compiler_params=pltpu.CompilerParams(has_side_effects=EFFECT),
    )(*grads, *lands, ssem, rsem, after)
    return list(outs[:n]), list(outs[n:])


def _pair_exchange_grads(grads, half_axis, tag):
    n = len(grads)

    def body(*refs):
        g, got = refs[:n], refs[n:2 * n]
        ssem, rsem = refs[2 * n:]
        x, y, c = _mesh_pos()
        _pair_barrier(x, y, c)
        cps = []
        for i in range(n):
            cp = pltpu.make_async_remote_copy(src_ref=_grad_half(g[i], grads[i].shape, half_axis[i], 1 - c),
                                              dst_ref=got[i], send_sem=ssem.at[i], recv_sem=rsem.at[i],
                                              device_id=(x, y, 1 - c), device_id_type=MESH)
            cp.start()
            cps.append(cp)
        for cp in cps:
            cp.wait()

    return pl.pallas_call(
        body, name="pair_exchange_grads_" + tag,
        in_specs=[ANY] * n, out_specs=[ANY] * n,
        out_shape=[_sds(_half_shape(a, half_axis[i]), a.dtype) for i, a in enumerate(grads)],
        scratch_shapes=[pltpu.SemaphoreType.DMA((n,)), pltpu.SemaphoreType.DMA((n,))],
        compiler_params=_pair_params(),
    )(*grads)


def _pair_sum(gs, gots, c_arr, col_sharded, tag):
    n = len(gs)
    g_specs, got_specs, out_specs, out_shapes = [], [], [], []
    for g, by_cols in zip(gs, col_sharded):
        if by_cols:
            rows, cols = g.shape
            rh, cs = rows // 2, cols // N_CHIPS
            g_specs.append(pl.BlockSpec((rh, cs), lambda k, c_ref: (c_ref[0], k)))
            got_specs.append(pl.BlockSpec((rh, cs), lambda k, c_ref: (0, k)))
        else:
            _, rows, cs = g.shape
            rh = rows // 2
            g_specs.append(pl.BlockSpec((1, rh, cs), lambda k, c_ref: (k, c_ref[0], 0)))
            got_specs.append(pl.BlockSpec((1, rh, cs), lambda k, c_ref: (k, 0, 0)))
        out_specs.append(pl.BlockSpec((1, rh, cs), lambda k, c_ref: (k, 0, 0)))
        out_shapes.append(_sds((N_CHIPS, rh, cs), BF16))

    def body(c_ref, *refs):
        for g_ref, got_ref, out_ref in zip(refs[:n], refs[n:2 * n], refs[2 * n:]):
            total = g_ref[...].astype(F32) + got_ref[...].astype(F32)
            out_ref[...] = total.astype(BF16).reshape(out_ref.shape)

    return list(pl.pallas_call(
        body, name="pair_sum_" + tag,
        grid_spec=pltpu.PrefetchScalarGridSpec(
            num_scalar_prefetch=1, grid=(N_CHIPS,), in_specs=g_specs + got_specs, out_specs=out_specs),
        out_shape=out_shapes,
        compiler_params=_params(("arbitrary",)),
    )(c_arr, *gs, *gots))


def _chip_sum(parts, gots, qc_arr, tag):
    n = len(parts)
    steps = 2 if all(p.shape[1] % 32 == 0 for p in parts) else 1
    part_specs, got_specs, out_specs, out_shapes = [], [], [], []
    for p in parts:
        _, rh, cs = p.shape
        rb = rh // steps
        part_specs.append(pl.BlockSpec((1, rb, cs), lambda i, qc: (qc[0], i, 0)))
        got_specs.append(pl.BlockSpec((3, rb, cs), lambda i, qc: (0, i, 0)))
        out_specs.append(pl.BlockSpec((rb, cs), lambda i, qc: (qc[1] * steps + i, 0)))
        out_shapes.append(_sds((2 * rh, cs), F32))

    def body(qc_ref, *refs):
        for part_ref, got_ref, out_ref in zip(refs[:n], refs[n:2 * n], refs[2 * n:]):
            total = part_ref[0].astype(F32)
            for j in range(3):
                total = total + got_ref[j].astype(F32)
            out_ref[...] = total

    return list(pl.pallas_call(
        body, name="chip_sum_" + tag,
        grid_spec=pltpu.PrefetchScalarGridSpec(
            num_scalar_prefetch=1, grid=(steps,), in_specs=part_specs + got_specs, out_specs=out_specs),
        out_shape=out_shapes,
        compiler_params=_params(("arbitrary",)),
    )(qc_arr, *parts, *gots))


def _pair_share_grads(grads, tag):
    n = len(grads)

    def body(*refs):
        g = refs[n:2 * n]
        ssem, rsem = refs[2 * n:]
        x, y, c = _mesh_pos()
        _pair_barrier(x, y, c)
        cps = []
        for i in range(n):
            mine = g[i].at[_half_rows(c, grads[i].shape[0] // 2), :]
            cp = pltpu.make_async_remote_copy(src_ref=mine, dst_ref=mine, send_sem=ssem.at[i], recv_sem=rsem.at[i],
                                              device_id=(x, y, 1 - c), device_id_type=MESH)
            cp.start()
            cps.append(cp)
        for cp in cps:
            cp.wait()

    return pl.pallas_call(
        body, name="pair_share_grads_" + tag,
        in_specs=[ANY] * n, out_specs=[ANY] * n,
        out_shape=[_sds(a.shape, a.dtype) for a in grads],
        input_output_aliases={i: i for i in range(n)},
        scratch_shapes=[pltpu.SemaphoreType.DMA((n,)), pltpu.SemaphoreType.DMA((n,))],
        compiler_params=_pair_params(),
    )(*grads)


def _pair_share_start(grads, after, tag):
    n = len(grads)

    def body(*refs):
        g = refs[:n]
        ssem, rsem = refs[n + 1], refs[n + 2]
        token = refs[-1]
        x, y, c = _mesh_pos()
        for i in range(n):
            mine = g[i].at[_half_rows(c, grads[i].shape[0] // 2), :]
            pltpu.make_async_remote_copy(src_ref=mine, dst_ref=mine, send_sem=ssem.at[i], recv_sem=rsem.at[i],
                                         device_id=(x, y, 1 - c), device_id_type=MESH).start()
        token[...] = jnp.zeros(token.shape, F32)

    outs = pl.pallas_call(
        body, name="pair_share_start_" + tag,
        in_specs=[HBM] * n + [ANY], out_specs=[SEM, SEM] + [HBM] * n + [VMEM],
        out_shape=[pltpu.SemaphoreType.DMA((n,)), pltpu.SemaphoreType.DMA((n,))]
        + [pltpu.HBM(a.shape, a.dtype) for a in grads] + [_sds((SUBLANES, LANES), F32)],
        input_output_aliases={i: 2 + i for i in range(n)},
        compiler_params=pltpu.CompilerParams(has_side_effects=EFFECT),
    )(*[_in_hbm(a) for a in grads], after)
    return outs[0], outs[1], list(outs[2:2 + n]), outs[-1]


def _pair_share_wait(ssem, rsem, grads, after, tag):
    n = len(grads)

    def body(*refs):
        g = refs[:n]
        ssem_, rsem_ = refs[n], refs[n + 1]
        x, y, c = _mesh_pos()
        for i in range(n):
            rows = grads[i].shape[0] // 2
            cp = pltpu.make_async_remote_copy(src_ref=g[i].at[_half_rows(c, rows), :], dst_ref=g[i].at[_half_rows(1 - c, rows), :],
                                              send_sem=ssem_.at[i], recv_sem=rsem_.at[i], device_id=(x, y, 1 - c),
                                              device_id_type=MESH)
            cp.wait_send()
            cp.wait_recv()

    outs = pl.pallas_call(
        body, name="pair_share_wait_" + tag,
        in_specs=[HBM] * n + [SEM, SEM, ANY], out_specs=[HBM] * n,
        out_shape=[pltpu.HBM(a.shape, a.dtype) for a in grads],
        input_output_aliases={i: i for i in range(n)},
        compiler_params=pltpu.CompilerParams(has_side_effects=EFFECT),
    )(*grads, ssem, rsem, after)
    return list(outs)


def _small_allreduce(parts, places, rows_total, width, after):
    n = len(parts)

    def body(*refs):
        ins, out_ref = refs[:n], refs[n + 1]
        pack, pair_got, chip_sum, got, ssem, rsem = refs[n + 2:]
        x, y, c = _mesh_pos()
        chip = 2 * x + y
        pack[...] = jnp.zeros(pack.shape, F32)
        for i in range(n):
            for row, col, src_row, rows in places[i]:
                w = parts[i].shape[1]
                pack[row:row + rows, col:col + w] = ins[i][src_row:src_row + rows, :]
        swap = pltpu.make_async_remote_copy(src_ref=pack, dst_ref=pair_got, send_sem=ssem.at[3], recv_sem=rsem.at[3],
                                            device_id=(x, y, 1 - c), device_id_type=MESH)
        swap.start()
        swap.wait()
        chip_sum[...] = pack[...] + pair_got[...]
        cps = []
        for j, (px, py, _) in enumerate(_other_chips(x, y)):
            cp = pltpu.make_async_remote_copy(src_ref=chip_sum, dst_ref=got.at[j], send_sem=ssem.at[j],
                                              recv_sem=rsem.at[j], device_id=(px, py, c), device_id_type=MESH)
            cp.start()
            cps.append(cp)
        for cp in cps:
            cp.wait()
        total = jnp.zeros(pack.shape, F32)
        for q in range(N_CHIPS):
            rel = jnp.bitwise_xor(chip, q)
            theirs = got[jnp.maximum(rel - 1, 0)]
            total = total + jnp.where(rel == 0, chip_sum[...], theirs)
        out_ref[...] = total

    return pl.pallas_call(
        body, name="small_allreduce",
        in_specs=[VMEM] * n + [ANY], out_specs=VMEM,
        out_shape=_sds((rows_total, width), F32),
        scratch_shapes=[pltpu.VMEM((rows_total, width), F32), pltpu.VMEM((rows_total, width), F32),
                        pltpu.VMEM((rows_total, width), F32), pltpu.VMEM((3, rows_total, width), F32),
                        pltpu.SemaphoreType.DMA((4,)), pltpu.SemaphoreType.DMA((4,))],
        compiler_params=_params(),
    )(*parts, after)


def _small_update(red, q_arr, takes, loss_at, ws, ms, vs):
    n_w = len(ws)

    def body(q_ref, red_ref, *refs):
        w_in, m_in, v_in = refs[0:n_w], refs[n_w:2 * n_w], refs[2 * n_w:3 * n_w]
        outs = refs[3 * n_w:]
        g_out, d_out, m_out, v_out = (outs[0:n_w], outs[n_w:2 * n_w], outs[2 * n_w:3 * n_w], outs[3 * n_w:4 * n_w])
        loss_ref = outs[4 * n_w]
        chip = q_ref[0]

        def put(g_ref, d0, nr, s0, lo, w):
            if len(g_ref.shape) == 3:
                for r in range(nr):
                    g_ref[d0 + r] = red_ref[s0 + r:s0 + r + 1, lo:lo + w]
            else:
                g_ref[d0:d0 + nr, :] = red_ref[s0:s0 + nr, lo:lo + w]

        def take_own_columns(g_ref, d0, nr, s0, c0, w):
            for k in range(N_CHIPS):
                @pl.when(chip == k)
                def _():
                    put(g_ref, d0, nr, s0, c0 + k * w, w)

        for j in range(n_w):
            w = ws[j].shape[-1]
            for d0, nr, s0, c0, sharded in takes[j]:
                if sharded:
                    take_own_columns(g_out[j], d0, nr, s0, c0, w)
                else:
                    put(g_out[j], d0, nr, s0, c0, w)
            d_out[j][...], m_out[j][...], v_out[j][...] = _adamw_math(w_in[j][...], g_out[j][...], m_in[j][...], v_in[j][...])
        loss_ref[...] = red_ref[loss_at[0]:loss_at[0] + 1, loss_at[1]:loss_at[1] + LANES]

    shapes = [_sds(w.shape, F32) for w in ws]
    outs = pl.pallas_call(
        body, name="small_update",
        in_specs=[pl.BlockSpec(memory_space=pltpu.SMEM)] + [VMEM] * (1 + 3 * n_w), out_specs=[VMEM] * (4 * n_w + 1),
        out_shape=shapes * 4 + [_sds((1, LANES), F32)],
        compiler_params=_params(),
    )(q_arr, red, *ws, *ms, *vs)
    return outs[0:n_w], outs[n_w:2 * n_w], outs[2 * n_w:3 * n_w], outs[3 * n_w:4 * n_w], outs[4 * n_w]


def _adamw_math(w, g, m, v):
    m2 = ADAM_B1 * m + (1.0 - ADAM_B1) * g
    v2 = ADAM_B2 * v + (1.0 - ADAM_B2) * (g * g)
    m_hat = m2 / (1.0 - ADAM_B1 ** ADAM_STEP)
    v_hat = v2 / (1.0 - ADAM_B2 ** ADAM_STEP)
    delta = -ADAM_LR * (m_hat / (jnp.sqrt(v_hat) + ADAM_EPS) + ADAM_WD * w)
    return delta, m2, v2


ADAMW_BLOCK_BYTES = 3 * 2 ** 19


def _adamw_big(ws, gs, ms, vs, tag):
    n = len(ws)

    def fits(steps):
        return all(w.shape[0] % steps == 0 and (w.shape[0] // steps) % SUBLANES == 0
                   and (w.shape[0] // steps) * w.shape[1] * 4 * n <= ADAMW_BLOCK_BYTES for w in ws)

    steps = next(s for s in range(1, min(w.shape[0] for w in ws) + 1) if fits(s))
    specs = [pl.BlockSpec((w.shape[0] // steps, w.shape[1]), lambda i: (i, 0)) for w in ws]

    def body(*refs):
        ins, outs = refs[:4 * n], refs[4 * n:]
        for i in range(n):
            w_ref, g_ref, m_ref, v_ref = ins[i], ins[n + i], ins[2 * n + i], ins[3 * n + i]
            gg = g_ref[...]
            outs[4 * i][...] = gg
            outs[4 * i + 1][...], outs[4 * i + 2][...], outs[4 * i + 3][...] = _adamw_math(
                w_ref[...], gg, m_ref[...], v_ref[...])

    outs = pl.pallas_call(
        body, name="adamw_" + tag, grid=(steps,),
        in_specs=specs * 4, out_specs=[s for s in specs for _ in range(4)],
        out_shape=[_sds(w.shape, F32) for w in ws for _ in range(4)],
        compiler_params=_params(("arbitrary",)),
    )(*ws, *gs, *ms, *vs)
    return [outs[4 * i:4 * i + 4] for i in range(n)]


SMALL_ROWS = 40
PACK_ROWS = 64


def kernel(x, meta_tokens, pre_mix_norm, w_in, conv_a_w, conv_b_w, conv_b_bias, ln_b_gain, ln_b_bias, w_out, post_mix_norm, pre_ffn_norm, w_gate, w_up, w_down, post_ffn_norm, loss_target, m_meta_tokens, m_pre_mix_norm, m_w_in, m_conv_a_w, m_conv_b_w, m_conv_b_bias, m_ln_b_gain, m_ln_b_bias, m_w_out, m_post_mix_norm, m_pre_ffn_norm, m_w_gate, m_w_up, m_w_down, m_post_ffn_norm, v_meta_tokens, v_pre_mix_norm, v_w_in, v_conv_a_w, v_conv_b_w, v_conv_b_bias, v_ln_b_gain, v_ln_b_bias, v_w_out, v_post_mix_norm, v_pre_ffn_norm, v_w_gate, v_w_up, v_w_down, v_post_ffn_norm):
    xq, yq, cq = lax.axis_index("x"), lax.axis_index("y"), lax.axis_index("c")
    chip = 2 * xq + yq
    c_arr = jnp.reshape(cq, (1,)).astype(jnp.int32)
    qc_arr = jnp.stack([chip, cq]).astype(jnp.int32)

    seq, d = x.shape[1], x.shape[2]
    x2, tgt2 = x[0], loss_target[0]
    tr = lambda a: jnp.swapaxes(a, 1, 2)[0]
    w_in2, w_out2, w_gate2, w_up2, w_down2 = w_in[0], w_out[0], tr(w_gate), tr(w_up), w_down[0]
    ka, wa_sh = conv_a_w.shape[1], conv_a_w.shape[2]
    kb = conv_b_w.shape[1]
    meta_sh = meta_tokens.shape[1]

    small = jnp.zeros((PACK_ROWS, meta_sh), F32)
    small = small.at[0:N_META, :].set(meta_tokens)
    small = small.at[16:16 + ka, 0:wa_sh].set(conv_a_w[0])
    small = small.at[24:24 + kb, 0:wa_sh].set(conv_b_w[0])
    q_arr = jnp.reshape(chip, (1,)).astype(jnp.int32)
    small_own = lax.dynamic_update_slice(jnp.zeros((N_CHIPS, PACK_ROWS, meta_sh), F32), small[None], (chip, 0, 0))
    i_ssem, i_rsem, first, i_token = _gather_start(_cast_place([w_in2], q_arr, "w_in") + [small_own], pre_mix_norm, "in")
    rest = _cast_place([w_out2, w_gate2, w_up2, w_down2], q_arr, "rest", i_token)
    g_ssem, g_rsem, lands, g_token = _gather_start(rest, i_token, "rest")
    win4, small4 = _forward_pair(_gather_wait([0, 1], i_ssem, i_rsem, first, g_token, "in"), "in")
    meta_f = jnp.concatenate([small4[k, 0:N_META, :] for k in range(N_CHIPS)], axis=1)
    wa_f = jnp.concatenate([small4[k, 16:16 + ka, 0:wa_sh] for k in range(N_CHIPS)], axis=1)
    wb_f = jnp.concatenate([small4[k, 24:24 + kb, 0:wa_sh] for k in range(N_CHIPS)], axis=1)

    tm = _row_tile(seq + TAIL_ROWS)
    tail = lax.dynamic_update_slice(jnp.zeros((tm, d), F32), meta_f, (seq % tm, 0))
    h, xn1, hp5 = _mm_in(x2, tail, win4, pre_mix_norm, g_token)
    ya, z = _mix_conv_fwd(hp5, wa_f, wb_f, conv_b_bias)
    (wout4,) = _forward_pair(_gather_wait([0], g_ssem, g_rsem, lands[0:1], z, "out"), "out")
    wout_f = wout4.reshape(N_CHIPS * wout4.shape[1], wout4.shape[2])
    yb, mix, h1, xn2 = _mm_out(ya, z, h, wout_f, ln_b_gain, ln_b_bias, post_mix_norm, pre_ffn_norm)
    wg4, wu4 = _forward_pair(_gather_wait([1, 2], g_ssem, g_rsem, lands[1:3], xn2, "gate_up"), "gate_up")
    stacked = lambda a: a.reshape(a.shape[0] * a.shape[1], a.shape[2])
    wg_f, wu_f = stacked(wg4), stacked(wu4)
    p_act, q_act, f_act = _ffn_up(xn2, wg_f, wu_f)
    (wd4,) = _forward_pair(_gather_wait([3], g_ssem, g_rsem, lands[3:4], f_act, "down"), "down")
    wd_f = stacked(wd4)
    dff, dh2, loss_blk, d_gpf = _ffn_down(f_act, wd_f, h1, tgt2, post_ffn_norm)

    da, du = _ffn_bwd_act(dff, wd_f, p_act, q_act)
    by_chip = lambda g: g.reshape(N_CHIPS, g.shape[0] // N_CHIPS, g.shape[1])
    g_down = by_chip(_grad_w_down(f_act, dff))
    g_gate, g_up = [by_chip(g) for g in _grad_w_gate_up(xn2, da, du)]
    ffn = [g_gate, g_up, g_down]
    p_ssem, p_rsem, ffn, p_lands, p_token = _pair_exchange_start(ffn, [1, 1, 1], dff, "ffn")
    dh1, dmix, d_g2, d_gpm = _ffn_bwd_in(da, du, wg_f, wu_f, h1, mix, dh2, pre_ffn_norm, post_mix_norm, p_token)
    ffn, got = _pair_exchange_wait(p_ssem, p_rsem, ffn, p_lands, [1, 1, 1], d_g2, "ffn")
    parts = _pair_sum(ffn, got, c_arr, [False] * 3, "ffn")
    f_ssem, f_rsem, parts, f_lands, f_token = _chip_exchange_start(parts, dff, "ffn")
    g_out = _grad_w_out(ya, yb, dmix, f_token)
    dya, dz, d_lg, d_lb = _mix_bwd_out(dmix, wout_f, z, ln_b_gain, ln_b_bias, f_token)
    dhp5, d_wa, d_wb, d_bb = _mix_conv_bwd(hp5, dya, dz, wa_f, wb_f)
    g_in = _grad_w_in(xn1, dhp5)

    g_out4 = g_out.reshape(N_CHIPS, g_out.shape[0] // N_CHIPS, g_out.shape[1])
    mixw = [g_in, g_out4]
    got2 = _pair_exchange_grads(mixw, [0, 1], "mix")
    parts2 = _pair_sum(mixw, got2, c_arr, [True, False], "mix")
    m_ssem, m_rsem, parts2, m_lands, m_token = _chip_exchange_start(parts2, dhp5, "mix")
    grad_x2, d_meta, d_g1 = _mix_bwd_in(dhp5, win4, h, dh1, pre_mix_norm, m_token)
    grad_x = grad_x2[None]

    parts, f_recv = _chip_exchange_wait(f_ssem, f_rsem, parts, f_lands, d_g1, "ffn")
    halves = _chip_sum(parts, f_recv, qc_arr, "ffn")
    s_ssem, s_rsem, halves, s_token = _pair_share_start(halves, d_g1, "ffn")

    hw = d // 2
    assert d_wa.shape == (3, hw) and d_wb.shape == (31, hw) and d_bb.shape == (1, hw)
    small_parts = [d_meta, d_g1, d_gpm, d_g2, d_gpf, d_bb, d_lg, d_lb, loss_blk[0:1, :], d_wa, d_wb]
    places = [[(0, 0, 0, N_META)], [(16, 0, 0, 1)], [(17, 0, 0, 1)], [(18, 0, 0, 1)], [(19, 0, 0, 1)],
              [(20, 0, 0, 1)], [(20, hw, 0, 1)], [(21, 0, 0, 1)], [(21, hw, 0, 1)], [(22, 0, 0, 3)],
              [(22, hw, 0, 3), (25, 0, 3, 14), (25, hw, 17, 14)]]
    red = _small_allreduce(small_parts, places, SMALL_ROWS, d, s_token)
    gsum_ffn = _pair_share_wait(s_ssem, s_rsem, halves, red, "ffn")

    names_big = ["w_in", "w_out", "w_gate", "w_up", "w_down"]
    w_big = dict(zip(names_big, [w_in2, w_out2, w_gate2, w_up2, w_down2]))
    m_big = dict(zip(names_big, [m_w_in[0], m_w_out[0], tr(m_w_gate), tr(m_w_up), m_w_down[0]]))
    v_big = dict(zip(names_big, [v_w_in[0], v_w_out[0], tr(v_w_gate), tr(v_w_up), v_w_down[0]]))
    grads, deltas, new_m, new_v = {}, {}, {}, {}

    def update(names, gs, tag):
        res = _adamw_big([w_big[k] for k in names], gs, [m_big[k] for k in names], [v_big[k] for k in names], tag)
        for nm, outs in zip(names, res):
            if nm in ("w_gate", "w_up"):
                outs = [jnp.swapaxes(o[None], 1, 2) for o in outs]
            else:
                outs = [o[None] for o in outs]
            grads[nm], deltas[nm], new_m[nm], new_v[nm] = outs
        return res[-1][1]

    last = update(["w_gate", "w_up", "w_down"], list(gsum_ffn), "ffn")

    names_small = ["meta_tokens", "pre_mix_norm", "conv_a_w", "conv_b_w", "conv_b_bias", "ln_b_gain", "ln_b_bias",
                   "post_mix_norm", "pre_ffn_norm", "post_ffn_norm"]
    takes = [[(0, N_META, 0, 0, True)], [(0, 1, 16, 0, False)], [(0, 3, 22, 0, True)],
             [(0, 3, 22, hw, True), (3, 14, 25, 0, True), (17, 14, 25, hw, True)], [(0, 1, 20, 0, False)],
             [(0, 1, 20, hw, False)], [(0, 1, 21, 0, False)], [(0, 1, 17, 0, False)], [(0, 1, 18, 0, False)],
             [(0, 1, 19, 0, False)]]
    taps = lambda a: jnp.swapaxes(a, 0, 1)
    w_small = [meta_tokens, pre_mix_norm, taps(conv_a_w), taps(conv_b_w), conv_b_bias, ln_b_gain, ln_b_bias, post_mix_norm,
               pre_ffn_norm, post_ffn_norm]
    m_small = [m_meta_tokens, m_pre_mix_norm, taps(m_conv_a_w), taps(m_conv_b_w), m_conv_b_bias, m_ln_b_gain, m_ln_b_bias,
               m_post_mix_norm, m_pre_ffn_norm, m_post_ffn_norm]
    v_small = [v_meta_tokens, v_pre_mix_norm, taps(v_conv_a_w), taps(v_conv_b_w), v_conv_b_bias, v_ln_b_gain, v_ln_b_bias,
               v_post_mix_norm, v_pre_ffn_norm, v_post_ffn_norm]
    g_s, d_s, m_s, v_s, loss_row = _small_update(red, q_arr, takes, (21, hw), w_small, m_small, v_small)
    loss = loss_row[0, 0]
    for i, nm in enumerate(names_small):
        fix = taps if nm in ("conv_a_w", "conv_b_w") else (lambda a: a)
        grads[nm], deltas[nm], new_m[nm], new_v[nm] = fix(g_s[i]), fix(d_s[i]), fix(m_s[i]), fix(v_s[i])

    parts2, m_recv = _chip_exchange_wait(m_ssem, m_rsem, parts2, m_lands, last, "mix")
    halves2 = _chip_sum(parts2, m_recv, qc_arr, "mix")
    gsum_mix = _pair_share_grads(halves2, "mix")
    update(["w_in", "w_out"], list(gsum_mix), "mix")

    order = ["meta_tokens", "pre_mix_norm", "w_in", "conv_a_w", "conv_b_w", "conv_b_bias", "ln_b_gain", "ln_b_bias", "w_out",
             "post_mix_norm", "pre_ffn_norm", "w_gate", "w_up", "w_down", "post_ffn_norm"]
    return (loss, grad_x, *[grads[k] for k in order], *[deltas[k] for k in order], *[new_m[k] for k in order],
            *[new_v[k] for k in order])
```

```python
import jax
import jax.numpy as jnp
from jax import lax
from jax.experimental import pallas as pl
from jax.experimental.pallas import tpu as pltpu

F32 = jnp.float32
BF16 = jnp.bfloat16
MESH = pl.DeviceIdType.MESH

N_META = 16
TAIL_ROWS = 128
RMS_EPS = 1e-6
LN_EPS = 1e-5
ADAM_LR = 0.001
ADAM_B1 = 0.9
ADAM_B2 = 0.999
ADAM_EPS = 1e-08
ADAM_WD = 0.01
ADAM_STEP = 10

N_CHIPS = 4
LANES = 128
SUBLANES = 8
BF16_ROWS = 16
MXU_TILE = 256
CONV_CHUNK = 48
CONV_HIST = 32
ROW_TILE_CAP = 640
VMEM_LIMIT = 56 * 1024 * 1024

NN = (((1,), (0,)), ((), ()))
NT = (((1,), (1,)), ((), ()))
TN = (((0,), (0,)), ((), ()))


def _dot(a, b, dims=NN):
    return lax.dot_general(a, b, dims, preferred_element_type=F32)


def _sig(v):
    return 1.0 / (1.0 + jnp.exp(-v))


def _mean(v):
    return jnp.mean(v, axis=-1, keepdims=True)


def _row_tile(rows):
    best = BF16_ROWS
    for t in range(BF16_ROWS, min(rows, ROW_TILE_CAP) + 1, BF16_ROWS):
        if rows % t == 0:
            best = t
    assert rows % best == 0
    return best


def _row_parts(tm, parts=2):
    units = tm // BF16_ROWS
    if tm % BF16_ROWS or units < parts:
        return [slice(0, tm)]
    cuts = [BF16_ROWS * ((units * k + parts - 1) // parts) for k in range(parts + 1)]
    return [slice(lo, hi) for lo, hi in zip(cuts[:-1], cuts[1:])]


def _concat_shards(w_ref, wcat_ref):
    n_sh, _, csh = w_ref.shape

    @pl.when(pl.program_id(0) == 0)
    def _():
        for k in range(n_sh):
            wcat_ref[:, k * csh:(k + 1) * csh] = w_ref[k]


def _params(semantics=None):
    kw = dict(vmem_limit_bytes=VMEM_LIMIT)
    if semantics is not None:
        kw["dimension_semantics"] = semantics
    return pltpu.CompilerParams(**kw)


def _full(shape):
    nd = len(shape)
    return pl.BlockSpec(shape, lambda *_: (0,) * nd)


def _resident(shape):
    nd = len(shape)
    return pl.BlockSpec(shape, lambda *_: (0,) * nd, pipeline_mode=pl.Buffered(1))


def _sds(shape, dtype):
    return jax.ShapeDtypeStruct(shape, dtype)


ANY = pl.BlockSpec(memory_space=pl.ANY)
VMEM = pl.BlockSpec(memory_space=pltpu.VMEM)


def _mesh_pos():
    return lax.axis_index("x"), lax.axis_index("y"), lax.axis_index("c")


def _flip(v, bit):
    return 1 - v if bit else v


def _mm_in(x, tail, win4, g1, after):
    seq, d = x.shape
    tp = seq + TAIL_ROWS
    tm = _row_tile(tp)
    n_sh, _, csh = win4.shape
    pw = n_sh * csh // 5

    def body(x_ref, tail_ref, w_ref, g_ref, after_ref, h_ref, xn_ref, hp_ref, wcat_ref):
        _concat_shards(w_ref, wcat_ref)
        rows = pl.program_id(0) * tm + lax.broadcasted_iota(jnp.int32, (tm, 1), 0)
        hh = jnp.where(rows < seq, x_ref[...], tail_ref[...])
        h_ref[...] = hh
        r = lax.rsqrt(_mean(hh * hh) + RMS_EPS)
        xn = (hh * r * g_ref[...]).astype(BF16)
        xn_ref[...] = xn
        for p in range(5):
            hp_ref[p] = _dot(xn, wcat_ref[:, p * pw:(p + 1) * pw])

    row = pl.BlockSpec((tm, d), lambda i: (i, 0))
    return pl.pallas_call(
        body, name="mm_in", grid=(tp // tm,),
        in_specs=[row, _full(tail.shape), _resident(win4.shape), _full(g1.shape), ANY],
        out_specs=[row, row, pl.BlockSpec((5, tm, pw), lambda i: (0, i, 0))],
        out_shape=[_sds((tp, d), F32), _sds((tp, d), BF16), _sds((5, tp, pw), F32)],
        scratch_shapes=[pltpu.VMEM((d, n_sh * csh), BF16)],
        compiler_params=_params(("arbitrary",)),
    )(x, tail, win4, g1, after)


def _seq_rows(tp):
    seq = tp - TAIL_ROWS
    nseq = seq + N_META
    assert nseq % CONV_CHUNK == 0 and seq % BF16_ROWS == 0
    return seq, nseq


def _conv_offsets(width, transpose):
    return [(width - 1 - k) if transpose else (CONV_HIST - (width - 1) + k) for k in range(width)]


def _shift_copies(src_ref, sh_ref, width, transpose):
    n = src_ref.shape[0] - SUBLANES
    for s in sorted({o % SUBLANES for o in _conv_offsets(width, transpose)} - {0}):
        sh_ref[s - 1, 0:n, :] = src_ref[s:s + n, :]


def _tap_rows(src_ref, sh_ref, base, off):
    start = pl.multiple_of(base + (off // SUBLANES) * SUBLANES, SUBLANES)
    if off % SUBLANES == 0:
        return src_ref[pl.ds(start, CONV_CHUNK), :]
    return sh_ref[off % SUBLANES - 1, pl.ds(start, CONV_CHUNK), :]


def _conv_taps(src_ref, sh_ref, w_ref, dst_ref, width, nseq, transpose, shifted=False):
    w = w_ref[...]
    offs = _conv_offsets(width, transpose)
    if not shifted:
        _shift_copies(src_ref, sh_ref, width, transpose)

    def step(n, carry):
        out0 = pl.multiple_of(CONV_HIST + n * CONV_CHUNK, SUBLANES)
        base = out0 if transpose else n * CONV_CHUNK
        acc = jnp.zeros((CONV_CHUNK, w.shape[1]), F32)
        for k, off in enumerate(offs):
            acc = acc + w[k:k + 1, :] * _tap_rows(src_ref, sh_ref, base, off)
        dst_ref[pl.ds(out0, CONV_CHUNK), :] = acc
        return carry

    lax.fori_loop(0, nseq // CONV_CHUNK, step, 0)


def _conv_wgrad(src_ref, sh_ref, dz_ref, acc_ref, width, nseq):
    acc_ref[...] = jnp.zeros(acc_ref.shape, F32)
    offs = _conv_offsets(width, False)

    def step(n, carry):
        dzc = dz_ref[pl.ds(pl.multiple_of(CONV_HIST + n * CONV_CHUNK, SUBLANES), CONV_CHUNK), :]
        for k, off in enumerate(offs):
            prod = dzc * _tap_rows(src_ref, sh_ref, n * CONV_CHUNK, off)
            part = prod[0:SUBLANES, :]
            for s in range(1, CONV_CHUNK // SUBLANES):
                part = part + prod[SUBLANES * s:SUBLANES * (s + 1), :]
            acc_ref[SUBLANES * k:SUBLANES * (k + 1), :] += part
        return carry

    lax.fori_loop(0, nseq // CONV_CHUNK, step, 0)


def _conv_wgrad_by_dz_shifts(src_ref, dz_ref, shd_ref, acc_ref, width, nseq):
    acc_ref[...] = jnp.zeros(acc_ref.shape, F32)
    offs = _conv_offsets(width, True)

    def step(n, carry):
        base = pl.multiple_of(CONV_HIST + n * CONV_CHUNK, SUBLANES)
        rows = src_ref[pl.ds(base, CONV_CHUNK), :]
        for k, off in enumerate(offs):
            prod = rows * _tap_rows(dz_ref, shd_ref, base, off)
            part = prod[0:SUBLANES, :]
            for s in range(1, CONV_CHUNK // SUBLANES):
                part = part + prod[SUBLANES * s:SUBLANES * (s + 1), :]
            acc_ref[SUBLANES * k:SUBLANES * (k + 1), :] += part
        return carry

    lax.fori_loop(0, nseq // CONV_CHUNK, step, 0)


def _to_seq(buf_ref, x_part, meta_part, seq):
    buf_ref[CONV_HIST:CONV_HIST + N_META, :] = meta_part
    buf_ref[CONV_HIST + N_META:CONV_HIST + N_META + seq, :] = x_part


def _zero_ends(buf_ref, nseq):
    zeros = jnp.zeros((CONV_HIST, buf_ref.shape[1]), F32)
    buf_ref[0:CONV_HIST, :] = zeros
    buf_ref[CONV_HIST + nseq:CONV_HIST + nseq + CONV_HIST, :] = zeros


def _mix_conv_fwd(hp5, wa, wb, bb):
    _, tp, wgrp = hp5.shape
    seq, nseq = _seq_rows(tp)
    sb = nseq + 2 * CONV_HIST
    ka, kb = wa.shape[0], wb.shape[0]
    xs, ms = slice(0, seq), slice(seq, seq + N_META)
    ox, om = slice(CONV_HIST + N_META, CONV_HIST + nseq), slice(CONV_HIST, CONV_HIST + N_META)

    def body(hp_ref, wa_ref, wb_ref, bb_ref, ya_ref, z_ref, s_ref, o_ref, sh_ref):
        _zero_ends(s_ref, nseq)
        _to_seq(s_ref, hp_ref[1, xs, :] * hp_ref[2, xs, :], hp_ref[1, ms, :] * hp_ref[2, ms, :], seq)
        _conv_taps(s_ref, sh_ref, wa_ref, o_ref, ka, nseq, False)
        ya_ref[xs, :] = (hp_ref[0, xs, :] * o_ref[ox, :]).astype(BF16)
        ya_ref[ms, :] = (hp_ref[0, ms, :] * o_ref[om, :]).astype(BF16)
        ya_ref[seq + N_META:tp, :] = jnp.zeros((tp - seq - N_META, LANES), BF16)
        _to_seq(s_ref, hp_ref[3, xs, :] * _sig(hp_ref[4, xs, :]), hp_ref[3, ms, :] * _sig(hp_ref[4, ms, :]), seq)
        _conv_taps(s_ref, sh_ref, wb_ref, o_ref, kb, nseq, False)
        z_ref[xs, :] = o_ref[ox, :] + bb_ref[...]
        z_ref[ms, :] = o_ref[om, :] + bb_ref[...]
        z_ref[seq + N_META:tp, :] = jnp.zeros((tp - seq - N_META, LANES), F32)

    col = lambda j: (0, j)
    return pl.pallas_call(
        body, name="mix_conv_fwd", grid=(wgrp // LANES,),
        in_specs=[pl.BlockSpec((5, tp, LANES), lambda j: (0, 0, j)), pl.BlockSpec((ka, LANES), col),
                  pl.BlockSpec((kb, LANES), col), pl.BlockSpec((1, LANES), col)],
        out_specs=[pl.BlockSpec((tp, LANES), col), pl.BlockSpec((tp, LANES), col)],
        out_shape=[_sds((tp, wgrp), BF16), _sds((tp, wgrp), F32)],
        scratch_shapes=[pltpu.VMEM((sb, LANES), F32), pltpu.VMEM((sb, LANES), F32),
                        pltpu.VMEM((SUBLANES - 1, sb, LANES), F32)],
        compiler_params=_params(("arbitrary",)),
    )(hp5, wa, wb, bb)


def _layer_norm_parts(z, lg, lb):
    mu = _mean(z)
    zc = z - mu
    rl = lax.rsqrt(_mean(zc * zc) + LN_EPS)
    zh = zc * rl
    return rl, zh, zh * lg + lb


def _mm_out(ya, z, h, wout, lg, lb, gpm, g2):
    tp, d = h.shape
    wa_ = ya.shape[1]
    tm = _row_tile(tp)

    def body(ya_ref, z_ref, h_ref, w_ref, lg_ref, lb_ref, gpm_ref, g2_ref, yb_ref, mix_ref, h1_ref, xn2_ref):
        for rs in _row_parts(tm, 3):
            _, _, l = _layer_norm_parts(z_ref[rs, :], lg_ref[...], lb_ref[...])
            yb = (l * _sig(l)).astype(BF16)
            yb_ref[rs, :] = yb
            mix = _dot(ya_ref[rs, :], w_ref[0:wa_, :]) + _dot(yb, w_ref[wa_:d, :])
            mix_ref[rs, :] = mix
            rm = lax.rsqrt(_mean(mix * mix) + RMS_EPS)
            h1 = h_ref[rs, :] + mix * rm * gpm_ref[...]
            h1_ref[rs, :] = h1
            r2 = lax.rsqrt(_mean(h1 * h1) + RMS_EPS)
            xn2_ref[rs, :] = (h1 * r2 * g2_ref[...]).astype(BF16)

    row = lambda i: (i, 0)
    return pl.pallas_call(
        body, name="mm_out", grid=(tp // tm,),
        in_specs=[pl.BlockSpec((tm, wa_), row), pl.BlockSpec((tm, wa_), row), pl.BlockSpec((tm, d), row),
                  _resident(wout.shape), _full(lg.shape), _full(lb.shape), _full(gpm.shape), _full(g2.shape)],
        out_specs=[pl.BlockSpec((tm, wa_), row), pl.BlockSpec((tm, d), row), pl.BlockSpec((tm, d), row),
                   pl.BlockSpec((tm, d), row)],
        out_shape=[_sds((tp, wa_), BF16), _sds((tp, d), F32), _sds((tp, d), F32), _sds((tp, d), BF16)],
        compiler_params=_params(("arbitrary",)),
    )(ya, z, h, wout, lg, lb, gpm, g2)


def _ffn_up(xn2, wg, wu):
    tp, d = xn2.shape
    ff_dim = wg.shape[0]
    tm = _row_tile(tp)
    assert ff_dim % MXU_TILE == 0

    def body(xn_ref, wg_ref, wu_ref, p_ref, q_ref, f_ref):
        xn = xn_ref[...]
        for lo in range(0, ff_dim, MXU_TILE):
            cols = slice(lo, lo + MXU_TILE)
            a = _dot(xn, wg_ref[cols, :], NT)
            u = _dot(xn, wu_ref[cols, :], NT)
            s = _sig(a)
            q = a * s
            p_ref[:, cols] = (u * (s + q * (1.0 - s))).astype(BF16)
            q_ref[:, cols] = q.astype(BF16)
            f_ref[:, cols] = (q * u).astype(BF16)

    ospec = pl.BlockSpec((tm, ff_dim), lambda i: (i, 0))
    return pl.pallas_call(
        body, name="ffn_up", grid=(tp // tm,),
        in_specs=[pl.BlockSpec((tm, d), lambda i: (i, 0)), _resident(wg.shape), _resident(wu.shape)],
        out_specs=[ospec, ospec, ospec],
        out_shape=[_sds((tp, ff_dim), BF16)] * 3,
        compiler_params=_params(("arbitrary",)),
    )(xn2, wg, wu)


def _ffn_down(f, wd, h1, tgt, gpf):
    tp, ff_dim = f.shape
    d = h1.shape[1]
    tm = _row_tile(tp)
    seq, _ = _seq_rows(tp)

    def body(f_ref, w_ref, h1_ref, t_ref, gpf_ref, dff_ref, dh2_ref, loss_ref, dgpf_ref):
        i = pl.program_id(0)
        gpf_ = gpf_ref[...]

        @pl.when(i == 0)
        def _():
            loss_ref[...] = jnp.zeros(loss_ref.shape, F32)
            dgpf_ref[...] = jnp.zeros(dgpf_ref.shape, F32)

        for rs in _row_parts(tm):
            ff = _dot(f_ref[rs, :], w_ref[...])
            rf = lax.rsqrt(_mean(ff * ff) + RMS_EPS)
            nf = ff * rf
            h2 = h1_ref[rs, :] + nf * gpf_
            rows = i * tm + rs.start + lax.broadcasted_iota(jnp.int32, (rs.stop - rs.start, 1), 0)
            err = jnp.where(rows < seq, h2 - t_ref[rs, :], 0.0)
            dh2 = err * (1.0 / d)
            dh2_ref[rs, :] = dh2
            dn = dh2 * gpf_
            dff_ref[rs, :] = (rf * (dn - nf * _mean(dn * nf))).astype(BF16)
            loss_ref[...] += (0.5 / d) * jnp.sum(err * err, axis=(0, 1), keepdims=True)
            dgpf_ref[...] += jnp.sum(dh2 * nf, axis=0, keepdims=True)

    row = lambda i: (i, 0)
    return pl.pallas_call(
        body, name="ffn_down", grid=(tp // tm,),
        in_specs=[pl.BlockSpec((tm, ff_dim), row), _resident(wd.shape), pl.BlockSpec((tm, d), row),
                  pl.BlockSpec((tm, d), row), _full(gpf.shape)],
        out_specs=[pl.BlockSpec((tm, d), row), pl.BlockSpec((tm, d), row), _full((SUBLANES, LANES)), _full((1, d))],
        out_shape=[_sds((tp, d), BF16), _sds((tp, d), F32), _sds((SUBLANES, LANES), F32), _sds((1, d), F32)],
        compiler_params=_params(("arbitrary",)),
    )(f, wd, h1, tgt, gpf)


def _ffn_bwd_act(dff, wd, p, q):
    tp, d = dff.shape
    ff_dim = wd.shape[0]
    tm = _row_tile(tp)

    def body(dff_ref, w_ref, p_ref, q_ref, da_ref, du_ref):
        dffv = dff_ref[...]
        for lo in range(0, ff_dim, MXU_TILE):
            cols = slice(lo, lo + MXU_TILE)
            df = _dot(dffv, w_ref[cols, :], NT).astype(BF16)
            da_ref[:, cols] = df * p_ref[:, cols]
            du_ref[:, cols] = df * q_ref[:, cols]

    aspec = pl.BlockSpec((tm, ff_dim), lambda i: (i, 0))
    return pl.pallas_call(
        body, name="ffn_bwd_act", grid=(tp // tm,),
        in_specs=[pl.BlockSpec((tm, d), lambda i: (i, 0)), _resident(wd.shape), aspec, aspec],
        out_specs=[aspec, aspec],
        out_shape=[_sds((tp, ff_dim), BF16)] * 2,
        compiler_params=_params(("arbitrary",)),
    )(dff, wd, p, q)


def _grad_blocks(ff_dim):
    rows = ff_dim // 2
    assert rows % LANES == 0
    return rows


def _grad_w_down(f, dff):
    tp, ff_dim = f.shape
    d = dff.shape[1]
    rows = _grad_blocks(ff_dim)

    def body(f_ref, dff_ref, g_ref):
        g_ref[...] = _dot(f_ref[...], dff_ref[...], TN).astype(BF16)

    return pl.pallas_call(
        body, name="grad_w_down", grid=(ff_dim // rows,),
        in_specs=[pl.BlockSpec((tp, rows), lambda k: (0, k)), _resident(dff.shape)],
        out_specs=pl.BlockSpec((rows, d), lambda k: (k, 0)),
        out_shape=_sds((ff_dim, d), BF16),
        compiler_params=_params(("arbitrary",)),
    )(f, dff)


def _grad_w_gate_up(xn2, da, du):
    tp, ff_dim = da.shape
    d = xn2.shape[1]
    rows = _grad_blocks(ff_dim)

    def body(xn_ref, da_ref, du_ref, gg_ref, gu_ref):
        xn = xn_ref[...]
        gg_ref[...] = _dot(da_ref[...], xn, TN).astype(BF16)
        gu_ref[...] = _dot(du_ref[...], xn, TN).astype(BF16)

    aspec = pl.BlockSpec((tp, rows), lambda k: (0, k))
    gspec = pl.BlockSpec((rows, d), lambda k: (k, 0))
    return pl.pallas_call(
        body, name="grad_w_gate_up", grid=(ff_dim // rows,),
        in_specs=[_resident(xn2.shape), aspec, aspec],
        out_specs=[gspec, gspec],
        out_shape=[_sds((ff_dim, d), BF16)] * 2,
        compiler_params=_params(("arbitrary",)),
    )(xn2, da, du)


def _rms_bwd(dy, x, r, g):
    n = x * r
    dn = dy * g
    return r * (dn - n * _mean(dn * n)), dy * n


def _ffn_bwd_in(da, du, wg, wu, h1, mix, dh2, g2, gpm, after):
    tp, ff_dim = da.shape
    d = h1.shape[1]
    tm = _row_tile(tp)

    def body(da_ref, du_ref, wg_ref, wu_ref, h1_ref, mix_ref, dh2_ref, g2_ref, gpm_ref, after_ref,
             dh1_ref, dmix_ref, dg2_ref, dgpm_ref):
        i = pl.program_id(0)

        @pl.when(i == 0)
        def _():
            dg2_ref[...] = jnp.zeros(dg2_ref.shape, F32)
            dgpm_ref[...] = jnp.zeros(dgpm_ref.shape, F32)

        for rs in _row_parts(tm, 3):
            dxn = _dot(da_ref[rs, :], wg_ref[...]) + _dot(du_ref[rs, :], wu_ref[...])
            h1v = h1_ref[rs, :]
            r2 = lax.rsqrt(_mean(h1v * h1v) + RMS_EPS)
            dres, dg2_rows = _rms_bwd(dxn, h1v, r2, g2_ref[...])
            dh1 = dh2_ref[rs, :] + dres
            dh1_ref[rs, :] = dh1
            mixv = mix_ref[rs, :]
            rm = lax.rsqrt(_mean(mixv * mixv) + RMS_EPS)
            dmix, dgpm_rows = _rms_bwd(dh1, mixv, rm, gpm_ref[...])
            dmix_ref[rs, :] = dmix.astype(BF16)
            dg2_ref[...] += jnp.sum(dg2_rows, axis=0, keepdims=True)
            dgpm_ref[...] += jnp.sum(dgpm_rows, axis=0, keepdims=True)

    aspec = pl.BlockSpec((tm, ff_dim), lambda i: (i, 0))
    row = pl.BlockSpec((tm, d), lambda i: (i, 0))
    return pl.pallas_call(
        body, name="ffn_bwd_in", grid=(tp // tm,),
        in_specs=[aspec, aspec, _resident(wg.shape), _resident(wu.shape), row, row, row, _full(g2.shape), _full(gpm.shape),
                  ANY],
        out_specs=[row, row, _full((1, d)), _full((1, d))],
        out_shape=[_sds((tp, d), F32), _sds((tp, d), BF16), _sds((1, d), F32), _sds((1, d), F32)],
        compiler_params=_params(("arbitrary",)),
    )(da, du, wg, wu, h1, mix, dh2, g2, gpm, after)


def _grad_w_out(ya, yb, dmix, after):
    tp, wa_ = ya.shape
    d = dmix.shape[1]

    def body(ya_ref, yb_ref, dmix_ref, after_ref, g_ref):
        dm = dmix_ref[...]
        g_ref[0:wa_, :] = _dot(ya_ref[...], dm, TN).astype(BF16)
        g_ref[wa_:2 * wa_, :] = _dot(yb_ref[...], dm, TN).astype(BF16)

    return pl.pallas_call(
        body, name="grad_w_out", grid=(1,),
        in_specs=[_full(ya.shape), _full(yb.shape), _full(dmix.shape), ANY],
        out_specs=_full((2 * wa_, d)),
        out_shape=_sds((2 * wa_, d), BF16),
        compiler_params=_params(("arbitrary",)),
    )(ya, yb, dmix, after)


def _mix_bwd_out(dmix, wout, z, lg, lb, after):
    tp, d = dmix.shape
    wa_ = z.shape[1]
    tm = _row_tile(tp)

    def body(dmix_ref, w_ref, z_ref, lg_ref, lb_ref, after_ref, dya_ref, dz_ref, dlg_ref, dlb_ref):
        i = pl.program_id(0)
        lg_ = lg_ref[...]

        @pl.when(i == 0)
        def _():
            dlg_ref[...] = jnp.zeros(dlg_ref.shape, F32)
            dlb_ref[...] = jnp.zeros(dlb_ref.shape, F32)

        for rs in _row_parts(tm):
            dm = dmix_ref[rs, :]
            dya_ref[rs, :] = _dot(dm, w_ref[0:wa_, :], NT)
            dyb = _dot(dm, w_ref[wa_:d, :], NT)
            rl, zh, l = _layer_norm_parts(z_ref[rs, :], lg_, lb_ref[...])
            sl = _sig(l)
            dl = dyb * (sl * (1.0 + l * (1.0 - sl)))
            dzh = dl * lg_
            dz_ref[rs, :] = rl * (dzh - _mean(dzh) - zh * _mean(dzh * zh))
            dlg_ref[...] += jnp.sum(dl * zh, axis=0, keepdims=True)
            dlb_ref[...] += jnp.sum(dl, axis=0, keepdims=True)

    row = lambda i: (i, 0)
    return pl.pallas_call(
        body, name="mix_bwd_out", grid=(tp // tm,),
        in_specs=[pl.BlockSpec((tm, d), row), _resident(wout.shape), pl.BlockSpec((tm, wa_), row), _full(lg.shape),
                  _full(lb.shape), ANY],
        out_specs=[pl.BlockSpec((tm, wa_), row), pl.BlockSpec((tm, wa_), row), _full((1, wa_)), _full((1, wa_))],
        out_shape=[_sds((tp, wa_), F32), _sds((tp, wa_), F32), _sds((1, wa_), F32), _sds((1, wa_), F32)],
        compiler_params=_params(("arbitrary",)),
    )(dmix, wout, z, lg, lb, after)


def _mix_conv_bwd(hp5, dya, dz, wa, wb):
    _, tp, wgrp = hp5.shape
    seq, nseq = _seq_rows(tp)
    sb = nseq + 2 * CONV_HIST
    ka, kb = wa.shape[0], wb.shape[0]
    xs, ms = slice(0, seq), slice(seq, seq + N_META)
    ox, om = slice(CONV_HIST + N_META, CONV_HIST + nseq), slice(CONV_HIST, CONV_HIST + N_META)
    n_tail = tp - seq - N_META

    def body(hp_ref, dya_ref, dz_ref, wa_ref, wb_ref, dhp_ref, dwa_ref, dwb_ref, dbb_ref, s_ref, d_ref, o_ref, acc_ref,
             shs_ref, shd_ref):
        _zero_ends(s_ref, nseq)
        _zero_ends(d_ref, nseq)

        def put(p, ox_val, om_val):
            dhp_ref[p, xs, :] = ox_val.astype(BF16)
            dhp_ref[p, ms, :] = om_val.astype(BF16)
            dhp_ref[p, seq + N_META:tp, :] = jnp.zeros((n_tail, LANES), BF16)

        def wgrad(dw_ref, width):
            for k in range(width):
                dw_ref[k:k + 1, :] = jnp.sum(acc_ref[8 * k:8 * k + 8, :], axis=0, keepdims=True)

        _to_seq(s_ref, hp_ref[1, xs, :] * hp_ref[2, xs, :], hp_ref[1, ms, :] * hp_ref[2, ms, :], seq)
        _conv_taps(s_ref, shs_ref, wa_ref, o_ref, ka, nseq, False)
        put(0, dya_ref[xs, :] * o_ref[ox, :], dya_ref[ms, :] * o_ref[om, :])
        _to_seq(d_ref, dya_ref[xs, :] * hp_ref[0, xs, :], dya_ref[ms, :] * hp_ref[0, ms, :], seq)
        _conv_wgrad(s_ref, shs_ref, d_ref, acc_ref, ka, nseq)
        wgrad(dwa_ref, ka)
        _conv_taps(d_ref, shd_ref, wa_ref, o_ref, ka, nseq, True)
        put(1, o_ref[ox, :] * hp_ref[2, xs, :], o_ref[om, :] * hp_ref[2, ms, :])
        put(2, o_ref[ox, :] * hp_ref[1, xs, :], o_ref[om, :] * hp_ref[1, ms, :])

        _to_seq(s_ref, hp_ref[3, xs, :] * _sig(hp_ref[4, xs, :]), hp_ref[3, ms, :] * _sig(hp_ref[4, ms, :]), seq)
        _to_seq(d_ref, dz_ref[xs, :], dz_ref[ms, :], seq)
        dbb_ref[...] = (jnp.sum(dz_ref[xs, :], axis=0, keepdims=True)
                        + jnp.sum(dz_ref[ms, :], axis=0, keepdims=True))
        _shift_copies(d_ref, shd_ref, kb, True)
        _conv_wgrad_by_dz_shifts(s_ref, d_ref, shd_ref, acc_ref, kb, nseq)
        wgrad(dwb_ref, kb)
        _conv_taps(d_ref, shd_ref, wb_ref, o_ref, kb, nseq, True, shifted=True)
        sx, sm = _sig(hp_ref[4, xs, :]), _sig(hp_ref[4, ms, :])
        put(3, o_ref[ox, :] * sx, o_ref[om, :] * sm)
        put(4, o_ref[ox, :] * hp_ref[3, xs, :] * sx * (1.0 - sx), o_ref[om, :] * hp_ref[3, ms, :] * sm * (1.0 - sm))

    col = lambda j: (0, j)
    blk5 = pl.BlockSpec((5, tp, LANES), lambda j: (0, 0, j))
    return pl.pallas_call(
        body, name="mix_conv_bwd", grid=(wgrp // LANES,),
        in_specs=[blk5, pl.BlockSpec((tp, LANES), col), pl.BlockSpec((tp, LANES), col),
                  pl.BlockSpec((ka, LANES), col), pl.BlockSpec((kb, LANES), col)],
        out_specs=[blk5, pl.BlockSpec((ka, LANES), col), pl.BlockSpec((kb, LANES), col), pl.BlockSpec((1, LANES), col)],
        out_shape=[_sds((5, tp, wgrp), BF16), _sds((ka, wgrp), F32), _sds((kb, wgrp), F32), _sds((1, wgrp), F32)],
        scratch_shapes=[pltpu.VMEM((sb, LANES), F32), pltpu.VMEM((sb, LANES), F32), pltpu.VMEM((sb, LANES), F32),
                        pltpu.VMEM((SUBLANES * kb, LANES), F32), pltpu.VMEM((SUBLANES - 1, sb, LANES), F32),
                        pltpu.VMEM((SUBLANES - 1, sb, LANES), F32)],
        compiler_params=_params(("arbitrary",)),
    )(hp5, dya, dz, wa, wb)


def _grad_w_in(xn1, dhp5):
    n_p, tp, pw = dhp5.shape
    d = xn1.shape[1]

    def body(xn_ref, dhp_ref, g_ref):
        g_ref[...] = _dot(xn_ref[...], dhp_ref[0], TN).astype(BF16)

    return pl.pallas_call(
        body, name="grad_w_in", grid=(n_p,),
        in_specs=[_resident(xn1.shape), pl.BlockSpec((1, tp, pw), lambda p: (p, 0, 0))],
        out_specs=pl.BlockSpec((d, pw), lambda p: (0, p)),
        out_shape=_sds((d, n_p * pw), BF16),
        compiler_params=_params(("arbitrary",)),
    )(xn1, dhp5)


def _mix_bwd_in(dhp5, win4, h, dh1, g1, after):
    n_p, tp, pw = dhp5.shape
    d = h.shape[1]
    n_sh, _, csh = win4.shape
    tm = _row_tile(tp)

    seq, _ = _seq_rows(tp)
    last, meta_off = seq // tm, seq % tm
    assert last == tp // tm - 1
    assert any(rs.start <= meta_off and meta_off + N_META <= rs.stop for rs in _row_parts(tm))

    def body(dhp_ref, w_ref, h_ref, dh1_ref, g_ref, after_ref, gx_ref, dmeta_ref, dg1_ref, wcat_ref):
        i = pl.program_id(0)
        _concat_shards(w_ref, wcat_ref)

        @pl.when(i == 0)
        def _():
            dg1_ref[...] = jnp.zeros(dg1_ref.shape, F32)

        for rs in _row_parts(tm):
            dxn = _dot(dhp_ref[0, rs, :], wcat_ref[:, 0:pw], NT)
            for p in range(1, n_p):
                dxn = dxn + _dot(dhp_ref[p, rs, :], wcat_ref[:, p * pw:(p + 1) * pw], NT)
            hh = h_ref[rs, :]
            r1 = lax.rsqrt(_mean(hh * hh) + RMS_EPS)
            dres, dg_rows = _rms_bwd(dxn, hh, r1, g_ref[...])
            dh = dh1_ref[rs, :] + dres
            gx_ref[rs, :] = dh
            dg1_ref[...] += jnp.sum(dg_rows, axis=0, keepdims=True)
            if rs.start <= meta_off and meta_off + N_META <= rs.stop:
                @pl.when(i == last)
                def _():
                    dmeta_ref[...] = dh[meta_off - rs.start:meta_off - rs.start + N_META, :]

    row = lambda i: (i, 0)
    return pl.pallas_call(
        body, name="mix_bwd_in", grid=(tp // tm,),
        in_specs=[pl.BlockSpec((n_p, tm, pw), lambda i: (0, i, 0)), _resident(win4.shape), pl.BlockSpec((tm, d), row),
                  pl.BlockSpec((tm, d), row), _full(g1.shape), ANY],
        out_specs=[pl.BlockSpec((tm, d), row), _full((N_META, d)), _full((1, d))],
        out_shape=[_sds((seq, d), F32), _sds((N_META, d), F32), _sds((1, d), F32)],
        scratch_shapes=[pltpu.VMEM((d, n_sh * csh), BF16)],
        compiler_params=_params(("arbitrary",)),
    )(dhp5, win4, h, dh1, g1, after)


def _other_chips(x, y):
    out = []
    for j in (1, 2, 3):
        px, py = _flip(x, j >> 1), _flip(y, j & 1)
        out.append((px, py, 2 * px + py))
    return out


PAIR_COLLECTIVE_ID = 0


def _pair_barrier(x, y, c):
    sem = pltpu.get_barrier_semaphore()
    pl.semaphore_signal(sem, inc=1, device_id=(x, y, 1 - c), device_id_type=MESH)
    pl.semaphore_wait(sem, 1)


def _pair_params():
    return pltpu.CompilerParams(collective_id=PAIR_COLLECTIVE_ID)


def _half_rows(c, rows_half):
    return pl.ds(pl.multiple_of(c * rows_half, SUBLANES), rows_half)


def _cast_place(ws, q_arr, tag, after=None):
    n = len(ws)
    extra = [] if after is None else [after]

    def fits(steps):
        return all(w.shape[0] % steps == 0 and (w.shape[0] // steps) % BF16_ROWS == 0
                   and w.shape[0] // steps <= ROW_TILE_CAP for w in ws)

    steps = next(s for s in range(1, min(w.shape[0] for w in ws) + 1) if fits(s))

    def body(q_ref, *refs):
        for w_ref, out_ref in zip(refs[:n], refs[n + len(extra):]):
            out_ref[0] = w_ref[...].astype(BF16)

    return list(pl.pallas_call(
        body, name="cast_place_" + tag,
        grid_spec=pltpu.PrefetchScalarGridSpec(
            num_scalar_prefetch=1, grid=(steps,),
            in_specs=[pl.BlockSpec((w.shape[0] // steps, w.shape[1]), lambda i, q: (i, 0)) for w in ws] + [ANY] * len(extra),
            out_specs=[pl.BlockSpec((1, w.shape[0] // steps, w.shape[1]), lambda i, q: (q[0], i, 0)) for w in ws]),
        out_shape=[_sds((N_CHIPS,) + w.shape, BF16) for w in ws],
        compiler_params=_params(("arbitrary",)),
    )(q_arr, *ws, *extra))


HBM = pl.BlockSpec(memory_space=pltpu.HBM)
SEM = pl.BlockSpec(memory_space=pltpu.SEMAPHORE)
EFFECT = pltpu.SideEffectType.DATAFLOW_SIDE_EFFECTING


def _in_hbm(a):
    return pltpu.with_memory_space_constraint(a, pltpu.HBM)


def _gather_start(fulls, after, tag):
    n = len(fulls)
    halves = [a.shape[1] // 2 for a in fulls]

    def body(*refs):
        land = refs[:n]
        ssem, rsem = refs[n + 1], refs[n + 2]
        token = refs[-1]
        x, y, c = _mesh_pos()
        q = 2 * x + y
        for i in range(n):
            for j, (px, py, _) in enumerate(_other_chips(x, y)):
                mine = land[i].at[q, _half_rows(c, halves[i]), :]
                pltpu.make_async_remote_copy(src_ref=mine, dst_ref=mine, send_sem=ssem.at[3 * i + j],
                                             recv_sem=rsem.at[3 * i + j], device_id=(px, py, c), device_id_type=MESH).start()
        token[...] = jnp.zeros(token.shape, F32)

    outs = pl.pallas_call(
        body, name="gather_start_" + tag,
        in_specs=[HBM] * n + [ANY], out_specs=[SEM, SEM] + [HBM] * n + [VMEM],
        out_shape=[pltpu.SemaphoreType.DMA((3 * n,)), pltpu.SemaphoreType.DMA((3 * n,))]
        + [pltpu.HBM(a.shape, a.dtype) for a in fulls] + [_sds((SUBLANES, LANES), F32)],
        input_output_aliases={i: 2 + i for i in range(n)},
        compiler_params=pltpu.CompilerParams(has_side_effects=EFFECT),
    )(*[_in_hbm(a) for a in fulls], after)
    return outs[0], outs[1], list(outs[2:2 + n]), outs[-1]


def _gather_wait(which, ssem, rsem, lands, after, tag):
    m = len(which)
    halves = [a.shape[1] // 2 for a in lands]

    def body(*refs):
        land = refs[:m]
        ssem_, rsem_ = refs[m], refs[m + 1]
        x, y, c = _mesh_pos()
        for t, i in enumerate(which):
            for j, (px, py, qj) in enumerate(_other_chips(x, y)):
                rows = _half_rows(c, halves[t])
                cp = pltpu.make_async_remote_copy(src_ref=land[t].at[2 * x + y, rows, :], dst_ref=land[t].at[qj, rows, :],
                                                  send_sem=ssem_.at[3 * i + j], recv_sem=rsem_.at[3 * i + j],
                                                  device_id=(px, py, c), device_id_type=MESH)
                cp.wait_send()
                cp.wait_recv()

    outs = pl.pallas_call(
        body, name="gather_wait_" + tag,
        in_specs=[HBM] * m + [SEM, SEM, ANY], out_specs=[HBM] * m,
        out_shape=[pltpu.HBM(a.shape, a.dtype) for a in lands],
        input_output_aliases={i: i for i in range(m)},
        compiler_params=pltpu.CompilerParams(has_side_effects=EFFECT),
    )(*lands, ssem, rsem, after)
    return list(outs)


def _forward_pair(lands, tag):
    n = len(lands)
    halves = [a.shape[1] // 2 for a in lands]

    def body(*refs):
        full = refs[n:2 * n]
        ssem, rsem = refs[2 * n:]
        x, y, c = _mesh_pos()
        _pair_barrier(x, y, c)
        cps = []
        for i in range(n):
            for j, (_, _, qj) in enumerate(_other_chips(x, y)):
                part = full[i].at[qj, _half_rows(c, halves[i]), :]
                cp = pltpu.make_async_remote_copy(src_ref=part, dst_ref=part, send_sem=ssem.at[3 * i + j],
                                                  recv_sem=rsem.at[3 * i + j], device_id=(x, y, 1 - c), device_id_type=MESH)
                cp.start()
                cps.append(cp)
        for cp in cps:
            cp.wait()

    return pl.pallas_call(
        body, name="forward_pair_" + tag,
        in_specs=[ANY] * n, out_specs=[ANY] * n,
        out_shape=[_sds(a.shape, a.dtype) for a in lands],
        input_output_aliases={i: i for i in range(n)},
        scratch_shapes=[pltpu.SemaphoreType.DMA((3 * n,)), pltpu.SemaphoreType.DMA((3 * n,))],
        compiler_params=_pair_params(),
    )(*lands)


def _chip_exchange_start(parts, after, tag):
    n = len(parts)

    def body(*refs):
        src, land = refs[:n], refs[n:2 * n]
        ssem, rsem = refs[2 * n + 1], refs[2 * n + 2]
        token = refs[-1]
        x, y, c = _mesh_pos()
        for i in range(n):
            for j, (px, py, qj) in enumerate(_other_chips(x, y)):
                pltpu.make_async_remote_copy(src_ref=src[i].at[qj], dst_ref=land[i].at[j], send_sem=ssem.at[3 * i + j],
                                             recv_sem=rsem.at[3 * i + j], device_id=(px, py, c), device_id_type=MESH).start()
        token[...] = jnp.zeros(token.shape, F32)

    lands = [lax.empty((3,) + a.shape[1:], a.dtype) for a in parts]
    outs = pl.pallas_call(
        body, name="chip_exchange_start_" + tag,
        in_specs=[HBM] * (2 * n) + [ANY], out_specs=[SEM, SEM] + [HBM] * (2 * n) + [VMEM],
        out_shape=[pltpu.SemaphoreType.DMA((3 * n,)), pltpu.SemaphoreType.DMA((3 * n,))]
        + [pltpu.HBM(a.shape, a.dtype) for a in parts] + [pltpu.HBM(a.shape, a.dtype) for a in lands]
        + [_sds((SUBLANES, LANES), F32)],
        input_output_aliases={i: 2 + i for i in range(2 * n)},
        compiler_params=pltpu.CompilerParams(has_side_effects=EFFECT),
    )(*[_in_hbm(a) for a in parts], *[_in_hbm(a) for a in lands], after)
    return outs[0], outs[1], list(outs[2:2 + n]), list(outs[2 + n:2 + 2 * n]), outs[-1]


def _chip_exchange_wait(ssem, rsem, parts, lands, after, tag):
    n = len(parts)

    def body(*refs):
        src, land = refs[:n], refs[n:2 * n]
        ssem_, rsem_ = refs[2 * n], refs[2 * n + 1]
        x, y, c = _mesh_pos()
        for i in range(n):
            for j, (px, py, qj) in enumerate(_other_chips(x, y)):
                cp = pltpu.make_async_remote_copy(src_ref=src[i].at[qj], dst_ref=land[i].at[j], send_sem=ssem_.at[3 * i + j],
                                                  recv_sem=rsem_.at[3 * i + j], device_id=(px, py, c), device_id_type=MESH)
                cp.wait_send()
                cp.wait_recv()

    outs = pl.pallas_call(
        body, name="chip_exchange_wait_" + tag,
        in_specs=[HBM] * (2 * n) + [SEM, SEM, ANY], out_specs=[HBM] * (2 * n),
        out_shape=[pltpu.HBM(a.shape, a.dtype) for a in parts] + [pltpu.HBM(a.shape, a.dtype) for a in lands],
        input_output_aliases={i: i for i in range(2 * n)},
        compiler_params=pltpu.CompilerParams(has_side_effects=EFFECT),
    )(*parts, *lands, ssem, rsem, after)
    return list(outs[:n]), list(outs[n:])


def _grad_half(ref, shape, axis, which):
    rows = shape[axis] // 2
    if axis == 0:
        return ref.at[_half_rows(which, rows), :]
    return ref.at[:, _half_rows(which, rows), :]


def _half_shape(a, axis):
    s = list(a.shape)
    s[axis] //= 2
    return tuple(s)


def _pair_exchange_start(grads, half_axis, after, tag):
    n = len(grads)

    def body(*refs):
        g, land = refs[:n], refs[n:2 * n]
        ssem, rsem = refs[2 * n + 1], refs[2 * n + 2]
        token = refs[-1]
        x, y, c = _mesh_pos()
        for i in range(n):
            pltpu.make_async_remote_copy(src_ref=_grad_half(g[i], grads[i].shape, half_axis[i], 1 - c), dst_ref=land[i],
                                         send_sem=ssem.at[i], recv_sem=rsem.at[i], device_id=(x, y, 1 - c),
                                         device_id_type=MESH).start()
        token[...] = jnp.zeros(token.shape, F32)

    lands = [lax.empty(_half_shape(a, half_axis[i]), a.dtype) for i, a in enumerate(grads)]
    outs = pl.pallas_call(
        body, name="pair_exchange_start_" + tag,
        in_specs=[HBM] * (2 * n) + [ANY], out_specs=[SEM, SEM] + [HBM] * (2 * n) + [VMEM],
        out_shape=[pltpu.SemaphoreType.DMA((n,)), pltpu.SemaphoreType.DMA((n,))]
        + [pltpu.HBM(a.shape, a.dtype) for a in grads] + [pltpu.HBM(a.shape, a.dtype) for a in lands]
        + [_sds((SUBLANES, LANES), F32)],
        input_output_aliases={i: 2 + i for i in range(2 * n)},
        compiler_params=pltpu.CompilerParams(has_side_effects=EFFECT),
    )(*[_in_hbm(a) for a in grads], *[_in_hbm(a) for a in lands], after)
    return outs[0], outs[1], list(outs[2:2 + n]), list(outs[2 + n:2 + 2 * n]), outs[-1]


def _pair_exchange_wait(ssem, rsem, grads, lands, half_axis, after, tag):
    n = len(grads)

    def body(*refs):
        g, land = refs[:n], refs[n:2 * n]
        ssem_, rsem_ = refs[2 * n], refs[2 * n + 1]
        x, y, c = _mesh_pos()
        for i in range(n):
            cp = pltpu.make_async_remote_copy(src_ref=_grad_half(g[i], grads[i].shape, half_axis[i], 1 - c),
                                              dst_ref=land[i], send_sem=ssem_.at[i], recv_sem=rsem_.at[i],
                                              device_id=(x, y, 1 - c), device_id_type=MESH)
            cp.wait_send()
            cp.wait_recv()

    outs = pl.pallas_call(
        body, name="pair_exchange_wait_" + tag,
        in_specs=[HBM] * (2 * n) + [SEM, SEM, ANY], out_specs=[HBM] * (2 * n),
        out_shape=[pltpu.HBM(a.shape, a.dtype) for a in grads] + [pltpu.HBM(a.shape, a.dtype) for a in lands],
        input_output_aliases={i: i for i in range(2 * n)},
        compiler_params=pltpu.CompilerParams(has_side_effects=EFFECT),
    )(*grads, *lands, ssem, rsem, after)
    return list(outs[:n]), list(outs[n:])


def _pair_exchange_grads(grads, half_axis, tag):
    n = len(grads)

    def body(*refs):
        g, got = refs[:n], refs[n:2 * n]
        ssem, rsem = refs[2 * n:]
        x, y, c = _mesh_pos()
        _pair_barrier(x, y, c)
        cps = []
        for i in range(n):
            cp = pltpu.make_async_remote_copy(src_ref=_grad_half(g[i], grads[i].shape, half_axis[i], 1 - c),
                                              dst_ref=got[i], send_sem=ssem.at[i], recv_sem=rsem.at[i],
                                              device_id=(x, y, 1 - c), device_id_type=MESH)
            cp.start()
            cps.append(cp)
        for cp in cps:
            cp.wait()

    return pl.pallas_call(
        body, name="pair_exchange_grads_" + tag,
        in_specs=[ANY] * n, out_specs=[ANY] * n,
        out_shape=[_sds(_half_shape(a, half_axis[i]), a.dtype) for i, a in enumerate(grads)],
        scratch_shapes=[pltpu.SemaphoreType.DMA((n,)), pltpu.SemaphoreType.DMA((n,))],
        compiler_params=_pair_params(),
    )(*grads)


def _pair_sum(gs, gots, c_arr, col_sharded, tag):
    n = len(gs)
    g_specs, got_specs, out_specs, out_shapes = [], [], [], []
    for g, by_cols in zip(gs, col_sharded):
        if by_cols:
            rows, cols = g.shape
            rh, cs = rows // 2, cols // N_CHIPS
            g_specs.append(pl.BlockSpec((rh, cs), lambda k, c_ref: (c_ref[0], k)))
            got_specs.append(pl.BlockSpec((rh, cs), lambda k, c_ref: (0, k)))
        else:
            _, rows, cs = g.shape
            rh = rows // 2
            g_specs.append(pl.BlockSpec((1, rh, cs), lambda k, c_ref: (k, c_ref[0], 0)))
            got_specs.append(pl.BlockSpec((1, rh, cs), lambda k, c_ref: (k, 0, 0)))
        out_specs.append(pl.BlockSpec((1, rh, cs), lambda k, c_ref: (k, 0, 0)))
        out_shapes.append(_sds((N_CHIPS, rh, cs), BF16))

    def body(c_ref, *refs):
        for g_ref, got_ref, out_ref in zip(refs[:n], refs[n:2 * n], refs[2 * n:]):
            total = g_ref[...].astype(F32) + got_ref[...].astype(F32)
            out_ref[...] = total.astype(BF16).reshape(out_ref.shape)

    return list(pl.pallas_call(
        body, name="pair_sum_" + tag,
        grid_spec=pltpu.PrefetchScalarGridSpec(
            num_scalar_prefetch=1, grid=(N_CHIPS,), in_specs=g_specs + got_specs, out_specs=out_specs),
        out_shape=out_shapes,
        compiler_params=_params(("arbitrary",)),
    )(c_arr, *gs, *gots))


def _chip_sum(parts, gots, qc_arr, tag):
    n = len(parts)
    steps = 2 if all(p.shape[1] % 32 == 0 for p in parts) else 1
    part_specs, got_specs, out_specs, out_shapes = [], [], [], []
    for p in parts:
        _, rh, cs = p.shape
        rb = rh // steps
        part_specs.append(pl.BlockSpec((1, rb, cs), lambda i, qc: (qc[0], i, 0)))
        got_specs.append(pl.BlockSpec((3, rb, cs), lambda i, qc: (0, i, 0)))
        out_specs.append(pl.BlockSpec((rb, cs), lambda i, qc: (qc[1] * steps + i, 0)))
        out_shapes.append(_sds((2 * rh, cs), F32))

    def body(qc_ref, *refs):
        for part_ref, got_ref, out_ref in zip(refs[:n], refs[n:2 * n], refs[2 * n:]):
            total = part_ref[0].astype(F32)
            for j in range(3):
                total = total + got_ref[j].astype(F32)
            out_ref[...] = total

    return list(pl.pallas_call(
        body, name="chip_sum_" + tag,
        grid_spec=pltpu.PrefetchScalarGridSpec(
            num_scalar_prefetch=1, grid=(steps,), in_specs=part_specs + got_specs, out_specs=out_specs),
        out_shape=out_shapes,
        compiler_params=_params(("arbitrary",)),
    )(qc_arr, *parts, *gots))


def _pair_share_grads(grads, tag):
    n = len(grads)

    def body(*refs):
        g = refs[n:2 * n]
        ssem, rsem = refs[2 * n:]
        x, y, c = _mesh_pos()
        _pair_barrier(x, y, c)
        cps = []
        for i in range(n):
            mine = g[i].at[_half_rows(c, grads[i].shape[0] // 2), :]
            cp = pltpu.make_async_remote_copy(src_ref=mine, dst_ref=mine, send_sem=ssem.at[i], recv_sem=rsem.at[i],
                                              device_id=(x, y, 1 - c), device_id_type=MESH)
            cp.start()
            cps.append(cp)
        for cp in cps:
            cp.wait()

    return pl.pallas_call(
        body, name="pair_share_grads_" + tag,
        in_specs=[ANY] * n, out_specs=[ANY] * n,
        out_shape=[_sds(a.shape, a.dtype) for a in grads],
        input_output_aliases={i: i for i in range(n)},
        scratch_shapes=[pltpu.SemaphoreType.DMA((n,)), pltpu.SemaphoreType.DMA((n,))],
        compiler_params=_pair_params(),
    )(*grads)


def _pair_share_start(grads, after, tag):
    n = len(grads)

    def body(*refs):
        g = refs[:n]
        ssem, rsem = refs[n + 1], refs[n + 2]
        token = refs[-1]
        x, y, c = _mesh_pos()
        for i in range(n):
            mine = g[i].at[_half_rows(c, grads[i].shape[0] // 2), :]
            pltpu.make_async_remote_copy(src_ref=mine, dst_ref=mine, send_sem=ssem.at[i], recv_sem=rsem.at[i],
                                         device_id=(x, y, 1 - c), device_id_type=MESH).start()
        token[...] = jnp.zeros(token.shape, F32)

    outs = pl.pallas_call(
        body, name="pair_share_start_" + tag,
        in_specs=[HBM] * n + [ANY], out_specs=[SEM, SEM] + [HBM] * n + [VMEM],
        out_shape=[pltpu.SemaphoreType.DMA((n,)), pltpu.SemaphoreType.DMA((n,))]
        + [pltpu.HBM(a.shape, a.dtype) for a in grads] + [_sds((SUBLANES, LANES), F32)],
        input_output_aliases={i: 2 + i for i in range(n)},
        compiler_params=pltpu.CompilerParams(has_side_effects=EFFECT),
    )(*[_in_hbm(a) for a in grads], after)
    return outs[0], outs[1], list(outs[2:2 + n]), outs[-1]


def _pair_share_wait(ssem, rsem, grads, after, tag):
    n = len(grads)

    def body(*refs):
        g = refs[:n]
        ssem_, rsem_ = refs[n], refs[n + 1]
        x, y, c = _mesh_pos()
        for i in range(n):
            rows = grads[i].shape[0] // 2
            cp = pltpu.make_async_remote_copy(src_ref=g[i].at[_half_rows(c, rows), :], dst_ref=g[i].at[_half_rows(1 - c, rows), :],
                                              send_sem=ssem_.at[i], recv_sem=rsem_.at[i], device_id=(x, y, 1 - c),
                                              device_id_type=MESH)
            cp.wait_send()
            cp.wait_recv()

    outs = pl.pallas_call(
        body, name="pair_share_wait_" + tag,
        in_specs=[HBM] * n + [SEM, SEM, ANY], out_specs=[HBM] * n,
        out_shape=[pltpu.HBM(a.shape, a.dtype) for a in grads],
        input_output_aliases={i: i for i in range(n)},
        compiler_params=pltpu.CompilerParams(has_side_effects=EFFECT),
    )(*grads, ssem, rsem, after)
    return list(outs)


def _small_allreduce(parts, places, rows_total, width, after):
    n = len(parts)

    def body(*refs):
        ins, out_ref = refs[:n], refs[n + 1]
        pack, pair_got, chip_sum, got, ssem, rsem = refs[n + 2:]
        x, y, c = _mesh_pos()
        chip = 2 * x + y
        pack[...] = jnp.zeros(pack.shape, F32)
        for i in range(n):
            for row, col, src_row, rows in places[i]:
                w = parts[i].shape[1]
                pack[row:row + rows, col:col + w] = ins[i][src_row:src_row + rows, :]
        swap = pltpu.make_async_remote_copy(src_ref=pack, dst_ref=pair_got, send_sem=ssem.at[3], recv_sem=rsem.at[3],
                                            device_id=(x, y, 1 - c), device_id_type=MESH)
        swap.start()
        swap.wait()
        chip_sum[...] = pack[...] + pair_got[...]
        cps = []
        for j, (px, py, _) in enumerate(_other_chips(x, y)):
            cp = pltpu.make_async_remote_copy(src_ref=chip_sum, dst_ref=got.at[j], send_sem=ssem.at[j],
                                              recv_sem=rsem.at[j], device_id=(px, py, c), device_id_type=MESH)
            cp.start()
            cps.append(cp)
        for cp in cps:
            cp.wait()
        total = jnp.zeros(pack.shape, F32)
        for q in range(N_CHIPS):
            rel = jnp.bitwise_xor(chip, q)
            theirs = got[jnp.maximum(rel - 1, 0)]
            total = total + jnp.where(rel == 0, chip_sum[...], theirs)
        out_ref[...] = total

    return pl.pallas_call(
        body, name="small_allreduce",
        in_specs=[VMEM] * n + [ANY], out_specs=VMEM,
        out_shape=_sds((rows_total, width), F32),
        scratch_shapes=[pltpu.VMEM((rows_total, width), F32), pltpu.VMEM((rows_total, width), F32),
                        pltpu.VMEM((rows_total, width), F32), pltpu.VMEM((3, rows_total, width), F32),
                        pltpu.SemaphoreType.DMA((4,)), pltpu.SemaphoreType.DMA((4,))],
        compiler_params=_params(),
    )(*parts, after)


def _small_update(red, q_arr, takes, loss_at, ws, ms, vs):
    n_w = len(ws)

    def body(q_ref, red_ref, *refs):
        w_in, m_in, v_in = refs[0:n_w], refs[n_w:2 * n_w], refs[2 * n_w:3 * n_w]
        outs = refs[3 * n_w:]
        g_out, d_out, m_out, v_out = (outs[0:n_w], outs[n_w:2 * n_w], outs[2 * n_w:3 * n_w], outs[3 * n_w:4 * n_w])
        loss_ref = outs[4 * n_w]
        chip = q_ref[0]

        def put(g_ref, d0, nr, s0, lo, w):
            if len(g_ref.shape) == 3:
                for r in range(nr):
                    g_ref[d0 + r] = red_ref[s0 + r:s0 + r + 1, lo:lo + w]
            else:
                g_ref[d0:d0 + nr, :] = red_ref[s0:s0 + nr, lo:lo + w]

        def take_own_columns(g_ref, d0, nr, s0, c0, w):
            for k in range(N_CHIPS):
                @pl.when(chip == k)
                def _():
                    put(g_ref, d0, nr, s0, c0 + k * w, w)

        for j in range(n_w):
            w = ws[j].shape[-1]
            for d0, nr, s0, c0, sharded in takes[j]:
                if sharded:
                    take_own_columns(g_out[j], d0, nr, s0, c0, w)
                else:
                    put(g_out[j], d0, nr, s0, c0, w)
            d_out[j][...], m_out[j][...], v_out[j][...] = _adamw_math(w_in[j][...], g_out[j][...], m_in[j][...], v_in[j][...])
        loss_ref[...] = red_ref[loss_at[0]:loss_at[0] + 1, loss_at[1]:loss_at[1] + LANES]

    shapes = [_sds(w.shape, F32) for w in ws]
    outs = pl.pallas_call(
        body, name="small_update",
        in_specs=[pl.BlockSpec(memory_space=pltpu.SMEM)] + [VMEM] * (1 + 3 * n_w), out_specs=[VMEM] * (4 * n_w + 1),
        out_shape=shapes * 4 + [_sds((1, LANES), F32)],
        compiler_params=_params(),
    )(q_arr, red, *ws, *ms, *vs)
    return outs[0:n_w], outs[n_w:2 * n_w], outs[2 * n_w:3 * n_w], outs[3 * n_w:4 * n_w], outs[4 * n_w]


def _adamw_math(w, g, m, v):
    m2 = ADAM_B1 * m + (1.0 - ADAM_B1) * g
    v2 = ADAM_B2 * v + (1.0 - ADAM_B2) * (g * g)
    m_hat = m2 / (1.0 - ADAM_B1 ** ADAM_STEP)
    v_hat = v2 / (1.0 - ADAM_B2 ** ADAM_STEP)
    delta = -ADAM_LR * (m_hat / (jnp.sqrt(v_hat) + ADAM_EPS) + ADAM_WD * w)
    return delta, m2, v2


ADAMW_BLOCK_BYTES = 3 * 2 ** 19


def _adamw_big(ws, gs, ms, vs, tag):
    n = len(ws)

    def fits(steps):
        return all(w.shape[0] % steps == 0 and (w.shape[0] // steps) % SUBLANES == 0
                   and (w.shape[0] // steps) * w.shape[1] * 4 * n <= ADAMW_BLOCK_BYTES for w in ws)

    steps = next(s for s in range(1, min(w.shape[0] for w in ws) + 1) if fits(s))
    specs = [pl.BlockSpec((w.shape[0] // steps, w.shape[1]), lambda i: (i, 0)) for w in ws]

    def body(*refs):
        ins, outs = refs[:4 * n], refs[4 * n:]
        for i in range(n):
            w_ref, g_ref, m_ref, v_ref = ins[i], ins[n + i], ins[2 * n + i], ins[3 * n + i]
            gg = g_ref[...]
            outs[4 * i][...] = gg
            outs[4 * i + 1][...], outs[4 * i + 2][...], outs[4 * i + 3][...] = _adamw_math(
                w_ref[...], gg, m_ref[...], v_ref[...])

    outs = pl.pallas_call(
        body, name="adamw_" + tag, grid=(steps,),
        in_specs=specs * 4, out_specs=[s for s in specs for _ in range(4)],
        out_shape=[_sds(w.shape, F32) for w in ws for _ in range(4)],
        compiler_params=_params(("arbitrary",)),
    )(*ws, *gs, *ms, *vs)
    return [outs[4 * i:4 * i + 4] for i in range(n)]


SMALL_ROWS = 40
PACK_ROWS = 64


def kernel(x, meta_tokens, pre_mix_norm, w_in, conv_a_w, conv_b_w, conv_b_bias, ln_b_gain, ln_b_bias, w_out, post_mix_norm, pre_ffn_norm, w_gate, w_up, w_down, post_ffn_norm, loss_target, m_meta_tokens, m_pre_mix_norm, m_w_in, m_conv_a_w, m_conv_b_w, m_conv_b_bias, m_ln_b_gain, m_ln_b_bias, m_w_out, m_post_mix_norm, m_pre_ffn_norm, m_w_gate, m_w_up, m_w_down, m_post_ffn_norm, v_meta_tokens, v_pre_mix_norm, v_w_in, v_conv_a_w, v_conv_b_w, v_conv_b_bias, v_ln_b_gain, v_ln_b_bias, v_w_out, v_post_mix_norm, v_pre_ffn_norm, v_w_gate, v_w_up, v_w_down, v_post_ffn_norm):
    xq, yq, cq = lax.axis_index("x"), lax.axis_index("y"), lax.axis_index("c")
    chip = 2 * xq + yq
    c_arr = jnp.reshape(cq, (1,)).astype(jnp.int32)
    qc_arr = jnp.stack([chip, cq]).astype(jnp.int32)

    seq, d = x.shape[1], x.shape[2]
    x2, tgt2 = x[0], loss_target[0]
    tr = lambda a: jnp.swapaxes(a, 1, 2)[0]
    w_in2, w_out2, w_gate2, w_up2, w_down2 = w_in[0], w_out[0], tr(w_gate), tr(w_up), w_down[0]
    ka, wa_sh = conv_a_w.shape[1], conv_a_w.shape[2]
    kb = conv_b_w.shape[1]
    meta_sh = meta_tokens.shape[1]

    small = jnp.zeros((PACK_ROWS, meta_sh), F32)
    small = small.at[0:N_META, :].set(meta_tokens)
    small = small.at[16:16 + ka, 0:wa_sh].set(conv_a_w[0])
    small = small.at[24:24 + kb, 0:wa_sh].set(conv_b_w[0])
    q_arr = jnp.reshape(chip, (1,)).astype(jnp.int32)
    small_own = lax.dynamic_update_slice(jnp.zeros((N_CHIPS, PACK_ROWS, meta_sh), F32), small[None], (chip, 0, 0))
    i_ssem, i_rsem, first, i_token = _gather_start(_cast_place([w_in2], q_arr, "w_in") + [small_own], pre_mix_norm, "in")
    rest = _cast_place([w_out2, w_gate2, w_up2, w_down2], q_arr, "rest", i_token)
    g_ssem, g_rsem, lands, g_token = _gather_start(rest, i_token, "rest")
    win4, small4 = _forward_pair(_gather_wait([0, 1], i_ssem, i_rsem, first, g_token, "in"), "in")
    meta_f = jnp.concatenate([small4[k, 0:N_META, :] for k in range(N_CHIPS)], axis=1)
    wa_f = jnp.concatenate([small4[k, 16:16 + ka, 0:wa_sh] for k in range(N_CHIPS)], axis=1)
    wb_f = jnp.concatenate([small4[k, 24:24 + kb, 0:wa_sh] for k in range(N_CHIPS)], axis=1)

    tm = _row_tile(seq + TAIL_ROWS)
    tail = lax.dynamic_update_slice(jnp.zeros((tm, d), F32), meta_f, (seq % tm, 0))
    h, xn1, hp5 = _mm_in(x2, tail, win4, pre_mix_norm, g_token)
    ya, z = _mix_conv_fwd(hp5, wa_f, wb_f, conv_b_bias)
    (wout4,) = _forward_pair(_gather_wait([0], g_ssem, g_rsem, lands[0:1], z, "out"), "out")
    wout_f = wout4.reshape(N_CHIPS * wout4.shape[1], wout4.shape[2])
    yb, mix, h1, xn2 = _mm_out(ya, z, h, wout_f, ln_b_gain, ln_b_bias, post_mix_norm, pre_ffn_norm)
    wg4, wu4 = _forward_pair(_gather_wait([1, 2], g_ssem, g_rsem, lands[1:3], xn2, "gate_up"), "gate_up")
    stacked = lambda a: a.reshape(a.shape[0] * a.shape[1], a.shape[2])
    wg_f, wu_f = stacked(wg4), stacked(wu4)
    p_act, q_act, f_act = _ffn_up(xn2, wg_f, wu_f)
    (wd4,) = _forward_pair(_gather_wait([3], g_ssem, g_rsem, lands[3:4], f_act, "down"), "down")
    wd_f = stacked(wd4)
    dff, dh2, loss_blk, d_gpf = _ffn_down(f_act, wd_f, h1, tgt2, post_ffn_norm)

    da, du = _ffn_bwd_act(dff, wd_f, p_act, q_act)
    by_chip = lambda g: g.reshape(N_CHIPS, g.shape[0] // N_CHIPS, g.shape[1])
    g_down = by_chip(_grad_w_down(f_act, dff))
    g_gate, g_up = [by_chip(g) for g in _grad_w_gate_up(xn2, da, du)]
    ffn = [g_gate, g_up, g_down]
    p_ssem, p_rsem, ffn, p_lands, p_token = _pair_exchange_start(ffn, [1, 1, 1], dff, "ffn")
    dh1, dmix, d_g2, d_gpm = _ffn_bwd_in(da, du, wg_f, wu_f, h1, mix, dh2, pre_ffn_norm, post_mix_norm, p_token)
    ffn, got = _pair_exchange_wait(p_ssem, p_rsem, ffn, p_lands, [1, 1, 1], d_g2, "ffn")
    parts = _pair_sum(ffn, got, c_arr, [False] * 3, "ffn")
    f_ssem, f_rsem, parts, f_lands, f_token = _chip_exchange_start(parts, dff, "ffn")
    g_out = _grad_w_out(ya, yb, dmix, f_token)
    dya, dz, d_lg, d_lb = _mix_bwd_out(dmix, wout_f, z, ln_b_gain, ln_b_bias, f_token)
    dhp5, d_wa, d_wb, d_bb = _mix_conv_bwd(hp5, dya, dz, wa_f, wb_f)
    g_in = _grad_w_in(xn1, dhp5)

    g_out4 = g_out.reshape(N_CHIPS, g_out.shape[0] // N_CHIPS, g_out.shape[1])
    mixw = [g_in, g_out4]
    got2 = _pair_exchange_grads(mixw, [0, 1], "mix")
    parts2 = _pair_sum(mixw, got2, c_arr, [True, False], "mix")
    m_ssem, m_rsem, parts2, m_lands, m_token = _chip_exchange_start(parts2, dhp5, "mix")
    grad_x2, d_meta, d_g1 = _mix_bwd_in(dhp5, win4, h, dh1, pre_mix_norm, m_token)
    grad_x = grad_x2[None]

    parts, f_recv = _chip_exchange_wait(f_ssem, f_rsem, parts, f_lands, d_g1, "ffn")
    halves = _chip_sum(parts, f_recv, qc_arr, "ffn")
    s_ssem, s_rsem, halves, s_token = _pair_share_start(halves, d_g1, "ffn")

    hw = d // 2
    assert d_wa.shape == (3, hw) and d_wb.shape == (31, hw) and d_bb.shape == (1, hw)
    small_parts = [d_meta, d_g1, d_gpm, d_g2, d_gpf, d_bb, d_lg, d_lb, loss_blk[0:1, :], d_wa, d_wb]
    places = [[(0, 0, 0, N_META)], [(16, 0, 0, 1)], [(17, 0, 0, 1)], [(18, 0, 0, 1)], [(19, 0, 0, 1)],
              [(20, 0, 0, 1)], [(20, hw, 0, 1)], [(21, 0, 0, 1)], [(21, hw, 0, 1)], [(22, 0, 0, 3)],
              [(22, hw, 0, 3), (25, 0, 3, 14), (25, hw, 17, 14)]]
    red = _small_allreduce(small_parts, places, SMALL_ROWS, d, s_token)
    gsum_ffn = _pair_share_wait(s_ssem, s_rsem, halves, red, "ffn")

    names_big = ["w_in", "w_out", "w_gate", "w_up", "w_down"]
    w_big = dict(zip(names_big, [w_in2, w_out2, w_gate2, w_up2, w_down2]))
    m_big = dict(zip(names_big, [m_w_in[0], m_w_out[0], tr(m_w_gate), tr(m_w_up), m_w_down[0]]))
    v_big = dict(zip(names_big, [v_w_in[0], v_w_out[0], tr(v_w_gate), tr(v_w_up), v_w_down[0]]))
    grads, deltas, new_m, new_v = {}, {}, {}, {}

    def update(names, gs, tag):
        res = _adamw_big([w_big[k] for k in names], gs, [m_big[k] for k in names], [v_big[k] for k in names], tag)
        for nm, outs in zip(names, res):
            if nm in ("w_gate", "w_up"):
                outs = [jnp.swapaxes(o[None], 1, 2) for o in outs]
            else:
                outs = [o[None] for o in outs]
            grads[nm], deltas[nm], new_m[nm], new_v[nm] = outs
        return res[-1][1]

    last = update(["w_gate", "w_up", "w_down"], list(gsum_ffn), "ffn")

    names_small = ["meta_tokens", "pre_mix_norm", "conv_a_w", "conv_b_w", "conv_b_bias", "ln_b_gain", "ln_b_bias",
                   "post_mix_norm", "pre_ffn_norm", "post_ffn_norm"]
    takes = [[(0, N_META, 0, 0, True)], [(0, 1, 16, 0, False)], [(0, 3, 22, 0, True)],
             [(0, 3, 22, hw, True), (3, 14, 25, 0, True), (17, 14, 25, hw, True)], [(0, 1, 20, 0, False)],
             [(0, 1, 20, hw, False)], [(0, 1, 21, 0, False)], [(0, 1, 17, 0, False)], [(0, 1, 18, 0, False)],
             [(0, 1, 19, 0, False)]]
    taps = lambda a: jnp.swapaxes(a, 0, 1)
    w_small = [meta_tokens, pre_mix_norm, taps(conv_a_w), taps(conv_b_w), conv_b_bias, ln_b_gain, ln_b_bias, post_mix_norm,
               pre_ffn_norm, post_ffn_norm]
    m_small = [m_meta_tokens, m_pre_mix_norm, taps(m_conv_a_w), taps(m_conv_b_w), m_conv_b_bias, m_ln_b_gain, m_ln_b_bias,
               m_post_mix_norm, m_pre_ffn_norm, m_post_ffn_norm]
    v_small = [v_meta_tokens, v_pre_mix_norm, taps(v_conv_a_w), taps(v_conv_b_w), v_conv_b_bias, v_ln_b_gain, v_ln_b_bias,
               v_post_mix_norm, v_pre_ffn_norm, v_post_ffn_norm]
    g_s, d_s, m_s, v_s, loss_row = _small_update(red, q_arr, takes, (21, hw), w_small, m_small, v_small)
    loss = loss_row[0, 0]
    for i, nm in enumerate(names_small):
        fix = taps if nm in ("conv_a_w", "conv_b_w") else (lambda a: a)
        grads[nm], deltas[nm], new_m[nm], new_v[nm] = fix(g_s[i]), fix(d_s[i]), fix(m_s[i]), fix(v_s[i])

    parts2, m_recv = _chip_exchange_wait(m_ssem, m_rsem, parts2, m_lands, last, "mix")
    halves2 = _chip_sum(parts2, m_recv, qc_arr, "mix")
    gsum_mix = _pair_share_grads(halves2, "mix")
    update(["w_in", "w_out"], list(gsum_mix), "mix")

    order = ["meta_tokens", "pre_mix_norm", "w_in", "conv_a_w", "conv_b_w", "conv_b_bias", "ln_b_gain", "ln_b_bias", "w_out",
             "post_mix_norm", "pre_ffn_norm", "w_gate", "w_up", "w_down", "post_ffn_norm"]
    return (loss, grad_x, *[grads[k] for k in order], *[deltas[k] for k in order], *[new_m[k] for k in order],
            *[new_v[k] for k in order])
```

```python
import jax
import jax.numpy as jnp
from jax import lax
from jax.experimental import pallas as pl
from jax.experimental.pallas import tpu as pltpu

F32 = jnp.float32
BF16 = jnp.bfloat16
MESH = pl.DeviceIdType.MESH

N_META = 16
TAIL_ROWS = 128
RMS_EPS = 1e-6
LN_EPS = 1e-5
ADAM_LR = 0.001
ADAM_B1 = 0.9
ADAM_B2 = 0.999
ADAM_EPS = 1e-08
ADAM_WD = 0.01
ADAM_STEP = 10

N_CHIPS = 4
LANES = 128
SUBLANES = 8
BF16_ROWS = 16
MXU_TILE = 256
CONV_CHUNK = 48
CONV_HIST = 32
ROW_TILE_CAP = 640
VMEM_LIMIT = 56 * 1024 * 1024

NN = (((1,), (0,)), ((), ()))
NT = (((1,), (1,)), ((), ()))
TN = (((0,), (0,)), ((), ()))


def _dot(a, b, dims=NN):
    return lax.dot_general(a, b, dims, preferred_element_type=F32)


def _sig(v):
    return 1.0 / (1.0 + jnp.exp(-v))


def _mean(v):
    return jnp.mean(v, axis=-1, keepdims=True)


def _row_tile(rows):
    best = BF16_ROWS
    for t in range(BF16_ROWS, min(rows, ROW_TILE_CAP) + 1, BF16_ROWS):
        if rows % t == 0:
            best = t
    assert rows % best == 0
    return best


def _row_parts(tm, parts=2):
    units = tm // BF16_ROWS
    if tm % BF16_ROWS or units < parts:
        return [slice(0, tm)]
    cuts = [BF16_ROWS * ((units * k + parts - 1) // parts) for k in range(parts + 1)]
    return [slice(lo, hi) for lo, hi in zip(cuts[:-1], cuts[1:])]


def _concat_shards(w_ref, wcat_ref):
    n_sh, _, csh = w_ref.shape

    @pl.when(pl.program_id(0) == 0)
    def _():
        for k in range(n_sh):
            wcat_ref[:, k * csh:(k + 1) * csh] = w_ref[k]


def _params(semantics=None):
    kw = dict(vmem_limit_bytes=VMEM_LIMIT)
    if semantics is not None:
        kw["dimension_semantics"] = semantics
    return pltpu.CompilerParams(**kw)


def _full(shape):
    nd = len(shape)
    return pl.BlockSpec(shape, lambda *_: (0,) * nd)


def _resident(shape):
    nd = len(shape)
    return pl.BlockSpec(shape, lambda *_: (0,) * nd, pipeline_mode=pl.Buffered(1))


def _sds(shape, dtype):
    return jax.ShapeDtypeStruct(shape, dtype)


ANY = pl.BlockSpec(memory_space=pl.ANY)
VMEM = pl.BlockSpec(memory_space=pltpu.VMEM)


def _mesh_pos():
    return lax.axis_index("x"), lax.axis_index("y"), lax.axis_index("c")


def _flip(v, bit):
    return 1 - v if bit else v


def _mm_in(x, small4, win4, g1, after):
    seq, d = x.shape
    tp = seq + TAIL_ROWS
    tm = _row_tile(tp)
    n_sh, _, csh = win4.shape
    pw = n_sh * csh // 5
    meta_off, meta_w = seq % tm, small4.shape[2]
    assert meta_off + N_META <= tm and seq // tm == tp // tm - 1 and N_CHIPS * meta_w == d

    def body(x_ref, small_ref, w_ref, g_ref, after_ref, h_ref, xn_ref, hp_ref, wcat_ref, tail_ref):
        _concat_shards(w_ref, wcat_ref)

        @pl.when(pl.program_id(0) == 0)
        def _():
            tail_ref[...] = jnp.zeros(tail_ref.shape, F32)
            for k in range(N_CHIPS):
                tail_ref[meta_off:meta_off + N_META, k * meta_w:(k + 1) * meta_w] = small_ref[k, 0:N_META, :]

        rows = pl.program_id(0) * tm + lax.broadcasted_iota(jnp.int32, (tm, 1), 0)
        hh = jnp.where(rows < seq, x_ref[...], tail_ref[...])
        h_ref[...] = hh
        r = lax.rsqrt(_mean(hh * hh) + RMS_EPS)
        xn = (hh * r * g_ref[...]).astype(BF16)
        xn_ref[...] = xn
        for p in range(5):
            hp_ref[p] = _dot(xn, wcat_ref[:, p * pw:(p + 1) * pw])

    row = pl.BlockSpec((tm, d), lambda i: (i, 0))
    return pl.pallas_call(
        body, name="mm_in", grid=(tp // tm,),
        in_specs=[row, _full(small4.shape), _resident(win4.shape), _full(g1.shape), ANY],
        out_specs=[row, row, pl.BlockSpec((5, tm, pw), lambda i: (0, i, 0))],
        out_shape=[_sds((tp, d), F32), _sds((tp, d), BF16), _sds((5, tp, pw), F32)],
        scratch_shapes=[pltpu.VMEM((d, n_sh * csh), BF16), pltpu.VMEM((tm, d), F32)],
        compiler_params=_params(("arbitrary",)),
    )(x, small4, win4, g1, after)


def _seq_rows(tp):
    seq = tp - TAIL_ROWS
    nseq = seq + N_META
    assert nseq % CONV_CHUNK == 0 and seq % BF16_ROWS == 0
    return seq, nseq


def _conv_offsets(width, transpose):
    return [(width - 1 - k) if transpose else (CONV_HIST - (width - 1) + k) for k in range(width)]


def _shift_copies(src_ref, sh_ref, width, transpose):
    n = src_ref.shape[0] - SUBLANES
    for s in sorted({o % SUBLANES for o in _conv_offsets(width, transpose)} - {0}):
        sh_ref[s - 1, 0:n, :] = src_ref[s:s + n, :]


def _tap_rows(src_ref, sh_ref, base, off):
    start = pl.multiple_of(base + (off // SUBLANES) * SUBLANES, SUBLANES)
    if off % SUBLANES == 0:
        return src_ref[pl.ds(start, CONV_CHUNK), :]
    return sh_ref[off % SUBLANES - 1, pl.ds(start, CONV_CHUNK), :]


def _conv_taps(src_ref, sh_ref, w_ref, dst_ref, width, nseq, transpose, shifted=False):
    w = w_ref[...]
    offs = _conv_offsets(width, transpose)
    if not shifted:
        _shift_copies(src_ref, sh_ref, width, transpose)

    def step(n, carry):
        out0 = pl.multiple_of(CONV_HIST + n * CONV_CHUNK, SUBLANES)
        base = out0 if transpose else n * CONV_CHUNK
        acc = jnp.zeros((CONV_CHUNK, w.shape[1]), F32)
        for k, off in enumerate(offs):
            acc = acc + w[k:k + 1, :] * _tap_rows(src_ref, sh_ref, base, off)
        dst_ref[pl.ds(out0, CONV_CHUNK), :] = acc
        return carry

    lax.fori_loop(0, nseq // CONV_CHUNK, step, 0)


def _conv_wgrad(src_ref, sh_ref, dz_ref, acc_ref, width, nseq):
    acc_ref[...] = jnp.zeros(acc_ref.shape, F32)
    offs = _conv_offsets(width, False)

    def step(n, carry):
        dzc = dz_ref[pl.ds(pl.multiple_of(CONV_HIST + n * CONV_CHUNK, SUBLANES), CONV_CHUNK), :]
        for k, off in enumerate(offs):
            prod = dzc * _tap_rows(src_ref, sh_ref, n * CONV_CHUNK, off)
            part = prod[0:SUBLANES, :]
            for s in range(1, CONV_CHUNK // SUBLANES):
                part = part + prod[SUBLANES * s:SUBLANES * (s + 1), :]
            acc_ref[SUBLANES * k:SUBLANES * (k + 1), :] += part
        return carry

    lax.fori_loop(0, nseq // CONV_CHUNK, step, 0)


def _conv_wgrad_by_dz_shifts(src_ref, dz_ref, shd_ref, acc_ref, width, nseq):
    acc_ref[...] = jnp.zeros(acc_ref.shape, F32)
    offs = _conv_offsets(width, True)

    def step(n, carry):
        base = pl.multiple_of(CONV_HIST + n * CONV_CHUNK, SUBLANES)
        rows = src_ref[pl.ds(base, CONV_CHUNK), :]
        for k, off in enumerate(offs):
            prod = rows * _tap_rows(dz_ref, shd_ref, base, off)
            part = prod[0:SUBLANES, :]
            for s in range(1, CONV_CHUNK // SUBLANES):
                part = part + prod[SUBLANES * s:SUBLANES * (s + 1), :]
            acc_ref[SUBLANES * k:SUBLANES * (k + 1), :] += part
        return carry

    lax.fori_loop(0, nseq // CONV_CHUNK, step, 0)


def _to_seq(buf_ref, x_part, meta_part, seq):
    buf_ref[CONV_HIST:CONV_HIST + N_META, :] = meta_part
    buf_ref[CONV_HIST + N_META:CONV_HIST + N_META + seq, :] = x_part


def _zero_ends(buf_ref, nseq):
    zeros = jnp.zeros((CONV_HIST, buf_ref.shape[1]), F32)
    buf_ref[0:CONV_HIST, :] = zeros
    buf_ref[CONV_HIST + nseq:CONV_HIST + nseq + CONV_HIST, :] = zeros


def _mix_conv_fwd(hp5, wa, wb, bb):
    _, tp, wgrp = hp5.shape
    seq, nseq = _seq_rows(tp)
    sb = nseq + 2 * CONV_HIST
    ka, kb = wa.shape[0], wb.shape[0]
    xs, ms = slice(0, seq), slice(seq, seq + N_META)
    ox, om = slice(CONV_HIST + N_META, CONV_HIST + nseq), slice(CONV_HIST, CONV_HIST + N_META)

    def body(hp_ref, wa_ref, wb_ref, bb_ref, ya_ref, z_ref, s_ref, o_ref, sh_ref):
        _zero_ends(s_ref, nseq)
        _to_seq(s_ref, hp_ref[1, xs, :] * hp_ref[2, xs, :], hp_ref[1, ms, :] * hp_ref[2, ms, :], seq)
        _conv_taps(s_ref, sh_ref, wa_ref, o_ref, ka, nseq, False)
        ya_ref[xs, :] = (hp_ref[0, xs, :] * o_ref[ox, :]).astype(BF16)
        ya_ref[ms, :] = (hp_ref[0, ms, :] * o_ref[om, :]).astype(BF16)
        ya_ref[seq + N_META:tp, :] = jnp.zeros((tp - seq - N_META, LANES), BF16)
        _to_seq(s_ref, hp_ref[3, xs, :] * _sig(hp_ref[4, xs, :]), hp_ref[3, ms, :] * _sig(hp_ref[4, ms, :]), seq)
        _conv_taps(s_ref, sh_ref, wb_ref, o_ref, kb, nseq, False)
        z_ref[xs, :] = o_ref[ox, :] + bb_ref[...]
        z_ref[ms, :] = o_ref[om, :] + bb_ref[...]
        z_ref[seq + N_META:tp, :] = jnp.zeros((tp - seq - N_META, LANES), F32)

    col = lambda j: (0, j)
    return pl.pallas_call(
        body, name="mix_conv_fwd", grid=(wgrp // LANES,),
        in_specs=[pl.BlockSpec((5, tp, LANES), lambda j: (0, 0, j)), pl.BlockSpec((ka, LANES), col),
                  pl.BlockSpec((kb, LANES), col), pl.BlockSpec((1, LANES), col)],
        out_specs=[pl.BlockSpec((tp, LANES), col), pl.BlockSpec((tp, LANES), col)],
        out_shape=[_sds((tp, wgrp), BF16), _sds((tp, wgrp), F32)],
        scratch_shapes=[pltpu.VMEM((sb, LANES), F32), pltpu.VMEM((sb, LANES), F32),
                        pltpu.VMEM((SUBLANES - 1, sb, LANES), F32)],
        compiler_params=_params(("arbitrary",)),
    )(hp5, wa, wb, bb)


def _layer_norm_parts(z, lg, lb):
    mu = _mean(z)
    zc = z - mu
    rl = lax.rsqrt(_mean(zc * zc) + LN_EPS)
    zh = zc * rl
    return rl, zh, zh * lg + lb


def _mm_out(ya, z, h, wout, lg, lb, gpm, g2):
    tp, d = h.shape
    wa_ = ya.shape[1]
    tm = _row_tile(tp)

    def body(ya_ref, z_ref, h_ref, w_ref, lg_ref, lb_ref, gpm_ref, g2_ref, yb_ref, mix_ref, h1_ref, xn2_ref):
        for rs in _row_parts(tm, 3):
            _, _, l = _layer_norm_parts(z_ref[rs, :], lg_ref[...], lb_ref[...])
            yb = (l * _sig(l)).astype(BF16)
            yb_ref[rs, :] = yb
            mix = _dot(ya_ref[rs, :], w_ref[0:wa_, :]) + _dot(yb, w_ref[wa_:d, :])
            mix_ref[rs, :] = mix
            rm = lax.rsqrt(_mean(mix * mix) + RMS_EPS)
            h1 = h_ref[rs, :] + mix * rm * gpm_ref[...]
            h1_ref[rs, :] = h1
            r2 = lax.rsqrt(_mean(h1 * h1) + RMS_EPS)
            xn2_ref[rs, :] = (h1 * r2 * g2_ref[...]).astype(BF16)

    row = lambda i: (i, 0)
    return pl.pallas_call(
        body, name="mm_out", grid=(tp // tm,),
        in_specs=[pl.BlockSpec((tm, wa_), row), pl.BlockSpec((tm, wa_), row), pl.BlockSpec((tm, d), row),
                  _resident(wout.shape), _full(lg.shape), _full(lb.shape), _full(gpm.shape), _full(g2.shape)],
        out_specs=[pl.BlockSpec((tm, wa_), row), pl.BlockSpec((tm, d), row), pl.BlockSpec((tm, d), row),
                   pl.BlockSpec((tm, d), row)],
        out_shape=[_sds((tp, wa_), BF16), _sds((tp, d), F32), _sds((tp, d), F32), _sds((tp, d), BF16)],
        compiler_params=_params(("arbitrary",)),
    )(ya, z, h, wout, lg, lb, gpm, g2)


def _ffn_up(xn2, wg, wu):
    tp, d = xn2.shape
    ff_dim = wg.shape[0]
    tm = _row_tile(tp)
    assert ff_dim % MXU_TILE == 0

    def body(xn_ref, wg_ref, wu_ref, p_ref, q_ref, f_ref):
        xn = xn_ref[...]
        for lo in range(0, ff_dim, MXU_TILE):
            cols = slice(lo, lo + MXU_TILE)
            a = _dot(xn, wg_ref[cols, :], NT)
            u = _dot(xn, wu_ref[cols, :], NT)
            s = _sig(a)
            q = a * s
            p_ref[:, cols] = (u * (s + q * (1.0 - s))).astype(BF16)
            q_ref[:, cols] = q.astype(BF16)
            f_ref[:, cols] = (q * u).astype(BF16)

    ospec = pl.BlockSpec((tm, ff_dim), lambda i: (i, 0))
    return pl.pallas_call(
        body, name="ffn_up", grid=(tp // tm,),
        in_specs=[pl.BlockSpec((tm, d), lambda i: (i, 0)), _resident(wg.shape), _resident(wu.shape)],
        out_specs=[ospec, ospec, ospec],
        out_shape=[_sds((tp, ff_dim), BF16)] * 3,
        compiler_params=_params(("arbitrary",)),
    )(xn2, wg, wu)


def _ffn_down(f, wd, h1, tgt, gpf):
    tp, ff_dim = f.shape
    d = h1.shape[1]
    tm = _row_tile(tp)
    seq, _ = _seq_rows(tp)

    def body(f_ref, w_ref, h1_ref, t_ref, gpf_ref, dff_ref, dh2_ref, loss_ref, dgpf_ref):
        i = pl.program_id(0)
        gpf_ = gpf_ref[...]

        @pl.when(i == 0)
        def _():
            loss_ref[...] = jnp.zeros(loss_ref.shape, F32)
            dgpf_ref[...] = jnp.zeros(dgpf_ref.shape, F32)

        for rs in _row_parts(tm):
            ff = _dot(f_ref[rs, :], w_ref[...])
            rf = lax.rsqrt(_mean(ff * ff) + RMS_EPS)
            nf = ff * rf
            h2 = h1_ref[rs, :] + nf * gpf_
            rows = i * tm + rs.start + lax.broadcasted_iota(jnp.int32, (rs.stop - rs.start, 1), 0)
            err = jnp.where(rows < seq, h2 - t_ref[rs, :], 0.0)
            dh2 = err * (1.0 / d)
            dh2_ref[rs, :] = dh2
            dn = dh2 * gpf_
            dff_ref[rs, :] = (rf * (dn - nf * _mean(dn * nf))).astype(BF16)
            loss_ref[...] += (0.5 / d) * jnp.sum(err * err, axis=(0, 1), keepdims=True)
            dgpf_ref[...] += jnp.sum(dh2 * nf, axis=0, keepdims=True)

    row = lambda i: (i, 0)
    return pl.pallas_call(
        body, name="ffn_down", grid=(tp // tm,),
        in_specs=[pl.BlockSpec((tm, ff_dim), row), _resident(wd.shape), pl.BlockSpec((tm, d), row),
                  pl.BlockSpec((tm, d), row), _full(gpf.shape)],
        out_specs=[pl.BlockSpec((tm, d), row), pl.BlockSpec((tm, d), row), _full((SUBLANES, LANES)), _full((1, d))],
        out_shape=[_sds((tp, d), BF16), _sds((tp, d), F32), _sds((SUBLANES, LANES), F32), _sds((1, d), F32)],
        compiler_params=_params(("arbitrary",)),
    )(f, wd, h1, tgt, gpf)


def _ffn_bwd_act(dff, wd, p, q):
    tp, d = dff.shape
    ff_dim = wd.shape[0]
    tm = _row_tile(tp)

    def body(dff_ref, w_ref, p_ref, q_ref, da_ref, du_ref):
        dffv = dff_ref[...]
        for lo in range(0, ff_dim, MXU_TILE):
            cols = slice(lo, lo + MXU_TILE)
            df = _dot(dffv, w_ref[cols, :], NT).astype(BF16)
            da_ref[:, cols] = df * p_ref[:, cols]
            du_ref[:, cols] = df * q_ref[:, cols]

    aspec = pl.BlockSpec((tm, ff_dim), lambda i: (i, 0))
    return pl.pallas_call(
        body, name="ffn_bwd_act", grid=(tp // tm,),
        in_specs=[pl.BlockSpec((tm, d), lambda i: (i, 0)), _resident(wd.shape), aspec, aspec],
        out_specs=[aspec, aspec],
        out_shape=[_sds((tp, ff_dim), BF16)] * 2,
        compiler_params=_params(("arbitrary",)),
    )(dff, wd, p, q)


def _grad_blocks(ff_dim):
    rows = ff_dim // 2
    assert rows % LANES == 0
    return rows


def _grad_w_down(f, dff):
    tp, ff_dim = f.shape
    d = dff.shape[1]
    rows = _grad_blocks(ff_dim)

    def body(f_ref, dff_ref, g_ref):
        g_ref[...] = _dot(f_ref[...], dff_ref[...], TN).astype(BF16)

    return pl.pallas_call(
        body, name="grad_w_down", grid=(ff_dim // rows,),
        in_specs=[pl.BlockSpec((tp, rows), lambda k: (0, k)), _resident(dff.shape)],
        out_specs=pl.BlockSpec((rows, d), lambda k: (k, 0)),
        out_shape=_sds((ff_dim, d), BF16),
        compiler_params=_params(("arbitrary",)),
    )(f, dff)


def _grad_w_gate_up(xn2, da, du):
    tp, ff_dim = da.shape
    d = xn2.shape[1]
    rows = _grad_blocks(ff_dim)

    def body(xn_ref, da_ref, du_ref, gg_ref, gu_ref):
        xn = xn_ref[...]
        gg_ref[...] = _dot(da_ref[...], xn, TN).astype(BF16)
        gu_ref[...] = _dot(du_ref[...], xn, TN).astype(BF16)

    aspec = pl.BlockSpec((tp, rows), lambda k: (0, k))
    gspec = pl.BlockSpec((rows, d), lambda k: (k, 0))
    return pl.pallas_call(
        body, name="grad_w_gate_up", grid=(ff_dim // rows,),
        in_specs=[_resident(xn2.shape), aspec, aspec],
        out_specs=[gspec, gspec],
        out_shape=[_sds((ff_dim, d), BF16)] * 2,
        compiler_params=_params(("arbitrary",)),
    )(xn2, da, du)


def _rms_bwd(dy, x, r, g):
    n = x * r
    dn = dy * g
    return r * (dn - n * _mean(dn * n)), dy * n


def _ffn_bwd_in(da, du, wg, wu, h1, mix, dh2, g2, gpm, after):
    tp, ff_dim = da.shape
    d = h1.shape[1]
    tm = _row_tile(tp)

    def body(da_ref, du_ref, wg_ref, wu_ref, h1_ref, mix_ref, dh2_ref, g2_ref, gpm_ref, after_ref,
             dh1_ref, dmix_ref, dg2_ref, dgpm_ref):
        i = pl.program_id(0)

        @pl.when(i == 0)
        def _():
            dg2_ref[...] = jnp.zeros(dg2_ref.shape, F32)
            dgpm_ref[...] = jnp.zeros(dgpm_ref.shape, F32)

        for rs in _row_parts(tm, 3):
            dxn = _dot(da_ref[rs, :], wg_ref[...]) + _dot(du_ref[rs, :], wu_ref[...])
            h1v = h1_ref[rs, :]
            r2 = lax.rsqrt(_mean(h1v * h1v) + RMS_EPS)
            dres, dg2_rows = _rms_bwd(dxn, h1v, r2, g2_ref[...])
            dh1 = dh2_ref[rs, :] + dres
            dh1_ref[rs, :] = dh1
            mixv = mix_ref[rs, :]
            rm = lax.rsqrt(_mean(mixv * mixv) + RMS_EPS)
            dmix, dgpm_rows = _rms_bwd(dh1, mixv, rm, gpm_ref[...])
            dmix_ref[rs, :] = dmix.astype(BF16)
            dg2_ref[...] += jnp.sum(dg2_rows, axis=0, keepdims=True)
            dgpm_ref[...] += jnp.sum(dgpm_rows, axis=0, keepdims=True)

    aspec = pl.BlockSpec((tm, ff_dim), lambda i: (i, 0))
    row = pl.BlockSpec((tm, d), lambda i: (i, 0))
    return pl.pallas_call(
        body, name="ffn_bwd_in", grid=(tp // tm,),
        in_specs=[aspec, aspec, _resident(wg.shape), _resident(wu.shape), row, row, row, _full(g2.shape), _full(gpm.shape),
                  ANY],
        out_specs=[row, row, _full((1, d)), _full((1, d))],
        out_shape=[_sds((tp, d), F32), _sds((tp, d), BF16), _sds((1, d), F32), _sds((1, d), F32)],
        compiler_params=_params(("arbitrary",)),
    )(da, du, wg, wu, h1, mix, dh2, g2, gpm, after)


def _grad_w_out(ya, yb, dmix, after):
    tp, wa_ = ya.shape
    d = dmix.shape[1]

    def body(ya_ref, yb_ref, dmix_ref, after_ref, g_ref):
        dm = dmix_ref[...]
        g_ref[0:wa_, :] = _dot(ya_ref[...], dm, TN).astype(BF16)
        g_ref[wa_:2 * wa_, :] = _dot(yb_ref[...], dm, TN).astype(BF16)

    return pl.pallas_call(
        body, name="grad_w_out", grid=(1,),
        in_specs=[_full(ya.shape), _full(yb.shape), _full(dmix.shape), ANY],
        out_specs=_full((2 * wa_, d)),
        out_shape=_sds((2 * wa_, d), BF16),
        compiler_params=_params(("arbitrary",)),
    )(ya, yb, dmix, after)


def _mix_bwd_out(dmix, wout, z, lg, lb, after):
    tp, d = dmix.shape
    wa_ = z.shape[1]
    tm = _row_tile(tp)

    def body(dmix_ref, w_ref, z_ref, lg_ref, lb_ref, after_ref, dya_ref, dz_ref, dlg_ref, dlb_ref):
        i = pl.program_id(0)
        lg_ = lg_ref[...]

        @pl.when(i == 0)
        def _():
            dlg_ref[...] = jnp.zeros(dlg_ref.shape, F32)
            dlb_ref[...] = jnp.zeros(dlb_ref.shape, F32)

        for rs in _row_parts(tm):
            dm = dmix_ref[rs, :]
            dya_ref[rs, :] = _dot(dm, w_ref[0:wa_, :], NT)
            dyb = _dot(dm, w_ref[wa_:d, :], NT)
            rl, zh, l = _layer_norm_parts(z_ref[rs, :], lg_, lb_ref[...])
            sl = _sig(l)
            dl = dyb * (sl * (1.0 + l * (1.0 - sl)))
            dzh = dl * lg_
            dz_ref[rs, :] = rl * (dzh - _mean(dzh) - zh * _mean(dzh * zh))
            dlg_ref[...] += jnp.sum(dl * zh, axis=0, keepdims=True)
            dlb_ref[...] += jnp.sum(dl, axis=0, keepdims=True)

    row = lambda i: (i, 0)
    return pl.pallas_call(
        body, name="mix_bwd_out", grid=(tp // tm,),
        in_specs=[pl.BlockSpec((tm, d), row), _resident(wout.shape), pl.BlockSpec((tm, wa_), row), _full(lg.shape),
                  _full(lb.shape), ANY],
        out_specs=[pl.BlockSpec((tm, wa_), row), pl.BlockSpec((tm, wa_), row), _full((1, wa_)), _full((1, wa_))],
        out_shape=[_sds((tp, wa_), F32), _sds((tp, wa_), F32), _sds((1, wa_), F32), _sds((1, wa_), F32)],
        compiler_params=_params(("arbitrary",)),
    )(dmix, wout, z, lg, lb, after)


def _mix_conv_bwd(hp5, dya, dz, wa, wb):
    _, tp, wgrp = hp5.shape
    seq, nseq = _seq_rows(tp)
    sb = nseq + 2 * CONV_HIST
    ka, kb = wa.shape[0], wb.shape[0]
    xs, ms = slice(0, seq), slice(seq, seq + N_META)
    ox, om = slice(CONV_HIST + N_META, CONV_HIST + nseq), slice(CONV_HIST, CONV_HIST + N_META)
    n_tail = tp - seq - N_META

    def body(hp_ref, dya_ref, dz_ref, wa_ref, wb_ref, dhp_ref, dwa_ref, dwb_ref, dbb_ref, s_ref, d_ref, o_ref, acc_ref,
             shs_ref, shd_ref):
        _zero_ends(s_ref, nseq)
        _zero_ends(d_ref, nseq)

        def put(p, ox_val, om_val):
            dhp_ref[p, xs, :] = ox_val.astype(BF16)
            dhp_ref[p, ms, :] = om_val.astype(BF16)
            dhp_ref[p, seq + N_META:tp, :] = jnp.zeros((n_tail, LANES), BF16)

        def wgrad(dw_ref, width):
            for k in range(width):
                dw_ref[k:k + 1, :] = jnp.sum(acc_ref[8 * k:8 * k + 8, :], axis=0, keepdims=True)

        _to_seq(s_ref, hp_ref[1, xs, :] * hp_ref[2, xs, :], hp_ref[1, ms, :] * hp_ref[2, ms, :], seq)
        _conv_taps(s_ref, shs_ref, wa_ref, o_ref, ka, nseq, False)
        put(0, dya_ref[xs, :] * o_ref[ox, :], dya_ref[ms, :] * o_ref[om, :])
        _to_seq(d_ref, dya_ref[xs, :] * hp_ref[0, xs, :], dya_ref[ms, :] * hp_ref[0, ms, :], seq)
        _conv_wgrad(s_ref, shs_ref, d_ref, acc_ref, ka, nseq)
        wgrad(dwa_ref, ka)
        _conv_taps(d_ref, shd_ref, wa_ref, o_ref, ka, nseq, True)
        put(1, o_ref[ox, :] * hp_ref[2, xs, :], o_ref[om, :] * hp_ref[2, ms, :])
        put(2, o_ref[ox, :] * hp_ref[1, xs, :], o_ref[om, :] * hp_ref[1, ms, :])

        _to_seq(s_ref, hp_ref[3, xs, :] * _sig(hp_ref[4, xs, :]), hp_ref[3, ms, :] * _sig(hp_ref[4, ms, :]), seq)
        _to_seq(d_ref, dz_ref[xs, :], dz_ref[ms, :], seq)
        dbb_ref[...] = (jnp.sum(dz_ref[xs, :], axis=0, keepdims=True)
                        + jnp.sum(dz_ref[ms, :], axis=0, keepdims=True))
        _shift_copies(d_ref, shd_ref, kb, True)
        _conv_wgrad_by_dz_shifts(s_ref, d_ref, shd_ref, acc_ref, kb, nseq)
        wgrad(dwb_ref, kb)
        _conv_taps(d_ref, shd_ref, wb_ref, o_ref, kb, nseq, True, shifted=True)
        sx, sm = _sig(hp_ref[4, xs, :]), _sig(hp_ref[4, ms, :])
        put(3, o_ref[ox, :] * sx, o_ref[om, :] * sm)
        put(4, o_ref[ox, :] * hp_ref[3, xs, :] * sx * (1.0 - sx), o_ref[om, :] * hp_ref[3, ms, :] * sm * (1.0 - sm))

    col = lambda j: (0, j)
    blk5 = pl.BlockSpec((5, tp, LANES), lambda j: (0, 0, j))
    return pl.pallas_call(
        body, name="mix_conv_bwd", grid=(wgrp // LANES,),
        in_specs=[blk5, pl.BlockSpec((tp, LANES), col), pl.BlockSpec((tp, LANES), col),
                  pl.BlockSpec((ka, LANES), col), pl.BlockSpec((kb, LANES), col)],
        out_specs=[blk5, pl.BlockSpec((ka, LANES), col), pl.BlockSpec((kb, LANES), col), pl.BlockSpec((1, LANES), col)],
        out_shape=[_sds((5, tp, wgrp), BF16), _sds((ka, wgrp), F32), _sds((kb, wgrp), F32), _sds((1, wgrp), F32)],
        scratch_shapes=[pltpu.VMEM((sb, LANES), F32), pltpu.VMEM((sb, LANES), F32), pltpu.VMEM((sb, LANES), F32),
                        pltpu.VMEM((SUBLANES * kb, LANES), F32), pltpu.VMEM((SUBLANES - 1, sb, LANES), F32),
                        pltpu.VMEM((SUBLANES - 1, sb, LANES), F32)],
        compiler_params=_params(("arbitrary",)),
    )(hp5, dya, dz, wa, wb)


def _grad_w_in(xn1, dhp5):
    n_p, tp, pw = dhp5.shape
    d = xn1.shape[1]

    def body(xn_ref, dhp_ref, g_ref):
        g_ref[...] = _dot(xn_ref[...], dhp_ref[0], TN).astype(BF16)

    return pl.pallas_call(
        body, name="grad_w_in", grid=(n_p,),
        in_specs=[_resident(xn1.shape), pl.BlockSpec((1, tp, pw), lambda p: (p, 0, 0))],
        out_specs=pl.BlockSpec((d, pw), lambda p: (0, p)),
        out_shape=_sds((d, n_p * pw), BF16),
        compiler_params=_params(("arbitrary",)),
    )(xn1, dhp5)


def _mix_bwd_in(dhp5, win4, h, dh1, g1, after):
    n_p, tp, pw = dhp5.shape
    d = h.shape[1]
    n_sh, _, csh = win4.shape
    tm = _row_tile(tp)

    seq, _ = _seq_rows(tp)
    last, meta_off = seq // tm, seq % tm
    assert last == tp // tm - 1
    assert any(rs.start <= meta_off and meta_off + N_META <= rs.stop for rs in _row_parts(tm))

    def body(dhp_ref, w_ref, h_ref, dh1_ref, g_ref, after_ref, gx_ref, dmeta_ref, dg1_ref, wcat_ref):
        i = pl.program_id(0)
        _concat_shards(w_ref, wcat_ref)

        @pl.when(i == 0)
        def _():
            dg1_ref[...] = jnp.zeros(dg1_ref.shape, F32)

        for rs in _row_parts(tm):
            dxn = _dot(dhp_ref[0, rs, :], wcat_ref[:, 0:pw], NT)
            for p in range(1, n_p):
                dxn = dxn + _dot(dhp_ref[p, rs, :], wcat_ref[:, p * pw:(p + 1) * pw], NT)
            hh = h_ref[rs, :]
            r1 = lax.rsqrt(_mean(hh * hh) + RMS_EPS)
            dres, dg_rows = _rms_bwd(dxn, hh, r1, g_ref[...])
            dh = dh1_ref[rs, :] + dres
            gx_ref[rs, :] = dh
            dg1_ref[...] += jnp.sum(dg_rows, axis=0, keepdims=True)
            if rs.start <= meta_off and meta_off + N_META <= rs.stop:
                @pl.when(i == last)
                def _():
                    dmeta_ref[...] = dh[meta_off - rs.start:meta_off - rs.start + N_META, :]

    row = lambda i: (i, 0)
    return pl.pallas_call(
        body, name="mix_bwd_in", grid=(tp // tm,),
        in_specs=[pl.BlockSpec((n_p, tm, pw), lambda i: (0, i, 0)), _resident(win4.shape), pl.BlockSpec((tm, d), row),
                  pl.BlockSpec((tm, d), row), _full(g1.shape), ANY],
        out_specs=[pl.BlockSpec((tm, d), row), _full((N_META, d)), _full((1, d))],
        out_shape=[_sds((seq, d), F32), _sds((N_META, d), F32), _sds((1, d), F32)],
        scratch_shapes=[pltpu.VMEM((d, n_sh * csh), BF16)],
        compiler_params=_params(("arbitrary",)),
    )(dhp5, win4, h, dh1, g1, after)


def _other_chips(x, y):
    out = []
    for j in (1, 2, 3):
        px, py = _flip(x, j >> 1), _flip(y, j & 1)
        out.append((px, py, 2 * px + py))
    return out


PAIR_COLLECTIVE_ID = 0


def _pair_barrier(x, y, c):
    sem = pltpu.get_barrier_semaphore()
    pl.semaphore_signal(sem, inc=1, device_id=(x, y, 1 - c), device_id_type=MESH)
    pl.semaphore_wait(sem, 1)


def _pair_params():
    return pltpu.CompilerParams(collective_id=PAIR_COLLECTIVE_ID)


def _half_rows(c, rows_half):
    return pl.ds(pl.multiple_of(c * rows_half, SUBLANES), rows_half)


def _cast_place(ws, q_arr, tag, after=None):
    n = len(ws)
    extra = [] if after is None else [after]

    def fits(steps):
        return all(w.shape[0] % steps == 0 and (w.shape[0] // steps) % BF16_ROWS == 0
                   and w.shape[0] // steps <= ROW_TILE_CAP for w in ws)

    steps = next(s for s in range(1, min(w.shape[0] for w in ws) + 1) if fits(s))

    def body(q_ref, *refs):
        for w_ref, out_ref in zip(refs[:n], refs[n + len(extra):]):
            out_ref[0] = w_ref[...].astype(BF16)

    return list(pl.pallas_call(
        body, name="cast_place_" + tag,
        grid_spec=pltpu.PrefetchScalarGridSpec(
            num_scalar_prefetch=1, grid=(steps,),
            in_specs=[pl.BlockSpec((w.shape[0] // steps, w.shape[1]), lambda i, q: (i, 0)) for w in ws] + [ANY] * len(extra),
            out_specs=[pl.BlockSpec((1, w.shape[0] // steps, w.shape[1]), lambda i, q: (q[0], i, 0)) for w in ws]),
        out_shape=[_sds((N_CHIPS,) + w.shape, BF16) for w in ws],
        compiler_params=_params(("arbitrary",)),
    )(q_arr, *ws, *extra))


HBM = pl.BlockSpec(memory_space=pltpu.HBM)
SEM = pl.BlockSpec(memory_space=pltpu.SEMAPHORE)
EFFECT = pltpu.SideEffectType.DATAFLOW_SIDE_EFFECTING


def _in_hbm(a):
    return pltpu.with_memory_space_constraint(a, pltpu.HBM)


def _gather_start(fulls, after, tag):
    n = len(fulls)
    halves = [a.shape[1] // 2 for a in fulls]

    def body(*refs):
        land = refs[:n]
        ssem, rsem = refs[n + 1], refs[n + 2]
        token = refs[-1]
        x, y, c = _mesh_pos()
        q = 2 * x + y
        for i in range(n):
            for j, (px, py, _) in enumerate(_other_chips(x, y)):
                mine = land[i].at[q, _half_rows(c, halves[i]), :]
                pltpu.make_async_remote_copy(src_ref=mine, dst_ref=mine, send_sem=ssem.at[3 * i + j],
                                             recv_sem=rsem.at[3 * i + j], device_id=(px, py, c), device_id_type=MESH).start()
        token[...] = jnp.zeros(token.shape, F32)

    outs = pl.pallas_call(
        body, name="gather_start_" + tag,
        in_specs=[HBM] * n + [ANY], out_specs=[SEM, SEM] + [HBM] * n + [VMEM],
        out_shape=[pltpu.SemaphoreType.DMA((3 * n,)), pltpu.SemaphoreType.DMA((3 * n,))]
        + [pltpu.HBM(a.shape, a.dtype) for a in fulls] + [_sds((SUBLANES, LANES), F32)],
        input_output_aliases={i: 2 + i for i in range(n)},
        compiler_params=pltpu.CompilerParams(has_side_effects=EFFECT),
    )(*[_in_hbm(a) for a in fulls], after)
    return outs[0], outs[1], list(outs[2:2 + n]), outs[-1]


def _gather_wait(which, ssem, rsem, lands, after, tag):
    m = len(which)
    halves = [a.shape[1] // 2 for a in lands]

    def body(*refs):
        land = refs[:m]
        ssem_, rsem_ = refs[m], refs[m + 1]
        x, y, c = _mesh_pos()
        for t, i in enumerate(which):
            for j, (px, py, qj) in enumerate(_other_chips(x, y)):
                rows = _half_rows(c, halves[t])
                cp = pltpu.make_async_remote_copy(src_ref=land[t].at[2 * x + y, rows, :], dst_ref=land[t].at[qj, rows, :],
                                                  send_sem=ssem_.at[3 * i + j], recv_sem=rsem_.at[3 * i + j],
                                                  device_id=(px, py, c), device_id_type=MESH)
                cp.wait_send()
                cp.wait_recv()

    outs = pl.pallas_call(
        body, name="gather_wait_" + tag,
        in_specs=[HBM] * m + [SEM, SEM, ANY], out_specs=[HBM] * m,
        out_shape=[pltpu.HBM(a.shape, a.dtype) for a in lands],
        input_output_aliases={i: i for i in range(m)},
        compiler_params=pltpu.CompilerParams(has_side_effects=EFFECT),
    )(*lands, ssem, rsem, after)
    return list(outs)


def _forward_pair(lands, tag):
    n = len(lands)
    halves = [a.shape[1] // 2 for a in lands]

    def body(*refs):
        full = refs[n:2 * n]
        ssem, rsem = refs[2 * n:]
        x, y, c = _mesh_pos()
        _pair_barrier(x, y, c)
        cps = []
        for i in range(n):
            for j, (_, _, qj) in enumerate(_other_chips(x, y)):
                part = full[i].at[qj, _half_rows(c, halves[i]), :]
                cp = pltpu.make_async_remote_copy(src_ref=part, dst_ref=part, send_sem=ssem.at[3 * i + j],
                                                  recv_sem=rsem.at[3 * i + j], device_id=(x, y, 1 - c), device_id_type=MESH)
                cp.start()
                cps.append(cp)
        for cp in cps:
            cp.wait()

    return pl.pallas_call(
        body, name="forward_pair_" + tag,
        in_specs=[ANY] * n, out_specs=[ANY] * n,
        out_shape=[_sds(a.shape, a.dtype) for a in lands],
        input_output_aliases={i: i for i in range(n)},
        scratch_shapes=[pltpu.SemaphoreType.DMA((3 * n,)), pltpu.SemaphoreType.DMA((3 * n,))],
        compiler_params=_pair_params(),
    )(*lands)


def _chip_exchange_start(parts, after, tag):
    n = len(parts)

    def body(*refs):
        src, land = refs[:n], refs[n:2 * n]
        ssem, rsem = refs[2 * n + 1], refs[2 * n + 2]
        token = refs[-1]
        x, y, c = _mesh_pos()
        for i in range(n):
            for j, (px, py, qj) in enumerate(_other_chips(x, y)):
                pltpu.make_async_remote_copy(src_ref=src[i].at[qj], dst_ref=land[i].at[j], send_sem=ssem.at[3 * i + j],
                                             recv_sem=rsem.at[3 * i + j], device_id=(px, py, c), device_id_type=MESH).start()
        token[...] = jnp.zeros(token.shape, F32)

    lands = [lax.empty((3,) + a.shape[1:], a.dtype) for a in parts]
    outs = pl.pallas_call(
        body, name="chip_exchange_start_" + tag,
        in_specs=[HBM] * (2 * n) + [ANY], out_specs=[SEM, SEM] + [HBM] * (2 * n) + [VMEM],
        out_shape=[pltpu.SemaphoreType.DMA((3 * n,)), pltpu.SemaphoreType.DMA((3 * n,))]
        + [pltpu.HBM(a.shape, a.dtype) for a in parts] + [pltpu.HBM(a.shape, a.dtype) for a in lands]
        + [_sds((SUBLANES, LANES), F32)],
        input_output_aliases={i: 2 + i for i in range(2 * n)},
        compiler_params=pltpu.CompilerParams(has_side_effects=EFFECT),
    )(*[_in_hbm(a) for a in parts], *[_in_hbm(a) for a in lands], after)
    return outs[0], outs[1], list(outs[2:2 + n]), list(outs[2 + n:2 + 2 * n]), outs[-1]


def _chip_exchange_wait(ssem, rsem, parts, lands, after, tag):
    n = len(parts)

    def body(*refs):
        src, land = refs[:n], refs[n:2 * n]
        ssem_, rsem_ = refs[2 * n], refs[2 * n + 1]
        x, y, c = _mesh_pos()
        for i in range(n):
            for j, (px, py, qj) in enumerate(_other_chips(x, y)):
                cp = pltpu.make_async_remote_copy(src_ref=src[i].at[qj], dst_ref=land[i].at[j], send_sem=ssem_.at[3 * i + j],
                                                  recv_sem=rsem_.at[3 * i + j], device_id=(px, py, c), device_id_type=MESH)
                cp.wait_send()
                cp.wait_recv()

    outs = pl.pallas_call(
        body, name="chip_exchange_wait_" + tag,
        in_specs=[HBM] * (2 * n) + [SEM, SEM, ANY], out_specs=[HBM] * (2 * n),
        out_shape=[pltpu.HBM(a.shape, a.dtype) for a in parts] + [pltpu.HBM(a.shape, a.dtype) for a in lands],
        input_output_aliases={i: i for i in range(2 * n)},
        compiler_params=pltpu.CompilerParams(has_side_effects=EFFECT),
    )(*parts, *lands, ssem, rsem, after)
    return list(outs[:n]), list(outs[n:])


def _grad_half(ref, shape, axis, which):
    rows = shape[axis] // 2
    if axis == 0:
        return ref.at[_half_rows(which, rows), :]
    return ref.at[:, _half_rows(which, rows), :]


def _half_shape(a, axis):
    s = list(a.shape)
    s[axis] //= 2
    return tuple(s)


def _pair_exchange_start(grads, half_axis, after, tag):
    n = len(grads)

    def body(*refs):
        g, land = refs[:n], refs[n:2 * n]
        ssem, rsem = refs[2 * n + 1], refs[2 * n + 2]
        token = refs[-1]
        x, y, c = _mesh_pos()
        for i in range(n):
            pltpu.make_async_remote_copy(src_ref=_grad_half(g[i], grads[i].shape, half_axis[i], 1 - c), dst_ref=land[i],
                                         send_sem=ssem.at[i], recv_sem=rsem.at[i], device_id=(x, y, 1 - c),
                                         device_id_type=MESH).start()
        token[...] = jnp.zeros(token.shape, F32)

    lands = [lax.empty(_half_shape(a, half_axis[i]), a.dtype) for i, a in enumerate(grads)]
    outs = pl.pallas_call(
        body, name="pair_exchange_start_" + tag,
        in_specs=[HBM] * (2 * n) + [ANY], out_specs=[SEM, SEM] + [HBM] * (2 * n) + [VMEM],
        out_shape=[pltpu.SemaphoreType.DMA((n,)), pltpu.SemaphoreType.DMA((n,))]
        + [pltpu.HBM(a.shape, a.dtype) for a in grads] + [pltpu.HBM(a.shape, a.dtype) for a in lands]
        + [_sds((SUBLANES, LANES), F32)],
        input_output_aliases={i: 2 + i for i in range(2 * n)},
        compiler_params=pltpu.CompilerParams(has_side_effects=EFFECT),
    )(*[_in_hbm(a) for a in grads], *[_in_hbm(a) for a in lands], after)
    return outs[0], outs[1], list(outs[2:2 + n]), list(outs[2 + n:2 + 2 * n]), outs[-1]


def _pair_exchange_wait(ssem, rsem, grads, lands, half_axis, after, tag):
    n = len(grads)

    def body(*refs):
        g, land = refs[:n], refs[n:2 * n]
        ssem_, rsem_ = refs[2 * n], refs[2 * n + 1]
        x, y, c = _mesh_pos()
        for i in range(n):
            cp = pltpu.make_async_remote_copy(src_ref=_grad_half(g[i], grads[i].shape, half_axis[i], 1 - c),
                                              dst_ref=land[i], send_sem=ssem_.at[i], recv_sem=rsem_.at[i],
                                              device_id=(x, y, 1 - c), device_id_type=MESH)
            cp.wait_send()
            cp.wait_recv()

    outs = pl.pallas_call(
        body, name="pair_exchange_wait_" + tag,
        in_specs=[HBM] * (2 * n) + [SEM, SEM, ANY], out_specs=[HBM] * (2 * n),
        out_shape=[pltpu.HBM(a.shape, a.dtype) for a in grads] + [pltpu.HBM(a.shape, a.dtype) for a in lands],
        input_output_aliases={i: i for i in range(2 * n)},
        compiler_params=pltpu.CompilerParams(has_side_effects=EFFECT),
    )(*grads, *lands, ssem, rsem, after)
    return list(outs[:n]), list(outs[n:])


def _pair_exchange_grads(grads, half_axis, tag):
    n = len(grads)

    def body(*refs):
        g, got = refs[:n], refs[n:2 * n]
        ssem, rsem = refs[2 * n:]
        x, y, c = _mesh_pos()
        _pair_barrier(x, y, c)
        cps = []
        for i in range(n):
            cp = pltpu.make_async_remote_copy(src_ref=_grad_half(g[i], grads[i].shape, half_axis[i], 1 - c),
                                              dst_ref=got[i], send_sem=ssem.at[i], recv_sem=rsem.at[i],
                                              device_id=(x, y, 1 - c), device_id_type=MESH)
            cp.start()
            cps.append(cp)
        for cp in cps:
            cp.wait()

    return pl.pallas_call(
        body, name="pair_exchange_grads_" + tag,
        in_specs=[ANY] * n, out_specs=[ANY] * n,
        out_shape=[_sds(_half_shape(a, half_axis[i]), a.dtype) for i, a in enumerate(grads)],
        scratch_shapes=[pltpu.SemaphoreType.DMA((n,)), pltpu.SemaphoreType.DMA((n,))],
        compiler_params=_pair_params(),
    )(*grads)


def _pair_sum(gs, gots, c_arr, col_sharded, tag):
    n = len(gs)
    g_specs, got_specs, out_specs, out_shapes = [], [], [], []
    for g, by_cols in zip(gs, col_sharded):
        if by_cols:
            rows, cols = g.shape
            rh, cs = rows // 2, cols // N_CHIPS
            g_specs.append(pl.BlockSpec((rh, cs), lambda k, c_ref: (c_ref[0], k)))
            got_specs.append(pl.BlockSpec((rh, cs), lambda k, c_ref: (0, k)))
        else:
            _, rows, cs = g.shape
            rh = rows // 2
            g_specs.append(pl.BlockSpec((1, rh, cs), lambda k, c_ref: (k, c_ref[0], 0)))
            got_specs.append(pl.BlockSpec((1, rh, cs), lambda k, c_ref: (k, 0, 0)))
        out_specs.append(pl.BlockSpec((1, rh, cs), lambda k, c_ref: (k, 0, 0)))
        out_shapes.append(_sds((N_CHIPS, rh, cs), BF16))

    def body(c_ref, *refs):
        for g_ref, got_ref, out_ref in zip(refs[:n], refs[n:2 * n], refs[2 * n:]):
            total = g_ref[...].astype(F32) + got_ref[...].astype(F32)
            out_ref[...] = total.astype(BF16).reshape(out_ref.shape)

    return list(pl.pallas_call(
        body, name="pair_sum_" + tag,
        grid_spec=pltpu.PrefetchScalarGridSpec(
            num_scalar_prefetch=1, grid=(N_CHIPS,), in_specs=g_specs + got_specs, out_specs=out_specs),
        out_shape=out_shapes,
        compiler_params=_params(("arbitrary",)),
    )(c_arr, *gs, *gots))


def _chip_sum(parts, gots, qc_arr, tag):
    n = len(parts)
    steps = 2 if all(p.shape[1] % 32 == 0 for p in parts) else 1
    part_specs, got_specs, out_specs, out_shapes = [], [], [], []
    for p in parts:
        _, rh, cs = p.shape
        rb = rh // steps
        part_specs.append(pl.BlockSpec((1, rb, cs), lambda i, qc: (qc[0], i, 0)))
        got_specs.append(pl.BlockSpec((3, rb, cs), lambda i, qc: (0, i, 0)))
        out_specs.append(pl.BlockSpec((rb, cs), lambda i, qc: (qc[1] * steps + i, 0)))
        out_shapes.append(_sds((2 * rh, cs), F32))

    def body(qc_ref, *refs):
        for part_ref, got_ref, out_ref in zip(refs[:n], refs[n:2 * n], refs[2 * n:]):
            total = part_ref[0].astype(F32)
            for j in range(3):
                total = total + got_ref[j].astype(F32)
            out_ref[...] = total

    return list(pl.pallas_call(
        body, name="chip_sum_" + tag,
        grid_spec=pltpu.PrefetchScalarGridSpec(
            num_scalar_prefetch=1, grid=(steps,), in_specs=part_specs + got_specs, out_specs=out_specs),
        out_shape=out_shapes,
        compiler_params=_params(("arbitrary",)),
    )(qc_arr, *parts, *gots))


def _pair_share_grads(grads, tag):
    n = len(grads)

    def body(*refs):
        g = refs[n:2 * n]
        ssem, rsem = refs[2 * n:]
        x, y, c = _mesh_pos()
        _pair_barrier(x, y, c)
        cps = []
        for i in range(n):
            mine = g[i].at[_half_rows(c, grads[i].shape[0] // 2), :]
            cp = pltpu.make_async_remote_copy(src_ref=mine, dst_ref=mine, send_sem=ssem.at[i], recv_sem=rsem.at[i],
                                              device_id=(x, y, 1 - c), device_id_type=MESH)
            cp.start()
            cps.append(cp)
        for cp in cps:
            cp.wait()

    return pl.pallas_call(
        body, name="pair_share_grads_" + tag,
        in_specs=[ANY] * n, out_specs=[ANY] * n,
        out_shape=[_sds(a.shape, a.dtype) for a in grads],
        input_output_aliases={i: i for i in range(n)},
        scratch_shapes=[pltpu.SemaphoreType.DMA((n,)), pltpu.SemaphoreType.DMA((n,))],
        compiler_params=_pair_params(),
    )(*grads)


def _pair_share_start(grads, after, tag):
    n = len(grads)

    def body(*refs):
        g = refs[:n]
        ssem, rsem = refs[n + 1], refs[n + 2]
        token = refs[-1]
        x, y, c = _mesh_pos()
        for i in range(n):
            mine = g[i].at[_half_rows(c, grads[i].shape[0] // 2), :]
            pltpu.make_async_remote_copy(src_ref=mine, dst_ref=mine, send_sem=ssem.at[i], recv_sem=rsem.at[i],
                                         device_id=(x, y, 1 - c), device_id_type=MESH).start()
        token[...] = jnp.zeros(token.shape, F32)

    outs = pl.pallas_call(
        body, name="pair_share_start_" + tag,
        in_specs=[HBM] * n + [ANY], out_specs=[SEM, SEM] + [HBM] * n + [VMEM],
        out_shape=[pltpu.SemaphoreType.DMA((n,)), pltpu.SemaphoreType.DMA((n,))]
        + [pltpu.HBM(a.shape, a.dtype) for a in grads] + [_sds((SUBLANES, LANES), F32)],
        input_output_aliases={i: 2 + i for i in range(n)},
        compiler_params=pltpu.CompilerParams(has_side_effects=EFFECT),
    )(*[_in_hbm(a) for a in grads], after)
    return outs[0], outs[1], list(outs[2:2 + n]), outs[-1]


def _pair_share_wait(ssem, rsem, grads, after, tag):
    n = len(grads)

    def body(*refs):
        g = refs[:n]
        ssem_, rsem_ = refs[n], refs[n + 1]
        x, y, c = _mesh_pos()
        for i in range(n):
            rows = grads[i].shape[0] // 2
            cp = pltpu.make_async_remote_copy(src_ref=g[i].at[_half_rows(c, rows), :], dst_ref=g[i].at[_half_rows(1 - c, rows), :],
                                              send_sem=ssem_.at[i], recv_sem=rsem_.at[i], device_id=(x, y, 1 - c),
                                              device_id_type=MESH)
            cp.wait_send()
            cp.wait_recv()

    outs = pl.pallas_call(
        body, name="pair_share_wait_" + tag,
        in_specs=[HBM] * n + [SEM, SEM, ANY], out_specs=[HBM] * n,
        out_shape=[pltpu.HBM(a.shape, a.dtype) for a in grads],
        input_output_aliases={i: i for i in range(n)},
        compiler_params=pltpu.CompilerParams(has_side_effects=EFFECT),
    )(*grads, ssem, rsem, after)
    return list(outs)


def _small_allreduce(parts, places, rows_total, width, after):
    n = len(parts)

    def body(*refs):
        ins, out_ref = refs[:n], refs[n + 1]
        pack, pair_got, chip_sum, got, ssem, rsem = refs[n + 2:]
        x, y, c = _mesh_pos()
        chip = 2 * x + y
        pack[...] = jnp.zeros(pack.shape, F32)
        for i in range(n):
            for row, col, src_row, rows in places[i]:
                w = parts[i].shape[1]
                pack[row:row + rows, col:col + w] = ins[i][src_row:src_row + rows, :]
        swap = pltpu.make_async_remote_copy(src_ref=pack, dst_ref=pair_got, send_sem=ssem.at[3], recv_sem=rsem.at[3],
                                            device_id=(x, y, 1 - c), device_id_type=MESH)
        swap.start()
        swap.wait()
        chip_sum[...] = pack[...] + pair_got[...]
        cps = []
        for j, (px, py, _) in enumerate(_other_chips(x, y)):
            cp = pltpu.make_async_remote_copy(src_ref=chip_sum, dst_ref=got.at[j], send_sem=ssem.at[j],
                                              recv_sem=rsem.at[j], device_id=(px, py, c), device_id_type=MESH)
            cp.start()
            cps.append(cp)
        for cp in cps:
            cp.wait()
        total = jnp.zeros(pack.shape, F32)
        for q in range(N_CHIPS):
            rel = jnp.bitwise_xor(chip, q)
            theirs = got[jnp.maximum(rel - 1, 0)]
            total = total + jnp.where(rel == 0, chip_sum[...], theirs)
        out_ref[...] = total

    return pl.pallas_call(
        body, name="small_allreduce",
        in_specs=[VMEM] * n + [ANY], out_specs=VMEM,
        out_shape=_sds((rows_total, width), F32),
        scratch_shapes=[pltpu.VMEM((rows_total, width), F32), pltpu.VMEM((rows_total, width), F32),
                        pltpu.VMEM((rows_total, width), F32), pltpu.VMEM((3, rows_total, width), F32),
                        pltpu.SemaphoreType.DMA((4,)), pltpu.SemaphoreType.DMA((4,))],
        compiler_params=_params(),
    )(*parts, after)


def _small_update(red, q_arr, takes, loss_at, ws, ms, vs):
    n_w = len(ws)

    def body(q_ref, red_ref, *refs):
        w_in, m_in, v_in = refs[0:n_w], refs[n_w:2 * n_w], refs[2 * n_w:3 * n_w]
        outs = refs[3 * n_w:]
        g_out, d_out, m_out, v_out = (outs[0:n_w], outs[n_w:2 * n_w], outs[2 * n_w:3 * n_w], outs[3 * n_w:4 * n_w])
        loss_ref = outs[4 * n_w]
        chip = q_ref[0]

        def put(g_ref, d0, nr, s0, lo, w):
            if len(g_ref.shape) == 3:
                for r in range(nr):
                    g_ref[d0 + r] = red_ref[s0 + r:s0 + r + 1, lo:lo + w]
            else:
                g_ref[d0:d0 + nr, :] = red_ref[s0:s0 + nr, lo:lo + w]

        def take_own_columns(g_ref, d0, nr, s0, c0, w):
            for k in range(N_CHIPS):
                @pl.when(chip == k)
                def _():
                    put(g_ref, d0, nr, s0, c0 + k * w, w)

        for j in range(n_w):
            w = ws[j].shape[-1]
            for d0, nr, s0, c0, sharded in takes[j]:
                if sharded:
                    take_own_columns(g_out[j], d0, nr, s0, c0, w)
                else:
                    put(g_out[j], d0, nr, s0, c0, w)
            d_out[j][...], m_out[j][...], v_out[j][...] = _adamw_math(w_in[j][...], g_out[j][...], m_in[j][...], v_in[j][...])
        loss_ref[...] = red_ref[loss_at[0]:loss_at[0] + 1, loss_at[1]:loss_at[1] + LANES]

    shapes = [_sds(w.shape, F32) for w in ws]
    outs = pl.pallas_call(
        body, name="small_update",
        in_specs=[pl.BlockSpec(memory_space=pltpu.SMEM)] + [VMEM] * (1 + 3 * n_w), out_specs=[VMEM] * (4 * n_w + 1),
        out_shape=shapes * 4 + [_sds((1, LANES), F32)],
        compiler_params=_params(),
    )(q_arr, red, *ws, *ms, *vs)
    return outs[0:n_w], outs[n_w:2 * n_w], outs[2 * n_w:3 * n_w], outs[3 * n_w:4 * n_w], outs[4 * n_w]


def _adamw_math(w, g, m, v):
    m2 = ADAM_B1 * m + (1.0 - ADAM_B1) * g
    v2 = ADAM_B2 * v + (1.0 - ADAM_B2) * (g * g)
    m_hat = m2 / (1.0 - ADAM_B1 ** ADAM_STEP)
    v_hat = v2 / (1.0 - ADAM_B2 ** ADAM_STEP)
    delta = -ADAM_LR * (m_hat / (jnp.sqrt(v_hat) + ADAM_EPS) + ADAM_WD * w)
    return delta, m2, v2


ADAMW_BLOCK_BYTES = 3 * 2 ** 19


def _adamw_big(ws, gs, ms, vs, tag):
    n = len(ws)

    def fits(steps):
        return all(w.shape[0] % steps == 0 and (w.shape[0] // steps) % SUBLANES == 0
                   and (w.shape[0] // steps) * w.shape[1] * 4 * n <= ADAMW_BLOCK_BYTES for w in ws)

    steps = next(s for s in range(1, min(w.shape[0] for w in ws) + 1) if fits(s))
    specs = [pl.BlockSpec((w.shape[0] // steps, w.shape[1]), lambda i: (i, 0)) for w in ws]

    def body(*refs):
        ins, outs = refs[:4 * n], refs[4 * n:]
        for i in range(n):
            w_ref, g_ref, m_ref, v_ref = ins[i], ins[n + i], ins[2 * n + i], ins[3 * n + i]
            gg = g_ref[...]
            outs[4 * i][...] = gg
            outs[4 * i + 1][...], outs[4 * i + 2][...], outs[4 * i + 3][...] = _adamw_math(
                w_ref[...], gg, m_ref[...], v_ref[...])

    outs = pl.pallas_call(
        body, name="adamw_" + tag, grid=(steps,),
        in_specs=specs * 4, out_specs=[s for s in specs for _ in range(4)],
        out_shape=[_sds(w.shape, F32) for w in ws for _ in range(4)],
        compiler_params=_params(("arbitrary",)),
    )(*ws, *gs, *ms, *vs)
    return [outs[4 * i:4 * i + 4] for i in range(n)]


SMALL_ROWS = 40
PACK_ROWS = 64


def kernel(x, meta_tokens, pre_mix_norm, w_in, conv_a_w, conv_b_w, conv_b_bias, ln_b_gain, ln_b_bias, w_out, post_mix_norm, pre_ffn_norm, w_gate, w_up, w_down, post_ffn_norm, loss_target, m_meta_tokens, m_pre_mix_norm, m_w_in, m_conv_a_w, m_conv_b_w, m_conv_b_bias, m_ln_b_gain, m_ln_b_bias, m_w_out, m_post_mix_norm, m_pre_ffn_norm, m_w_gate, m_w_up, m_w_down, m_post_ffn_norm, v_meta_tokens, v_pre_mix_norm, v_w_in, v_conv_a_w, v_conv_b_w, v_conv_b_bias, v_ln_b_gain, v_ln_b_bias, v_w_out, v_post_mix_norm, v_pre_ffn_norm, v_w_gate, v_w_up, v_w_down, v_post_ffn_norm):
    xq, yq, cq = lax.axis_index("x"), lax.axis_index("y"), lax.axis_index("c")
    chip = 2 * xq + yq
    c_arr = jnp.reshape(cq, (1,)).astype(jnp.int32)
    qc_arr = jnp.stack([chip, cq]).astype(jnp.int32)

    seq, d = x.shape[1], x.shape[2]
    x2, tgt2 = x[0], loss_target[0]
    tr = lambda a: jnp.swapaxes(a, 1, 2)[0]
    w_in2, w_out2, w_gate2, w_up2, w_down2 = w_in[0], w_out[0], tr(w_gate), tr(w_up), w_down[0]
    ka, wa_sh = conv_a_w.shape[1], conv_a_w.shape[2]
    kb = conv_b_w.shape[1]
    meta_sh = meta_tokens.shape[1]

    small = jnp.zeros((PACK_ROWS, meta_sh), F32)
    small = small.at[0:N_META, :].set(meta_tokens)
    small = small.at[16:16 + ka, 0:wa_sh].set(conv_a_w[0])
    small = small.at[24:24 + kb, 0:wa_sh].set(conv_b_w[0])
    q_arr = jnp.reshape(chip, (1,)).astype(jnp.int32)
    small_own = lax.dynamic_update_slice(jnp.zeros((N_CHIPS, PACK_ROWS, meta_sh), F32), small[None], (chip, 0, 0))
    i_ssem, i_rsem, first, i_token = _gather_start(_cast_place([w_in2], q_arr, "w_in") + [small_own], pre_mix_norm, "in")
    rest = _cast_place([w_out2, w_gate2, w_up2, w_down2], q_arr, "rest", i_token)
    g_ssem, g_rsem, lands, g_token = _gather_start(rest, i_token, "rest")
    win4, small4 = _forward_pair(_gather_wait([0, 1], i_ssem, i_rsem, first, g_token, "in"), "in")
    wa_f = jnp.concatenate([small4[k, 16:16 + ka, 0:wa_sh] for k in range(N_CHIPS)], axis=1)
    wb_f = jnp.concatenate([small4[k, 24:24 + kb, 0:wa_sh] for k in range(N_CHIPS)], axis=1)

    h, xn1, hp5 = _mm_in(x2, small4, win4, pre_mix_norm, g_token)
    ya, z = _mix_conv_fwd(hp5, wa_f, wb_f, conv_b_bias)
    (wout4,) = _forward_pair(_gather_wait([0], g_ssem, g_rsem, lands[0:1], z, "out"), "out")
    wout_f = wout4.reshape(N_CHIPS * wout4.shape[1], wout4.shape[2])
    yb, mix, h1, xn2 = _mm_out(ya, z, h, wout_f, ln_b_gain, ln_b_bias, post_mix_norm, pre_ffn_norm)
    wg4, wu4 = _forward_pair(_gather_wait([1, 2], g_ssem, g_rsem, lands[1:3], xn2, "gate_up"), "gate_up")
    stacked = lambda a: a.reshape(a.shape[0] * a.shape[1], a.shape[2])
    wg_f, wu_f = stacked(wg4), stacked(wu4)
    p_act, q_act, f_act = _ffn_up(xn2, wg_f, wu_f)
    (wd4,) = _forward_pair(_gather_wait([3], g_ssem, g_rsem, lands[3:4], f_act, "down"), "down")
    wd_f = stacked(wd4)
    dff, dh2, loss_blk, d_gpf = _ffn_down(f_act, wd_f, h1, tgt2, post_ffn_norm)

    da, du = _ffn_bwd_act(dff, wd_f, p_act, q_act)
    by_chip = lambda g: g.reshape(N_CHIPS, g.shape[0] // N_CHIPS, g.shape[1])
    g_down = by_chip(_grad_w_down(f_act, dff))
    g_gate, g_up = [by_chip(g) for g in _grad_w_gate_up(xn2, da, du)]
    ffn = [g_gate, g_up, g_down]
    p_ssem, p_rsem, ffn, p_lands, p_token = _pair_exchange_start(ffn, [1, 1, 1], dff, "ffn")
    dh1, dmix, d_g2, d_gpm = _ffn_bwd_in(da, du, wg_f, wu_f, h1, mix, dh2, pre_ffn_norm, post_mix_norm, p_token)
    ffn, got = _pair_exchange_wait(p_ssem, p_rsem, ffn, p_lands, [1, 1, 1], d_g2, "ffn")
    parts = _pair_sum(ffn, got, c_arr, [False] * 3, "ffn")
    f_ssem, f_rsem, parts, f_lands, f_token = _chip_exchange_start(parts, dff, "ffn")
    g_out = _grad_w_out(ya, yb, dmix, f_token)
    dya, dz, d_lg, d_lb = _mix_bwd_out(dmix, wout_f, z, ln_b_gain, ln_b_bias, f_token)
    dhp5, d_wa, d_wb, d_bb = _mix_conv_bwd(hp5, dya, dz, wa_f, wb_f)
    g_in = _grad_w_in(xn1, dhp5)

    g_out4 = g_out.reshape(N_CHIPS, g_out.shape[0] // N_CHIPS, g_out.shape[1])
    mixw = [g_in, g_out4]
    got2 = _pair_exchange_grads(mixw, [0, 1], "mix")
    parts2 = _pair_sum(mixw, got2, c_arr, [True, False], "mix")
    m_ssem, m_rsem, parts2, m_lands, m_token = _chip_exchange_start(parts2, dhp5, "mix")
    grad_x2, d_meta, d_g1 = _mix_bwd_in(dhp5, win4, h, dh1, pre_mix_norm, m_token)
    grad_x = grad_x2[None]

    parts, f_recv = _chip_exchange_wait(f_ssem, f_rsem, parts, f_lands, d_g1, "ffn")
    halves = _chip_sum(parts, f_recv, qc_arr, "ffn")
    s_ssem, s_rsem, halves, s_token = _pair_share_start(halves, d_g1, "ffn")

    hw = d // 2
    assert d_wa.shape == (3, hw) and d_wb.shape == (31, hw) and d_bb.shape == (1, hw)
    small_parts = [d_meta, d_g1, d_gpm, d_g2, d_gpf, d_bb, d_lg, d_lb, loss_blk[0:1, :], d_wa, d_wb]
    places = [[(0, 0, 0, N_META)], [(16, 0, 0, 1)], [(17, 0, 0, 1)], [(18, 0, 0, 1)], [(19, 0, 0, 1)],
              [(20, 0, 0, 1)], [(20, hw, 0, 1)], [(21, 0, 0, 1)], [(21, hw, 0, 1)], [(22, 0, 0, 3)],
              [(22, hw, 0, 3), (25, 0, 3, 14), (25, hw, 17, 14)]]
    red = _small_allreduce(small_parts, places, SMALL_ROWS, d, s_token)
    gsum_ffn = _pair_share_wait(s_ssem, s_rsem, halves, red, "ffn")

    names_big = ["w_in", "w_out", "w_gate", "w_up", "w_down"]
    w_big = dict(zip(names_big, [w_in2, w_out2, w_gate2, w_up2, w_down2]))
    m_big = dict(zip(names_big, [m_w_in[0], m_w_out[0], tr(m_w_gate), tr(m_w_up), m_w_down[0]]))
    v_big = dict(zip(names_big, [v_w_in[0], v_w_out[0], tr(v_w_gate), tr(v_w_up), v_w_down[0]]))
    grads, deltas, new_m, new_v = {}, {}, {}, {}

    def update(names, gs, tag):
        res = _adamw_big([w_big[k] for k in names], gs, [m_big[k] for k in names], [v_big[k] for k in names], tag)
        for nm, outs in zip(names, res):
            if nm in ("w_gate", "w_up"):
                outs = [jnp.swapaxes(o[None], 1, 2) for o in outs]
            else:
                outs = [o[None] for o in outs]
            grads[nm], deltas[nm], new_m[nm], new_v[nm] = outs
        return res[-1][1]

    last = update(["w_gate", "w_up", "w_down"], list(gsum_ffn), "ffn")

    names_small = ["meta_tokens", "pre_mix_norm", "conv_a_w", "conv_b_w", "conv_b_bias", "ln_b_gain", "ln_b_bias",
                   "post_mix_norm", "pre_ffn_norm", "post_ffn_norm"]
    takes = [[(0, N_META, 0, 0, True)], [(0, 1, 16, 0, False)], [(0, 3, 22, 0, True)],
             [(0, 3, 22, hw, True), (3, 14, 25, 0, True), (17, 14, 25, hw, True)], [(0, 1, 20, 0, False)],
             [(0, 1, 20, hw, False)], [(0, 1, 21, 0, False)], [(0, 1, 17, 0, False)], [(0, 1, 18, 0, False)],
             [(0, 1, 19, 0, False)]]
    taps = lambda a: jnp.swapaxes(a, 0, 1)
    w_small = [meta_tokens, pre_mix_norm, taps(conv_a_w), taps(conv_b_w), conv_b_bias, ln_b_gain, ln_b_bias, post_mix_norm,
               pre_ffn_norm, post_ffn_norm]
    m_small = [m_meta_tokens, m_pre_mix_norm, taps(m_conv_a_w), taps(m_conv_b_w), m_conv_b_bias, m_ln_b_gain, m_ln_b_bias,
               m_post_mix_norm, m_pre_ffn_norm, m_post_ffn_norm]
    v_small = [v_meta_tokens, v_pre_mix_norm, taps(v_conv_a_w), taps(v_conv_b_w), v_conv_b_bias, v_ln_b_gain, v_ln_b_bias,
               v_post_mix_norm, v_pre_ffn_norm, v_post_ffn_norm]
    g_s, d_s, m_s, v_s, loss_row = _small_update(red, q_arr, takes, (21, hw), w_small, m_small, v_small)
    loss = loss_row[0, 0]
    for i, nm in enumerate(names_small):
        fix = taps if nm in ("conv_a_w", "conv_b_w") else (lambda a: a)
        grads[nm], deltas[nm], new_m[nm], new_v[nm] = fix(g_s[i]), fix(d_s[i]), fix(m_s[i]), fix(v_s[i])

    parts2, m_recv = _chip_exchange_wait(m_ssem, m_rsem, parts2, m_lands, last, "mix")
    halves2 = _chip_sum(parts2, m_recv, qc_arr, "mix")
    gsum_mix = _pair_share_grads(halves2, "mix")
    update(["w_in", "w_out"], list(gsum_mix), "mix")

    order = ["meta_tokens", "pre_mix_norm", "w_in", "conv_a_w", "conv_b_w", "conv_b_bias", "ln_b_gain", "ln_b_bias", "w_out",
             "post_mix_norm", "pre_ffn_norm", "w_gate", "w_up", "w_down", "post_ffn_norm"]
    return (loss, grad_x, *[grads[k] for k in order], *[deltas[k] for k in order], *[new_m[k] for k in order],
            *[new_v[k] for k in order])
```

```python
import jax
import jax.numpy as jnp
from jax import lax
from jax.experimental import pallas as pl
from jax.experimental.pallas import tpu as pltpu

F32 = jnp.float32
BF16 = jnp.bfloat16
MESH = pl.DeviceIdType.MESH

N_META = 16
TAIL_ROWS = 128
RMS_EPS = 1e-6
LN_EPS = 1e-5
ADAM_LR = 0.001
ADAM_B1 = 0.9
ADAM_B2 = 0.999
ADAM_EPS = 1e-08
ADAM_WD = 0.01
ADAM_STEP = 10

N_CHIPS = 4
LANES = 128
SUBLANES = 8
BF16_ROWS = 16
MXU_TILE = 256
CONV_CHUNK = 48
CONV_HIST = 32
ROW_TILE_CAP = 640
VMEM_LIMIT = 56 * 1024 * 1024

NN = (((1,), (0,)), ((), ()))
NT = (((1,), (1,)), ((), ()))
TN = (((0,), (0,)), ((), ()))


def _dot(a, b, dims=NN):
    return lax.dot_general(a, b, dims, preferred_element_type=F32)


def _sig(v):
    return 1.0 / (1.0 + jnp.exp(-v))


def _mean(v):
    return jnp.mean(v, axis=-1, keepdims=True)


def _row_tile(rows):
    best = BF16_ROWS
    for t in range(BF16_ROWS, min(rows, ROW_TILE_CAP) + 1, BF16_ROWS):
        if rows % t == 0:
            best = t
    assert rows % best == 0
    return best


def _row_parts(tm, parts=2):
    units = tm // BF16_ROWS
    if tm % BF16_ROWS or units < parts:
        return [slice(0, tm)]
    cuts = [BF16_ROWS * ((units * k + parts - 1) // parts) for k in range(parts + 1)]
    return [slice(lo, hi) for lo, hi in zip(cuts[:-1], cuts[1:])]


def _concat_shards(w_ref, wcat_ref):
    n_sh, _, csh = w_ref.shape

    @pl.when(pl.program_id(0) == 0)
    def _():
        for k in range(n_sh):
            wcat_ref[:, k * csh:(k + 1) * csh] = w_ref[k]


def _params(semantics=None):
    kw = dict(vmem_limit_bytes=VMEM_LIMIT)
    if semantics is not None:
        kw["dimension_semantics"] = semantics
    return pltpu.CompilerParams(**kw)


def _full(shape):
    nd = len(shape)
    return pl.BlockSpec(shape, lambda *_: (0,) * nd)


def _resident(shape):
    nd = len(shape)
    return pl.BlockSpec(shape, lambda *_: (0,) * nd, pipeline_mode=pl.Buffered(1))


def _sds(shape, dtype):
    return jax.ShapeDtypeStruct(shape, dtype)


ANY = pl.BlockSpec(memory_space=pl.ANY)
VMEM = pl.BlockSpec(memory_space=pltpu.VMEM)


def _mesh_pos():
    return lax.axis_index("x"), lax.axis_index("y"), lax.axis_index("c")


def _flip(v, bit):
    return 1 - v if bit else v


def _mm_in(x, small4, win4, g1, after):
    seq, d = x.shape
    tp = seq + TAIL_ROWS
    tm = _row_tile(tp)
    n_sh, _, csh = win4.shape
    pw = n_sh * csh // 5
    meta_off, meta_w = seq % tm, small4.shape[2]
    assert meta_off + N_META <= tm and seq // tm == tp // tm - 1 and N_CHIPS * meta_w == d

    def body(x_ref, small_ref, w_ref, g_ref, after_ref, h_ref, xn_ref, hp_ref, wcat_ref, tail_ref):
        _concat_shards(w_ref, wcat_ref)

        @pl.when(pl.program_id(0) == 0)
        def _():
            tail_ref[...] = jnp.zeros(tail_ref.shape, F32)
            for k in range(N_CHIPS):
                tail_ref[meta_off:meta_off + N_META, k * meta_w:(k + 1) * meta_w] = small_ref[k, 0:N_META, :]

        rows = pl.program_id(0) * tm + lax.broadcasted_iota(jnp.int32, (tm, 1), 0)
        hh = jnp.where(rows < seq, x_ref[...], tail_ref[...])
        h_ref[...] = hh
        r = lax.rsqrt(_mean(hh * hh) + RMS_EPS)
        xn = (hh * r * g_ref[...]).astype(BF16)
        xn_ref[...] = xn
        for p in range(5):
            hp_ref[p] = _dot(xn, wcat_ref[:, p * pw:(p + 1) * pw])

    row = pl.BlockSpec((tm, d), lambda i: (i, 0))
    return pl.pallas_call(
        body, name="mm_in", grid=(tp // tm,),
        in_specs=[row, _full(small4.shape), _resident(win4.shape), _full(g1.shape), ANY],
        out_specs=[row, row, pl.BlockSpec((5, tm, pw), lambda i: (0, i, 0))],
        out_shape=[_sds((tp, d), F32), _sds((tp, d), BF16), _sds((5, tp, pw), F32)],
        scratch_shapes=[pltpu.VMEM((d, n_sh * csh), BF16), pltpu.VMEM((tm, d), F32)],
        compiler_params=_params(("arbitrary",)),
    )(x, small4, win4, g1, after)


def _seq_rows(tp):
    seq = tp - TAIL_ROWS
    nseq = seq + N_META
    assert nseq % CONV_CHUNK == 0 and seq % BF16_ROWS == 0
    return seq, nseq


def _conv_offsets(width, transpose):
    return [(width - 1 - k) if transpose else (CONV_HIST - (width - 1) + k) for k in range(width)]


def _shift_copies(src_ref, sh_ref, width, transpose):
    n = src_ref.shape[0] - SUBLANES
    for s in sorted({o % SUBLANES for o in _conv_offsets(width, transpose)} - {0}):
        sh_ref[s - 1, 0:n, :] = src_ref[s:s + n, :]


def _tap_rows(src_ref, sh_ref, base, off):
    start = pl.multiple_of(base + (off // SUBLANES) * SUBLANES, SUBLANES)
    if off % SUBLANES == 0:
        return src_ref[pl.ds(start, CONV_CHUNK), :]
    return sh_ref[off % SUBLANES - 1, pl.ds(start, CONV_CHUNK), :]


def _conv_taps(src_ref, sh_ref, w, dst_ref, width, nseq, transpose, shifted=False):
    offs = _conv_offsets(width, transpose)
    if not shifted:
        _shift_copies(src_ref, sh_ref, width, transpose)

    def step(n, carry):
        out0 = pl.multiple_of(CONV_HIST + n * CONV_CHUNK, SUBLANES)
        base = out0 if transpose else n * CONV_CHUNK
        acc = jnp.zeros((CONV_CHUNK, w.shape[1]), F32)
        for k, off in enumerate(offs):
            acc = acc + w[k:k + 1, :] * _tap_rows(src_ref, sh_ref, base, off)
        dst_ref[pl.ds(out0, CONV_CHUNK), :] = acc
        return carry

    lax.fori_loop(0, nseq // CONV_CHUNK, step, 0)


def _conv_wgrad(src_ref, sh_ref, dz_ref, acc_ref, width, nseq):
    acc_ref[...] = jnp.zeros(acc_ref.shape, F32)
    offs = _conv_offsets(width, False)

    def step(n, carry):
        dzc = dz_ref[pl.ds(pl.multiple_of(CONV_HIST + n * CONV_CHUNK, SUBLANES), CONV_CHUNK), :]
        for k, off in enumerate(offs):
            prod = dzc * _tap_rows(src_ref, sh_ref, n * CONV_CHUNK, off)
            part = prod[0:SUBLANES, :]
            for s in range(1, CONV_CHUNK // SUBLANES):
                part = part + prod[SUBLANES * s:SUBLANES * (s + 1), :]
            acc_ref[SUBLANES * k:SUBLANES * (k + 1), :] += part
        return carry

    lax.fori_loop(0, nseq // CONV_CHUNK, step, 0)


def _conv_wgrad_by_dz_shifts(src_ref, dz_ref, shd_ref, acc_ref, width, nseq):
    acc_ref[...] = jnp.zeros(acc_ref.shape, F32)
    offs = _conv_offsets(width, True)

    def step(n, carry):
        base = pl.multiple_of(CONV_HIST + n * CONV_CHUNK, SUBLANES)
        rows = src_ref[pl.ds(base, CONV_CHUNK), :]
        for k, off in enumerate(offs):
            prod = rows * _tap_rows(dz_ref, shd_ref, base, off)
            part = prod[0:SUBLANES, :]
            for s in range(1, CONV_CHUNK // SUBLANES):
                part = part + prod[SUBLANES * s:SUBLANES * (s + 1), :]
            acc_ref[SUBLANES * k:SUBLANES * (k + 1), :] += part
        return carry

    lax.fori_loop(0, nseq // CONV_CHUNK, step, 0)


def _to_seq(buf_ref, x_part, meta_part, seq):
    buf_ref[CONV_HIST:CONV_HIST + N_META, :] = meta_part
    buf_ref[CONV_HIST + N_META:CONV_HIST + N_META + seq, :] = x_part


def _zero_ends(buf_ref, nseq):
    zeros = jnp.zeros((CONV_HIST, buf_ref.shape[1]), F32)
    buf_ref[0:CONV_HIST, :] = zeros
    buf_ref[CONV_HIST + nseq:CONV_HIST + nseq + CONV_HIST, :] = zeros


def _conv_weights(small_ref, ka, kb):
    return (small_ref[0, PACK_CONV_A_ROW:PACK_CONV_A_ROW + ka, 0:LANES],
            small_ref[0, PACK_CONV_B_ROW:PACK_CONV_B_ROW + kb, 0:LANES])


def _mix_conv_fwd(hp5, small4, ka, kb, bb):
    _, tp, wgrp = hp5.shape
    seq, nseq = _seq_rows(tp)
    sb = nseq + 2 * CONV_HIST
    assert wgrp // LANES == small4.shape[0]
    xs, ms = slice(0, seq), slice(seq, seq + N_META)
    ox, om = slice(CONV_HIST + N_META, CONV_HIST + nseq), slice(CONV_HIST, CONV_HIST + N_META)

    def body(hp_ref, small_ref, bb_ref, ya_ref, z_ref, s_ref, o_ref, sh_ref):
        wa, wb = _conv_weights(small_ref, ka, kb)
        _zero_ends(s_ref, nseq)
        _to_seq(s_ref, hp_ref[1, xs, :] * hp_ref[2, xs, :], hp_ref[1, ms, :] * hp_ref[2, ms, :], seq)
        _conv_taps(s_ref, sh_ref, wa, o_ref, ka, nseq, False)
        ya_ref[xs, :] = (hp_ref[0, xs, :] * o_ref[ox, :]).astype(BF16)
        ya_ref[ms, :] = (hp_ref[0, ms, :] * o_ref[om, :]).astype(BF16)
        ya_ref[seq + N_META:tp, :] = jnp.zeros((tp - seq - N_META, LANES), BF16)
        _to_seq(s_ref, hp_ref[3, xs, :] * _sig(hp_ref[4, xs, :]), hp_ref[3, ms, :] * _sig(hp_ref[4, ms, :]), seq)
        _conv_taps(s_ref, sh_ref, wb, o_ref, kb, nseq, False)
        z_ref[xs, :] = o_ref[ox, :] + bb_ref[...]
        z_ref[ms, :] = o_ref[om, :] + bb_ref[...]
        z_ref[seq + N_META:tp, :] = jnp.zeros((tp - seq - N_META, LANES), F32)

    col = lambda j: (0, j)
    return pl.pallas_call(
        body, name="mix_conv_fwd", grid=(wgrp // LANES,),
        in_specs=[pl.BlockSpec((5, tp, LANES), lambda j: (0, 0, j)),
                  pl.BlockSpec((1,) + small4.shape[1:], lambda j: (j, 0, 0)), pl.BlockSpec((1, LANES), col)],
        out_specs=[pl.BlockSpec((tp, LANES), col), pl.BlockSpec((tp, LANES), col)],
        out_shape=[_sds((tp, wgrp), BF16), _sds((tp, wgrp), F32)],
        scratch_shapes=[pltpu.VMEM((sb, LANES), F32), pltpu.VMEM((sb, LANES), F32),
                        pltpu.VMEM((SUBLANES - 1, sb, LANES), F32)],
        compiler_params=_params(("arbitrary",)),
    )(hp5, small4, bb)


def _layer_norm_parts(z, lg, lb):
    mu = _mean(z)
    zc = z - mu
    rl = lax.rsqrt(_mean(zc * zc) + LN_EPS)
    zh = zc * rl
    return rl, zh, zh * lg + lb


def _mm_out(ya, z, h, wout, lg, lb, gpm, g2):
    tp, d = h.shape
    wa_ = ya.shape[1]
    tm = _row_tile(tp)

    def body(ya_ref, z_ref, h_ref, w_ref, lg_ref, lb_ref, gpm_ref, g2_ref, yb_ref, mix_ref, h1_ref, xn2_ref):
        for rs in _row_parts(tm, 3):
            _, _, l = _layer_norm_parts(z_ref[rs, :], lg_ref[...], lb_ref[...])
            yb = (l * _sig(l)).astype(BF16)
            yb_ref[rs, :] = yb
            mix = _dot(ya_ref[rs, :], w_ref[0:wa_, :]) + _dot(yb, w_ref[wa_:d, :])
            mix_ref[rs, :] = mix
            rm = lax.rsqrt(_mean(mix * mix) + RMS_EPS)
            h1 = h_ref[rs, :] + mix * rm * gpm_ref[...]
            h1_ref[rs, :] = h1
            r2 = lax.rsqrt(_mean(h1 * h1) + RMS_EPS)
            xn2_ref[rs, :] = (h1 * r2 * g2_ref[...]).astype(BF16)

    row = lambda i: (i, 0)
    return pl.pallas_call(
        body, name="mm_out", grid=(tp // tm,),
        in_specs=[pl.BlockSpec((tm, wa_), row), pl.BlockSpec((tm, wa_), row), pl.BlockSpec((tm, d), row),
                  _resident(wout.shape), _full(lg.shape), _full(lb.shape), _full(gpm.shape), _full(g2.shape)],
        out_specs=[pl.BlockSpec((tm, wa_), row), pl.BlockSpec((tm, d), row), pl.BlockSpec((tm, d), row),
                   pl.BlockSpec((tm, d), row)],
        out_shape=[_sds((tp, wa_), BF16), _sds((tp, d), F32), _sds((tp, d), F32), _sds((tp, d), BF16)],
        compiler_params=_params(("arbitrary",)),
    )(ya, z, h, wout, lg, lb, gpm, g2)


def _ffn_up(xn2, wg, wu):
    tp, d = xn2.shape
    ff_dim = wg.shape[0]
    tm = _row_tile(tp)
    assert ff_dim % MXU_TILE == 0

    def body(xn_ref, wg_ref, wu_ref, p_ref, q_ref, f_ref):
        xn = xn_ref[...]
        for lo in range(0, ff_dim, MXU_TILE):
            cols = slice(lo, lo + MXU_TILE)
            a = _dot(xn, wg_ref[cols, :], NT)
            u = _dot(xn, wu_ref[cols, :], NT)
            s = _sig(a)
            q = a * s
            p_ref[:, cols] = (u * (s + q * (1.0 - s))).astype(BF16)
            q_ref[:, cols] = q.astype(BF16)
            f_ref[:, cols] = (q * u).astype(BF16)

    ospec = pl.BlockSpec((tm, ff_dim), lambda i: (i, 0))
    return pl.pallas_call(
        body, name="ffn_up", grid=(tp // tm,),
        in_specs=[pl.BlockSpec((tm, d), lambda i: (i, 0)), _resident(wg.shape), _resident(wu.shape)],
        out_specs=[ospec, ospec, ospec],
        out_shape=[_sds((tp, ff_dim), BF16)] * 3,
        compiler_params=_params(("arbitrary",)),
    )(xn2, wg, wu)


def _ffn_down(f, wd, h1, tgt, gpf):
    tp, ff_dim = f.shape
    d = h1.shape[1]
    tm = _row_tile(tp)
    seq, _ = _seq_rows(tp)

    def body(f_ref, w_ref, h1_ref, t_ref, gpf_ref, dff_ref, dh2_ref, loss_ref, dgpf_ref):
        i = pl.program_id(0)
        gpf_ = gpf_ref[...]

        @pl.when(i == 0)
        def _():
            loss_ref[...] = jnp.zeros(loss_ref.shape, F32)
            dgpf_ref[...] = jnp.zeros(dgpf_ref.shape, F32)

        for rs in _row_parts(tm):
            ff = _dot(f_ref[rs, :], w_ref[...])
            rf = lax.rsqrt(_mean(ff * ff) + RMS_EPS)
            nf = ff * rf
            h2 = h1_ref[rs, :] + nf * gpf_
            rows = i * tm + rs.start + lax.broadcasted_iota(jnp.int32, (rs.stop - rs.start, 1), 0)
            err = jnp.where(rows < seq, h2 - t_ref[rs, :], 0.0)
            dh2 = err * (1.0 / d)
            dh2_ref[rs, :] = dh2
            dn = dh2 * gpf_
            dff_ref[rs, :] = (rf * (dn - nf * _mean(dn * nf))).astype(BF16)
            loss_ref[...] += (0.5 / d) * jnp.sum(err * err, axis=(0, 1), keepdims=True)
            dgpf_ref[...] += jnp.sum(dh2 * nf, axis=0, keepdims=True)

    row = lambda i: (i, 0)
    return pl.pallas_call(
        body, name="ffn_down", grid=(tp // tm,),
        in_specs=[pl.BlockSpec((tm, ff_dim), row), _resident(wd.shape), pl.BlockSpec((tm, d), row),
                  pl.BlockSpec((tm, d), row), _full(gpf.shape)],
        out_specs=[pl.BlockSpec((tm, d), row), pl.BlockSpec((tm, d), row), _full((SUBLANES, LANES)), _full((1, d))],
        out_shape=[_sds((tp, d), BF16), _sds((tp, d), F32), _sds((SUBLANES, LANES), F32), _sds((1, d), F32)],
        compiler_params=_params(("arbitrary",)),
    )(f, wd, h1, tgt, gpf)


def _ffn_bwd_act(dff, wd, p, q):
    tp, d = dff.shape
    ff_dim = wd.shape[0]
    tm = _row_tile(tp)

    def body(dff_ref, w_ref, p_ref, q_ref, da_ref, du_ref):
        dffv = dff_ref[...]
        for lo in range(0, ff_dim, MXU_TILE):
            cols = slice(lo, lo + MXU_TILE)
            df = _dot(dffv, w_ref[cols, :], NT).astype(BF16)
            da_ref[:, cols] = df * p_ref[:, cols]
            du_ref[:, cols] = df * q_ref[:, cols]

    aspec = pl.BlockSpec((tm, ff_dim), lambda i: (i, 0))
    return pl.pallas_call(
        body, name="ffn_bwd_act", grid=(tp // tm,),
        in_specs=[pl.BlockSpec((tm, d), lambda i: (i, 0)), _resident(wd.shape), aspec, aspec],
        out_specs=[aspec, aspec],
        out_shape=[_sds((tp, ff_dim), BF16)] * 2,
        compiler_params=_params(("arbitrary",)),
    )(dff, wd, p, q)


def _grad_blocks(ff_dim):
    rows = ff_dim // 2
    assert rows % LANES == 0
    return rows


def _grad_w_down(f, dff):
    tp, ff_dim = f.shape
    d = dff.shape[1]
    rows = _grad_blocks(ff_dim)

    def body(f_ref, dff_ref, g_ref):
        g_ref[...] = _dot(f_ref[...], dff_ref[...], TN).astype(BF16)

    return pl.pallas_call(
        body, name="grad_w_down", grid=(ff_dim // rows,),
        in_specs=[pl.BlockSpec((tp, rows), lambda k: (0, k)), _resident(dff.shape)],
        out_specs=pl.BlockSpec((rows, d), lambda k: (k, 0)),
        out_shape=_sds((ff_dim, d), BF16),
        compiler_params=_params(("arbitrary",)),
    )(f, dff)


def _grad_w_gate_up(xn2, da, du):
    tp, ff_dim = da.shape
    d = xn2.shape[1]
    rows = _grad_blocks(ff_dim)

    def body(xn_ref, da_ref, du_ref, gg_ref, gu_ref):
        xn = xn_ref[...]
        gg_ref[...] = _dot(da_ref[...], xn, TN).astype(BF16)
        gu_ref[...] = _dot(du_ref[...], xn, TN).astype(BF16)

    aspec = pl.BlockSpec((tp, rows), lambda k: (0, k))
    gspec = pl.BlockSpec((rows, d), lambda k: (k, 0))
    return pl.pallas_call(
        body, name="grad_w_gate_up", grid=(ff_dim // rows,),
        in_specs=[_resident(xn2.shape), aspec, aspec],
        out_specs=[gspec, gspec],
        out_shape=[_sds((ff_dim, d), BF16)] * 2,
        compiler_params=_params(("arbitrary",)),
    )(xn2, da, du)


def _rms_bwd(dy, x, r, g):
    n = x * r
    dn = dy * g
    return r * (dn - n * _mean(dn * n)), dy * n


def _ffn_bwd_in(da, du, wg, wu, h1, mix, dh2, g2, gpm, after):
    tp, ff_dim = da.shape
    d = h1.shape[1]
    tm = _row_tile(tp)

    def body(da_ref, du_ref, wg_ref, wu_ref, h1_ref, mix_ref, dh2_ref, g2_ref, gpm_ref, after_ref,
             dh1_ref, dmix_ref, dg2_ref, dgpm_ref):
        i = pl.program_id(0)

        @pl.when(i == 0)
        def _():
            dg2_ref[...] = jnp.zeros(dg2_ref.shape, F32)
            dgpm_ref[...] = jnp.zeros(dgpm_ref.shape, F32)

        for rs in _row_parts(tm, 3):
            dxn = _dot(da_ref[rs, :], wg_ref[...]) + _dot(du_ref[rs, :], wu_ref[...])
            h1v = h1_ref[rs, :]
            r2 = lax.rsqrt(_mean(h1v * h1v) + RMS_EPS)
            dres, dg2_rows = _rms_bwd(dxn, h1v, r2, g2_ref[...])
            dh1 = dh2_ref[rs, :] + dres
            dh1_ref[rs, :] = dh1
            mixv = mix_ref[rs, :]
            rm = lax.rsqrt(_mean(mixv * mixv) + RMS_EPS)
            dmix, dgpm_rows = _rms_bwd(dh1, mixv, rm, gpm_ref[...])
            dmix_ref[rs, :] = dmix.astype(BF16)
            dg2_ref[...] += jnp.sum(dg2_rows, axis=0, keepdims=True)
            dgpm_ref[...] += jnp.sum(dgpm_rows, axis=0, keepdims=True)

    aspec = pl.BlockSpec((tm, ff_dim), lambda i: (i, 0))
    row = pl.BlockSpec((tm, d), lambda i: (i, 0))
    return pl.pallas_call(
        body, name="ffn_bwd_in", grid=(tp // tm,),
        in_specs=[aspec, aspec, _resident(wg.shape), _resident(wu.shape), row, row, row, _full(g2.shape), _full(gpm.shape),
                  ANY],
        out_specs=[row, row, _full((1, d)), _full((1, d))],
        out_shape=[_sds((tp, d), F32), _sds((tp, d), BF16), _sds((1, d), F32), _sds((1, d), F32)],
        compiler_params=_params(("arbitrary",)),
    )(da, du, wg, wu, h1, mix, dh2, g2, gpm, after)


def _grad_w_out(ya, yb, dmix, after):
    tp, wa_ = ya.shape
    d = dmix.shape[1]

    def body(ya_ref, yb_ref, dmix_ref, after_ref, g_ref):
        dm = dmix_ref[...]
        g_ref[0:wa_, :] = _dot(ya_ref[...], dm, TN).astype(BF16)
        g_ref[wa_:2 * wa_, :] = _dot(yb_ref[...], dm, TN).astype(BF16)

    return pl.pallas_call(
        body, name="grad_w_out", grid=(1,),
        in_specs=[_full(ya.shape), _full(yb.shape), _full(dmix.shape), ANY],
        out_specs=_full((2 * wa_, d)),
        out_shape=_sds((2 * wa_, d), BF16),
        compiler_params=_params(("arbitrary",)),
    )(ya, yb, dmix, after)


def _mix_bwd_out(dmix, wout, z, lg, lb, after):
    tp, d = dmix.shape
    wa_ = z.shape[1]
    tm = _row_tile(tp)

    def body(dmix_ref, w_ref, z_ref, lg_ref, lb_ref, after_ref, dya_ref, dz_ref, dlg_ref, dlb_ref):
        i = pl.program_id(0)
        lg_ = lg_ref[...]

        @pl.when(i == 0)
        def _():
            dlg_ref[...] = jnp.zeros(dlg_ref.shape, F32)
            dlb_ref[...] = jnp.zeros(dlb_ref.shape, F32)

        for rs in _row_parts(tm):
            dm = dmix_ref[rs, :]
            dya_ref[rs, :] = _dot(dm, w_ref[0:wa_, :], NT)
            dyb = _dot(dm, w_ref[wa_:d, :], NT)
            rl, zh, l = _layer_norm_parts(z_ref[rs, :], lg_, lb_ref[...])
            sl = _sig(l)
            dl = dyb * (sl * (1.0 + l * (1.0 - sl)))
            dzh = dl * lg_
            dz_ref[rs, :] = rl * (dzh - _mean(dzh) - zh * _mean(dzh * zh))
            dlg_ref[...] += jnp.sum(dl * zh, axis=0, keepdims=True)
            dlb_ref[...] += jnp.sum(dl, axis=0, keepdims=True)

    row = lambda i: (i, 0)
    return pl.pallas_call(
        body, name="mix_bwd_out", grid=(tp // tm,),
        in_specs=[pl.BlockSpec((tm, d), row), _resident(wout.shape), pl.BlockSpec((tm, wa_), row), _full(lg.shape),
                  _full(lb.shape), ANY],
        out_specs=[pl.BlockSpec((tm, wa_), row), pl.BlockSpec((tm, wa_), row), _full((1, wa_)), _full((1, wa_))],
        out_shape=[_sds((tp, wa_), F32), _sds((tp, wa_), F32), _sds((1, wa_), F32), _sds((1, wa_), F32)],
        compiler_params=_params(("arbitrary",)),
    )(dmix, wout, z, lg, lb, after)


def _mix_conv_bwd(hp5, dya, dz, small4, ka, kb):
    _, tp, wgrp = hp5.shape
    seq, nseq = _seq_rows(tp)
    sb = nseq + 2 * CONV_HIST
    assert wgrp // LANES == small4.shape[0]
    xs, ms = slice(0, seq), slice(seq, seq + N_META)
    ox, om = slice(CONV_HIST + N_META, CONV_HIST + nseq), slice(CONV_HIST, CONV_HIST + N_META)
    n_tail = tp - seq - N_META

    def body(hp_ref, dya_ref, dz_ref, small_ref, dhp_ref, dwa_ref, dwb_ref, dbb_ref, s_ref, d_ref, o_ref, acc_ref,
             shs_ref, shd_ref):
        wa, wb = _conv_weights(small_ref, ka, kb)
        _zero_ends(s_ref, nseq)
        _zero_ends(d_ref, nseq)

        def put(p, ox_val, om_val):
            dhp_ref[p, xs, :] = ox_val.astype(BF16)
            dhp_ref[p, ms, :] = om_val.astype(BF16)
            dhp_ref[p, seq + N_META:tp, :] = jnp.zeros((n_tail, LANES), BF16)

        def wgrad(dw_ref, width):
            for k in range(width):
                dw_ref[k:k + 1, :] = jnp.sum(acc_ref[8 * k:8 * k + 8, :], axis=0, keepdims=True)

        _to_seq(s_ref, hp_ref[1, xs, :] * hp_ref[2, xs, :], hp_ref[1, ms, :] * hp_ref[2, ms, :], seq)
        _conv_taps(s_ref, shs_ref, wa, o_ref, ka, nseq, False)
        put(0, dya_ref[xs, :] * o_ref[ox, :], dya_ref[ms, :] * o_ref[om, :])
        _to_seq(d_ref, dya_ref[xs, :] * hp_ref[0, xs, :], dya_ref[ms, :] * hp_ref[0, ms, :], seq)
        _conv_wgrad(s_ref, shs_ref, d_ref, acc_ref, ka, nseq)
        wgrad(dwa_ref, ka)
        _conv_taps(d_ref, shd_ref, wa, o_ref, ka, nseq, True)
        put(1, o_ref[ox, :] * hp_ref[2, xs, :], o_ref[om, :] * hp_ref[2, ms, :])
        put(2, o_ref[ox, :] * hp_ref[1, xs, :], o_ref[om, :] * hp_ref[1, ms, :])

        _to_seq(s_ref, hp_ref[3, xs, :] * _sig(hp_ref[4, xs, :]), hp_ref[3, ms, :] * _sig(hp_ref[4, ms, :]), seq)
        _to_seq(d_ref, dz_ref[xs, :], dz_ref[ms, :], seq)
        dbb_ref[...] = (jnp.sum(dz_ref[xs, :], axis=0, keepdims=True)
                        + jnp.sum(dz_ref[ms, :], axis=0, keepdims=True))
        _shift_copies(d_ref, shd_ref, kb, True)
        _conv_wgrad_by_dz_shifts(s_ref, d_ref, shd_ref, acc_ref, kb, nseq)
        wgrad(dwb_ref, kb)
        _conv_taps(d_ref, shd_ref, wb, o_ref, kb, nseq, True, shifted=True)
        sx, sm = _sig(hp_ref[4, xs, :]), _sig(hp_ref[4, ms, :])
        put(3, o_ref[ox, :] * sx, o_ref[om, :] * sm)
        put(4, o_ref[ox, :] * hp_ref[3, xs, :] * sx * (1.0 - sx), o_ref[om, :] * hp_ref[3, ms, :] * sm * (1.0 - sm))

    col = lambda j: (0, j)
    blk5 = pl.BlockSpec((5, tp, LANES), lambda j: (0, 0, j))
    return pl.pallas_call(
        body, name="mix_conv_bwd", grid=(wgrp // LANES,),
        in_specs=[blk5, pl.BlockSpec((tp, LANES), col), pl.BlockSpec((tp, LANES), col),
                  pl.BlockSpec((1,) + small4.shape[1:], lambda j: (j, 0, 0))],
        out_specs=[blk5, pl.BlockSpec((ka, LANES), col), pl.BlockSpec((kb, LANES), col), pl.BlockSpec((1, LANES), col)],
        out_shape=[_sds((5, tp, wgrp), BF16), _sds((ka, wgrp), F32), _sds((kb, wgrp), F32), _sds((1, wgrp), F32)],
        scratch_shapes=[pltpu.VMEM((sb, LANES), F32), pltpu.VMEM((sb, LANES), F32), pltpu.VMEM((sb, LANES), F32),
                        pltpu.VMEM((SUBLANES * kb, LANES), F32), pltpu.VMEM((SUBLANES - 1, sb, LANES), F32),
                        pltpu.VMEM((SUBLANES - 1, sb, LANES), F32)],
        compiler_params=_params(("arbitrary",)),
    )(hp5, dya, dz, small4)


def _grad_w_in(xn1, dhp5):
    n_p, tp, pw = dhp5.shape
    d = xn1.shape[1]

    def body(xn_ref, dhp_ref, g_ref):
        g_ref[...] = _dot(xn_ref[...], dhp_ref[0], TN).astype(BF16)

    return pl.pallas_call(
        body, name="grad_w_in", grid=(n_p,),
        in_specs=[_resident(xn1.shape), pl.BlockSpec((1, tp, pw), lambda p: (p, 0, 0))],
        out_specs=pl.BlockSpec((d, pw), lambda p: (0, p)),
        out_shape=_sds((d, n_p * pw), BF16),
        compiler_params=_params(("arbitrary",)),
    )(xn1, dhp5)


def _mix_bwd_in(dhp5, win4, h, dh1, g1, after):
    n_p, tp, pw = dhp5.shape
    d = h.shape[1]
    n_sh, _, csh = win4.shape
    tm = _row_tile(tp)

    seq, _ = _seq_rows(tp)
    last, meta_off = seq // tm, seq % tm
    assert last == tp // tm - 1
    assert any(rs.start <= meta_off and meta_off + N_META <= rs.stop for rs in _row_parts(tm))

    def body(dhp_ref, w_ref, h_ref, dh1_ref, g_ref, after_ref, gx_ref, dmeta_ref, dg1_ref, wcat_ref):
        i = pl.program_id(0)
        _concat_shards(w_ref, wcat_ref)

        @pl.when(i == 0)
        def _():
            dg1_ref[...] = jnp.zeros(dg1_ref.shape, F32)

        for rs in _row_parts(tm):
            dxn = _dot(dhp_ref[0, rs, :], wcat_ref[:, 0:pw], NT)
            for p in range(1, n_p):
                dxn = dxn + _dot(dhp_ref[p, rs, :], wcat_ref[:, p * pw:(p + 1) * pw], NT)
            hh = h_ref[rs, :]
            r1 = lax.rsqrt(_mean(hh * hh) + RMS_EPS)
            dres, dg_rows = _rms_bwd(dxn, hh, r1, g_ref[...])
            dh = dh1_ref[rs, :] + dres
            gx_ref[rs, :] = dh
            dg1_ref[...] += jnp.sum(dg_rows, axis=0, keepdims=True)
            if rs.start <= meta_off and meta_off + N_META <= rs.stop:
                @pl.when(i == last)
                def _():
                    dmeta_ref[...] = dh[meta_off - rs.start:meta_off - rs.start + N_META, :]

    row = lambda i: (i, 0)
    return pl.pallas_call(
        body, name="mix_bwd_in", grid=(tp // tm,),
        in_specs=[pl.BlockSpec((n_p, tm, pw), lambda i: (0, i, 0)), _resident(win4.shape), pl.BlockSpec((tm, d), row),
                  pl.BlockSpec((tm, d), row), _full(g1.shape), ANY],
        out_specs=[pl.BlockSpec((tm, d), row), _full((N_META, d)), _full((1, d))],
        out_shape=[_sds((seq, d), F32), _sds((N_META, d), F32), _sds((1, d), F32)],
        scratch_shapes=[pltpu.VMEM((d, n_sh * csh), BF16)],
        compiler_params=_params(("arbitrary",)),
    )(dhp5, win4, h, dh1, g1, after)


def _other_chips(x, y):
    out = []
    for j in (1, 2, 3):
        px, py = _flip(x, j >> 1), _flip(y, j & 1)
        out.append((px, py, 2 * px + py))
    return out


PAIR_COLLECTIVE_ID = 0


def _pair_barrier(x, y, c):
    sem = pltpu.get_barrier_semaphore()
    pl.semaphore_signal(sem, inc=1, device_id=(x, y, 1 - c), device_id_type=MESH)
    pl.semaphore_wait(sem, 1)


def _pair_params():
    return pltpu.CompilerParams(collective_id=PAIR_COLLECTIVE_ID)


def _half_rows(c, rows_half):
    return pl.ds(pl.multiple_of(c * rows_half, SUBLANES), rows_half)


def _cast_place(ws, q_arr, tag, after=None):
    n = len(ws)
    extra = [] if after is None else [after]

    def fits(steps):
        return all(w.shape[0] % steps == 0 and (w.shape[0] // steps) % BF16_ROWS == 0
                   and w.shape[0] // steps <= ROW_TILE_CAP for w in ws)

    steps = next(s for s in range(1, min(w.shape[0] for w in ws) + 1) if fits(s))

    def body(q_ref, *refs):
        for w_ref, out_ref in zip(refs[:n], refs[n + len(extra):]):
            out_ref[0] = w_ref[...].astype(BF16)

    return list(pl.pallas_call(
        body, name="cast_place_" + tag,
        grid_spec=pltpu.PrefetchScalarGridSpec(
            num_scalar_prefetch=1, grid=(steps,),
            in_specs=[pl.BlockSpec((w.shape[0] // steps, w.shape[1]), lambda i, q: (i, 0)) for w in ws] + [ANY] * len(extra),
            out_specs=[pl.BlockSpec((1, w.shape[0] // steps, w.shape[1]), lambda i, q: (q[0], i, 0)) for w in ws]),
        out_shape=[_sds((N_CHIPS,) + w.shape, BF16) for w in ws],
        compiler_params=_params(("arbitrary",)),
    )(q_arr, *ws, *extra))


HBM = pl.BlockSpec(memory_space=pltpu.HBM)
SEM = pl.BlockSpec(memory_space=pltpu.SEMAPHORE)
EFFECT = pltpu.SideEffectType.DATAFLOW_SIDE_EFFECTING


def _in_hbm(a):
    return pltpu.with_memory_space_constraint(a, pltpu.HBM)


def _gather_start(fulls, after, tag):
    n = len(fulls)
    halves = [a.shape[1] // 2 for a in fulls]

    def body(*refs):
        land = refs[:n]
        ssem, rsem = refs[n + 1], refs[n + 2]
        token = refs[-1]
        x, y, c = _mesh_pos()
        q = 2 * x + y
        for i in range(n):
            for j, (px, py, _) in enumerate(_other_chips(x, y)):
                mine = land[i].at[q, _half_rows(c, halves[i]), :]
                pltpu.make_async_remote_copy(src_ref=mine, dst_ref=mine, send_sem=ssem.at[3 * i + j],
                                             recv_sem=rsem.at[3 * i + j], device_id=(px, py, c), device_id_type=MESH).start()
        token[...] = jnp.zeros(token.shape, F32)

    outs = pl.pallas_call(
        body, name="gather_start_" + tag,
        in_specs=[HBM] * n + [ANY], out_specs=[SEM, SEM] + [HBM] * n + [VMEM],
        out_shape=[pltpu.SemaphoreType.DMA((3 * n,)), pltpu.SemaphoreType.DMA((3 * n,))]
        + [pltpu.HBM(a.shape, a.dtype) for a in fulls] + [_sds((SUBLANES, LANES), F32)],
        input_output_aliases={i: 2 + i for i in range(n)},
        compiler_params=pltpu.CompilerParams(has_side_effects=EFFECT),
    )(*[_in_hbm(a) for a in fulls], after)
    return outs[0], outs[1], list(outs[2:2 + n]), outs[-1]


def _gather_wait(which, ssem, rsem, lands, after, tag):
    m = len(which)
    halves = [a.shape[1] // 2 for a in lands]

    def body(*refs):
        land = refs[:m]
        ssem_, rsem_ = refs[m], refs[m + 1]
        x, y, c = _mesh_pos()
        for t, i in enumerate(which):
            for j, (px, py, qj) in enumerate(_other_chips(x, y)):
                rows = _half_rows(c, halves[t])
                cp = pltpu.make_async_remote_copy(src_ref=land[t].at[2 * x + y, rows, :], dst_ref=land[t].at[qj, rows, :],
                                                  send_sem=ssem_.at[3 * i + j], recv_sem=rsem_.at[3 * i + j],
                                                  device_id=(px, py, c), device_id_type=MESH)
                cp.wait_send()
                cp.wait_recv()

    outs = pl.pallas_call(
        body, name="gather_wait_" + tag,
        in_specs=[HBM] * m + [SEM, SEM, ANY], out_specs=[HBM] * m,
        out_shape=[pltpu.HBM(a.shape, a.dtype) for a in lands],
        input_output_aliases={i: i for i in range(m)},
        compiler_params=pltpu.CompilerParams(has_side_effects=EFFECT),
    )(*lands, ssem, rsem, after)
    return list(outs)


def _forward_pair(lands, tag):
    n = len(lands)
    halves = [a.shape[1] // 2 for a in lands]

    def body(*refs):
        full = refs[n:2 * n]
        ssem, rsem = refs[2 * n:]
        x, y, c = _mesh_pos()
        _pair_barrier(x, y, c)
        cps = []
        for i in range(n):
            for j, (_, _, qj) in enumerate(_other_chips(x, y)):
                part = full[i].at[qj, _half_rows(c, halves[i]), :]
                cp = pltpu.make_async_remote_copy(src_ref=part, dst_ref=part, send_sem=ssem.at[3 * i + j],
                                                  recv_sem=rsem.at[3 * i + j], device_id=(x, y, 1 - c), device_id_type=MESH)
                cp.start()
                cps.append(cp)
        for cp in cps:
            cp.wait()

    return pl.pallas_call(
        body, name="forward_pair_" + tag,
        in_specs=[ANY] * n, out_specs=[ANY] * n,
        out_shape=[_sds(a.shape, a.dtype) for a in lands],
        input_output_aliases={i: i for i in range(n)},
        scratch_shapes=[pltpu.SemaphoreType.DMA((3 * n,)), pltpu.SemaphoreType.DMA((3 * n,))],
        compiler_params=_pair_params(),
    )(*lands)


def _chip_exchange_start(parts, after, tag):
    n = len(parts)

    def body(*refs):
        src, land = refs[:n], refs[n:2 * n]
        ssem, rsem = refs[2 * n + 1], refs[2 * n + 2]
        token = refs[-1]
        x, y, c = _mesh_pos()
        for i in range(n):
            for j, (px, py, qj) in enumerate(_other_chips(x, y)):
                pltpu.make_async_remote_copy(src_ref=src[i].at[qj], dst_ref=land[i].at[j], send_sem=ssem.at[3 * i + j],
                                             recv_sem=rsem.at[3 * i + j], device_id=(px, py, c), device_id_type=MESH).start()
        token[...] = jnp.zeros(token.shape, F32)

    lands = [lax.empty((3,) + a.shape[1:], a.dtype) for a in parts]
    outs = pl.pallas_call(
        body, name="chip_exchange_start_" + tag,
        in_specs=[HBM] * (2 * n) + [ANY], out_specs=[SEM, SEM] + [HBM] * (2 * n) + [VMEM],
        out_shape=[pltpu.SemaphoreType.DMA((3 * n,)), pltpu.SemaphoreType.DMA((3 * n,))]
        + [pltpu.HBM(a.shape, a.dtype) for a in parts] + [pltpu.HBM(a.shape, a.dtype) for a in lands]
        + [_sds((SUBLANES, LANES), F32)],
        input_output_aliases={i: 2 + i for i in range(2 * n)},
        compiler_params=pltpu.CompilerParams(has_side_effects=EFFECT),
    )(*[_in_hbm(a) for a in parts], *[_in_hbm(a) for a in lands], after)
    return outs[0], outs[1], list(outs[2:2 + n]), list(outs[2 + n:2 + 2 * n]), outs[-1]


def _chip_exchange_wait(ssem, rsem, parts, lands, after, tag):
    n = len(parts)

    def body(*refs):
        src, land = refs[:n], refs[n:2 * n]
        ssem_, rsem_ = refs[2 * n], refs[2 * n + 1]
        x, y, c = _mesh_pos()
        for i in range(n):
            for j, (px, py, qj) in enumerate(_other_chips(x, y)):
                cp = pltpu.make_async_remote_copy(src_ref=src[i].at[qj], dst_ref=land[i].at[j], send_sem=ssem_.at[3 * i + j],
                                                  recv_sem=rsem_.at[3 * i + j], device_id=(px, py, c), device_id_type=MESH)
                cp.wait_send()
                cp.wait_recv()

    outs = pl.pallas_call(
        body, name="chip_exchange_wait_" + tag,
        in_specs=[HBM] * (2 * n) + [SEM, SEM, ANY], out_specs=[HBM] * (2 * n),
        out_shape=[pltpu.HBM(a.shape, a.dtype) for a in parts] + [pltpu.HBM(a.shape, a.dtype) for a in lands],
        input_output_aliases={i: i for i in range(2 * n)},
        compiler_params=pltpu.CompilerParams(has_side_effects=EFFECT),
    )(*parts, *lands, ssem, rsem, after)
    return list(outs[:n]), list(outs[n:])


def _grad_half(ref, shape, axis, which):
    rows = shape[axis] // 2
    if axis == 0:
        return ref.at[_half_rows(which, rows), :]
    return ref.at[:, _half_rows(which, rows), :]


def _half_shape(a, axis):
    s = list(a.shape)
    s[axis] //= 2
    return tuple(s)


def _pair_exchange_start(grads, half_axis, after, tag):
    n = len(grads)

    def body(*refs):
        g, land = refs[:n], refs[n:2 * n]
        ssem, rsem = refs[2 * n + 1], refs[2 * n + 2]
        token = refs[-1]
        x, y, c = _mesh_pos()
        for i in range(n):
            pltpu.make_async_remote_copy(src_ref=_grad_half(g[i], grads[i].shape, half_axis[i], 1 - c), dst_ref=land[i],
                                         send_sem=ssem.at[i], recv_sem=rsem.at[i], device_id=(x, y, 1 - c),
                                         device_id_type=MESH).start()
        token[...] = jnp.zeros(token.shape, F32)

    lands = [lax.empty(_half_shape(a, half_axis[i]), a.dtype) for i, a in enumerate(grads)]
    outs = pl.pallas_call(
        body, name="pair_exchange_start_" + tag,
        in_specs=[HBM] * (2 * n) + [ANY], out_specs=[SEM, SEM] + [HBM] * (2 * n) + [VMEM],
        out_shape=[pltpu.SemaphoreType.DMA((n,)), pltpu.SemaphoreType.DMA((n,))]
        + [pltpu.HBM(a.shape, a.dtype) for a in grads] + [pltpu.HBM(a.shape, a.dtype) for a in lands]
        + [_sds((SUBLANES, LANES), F32)],
        input_output_aliases={i: 2 + i for i in range(2 * n)},
        compiler_params=pltpu.CompilerParams(has_side_effects=EFFECT),
    )(*[_in_hbm(a) for a in grads], *[_in_hbm(a) for a in lands], after)
    return outs[0], outs[1], list(outs[2:2 + n]), list(outs[2 + n:2 + 2 * n]), outs[-1]


def _pair_exchange_wait(ssem, rsem, grads, lands, half_axis, after, tag):
    n = len(grads)

    def body(*refs):
        g, land = refs[:n], refs[n:2 * n]
        ssem_, rsem_ = refs[2 * n], refs[2 * n + 1]
        x, y, c = _mesh_pos()
        for i in range(n):
            cp = pltpu.make_async_remote_copy(src_ref=_grad_half(g[i], grads[i].shape, half_axis[i], 1 - c),
                                              dst_ref=land[i], send_sem=ssem_.at[i], recv_sem=rsem_.at[i],
                                              device_id=(x, y, 1 - c), device_id_type=MESH)
            cp.wait_send()
            cp.wait_recv()

    outs = pl.pallas_call(
        body, name="pair_exchange_wait_" + tag,
        in_specs=[HBM] * (2 * n) + [SEM, SEM, ANY], out_specs=[HBM] * (2 * n),
        out_shape=[pltpu.HBM(a.shape, a.dtype) for a in grads] + [pltpu.HBM(a.shape, a.dtype) for a in lands],
        input_output_aliases={i: i for i in range(2 * n)},
        compiler_params=pltpu.CompilerParams(has_side_effects=EFFECT),
    )(*grads, *lands, ssem, rsem, after)
    return list(outs[:n]), list(outs[n:])


def _pair_exchange_grads(grads, half_axis, tag):
    n = len(grads)

    def body(*refs):
        g, got = refs[:n], refs[n:2 * n]
        ssem, rsem = refs[2 * n:]
        x, y, c = _mesh_pos()
        _pair_barrier(x, y, c)
        cps = []
        for i in range(n):
            cp = pltpu.make_async_remote_copy(src_ref=_grad_half(g[i], grads[i].shape, half_axis[i], 1 - c),
                                              dst_ref=got[i], send_sem=ssem.at[i], recv_sem=rsem.at[i],
                                              device_id=(x, y, 1 - c), device_id_type=MESH)
            cp.start()
            cps.append(cp)
        for cp in cps:
            cp.wait()

    return pl.pallas_call(
        body, name="pair_exchange_grads_" + tag,
        in_specs=[ANY] * n, out_specs=[ANY] * n,
        out_shape=[_sds(_half_shape(a, half_axis[i]), a.dtype) for i, a in enumerate(grads)],
        scratch_shapes=[pltpu.SemaphoreType.DMA((n,)), pltpu.SemaphoreType.DMA((n,))],
        compiler_params=_pair_params(),
    )(*grads)


def _pair_sum(gs, gots, c_arr, col_sharded, tag):
    n = len(gs)
    g_specs, got_specs, out_specs, out_shapes = [], [], [], []
    for g, by_cols in zip(gs, col_sharded):
        if by_cols:
            rows, cols = g.shape
            rh, cs = rows // 2, cols // N_CHIPS
            g_specs.append(pl.BlockSpec((rh, cs), lambda k, c_ref: (c_ref[0], k)))
            got_specs.append(pl.BlockSpec((rh, cs), lambda k, c_ref: (0, k)))
        else:
            _, rows, cs = g.shape
            rh = rows // 2
            g_specs.append(pl.BlockSpec((1, rh, cs), lambda k, c_ref: (k, c_ref[0], 0)))
            got_specs.append(pl.BlockSpec((1, rh, cs), lambda k, c_ref: (k, 0, 0)))
        out_specs.append(pl.BlockSpec((1, rh, cs), lambda k, c_ref: (k, 0, 0)))
        out_shapes.append(_sds((N_CHIPS, rh, cs), BF16))

    def body(c_ref, *refs):
        for g_ref, got_ref, out_ref in zip(refs[:n], refs[n:2 * n], refs[2 * n:]):
            total = g_ref[...].astype(F32) + got_ref[...].astype(F32)
            out_ref[...] = total.astype(BF16).reshape(out_ref.shape)

    return list(pl.pallas_call(
        body, name="pair_sum_" + tag,
        grid_spec=pltpu.PrefetchScalarGridSpec(
            num_scalar_prefetch=1, grid=(N_CHIPS,), in_specs=g_specs + got_specs, out_specs=out_specs),
        out_shape=out_shapes,
        compiler_params=_params(("arbitrary",)),
    )(c_arr, *gs, *gots))


def _chip_sum(parts, gots, qc_arr, tag):
    n = len(parts)
    steps = 2 if all(p.shape[1] % 32 == 0 for p in parts) else 1
    part_specs, got_specs, out_specs, out_shapes = [], [], [], []
    for p in parts:
        _, rh, cs = p.shape
        rb = rh // steps
        part_specs.append(pl.BlockSpec((1, rb, cs), lambda i, qc: (qc[0], i, 0)))
        got_specs.append(pl.BlockSpec((3, rb, cs), lambda i, qc: (0, i, 0)))
        out_specs.append(pl.BlockSpec((rb, cs), lambda i, qc: (qc[1] * steps + i, 0)))
        out_shapes.append(_sds((2 * rh, cs), F32))

    def body(qc_ref, *refs):
        for part_ref, got_ref, out_ref in zip(refs[:n], refs[n:2 * n], refs[2 * n:]):
            total = part_ref[0].astype(F32)
            for j in range(3):
                total = total + got_ref[j].astype(F32)
            out_ref[...] = total

    return list(pl.pallas_call(
        body, name="chip_sum_" + tag,
        grid_spec=pltpu.PrefetchScalarGridSpec(
            num_scalar_prefetch=1, grid=(steps,), in_specs=part_specs + got_specs, out_specs=out_specs),
        out_shape=out_shapes,
        compiler_params=_params(("arbitrary",)),
    )(qc_arr, *parts, *gots))


def _pair_share_grads(grads, tag):
    n = len(grads)

    def body(*refs):
        g = refs[n:2 * n]
        ssem, rsem = refs[2 * n:]
        x, y, c = _mesh_pos()
        _pair_barrier(x, y, c)
        cps = []
        for i in range(n):
            mine = g[i].at[_half_rows(c, grads[i].shape[0] // 2), :]
            cp = pltpu.make_async_remote_copy(src_ref=mine, dst_ref=mine, send_sem=ssem.at[i], recv_sem=rsem.at[i],
                                              device_id=(x, y, 1 - c), device_id_type=MESH)
            cp.start()
            cps.append(cp)
        for cp in cps:
            cp.wait()

    return pl.pallas_call(
        body, name="pair_share_grads_" + tag,
        in_specs=[ANY] * n, out_specs=[ANY] * n,
        out_shape=[_sds(a.shape, a.dtype) for a in grads],
        input_output_aliases={i: i for i in range(n)},
        scratch_shapes=[pltpu.SemaphoreType.DMA((n,)), pltpu.SemaphoreType.DMA((n,))],
        compiler_params=_pair_params(),
    )(*grads)


def _pair_share_start(grads, after, tag):
    n = len(grads)

    def body(*refs):
        g = refs[:n]
        ssem, rsem = refs[n + 1], refs[n + 2]
        token = refs[-1]
        x, y, c = _mesh_pos()
        for i in range(n):
            mine = g[i].at[_half_rows(c, grads[i].shape[0] // 2), :]
            pltpu.make_async_remote_copy(src_ref=mine, dst_ref=mine, send_sem=ssem.at[i], recv_sem=rsem.at[i],
                                         device_id=(x, y, 1 - c), device_id_type=MESH).start()
        token[...] = jnp.zeros(token.shape, F32)

    outs = pl.pallas_call(
        body, name="pair_share_start_" + tag,
        in_specs=[HBM] * n + [ANY], out_specs=[SEM, SEM] + [HBM] * n + [VMEM],
        out_shape=[pltpu.SemaphoreType.DMA((n,)), pltpu.SemaphoreType.DMA((n,))]
        + [pltpu.HBM(a.shape, a.dtype) for a in grads] + [_sds((SUBLANES, LANES), F32)],
        input_output_aliases={i: 2 + i for i in range(n)},
        compiler_params=pltpu.CompilerParams(has_side_effects=EFFECT),
    )(*[_in_hbm(a) for a in grads], after)
    return outs[0], outs[1], list(outs[2:2 + n]), outs[-1]


def _pair_share_wait(ssem, rsem, grads, after, tag):
    n = len(grads)

    def body(*refs):
        g = refs[:n]
        ssem_, rsem_ = refs[n], refs[n + 1]
        x, y, c = _mesh_pos()
        for i in range(n):
            rows = grads[i].shape[0] // 2
            cp = pltpu.make_async_remote_copy(src_ref=g[i].at[_half_rows(c, rows), :], dst_ref=g[i].at[_half_rows(1 - c, rows), :],
                                              send_sem=ssem_.at[i], recv_sem=rsem_.at[i], device_id=(x, y, 1 - c),
                                              device_id_type=MESH)
            cp.wait_send()
            cp.wait_recv()

    outs = pl.pallas_call(
        body, name="pair_share_wait_" + tag,
        in_specs=[HBM] * n + [SEM, SEM, ANY], out_specs=[HBM] * n,
        out_shape=[pltpu.HBM(a.shape, a.dtype) for a in grads],
        input_output_aliases={i: i for i in range(n)},
        compiler_params=pltpu.CompilerParams(has_side_effects=EFFECT),
    )(*grads, ssem, rsem, after)
    return list(outs)


def _small_allreduce(parts, places, rows_total, width, after):
    n = len(parts)

    def body(*refs):
        ins, out_ref = refs[:n], refs[n + 1]
        pack, pair_got, chip_sum, got, ssem, rsem = refs[n + 2:]
        x, y, c = _mesh_pos()
        chip = 2 * x + y
        pack[...] = jnp.zeros(pack.shape, F32)
        for i in range(n):
            for row, col, src_row, rows in places[i]:
                w = parts[i].shape[1]
                pack[row:row + rows, col:col + w] = ins[i][src_row:src_row + rows, :]
        swap = pltpu.make_async_remote_copy(src_ref=pack, dst_ref=pair_got, send_sem=ssem.at[3], recv_sem=rsem.at[3],
                                            device_id=(x, y, 1 - c), device_id_type=MESH)
        swap.start()
        swap.wait()
        chip_sum[...] = pack[...] + pair_got[...]
        cps = []
        for j, (px, py, _) in enumerate(_other_chips(x, y)):
            cp = pltpu.make_async_remote_copy(src_ref=chip_sum, dst_ref=got.at[j], send_sem=ssem.at[j],
                                              recv_sem=rsem.at[j], device_id=(px, py, c), device_id_type=MESH)
            cp.start()
            cps.append(cp)
        for cp in cps:
            cp.wait()
        total = jnp.zeros(pack.shape, F32)
        for q in range(N_CHIPS):
            rel = jnp.bitwise_xor(chip, q)
            theirs = got[jnp.maximum(rel - 1, 0)]
            total = total + jnp.where(rel == 0, chip_sum[...], theirs)
        out_ref[...] = total

    return pl.pallas_call(
        body, name="small_allreduce",
        in_specs=[VMEM] * n + [ANY], out_specs=VMEM,
        out_shape=_sds((rows_total, width), F32),
        scratch_shapes=[pltpu.VMEM((rows_total, width), F32), pltpu.VMEM((rows_total, width), F32),
                        pltpu.VMEM((rows_total, width), F32), pltpu.VMEM((3, rows_total, width), F32),
                        pltpu.SemaphoreType.DMA((4,)), pltpu.SemaphoreType.DMA((4,))],
        compiler_params=_params(),
    )(*parts, after)


def _small_update(red, q_arr, takes, loss_at, ws, ms, vs):
    n_w = len(ws)

    def body(q_ref, red_ref, *refs):
        w_in, m_in, v_in = refs[0:n_w], refs[n_w:2 * n_w], refs[2 * n_w:3 * n_w]
        outs = refs[3 * n_w:]
        g_out, d_out, m_out, v_out = (outs[0:n_w], outs[n_w:2 * n_w], outs[2 * n_w:3 * n_w], outs[3 * n_w:4 * n_w])
        loss_ref = outs[4 * n_w]
        chip = q_ref[0]

        def put(g_ref, d0, nr, s0, lo, w):
            if len(g_ref.shape) == 3:
                for r in range(nr):
                    g_ref[d0 + r] = red_ref[s0 + r:s0 + r + 1, lo:lo + w]
            else:
                g_ref[d0:d0 + nr, :] = red_ref[s0:s0 + nr, lo:lo + w]

        def take_own_columns(g_ref, d0, nr, s0, c0, w):
            for k in range(N_CHIPS):
                @pl.when(chip == k)
                def _():
                    put(g_ref, d0, nr, s0, c0 + k * w, w)

        for j in range(n_w):
            w = ws[j].shape[-1]
            for d0, nr, s0, c0, sharded in takes[j]:
                if sharded:
                    take_own_columns(g_out[j], d0, nr, s0, c0, w)
                else:
                    put(g_out[j], d0, nr, s0, c0, w)
            d_out[j][...], m_out[j][...], v_out[j][...] = _adamw_math(w_in[j][...], g_out[j][...], m_in[j][...], v_in[j][...])
        loss_ref[...] = red_ref[loss_at[0]:loss_at[0] + 1, loss_at[1]:loss_at[1] + LANES]

    shapes = [_sds(w.shape, F32) for w in ws]
    outs = pl.pallas_call(
        body, name="small_update",
        in_specs=[pl.BlockSpec(memory_space=pltpu.SMEM)] + [VMEM] * (1 + 3 * n_w), out_specs=[VMEM] * (4 * n_w + 1),
        out_shape=shapes * 4 + [_sds((1, LANES), F32)],
        compiler_params=_params(),
    )(q_arr, red, *ws, *ms, *vs)
    return outs[0:n_w], outs[n_w:2 * n_w], outs[2 * n_w:3 * n_w], outs[3 * n_w:4 * n_w], outs[4 * n_w]


def _adamw_math(w, g, m, v):
    m2 = ADAM_B1 * m + (1.0 - ADAM_B1) * g
    v2 = ADAM_B2 * v + (1.0 - ADAM_B2) * (g * g)
    m_hat = m2 / (1.0 - ADAM_B1 ** ADAM_STEP)
    v_hat = v2 / (1.0 - ADAM_B2 ** ADAM_STEP)
    delta = -ADAM_LR * (m_hat / (jnp.sqrt(v_hat) + ADAM_EPS) + ADAM_WD * w)
    return delta, m2, v2


ADAMW_BLOCK_BYTES = 3 * 2 ** 19


def _adamw_big(ws, gs, ms, vs, tag):
    n = len(ws)

    def fits(steps):
        return all(w.shape[0] % steps == 0 and (w.shape[0] // steps) % SUBLANES == 0
                   and (w.shape[0] // steps) * w.shape[1] * 4 * n <= ADAMW_BLOCK_BYTES for w in ws)

    steps = next(s for s in range(1, min(w.shape[0] for w in ws) + 1) if fits(s))
    specs = [pl.BlockSpec((w.shape[0] // steps, w.shape[1]), lambda i: (i, 0)) for w in ws]

    def body(*refs):
        ins, outs = refs[:4 * n], refs[4 * n:]
        for i in range(n):
            w_ref, g_ref, m_ref, v_ref = ins[i], ins[n + i], ins[2 * n + i], ins[3 * n + i]
            gg = g_ref[...]
            outs[4 * i][...] = gg
            outs[4 * i + 1][...], outs[4 * i + 2][...], outs[4 * i + 3][...] = _adamw_math(
                w_ref[...], gg, m_ref[...], v_ref[...])

    outs = pl.pallas_call(
        body, name="adamw_" + tag, grid=(steps,),
        in_specs=specs * 4, out_specs=[s for s in specs for _ in range(4)],
        out_shape=[_sds(w.shape, F32) for w in ws for _ in range(4)],
        compiler_params=_params(("arbitrary",)),
    )(*ws, *gs, *ms, *vs)
    return [outs[4 * i:4 * i + 4] for i in range(n)]


SMALL_ROWS = 40
PACK_ROWS = 64
PACK_CONV_A_ROW = 16
PACK_CONV_B_ROW = 24


def kernel(x, meta_tokens, pre_mix_norm, w_in, conv_a_w, conv_b_w, conv_b_bias, ln_b_gain, ln_b_bias, w_out, post_mix_norm, pre_ffn_norm, w_gate, w_up, w_down, post_ffn_norm, loss_target, m_meta_tokens, m_pre_mix_norm, m_w_in, m_conv_a_w, m_conv_b_w, m_conv_b_bias, m_ln_b_gain, m_ln_b_bias, m_w_out, m_post_mix_norm, m_pre_ffn_norm, m_w_gate, m_w_up, m_w_down, m_post_ffn_norm, v_meta_tokens, v_pre_mix_norm, v_w_in, v_conv_a_w, v_conv_b_w, v_conv_b_bias, v_ln_b_gain, v_ln_b_bias, v_w_out, v_post_mix_norm, v_pre_ffn_norm, v_w_gate, v_w_up, v_w_down, v_post_ffn_norm):
    xq, yq, cq = lax.axis_index("x"), lax.axis_index("y"), lax.axis_index("c")
    chip = 2 * xq + yq
    c_arr = jnp.reshape(cq, (1,)).astype(jnp.int32)
    qc_arr = jnp.stack([chip, cq]).astype(jnp.int32)

    seq, d = x.shape[1], x.shape[2]
    x2, tgt2 = x[0], loss_target[0]
    tr = lambda a: jnp.swapaxes(a, 1, 2)[0]
    w_in2, w_out2, w_gate2, w_up2, w_down2 = w_in[0], w_out[0], tr(w_gate), tr(w_up), w_down[0]
    ka, wa_sh = conv_a_w.shape[1], conv_a_w.shape[2]
    kb = conv_b_w.shape[1]
    meta_sh = meta_tokens.shape[1]

    small = jnp.zeros((PACK_ROWS, meta_sh), F32)
    small = small.at[0:N_META, :].set(meta_tokens)
    small = small.at[PACK_CONV_A_ROW:PACK_CONV_A_ROW + ka, 0:wa_sh].set(conv_a_w[0])
    small = small.at[PACK_CONV_B_ROW:PACK_CONV_B_ROW + kb, 0:wa_sh].set(conv_b_w[0])
    q_arr = jnp.reshape(chip, (1,)).astype(jnp.int32)
    small_own = lax.dynamic_update_slice(jnp.zeros((N_CHIPS, PACK_ROWS, meta_sh), F32), small[None], (chip, 0, 0))
    i_ssem, i_rsem, first, i_token = _gather_start(_cast_place([w_in2], q_arr, "w_in") + [small_own], pre_mix_norm, "in")
    rest = _cast_place([w_out2, w_gate2, w_up2, w_down2], q_arr, "rest", i_token)
    g_ssem, g_rsem, lands, g_token = _gather_start(rest, i_token, "rest")
    win4, small4 = _forward_pair(_gather_wait([0, 1], i_ssem, i_rsem, first, g_token, "in"), "in")
    assert wa_sh == LANES

    h, xn1, hp5 = _mm_in(x2, small4, win4, pre_mix_norm, g_token)
    ya, z = _mix_conv_fwd(hp5, small4, ka, kb, conv_b_bias)
    (wout4,) = _forward_pair(_gather_wait([0], g_ssem, g_rsem, lands[0:1], z, "out"), "out")
    wout_f = wout4.reshape(N_CHIPS * wout4.shape[1], wout4.shape[2])
    yb, mix, h1, xn2 = _mm_out(ya, z, h, wout_f, ln_b_gain, ln_b_bias, post_mix_norm, pre_ffn_norm)
    wg4, wu4 = _forward_pair(_gather_wait([1, 2], g_ssem, g_rsem, lands[1:3], xn2, "gate_up"), "gate_up")
    stacked = lambda a: a.reshape(a.shape[0] * a.shape[1], a.shape[2])
    wg_f, wu_f = stacked(wg4), stacked(wu4)
    p_act, q_act, f_act = _ffn_up(xn2, wg_f, wu_f)
    (wd4,) = _forward_pair(_gather_wait([3], g_ssem, g_rsem, lands[3:4], f_act, "down"), "down")
    wd_f = stacked(wd4)
    dff, dh2, loss_blk, d_gpf = _ffn_down(f_act, wd_f, h1, tgt2, post_ffn_norm)

    da, du = _ffn_bwd_act(dff, wd_f, p_act, q_act)
    by_chip = lambda g: g.reshape(N_CHIPS, g.shape[0] // N_CHIPS, g.shape[1])
    g_down = by_chip(_grad_w_down(f_act, dff))
    g_gate, g_up = [by_chip(g) for g in _grad_w_gate_up(xn2, da, du)]
    ffn = [g_gate, g_up, g_down]
    p_ssem, p_rsem, ffn, p_lands, p_token = _pair_exchange_start(ffn, [1, 1, 1], dff, "ffn")
    dh1, dmix, d_g2, d_gpm = _ffn_bwd_in(da, du, wg_f, wu_f, h1, mix, dh2, pre_ffn_norm, post_mix_norm, p_token)
    ffn, got = _pair_exchange_wait(p_ssem, p_rsem, ffn, p_lands, [1, 1, 1], d_g2, "ffn")
    parts = _pair_sum(ffn, got, c_arr, [False] * 3, "ffn")
    f_ssem, f_rsem, parts, f_lands, f_token = _chip_exchange_start(parts, dff, "ffn")
    g_out = _grad_w_out(ya, yb, dmix, f_token)
    dya, dz, d_lg, d_lb = _mix_bwd_out(dmix, wout_f, z, ln_b_gain, ln_b_bias, f_token)
    dhp5, d_wa, d_wb, d_bb = _mix_conv_bwd(hp5, dya, dz, small4, ka, kb)
    g_in = _grad_w_in(xn1, dhp5)

    g_out4 = g_out.reshape(N_CHIPS, g_out.shape[0] // N_CHIPS, g_out.shape[1])
    mixw = [g_in, g_out4]
    got2 = _pair_exchange_grads(mixw, [0, 1], "mix")
    parts2 = _pair_sum(mixw, got2, c_arr, [True, False], "mix")
    m_ssem, m_rsem, parts2, m_lands, m_token = _chip_exchange_start(parts2, dhp5, "mix")
    grad_x2, d_meta, d_g1 = _mix_bwd_in(dhp5, win4, h, dh1, pre_mix_norm, m_token)
    grad_x = grad_x2[None]

    parts, f_recv = _chip_exchange_wait(f_ssem, f_rsem, parts, f_lands, d_g1, "ffn")
    halves = _chip_sum(parts, f_recv, qc_arr, "ffn")
    s_ssem, s_rsem, halves, s_token = _pair_share_start(halves, d_g1, "ffn")

    hw = d // 2
    assert d_wa.shape == (3, hw) and d_wb.shape == (31, hw) and d_bb.shape == (1, hw)
    small_parts = [d_meta, d_g1, d_gpm, d_g2, d_gpf, d_bb, d_lg, d_lb, loss_blk[0:1, :], d_wa, d_wb]
    places = [[(0, 0, 0, N_META)], [(16, 0, 0, 1)], [(17, 0, 0, 1)], [(18, 0, 0, 1)], [(19, 0, 0, 1)],
              [(20, 0, 0, 1)], [(20, hw, 0, 1)], [(21, 0, 0, 1)], [(21, hw, 0, 1)], [(22, 0, 0, 3)],
              [(22, hw, 0, 3), (25, 0, 3, 14), (25, hw, 17, 14)]]
    red = _small_allreduce(small_parts, places, SMALL_ROWS, d, s_token)
    gsum_ffn = _pair_share_wait(s_ssem, s_rsem, halves, red, "ffn")

    names_big = ["w_in", "w_out", "w_gate", "w_up", "w_down"]
    w_big = dict(zip(names_big, [w_in2, w_out2, w_gate2, w_up2, w_down2]))
    m_big = dict(zip(names_big, [m_w_in[0], m_w_out[0], tr(m_w_gate), tr(m_w_up), m_w_down[0]]))
    v_big = dict(zip(names_big, [v_w_in[0], v_w_out[0], tr(v_w_gate), tr(v_w_up), v_w_down[0]]))
    grads, deltas, new_m, new_v = {}, {}, {}, {}

    def update(names, gs, tag):
        res = _adamw_big([w_big[k] for k in names], gs, [m_big[k] for k in names], [v_big[k] for k in names], tag)
        for nm, outs in zip(names, res):
            if nm in ("w_gate", "w_up"):
                outs = [jnp.swapaxes(o[None], 1, 2) for o in outs]
            else:
                outs = [o[None] for o in outs]
            grads[nm], deltas[nm], new_m[nm], new_v[nm] = outs
        return res[-1][1]

    last = update(["w_gate", "w_up", "w_down"], list(gsum_ffn), "ffn")

    names_small = ["meta_tokens", "pre_mix_norm", "conv_a_w", "conv_b_w", "conv_b_bias", "ln_b_gain", "ln_b_bias",
                   "post_mix_norm", "pre_ffn_norm", "post_ffn_norm"]
    takes = [[(0, N_META, 0, 0, True)], [(0, 1, 16, 0, False)], [(0, 3, 22, 0, True)],
             [(0, 3, 22, hw, True), (3, 14, 25, 0, True), (17, 14, 25, hw, True)], [(0, 1, 20, 0, False)],
             [(0, 1, 20, hw, False)], [(0, 1, 21, 0, False)], [(0, 1, 17, 0, False)], [(0, 1, 18, 0, False)],
             [(0, 1, 19, 0, False)]]
    taps = lambda a: jnp.swapaxes(a, 0, 1)
    w_small = [meta_tokens, pre_mix_norm, taps(conv_a_w), taps(conv_b_w), conv_b_bias, ln_b_gain, ln_b_bias, post_mix_norm,
               pre_ffn_norm, post_ffn_norm]
    m_small = [m_meta_tokens, m_pre_mix_norm, taps(m_conv_a_w), taps(m_conv_b_w), m_conv_b_bias, m_ln_b_gain, m_ln_b_bias,
               m_post_mix_norm, m_pre_ffn_norm, m_post_ffn_norm]
    v_small = [v_meta_tokens, v_pre_mix_norm, taps(v_conv_a_w), taps(v_conv_b_w), v_conv_b_bias, v_ln_b_gain, v_ln_b_bias,
               v_post_mix_norm, v_pre_ffn_norm, v_post_ffn_norm]
    g_s, d_s, m_s, v_s, loss_row = _small_update(red, q_arr, takes, (21, hw), w_small, m_small, v_small)
    loss = loss_row[0, 0]
    for i, nm in enumerate(names_small):
        fix = taps if nm in ("conv_a_w", "conv_b_w") else (lambda a: a)
        grads[nm], deltas[nm], new_m[nm], new_v[nm] = fix(g_s[i]), fix(d_s[i]), fix(m_s[i]), fix(v_s[i])

    parts2, m_recv = _chip_exchange_wait(m_ssem, m_rsem, parts2, m_lands, last, "mix")
    halves2 = _chip_sum(parts2, m_recv, qc_arr, "mix")
    gsum_mix = _pair_share_grads(halves2, "mix")
    update(["w_in", "w_out"], list(gsum_mix), "mix")

    order = ["meta_tokens", "pre_mix_norm", "w_in", "conv_a_w", "conv_b_w", "conv_b_bias", "ln_b_gain", "ln_b_bias", "w_out",
             "post_mix_norm", "pre_ffn_norm", "w_gate", "w_up", "w_down", "post_ffn_norm"]
    return (loss, grad_x, *[grads[k] for k in order], *[deltas[k] for k in order], *[new_m[k] for k in order],
            *[new_v[k] for k in order])
```

```python
import jax
import jax.numpy as jnp
from jax import lax
from jax.experimental import pallas as pl
from jax.experimental.pallas import tpu as pltpu

F32 = jnp.float32
BF16 = jnp.bfloat16
MESH = pl.DeviceIdType.MESH

N_META = 16
TAIL_ROWS = 128
RMS_EPS = 1e-6
LN_EPS = 1e-5
ADAM_LR = 0.001
ADAM_B1 = 0.9
ADAM_B2 = 0.999
ADAM_EPS = 1e-08
ADAM_WD = 0.01
ADAM_STEP = 10

N_CHIPS = 4
LANES = 128
SUBLANES = 8
BF16_ROWS = 16
MXU_TILE = 256
CONV_CHUNK = 48
CONV_HIST = 32
ROW_TILE_CAP = 640
VMEM_LIMIT = 56 * 1024 * 1024

NN = (((1,), (0,)), ((), ()))
NT = (((1,), (1,)), ((), ()))
TN = (((0,), (0,)), ((), ()))


def _dot(a, b, dims=NN):
    return lax.dot_general(a, b, dims, preferred_element_type=F32)


def _sig(v):
    return 1.0 / (1.0 + jnp.exp(-v))


def _mean(v):
    return jnp.mean(v, axis=-1, keepdims=True)


def _row_tile(rows):
    best = BF16_ROWS
    for t in range(BF16_ROWS, min(rows, ROW_TILE_CAP) + 1, BF16_ROWS):
        if rows % t == 0:
            best = t
    assert rows % best == 0
    return best


def _row_parts(tm, parts=2):
    units = tm // BF16_ROWS
    if tm % BF16_ROWS or units < parts:
        return [slice(0, tm)]
    cuts = [BF16_ROWS * ((units * k + parts - 1) // parts) for k in range(parts + 1)]
    return [slice(lo, hi) for lo, hi in zip(cuts[:-1], cuts[1:])]


def _concat_shards(w_ref, wcat_ref):
    n_sh, _, csh = w_ref.shape

    @pl.when(pl.program_id(0) == 0)
    def _():
        for k in range(n_sh):
            wcat_ref[:, k * csh:(k + 1) * csh] = w_ref[k]


def _params(semantics=None):
    kw = dict(vmem_limit_bytes=VMEM_LIMIT)
    if semantics is not None:
        kw["dimension_semantics"] = semantics
    return pltpu.CompilerParams(**kw)


def _full(shape):
    nd = len(shape)
    return pl.BlockSpec(shape, lambda *_: (0,) * nd)


def _resident(shape):
    nd = len(shape)
    return pl.BlockSpec(shape, lambda *_: (0,) * nd, pipeline_mode=pl.Buffered(1))


def _sds(shape, dtype):
    return jax.ShapeDtypeStruct(shape, dtype)


ANY = pl.BlockSpec(memory_space=pl.ANY)
VMEM = pl.BlockSpec(memory_space=pltpu.VMEM)


def _mesh_pos():
    return lax.axis_index("x"), lax.axis_index("y"), lax.axis_index("c")


def _flip(v, bit):
    return 1 - v if bit else v


def _mm_in(x, small4, win4, g1, after):
    seq, d = x.shape
    tp = seq + TAIL_ROWS
    tm = _row_tile(tp)
    n_sh, _, csh = win4.shape
    pw = n_sh * csh // 5
    meta_off, meta_w = seq % tm, small4.shape[2]
    assert meta_off + N_META <= tm and seq // tm == tp // tm - 1 and N_CHIPS * meta_w == d

    def body(x_ref, small_ref, w_ref, g_ref, after_ref, h_ref, xn_ref, hp_ref, wcat_ref, tail_ref):
        _concat_shards(w_ref, wcat_ref)

        @pl.when(pl.program_id(0) == 0)
        def _():
            tail_ref[...] = jnp.zeros(tail_ref.shape, F32)
            for k in range(N_CHIPS):
                tail_ref[meta_off:meta_off + N_META, k * meta_w:(k + 1) * meta_w] = small_ref[k, 0:N_META, :]

        rows = pl.program_id(0) * tm + lax.broadcasted_iota(jnp.int32, (tm, 1), 0)
        hh = jnp.where(rows < seq, x_ref[...], tail_ref[...])
        h_ref[...] = hh
        r = lax.rsqrt(_mean(hh * hh) + RMS_EPS)
        xn = (hh * r * g_ref[...]).astype(BF16)
        xn_ref[...] = xn
        for p in range(5):
            hp_ref[p] = _dot(xn, wcat_ref[:, p * pw:(p + 1) * pw])

    row = pl.BlockSpec((tm, d), lambda i: (i, 0))
    return pl.pallas_call(
        body, name="mm_in", grid=(tp // tm,),
        in_specs=[row, _full(small4.shape), _resident(win4.shape), _full(g1.shape), ANY],
        out_specs=[row, row, pl.BlockSpec((5, tm, pw), lambda i: (0, i, 0))],
        out_shape=[_sds((tp, d), F32), _sds((tp, d), BF16), _sds((5, tp, pw), F32)],
        scratch_shapes=[pltpu.VMEM((d, n_sh * csh), BF16), pltpu.VMEM((tm, d), F32)],
        compiler_params=_params(("arbitrary",)),
    )(x, small4, win4, g1, after)


def _seq_rows(tp):
    seq = tp - TAIL_ROWS
    nseq = seq + N_META
    assert nseq % CONV_CHUNK == 0 and seq % BF16_ROWS == 0
    return seq, nseq


def _conv_offsets(width, transpose):
    return [(width - 1 - k) if transpose else (CONV_HIST - (width - 1) + k) for k in range(width)]


def _shift_copies(src_ref, sh_ref, width, transpose):
    n = src_ref.shape[0] - SUBLANES
    for s in sorted({o % SUBLANES for o in _conv_offsets(width, transpose)} - {0}):
        sh_ref[s - 1, 0:n, :] = src_ref[s:s + n, :]


def _tap_rows(src_ref, sh_ref, base, off):
    start = pl.multiple_of(base + (off // SUBLANES) * SUBLANES, SUBLANES)
    if off % SUBLANES == 0:
        return src_ref[pl.ds(start, CONV_CHUNK), :]
    return sh_ref[off % SUBLANES - 1, pl.ds(start, CONV_CHUNK), :]


def _conv_taps(src_ref, sh_ref, w, dst_ref, width, nseq, transpose, shifted=False):
    offs = _conv_offsets(width, transpose)
    if not shifted:
        _shift_copies(src_ref, sh_ref, width, transpose)

    def step(n, carry):
        out0 = pl.multiple_of(CONV_HIST + n * CONV_CHUNK, SUBLANES)
        base = out0 if transpose else n * CONV_CHUNK
        acc = jnp.zeros((CONV_CHUNK, w.shape[1]), F32)
        for k, off in enumerate(offs):
            acc = acc + w[k:k + 1, :] * _tap_rows(src_ref, sh_ref, base, off)
        dst_ref[pl.ds(out0, CONV_CHUNK), :] = acc
        return carry

    lax.fori_loop(0, nseq // CONV_CHUNK, step, 0)


def _conv_wgrad(src_ref, sh_ref, dz_ref, acc_ref, width, nseq):
    acc_ref[...] = jnp.zeros(acc_ref.shape, F32)
    offs = _conv_offsets(width, False)

    def step(n, carry):
        dzc = dz_ref[pl.ds(pl.multiple_of(CONV_HIST + n * CONV_CHUNK, SUBLANES), CONV_CHUNK), :]
        for k, off in enumerate(offs):
            prod = dzc * _tap_rows(src_ref, sh_ref, n * CONV_CHUNK, off)
            part = prod[0:SUBLANES, :]
            for s in range(1, CONV_CHUNK // SUBLANES):
                part = part + prod[SUBLANES * s:SUBLANES * (s + 1), :]
            acc_ref[SUBLANES * k:SUBLANES * (k + 1), :] += part
        return carry

    lax.fori_loop(0, nseq // CONV_CHUNK, step, 0)


def _conv_wgrad_by_dz_shifts(src_ref, dz_ref, shd_ref, acc_ref, width, nseq):
    acc_ref[...] = jnp.zeros(acc_ref.shape, F32)
    offs = _conv_offsets(width, True)

    def step(n, carry):
        base = pl.multiple_of(CONV_HIST + n * CONV_CHUNK, SUBLANES)
        rows = src_ref[pl.ds(base, CONV_CHUNK), :]
        for k, off in enumerate(offs):
            prod = rows * _tap_rows(dz_ref, shd_ref, base, off)
            part = prod[0:SUBLANES, :]
            for s in range(1, CONV_CHUNK // SUBLANES):
                part = part + prod[SUBLANES * s:SUBLANES * (s + 1), :]
            acc_ref[SUBLANES * k:SUBLANES * (k + 1), :] += part
        return carry

    lax.fori_loop(0, nseq // CONV_CHUNK, step, 0)


def _to_seq(buf_ref, x_part, meta_part, seq):
    buf_ref[CONV_HIST:CONV_HIST + N_META, :] = meta_part
    buf_ref[CONV_HIST + N_META:CONV_HIST + N_META + seq, :] = x_part


def _zero_ends(buf_ref, nseq):
    zeros = jnp.zeros((CONV_HIST, buf_ref.shape[1]), F32)
    buf_ref[0:CONV_HIST, :] = zeros
    buf_ref[CONV_HIST + nseq:CONV_HIST + nseq + CONV_HIST, :] = zeros


def _conv_weights(small_ref, ka, kb):
    return (small_ref[0, PACK_CONV_A_ROW:PACK_CONV_A_ROW + ka, 0:LANES],
            small_ref[0, PACK_CONV_B_ROW:PACK_CONV_B_ROW + kb, 0:LANES])


def _mix_conv_fwd(hp5, small4, ka, kb, bb):
    _, tp, wgrp = hp5.shape
    seq, nseq = _seq_rows(tp)
    sb = nseq + 2 * CONV_HIST
    assert wgrp // LANES == small4.shape[0]
    xs, ms = slice(0, seq), slice(seq, seq + N_META)
    ox, om = slice(CONV_HIST + N_META, CONV_HIST + nseq), slice(CONV_HIST, CONV_HIST + N_META)

    def body(hp_ref, small_ref, bb_ref, ya_ref, z_ref, s_ref, o_ref, sh_ref):
        wa, wb = _conv_weights(small_ref, ka, kb)
        _zero_ends(s_ref, nseq)
        _to_seq(s_ref, hp_ref[1, xs, :] * hp_ref[2, xs, :], hp_ref[1, ms, :] * hp_ref[2, ms, :], seq)
        _conv_taps(s_ref, sh_ref, wa, o_ref, ka, nseq, False)
        ya_ref[xs, :] = (hp_ref[0, xs, :] * o_ref[ox, :]).astype(BF16)
        ya_ref[ms, :] = (hp_ref[0, ms, :] * o_ref[om, :]).astype(BF16)
        ya_ref[seq + N_META:tp, :] = jnp.zeros((tp - seq - N_META, LANES), BF16)
        _to_seq(s_ref, hp_ref[3, xs, :] * _sig(hp_ref[4, xs, :]), hp_ref[3, ms, :] * _sig(hp_ref[4, ms, :]), seq)
        _conv_taps(s_ref, sh_ref, wb, o_ref, kb, nseq, False)
        z_ref[xs, :] = o_ref[ox, :] + bb_ref[...]
        z_ref[ms, :] = o_ref[om, :] + bb_ref[...]
        z_ref[seq + N_META:tp, :] = jnp.zeros((tp - seq - N_META, LANES), F32)

    col = lambda j: (0, j)
    return pl.pallas_call(
        body, name="mix_conv_fwd", grid=(wgrp // LANES,),
        in_specs=[pl.BlockSpec((5, tp, LANES), lambda j: (0, 0, j)),
                  pl.BlockSpec((1,) + small4.shape[1:], lambda j: (j, 0, 0)), pl.BlockSpec((1, LANES), col)],
        out_specs=[pl.BlockSpec((tp, LANES), col), pl.BlockSpec((tp, LANES), col)],
        out_shape=[_sds((tp, wgrp), BF16), _sds((tp, wgrp), F32)],
        scratch_shapes=[pltpu.VMEM((sb, LANES), F32), pltpu.VMEM((sb, LANES), F32),
                        pltpu.VMEM((SUBLANES - 1, sb, LANES), F32)],
        compiler_params=_params(("arbitrary",)),
    )(hp5, small4, bb)


def _layer_norm_parts(z, lg, lb):
    mu = _mean(z)
    zc = z - mu
    rl = lax.rsqrt(_mean(zc * zc) + LN_EPS)
    zh = zc * rl
    return rl, zh, zh * lg + lb


def _mm_out(ya, z, h, wout, lg, lb, gpm, g2):
    tp, d = h.shape
    wa_ = ya.shape[1]
    tm = _row_tile(tp)

    def body(ya_ref, z_ref, h_ref, w_ref, lg_ref, lb_ref, gpm_ref, g2_ref, yb_ref, mix_ref, h1_ref, xn2_ref):
        for rs in _row_parts(tm, 3):
            _, _, l = _layer_norm_parts(z_ref[rs, :], lg_ref[...], lb_ref[...])
            yb = (l * _sig(l)).astype(BF16)
            yb_ref[rs, :] = yb
            mix = _dot(ya_ref[rs, :], w_ref[0:wa_, :]) + _dot(yb, w_ref[wa_:d, :])
            mix_ref[rs, :] = mix
            rm = lax.rsqrt(_mean(mix * mix) + RMS_EPS)
            h1 = h_ref[rs, :] + mix * rm * gpm_ref[...]
            h1_ref[rs, :] = h1
            r2 = lax.rsqrt(_mean(h1 * h1) + RMS_EPS)
            xn2_ref[rs, :] = (h1 * r2 * g2_ref[...]).astype(BF16)

    row = lambda i: (i, 0)
    return pl.pallas_call(
        body, name="mm_out", grid=(tp // tm,),
        in_specs=[pl.BlockSpec((tm, wa_), row), pl.BlockSpec((tm, wa_), row), pl.BlockSpec((tm, d), row),
                  _resident(wout.shape), _full(lg.shape), _full(lb.shape), _full(gpm.shape), _full(g2.shape)],
        out_specs=[pl.BlockSpec((tm, wa_), row), pl.BlockSpec((tm, d), row), pl.BlockSpec((tm, d), row),
                   pl.BlockSpec((tm, d), row)],
        out_shape=[_sds((tp, wa_), BF16), _sds((tp, d), F32), _sds((tp, d), F32), _sds((tp, d), BF16)],
        compiler_params=_params(("arbitrary",)),
    )(ya, z, h, wout, lg, lb, gpm, g2)


def _ffn_up(xn2, wg, wu):
    tp, d = xn2.shape
    ff_dim = wg.shape[0]
    tm = _row_tile(tp)
    assert ff_dim % MXU_TILE == 0

    def body(xn_ref, wg_ref, wu_ref, p_ref, q_ref, f_ref):
        xn = xn_ref[...]
        for lo in range(0, ff_dim, MXU_TILE):
            cols = slice(lo, lo + MXU_TILE)
            a = _dot(xn, wg_ref[cols, :], NT)
            u = _dot(xn, wu_ref[cols, :], NT)
            s = _sig(a)
            q = a * s
            p_ref[:, cols] = (u * (s + q * (1.0 - s))).astype(BF16)
            q_ref[:, cols] = q.astype(BF16)
            f_ref[:, cols] = (q * u).astype(BF16)

    ospec = pl.BlockSpec((tm, ff_dim), lambda i: (i, 0))
    return pl.pallas_call(
        body, name="ffn_up", grid=(tp // tm,),
        in_specs=[pl.BlockSpec((tm, d), lambda i: (i, 0)), _resident(wg.shape), _resident(wu.shape)],
        out_specs=[ospec, ospec, ospec],
        out_shape=[_sds((tp, ff_dim), BF16)] * 3,
        compiler_params=_params(("arbitrary",)),
    )(xn2, wg, wu)


def _ffn_down(f, wd, h1, tgt, gpf):
    tp, ff_dim = f.shape
    d = h1.shape[1]
    tm = _row_tile(tp)
    seq, _ = _seq_rows(tp)

    def body(f_ref, w_ref, h1_ref, t_ref, gpf_ref, dff_ref, dh2_ref, loss_ref, dgpf_ref):
        i = pl.program_id(0)
        gpf_ = gpf_ref[...]

        @pl.when(i == 0)
        def _():
            loss_ref[...] = jnp.zeros(loss_ref.shape, F32)
            dgpf_ref[...] = jnp.zeros(dgpf_ref.shape, F32)

        for rs in _row_parts(tm):
            ff = _dot(f_ref[rs, :], w_ref[...])
            rf = lax.rsqrt(_mean(ff * ff) + RMS_EPS)
            nf = ff * rf
            h2 = h1_ref[rs, :] + nf * gpf_
            rows = i * tm + rs.start + lax.broadcasted_iota(jnp.int32, (rs.stop - rs.start, 1), 0)
            err = jnp.where(rows < seq, h2 - t_ref[rs, :], 0.0)
            dh2 = err * (1.0 / d)
            dh2_ref[rs, :] = dh2
            dn = dh2 * gpf_
            dff_ref[rs, :] = (rf * (dn - nf * _mean(dn * nf))).astype(BF16)
            loss_ref[...] += (0.5 / d) * jnp.sum(err * err, axis=(0, 1), keepdims=True)
            dgpf_ref[...] += jnp.sum(dh2 * nf, axis=0, keepdims=True)

    row = lambda i: (i, 0)
    return pl.pallas_call(
        body, name="ffn_down", grid=(tp // tm,),
        in_specs=[pl.BlockSpec((tm, ff_dim), row), _resident(wd.shape), pl.BlockSpec((tm, d), row),
                  pl.BlockSpec((tm, d), row), _full(gpf.shape)],
        out_specs=[pl.BlockSpec((tm, d), row), pl.BlockSpec((tm, d), row), _full((SUBLANES, LANES)), _full((1, d))],
        out_shape=[_sds((tp, d), BF16), _sds((tp, d), F32), _sds((SUBLANES, LANES), F32), _sds((1, d), F32)],
        compiler_params=_params(("arbitrary",)),
    )(f, wd, h1, tgt, gpf)


def _ffn_bwd_act(dff, wd, p, q):
    tp, d = dff.shape
    ff_dim = wd.shape[0]
    tm = _row_tile(tp)

    def body(dff_ref, w_ref, p_ref, q_ref, da_ref, du_ref):
        dffv = dff_ref[...]
        for lo in range(0, ff_dim, MXU_TILE):
            cols = slice(lo, lo + MXU_TILE)
            df = _dot(dffv, w_ref[cols, :], NT).astype(BF16)
            da_ref[:, cols] = df * p_ref[:, cols]
            du_ref[:, cols] = df * q_ref[:, cols]

    aspec = pl.BlockSpec((tm, ff_dim), lambda i: (i, 0))
    return pl.pallas_call(
        body, name="ffn_bwd_act", grid=(tp // tm,),
        in_specs=[pl.BlockSpec((tm, d), lambda i: (i, 0)), _resident(wd.shape), aspec, aspec],
        out_specs=[aspec, aspec],
        out_shape=[_sds((tp, ff_dim), BF16)] * 2,
        compiler_params=_params(("arbitrary",)),
    )(dff, wd, p, q)


def _grad_blocks(ff_dim):
    rows = ff_dim // 2
    assert rows % LANES == 0
    return rows


def _grad_w_down(f, dff):
    tp, ff_dim = f.shape
    d = dff.shape[1]
    rows = _grad_blocks(ff_dim)

    def body(f_ref, dff_ref, g_ref):
        g_ref[...] = _dot(f_ref[...], dff_ref[...], TN).astype(BF16)

    return pl.pallas_call(
        body, name="grad_w_down", grid=(ff_dim // rows,),
        in_specs=[pl.BlockSpec((tp, rows), lambda k: (0, k)), _resident(dff.shape)],
        out_specs=pl.BlockSpec((rows, d), lambda k: (k, 0)),
        out_shape=_sds((ff_dim, d), BF16),
        compiler_params=_params(("arbitrary",)),
    )(f, dff)


def _grad_w_gate_up(xn2, da, du):
    tp, ff_dim = da.shape
    d = xn2.shape[1]
    rows = _grad_blocks(ff_dim)

    def body(xn_ref, da_ref, du_ref, gg_ref, gu_ref):
        xn = xn_ref[...]
        gg_ref[...] = _dot(da_ref[...], xn, TN).astype(BF16)
        gu_ref[...] = _dot(du_ref[...], xn, TN).astype(BF16)

    aspec = pl.BlockSpec((tp, rows), lambda k: (0, k))
    gspec = pl.BlockSpec((rows, d), lambda k: (k, 0))
    return pl.pallas_call(
        body, name="grad_w_gate_up", grid=(ff_dim // rows,),
        in_specs=[_resident(xn2.shape), aspec, aspec],
        out_specs=[gspec, gspec],
        out_shape=[_sds((ff_dim, d), BF16)] * 2,
        compiler_params=_params(("arbitrary",)),
    )(xn2, da, du)


def _rms_bwd(dy, x, r, g):
    n = x * r
    dn = dy * g
    return r * (dn - n * _mean(dn * n)), dy * n


def _ffn_bwd_in(da, du, wg, wu, h1, mix, dh2, g2, gpm, after):
    tp, ff_dim = da.shape
    d = h1.shape[1]
    tm = _row_tile(tp)

    def body(da_ref, du_ref, wg_ref, wu_ref, h1_ref, mix_ref, dh2_ref, g2_ref, gpm_ref, after_ref,
             dh1_ref, dmix_ref, dg2_ref, dgpm_ref):
        i = pl.program_id(0)

        @pl.when(i == 0)
        def _():
            dg2_ref[...] = jnp.zeros(dg2_ref.shape, F32)
            dgpm_ref[...] = jnp.zeros(dgpm_ref.shape, F32)

        for rs in _row_parts(tm, 3):
            dxn = _dot(da_ref[rs, :], wg_ref[...]) + _dot(du_ref[rs, :], wu_ref[...])
            h1v = h1_ref[rs, :]
            r2 = lax.rsqrt(_mean(h1v * h1v) + RMS_EPS)
            dres, dg2_rows = _rms_bwd(dxn, h1v, r2, g2_ref[...])
            dh1 = dh2_ref[rs, :] + dres
            dh1_ref[rs, :] = dh1
            mixv = mix_ref[rs, :]
            rm = lax.rsqrt(_mean(mixv * mixv) + RMS_EPS)
            dmix, dgpm_rows = _rms_bwd(dh1, mixv, rm, gpm_ref[...])
            dmix_ref[rs, :] = dmix.astype(BF16)
            dg2_ref[...] += jnp.sum(dg2_rows, axis=0, keepdims=True)
            dgpm_ref[...] += jnp.sum(dgpm_rows, axis=0, keepdims=True)

    aspec = pl.BlockSpec((tm, ff_dim), lambda i: (i, 0))
    row = pl.BlockSpec((tm, d), lambda i: (i, 0))
    return pl.pallas_call(
        body, name="ffn_bwd_in", grid=(tp // tm,),
        in_specs=[aspec, aspec, _resident(wg.shape), _resident(wu.shape), row, row, row, _full(g2.shape), _full(gpm.shape),
                  ANY],
        out_specs=[row, row, _full((1, d)), _full((1, d))],
        out_shape=[_sds((tp, d), F32), _sds((tp, d), BF16), _sds((1, d), F32), _sds((1, d), F32)],
        compiler_params=_params(("arbitrary",)),
    )(da, du, wg, wu, h1, mix, dh2, g2, gpm, after)


def _grad_w_out(ya, yb, dmix, after):
    tp, wa_ = ya.shape
    d = dmix.shape[1]

    def body(ya_ref, yb_ref, dmix_ref, after_ref, g_ref):
        dm = dmix_ref[...]
        g_ref[0:wa_, :] = _dot(ya_ref[...], dm, TN).astype(BF16)
        g_ref[wa_:2 * wa_, :] = _dot(yb_ref[...], dm, TN).astype(BF16)

    return pl.pallas_call(
        body, name="grad_w_out", grid=(1,),
        in_specs=[_full(ya.shape), _full(yb.shape), _full(dmix.shape), ANY],
        out_specs=_full((2 * wa_, d)),
        out_shape=_sds((2 * wa_, d), BF16),
        compiler_params=_params(("arbitrary",)),
    )(ya, yb, dmix, after)


def _mix_bwd_out(dmix, wout, z, lg, lb, after):
    tp, d = dmix.shape
    wa_ = z.shape[1]
    tm = _row_tile(tp)

    def body(dmix_ref, w_ref, z_ref, lg_ref, lb_ref, after_ref, dya_ref, dz_ref, dlg_ref, dlb_ref):
        i = pl.program_id(0)
        lg_ = lg_ref[...]

        @pl.when(i == 0)
        def _():
            dlg_ref[...] = jnp.zeros(dlg_ref.shape, F32)
            dlb_ref[...] = jnp.zeros(dlb_ref.shape, F32)

        for rs in _row_parts(tm):
            dm = dmix_ref[rs, :]
            dya_ref[rs, :] = _dot(dm, w_ref[0:wa_, :], NT)
            dyb = _dot(dm, w_ref[wa_:d, :], NT)
            rl, zh, l = _layer_norm_parts(z_ref[rs, :], lg_, lb_ref[...])
            sl = _sig(l)
            dl = dyb * (sl * (1.0 + l * (1.0 - sl)))
            dzh = dl * lg_
            dz_ref[rs, :] = rl * (dzh - _mean(dzh) - zh * _mean(dzh * zh))
            dlg_ref[...] += jnp.sum(dl * zh, axis=0, keepdims=True)
            dlb_ref[...] += jnp.sum(dl, axis=0, keepdims=True)

    row = lambda i: (i, 0)
    return pl.pallas_call(
        body, name="mix_bwd_out", grid=(tp // tm,),
        in_specs=[pl.BlockSpec((tm, d), row), _resident(wout.shape), pl.BlockSpec((tm, wa_), row), _full(lg.shape),
                  _full(lb.shape), ANY],
        out_specs=[pl.BlockSpec((tm, wa_), row), pl.BlockSpec((tm, wa_), row), _full((1, wa_)), _full((1, wa_))],
        out_shape=[_sds((tp, wa_), F32), _sds((tp, wa_), F32), _sds((1, wa_), F32), _sds((1, wa_), F32)],
        compiler_params=_params(("arbitrary",)),
    )(dmix, wout, z, lg, lb, after)


def _mix_conv_bwd(hp5, dya, dz, small4, ka, kb):
    _, tp, wgrp = hp5.shape
    seq, nseq = _seq_rows(tp)
    sb = nseq + 2 * CONV_HIST
    assert wgrp // LANES == small4.shape[0]
    xs, ms = slice(0, seq), slice(seq, seq + N_META)
    ox, om = slice(CONV_HIST + N_META, CONV_HIST + nseq), slice(CONV_HIST, CONV_HIST + N_META)
    n_tail = tp - seq - N_META

    def body(hp_ref, dya_ref, dz_ref, small_ref, dhp_ref, dwa_ref, dwb_ref, dbb_ref, s_ref, d_ref, o_ref, acc_ref,
             shs_ref, shd_ref):
        wa, wb = _conv_weights(small_ref, ka, kb)
        _zero_ends(s_ref, nseq)
        _zero_ends(d_ref, nseq)

        def put(p, ox_val, om_val):
            dhp_ref[p, xs, :] = ox_val.astype(BF16)
            dhp_ref[p, ms, :] = om_val.astype(BF16)
            dhp_ref[p, seq + N_META:tp, :] = jnp.zeros((n_tail, LANES), BF16)

        def wgrad(dw_ref, width):
            for k in range(width):
                dw_ref[k:k + 1, :] = jnp.sum(acc_ref[8 * k:8 * k + 8, :], axis=0, keepdims=True)

        _to_seq(s_ref, hp_ref[1, xs, :] * hp_ref[2, xs, :], hp_ref[1, ms, :] * hp_ref[2, ms, :], seq)
        _conv_taps(s_ref, shs_ref, wa, o_ref, ka, nseq, False)
        put(0, dya_ref[xs, :] * o_ref[ox, :], dya_ref[ms, :] * o_ref[om, :])
        _to_seq(d_ref, dya_ref[xs, :] * hp_ref[0, xs, :], dya_ref[ms, :] * hp_ref[0, ms, :], seq)
        _conv_wgrad(s_ref, shs_ref, d_ref, acc_ref, ka, nseq)
        wgrad(dwa_ref, ka)
        _conv_taps(d_ref, shd_ref, wa, o_ref, ka, nseq, True)
        put(1, o_ref[ox, :] * hp_ref[2, xs, :], o_ref[om, :] * hp_ref[2, ms, :])
        put(2, o_ref[ox, :] * hp_ref[1, xs, :], o_ref[om, :] * hp_ref[1, ms, :])

        _to_seq(s_ref, hp_ref[3, xs, :] * _sig(hp_ref[4, xs, :]), hp_ref[3, ms, :] * _sig(hp_ref[4, ms, :]), seq)
        _to_seq(d_ref, dz_ref[xs, :], dz_ref[ms, :], seq)
        dbb_ref[...] = (jnp.sum(dz_ref[xs, :], axis=0, keepdims=True)
                        + jnp.sum(dz_ref[ms, :], axis=0, keepdims=True))
        _shift_copies(d_ref, shd_ref, kb, True)
        _conv_wgrad_by_dz_shifts(s_ref, d_ref, shd_ref, acc_ref, kb, nseq)
        wgrad(dwb_ref, kb)
        _conv_taps(d_ref, shd_ref, wb, o_ref, kb, nseq, True, shifted=True)
        sx, sm = _sig(hp_ref[4, xs, :]), _sig(hp_ref[4, ms, :])
        put(3, o_ref[ox, :] * sx, o_ref[om, :] * sm)
        put(4, o_ref[ox, :] * hp_ref[3, xs, :] * sx * (1.0 - sx), o_ref[om, :] * hp_ref[3, ms, :] * sm * (1.0 - sm))

    col = lambda j: (0, j)
    blk5 = pl.BlockSpec((5, tp, LANES), lambda j: (0, 0, j))
    return pl.pallas_call(
        body, name="mix_conv_bwd", grid=(wgrp // LANES,),
        in_specs=[blk5, pl.BlockSpec((tp, LANES), col), pl.BlockSpec((tp, LANES), col),
                  pl.BlockSpec((1,) + small4.shape[1:], lambda j: (j, 0, 0))],
        out_specs=[blk5, pl.BlockSpec((ka, LANES), col), pl.BlockSpec((kb, LANES), col), pl.BlockSpec((1, LANES), col)],
        out_shape=[_sds((5, tp, wgrp), BF16), _sds((ka, wgrp), F32), _sds((kb, wgrp), F32), _sds((1, wgrp), F32)],
        scratch_shapes=[pltpu.VMEM((sb, LANES), F32), pltpu.VMEM((sb, LANES), F32), pltpu.VMEM((sb, LANES), F32),
                        pltpu.VMEM((SUBLANES * kb, LANES), F32), pltpu.VMEM((SUBLANES - 1, sb, LANES), F32),
                        pltpu.VMEM((SUBLANES - 1, sb, LANES), F32)],
        compiler_params=_params(("arbitrary",)),
    )(hp5, dya, dz, small4)


def _grad_w_in(xn1, dhp5):
    n_p, tp, pw = dhp5.shape
    d = xn1.shape[1]

    def body(xn_ref, dhp_ref, g_ref):
        g_ref[...] = _dot(xn_ref[...], dhp_ref[0], TN).astype(BF16)

    return pl.pallas_call(
        body, name="grad_w_in", grid=(n_p,),
        in_specs=[_resident(xn1.shape), pl.BlockSpec((1, tp, pw), lambda p: (p, 0, 0))],
        out_specs=pl.BlockSpec((d, pw), lambda p: (0, p)),
        out_shape=_sds((d, n_p * pw), BF16),
        compiler_params=_params(("arbitrary",)),
    )(xn1, dhp5)


def _mix_bwd_in(dhp5, win4, h, dh1, g1, after):
    n_p, tp, pw = dhp5.shape
    d = h.shape[1]
    n_sh, _, csh = win4.shape
    tm = _row_tile(tp)

    seq, _ = _seq_rows(tp)
    last, meta_off = seq // tm, seq % tm
    assert last == tp // tm - 1
    assert any(rs.start <= meta_off and meta_off + N_META <= rs.stop for rs in _row_parts(tm))

    def body(dhp_ref, w_ref, h_ref, dh1_ref, g_ref, after_ref, gx_ref, dmeta_ref, dg1_ref, wcat_ref):
        i = pl.program_id(0)
        _concat_shards(w_ref, wcat_ref)

        @pl.when(i == 0)
        def _():
            dg1_ref[...] = jnp.zeros(dg1_ref.shape, F32)

        for rs in _row_parts(tm):
            dxn = _dot(dhp_ref[0, rs, :], wcat_ref[:, 0:pw], NT)
            for p in range(1, n_p):
                dxn = dxn + _dot(dhp_ref[p, rs, :], wcat_ref[:, p * pw:(p + 1) * pw], NT)
            hh = h_ref[rs, :]
            r1 = lax.rsqrt(_mean(hh * hh) + RMS_EPS)
            dres, dg_rows = _rms_bwd(dxn, hh, r1, g_ref[...])
            dh = dh1_ref[rs, :] + dres
            gx_ref[rs, :] = dh
            dg1_ref[...] += jnp.sum(dg_rows, axis=0, keepdims=True)
            if rs.start <= meta_off and meta_off + N_META <= rs.stop:
                @pl.when(i == last)
                def _():
                    dmeta_ref[...] = dh[meta_off - rs.start:meta_off - rs.start + N_META, :]

    row = lambda i: (i, 0)
    return pl.pallas_call(
        body, name="mix_bwd_in", grid=(tp // tm,),
        in_specs=[pl.BlockSpec((n_p, tm, pw), lambda i: (0, i, 0)), _resident(win4.shape), pl.BlockSpec((tm, d), row),
                  pl.BlockSpec((tm, d), row), _full(g1.shape), ANY],
        out_specs=[pl.BlockSpec((tm, d), row), _full((N_META, d)), _full((1, d))],
        out_shape=[_sds((seq, d), F32), _sds((N_META, d), F32), _sds((1, d), F32)],
        scratch_shapes=[pltpu.VMEM((d, n_sh * csh), BF16)],
        compiler_params=_params(("arbitrary",)),
    )(dhp5, win4, h, dh1, g1, after)


def _other_chips(x, y):
    out = []
    for j in (1, 2, 3):
        px, py = _flip(x, j >> 1), _flip(y, j & 1)
        out.append((px, py, 2 * px + py))
    return out


PAIR_COLLECTIVE_ID = 0


def _pair_barrier(x, y, c):
    sem = pltpu.get_barrier_semaphore()
    pl.semaphore_signal(sem, inc=1, device_id=(x, y, 1 - c), device_id_type=MESH)
    pl.semaphore_wait(sem, 1)


def _pair_params():
    return pltpu.CompilerParams(collective_id=PAIR_COLLECTIVE_ID)


def _half_rows(c, rows_half):
    return pl.ds(pl.multiple_of(c * rows_half, SUBLANES), rows_half)


def _cast_place(ws, q_arr, tag, after=None):
    n = len(ws)
    extra = [] if after is None else [after]

    def fits(steps):
        return all(w.shape[0] % steps == 0 and (w.shape[0] // steps) % BF16_ROWS == 0
                   and w.shape[0] // steps <= ROW_TILE_CAP for w in ws)

    steps = next(s for s in range(1, min(w.shape[0] for w in ws) + 1) if fits(s))

    def body(q_ref, *refs):
        for w_ref, out_ref in zip(refs[:n], refs[n + len(extra):]):
            out_ref[0] = w_ref[...].astype(BF16)

    return list(pl.pallas_call(
        body, name="cast_place_" + tag,
        grid_spec=pltpu.PrefetchScalarGridSpec(
            num_scalar_prefetch=1, grid=(steps,),
            in_specs=[pl.BlockSpec((w.shape[0] // steps, w.shape[1]), lambda i, q: (i, 0)) for w in ws] + [ANY] * len(extra),
            out_specs=[pl.BlockSpec((1, w.shape[0] // steps, w.shape[1]), lambda i, q: (q[0], i, 0)) for w in ws]),
        out_shape=[_sds((N_CHIPS,) + w.shape, BF16) for w in ws],
        compiler_params=_params(("arbitrary",)),
    )(q_arr, *ws, *extra))


HBM = pl.BlockSpec(memory_space=pltpu.HBM)
SEM = pl.BlockSpec(memory_space=pltpu.SEMAPHORE)
EFFECT = pltpu.SideEffectType.DATAFLOW_SIDE_EFFECTING


def _in_hbm(a):
    return pltpu.with_memory_space_constraint(a, pltpu.HBM)


def _gather_start(fulls, after, tag):
    n = len(fulls)
    halves = [a.shape[1] // 2 for a in fulls]

    def body(*refs):
        land = refs[:n]
        ssem, rsem = refs[n + 1], refs[n + 2]
        token = refs[-1]
        x, y, c = _mesh_pos()
        q = 2 * x + y
        for i in range(n):
            for j, (px, py, _) in enumerate(_other_chips(x, y)):
                mine = land[i].at[q, _half_rows(c, halves[i]), :]
                pltpu.make_async_remote_copy(src_ref=mine, dst_ref=mine, send_sem=ssem.at[3 * i + j],
                                             recv_sem=rsem.at[3 * i + j], device_id=(px, py, c), device_id_type=MESH).start()
        token[...] = jnp.zeros(token.shape, F32)

    outs = pl.pallas_call(
        body, name="gather_start_" + tag,
        in_specs=[HBM] * n + [ANY], out_specs=[SEM, SEM] + [HBM] * n + [VMEM],
        out_shape=[pltpu.SemaphoreType.DMA((3 * n,)), pltpu.SemaphoreType.DMA((3 * n,))]
        + [pltpu.HBM(a.shape, a.dtype) for a in fulls] + [_sds((SUBLANES, LANES), F32)],
        input_output_aliases={i: 2 + i for i in range(n)},
        compiler_params=pltpu.CompilerParams(has_side_effects=EFFECT),
    )(*[_in_hbm(a) for a in fulls], after)
    return outs[0], outs[1], list(outs[2:2 + n]), outs[-1]


def _gather_wait(which, ssem, rsem, lands, after, tag):
    m = len(which)
    halves = [a.shape[1] // 2 for a in lands]

    def body(*refs):
        land = refs[:m]
        ssem_, rsem_ = refs[m], refs[m + 1]
        x, y, c = _mesh_pos()
        for t, i in enumerate(which):
            for j, (px, py, qj) in enumerate(_other_chips(x, y)):
                rows = _half_rows(c, halves[t])
                cp = pltpu.make_async_remote_copy(src_ref=land[t].at[2 * x + y, rows, :], dst_ref=land[t].at[qj, rows, :],
                                                  send_sem=ssem_.at[3 * i + j], recv_sem=rsem_.at[3 * i + j],
                                                  device_id=(px, py, c), device_id_type=MESH)
                cp.wait_send()
                cp.wait_recv()

    outs = pl.pallas_call(
        body, name="gather_wait_" + tag,
        in_specs=[HBM] * m + [SEM, SEM, ANY], out_specs=[HBM] * m,
        out_shape=[pltpu.HBM(a.shape, a.dtype) for a in lands],
        input_output_aliases={i: i for i in range(m)},
        compiler_params=pltpu.CompilerParams(has_side_effects=EFFECT),
    )(*lands, ssem, rsem, after)
    return list(outs)


def _forward_pair(lands, tag):
    n = len(lands)
    halves = [a.shape[1] // 2 for a in lands]

    def body(*refs):
        full = refs[n:2 * n]
        ssem, rsem = refs[2 * n:]
        x, y, c = _mesh_pos()
        _pair_barrier(x, y, c)
        cps = []
        for i in range(n):
            for j, (_, _, qj) in enumerate(_other_chips(x, y)):
                part = full[i].at[qj, _half_rows(c, halves[i]), :]
                cp = pltpu.make_async_remote_copy(src_ref=part, dst_ref=part, send_sem=ssem.at[3 * i + j],
                                                  recv_sem=rsem.at[3 * i + j], device_id=(x, y, 1 - c), device_id_type=MESH)
                cp.start()
                cps.append(cp)
        for cp in cps:
            cp.wait()

    return pl.pallas_call(
        body, name="forward_pair_" + tag,
        in_specs=[ANY] * n, out_specs=[ANY] * n,
        out_shape=[_sds(a.shape, a.dtype) for a in lands],
        input_output_aliases={i: i for i in range(n)},
        scratch_shapes=[pltpu.SemaphoreType.DMA((3 * n,)), pltpu.SemaphoreType.DMA((3 * n,))],
        compiler_params=_pair_params(),
    )(*lands)


def _chip_exchange_start(parts, after, tag):
    n = len(parts)

    def body(*refs):
        src, land = refs[:n], refs[n:2 * n]
        ssem, rsem = refs[2 * n + 1], refs[2 * n + 2]
        token = refs[-1]
        x, y, c = _mesh_pos()
        for i in range(n):
            for j, (px, py, qj) in enumerate(_other_chips(x, y)):
                pltpu.make_async_remote_copy(src_ref=src[i].at[qj], dst_ref=land[i].at[j], send_sem=ssem.at[3 * i + j],
                                             recv_sem=rsem.at[3 * i + j], device_id=(px, py, c), device_id_type=MESH).start()
        token[...] = jnp.zeros(token.shape, F32)

    lands = [lax.empty((3,) + a.shape[1:], a.dtype) for a in parts]
    outs = pl.pallas_call(
        body, name="chip_exchange_start_" + tag,
        in_specs=[HBM] * (2 * n) + [ANY], out_specs=[SEM, SEM] + [HBM] * (2 * n) + [VMEM],
        out_shape=[pltpu.SemaphoreType.DMA((3 * n,)), pltpu.SemaphoreType.DMA((3 * n,))]
        + [pltpu.HBM(a.shape, a.dtype) for a in parts] + [pltpu.HBM(a.shape, a.dtype) for a in lands]
        + [_sds((SUBLANES, LANES), F32)],
        input_output_aliases={i: 2 + i for i in range(2 * n)},
        compiler_params=pltpu.CompilerParams(has_side_effects=EFFECT),
    )(*[_in_hbm(a) for a in parts], *[_in_hbm(a) for a in lands], after)
    return outs[0], outs[1], list(outs[2:2 + n]), list(outs[2 + n:2 + 2 * n]), outs[-1]


def _chip_exchange_wait(ssem, rsem, parts, lands, after, tag):
    n = len(parts)

    def body(*refs):
        src, land = refs[:n], refs[n:2 * n]
        ssem_, rsem_ = refs[2 * n], refs[2 * n + 1]
        x, y, c = _mesh_pos()
        for i in range(n):
            for j, (px, py, qj) in enumerate(_other_chips(x, y)):
                cp = pltpu.make_async_remote_copy(src_ref=src[i].at[qj], dst_ref=land[i].at[j], send_sem=ssem_.at[3 * i + j],
                                                  recv_sem=rsem_.at[3 * i + j], device_id=(px, py, c), device_id_type=MESH)
                cp.wait_send()
                cp.wait_recv()

    outs = pl.pallas_call(
        body, name="chip_exchange_wait_" + tag,
        in_specs=[HBM] * (2 * n) + [SEM, SEM, ANY], out_specs=[HBM] * (2 * n),
        out_shape=[pltpu.HBM(a.shape, a.dtype) for a in parts] + [pltpu.HBM(a.shape, a.dtype) for a in lands],
        input_output_aliases={i: i for i in range(2 * n)},
        compiler_params=pltpu.CompilerParams(has_side_effects=EFFECT),
    )(*parts, *lands, ssem, rsem, after)
    return list(outs[:n]), list(outs[n:])


def _grad_half(ref, shape, axis, which):
    rows = shape[axis] // 2
    if axis == 0:
        return ref.at[_half_rows(which, rows), :]
    return ref.at[:, _half_rows(which, rows), :]


def _half_shape(a, axis):
    s = list(a.shape)
    s[axis] //= 2
    return tuple(s)


def _pair_exchange_start(grads, half_axis, after, tag):
    n = len(grads)

    def body(*refs):
        g, land = refs[:n], refs[n:2 * n]
        ssem, rsem = refs[2 * n + 1], refs[2 * n + 2]
        token = refs[-1]
        x, y, c = _mesh_pos()
        for i in range(n):
            pltpu.make_async_remote_copy(src_ref=_grad_half(g[i], grads[i].shape, half_axis[i], 1 - c), dst_ref=land[i],
                                         send_sem=ssem.at[i], recv_sem=rsem.at[i], device_id=(x, y, 1 - c),
                                         device_id_type=MESH).start()
        token[...] = jnp.zeros(token.shape, F32)

    lands = [lax.empty(_half_shape(a, half_axis[i]), a.dtype) for i, a in enumerate(grads)]
    outs = pl.pallas_call(
        body, name="pair_exchange_start_" + tag,
        in_specs=[HBM] * (2 * n) + [ANY], out_specs=[SEM, SEM] + [HBM] * (2 * n) + [VMEM],
        out_shape=[pltpu.SemaphoreType.DMA((n,)), pltpu.SemaphoreType.DMA((n,))]
        + [pltpu.HBM(a.shape, a.dtype) for a in grads] + [pltpu.HBM(a.shape, a.dtype) for a in lands]
        + [_sds((SUBLANES, LANES), F32)],
        input_output_aliases={i: 2 + i for i in range(2 * n)},
        compiler_params=pltpu.CompilerParams(has_side_effects=EFFECT),
    )(*[_in_hbm(a) for a in grads], *[_in_hbm(a) for a in lands], after)
    return outs[0], outs[1], list(outs[2:2 + n]), list(outs[2 + n:2 + 2 * n]), outs[-1]


def _pair_exchange_wait(ssem, rsem, grads, lands, half_axis, after, tag):
    n = len(grads)

    def body(*refs):
        g, land = refs[:n], refs[n:2 * n]
        ssem_, rsem_ = refs[2 * n], refs[2 * n + 1]
        x, y, c = _mesh_pos()
        for i in range(n):
            cp = pltpu.make_async_remote_copy(src_ref=_grad_half(g[i], grads[i].shape, half_axis[i], 1 - c),
                                              dst_ref=land[i], send_sem=ssem_.at[i], recv_sem=rsem_.at[i],
                                              device_id=(x, y, 1 - c), device_id_type=MESH)
            cp.wait_send()
            cp.wait_recv()

    outs = pl.pallas_call(
        body, name="pair_exchange_wait_" + tag,
        in_specs=[HBM] * (2 * n) + [SEM, SEM, ANY], out_specs=[HBM] * (2 * n),
        out_shape=[pltpu.HBM(a.shape, a.dtype) for a in grads] + [pltpu.HBM(a.shape, a.dtype) for a in lands],
        input_output_aliases={i: i for i in range(2 * n)},
        compiler_params=pltpu.CompilerParams(has_side_effects=EFFECT),
    )(*grads, *lands, ssem, rsem, after)
    return list(outs[:n]), list(outs[n:])


def _pair_exchange_grads(grads, half_axis, tag):
    n = len(grads)

    def body(*refs):
        g, got = refs[:n], refs[n:2 * n]
        ssem, rsem = refs[2 * n:]
        x, y, c = _mesh_pos()
        _pair_barrier(x, y, c)
        cps = []
        for i in range(n):
            cp = pltpu.make_async_remote_copy(src_ref=_grad_half(g[i], grads[i].shape, half_axis[i], 1 - c),
                                              dst_ref=got[i], send_sem=ssem.at[i], recv_sem=rsem.at[i],
                                              device_id=(x, y, 1 - c), device_id_type=MESH)
            cp.start()
            cps.append(cp)
        for cp in cps:
            cp.wait()

    return pl.pallas_call(
        body, name="pair_exchange_grads_" + tag,
        in_specs=[ANY] * n, out_specs=[ANY] * n,
        out_shape=[_sds(_half_shape(a, half_axis[i]), a.dtype) for i, a in enumerate(grads)],
        scratch_shapes=[pltpu.SemaphoreType.DMA((n,)), pltpu.SemaphoreType.DMA((n,))],
        compiler_params=_pair_params(),
    )(*grads)


def _pair_sum(gs, gots, c_arr, col_sharded, tag):
    n = len(gs)
    g_specs, got_specs, out_specs, out_shapes = [], [], [], []
    for g, by_cols in zip(gs, col_sharded):
        if by_cols:
            rows, cols = g.shape
            rh, cs = rows // 2, cols // N_CHIPS
            g_specs.append(pl.BlockSpec((rh, cs), lambda k, c_ref: (c_ref[0], k)))
            got_specs.append(pl.BlockSpec((rh, cs), lambda k, c_ref: (0, k)))
        else:
            _, rows, cs = g.shape
            rh = rows // 2
            g_specs.append(pl.BlockSpec((1, rh, cs), lambda k, c_ref: (k, c_ref[0], 0)))
            got_specs.append(pl.BlockSpec((1, rh, cs), lambda k, c_ref: (k, 0, 0)))
        out_specs.append(pl.BlockSpec((1, rh, cs), lambda k, c_ref: (k, 0, 0)))
        out_shapes.append(_sds((N_CHIPS, rh, cs), BF16))

    def body(c_ref, *refs):
        for g_ref, got_ref, out_ref in zip(refs[:n], refs[n:2 * n], refs[2 * n:]):
            total = g_ref[...].astype(F32) + got_ref[...].astype(F32)
            out_ref[...] = total.astype(BF16).reshape(out_ref.shape)

    return list(pl.pallas_call(
        body, name="pair_sum_" + tag,
        grid_spec=pltpu.PrefetchScalarGridSpec(
            num_scalar_prefetch=1, grid=(N_CHIPS,), in_specs=g_specs + got_specs, out_specs=out_specs),
        out_shape=out_shapes,
        compiler_params=_params(("arbitrary",)),
    )(c_arr, *gs, *gots))


def _chip_sum(parts, gots, qc_arr, tag):
    n = len(parts)
    steps = 2 if all(p.shape[1] % 32 == 0 for p in parts) else 1
    part_specs, got_specs, out_specs, out_shapes = [], [], [], []
    for p in parts:
        _, rh, cs = p.shape
        rb = rh // steps
        part_specs.append(pl.BlockSpec((1, rb, cs), lambda i, qc: (qc[0], i, 0)))
        got_specs.append(pl.BlockSpec((3, rb, cs), lambda i, qc: (0, i, 0)))
        out_specs.append(pl.BlockSpec((rb, cs), lambda i, qc: (qc[1] * steps + i, 0)))
        out_shapes.append(_sds((2 * rh, cs), F32))

    def body(qc_ref, *refs):
        for part_ref, got_ref, out_ref in zip(refs[:n], refs[n:2 * n], refs[2 * n:]):
            total = part_ref[0].astype(F32)
            for j in range(3):
                total = total + got_ref[j].astype(F32)
            out_ref[...] = total

    return list(pl.pallas_call(
        body, name="chip_sum_" + tag,
        grid_spec=pltpu.PrefetchScalarGridSpec(
            num_scalar_prefetch=1, grid=(steps,), in_specs=part_specs + got_specs, out_specs=out_specs),
        out_shape=out_shapes,
        compiler_params=_params(("arbitrary",)),
    )(qc_arr, *parts, *gots))


def _pair_share_grads(grads, tag):
    n = len(grads)

    def body(*refs):
        g = refs[n:2 * n]
        ssem, rsem = refs[2 * n:]
        x, y, c = _mesh_pos()
        _pair_barrier(x, y, c)
        cps = []
        for i in range(n):
            mine = g[i].at[_half_rows(c, grads[i].shape[0] // 2), :]
            cp = pltpu.make_async_remote_copy(src_ref=mine, dst_ref=mine, send_sem=ssem.at[i], recv_sem=rsem.at[i],
                                              device_id=(x, y, 1 - c), device_id_type=MESH)
            cp.start()
            cps.append(cp)
        for cp in cps:
            cp.wait()

    return pl.pallas_call(
        body, name="pair_share_grads_" + tag,
        in_specs=[ANY] * n, out_specs=[ANY] * n,
        out_shape=[_sds(a.shape, a.dtype) for a in grads],
        input_output_aliases={i: i for i in range(n)},
        scratch_shapes=[pltpu.SemaphoreType.DMA((n,)), pltpu.SemaphoreType.DMA((n,))],
        compiler_params=_pair_params(),
    )(*grads)


def _pair_share_start(grads, after, tag):
    n = len(grads)

    def body(*refs):
        g = refs[:n]
        ssem, rsem = refs[n + 1], refs[n + 2]
        token = refs[-1]
        x, y, c = _mesh_pos()
        for i in range(n):
            mine = g[i].at[_half_rows(c, grads[i].shape[0] // 2), :]
            pltpu.make_async_remote_copy(src_ref=mine, dst_ref=mine, send_sem=ssem.at[i], recv_sem=rsem.at[i],
                                         device_id=(x, y, 1 - c), device_id_type=MESH).start()
        token[...] = jnp.zeros(token.shape, F32)

    outs = pl.pallas_call(
        body, name="pair_share_start_" + tag,
        in_specs=[HBM] * n + [ANY], out_specs=[SEM, SEM] + [HBM] * n + [VMEM],
        out_shape=[pltpu.SemaphoreType.DMA((n,)), pltpu.SemaphoreType.DMA((n,))]
        + [pltpu.HBM(a.shape, a.dtype) for a in grads] + [_sds((SUBLANES, LANES), F32)],
        input_output_aliases={i: 2 + i for i in range(n)},
        compiler_params=pltpu.CompilerParams(has_side_effects=EFFECT),
    )(*[_in_hbm(a) for a in grads], after)
    return outs[0], outs[1], list(outs[2:2 + n]), outs[-1]


def _pair_share_wait(ssem, rsem, grads, after, tag):
    n = len(grads)

    def body(*refs):
        g = refs[:n]
        ssem_, rsem_ = refs[n], refs[n + 1]
        x, y, c = _mesh_pos()
        for i in range(n):
            rows = grads[i].shape[0] // 2
            cp = pltpu.make_async_remote_copy(src_ref=g[i].at[_half_rows(c, rows), :], dst_ref=g[i].at[_half_rows(1 - c, rows), :],
                                              send_sem=ssem_.at[i], recv_sem=rsem_.at[i], device_id=(x, y, 1 - c),
                                              device_id_type=MESH)
            cp.wait_send()
            cp.wait_recv()

    outs = pl.pallas_call(
        body, name="pair_share_wait_" + tag,
        in_specs=[HBM] * n + [SEM, SEM, ANY], out_specs=[HBM] * n,
        out_shape=[pltpu.HBM(a.shape, a.dtype) for a in grads],
        input_output_aliases={i: i for i in range(n)},
        compiler_params=pltpu.CompilerParams(has_side_effects=EFFECT),
    )(*grads, ssem, rsem, after)
    return list(outs)


def _small_allreduce(parts, places, rows_total, width, after):
    n = len(parts)

    def body(*refs):
        ins, out_ref = refs[:n], refs[n + 1]
        pack, pair_got, chip_sum, got, ssem, rsem = refs[n + 2:]
        x, y, c = _mesh_pos()
        chip = 2 * x + y
        pack[...] = jnp.zeros(pack.shape, F32)
        for i in range(n):
            for row, col, src_row, rows in places[i]:
                w = parts[i].shape[1]
                pack[row:row + rows, col:col + w] = ins[i][src_row:src_row + rows, :]
        swap = pltpu.make_async_remote_copy(src_ref=pack, dst_ref=pair_got, send_sem=ssem.at[3], recv_sem=rsem.at[3],
                                            device_id=(x, y, 1 - c), device_id_type=MESH)
        swap.start()
        swap.wait()
        chip_sum[...] = pack[...] + pair_got[...]
        cps = []
        for j, (px, py, _) in enumerate(_other_chips(x, y)):
            cp = pltpu.make_async_remote_copy(src_ref=chip_sum, dst_ref=got.at[j], send_sem=ssem.at[j],
                                              recv_sem=rsem.at[j], device_id=(px, py, c), device_id_type=MESH)
            cp.start()
            cps.append(cp)
        for cp in cps:
            cp.wait()
        total = jnp.zeros(pack.shape, F32)
        for q in range(N_CHIPS):
            rel = jnp.bitwise_xor(chip, q)
            theirs = got[jnp.maximum(rel - 1, 0)]
            total = total + jnp.where(rel == 0, chip_sum[...], theirs)
        out_ref[...] = total

    return pl.pallas_call(
        body, name="small_allreduce",
        in_specs=[VMEM] * n + [ANY], out_specs=VMEM,
        out_shape=_sds((rows_total, width), F32),
        scratch_shapes=[pltpu.VMEM((rows_total, width), F32), pltpu.VMEM((rows_total, width), F32),
                        pltpu.VMEM((rows_total, width), F32), pltpu.VMEM((3, rows_total, width), F32),
                        pltpu.SemaphoreType.DMA((4,)), pltpu.SemaphoreType.DMA((4,))],
        compiler_params=_params(),
    )(*parts, after)


def _small_update(red, q_arr, takes, loss_at, ws, ms, vs):
    n_w = len(ws)

    def body(q_ref, red_ref, *refs):
        w_in, m_in, v_in = refs[0:n_w], refs[n_w:2 * n_w], refs[2 * n_w:3 * n_w]
        outs = refs[3 * n_w:]
        g_out, d_out, m_out, v_out = (outs[0:n_w], outs[n_w:2 * n_w], outs[2 * n_w:3 * n_w], outs[3 * n_w:4 * n_w])
        loss_ref = outs[4 * n_w]
        chip = q_ref[0]

        def put(g_ref, d0, nr, s0, lo, w):
            if len(g_ref.shape) == 3:
                for r in range(nr):
                    g_ref[d0 + r] = red_ref[s0 + r:s0 + r + 1, lo:lo + w]
            else:
                g_ref[d0:d0 + nr, :] = red_ref[s0:s0 + nr, lo:lo + w]

        def take_own_columns(g_ref, d0, nr, s0, c0, w):
            for k in range(N_CHIPS):
                @pl.when(chip == k)
                def _():
                    put(g_ref, d0, nr, s0, c0 + k * w, w)

        for j in range(n_w):
            w = ws[j].shape[-1]
            for d0, nr, s0, c0, sharded in takes[j]:
                if sharded:
                    take_own_columns(g_out[j], d0, nr, s0, c0, w)
                else:
                    put(g_out[j], d0, nr, s0, c0, w)
            d_out[j][...], m_out[j][...], v_out[j][...] = _adamw_math(w_in[j][...], g_out[j][...], m_in[j][...], v_in[j][...])
        loss_ref[...] = red_ref[loss_at[0]:loss_at[0] + 1, loss_at[1]:loss_at[1] + LANES]

    shapes = [_sds(w.shape, F32) for w in ws]
    outs = pl.pallas_call(
        body, name="small_update",
        in_specs=[pl.BlockSpec(memory_space=pltpu.SMEM)] + [VMEM] * (1 + 3 * n_w), out_specs=[VMEM] * (4 * n_w + 1),
        out_shape=shapes * 4 + [_sds((1, LANES), F32)],
        compiler_params=_params(),
    )(q_arr, red, *ws, *ms, *vs)
    return outs[0:n_w], outs[n_w:2 * n_w], outs[2 * n_w:3 * n_w], outs[3 * n_w:4 * n_w], outs[4 * n_w]


def _adamw_math(w, g, m, v):
    m2 = ADAM_B1 * m + (1.0 - ADAM_B1) * g
    v2 = ADAM_B2 * v + (1.0 - ADAM_B2) * (g * g)
    m_hat = m2 / (1.0 - ADAM_B1 ** ADAM_STEP)
    v_hat = v2 / (1.0 - ADAM_B2 ** ADAM_STEP)
    delta = -ADAM_LR * (m_hat / (jnp.sqrt(v_hat) + ADAM_EPS) + ADAM_WD * w)
    return delta, m2, v2


ADAMW_BLOCK_BYTES = 3 * 2 ** 19


def _adamw_big(ws, gs, ms, vs, tag, after=None):
    n = len(ws)
    extra = [] if after is None else [after]

    def fits(steps):
        return all(w.shape[0] % steps == 0 and (w.shape[0] // steps) % SUBLANES == 0
                   and (w.shape[0] // steps) * w.shape[1] * 4 * n <= ADAMW_BLOCK_BYTES for w in ws)

    steps = next(s for s in range(1, min(w.shape[0] for w in ws) + 1) if fits(s))
    specs = [pl.BlockSpec((w.shape[0] // steps, w.shape[1]), lambda i: (i, 0)) for w in ws]

    def body(*refs):
        ins, outs = refs[:4 * n], refs[4 * n + len(extra):]
        for i in range(n):
            w_ref, g_ref, m_ref, v_ref = ins[i], ins[n + i], ins[2 * n + i], ins[3 * n + i]
            gg = g_ref[...]
            outs[4 * i][...] = gg
            outs[4 * i + 1][...], outs[4 * i + 2][...], outs[4 * i + 3][...] = _adamw_math(
                w_ref[...], gg, m_ref[...], v_ref[...])

    outs = pl.pallas_call(
        body, name="adamw_" + tag, grid=(steps,),
        in_specs=specs * 4 + [ANY] * len(extra), out_specs=[s for s in specs for _ in range(4)],
        out_shape=[_sds(w.shape, F32) for w in ws for _ in range(4)],
        compiler_params=_params(("arbitrary",)),
    )(*ws, *gs, *ms, *vs, *extra)
    return [outs[4 * i:4 * i + 4] for i in range(n)]


SMALL_ROWS = 40
PACK_ROWS = 64
PACK_CONV_A_ROW = 16
PACK_CONV_B_ROW = 24


def kernel(x, meta_tokens, pre_mix_norm, w_in, conv_a_w, conv_b_w, conv_b_bias, ln_b_gain, ln_b_bias, w_out, post_mix_norm, pre_ffn_norm, w_gate, w_up, w_down, post_ffn_norm, loss_target, m_meta_tokens, m_pre_mix_norm, m_w_in, m_conv_a_w, m_conv_b_w, m_conv_b_bias, m_ln_b_gain, m_ln_b_bias, m_w_out, m_post_mix_norm, m_pre_ffn_norm, m_w_gate, m_w_up, m_w_down, m_post_ffn_norm, v_meta_tokens, v_pre_mix_norm, v_w_in, v_conv_a_w, v_conv_b_w, v_conv_b_bias, v_ln_b_gain, v_ln_b_bias, v_w_out, v_post_mix_norm, v_pre_ffn_norm, v_w_gate, v_w_up, v_w_down, v_post_ffn_norm):
    xq, yq, cq = lax.axis_index("x"), lax.axis_index("y"), lax.axis_index("c")
    chip = 2 * xq + yq
    c_arr = jnp.reshape(cq, (1,)).astype(jnp.int32)
    qc_arr = jnp.stack([chip, cq]).astype(jnp.int32)

    seq, d = x.shape[1], x.shape[2]
    x2, tgt2 = x[0], loss_target[0]
    tr = lambda a: jnp.swapaxes(a, 1, 2)[0]
    w_in2, w_out2, w_gate2, w_up2, w_down2 = w_in[0], w_out[0], tr(w_gate), tr(w_up), w_down[0]
    ka, wa_sh = conv_a_w.shape[1], conv_a_w.shape[2]
    kb = conv_b_w.shape[1]
    meta_sh = meta_tokens.shape[1]

    small = jnp.zeros((PACK_ROWS, meta_sh), F32)
    small = small.at[0:N_META, :].set(meta_tokens)
    small = small.at[PACK_CONV_A_ROW:PACK_CONV_A_ROW + ka, 0:wa_sh].set(conv_a_w[0])
    small = small.at[PACK_CONV_B_ROW:PACK_CONV_B_ROW + kb, 0:wa_sh].set(conv_b_w[0])
    q_arr = jnp.reshape(chip, (1,)).astype(jnp.int32)
    small_own = lax.dynamic_update_slice(jnp.zeros((N_CHIPS, PACK_ROWS, meta_sh), F32), small[None], (chip, 0, 0))
    i_ssem, i_rsem, first, i_token = _gather_start(_cast_place([w_in2], q_arr, "w_in") + [small_own], pre_mix_norm, "in")
    rest = _cast_place([w_out2, w_gate2, w_up2, w_down2], q_arr, "rest", i_token)
    g_ssem, g_rsem, lands, g_token = _gather_start(rest, i_token, "rest")
    win4, small4 = _forward_pair(_gather_wait([0, 1], i_ssem, i_rsem, first, g_token, "in"), "in")
    assert wa_sh == LANES

    h, xn1, hp5 = _mm_in(x2, small4, win4, pre_mix_norm, g_token)
    ya, z = _mix_conv_fwd(hp5, small4, ka, kb, conv_b_bias)
    (wout4,) = _forward_pair(_gather_wait([0], g_ssem, g_rsem, lands[0:1], z, "out"), "out")
    wout_f = wout4.reshape(N_CHIPS * wout4.shape[1], wout4.shape[2])
    yb, mix, h1, xn2 = _mm_out(ya, z, h, wout_f, ln_b_gain, ln_b_bias, post_mix_norm, pre_ffn_norm)
    wg4, wu4 = _forward_pair(_gather_wait([1, 2], g_ssem, g_rsem, lands[1:3], xn2, "gate_up"), "gate_up")
    stacked = lambda a: a.reshape(a.shape[0] * a.shape[1], a.shape[2])
    wg_f, wu_f = stacked(wg4), stacked(wu4)
    p_act, q_act, f_act = _ffn_up(xn2, wg_f, wu_f)
    (wd4,) = _forward_pair(_gather_wait([3], g_ssem, g_rsem, lands[3:4], f_act, "down"), "down")
    wd_f = stacked(wd4)
    dff, dh2, loss_blk, d_gpf = _ffn_down(f_act, wd_f, h1, tgt2, post_ffn_norm)

    da, du = _ffn_bwd_act(dff, wd_f, p_act, q_act)
    by_chip = lambda g: g.reshape(N_CHIPS, g.shape[0] // N_CHIPS, g.shape[1])
    g_down = by_chip(_grad_w_down(f_act, dff))
    g_gate, g_up = [by_chip(g) for g in _grad_w_gate_up(xn2, da, du)]
    ffn = [g_gate, g_up, g_down]
    p_ssem, p_rsem, ffn, p_lands, p_token = _pair_exchange_start(ffn, [1, 1, 1], dff, "ffn")
    dh1, dmix, d_g2, d_gpm = _ffn_bwd_in(da, du, wg_f, wu_f, h1, mix, dh2, pre_ffn_norm, post_mix_norm, p_token)
    ffn, got = _pair_exchange_wait(p_ssem, p_rsem, ffn, p_lands, [1, 1, 1], d_g2, "ffn")
    parts = _pair_sum(ffn, got, c_arr, [False] * 3, "ffn")
    f_ssem, f_rsem, parts, f_lands, f_token = _chip_exchange_start(parts, dff, "ffn")
    g_out = _grad_w_out(ya, yb, dmix, f_token)
    dya, dz, d_lg, d_lb = _mix_bwd_out(dmix, wout_f, z, ln_b_gain, ln_b_bias, f_token)
    dhp5, d_wa, d_wb, d_bb = _mix_conv_bwd(hp5, dya, dz, small4, ka, kb)
    g_in = _grad_w_in(xn1, dhp5)

    g_out4 = g_out.reshape(N_CHIPS, g_out.shape[0] // N_CHIPS, g_out.shape[1])
    mixw = [g_in, g_out4]
    got2 = _pair_exchange_grads(mixw, [0, 1], "mix")
    parts2 = _pair_sum(mixw, got2, c_arr, [True, False], "mix")
    m_ssem, m_rsem, parts2, m_lands, m_token = _chip_exchange_start(parts2, dhp5, "mix")
    grad_x2, d_meta, d_g1 = _mix_bwd_in(dhp5, win4, h, dh1, pre_mix_norm, m_token)
    grad_x = grad_x2[None]

    parts, f_recv = _chip_exchange_wait(f_ssem, f_rsem, parts, f_lands, d_g1, "ffn")
    halves = _chip_sum(parts, f_recv, qc_arr, "ffn")
    s_ssem, s_rsem, halves, s_token = _pair_share_start(halves, d_g1, "ffn")

    hw = d // 2
    assert d_wa.shape == (3, hw) and d_wb.shape == (31, hw) and d_bb.shape == (1, hw)
    small_parts = [d_meta, d_g1, d_gpm, d_g2, d_gpf, d_bb, d_lg, d_lb, loss_blk[0:1, :], d_wa, d_wb]
    places = [[(0, 0, 0, N_META)], [(16, 0, 0, 1)], [(17, 0, 0, 1)], [(18, 0, 0, 1)], [(19, 0, 0, 1)],
              [(20, 0, 0, 1)], [(20, hw, 0, 1)], [(21, 0, 0, 1)], [(21, hw, 0, 1)], [(22, 0, 0, 3)],
              [(22, hw, 0, 3), (25, 0, 3, 14), (25, hw, 17, 14)]]
    red = _small_allreduce(small_parts, places, SMALL_ROWS, d, s_token)
    gsum_ffn = _pair_share_wait(s_ssem, s_rsem, halves, red, "ffn")

    names_big = ["w_in", "w_out", "w_gate", "w_up", "w_down"]
    w_big = dict(zip(names_big, [w_in2, w_out2, w_gate2, w_up2, w_down2]))
    m_big = dict(zip(names_big, [m_w_in[0], m_w_out[0], tr(m_w_gate), tr(m_w_up), m_w_down[0]]))
    v_big = dict(zip(names_big, [v_w_in[0], v_w_out[0], tr(v_w_gate), tr(v_w_up), v_w_down[0]]))
    grads, deltas, new_m, new_v = {}, {}, {}, {}

    def update(names, gs, tag, after=None):
        res = _adamw_big([w_big[k] for k in names], gs, [m_big[k] for k in names], [v_big[k] for k in names], tag, after)
        for nm, outs in zip(names, res):
            if nm in ("w_gate", "w_up"):
                outs = [jnp.swapaxes(o[None], 1, 2) for o in outs]
            else:
                outs = [o[None] for o in outs]
            grads[nm], deltas[nm], new_m[nm], new_v[nm] = outs
        return res[-1][1]

    parts2, m_recv = _chip_exchange_wait(m_ssem, m_rsem, parts2, m_lands, gsum_ffn[0], "mix")
    halves2 = _chip_sum(parts2, m_recv, qc_arr, "mix")
    x_ssem, x_rsem, halves2, x_token = _pair_share_start(halves2, red, "mix")
    last = update(["w_gate", "w_up", "w_down"], list(gsum_ffn), "ffn", x_token)

    names_small = ["meta_tokens", "pre_mix_norm", "conv_a_w", "conv_b_w", "conv_b_bias", "ln_b_gain", "ln_b_bias",
                   "post_mix_norm", "pre_ffn_norm", "post_ffn_norm"]
    takes = [[(0, N_META, 0, 0, True)], [(0, 1, 16, 0, False)], [(0, 3, 22, 0, True)],
             [(0, 3, 22, hw, True), (3, 14, 25, 0, True), (17, 14, 25, hw, True)], [(0, 1, 20, 0, False)],
             [(0, 1, 20, hw, False)], [(0, 1, 21, 0, False)], [(0, 1, 17, 0, False)], [(0, 1, 18, 0, False)],
             [(0, 1, 19, 0, False)]]
    taps = lambda a: jnp.swapaxes(a, 0, 1)
    w_small = [meta_tokens, pre_mix_norm, taps(conv_a_w), taps(conv_b_w), conv_b_bias, ln_b_gain, ln_b_bias, post_mix_norm,
               pre_ffn_norm, post_ffn_norm]
    m_small = [m_meta_tokens, m_pre_mix_norm, taps(m_conv_a_w), taps(m_conv_b_w), m_conv_b_bias, m_ln_b_gain, m_ln_b_bias,
               m_post_mix_norm, m_pre_ffn_norm, m_post_ffn_norm]
    v_small = [v_meta_tokens, v_pre_mix_norm, taps(v_conv_a_w), taps(v_conv_b_w), v_conv_b_bias, v_ln_b_gain, v_ln_b_bias,
               v_post_mix_norm, v_pre_ffn_norm, v_post_ffn_norm]
    g_s, d_s, m_s, v_s, loss_row = _small_update(red, q_arr, takes, (21, hw), w_small, m_small, v_small)
    loss = loss_row[0, 0]
    for i, nm in enumerate(names_small):
        fix = taps if nm in ("conv_a_w", "conv_b_w") else (lambda a: a)
        grads[nm], deltas[nm], new_m[nm], new_v[nm] = fix(g_s[i]), fix(d_s[i]), fix(m_s[i]), fix(v_s[i])

    gsum_mix = _pair_share_wait(x_ssem, x_rsem, halves2, last, "mix")
    update(["w_in", "w_out"], list(gsum_mix), "mix")

    order = ["meta_tokens", "pre_mix_norm", "w_in", "conv_a_w", "conv_b_w", "conv_b_bias", "ln_b_gain", "ln_b_bias", "w_out",
             "post_mix_norm", "pre_ffn_norm", "w_gate", "w_up", "w_down", "post_ffn_norm"]
    return (loss, grad_x, *[grads[k] for k in order], *[deltas[k] for k in order], *[new_m[k] for k in order],
            *[new_v[k] for k in order])
```
